```python
import jax, jax.numpy as jnp
from jax import lax
import numpy as np

D_MODEL = 1024
BATCH = 8
SEQ = 2048
DEPTH = 2
DEC_BATCH = 128
DEC_SEQ = 1
PAST_LEN = 16384
PAGE_SIZE = 128

GLA_HEADS = 4
GLA_DK = D_MODEL // 2 // GLA_HEADS
GLA_DV = D_MODEL // GLA_HEADS
GLA_KEY_WIDTH = GLA_HEADS * GLA_DK
GLA_VAL_WIDTH = GLA_HEADS * GLA_DV
GATE_RANK = 16
GATE_TAU = 16.0
CHUNK = 64
POOL_WINDOWS = (2, 4, 8, 16)
POOL_GROUPS = 4
POOL_WIDTH = D_MODEL
POOL_GROUP_DIM = POOL_WIDTH // POOL_GROUPS
POOL_HIST = max(POOL_WINDOWS) - 1
N_MEM = 256
XA_HEADS = 4
XA_HEAD_DIM = D_MODEL // XA_HEADS
XA_WIDTH = XA_HEADS * XA_HEAD_DIM
N_BRANCH = 3
BRANCH_WIDTH = D_MODEL
EPS = 1e-6
SPLITS = (GLA_KEY_WIDTH, GLA_KEY_WIDTH, GLA_VAL_WIDTH, GLA_VAL_WIDTH, GATE_RANK,
          POOL_WIDTH, POOL_WIDTH, XA_WIDTH, XA_WIDTH, N_BRANCH * D_MODEL)
N_IN = sum(SPLITS)

kernel_name = 'gla_pool_xattn_hybrid_step'


def rmsnorm(x, g):
    xf = x.astype(jnp.float32)
    y = xf * lax.rsqrt(jnp.mean(xf * xf, axis=-1, keepdims=True) + EPS)
    return (y * g.astype(jnp.float32)).astype(x.dtype)


def gla_chunked(q, k, v, log_a, S0):
    B, T, H, _ = q.shape
    C = CHUNK if T % CHUNK == 0 else T
    N = T // C

    def blk(z):
        return z.astype(jnp.float32).reshape(B, N, C, H, -1).transpose(0, 3, 1, 2, 4)

    qb, kb, vb, ab = blk(q), blk(k), blk(v), blk(log_a)
    b = jnp.cumsum(ab, axis=3)
    b_last = b[:, :, :, -1:]
    q_dec = qb * jnp.exp(b)
    k_inv = kb * jnp.exp(-b)
    k_end = kb * jnp.exp(b_last - b)
    mask = np.tril(np.ones((C, C), dtype=bool))
    att = jnp.where(mask, jnp.einsum('bhncd,bhnsd->bhncs', q_dec, k_inv), 0.0)
    o_intra = jnp.einsum('bhncs,bhnse->bhnce', att, vb)
    dS = jnp.einsum('bhncd,bhnce->bhnde', k_end, vb)
    decay = jnp.exp(b_last[:, :, :, 0])

    def step(S, inp):
        dS_n, dec_n = inp
        return dec_n[..., None] * S + dS_n, S

    S_final, S_before = lax.scan(step, S0.astype(jnp.float32),
                                 (dS.transpose(2, 0, 1, 3, 4), decay.transpose(2, 0, 1, 3)))
    S_before = S_before.transpose(1, 2, 0, 3, 4)
    o_inter = jnp.einsum('bhncd,bhnde->bhnce', q_dec, S_before)
    o = (o_inter + o_intra).transpose(0, 2, 3, 1, 4).reshape(B, T, H, -1)
    return o, S_final


def multiscale_pool(u, hist, pos0, pool_w, pool_scale):
    B, T, W = u.shape
    n_hist = hist.shape[1]
    ext = jnp.concatenate([hist.astype(u.dtype), u], axis=1).astype(jnp.float32)
    cs = jnp.concatenate([jnp.zeros((B, 1, W), jnp.float32), jnp.cumsum(ext, axis=1)], axis=1)
    ends = np.arange(n_hist + 1, n_hist + T + 1)
    pos = pos0 + np.arange(T)
    groups = []
    for g, w in enumerate(POOL_WINDOWS):
        sl = slice(g * POOL_GROUP_DIM, (g + 1) * POOL_GROUP_DIM)
        starts = np.maximum(ends - w, 0)
        count = np.minimum(w, pos + 1).astype(np.float32)
        s = cs[:, ends, sl] - cs[:, starts, sl]
        groups.append(s / count[None, :, None])
    mean = jnp.stack(groups, axis=2)
    diff = mean - u.astype(jnp.float32).reshape(B, T, POOL_GROUPS, POOL_GROUP_DIM)
    mixed = jnp.einsum('btgc,gce->btge', diff, pool_w.astype(jnp.float32)).reshape(B, T, W)
    mixed = mixed * pool_scale.astype(jnp.float32)
    new_hist = ext[:, -POOL_HIST:].astype(u.dtype)
    return mixed.astype(u.dtype), new_hist


def cross_attn(q, mk, mv):
    s = jnp.einsum('bthd,bmhd->bhtm', q.astype(jnp.float32), mk.astype(jnp.float32)) * (XA_HEAD_DIM ** -0.5)
    p = jax.nn.softmax(s, axis=-1)
    o = jnp.einsum('bhtm,bmhd->bthd', p, mv.astype(jnp.float32))
    return o.astype(q.dtype)


def mixer_layer(x, mem_k, mem_v, S0, pool_hist, pos0,
                w_in, w_a2, b_a, gla_gain, pool_w, pool_scale, w_branch, w_out, norm_gain):
    B, T, _ = x.shape
    h = rmsnorm(x, norm_gain)
    proj = h @ w_in
    split_pts = [int(p) for p in np.cumsum(SPLITS)[:-1]]
    q, k, v, gla_g, a_lr, u, pool_g, xq, xg, merge = jnp.split(proj, split_pts, axis=-1)
    q = q.reshape(B, T, GLA_HEADS, GLA_DK) * (GLA_DK ** -0.5)
    k = k.reshape(B, T, GLA_HEADS, GLA_DK)
    v = v.reshape(B, T, GLA_HEADS, GLA_DV)
    z = (a_lr @ w_a2 + b_a).astype(jnp.float32)
    log_a = (jax.nn.log_sigmoid(z) / GATE_TAU).reshape(B, T, GLA_HEADS, GLA_DK)
    o, S = gla_chunked(q, k, v, log_a, S0)
    o = rmsnorm(o, gla_gain.reshape(GLA_HEADS, GLA_DV)).reshape(B, T, GLA_VAL_WIDTH).astype(x.dtype)
    br_gla = o * jax.nn.silu(gla_g)
    p, new_hist = multiscale_pool(u, pool_hist, pos0, pool_w, pool_scale)
    br_pool = p * jax.nn.silu(pool_g)
    c = cross_attn(xq.reshape(B, T, XA_HEADS, XA_HEAD_DIM), mem_k, mem_v).reshape(B, T, XA_WIDTH)
    br_x = c * jax.nn.silu(xg)
    branches = jnp.stack([br_gla, br_pool, br_x], axis=2)
    bproj = jnp.einsum('btnw,nwd->btnd', branches, w_branch)
    gates = jax.nn.sigmoid(merge.astype(jnp.float32)).reshape(B, T, N_BRANCH, D_MODEL)
    merged = jnp.sum(gates * bproj.astype(jnp.float32), axis=2).astype(x.dtype)
    y = merged @ w_out
    return x + y, S.astype(x.dtype), new_hist


def setup_inputs(seed: int = 0) -> dict:
    key = jax.random.key(seed)
    ks = jax.random.split(key, 24)
    f32 = jnp.float32
    nrm = lambda kk, shape, scale: jax.random.normal(kk, shape, f32) * scale
    return {
        'x_prompt': nrm(ks[0], (BATCH, SEQ, D_MODEL), 1.0),
        'x_sample': nrm(ks[1], (DEC_BATCH, DEC_SEQ, D_MODEL), 1.0),
        'mem_prompt': nrm(ks[2], (BATCH, N_MEM, D_MODEL), 1.0),
        'cache_mem_k': nrm(ks[3], (DEPTH, DEC_BATCH, N_MEM, XA_HEADS, XA_HEAD_DIM), 1.0),
        'cache_mem_v': nrm(ks[4], (DEPTH, DEC_BATCH, N_MEM, XA_HEADS, XA_HEAD_DIM), 1.0),
        'state_gla': nrm(ks[5], (DEPTH, DEC_BATCH, GLA_HEADS, GLA_DK, GLA_DV), 1.0),
        'state_pool': nrm(ks[6], (DEPTH, DEC_BATCH, POOL_HIST, POOL_WIDTH), 1.0),
        'w_in': nrm(ks[7], (DEPTH, D_MODEL, N_IN), D_MODEL ** -0.5),
        'w_a2': nrm(ks[8], (DEPTH, GATE_RANK, GLA_KEY_WIDTH), GATE_RANK ** -0.5),
        'b_a': nrm(ks[9], (DEPTH, GLA_KEY_WIDTH), 0.1),
        'gla_gain': 1.0 + nrm(ks[10], (DEPTH, GLA_VAL_WIDTH), 0.1),
        'pool_w': nrm(ks[11], (DEPTH, POOL_GROUPS, POOL_GROUP_DIM, POOL_GROUP_DIM), POOL_GROUP_DIM ** -0.5),
        'pool_scale': 1.0 + nrm(ks[12], (DEPTH, POOL_WIDTH), 0.1),
        'w_mk': nrm(ks[13], (DEPTH, D_MODEL, XA_WIDTH), D_MODEL ** -0.5),
        'w_mv': nrm(ks[14], (DEPTH, D_MODEL, XA_WIDTH), D_MODEL ** -0.5),
        'w_branch': nrm(ks[15], (DEPTH, N_BRANCH, BRANCH_WIDTH, D_MODEL), BRANCH_WIDTH ** -0.5),
        'w_out': nrm(ks[16], (DEPTH, D_MODEL, D_MODEL), D_MODEL ** -0.5),
        'norm_gain': 1.0 + nrm(ks[17], (DEPTH, D_MODEL), 0.1),
        'final_gain': 1.0 + nrm(ks[18], (D_MODEL,), 0.1),
    }


def reference(x_prompt, x_sample, mem_prompt, cache_mem_k, cache_mem_v, state_gla, state_pool,
              w_in, w_a2, b_a, gla_gain, pool_w, pool_scale, w_mk, w_mv, w_branch, w_out,
              norm_gain, final_gain):
    xp, xs = x_prompt, x_sample
    Bp, Mp = mem_prompt.shape[0], mem_prompt.shape[1]
    mk_list, mv_list, glap_list, poolp_list, glas_list, pools_list = [], [], [], [], [], []
    for l in range(DEPTH):
        params = (w_in[l], w_a2[l], b_a[l], gla_gain[l], pool_w[l], pool_scale[l], w_branch[l], w_out[l], norm_gain[l])
        mk = (mem_prompt @ w_mk[l]).reshape(Bp, Mp, XA_HEADS, XA_HEAD_DIM)
        mv = (mem_prompt @ w_mv[l]).reshape(Bp, Mp, XA_HEADS, XA_HEAD_DIM)
        S0p = jnp.zeros((xp.shape[0], GLA_HEADS, GLA_DK, GLA_DV), jnp.float32)
        hist0 = jnp.zeros((xp.shape[0], 0, POOL_WIDTH), xp.dtype)
        xp, Sp, hp = mixer_layer(xp, mk, mv, S0p, hist0, 0, *params)
        xs, Ss, hs = mixer_layer(xs, cache_mem_k[l], cache_mem_v[l], state_gla[l].astype(jnp.float32),
                                 state_pool[l], PAST_LEN, *params)
        mk_list.append(mk)
        mv_list.append(mv)
        glap_list.append(Sp)
        poolp_list.append(hp)
        glas_list.append(Ss)
        pools_list.append(hs)
    y_prompt = rmsnorm(xp, final_gain)
    y_sample = rmsnorm(xs, final_gain)
    return (y_prompt, y_sample, jnp.stack(mk_list), jnp.stack(mv_list), jnp.stack(glap_list),
            jnp.stack(poolp_list), jnp.stack(glas_list), jnp.stack(pools_list))
```

```python
import functools

import jax
import jax.numpy as jnp
from jax import lax
from jax.experimental import pallas as pl
from jax.experimental.pallas import tpu as pltpu

D_MODEL = 1024
DEPTH = 2
PAST_LEN = 16384
GLA_HEADS = 4
GLA_DK = 128
GLA_DV = 256
GLA_KEY_WIDTH = GLA_HEADS * GLA_DK
GATE_RANK = 16
GATE_TAU = 16.0
CHUNK = 64
POOL_WINDOWS = (2, 4, 8, 16)
POOL_GROUP_DIM = 256
POOL_HIST = 15
HIST_PAD = 16
XA_HEADS = 4
XA_HEAD_DIM = 256
N_BRANCH = 3
EPS = 1e-6

SLAB_QK, SLAB_V, SLAB_GLA_G, SLAB_U, SLAB_POOL_G, SLAB_XQ, SLAB_XG, SLAB_MERGE = 0, 1, 2, 3, 4, 5, 6, 7
N_SLABS = 10
N_MIX_SLABS = 7
ALR_START = 2 * GLA_KEY_WIDTH + 2 * D_MODEL

BF = jnp.bfloat16
F32 = jnp.float32
MIB = 1 << 20


def _dot(a, b):
    return jnp.dot(a, b, preferred_element_type=F32)


def _dot_nt(a, b):
    return lax.dot_general(a, b, (((1,), (1,)), ((), ())), preferred_element_type=F32)


def _dot_tn(a, b):
    return lax.dot_general(a, b, (((0,), (0,)), ((), ())), preferred_element_type=F32)


def _silu(x):
    return x * jax.nn.sigmoid(x)


def _rms(x, gain):
    ms = jnp.mean(x * x, axis=-1, keepdims=True)
    return x * lax.rsqrt(ms + EPS) * gain


def _const_spec(shape):
    zeros = (0,) * len(shape)
    return pl.BlockSpec(shape, lambda *_: zeros, pipeline_mode=pl.Buffered(1))


def _params(sem, vmem_mib):
    return pltpu.CompilerParams(dimension_semantics=sem, vmem_limit_bytes=vmem_mib * MIB)


def _kvproj_kernel(m_ref, wk_ref, wv_ref, k_ref, v_ref, kb_ref, vb_ref):
    m = m_ref[...].astype(BF)
    k = _dot(m, wk_ref[...])
    v = _dot(m, wv_ref[...])
    k_ref[...] = k
    v_ref[...] = v
    kb_ref[...] = k.astype(BF)
    vb_ref[...] = v.astype(BF)


def _kvproj(mem, wk, wv, tm=256):
    m_rows = mem.shape[0]
    row = pl.BlockSpec((tm, D_MODEL), lambda i: (i, 0))
    return pl.pallas_call(
        _kvproj_kernel,
        grid=(m_rows // tm,),
        in_specs=[row, _const_spec((D_MODEL, D_MODEL)), _const_spec((D_MODEL, D_MODEL))],
        out_specs=[row, row, row, row],
        out_shape=[jax.ShapeDtypeStruct((m_rows, D_MODEL), F32)] * 2
        + [jax.ShapeDtypeStruct((m_rows, D_MODEL), BF)] * 2,
        compiler_params=_params(("parallel",), 24),
        name="kvproj",
    )(mem, wk, wv)


def _inproj_kernel(x_ref, g_ref, w_ref, walr_ref, wa2_ref, ba_ref, p_ref, la_ref):
    h = _rms(x_ref[...], g_ref[...]).astype(BF)
    for j in range(N_SLABS):
        p_ref[j] = _dot(h, w_ref[:, j * D_MODEL:(j + 1) * D_MODEL]).astype(BF)
    alr = _dot(h, walr_ref[...])
    z = _dot(alr.astype(BF), wa2_ref[...]) + ba_ref[...]
    la_ref[...] = (jnp.minimum(z, 0.0) - jnp.log1p(jnp.exp(-jnp.abs(z)))) * (1.0 / GATE_TAU)


def _inproj(x, gain, w_main, w_alr, w_a2, b_a, tm):
    m_rows = x.shape[0]
    return pl.pallas_call(
        _inproj_kernel,
        grid=(m_rows // tm,),
        in_specs=[
            pl.BlockSpec((tm, D_MODEL), lambda i: (i, 0)),
            _const_spec((1, D_MODEL)),
            _const_spec((D_MODEL, N_SLABS * D_MODEL)),
            _const_spec((D_MODEL, GATE_RANK)),
            _const_spec((GATE_RANK, GLA_KEY_WIDTH)),
            _const_spec((1, GLA_KEY_WIDTH)),
        ],
        out_specs=[
            pl.BlockSpec((N_SLABS, tm, D_MODEL), lambda i: (0, i, 0)),
            pl.BlockSpec((tm, GLA_KEY_WIDTH), lambda i: (i, 0)),
        ],
        out_shape=[
            jax.ShapeDtypeStruct((N_SLABS, m_rows, D_MODEL), BF),
            jax.ShapeDtypeStruct((m_rows, GLA_KEY_WIDTH), F32),
        ],
        compiler_params=_params(("parallel",), 48),
        name="inproj",
    )(x, gain, w_main, w_alr, w_a2, b_a)


def _prompt_mix_kernel(p_ref, la_ref, mk_ref, mv_ref, pw_ref, ps_ref, gg_ref,
                       br_ref, s_out_ref, st_ref, ubuf_ref, *, tm):
    t = pl.program_id(1)

    @pl.when(t == 0)
    def _():
        st_ref[...] = jnp.zeros_like(st_ref)
        ubuf_ref[0:HIST_PAD, :] = jnp.zeros((HIST_PAD, D_MODEL), F32)

    row = lax.broadcasted_iota(jnp.int32, (tm, tm), 0)
    col = lax.broadcasted_iota(jnp.int32, (tm, tm), 1)
    same_chunk = (row // CHUNK) == (col // CHUNK)
    cum_mat = jnp.where(same_chunk & (row >= col), 1.0, 0.0).astype(BF)
    la = la_ref[...]
    la_hi = la.astype(BF)
    la_lo = (la - la_hi.astype(F32)).astype(BF)
    bcum = _dot(cum_mat, la_hi) + _dot(cum_mat, la_lo)
    crow = lax.broadcasted_iota(jnp.int32, (CHUNK, CHUNK), 0)
    ccol = lax.broadcasted_iota(jnp.int32, (CHUNK, CHUNK), 1)
    causal = crow >= ccol
    for c in range(tm // CHUNK):
        rows = slice(c * CHUNK, (c + 1) * CHUNK)
        for h in range(GLA_HEADS):
            kc = slice(h * GLA_DK, (h + 1) * GLA_DK)
            kc2 = slice(GLA_KEY_WIDTH + h * GLA_DK, GLA_KEY_WIDTH + (h + 1) * GLA_DK)
            vc = slice(h * GLA_DV, (h + 1) * GLA_DV)
            b = bcum[rows, kc]
            b_last = b[CHUNK - 1:CHUNK, :]
            q = p_ref[SLAB_QK, rows, kc].astype(F32)
            k = p_ref[SLAB_QK, rows, kc2].astype(F32)
            v = p_ref[SLAB_V, rows, vc]
            q_dec = (q * (GLA_DK ** -0.5) * jnp.exp(b)).astype(BF)
            k_inv = (k * jnp.exp(-b)).astype(BF)
            k_end = (k * jnp.exp(b_last - b)).astype(BF)
            att = jnp.where(causal, _dot_nt(q_dec, k_inv), 0.0)
            st = st_ref[h]
            o = _dot(att.astype(BF), v) + _dot_nt(q_dec, st.astype(BF))
            st_ref[h] = jnp.exp(b_last) * st + _dot_tn(v, k_end)
            g = p_ref[SLAB_GLA_G, rows, vc].astype(F32)
            br_ref[0, rows, vc] = (_rms(o, gg_ref[:, vc]) * _silu(g)).astype(BF)

    u = p_ref[SLAB_U].astype(F32)
    ubuf_ref[HIST_PAD:HIST_PAD + tm, :] = u
    pos = t * tm + lax.broadcasted_iota(jnp.int32, (tm, 1), 0)
    for g, w in enumerate(POOL_WINDOWS):
        cs = slice(g * POOL_GROUP_DIM, (g + 1) * POOL_GROUP_DIM)
        ug = u[:, cs]
        s = ug
        for j in range(1, w):
            s = s + ubuf_ref[HIST_PAD - j:HIST_PAD - j + tm, cs]
        cnt = jnp.minimum(w, pos + 1).astype(F32)
        diff = s / cnt - ug
        mixed = _dot(diff.astype(BF), pw_ref[g]) * ps_ref[:, cs]
        pg = p_ref[SLAB_POOL_G, :, cs].astype(F32)
        br_ref[1, :, cs] = (mixed * _silu(pg)).astype(BF)
    ubuf_ref[0:HIST_PAD, :] = ubuf_ref[tm:tm + HIST_PAD, :]

    for h in range(XA_HEADS):
        cs = slice(h * XA_HEAD_DIM, (h + 1) * XA_HEAD_DIM)
        s = _dot_nt(p_ref[SLAB_XQ, :, cs], mk_ref[:, cs]) * (XA_HEAD_DIM ** -0.5)
        p = jnp.exp(s - jnp.max(s, axis=-1, keepdims=True))
        denom = jnp.sum(p, axis=-1, keepdims=True)
        o = _dot(p.astype(BF), mv_ref[:, cs]) / denom
        xg = p_ref[SLAB_XG, :, cs].astype(F32)
        br_ref[2, :, cs] = (o * _silu(xg)).astype(BF)

    @pl.when(t == pl.num_programs(1) - 1)
    def _():
        for h in range(GLA_HEADS):
            s_out_ref[0, h] = st_ref[h].T


def _prompt_mix(p, la, mk, mv, pool_w, pool_scale, gla_gain, batch, seq, n_mem, tm):
    nt = seq // tm
    m_rows = batch * seq
    kern = functools.partial(_prompt_mix_kernel, tm=tm)
    return pl.pallas_call(
        kern,
        grid=(batch, nt),
        in_specs=[
            pl.BlockSpec((N_MIX_SLABS, tm, D_MODEL), lambda b, t: (0, b * nt + t, 0)),
            pl.BlockSpec((tm, GLA_KEY_WIDTH), lambda b, t: (b * nt + t, 0)),
            pl.BlockSpec((n_mem, D_MODEL), lambda b, t: (b, 0)),
            pl.BlockSpec((n_mem, D_MODEL), lambda b, t: (b, 0)),
            _const_spec((len(POOL_WINDOWS), POOL_GROUP_DIM, POOL_GROUP_DIM)),
            _const_spec((1, D_MODEL)),
            _const_spec((1, D_MODEL)),
        ],
        out_specs=[
            pl.BlockSpec((N_BRANCH, tm, D_MODEL), lambda b, t: (0, b * nt + t, 0)),
            pl.BlockSpec((1, GLA_HEADS, GLA_DK, GLA_DV), lambda b, t: (b, 0, 0, 0)),
        ],
        out_shape=[
            jax.ShapeDtypeStruct((N_BRANCH, m_rows, D_MODEL), BF),
            jax.ShapeDtypeStruct((batch, GLA_HEADS, GLA_DK, GLA_DV), F32),
        ],
        scratch_shapes=[
            pltpu.VMEM((GLA_HEADS, GLA_DV, GLA_DK), F32),
            pltpu.VMEM((HIST_PAD + tm, D_MODEL), F32),
        ],
        compiler_params=_params(("parallel", "arbitrary"), 40),
        name="prompt_mix",
    )(p, la, mk, mv, pool_w, pool_scale, gla_gain)


def _sample_mix_kernel(p_ref, la_ref, s0_ref, hist_ref, ck_ref, cv_ref, pw_ref, ps_ref, gg_ref,
                       br_ref, s_out_ref, hist_out_ref, diff_ref, mixed_ref, *, sb):
    diff_ref[...] = jnp.zeros_like(diff_ref)
    erow = lax.broadcasted_iota(jnp.int32, (GLA_DK, GLA_DK), 0)
    ecol = lax.broadcasted_iota(jnp.int32, (GLA_DK, GLA_DK), 1)
    eye = erow == ecol

    def to_col(x):
        return jnp.sum(jnp.where(eye, jnp.broadcast_to(x, (GLA_DK, GLA_DK)), 0.0), axis=1, keepdims=True)

    for i in range(sb):
        la = la_ref[i]
        qk = p_ref[SLAB_QK, i].astype(F32)
        vv = p_ref[SLAB_V, i].astype(F32)
        gla_g = p_ref[SLAB_GLA_G, i].astype(F32)
        for h in range(GLA_HEADS):
            kc = slice(h * GLA_DK, (h + 1) * GLA_DK)
            kc2 = slice(GLA_KEY_WIDTH + h * GLA_DK, GLA_KEY_WIDTH + (h + 1) * GLA_DK)
            vc = slice(h * GLA_DV, (h + 1) * GLA_DV)
            a_col = to_col(jnp.exp(la[:, kc]))
            q_col = to_col(qk[:, kc] * (GLA_DK ** -0.5))
            k_col = to_col(qk[:, kc2])
            s_new = a_col * s0_ref[0, i, h] + k_col * vv[:, vc]
            s_out_ref[i, h] = s_new
            o = jnp.sum(q_col * s_new, axis=0, keepdims=True)
            br_ref[0, i, :, vc] = _rms(o, gg_ref[:, vc]) * _silu(gla_g[:, vc])

        u = p_ref[SLAB_U, i].astype(F32)
        for g, w in enumerate(POOL_WINDOWS):
            cs = slice(g * POOL_GROUP_DIM, (g + 1) * POOL_GROUP_DIM)
            past = jnp.sum(hist_ref[0, i, POOL_HIST - (w - 1):POOL_HIST, cs], axis=0, keepdims=True)
            diff_ref[i:i + 1, cs] = (u[:, cs] + past) / float(w) - u[:, cs]
        hist_out_ref[i, 0:POOL_HIST - 1, :] = hist_ref[0, i, 1:POOL_HIST, :]
        hist_out_ref[i, POOL_HIST - 1:POOL_HIST, :] = u

        xq = p_ref[SLAB_XQ, i].astype(F32)
        xg = p_ref[SLAB_XG, i].astype(F32)
        for h in range(XA_HEADS):
            cs = slice(h * XA_HEAD_DIM, (h + 1) * XA_HEAD_DIM)
            s = jnp.sum(ck_ref[0, i, :, h, :] * xq[:, cs], axis=1, keepdims=True) * (XA_HEAD_DIM ** -0.5)
            p = jnp.exp(s - jnp.max(s, axis=0, keepdims=True))
            denom = jnp.sum(p, axis=0, keepdims=True)
            o = jnp.sum(p * cv_ref[0, i, :, h, :], axis=0, keepdims=True) / denom
            br_ref[2, i, :, cs] = o * _silu(xg[:, cs])

    for g in range(len(POOL_WINDOWS)):
        cs = slice(g * POOL_GROUP_DIM, (g + 1) * POOL_GROUP_DIM)
        mixed_ref[:, cs] = _dot(diff_ref[:, cs].astype(BF), pw_ref[g]) * ps_ref[:, cs]
    for i in range(sb):
        pg = p_ref[SLAB_POOL_G, i].astype(F32)
        br_ref[1, i] = mixed_ref[i:i + 1, :] * _silu(pg)


def _sample_mix(p, la, s0, hist, ck, cv, pool_w, pool_scale, gla_gain, layer, sb=4):
    nb = la.shape[0]
    n_mem = ck.shape[2]
    p4 = p.reshape(N_SLABS, nb, 1, D_MODEL)
    la3 = la.reshape(nb, 1, GLA_KEY_WIDTH)
    kern = functools.partial(_sample_mix_kernel, sb=sb)
    rows_pad = 8
    assert sb <= rows_pad
    return pl.pallas_call(
        kern,
        grid=(nb // sb,),
        in_specs=[
            pl.BlockSpec((N_MIX_SLABS, sb, 1, D_MODEL), lambda i: (0, i, 0, 0)),
            pl.BlockSpec((sb, 1, GLA_KEY_WIDTH), lambda i: (i, 0, 0)),
            pl.BlockSpec((1, sb, GLA_HEADS, GLA_DK, GLA_DV), lambda i: (layer, i, 0, 0, 0)),
            pl.BlockSpec((1, sb, POOL_HIST, D_MODEL), lambda i: (layer, i, 0, 0)),
            pl.BlockSpec((1, sb, n_mem, XA_HEADS, XA_HEAD_DIM), lambda i: (layer, i, 0, 0, 0)),
            pl.BlockSpec((1, sb, n_mem, XA_HEADS, XA_HEAD_DIM), lambda i: (layer, i, 0, 0, 0)),
            _const_spec((len(POOL_WINDOWS), POOL_GROUP_DIM, POOL_GROUP_DIM)),
            _const_spec((1, D_MODEL)),
            _const_spec((1, D_MODEL)),
        ],
        out_specs=[
            pl.BlockSpec((N_BRANCH, sb, 1, D_MODEL), lambda i: (0, i, 0, 0)),
            pl.BlockSpec((sb, GLA_HEADS, GLA_DK, GLA_DV), lambda i: (i, 0, 0, 0)),
            pl.BlockSpec((sb, POOL_HIST, D_MODEL), lambda i: (i, 0, 0)),
        ],
        out_shape=[
            jax.ShapeDtypeStruct((N_BRANCH, nb, 1, D_MODEL), F32),
            jax.ShapeDtypeStruct(s0.shape[1:], F32),
            jax.ShapeDtypeStruct(hist.shape[1:], F32),
        ],
        scratch_shapes=[
            pltpu.VMEM((rows_pad, D_MODEL), F32),
            pltpu.VMEM((rows_pad, D_MODEL), F32),
        ],
        compiler_params=_params(("parallel",), 40),
        name="sample_mix",
    )(p4, la3, s0, hist, ck, cv, pool_w, pool_scale, gla_gain)


def _merge_kernel(br_ref, m0_ref, m1_ref, m2_ref, x_ref, wb_ref, wo_ref, fg_ref, out_ref, *, final):
    merged = None
    for n, m_ref in enumerate((m0_ref, m1_ref, m2_ref)):
        term = jax.nn.sigmoid(m_ref[0].astype(F32)) * _dot(br_ref[n].astype(BF), wb_ref[n])
        merged = term if merged is None else merged + term
    x_new = x_ref[...] + _dot(merged.astype(BF), wo_ref[...])
    out_ref[...] = _rms(x_new, fg_ref[...]) if final else x_new


def _merge_out(br, p, x, w_branch, w_out, final_gain, final, tm):
    m_rows = x.shape[0]
    kern = functools.partial(_merge_kernel, final=final)
    row = pl.BlockSpec((tm, D_MODEL), lambda i: (i, 0))

    def slab(j):
        return pl.BlockSpec((1, tm, D_MODEL), lambda i: (j, i, 0))

    return pl.pallas_call(
        kern,
        grid=(m_rows // tm,),
        in_specs=[
            pl.BlockSpec((N_BRANCH, tm, D_MODEL), lambda i: (0, i, 0)),
            slab(SLAB_MERGE), slab(SLAB_MERGE + 1), slab(SLAB_MERGE + 2),
            row,
            _const_spec((N_BRANCH, D_MODEL, D_MODEL)),
            _const_spec((D_MODEL, D_MODEL)),
            _const_spec((1, D_MODEL)),
        ],
        out_specs=row,
        out_shape=jax.ShapeDtypeStruct((m_rows, D_MODEL), F32),
        compiler_params=_params(("parallel",), 32),
        name="merge_out",
    )(br, p, p, p, x, w_branch, w_out, final_gain)


def kernel(x_prompt, x_sample, mem_prompt, cache_mem_k, cache_mem_v, state_gla, state_pool, w_in, w_a2, b_a, gla_gain, pool_w, pool_scale, w_mk, w_mv, w_branch, w_out, norm_gain, final_gain):
    batch, seq, _ = x_prompt.shape
    nb = x_sample.shape[0]
    n_mem = mem_prompt.shape[1]
    depth = w_in.shape[0]
    tm = min(256, seq)

    xp = x_prompt.reshape(batch * seq, D_MODEL)
    xs = x_sample.reshape(nb, D_MODEL)
    mem = mem_prompt.reshape(batch * n_mem, D_MODEL)
    fgain = final_gain.reshape(1, D_MODEL)

    mk_list, mv_list, glap_list, poolp_list, glas_list, pools_list = [], [], [], [], [], []
    for l in range(depth):
        final = l == depth - 1
        w_main = jnp.concatenate([w_in[l, :, :ALR_START], w_in[l, :, ALR_START + GATE_RANK:]], axis=1).astype(BF)
        w_alr = w_in[l, :, ALR_START:ALR_START + GATE_RANK].astype(BF)
        wa2 = w_a2[l].astype(BF)
        ba = b_a[l].reshape(1, GLA_KEY_WIDTH)
        ngain = norm_gain[l].reshape(1, D_MODEL)
        ggain = gla_gain[l].reshape(1, D_MODEL)
        pscale = pool_scale[l].reshape(1, D_MODEL)
        pw = pool_w[l].astype(BF)
        wb = w_branch[l].astype(BF)
        wo = w_out[l].astype(BF)

        mk, mv, mk_bf, mv_bf = _kvproj(mem, w_mk[l].astype(BF), w_mv[l].astype(BF), tm=min(256, batch * n_mem))

        p, la = _inproj(xp, ngain, w_main, w_alr, wa2, ba, tm)
        br, s_fin = _prompt_mix(p, la, mk_bf, mv_bf, pw, pscale, ggain, batch, seq, n_mem, tm)
        u_all = p[SLAB_U].reshape(batch, seq, D_MODEL)
        xp = _merge_out(br, p, xp, wb, wo, fgain, final, tm)

        ps, las = _inproj(xs, ngain, w_main, w_alr, wa2, ba, nb)
        brs, s_new, hist_new = _sample_mix(
            ps, las, state_gla, state_pool, cache_mem_k, cache_mem_v, pw, pscale, ggain, l)
        xs = _merge_out(brs.reshape(N_BRANCH, nb, D_MODEL), ps, xs, wb, wo, fgain, final, nb)

        mk_list.append(mk.reshape(batch, n_mem, XA_HEADS, XA_HEAD_DIM))
        mv_list.append(mv.reshape(batch, n_mem, XA_HEADS, XA_HEAD_DIM))
        glap_list.append(s_fin)
        poolp_list.append(u_all[:, seq - POOL_HIST:, :].astype(F32))
        glas_list.append(s_new)
        pools_list.append(hist_new)

    return (xp.reshape(batch, seq, D_MODEL), xs.reshape(nb, 1, D_MODEL),
            jnp.stack(mk_list), jnp.stack(mv_list), jnp.stack(glap_list),
            jnp.stack(poolp_list), jnp.stack(glas_list), jnp.stack(pools_list))
```

```python
import functools

import jax
import jax.numpy as jnp
from jax import lax
from jax.experimental import pallas as pl
from jax.experimental.pallas import tpu as pltpu

D_MODEL = 1024
GLA_HEADS = 4
GLA_DK = 128
GLA_DV = 256
GLA_KEY_WIDTH = GLA_HEADS * GLA_DK
GATE_RANK = 16
GATE_TAU = 16.0
CHUNK = 64
POOL_WINDOWS = (2, 4, 8, 16)
POOL_GROUP_DIM = 256
POOL_HIST = 15
HIST_PAD = 16
XA_HEADS = 4
XA_HEAD_DIM = 256
N_BRANCH = 3
EPS = 1e-6
SUBLANES = 8

SLAB_QK, SLAB_V, SLAB_GLA_G, SLAB_U, SLAB_POOL_G, SLAB_XQ, SLAB_XG, SLAB_MERGE = 0, 1, 2, 3, 4, 5, 6, 7
N_SLABS = 10
N_MIX_SLABS = 7
N_HEAD_SLABS = 3
ALR_START = N_HEAD_SLABS * D_MODEL

BF = jnp.bfloat16
F32 = jnp.float32
MIB = 1 << 20


def _dot(a, b):
    return jnp.dot(a, b, preferred_element_type=F32)


def _dot_nt(a, b):
    return lax.dot_general(a, b, (((1,), (1,)), ((), ())), preferred_element_type=F32)


def _dot_tn(a, b):
    return lax.dot_general(a, b, (((0,), (0,)), ((), ())), preferred_element_type=F32)


def _silu(x):
    return x * jax.nn.sigmoid(x)


def _rms(x, gain):
    ms = jnp.mean(x * x, axis=-1, keepdims=True)
    return x * lax.rsqrt(ms + EPS) * gain


def _const_spec(shape):
    zeros = (0,) * len(shape)
    return pl.BlockSpec(shape, lambda *_: zeros, pipeline_mode=pl.Buffered(1))


def _params(sem, vmem_mib):
    return pltpu.CompilerParams(dimension_semantics=sem, vmem_limit_bytes=vmem_mib * MIB)


def _drop_alias_refs(body, n_in, n_alias):
    def kern(*refs):
        return body(*refs[:n_in], *refs[n_in + n_alias:])
    return kern


def _kvproj_kernel(m_ref, wk_ref, wv_ref, k_ref, v_ref, kb_ref, vb_ref):
    m = m_ref[...].astype(BF)
    k = _dot(m, wk_ref[0])
    v = _dot(m, wv_ref[0])
    for h in range(XA_HEADS):
        cs = slice(h * XA_HEAD_DIM, (h + 1) * XA_HEAD_DIM)
        k_ref[0, 0, :, h, :] = k[:, cs]
        v_ref[0, 0, :, h, :] = v[:, cs]
    kb_ref[0] = k.astype(BF)
    vb_ref[0] = v.astype(BF)


def _kvproj(mem, wk, wv, batch, n_mem):
    depth = wk.shape[0]
    w_spec = pl.BlockSpec((1, D_MODEL, D_MODEL), lambda l, b: (l, 0, 0))
    out5 = pl.BlockSpec((1, 1, n_mem, XA_HEADS, XA_HEAD_DIM), lambda l, b: (l, b, 0, 0, 0))
    out_bf = pl.BlockSpec((1, n_mem, D_MODEL), lambda l, b: (l, b, 0))
    return pl.pallas_call(
        _kvproj_kernel,
        grid=(depth, batch),
        in_specs=[pl.BlockSpec((n_mem, D_MODEL), lambda l, b: (b, 0)), w_spec, w_spec],
        out_specs=[out5, out5, out_bf, out_bf],
        out_shape=[jax.ShapeDtypeStruct((depth, batch, n_mem, XA_HEADS, XA_HEAD_DIM), F32)] * 2
        + [jax.ShapeDtypeStruct((depth, batch * n_mem, D_MODEL), BF)] * 2,
        compiler_params=_params(("parallel", "parallel"), 32),
        name="kvproj",
    )(mem, wk, wv)


def _inproj_kernel(x_ref, g_ref, wa_ref, wb_ref, walr_ref, wa2_ref, ba_ref, p_ref, la_ref):
    h = _rms(x_ref[...], g_ref[...]).astype(BF)
    for j in range(N_SLABS):
        if j < N_HEAD_SLABS:
            w = wa_ref[:, j * D_MODEL:(j + 1) * D_MODEL]
        else:
            w = wb_ref[:, (j - N_HEAD_SLABS) * D_MODEL:(j - N_HEAD_SLABS + 1) * D_MODEL]
        p_ref[j] = _dot(h, w).astype(p_ref.dtype)
    alr = _dot(h, walr_ref[...])
    z = _dot(alr.astype(BF), wa2_ref[...]) + ba_ref[...]
    la_ref[...] = (jnp.minimum(z, 0.0) - jnp.log1p(jnp.exp(-jnp.abs(z)))) * (1.0 / GATE_TAU)


def _inproj(x, gain, w_head, w_tail, w_alr, w_a2, b_a, tm, out_dtype):
    m_rows = x.shape[0]
    return pl.pallas_call(
        _inproj_kernel,
        grid=(m_rows // tm,),
        in_specs=[
            pl.BlockSpec((tm, D_MODEL), lambda i: (i, 0)),
            _const_spec((1, D_MODEL)),
            _const_spec((D_MODEL, N_HEAD_SLABS * D_MODEL)),
            _const_spec((D_MODEL, (N_SLABS - N_HEAD_SLABS) * D_MODEL)),
            _const_spec((D_MODEL, GATE_RANK)),
            _const_spec((GATE_RANK, GLA_KEY_WIDTH)),
            _const_spec((1, GLA_KEY_WIDTH)),
        ],
        out_specs=[
            pl.BlockSpec((N_SLABS, tm, D_MODEL), lambda i: (0, i, 0)),
            pl.BlockSpec((tm, GLA_KEY_WIDTH), lambda i: (i, 0)),
        ],
        out_shape=[
            jax.ShapeDtypeStruct((N_SLABS, m_rows, D_MODEL), out_dtype),
            jax.ShapeDtypeStruct((m_rows, GLA_KEY_WIDTH), F32),
        ],
        compiler_params=_params(("parallel",), 48),
        name="inproj",
    )(x, gain, w_head, w_tail, w_alr, w_a2, b_a)


def _prompt_mix_kernel(p_ref, la_ref, mk_ref, mv_ref, pw_ref, ps_ref, gg_ref,
                       br_ref, s_out_ref, hist_out_ref, st_ref, ubuf_ref, *, tm):
    t = pl.program_id(1)

    @pl.when(t == 0)
    def _():
        st_ref[...] = jnp.zeros_like(st_ref)
        ubuf_ref[0:HIST_PAD, :] = jnp.zeros((HIST_PAD, D_MODEL), F32)

    row = lax.broadcasted_iota(jnp.int32, (tm, tm), 0)
    col = lax.broadcasted_iota(jnp.int32, (tm, tm), 1)
    same_chunk = (row // CHUNK) == (col // CHUNK)
    cum_mat = jnp.where(same_chunk & (row >= col), 1.0, 0.0).astype(BF)
    la = la_ref[...]
    la_hi = la.astype(BF)
    la_lo = (la - la_hi.astype(F32)).astype(BF)
    bcum = _dot(cum_mat, la_hi) + _dot(cum_mat, la_lo)
    crow = lax.broadcasted_iota(jnp.int32, (CHUNK, CHUNK), 0)
    ccol = lax.broadcasted_iota(jnp.int32, (CHUNK, CHUNK), 1)
    causal = crow >= ccol
    for c in range(tm // CHUNK):
        rows = slice(c * CHUNK, (c + 1) * CHUNK)
        for h in range(GLA_HEADS):
            kc = slice(h * GLA_DK, (h + 1) * GLA_DK)
            kc2 = slice(GLA_KEY_WIDTH + h * GLA_DK, GLA_KEY_WIDTH + (h + 1) * GLA_DK)
            vc = slice(h * GLA_DV, (h + 1) * GLA_DV)
            b = bcum[rows, kc]
            b_last = b[CHUNK - 1:CHUNK, :]
            q = p_ref[SLAB_QK, rows, kc].astype(F32)
            k = p_ref[SLAB_QK, rows, kc2].astype(F32)
            v = p_ref[SLAB_V, rows, vc]
            q_dec = (q * (GLA_DK ** -0.5) * jnp.exp(b)).astype(BF)
            k_inv = (k * jnp.exp(-b)).astype(BF)
            k_end = (k * jnp.exp(b_last - b)).astype(BF)
            att = jnp.where(causal, _dot_nt(q_dec, k_inv), 0.0)
            st = st_ref[h]
            o = _dot(att.astype(BF), v) + _dot_nt(q_dec, st.astype(BF))
            st_ref[h] = jnp.exp(b_last) * st + _dot_tn(v, k_end)
            g = p_ref[SLAB_GLA_G, rows, vc].astype(F32)
            br_ref[0, rows, vc] = (_rms(o, gg_ref[:, vc]) * _silu(g)).astype(BF)

    u = p_ref[SLAB_U].astype(F32)
    ubuf_ref[HIST_PAD:HIST_PAD + tm, :] = u
    pos = t * tm + lax.broadcasted_iota(jnp.int32, (tm, 1), 0)
    for g, w in enumerate(POOL_WINDOWS):
        cs = slice(g * POOL_GROUP_DIM, (g + 1) * POOL_GROUP_DIM)
        ug = u[:, cs]
        s = ug
        for j in range(1, w):
            s = s + ubuf_ref[HIST_PAD - j:HIST_PAD - j + tm, cs]
        cnt = jnp.minimum(w, pos + 1).astype(F32)
        diff = s / cnt - ug
        mixed = _dot(diff.astype(BF), pw_ref[g]) * ps_ref[:, cs]
        pg = p_ref[SLAB_POOL_G, :, cs].astype(F32)
        br_ref[1, :, cs] = (mixed * _silu(pg)).astype(BF)
    ubuf_ref[0:HIST_PAD, :] = ubuf_ref[tm:tm + HIST_PAD, :]

    for h in range(XA_HEADS):
        cs = slice(h * XA_HEAD_DIM, (h + 1) * XA_HEAD_DIM)
        s = _dot_nt(p_ref[SLAB_XQ, :, cs], mk_ref[0, :, cs]) * (XA_HEAD_DIM ** -0.5)
        p = jnp.exp(s - jnp.max(s, axis=-1, keepdims=True))
        denom = jnp.sum(p, axis=-1, keepdims=True)
        o = _dot(p.astype(BF), mv_ref[0, :, cs]) / denom
        xg = p_ref[SLAB_XG, :, cs].astype(F32)
        br_ref[2, :, cs] = (o * _silu(xg)).astype(BF)

    @pl.when(t == pl.num_programs(1) - 1)
    def _():
        for h in range(GLA_HEADS):
            s_out_ref[0, 0, h] = st_ref[h].T
        hist_out_ref[0, 0] = ubuf_ref[1:HIST_PAD, :]


def _prompt_mix(p, la, mk, mv, pool_w, pool_scale, gla_gain, carry, layer, depth, batch, seq, n_mem, tm):
    nt = seq // tm
    n_alias = 0 if carry is None else len(carry)
    n_in = 7
    kern = _drop_alias_refs(functools.partial(_prompt_mix_kernel, tm=tm), n_in, n_alias)
    any_spec = pl.BlockSpec(memory_space=pl.ANY)
    return pl.pallas_call(
        kern,
        grid=(batch, nt),
        in_specs=[
            pl.BlockSpec((N_MIX_SLABS, tm, D_MODEL), lambda b, t: (0, b * nt + t, 0)),
            pl.BlockSpec((tm, GLA_KEY_WIDTH), lambda b, t: (b * nt + t, 0)),
            pl.BlockSpec((1, n_mem, D_MODEL), lambda b, t: (layer, b, 0)),
            pl.BlockSpec((1, n_mem, D_MODEL), lambda b, t: (layer, b, 0)),
            _const_spec((len(POOL_WINDOWS), POOL_GROUP_DIM, POOL_GROUP_DIM)),
            _const_spec((1, D_MODEL)),
            _const_spec((1, D_MODEL)),
        ] + [any_spec] * n_alias,
        out_specs=[
            pl.BlockSpec((N_BRANCH, tm, D_MODEL), lambda b, t: (0, b * nt + t, 0)),
            pl.BlockSpec((1, 1, GLA_HEADS, GLA_DK, GLA_DV), lambda b, t: (layer, b, 0, 0, 0)),
            pl.BlockSpec((1, 1, POOL_HIST, D_MODEL), lambda b, t: (layer, b, 0, 0)),
        ],
        out_shape=[
            jax.ShapeDtypeStruct((N_BRANCH, batch * seq, D_MODEL), BF),
            jax.ShapeDtypeStruct((depth, batch, GLA_HEADS, GLA_DK, GLA_DV), F32),
            jax.ShapeDtypeStruct((depth, batch, POOL_HIST, D_MODEL), F32),
        ],
        scratch_shapes=[
            pltpu.VMEM((GLA_HEADS, GLA_DV, GLA_DK), F32),
            pltpu.VMEM((HIST_PAD + tm, D_MODEL), F32),
        ],
        input_output_aliases={n_in + a: 1 + a for a in range(n_alias)},
        compiler_params=_params(("parallel", "arbitrary"), 40),
        name="prompt_mix",
    )(p, la, mk, mv, pool_w, pool_scale, gla_gain, *(carry or ()))


def _sample_mix_kernel(p_ref, la_ref, s0_ref, hist_ref, ck_ref, cv_ref, pw_ref, ps_ref, gg_ref,
                       br_ref, s_out_ref, hist_out_ref, diff_ref, *, sb):
    r0 = pl.program_id(1) * sb
    erow = lax.broadcasted_iota(jnp.int32, (GLA_DK, GLA_DK), 0)
    ecol = lax.broadcasted_iota(jnp.int32, (GLA_DK, GLA_DK), 1)
    eye = erow == ecol

    def to_col(x):
        return jnp.sum(jnp.where(eye, jnp.broadcast_to(x, (GLA_DK, GLA_DK)), 0.0), axis=1, keepdims=True)

    for i in range(sb):
        r = pl.ds(r0 + i, 1)
        la = la_ref[r, :]
        qk = p_ref[SLAB_QK, r, :]
        vv = p_ref[SLAB_V, r, :]
        gla_g = p_ref[SLAB_GLA_G, r, :]
        for h in range(GLA_HEADS):
            kc = slice(h * GLA_DK, (h + 1) * GLA_DK)
            kc2 = slice(GLA_KEY_WIDTH + h * GLA_DK, GLA_KEY_WIDTH + (h + 1) * GLA_DK)
            vc = slice(h * GLA_DV, (h + 1) * GLA_DV)
            a_col = to_col(jnp.exp(la[:, kc]))
            q_col = to_col(qk[:, kc] * (GLA_DK ** -0.5))
            k_col = to_col(qk[:, kc2])
            s_new = a_col * s0_ref[0, i, h] + k_col * vv[:, vc]
            s_out_ref[0, i, h] = s_new
            o = jnp.sum(q_col * s_new, axis=0, keepdims=True)
            br_ref[0, r, vc] = _rms(o, gg_ref[:, vc]) * _silu(gla_g[:, vc])

        u = p_ref[SLAB_U, r, :]
        for g, w in enumerate(POOL_WINDOWS):
            cs = slice(g * POOL_GROUP_DIM, (g + 1) * POOL_GROUP_DIM)
            past = jnp.sum(hist_ref[0, i, POOL_HIST - (w - 1):POOL_HIST, cs], axis=0, keepdims=True)
            diff_ref[r, cs] = (u[:, cs] + past) / float(w) - u[:, cs]
        hist_out_ref[0, i, 0:POOL_HIST - 1, :] = hist_ref[0, i, 1:POOL_HIST, :]
        hist_out_ref[0, i, POOL_HIST - 1:POOL_HIST, :] = u

        xq = p_ref[SLAB_XQ, r, :]
        xg = p_ref[SLAB_XG, r, :]
        for h in range(XA_HEADS):
            cs = slice(h * XA_HEAD_DIM, (h + 1) * XA_HEAD_DIM)
            s = jnp.sum(ck_ref[0, i, :, h, :] * xq[:, cs], axis=1, keepdims=True) * (XA_HEAD_DIM ** -0.5)
            p = jnp.exp(s - jnp.max(s, axis=0, keepdims=True))
            denom = jnp.sum(p, axis=0, keepdims=True)
            o = jnp.sum(p * cv_ref[0, i, :, h, :], axis=0, keepdims=True) / denom
            br_ref[2, r, cs] = o * _silu(xg[:, cs])

    @pl.when(pl.program_id(1) == pl.num_programs(1) - 1)
    def _():
        for g in range(len(POOL_WINDOWS)):
            cs = slice(g * POOL_GROUP_DIM, (g + 1) * POOL_GROUP_DIM)
            mixed = _dot(diff_ref[:, cs].astype(BF), pw_ref[g]) * ps_ref[:, cs]
            br_ref[1, :, cs] = mixed * _silu(p_ref[SLAB_POOL_G, :, cs])


def _sample_mix(p, la, s0, hist, ck, cv, pool_w, pool_scale, gla_gain, carry, layer, sb=4):
    depth, nb = s0.shape[0], s0.shape[1]
    n_mem = ck.shape[2]
    rb = SUBLANES
    halves = rb // sb
    n_alias = 0 if carry is None else len(carry)
    n_in = 9
    kern = _drop_alias_refs(functools.partial(_sample_mix_kernel, sb=sb), n_in, n_alias)
    any_spec = pl.BlockSpec(memory_space=pl.ANY)
    return pl.pallas_call(
        kern,
        grid=(nb // rb, halves),
        in_specs=[
            pl.BlockSpec((N_MIX_SLABS, rb, D_MODEL), lambda i, j: (0, i, 0)),
            pl.BlockSpec((rb, GLA_KEY_WIDTH), lambda i, j: (i, 0)),
            pl.BlockSpec((1, sb, GLA_HEADS, GLA_DK, GLA_DV), lambda i, j: (layer, i * halves + j, 0, 0, 0)),
            pl.BlockSpec((1, sb, POOL_HIST, D_MODEL), lambda i, j: (layer, i * halves + j, 0, 0)),
            pl.BlockSpec((1, sb, n_mem, XA_HEADS, XA_HEAD_DIM), lambda i, j: (layer, i * halves + j, 0, 0, 0)),
            pl.BlockSpec((1, sb, n_mem, XA_HEADS, XA_HEAD_DIM), lambda i, j: (layer, i * halves + j, 0, 0, 0)),
            _const_spec((len(POOL_WINDOWS), POOL_GROUP_DIM, POOL_GROUP_DIM)),
            _const_spec((1, D_MODEL)),
            _const_spec((1, D_MODEL)),
        ] + [any_spec] * n_alias,
        out_specs=[
            pl.BlockSpec((N_BRANCH, rb, D_MODEL), lambda i, j: (0, i, 0)),
            pl.BlockSpec((1, sb, GLA_HEADS, GLA_DK, GLA_DV), lambda i, j: (layer, i * halves + j, 0, 0, 0)),
            pl.BlockSpec((1, sb, POOL_HIST, D_MODEL), lambda i, j: (layer, i * halves + j, 0, 0)),
        ],
        out_shape=[
            jax.ShapeDtypeStruct((N_BRANCH, nb, D_MODEL), F32),
            jax.ShapeDtypeStruct(s0.shape, F32),
            jax.ShapeDtypeStruct(hist.shape, F32),
        ],
        scratch_shapes=[pltpu.VMEM((rb, D_MODEL), F32)],
        input_output_aliases={n_in + a: 1 + a for a in range(n_alias)},
        compiler_params=_params(("parallel", "arbitrary"), 40),
        name="sample_mix",
    )(p, la, s0, hist, ck, cv, pool_w, pool_scale, gla_gain, *(carry or ()))


def _merge_kernel(br_ref, m0_ref, m1_ref, m2_ref, x_ref, wb_ref, wo_ref, fg_ref, out_ref, *, final):
    merged = None
    for n, m_ref in enumerate((m0_ref, m1_ref, m2_ref)):
        gate = jax.nn.sigmoid(m_ref[0].astype(F32))
        term = gate * _dot(br_ref[n].astype(BF), wb_ref[n])
        merged = term if merged is None else merged + term
    x_new = x_ref[...] + _dot(merged.astype(BF), wo_ref[...])
    out_ref[...] = _rms(x_new, fg_ref[...]) if final else x_new


def _merge_out(br, p, x, w_branch, w_out, final_gain, final, tm):
    m_rows = x.shape[0]
    kern = functools.partial(_merge_kernel, final=final)
    row = pl.BlockSpec((tm, D_MODEL), lambda i: (i, 0))

    def slab(j):
        return pl.BlockSpec((1, tm, D_MODEL), lambda i: (j, i, 0))

    return pl.pallas_call(
        kern,
        grid=(m_rows // tm,),
        in_specs=[
            pl.BlockSpec((N_BRANCH, tm, D_MODEL), lambda i: (0, i, 0)),
            slab(SLAB_MERGE), slab(SLAB_MERGE + 1), slab(SLAB_MERGE + 2),
            row,
            _const_spec((N_BRANCH, D_MODEL, D_MODEL)),
            _const_spec((D_MODEL, D_MODEL)),
            _const_spec((1, D_MODEL)),
        ],
        out_specs=row,
        out_shape=jax.ShapeDtypeStruct((m_rows, D_MODEL), F32),
        compiler_params=_params(("parallel",), 32),
        name="merge_out",
    )(br, p, p, p, x, w_branch, w_out, final_gain)


def kernel(x_prompt, x_sample, mem_prompt, cache_mem_k, cache_mem_v, state_gla, state_pool, w_in, w_a2, b_a, gla_gain, pool_w, pool_scale, w_mk, w_mv, w_branch, w_out, norm_gain, final_gain):
    batch, seq, _ = x_prompt.shape
    nb = x_sample.shape[0]
    n_mem = mem_prompt.shape[1]
    depth = w_in.shape[0]
    tm = min(256, seq)

    xp = x_prompt.reshape(batch * seq, D_MODEL)
    xs = x_sample.reshape(nb, D_MODEL)
    mem = mem_prompt.reshape(batch * n_mem, D_MODEL)
    fgain = final_gain.reshape(1, D_MODEL)

    mk, mv, mk_bf, mv_bf = _kvproj(mem, w_mk.astype(BF), w_mv.astype(BF), batch, n_mem)

    carry_p, carry_s = None, None
    for l in range(depth):
        final = l == depth - 1
        w_head = w_in[l, :, :ALR_START].astype(BF)
        w_tail = w_in[l, :, ALR_START + GATE_RANK:].astype(BF)
        w_alr = w_in[l, :, ALR_START:ALR_START + GATE_RANK].astype(BF)
        wa2 = w_a2[l].astype(BF)
        ba = b_a[l].reshape(1, GLA_KEY_WIDTH)
        ngain = norm_gain[l].reshape(1, D_MODEL)
        ggain = gla_gain[l].reshape(1, D_MODEL)
        pscale = pool_scale[l].reshape(1, D_MODEL)
        pw = pool_w[l].astype(BF)
        wb = w_branch[l].astype(BF)
        wo = w_out[l].astype(BF)

        p, la = _inproj(xp, ngain, w_head, w_tail, w_alr, wa2, ba, tm, BF)
        br, s_all, hist_all = _prompt_mix(p, la, mk_bf, mv_bf, pw, pscale, ggain, carry_p, l, depth,
                                          batch, seq, n_mem, tm)
        carry_p = (s_all, hist_all)
        xp = _merge_out(br, p, xp, wb, wo, fgain, final, tm)

        ps, las = _inproj(xs, ngain, w_head, w_tail, w_alr, wa2, ba, nb, F32)
        brs, s_new, hist_new = _sample_mix(ps, las, state_gla, state_pool, cache_mem_k, cache_mem_v,
                                           pw, pscale, ggain, carry_s, l)
        carry_s = (s_new, hist_new)
        xs = _merge_out(brs, ps, xs, wb, wo, fgain, final, nb)

    return (xp.reshape(batch, seq, D_MODEL), xs.reshape(nb, 1, D_MODEL),
            mk, mv, carry_p[0], carry_p[1], carry_s[0], carry_s[1])
```

```python
import functools

import jax
import jax.numpy as jnp
from jax import lax
from jax.experimental import pallas as pl
from jax.experimental.pallas import tpu as pltpu

D_MODEL = 1024
GLA_HEADS = 4
GLA_DK = 128
GLA_DV = 256
GLA_KEY_WIDTH = GLA_HEADS * GLA_DK
GATE_RANK = 16
GATE_TAU = 16.0
CHUNK = 64
POOL_WINDOWS = (2, 4, 8, 16)
POOL_GROUP_DIM = 256
POOL_HIST = 15
HIST_PAD = 16
XA_HEADS = 4
XA_HEAD_DIM = 256
N_BRANCH = 3
EPS = 1e-6
SUBLANES = 8

SLAB_QK, SLAB_V, SLAB_GLA_G, SLAB_U, SLAB_POOL_G, SLAB_XQ, SLAB_XG, SLAB_MERGE = 0, 1, 2, 3, 4, 5, 6, 7
N_SLABS = 10
N_MIX_SLABS = 7
N_HEAD_SLABS = 3
ALR_START = N_HEAD_SLABS * D_MODEL

BF = jnp.bfloat16
F32 = jnp.float32
MIB = 1 << 20


def _dot(a, b):
    return jnp.dot(a, b, preferred_element_type=F32)


def _dot_nt(a, b):
    return lax.dot_general(a, b, (((1,), (1,)), ((), ())), preferred_element_type=F32)


def _dot_tn(a, b):
    return lax.dot_general(a, b, (((0,), (0,)), ((), ())), preferred_element_type=F32)


def _silu(x):
    return x * jax.nn.sigmoid(x)


def _rms(x, gain):
    ms = jnp.mean(x * x, axis=-1, keepdims=True)
    return x * lax.rsqrt(ms + EPS) * gain


def _const_spec(shape):
    zeros = (0,) * len(shape)
    return pl.BlockSpec(shape, lambda *_: zeros, pipeline_mode=pl.Buffered(1))


def _params(sem, vmem_mib):
    return pltpu.CompilerParams(dimension_semantics=sem, vmem_limit_bytes=vmem_mib * MIB)


def _drop_alias_refs(body, n_in, n_alias):
    def kern(*refs):
        return body(*refs[:n_in], *refs[n_in + n_alias:])
    return kern


def _kvproj_kernel(m_ref, wk_ref, wv_ref, k_ref, v_ref, kb_ref, vb_ref):
    m = m_ref[...].astype(BF)
    k = _dot(m, wk_ref[0])
    v = _dot(m, wv_ref[0])
    for h in range(XA_HEADS):
        cs = slice(h * XA_HEAD_DIM, (h + 1) * XA_HEAD_DIM)
        k_ref[0, 0, :, h, :] = k[:, cs]
        v_ref[0, 0, :, h, :] = v[:, cs]
    kb_ref[0] = k.astype(BF)
    vb_ref[0] = v.astype(BF)


def _kvproj(mem, wk, wv, batch, n_mem):
    depth = wk.shape[0]
    w_spec = pl.BlockSpec((1, D_MODEL, D_MODEL), lambda l, b: (l, 0, 0))
    out5 = pl.BlockSpec((1, 1, n_mem, XA_HEADS, XA_HEAD_DIM), lambda l, b: (l, b, 0, 0, 0))
    out_bf = pl.BlockSpec((1, n_mem, D_MODEL), lambda l, b: (l, b, 0))
    return pl.pallas_call(
        _kvproj_kernel,
        grid=(depth, batch),
        in_specs=[pl.BlockSpec((n_mem, D_MODEL), lambda l, b: (b, 0)), w_spec, w_spec],
        out_specs=[out5, out5, out_bf, out_bf],
        out_shape=[jax.ShapeDtypeStruct((depth, batch, n_mem, XA_HEADS, XA_HEAD_DIM), F32)] * 2
        + [jax.ShapeDtypeStruct((depth, batch * n_mem, D_MODEL), BF)] * 2,
        compiler_params=_params(("parallel", "parallel"), 32),
        name="kvproj",
    )(mem, wk, wv)


def _inproj_body(x, g_ref, wa_ref, wb_ref, walr_ref, wa2_ref, ba_ref, p_out, la_out):
    h = _rms(x, g_ref[...]).astype(BF)
    for j in range(N_SLABS):
        if j < N_HEAD_SLABS:
            w = wa_ref[:, j * D_MODEL:(j + 1) * D_MODEL]
        else:
            w = wb_ref[:, (j - N_HEAD_SLABS) * D_MODEL:(j - N_HEAD_SLABS + 1) * D_MODEL]
        p_out[j] = _dot(h, w).astype(p_out.dtype)
    alr = _dot(h, walr_ref[...])
    z = _dot(alr.astype(BF), wa2_ref[...]) + ba_ref[...]
    la_out[...] = (jnp.minimum(z, 0.0) - jnp.log1p(jnp.exp(-jnp.abs(z)))) * (1.0 / GATE_TAU)


def _merge_body(br_ref, p, x, wbr_ref, wo_ref, fg_ref, final):
    merged = None
    for n in range(N_BRANCH):
        gate = jax.nn.sigmoid(p[SLAB_MERGE + n].astype(F32))
        term = gate * _dot(br_ref[n].astype(BF), wbr_ref[n])
        merged = term if merged is None else merged + term
    x_new = x + _dot(merged.astype(BF), wo_ref[...])
    return _rms(x_new, fg_ref[...]) if final else x_new


def _prompt_mix_body(p, la_ref, mk_ref, mv_ref, pw_ref, ps_ref, gg_ref, br_ref, st_ref, ubuf_ref, t, tm):
    row = lax.broadcasted_iota(jnp.int32, (tm, tm), 0)
    col = lax.broadcasted_iota(jnp.int32, (tm, tm), 1)
    same_chunk = (row // CHUNK) == (col // CHUNK)
    cum_mat = jnp.where(same_chunk & (row >= col), 1.0, 0.0).astype(BF)
    la = la_ref[...]
    la_hi = la.astype(BF)
    la_lo = (la - la_hi.astype(F32)).astype(BF)
    bcum = _dot(cum_mat, la_hi) + _dot(cum_mat, la_lo)
    crow = lax.broadcasted_iota(jnp.int32, (CHUNK, CHUNK), 0)
    ccol = lax.broadcasted_iota(jnp.int32, (CHUNK, CHUNK), 1)
    causal = crow >= ccol
    for c in range(tm // CHUNK):
        rows = slice(c * CHUNK, (c + 1) * CHUNK)
        for h in range(GLA_HEADS):
            kc = slice(h * GLA_DK, (h + 1) * GLA_DK)
            kc2 = slice(GLA_KEY_WIDTH + h * GLA_DK, GLA_KEY_WIDTH + (h + 1) * GLA_DK)
            vc = slice(h * GLA_DV, (h + 1) * GLA_DV)
            b = bcum[rows, kc]
            b_last = b[CHUNK - 1:CHUNK, :]
            q = p[SLAB_QK, rows, kc].astype(F32)
            k = p[SLAB_QK, rows, kc2].astype(F32)
            v = p[SLAB_V, rows, vc]
            q_dec = (q * (GLA_DK ** -0.5) * jnp.exp(b)).astype(BF)
            k_inv = (k * jnp.exp(-b)).astype(BF)
            k_end = (k * jnp.exp(b_last - b)).astype(BF)
            att = jnp.where(causal, _dot_nt(q_dec, k_inv), 0.0)
            st = st_ref[h]
            o = _dot(att.astype(BF), v) + _dot_nt(q_dec, st.astype(BF))
            st_ref[h] = jnp.exp(b_last) * st + _dot_tn(v, k_end)
            g = p[SLAB_GLA_G, rows, vc].astype(F32)
            br_ref[0, rows, vc] = (_rms(o, gg_ref[:, vc]) * _silu(g)).astype(BF)

    u = p[SLAB_U].astype(F32)
    ubuf_ref[HIST_PAD:HIST_PAD + tm, :] = u
    pos = t * tm + lax.broadcasted_iota(jnp.int32, (tm, 1), 0)
    for g, w in enumerate(POOL_WINDOWS):
        cs = slice(g * POOL_GROUP_DIM, (g + 1) * POOL_GROUP_DIM)
        ug = u[:, cs]
        s = ug
        for j in range(1, w):
            s = s + ubuf_ref[HIST_PAD - j:HIST_PAD - j + tm, cs]
        cnt = jnp.minimum(w, pos + 1).astype(F32)
        diff = s / cnt - ug
        mixed = _dot(diff.astype(BF), pw_ref[g]) * ps_ref[:, cs]
        pg = p[SLAB_POOL_G, :, cs].astype(F32)
        br_ref[1, :, cs] = (mixed * _silu(pg)).astype(BF)
    ubuf_ref[0:HIST_PAD, :] = ubuf_ref[tm:tm + HIST_PAD, :]

    for h in range(XA_HEADS):
        cs = slice(h * XA_HEAD_DIM, (h + 1) * XA_HEAD_DIM)
        s = _dot_nt(p[SLAB_XQ, :, cs], mk_ref[0, :, cs]) * (XA_HEAD_DIM ** -0.5)
        pr = jnp.exp(s - jnp.max(s, axis=-1, keepdims=True))
        denom = jnp.sum(pr, axis=-1, keepdims=True)
        o = _dot(pr.astype(BF), mv_ref[0, :, cs]) / denom
        xg = p[SLAB_XG, :, cs].astype(F32)
        br_ref[2, :, cs] = (o * _silu(xg)).astype(BF)


def _prompt_layer_kernel(xn_ref, xc_ref, g_ref, wa_ref, wb_ref, walr_ref, wa2_ref, ba_ref,
                         mk_ref, mv_ref, pw_ref, ps_ref, gg_ref, wbr_ref, wo_ref, fg_ref,
                         out_ref, s_out_ref, hist_out_ref,
                         p_scr, la_scr, br_scr, st_ref, ubuf_ref, *, tm, nt, final):
    s = pl.program_id(0)
    t = jnp.maximum(s - 1, 0) % nt
    slot_w = s % 2
    slot_r = 1 - slot_w

    @pl.when(s == 0)
    def _():
        p_scr[1] = jnp.zeros(p_scr.shape[1:], p_scr.dtype)
        la_scr[1] = jnp.zeros(la_scr.shape[1:], la_scr.dtype)

    @pl.when(t == 0)
    def _():
        st_ref[...] = jnp.zeros_like(st_ref)
        ubuf_ref[0:HIST_PAD, :] = jnp.zeros((HIST_PAD, D_MODEL), F32)

    _inproj_body(xn_ref[...], g_ref, wa_ref, wb_ref, walr_ref, wa2_ref, ba_ref,
                 p_scr.at[slot_w], la_scr.at[slot_w])
    p = p_scr.at[slot_r]
    _prompt_mix_body(p, la_scr.at[slot_r], mk_ref, mv_ref, pw_ref, ps_ref, gg_ref, br_scr, st_ref, ubuf_ref, t, tm)
    out_ref[...] = _merge_body(br_scr, p, xc_ref[...], wbr_ref, wo_ref, fg_ref, final)

    @pl.when((s > 0) & (t == nt - 1))
    def _():
        for h in range(GLA_HEADS):
            s_out_ref[0, 0, h] = st_ref[h].T
        hist_out_ref[0, 0] = ubuf_ref[1:HIST_PAD, :]


def _prompt_layer(x, ngain, w_head, w_tail, w_alr, wa2, ba, mk, mv, pool_w, pool_scale, gla_gain,
                  w_branch, w_out, fgain, carry, layer, depth, batch, seq, n_mem, tm, final):
    nt = seq // tm
    n_tiles = batch * nt
    n_alias = 0 if carry is None else len(carry)
    n_in = 16
    kern = _drop_alias_refs(functools.partial(_prompt_layer_kernel, tm=tm, nt=nt, final=final), n_in, n_alias)
    any_spec = pl.BlockSpec(memory_space=pl.ANY)

    def cur(s):
        return jnp.maximum(s - 1, 0)

    def seq_of(s):
        return cur(s) // nt

    return pl.pallas_call(
        kern,
        grid=(n_tiles + 1,),
        in_specs=[
            pl.BlockSpec((tm, D_MODEL), lambda s: (jnp.minimum(s, n_tiles - 1), 0)),
            pl.BlockSpec((tm, D_MODEL), lambda s: (cur(s), 0)),
            _const_spec((1, D_MODEL)),
            _const_spec((D_MODEL, N_HEAD_SLABS * D_MODEL)),
            _const_spec((D_MODEL, (N_SLABS - N_HEAD_SLABS) * D_MODEL)),
            _const_spec((D_MODEL, GATE_RANK)),
            _const_spec((GATE_RANK, GLA_KEY_WIDTH)),
            _const_spec((1, GLA_KEY_WIDTH)),
            pl.BlockSpec((1, n_mem, D_MODEL), lambda s: (layer, seq_of(s), 0)),
            pl.BlockSpec((1, n_mem, D_MODEL), lambda s: (layer, seq_of(s), 0)),
            _const_spec((len(POOL_WINDOWS), POOL_GROUP_DIM, POOL_GROUP_DIM)),
            _const_spec((1, D_MODEL)),
            _const_spec((1, D_MODEL)),
            _const_spec((N_BRANCH, D_MODEL, D_MODEL)),
            _const_spec((D_MODEL, D_MODEL)),
            _const_spec((1, D_MODEL)),
        ] + [any_spec] * n_alias,
        out_specs=[
            pl.BlockSpec((tm, D_MODEL), lambda s: (cur(s), 0)),
            pl.BlockSpec((1, 1, GLA_HEADS, GLA_DK, GLA_DV), lambda s: (layer, seq_of(s), 0, 0, 0)),
            pl.BlockSpec((1, 1, POOL_HIST, D_MODEL), lambda s: (layer, seq_of(s), 0, 0)),
        ],
        out_shape=[
            jax.ShapeDtypeStruct((n_tiles * tm, D_MODEL), F32),
            jax.ShapeDtypeStruct((depth, batch, GLA_HEADS, GLA_DK, GLA_DV), F32),
            jax.ShapeDtypeStruct((depth, batch, POOL_HIST, D_MODEL), F32),
        ],
        scratch_shapes=[
            pltpu.VMEM((2, N_SLABS, tm, D_MODEL), BF),
            pltpu.VMEM((2, tm, GLA_KEY_WIDTH), F32),
            pltpu.VMEM((N_BRANCH, tm, D_MODEL), BF),
            pltpu.VMEM((GLA_HEADS, GLA_DV, GLA_DK), F32),
            pltpu.VMEM((HIST_PAD + tm, D_MODEL), F32),
        ],
        input_output_aliases={n_in + a: 1 + a for a in range(n_alias)},
        compiler_params=_params(("arbitrary",), 60),
        name="prompt_layer",
    )(x, x, ngain, w_head, w_tail, w_alr, wa2, ba, mk, mv, pool_w, pool_scale, gla_gain,
      w_branch, w_out, fgain, *(carry or ()))


def _inproj_kernel(x_ref, g_ref, wa_ref, wb_ref, walr_ref, wa2_ref, ba_ref, p_ref, la_ref):
    _inproj_body(x_ref[...], g_ref, wa_ref, wb_ref, walr_ref, wa2_ref, ba_ref, p_ref, la_ref)


def _inproj(x, gain, w_head, w_tail, w_alr, w_a2, b_a, tm, out_dtype):
    m_rows = x.shape[0]
    return pl.pallas_call(
        _inproj_kernel,
        grid=(m_rows // tm,),
        in_specs=[
            pl.BlockSpec((tm, D_MODEL), lambda i: (i, 0)),
            _const_spec((1, D_MODEL)),
            _const_spec((D_MODEL, N_HEAD_SLABS * D_MODEL)),
            _const_spec((D_MODEL, (N_SLABS - N_HEAD_SLABS) * D_MODEL)),
            _const_spec((D_MODEL, GATE_RANK)),
            _const_spec((GATE_RANK, GLA_KEY_WIDTH)),
            _const_spec((1, GLA_KEY_WIDTH)),
        ],
        out_specs=[
            pl.BlockSpec((N_SLABS, tm, D_MODEL), lambda i: (0, i, 0)),
            pl.BlockSpec((tm, GLA_KEY_WIDTH), lambda i: (i, 0)),
        ],
        out_shape=[
            jax.ShapeDtypeStruct((N_SLABS, m_rows, D_MODEL), out_dtype),
            jax.ShapeDtypeStruct((m_rows, GLA_KEY_WIDTH), F32),
        ],
        compiler_params=_params(("parallel",), 48),
        name="inproj",
    )(x, gain, w_head, w_tail, w_alr, w_a2, b_a)


def _sample_mix_kernel(p_ref, la_ref, s0_ref, hist_ref, ck_ref, cv_ref, pw_ref, ps_ref, gg_ref,
                       br_ref, s_out_ref, hist_out_ref, diff_ref, *, sb):
    r0 = pl.program_id(1) * sb
    erow = lax.broadcasted_iota(jnp.int32, (GLA_DK, GLA_DK), 0)
    ecol = lax.broadcasted_iota(jnp.int32, (GLA_DK, GLA_DK), 1)
    eye = erow == ecol

    def to_col(x):
        return jnp.sum(jnp.where(eye, jnp.broadcast_to(x, (GLA_DK, GLA_DK)), 0.0), axis=1, keepdims=True)

    for i in range(sb):
        r = pl.ds(r0 + i, 1)
        la = la_ref[r, :]
        qk = p_ref[SLAB_QK, r, :]
        vv = p_ref[SLAB_V, r, :]
        gla_g = p_ref[SLAB_GLA_G, r, :]
        for h in range(GLA_HEADS):
            kc = slice(h * GLA_DK, (h + 1) * GLA_DK)
            kc2 = slice(GLA_KEY_WIDTH + h * GLA_DK, GLA_KEY_WIDTH + (h + 1) * GLA_DK)
            vc = slice(h * GLA_DV, (h + 1) * GLA_DV)
            a_col = to_col(jnp.exp(la[:, kc]))
            q_col = to_col(qk[:, kc] * (GLA_DK ** -0.5))
            k_col = to_col(qk[:, kc2])
            s_new = a_col * s0_ref[0, i, h] + k_col * vv[:, vc]
            s_out_ref[0, i, h] = s_new
            o = jnp.sum(q_col * s_new, axis=0, keepdims=True)
            br_ref[0, r, vc] = _rms(o, gg_ref[:, vc]) * _silu(gla_g[:, vc])

        u = p_ref[SLAB_U, r, :]
        for g, w in enumerate(POOL_WINDOWS):
            cs = slice(g * POOL_GROUP_DIM, (g + 1) * POOL_GROUP_DIM)
            past = jnp.sum(hist_ref[0, i, POOL_HIST - (w - 1):POOL_HIST, cs], axis=0, keepdims=True)
            diff_ref[r, cs] = (u[:, cs] + past) / float(w) - u[:, cs]
        hist_out_ref[0, i, 0:POOL_HIST - 1, :] = hist_ref[0, i, 1:POOL_HIST, :]
        hist_out_ref[0, i, POOL_HIST - 1:POOL_HIST, :] = u

        xq = p_ref[SLAB_XQ, r, :]
        xg = p_ref[SLAB_XG, r, :]
        for h in range(XA_HEADS):
            cs = slice(h * XA_HEAD_DIM, (h + 1) * XA_HEAD_DIM)
            s = jnp.sum(ck_ref[0, i, :, h, :] * xq[:, cs], axis=1, keepdims=True) * (XA_HEAD_DIM ** -0.5)
            p = jnp.exp(s - jnp.max(s, axis=0, keepdims=True))
            denom = jnp.sum(p, axis=0, keepdims=True)
            o = jnp.sum(p * cv_ref[0, i, :, h, :], axis=0, keepdims=True) / denom
            br_ref[2, r, cs] = o * _silu(xg[:, cs])

    @pl.when(pl.program_id(1) == pl.num_programs(1) - 1)
    def _():
        for g in range(len(POOL_WINDOWS)):
            cs = slice(g * POOL_GROUP_DIM, (g + 1) * POOL_GROUP_DIM)
            mixed = _dot(diff_ref[:, cs].astype(BF), pw_ref[g]) * ps_ref[:, cs]
            br_ref[1, :, cs] = mixed * _silu(p_ref[SLAB_POOL_G, :, cs])


def _sample_mix(p, la, s0, hist, ck, cv, pool_w, pool_scale, gla_gain, carry, layer, sb=4):
    nb = s0.shape[1]
    n_mem = ck.shape[2]
    rb = SUBLANES
    halves = rb // sb
    n_alias = 0 if carry is None else len(carry)
    n_in = 9
    kern = _drop_alias_refs(functools.partial(_sample_mix_kernel, sb=sb), n_in, n_alias)
    any_spec = pl.BlockSpec(memory_space=pl.ANY)
    return pl.pallas_call(
        kern,
        grid=(nb // rb, halves),
        in_specs=[
            pl.BlockSpec((N_MIX_SLABS, rb, D_MODEL), lambda i, j: (0, i, 0)),
            pl.BlockSpec((rb, GLA_KEY_WIDTH), lambda i, j: (i, 0)),
            pl.BlockSpec((1, sb, GLA_HEADS, GLA_DK, GLA_DV), lambda i, j: (layer, i * halves + j, 0, 0, 0)),
            pl.BlockSpec((1, sb, POOL_HIST, D_MODEL), lambda i, j: (layer, i * halves + j, 0, 0)),
            pl.BlockSpec((1, sb, n_mem, XA_HEADS, XA_HEAD_DIM), lambda i, j: (layer, i * halves + j, 0, 0, 0)),
            pl.BlockSpec((1, sb, n_mem, XA_HEADS, XA_HEAD_DIM), lambda i, j: (layer, i * halves + j, 0, 0, 0)),
            _const_spec((len(POOL_WINDOWS), POOL_GROUP_DIM, POOL_GROUP_DIM)),
            _const_spec((1, D_MODEL)),
            _const_spec((1, D_MODEL)),
        ] + [any_spec] * n_alias,
        out_specs=[
            pl.BlockSpec((N_BRANCH, rb, D_MODEL), lambda i, j: (0, i, 0)),
            pl.BlockSpec((1, sb, GLA_HEADS, GLA_DK, GLA_DV), lambda i, j: (layer, i * halves + j, 0, 0, 0)),
            pl.BlockSpec((1, sb, POOL_HIST, D_MODEL), lambda i, j: (layer, i * halves + j, 0, 0)),
        ],
        out_shape=[
            jax.ShapeDtypeStruct((N_BRANCH, nb, D_MODEL), F32),
            jax.ShapeDtypeStruct(s0.shape, F32),
            jax.ShapeDtypeStruct(hist.shape, F32),
        ],
        scratch_shapes=[pltpu.VMEM((rb, D_MODEL), F32)],
        input_output_aliases={n_in + a: 1 + a for a in range(n_alias)},
        compiler_params=_params(("parallel", "arbitrary"), 40),
        name="sample_mix",
    )(p, la, s0, hist, ck, cv, pool_w, pool_scale, gla_gain, *(carry or ()))


def _merge_kernel(br_ref, p_ref, x_ref, wbr_ref, wo_ref, fg_ref, out_ref, *, final):
    out_ref[...] = _merge_body(br_ref, p_ref, x_ref[...], wbr_ref, wo_ref, fg_ref, final)


def _merge_out(br, p, x, w_branch, w_out, final_gain, final):
    m_rows = x.shape[0]
    whole = lambda shape: pl.BlockSpec(shape, lambda i: (0,) * len(shape))
    return pl.pallas_call(
        functools.partial(_merge_kernel, final=final),
        grid=(1,),
        in_specs=[
            whole((N_BRANCH, m_rows, D_MODEL)),
            whole((N_SLABS, m_rows, D_MODEL)),
            whole((m_rows, D_MODEL)),
            _const_spec((N_BRANCH, D_MODEL, D_MODEL)),
            _const_spec((D_MODEL, D_MODEL)),
            _const_spec((1, D_MODEL)),
        ],
        out_specs=whole((m_rows, D_MODEL)),
        out_shape=jax.ShapeDtypeStruct((m_rows, D_MODEL), F32),
        compiler_params=_params(("arbitrary",), 40),
        name="merge_out",
    )(br, p, x, w_branch, w_out, final_gain)


def kernel(x_prompt, x_sample, mem_prompt, cache_mem_k, cache_mem_v, state_gla, state_pool, w_in, w_a2, b_a, gla_gain, pool_w, pool_scale, w_mk, w_mv, w_branch, w_out, norm_gain, final_gain):
    batch, seq, _ = x_prompt.shape
    nb = x_sample.shape[0]
    n_mem = mem_prompt.shape[1]
    depth = w_in.shape[0]
    tm = min(256, seq)

    xp = x_prompt.reshape(batch * seq, D_MODEL)
    xs = x_sample.reshape(nb, D_MODEL)
    mem = mem_prompt.reshape(batch * n_mem, D_MODEL)
    fgain = final_gain.reshape(1, D_MODEL)

    mk, mv, mk_bf, mv_bf = _kvproj(mem, w_mk.astype(BF), w_mv.astype(BF), batch, n_mem)

    carry_p, carry_s = None, None
    for l in range(depth):
        final = l == depth - 1
        w_head = w_in[l, :, :ALR_START].astype(BF)
        w_tail = w_in[l, :, ALR_START + GATE_RANK:].astype(BF)
        w_alr = w_in[l, :, ALR_START:ALR_START + GATE_RANK].astype(BF)
        wa2 = w_a2[l].astype(BF)
        ba = b_a[l].reshape(1, GLA_KEY_WIDTH)
        ngain = norm_gain[l].reshape(1, D_MODEL)
        ggain = gla_gain[l].reshape(1, D_MODEL)
        pscale = pool_scale[l].reshape(1, D_MODEL)
        pw = pool_w[l].astype(BF)
        wb = w_branch[l].astype(BF)
        wo = w_out[l].astype(BF)

        xp, s_all, hist_all = _prompt_layer(xp, ngain, w_head, w_tail, w_alr, wa2, ba, mk_bf, mv_bf, pw, pscale,
                                            ggain, wb, wo, fgain, carry_p, l, depth, batch, seq, n_mem, tm, final)
        carry_p = (s_all, hist_all)

        ps, las = _inproj(xs, ngain, w_head, w_tail, w_alr, wa2, ba, nb, F32)
        brs, s_new, hist_new = _sample_mix(ps, las, state_gla, state_pool, cache_mem_k, cache_mem_v,
                                           pw, pscale, ggain, carry_s, l)
        carry_s = (s_new, hist_new)
        xs = _merge_out(brs, ps, xs, wb, wo, fgain, final)

    return (xp.reshape(batch, seq, D_MODEL), xs.reshape(nb, 1, D_MODEL),
            mk, mv, carry_p[0], carry_p[1], carry_s[0], carry_s[1])
```

```python
import functools

import jax
import jax.numpy as jnp
from jax import lax
from jax.experimental import pallas as pl
from jax.experimental.pallas import tpu as pltpu

D_MODEL = 1024
GLA_HEADS = 4
GLA_DK = 128
GLA_DV = 256
GLA_KEY_WIDTH = GLA_HEADS * GLA_DK
GATE_RANK = 16
GATE_TAU = 16.0
CHUNK = 64
POOL_WINDOWS = (2, 4, 8, 16)
POOL_GROUP_DIM = 256
POOL_HIST = 15
HIST_PAD = 16
XA_HEADS = 4
XA_HEAD_DIM = 256
N_BRANCH = 3
EPS = 1e-6
SUBLANES = 8

SLAB_QK, SLAB_V, SLAB_GLA_G, SLAB_U, SLAB_POOL_G, SLAB_XQ, SLAB_XG, SLAB_MERGE = 0, 1, 2, 3, 4, 5, 6, 7
N_SLABS = 10
N_MIX_SLABS = 7
N_HEAD_SLABS = 3
ALR_START = N_HEAD_SLABS * D_MODEL

BF = jnp.bfloat16
F32 = jnp.float32
MIB = 1 << 20


def _dot(a, b):
    return jnp.dot(a, b, preferred_element_type=F32)


def _dot_nt(a, b):
    return lax.dot_general(a, b, (((1,), (1,)), ((), ())), preferred_element_type=F32)


def _dot_tn(a, b):
    return lax.dot_general(a, b, (((0,), (0,)), ((), ())), preferred_element_type=F32)


def _silu(x):
    return x * jax.nn.sigmoid(x)


def _rms(x, gain):
    ms = jnp.mean(x * x, axis=-1, keepdims=True)
    return x * lax.rsqrt(ms + EPS) * gain


def _const_spec(shape):
    zeros = (0,) * len(shape)
    return pl.BlockSpec(shape, lambda *_: zeros, pipeline_mode=pl.Buffered(1))


def _params(sem, vmem_mib):
    return pltpu.CompilerParams(dimension_semantics=sem, vmem_limit_bytes=vmem_mib * MIB)


def _drop_alias_refs(body, n_in, n_alias):
    def kern(*refs):
        return body(*refs[:n_in], *refs[n_in + n_alias:])
    return kern


def _kvproj_kernel(m_ref, wk_ref, wv_ref, k_ref, v_ref, kb_ref, vb_ref):
    m = m_ref[...].astype(BF)
    k = _dot(m, wk_ref[0])
    v = _dot(m, wv_ref[0])
    for h in range(XA_HEADS):
        cs = slice(h * XA_HEAD_DIM, (h + 1) * XA_HEAD_DIM)
        k_ref[0, 0, :, h, :] = k[:, cs]
        v_ref[0, 0, :, h, :] = v[:, cs]
    kb_ref[0] = k.astype(BF)
    vb_ref[0] = v.astype(BF)


def _kvproj(mem, wk, wv, batch, n_mem):
    depth = wk.shape[0]
    w_spec = pl.BlockSpec((1, D_MODEL, D_MODEL), lambda l, b: (l, 0, 0))
    out5 = pl.BlockSpec((1, 1, n_mem, XA_HEADS, XA_HEAD_DIM), lambda l, b: (l, b, 0, 0, 0))
    out_bf = pl.BlockSpec((1, n_mem, D_MODEL), lambda l, b: (l, b, 0))
    return pl.pallas_call(
        _kvproj_kernel,
        grid=(depth, batch),
        in_specs=[pl.BlockSpec((n_mem, D_MODEL), lambda l, b: (b, 0)), w_spec, w_spec],
        out_specs=[out5, out5, out_bf, out_bf],
        out_shape=[jax.ShapeDtypeStruct((depth, batch, n_mem, XA_HEADS, XA_HEAD_DIM), F32)] * 2
        + [jax.ShapeDtypeStruct((depth, batch * n_mem, D_MODEL), BF)] * 2,
        compiler_params=_params(("parallel", "parallel"), 32),
        name="kvproj",
    )(mem, wk, wv)


def _inproj_slab(h, wa_ref, wb_ref, p_out, j):
    if j < N_HEAD_SLABS:
        w = wa_ref[:, j * D_MODEL:(j + 1) * D_MODEL]
    else:
        w = wb_ref[:, (j - N_HEAD_SLABS) * D_MODEL:(j - N_HEAD_SLABS + 1) * D_MODEL]
    p_out[j] = _dot(h, w).astype(p_out.dtype)


def _inproj_gate_lowrank(h, walr_ref):
    return _dot(h, walr_ref[...]).astype(BF)


def _inproj_gate(alr, wa2_ref, ba_ref, la_out):
    z = _dot(alr, wa2_ref[...]) + ba_ref[...]
    la_out[...] = (jnp.minimum(z, 0.0) - jnp.log1p(jnp.exp(-jnp.abs(z)))) * (1.0 / GATE_TAU)


def _merge_term(br_ref, p, wbr_ref, n):
    return jax.nn.sigmoid(p[SLAB_MERGE + n].astype(F32)) * _dot(br_ref[n].astype(BF), wbr_ref[n])


def _merge_finish(merged, x, wo_ref, fg_ref, final):
    x_new = x + _dot(merged.astype(BF), wo_ref[...])
    return _rms(x_new, fg_ref[...]) if final else x_new


def _gla_cumdecay(la_ref, tm):
    row = lax.broadcasted_iota(jnp.int32, (tm, tm), 0)
    col = lax.broadcasted_iota(jnp.int32, (tm, tm), 1)
    same_chunk = (row // CHUNK) == (col // CHUNK)
    cum_mat = jnp.where(same_chunk & (row >= col), 1.0, 0.0).astype(BF)
    la = la_ref[...]
    la_hi = la.astype(BF)
    la_lo = (la - la_hi.astype(F32)).astype(BF)
    return _dot(cum_mat, la_hi) + _dot(cum_mat, la_lo)


def _gla_chunk(p, bcum, gg_ref, br_ref, st_ref, c, between=None):
    crow = lax.broadcasted_iota(jnp.int32, (CHUNK, CHUNK), 0)
    ccol = lax.broadcasted_iota(jnp.int32, (CHUNK, CHUNK), 1)
    causal = crow >= ccol
    rows = slice(c * CHUNK, (c + 1) * CHUNK)
    heads = range(GLA_HEADS)
    q_dec, k_end, decay, att = [], [], [], []
    for h in heads:
        kc = slice(h * GLA_DK, (h + 1) * GLA_DK)
        kc2 = slice(GLA_KEY_WIDTH + h * GLA_DK, GLA_KEY_WIDTH + (h + 1) * GLA_DK)
        b = bcum[rows, kc]
        b_last = b[CHUNK - 1:CHUNK, :]
        q = p[SLAB_QK, rows, kc].astype(F32)
        k = p[SLAB_QK, rows, kc2].astype(F32)
        q_dec.append((q * (GLA_DK ** -0.5) * jnp.exp(b)).astype(BF))
        k_inv = (k * jnp.exp(-b)).astype(BF)
        k_end.append((k * jnp.exp(b_last - b)).astype(BF))
        decay.append(jnp.exp(b_last))
        att.append(_dot_nt(q_dec[h], k_inv))
    if between is not None:
        between()
    o = []
    for h in heads:
        vc = slice(h * GLA_DV, (h + 1) * GLA_DV)
        a = jnp.where(causal, att[h], 0.0).astype(BF)
        o.append(_dot(a, p[SLAB_V, rows, vc]) + _dot_nt(q_dec[h], st_ref[h].astype(BF)))
    for h in heads:
        vc = slice(h * GLA_DV, (h + 1) * GLA_DV)
        st_ref[h] = decay[h] * st_ref[h] + _dot_tn(p[SLAB_V, rows, vc], k_end[h])
    for h in heads:
        vc = slice(h * GLA_DV, (h + 1) * GLA_DV)
        g = p[SLAB_GLA_G, rows, vc].astype(F32)
        br_ref[0, rows, vc] = (_rms(o[h], gg_ref[:, vc]) * _silu(g)).astype(BF)


def _pool_branch(p, pw_ref, ps_ref, br_ref, ubuf_ref, t, tm):
    u = p[SLAB_U].astype(F32)
    ubuf_ref[HIST_PAD:HIST_PAD + tm, :] = u
    pos = t * tm + lax.broadcasted_iota(jnp.int32, (tm, 1), 0)
    for g, w in enumerate(POOL_WINDOWS):
        cs = slice(g * POOL_GROUP_DIM, (g + 1) * POOL_GROUP_DIM)
        ug = u[:, cs]
        s = ug
        for j in range(1, w):
            s = s + ubuf_ref[HIST_PAD - j:HIST_PAD - j + tm, cs]
        cnt = jnp.minimum(w, pos + 1).astype(F32)
        diff = s / cnt - ug
        mixed = _dot(diff.astype(BF), pw_ref[g]) * ps_ref[:, cs]
        pg = p[SLAB_POOL_G, :, cs].astype(F32)
        br_ref[1, :, cs] = (mixed * _silu(pg)).astype(BF)
    ubuf_ref[0:HIST_PAD, :] = ubuf_ref[tm:tm + HIST_PAD, :]


def _xattn_probs(p, mk_ref):
    out = []
    for h in range(XA_HEADS):
        cs = slice(h * XA_HEAD_DIM, (h + 1) * XA_HEAD_DIM)
        s = _dot_nt(p[SLAB_XQ, :, cs], mk_ref[0, :, cs]) * (XA_HEAD_DIM ** -0.5)
        pr = jnp.exp(s - jnp.max(s, axis=-1, keepdims=True))
        out.append((pr.astype(BF), jnp.sum(pr, axis=-1, keepdims=True)))
    return out


def _xattn_branch(p, probs, mv_ref, br_ref):
    for h in range(XA_HEADS):
        cs = slice(h * XA_HEAD_DIM, (h + 1) * XA_HEAD_DIM)
        pr, denom = probs[h]
        o = _dot(pr, mv_ref[0, :, cs]) / denom
        xg = p[SLAB_XG, :, cs].astype(F32)
        br_ref[2, :, cs] = (o * _silu(xg)).astype(BF)


def _prompt_layer_kernel(xn_ref, xc_ref, g_ref, wa_ref, wb_ref, walr_ref, wa2_ref, ba_ref,
                         mk_ref, mv_ref, pw_ref, ps_ref, gg_ref, wbr_ref, wo_ref, fg_ref,
                         out_ref, s_out_ref, hist_out_ref,
                         h_scr, p_scr, la_scr, br_scr, st_ref, ubuf_ref, *, tm, nt, final):
    s = pl.program_id(0)
    t = jnp.maximum(s - 1, 0) % nt
    slot_w = s % 2
    slot_r = 1 - slot_w

    @pl.when(s == 0)
    def _():
        p_scr[1] = jnp.zeros(p_scr.shape[1:], p_scr.dtype)
        la_scr[1] = jnp.zeros(la_scr.shape[1:], la_scr.dtype)

    @pl.when(t == 0)
    def _():
        st_ref[...] = jnp.zeros_like(st_ref)
        ubuf_ref[0:HIST_PAD, :] = jnp.zeros((HIST_PAD, D_MODEL), F32)

    @pl.when(s == 0)
    def _():
        h_scr[0] = _rms(xc_ref[...], g_ref[...]).astype(BF)

    p_next = p_scr.at[slot_w]
    p = p_scr.at[slot_r]

    def proj(j):
        _inproj_slab(h_scr[slot_w], wa_ref, wb_ref, p_next, j)

    n_chunks = tm // CHUNK
    slabs = list(range(N_SLABS))
    alr = _inproj_gate_lowrank(h_scr[slot_w], walr_ref)
    bcum = _gla_cumdecay(la_scr.at[slot_r], tm)
    proj(slabs.pop(0))
    _inproj_gate(alr, wa2_ref, ba_ref, la_scr.at[slot_w])
    probs = None
    for c in range(n_chunks):
        mid = slabs.pop(0) if slabs else None
        _gla_chunk(p, bcum, gg_ref, br_scr, st_ref, c,
                   between=None if mid is None else functools.partial(proj, mid))
        if slabs:
            proj(slabs.pop(0))
        if c == 0:
            probs = _xattn_probs(p, mk_ref)
        elif c == 1:
            _xattn_branch(p, probs, mv_ref, br_scr)
        elif c == 2:
            _pool_branch(p, pw_ref, ps_ref, br_scr, ubuf_ref, t, tm)
    h_scr[slot_r] = _rms(xn_ref[...], g_ref[...]).astype(BF)
    merged = _merge_term(br_scr, p, wbr_ref, 2)
    merged = merged + _merge_term(br_scr, p, wbr_ref, 1)
    merged = merged + _merge_term(br_scr, p, wbr_ref, 0)
    for j in slabs:
        proj(j)
    out_ref[...] = _merge_finish(merged, xc_ref[...], wo_ref, fg_ref, final)

    @pl.when((s > 0) & (t == nt - 1))
    def _():
        for h in range(GLA_HEADS):
            s_out_ref[0, 0, h] = st_ref[h].T
        hist_out_ref[0, 0] = ubuf_ref[1:HIST_PAD, :]


def _prompt_layer(x, ngain, w_head, w_tail, w_alr, wa2, ba, mk, mv, pool_w, pool_scale, gla_gain,
                  w_branch, w_out, fgain, carry, layer, depth, batch, seq, n_mem, tm, final):
    nt = seq // tm
    n_tiles = batch * nt
    assert tm // CHUNK >= 3, "the mixer stages are spread over the first three GLA chunks of a tile"
    n_alias = 0 if carry is None else len(carry)
    n_in = 16
    kern = _drop_alias_refs(functools.partial(_prompt_layer_kernel, tm=tm, nt=nt, final=final), n_in, n_alias)
    any_spec = pl.BlockSpec(memory_space=pl.ANY)

    def cur(s):
        return jnp.maximum(s - 1, 0)

    def seq_of(s):
        return cur(s) // nt

    return pl.pallas_call(
        kern,
        grid=(n_tiles + 1,),
        in_specs=[
            pl.BlockSpec((tm, D_MODEL), lambda s: (jnp.minimum(s + 1, n_tiles - 1), 0)),
            pl.BlockSpec((tm, D_MODEL), lambda s: (cur(s), 0)),
            _const_spec((1, D_MODEL)),
            _const_spec((D_MODEL, N_HEAD_SLABS * D_MODEL)),
            _const_spec((D_MODEL, (N_SLABS - N_HEAD_SLABS) * D_MODEL)),
            _const_spec((D_MODEL, GATE_RANK)),
            _const_spec((GATE_RANK, GLA_KEY_WIDTH)),
            _const_spec((1, GLA_KEY_WIDTH)),
            pl.BlockSpec((1, n_mem, D_MODEL), lambda s: (layer, seq_of(s), 0)),
            pl.BlockSpec((1, n_mem, D_MODEL), lambda s: (layer, seq_of(s), 0)),
            _const_spec((len(POOL_WINDOWS), POOL_GROUP_DIM, POOL_GROUP_DIM)),
            _const_spec((1, D_MODEL)),
            _const_spec((1, D_MODEL)),
            _const_spec((N_BRANCH, D_MODEL, D_MODEL)),
            _const_spec((D_MODEL, D_MODEL)),
            _const_spec((1, D_MODEL)),
        ] + [any_spec] * n_alias,
        out_specs=[
            pl.BlockSpec((tm, D_MODEL), lambda s: (cur(s), 0)),
            pl.BlockSpec((1, 1, GLA_HEADS, GLA_DK, GLA_DV), lambda s: (layer, seq_of(s), 0, 0, 0)),
            pl.BlockSpec((1, 1, POOL_HIST, D_MODEL), lambda s: (layer, seq_of(s), 0, 0)),
        ],
        out_shape=[
            jax.ShapeDtypeStruct((n_tiles * tm, D_MODEL), F32),
            jax.ShapeDtypeStruct((depth, batch, GLA_HEADS, GLA_DK, GLA_DV), F32),
            jax.ShapeDtypeStruct((depth, batch, POOL_HIST, D_MODEL), F32),
        ],
        scratch_shapes=[
            pltpu.VMEM((2, tm, D_MODEL), BF),
            pltpu.VMEM((2, N_SLABS, tm, D_MODEL), BF),
            pltpu.VMEM((2, tm, GLA_KEY_WIDTH), F32),
            pltpu.VMEM((N_BRANCH, tm, D_MODEL), BF),
            pltpu.VMEM((GLA_HEADS, GLA_DV, GLA_DK), F32),
            pltpu.VMEM((HIST_PAD + tm, D_MODEL), F32),
        ],
        input_output_aliases={n_in + a: 1 + a for a in range(n_alias)},
        compiler_params=_params(("arbitrary",), 60),
        name="prompt_layer",
    )(x, x, ngain, w_head, w_tail, w_alr, wa2, ba, mk, mv, pool_w, pool_scale, gla_gain,
      w_branch, w_out, fgain, *(carry or ()))


def _inproj_kernel(x_ref, g_ref, wa_ref, wb_ref, walr_ref, wa2_ref, ba_ref, p_ref, la_ref):
    h = _rms(x_ref[...], g_ref[...]).astype(BF)
    alr = _inproj_gate_lowrank(h, walr_ref)
    for j in range(N_SLABS):
        _inproj_slab(h, wa_ref, wb_ref, p_ref, j)
    _inproj_gate(alr, wa2_ref, ba_ref, la_ref)


def _inproj(x, gain, w_head, w_tail, w_alr, w_a2, b_a, tm, out_dtype):
    m_rows = x.shape[0]
    return pl.pallas_call(
        _inproj_kernel,
        grid=(m_rows // tm,),
        in_specs=[
            pl.BlockSpec((tm, D_MODEL), lambda i: (i, 0)),
            _const_spec((1, D_MODEL)),
            _const_spec((D_MODEL, N_HEAD_SLABS * D_MODEL)),
            _const_spec((D_MODEL, (N_SLABS - N_HEAD_SLABS) * D_MODEL)),
            _const_spec((D_MODEL, GATE_RANK)),
            _const_spec((GATE_RANK, GLA_KEY_WIDTH)),
            _const_spec((1, GLA_KEY_WIDTH)),
        ],
        out_specs=[
            pl.BlockSpec((N_SLABS, tm, D_MODEL), lambda i: (0, i, 0)),
            pl.BlockSpec((tm, GLA_KEY_WIDTH), lambda i: (i, 0)),
        ],
        out_shape=[
            jax.ShapeDtypeStruct((N_SLABS, m_rows, D_MODEL), out_dtype),
            jax.ShapeDtypeStruct((m_rows, GLA_KEY_WIDTH), F32),
        ],
        compiler_params=_params(("parallel",), 48),
        name="inproj",
    )(x, gain, w_head, w_tail, w_alr, w_a2, b_a)


def _sample_mix_kernel(p_ref, la_ref, s0_ref, hist_ref, ck_ref, cv_ref, pw_ref, ps_ref, gg_ref,
                       br_ref, s_out_ref, hist_out_ref, diff_ref, *, sb):
    r0 = pl.program_id(1) * sb
    erow = lax.broadcasted_iota(jnp.int32, (GLA_DK, GLA_DK), 0)
    ecol = lax.broadcasted_iota(jnp.int32, (GLA_DK, GLA_DK), 1)
    eye = erow == ecol

    def to_col(x):
        return jnp.sum(jnp.where(eye, jnp.broadcast_to(x, (GLA_DK, GLA_DK)), 0.0), axis=1, keepdims=True)

    for i in range(sb):
        r = pl.ds(r0 + i, 1)
        la = la_ref[r, :]
        qk = p_ref[SLAB_QK, r, :]
        vv = p_ref[SLAB_V, r, :]
        gla_g = p_ref[SLAB_GLA_G, r, :]
        for h in range(GLA_HEADS):
            kc = slice(h * GLA_DK, (h + 1) * GLA_DK)
            kc2 = slice(GLA_KEY_WIDTH + h * GLA_DK, GLA_KEY_WIDTH + (h + 1) * GLA_DK)
            vc = slice(h * GLA_DV, (h + 1) * GLA_DV)
            a_col = to_col(jnp.exp(la[:, kc]))
            q_col = to_col(qk[:, kc] * (GLA_DK ** -0.5))
            k_col = to_col(qk[:, kc2])
            s_new = a_col * s0_ref[0, i, h] + k_col * vv[:, vc]
            s_out_ref[0, i, h] = s_new
            o = jnp.sum(q_col * s_new, axis=0, keepdims=True)
            br_ref[0, r, vc] = _rms(o, gg_ref[:, vc]) * _silu(gla_g[:, vc])

        u = p_ref[SLAB_U, r, :]
        for g, w in enumerate(POOL_WINDOWS):
            cs = slice(g * POOL_GROUP_DIM, (g + 1) * POOL_GROUP_DIM)
            past = jnp.sum(hist_ref[0, i, POOL_HIST - (w - 1):POOL_HIST, cs], axis=0, keepdims=True)
            diff_ref[r, cs] = (u[:, cs] + past) / float(w) - u[:, cs]
        hist_out_ref[0, i, 0:POOL_HIST - 1, :] = hist_ref[0, i, 1:POOL_HIST, :]
        hist_out_ref[0, i, POOL_HIST - 1:POOL_HIST, :] = u

        xq = p_ref[SLAB_XQ, r, :]
        xg = p_ref[SLAB_XG, r, :]
        for h in range(XA_HEADS):
            cs = slice(h * XA_HEAD_DIM, (h + 1) * XA_HEAD_DIM)
            s = jnp.sum(ck_ref[0, i, :, h, :] * xq[:, cs], axis=1, keepdims=True) * (XA_HEAD_DIM ** -0.5)
            p = jnp.exp(s - jnp.max(s, axis=0, keepdims=True))
            denom = jnp.sum(p, axis=0, keepdims=True)
            o = jnp.sum(p * cv_ref[0, i, :, h, :], axis=0, keepdims=True) / denom
            br_ref[2, r, cs] = o * _silu(xg[:, cs])

    @pl.when(pl.program_id(1) == pl.num_programs(1) - 1)
    def _():
        for g in range(len(POOL_WINDOWS)):
            cs = slice(g * POOL_GROUP_DIM, (g + 1) * POOL_GROUP_DIM)
            mixed = _dot(diff_ref[:, cs].astype(BF), pw_ref[g]) * ps_ref[:, cs]
            br_ref[1, :, cs] = mixed * _silu(p_ref[SLAB_POOL_G, :, cs])


def _sample_mix(p, la, s0, hist, ck, cv, pool_w, pool_scale, gla_gain, carry, layer, sb=4):
    nb = s0.shape[1]
    n_mem = ck.shape[2]
    rb = SUBLANES
    halves = rb // sb
    n_alias = 0 if carry is None else len(carry)
    n_in = 9
    kern = _drop_alias_refs(functools.partial(_sample_mix_kernel, sb=sb), n_in, n_alias)
    any_spec = pl.BlockSpec(memory_space=pl.ANY)
    return pl.pallas_call(
        kern,
        grid=(nb // rb, halves),
        in_specs=[
            pl.BlockSpec((N_MIX_SLABS, rb, D_MODEL), lambda i, j: (0, i, 0)),
            pl.BlockSpec((rb, GLA_KEY_WIDTH), lambda i, j: (i, 0)),
            pl.BlockSpec((1, sb, GLA_HEADS, GLA_DK, GLA_DV), lambda i, j: (layer, i * halves + j, 0, 0, 0)),
            pl.BlockSpec((1, sb, POOL_HIST, D_MODEL), lambda i, j: (layer, i * halves + j, 0, 0)),
            pl.BlockSpec((1, sb, n_mem, XA_HEADS, XA_HEAD_DIM), lambda i, j: (layer, i * halves + j, 0, 0, 0)),
            pl.BlockSpec((1, sb, n_mem, XA_HEADS, XA_HEAD_DIM), lambda i, j: (layer, i * halves + j, 0, 0, 0)),
            _const_spec((len(POOL_WINDOWS), POOL_GROUP_DIM, POOL_GROUP_DIM)),
            _const_spec((1, D_MODEL)),
            _const_spec((1, D_MODEL)),
        ] + [any_spec] * n_alias,
        out_specs=[
            pl.BlockSpec((N_BRANCH, rb, D_MODEL), lambda i, j: (0, i, 0)),
            pl.BlockSpec((1, sb, GLA_HEADS, GLA_DK, GLA_DV), lambda i, j: (layer, i * halves + j, 0, 0, 0)),
            pl.BlockSpec((1, sb, POOL_HIST, D_MODEL), lambda i, j: (layer, i * halves + j, 0, 0)),
        ],
        out_shape=[
            jax.ShapeDtypeStruct((N_BRANCH, nb, D_MODEL), F32),
            jax.ShapeDtypeStruct(s0.shape, F32),
            jax.ShapeDtypeStruct(hist.shape, F32),
        ],
        scratch_shapes=[pltpu.VMEM((rb, D_MODEL), F32)],
        input_output_aliases={n_in + a: 1 + a for a in range(n_alias)},
        compiler_params=_params(("parallel", "arbitrary"), 40),
        name="sample_mix",
    )(p, la, s0, hist, ck, cv, pool_w, pool_scale, gla_gain, *(carry or ()))


def _merge_kernel(br_ref, p_ref, x_ref, wbr_ref, wo_ref, fg_ref, out_ref, *, final):
    merged = _merge_term(br_ref, p_ref, wbr_ref, 0)
    for n in range(1, N_BRANCH):
        merged = merged + _merge_term(br_ref, p_ref, wbr_ref, n)
    out_ref[...] = _merge_finish(merged, x_ref[...], wo_ref, fg_ref, final)


def _merge_out(br, p, x, w_branch, w_out, final_gain, final):
    m_rows = x.shape[0]
    whole = lambda shape: pl.BlockSpec(shape, lambda i: (0,) * len(shape))
    return pl.pallas_call(
        functools.partial(_merge_kernel, final=final),
        grid=(1,),
        in_specs=[
            whole((N_BRANCH, m_rows, D_MODEL)),
            whole((N_SLABS, m_rows, D_MODEL)),
            whole((m_rows, D_MODEL)),
            _const_spec((N_BRANCH, D_MODEL, D_MODEL)),
            _const_spec((D_MODEL, D_MODEL)),
            _const_spec((1, D_MODEL)),
        ],
        out_specs=whole((m_rows, D_MODEL)),
        out_shape=jax.ShapeDtypeStruct((m_rows, D_MODEL), F32),
        compiler_params=_params(("arbitrary",), 40),
        name="merge_out",
    )(br, p, x, w_branch, w_out, final_gain)


def kernel(x_prompt, x_sample, mem_prompt, cache_mem_k, cache_mem_v, state_gla, state_pool, w_in, w_a2, b_a, gla_gain, pool_w, pool_scale, w_mk, w_mv, w_branch, w_out, norm_gain, final_gain):
    batch, seq, _ = x_prompt.shape
    nb = x_sample.shape[0]
    n_mem = mem_prompt.shape[1]
    depth = w_in.shape[0]
    tm = min(256, seq)

    xp = x_prompt.reshape(batch * seq, D_MODEL)
    xs = x_sample.reshape(nb, D_MODEL)
    mem = mem_prompt.reshape(batch * n_mem, D_MODEL)
    fgain = final_gain.reshape(1, D_MODEL)

    mk, mv, mk_bf, mv_bf = _kvproj(mem, w_mk.astype(BF), w_mv.astype(BF), batch, n_mem)

    carry_p, carry_s = None, None
    for l in range(depth):
        final = l == depth - 1
        w_head = w_in[l, :, :ALR_START].astype(BF)
        w_tail = w_in[l, :, ALR_START + GATE_RANK:].astype(BF)
        w_alr = w_in[l, :, ALR_START:ALR_START + GATE_RANK].astype(BF)
        wa2 = w_a2[l].astype(BF)
        ba = b_a[l].reshape(1, GLA_KEY_WIDTH)
        ngain = norm_gain[l].reshape(1, D_MODEL)
        ggain = gla_gain[l].reshape(1, D_MODEL)
        pscale = pool_scale[l].reshape(1, D_MODEL)
        pw = pool_w[l].astype(BF)
        wb = w_branch[l].astype(BF)
        wo = w_out[l].astype(BF)

        xp, s_all, hist_all = _prompt_layer(xp, ngain, w_head, w_tail, w_alr, wa2, ba, mk_bf, mv_bf, pw, pscale,
                                            ggain, wb, wo, fgain, carry_p, l, depth, batch, seq, n_mem, tm, final)
        carry_p = (s_all, hist_all)

        ps, las = _inproj(xs, ngain, w_head, w_tail, w_alr, wa2, ba, nb, F32)
        brs, s_new, hist_new = _sample_mix(ps, las, state_gla, state_pool, cache_mem_k, cache_mem_v,
                                           pw, pscale, ggain, carry_s, l)
        carry_s = (s_new, hist_new)
        xs = _merge_out(brs, ps, xs, wb, wo, fgain, final)

    return (xp.reshape(batch, seq, D_MODEL), xs.reshape(nb, 1, D_MODEL),
            mk, mv, carry_p[0], carry_p[1], carry_s[0], carry_s[1])
```

```python
import functools

import jax
import jax.numpy as jnp
from jax import lax
from jax.experimental import pallas as pl
from jax.experimental.pallas import tpu as pltpu

D_MODEL = 1024
GLA_HEADS = 4
GLA_DK = 128
GLA_DV = 256
GLA_KEY_WIDTH = GLA_HEADS * GLA_DK
GATE_RANK = 16
GATE_TAU = 16.0
CHUNK = 64
POOL_WINDOWS = (2, 4, 8, 16)
POOL_GROUP_DIM = 256
POOL_HIST = 15
HIST_PAD = 16
XA_HEADS = 4
XA_HEAD_DIM = 256
N_BRANCH = 3
EPS = 1e-6
SUBLANES = 8
LANES = 128

SLAB_QK, SLAB_V, SLAB_GLA_G, SLAB_U, SLAB_POOL_G, SLAB_XQ, SLAB_XG, SLAB_MERGE = 0, 1, 2, 3, 4, 5, 6, 7
N_SLABS = 10
N_MIX_SLABS = 7
N_HEAD_SLABS = 3
ALR_START = N_HEAD_SLABS * D_MODEL
PROJ_PARTS = 2

BF = jnp.bfloat16
F32 = jnp.float32
MIB = 1 << 20


def _dot(a, b):
    return jnp.dot(a, b, preferred_element_type=F32)


def _dot_nt(a, b):
    return lax.dot_general(a, b, (((1,), (1,)), ((), ())), preferred_element_type=F32)


def _dot_tn(a, b):
    return lax.dot_general(a, b, (((0,), (0,)), ((), ())), preferred_element_type=F32)


def _pack_rows(w):
    *lead, k, n = w.shape
    pairs = jnp.swapaxes(w.astype(BF).reshape(*lead, k // 2, 2, n), -1, -2)
    return lax.bitcast_convert_type(pairs, jnp.uint32)


def _unpack_rows(w_words):
    return pltpu.bitcast(w_words, BF)


def _silu(x):
    return x * jax.nn.sigmoid(x)


def _rms(x, gain):
    ms = jnp.mean(x * x, axis=-1, keepdims=True)
    return x * lax.rsqrt(ms + EPS) * gain


def _const_spec(shape):
    zeros = (0,) * len(shape)
    return pl.BlockSpec(shape, lambda *_: zeros, pipeline_mode=pl.Buffered(1))


def _params(sem, vmem_mib):
    return pltpu.CompilerParams(dimension_semantics=sem, vmem_limit_bytes=vmem_mib * MIB)


def _drop_alias_refs(body, n_in, n_alias):
    def kern(*refs):
        return body(*refs[:n_in], *refs[n_in + n_alias:])
    return kern


def _kvproj_kernel(m_ref, wk_ref, wv_ref, k_ref, v_ref, kb_ref, vb_ref):
    m = m_ref[...].astype(BF)
    k = _dot(m, wk_ref[0])
    v = _dot(m, wv_ref[0])
    for h in range(XA_HEADS):
        cs = slice(h * XA_HEAD_DIM, (h + 1) * XA_HEAD_DIM)
        k_ref[0, 0, :, h, :] = k[:, cs]
        v_ref[0, 0, :, h, :] = v[:, cs]
    kb_ref[0] = k.astype(BF)
    vb_ref[0] = v.astype(BF)


def _kvproj(mem, wk, wv, batch, n_mem):
    depth = wk.shape[0]
    w_spec = pl.BlockSpec((1, D_MODEL, D_MODEL), lambda l, b: (l, 0, 0))
    out5 = pl.BlockSpec((1, 1, n_mem, XA_HEADS, XA_HEAD_DIM), lambda l, b: (l, b, 0, 0, 0))
    out_bf = pl.BlockSpec((1, n_mem, D_MODEL), lambda l, b: (l, b, 0))
    return pl.pallas_call(
        _kvproj_kernel,
        grid=(depth, batch),
        in_specs=[pl.BlockSpec((n_mem, D_MODEL), lambda l, b: (b, 0)), w_spec, w_spec],
        out_specs=[out5, out5, out_bf, out_bf],
        out_shape=[jax.ShapeDtypeStruct((depth, batch, n_mem, XA_HEADS, XA_HEAD_DIM), F32)] * 2
        + [jax.ShapeDtypeStruct((depth, batch * n_mem, D_MODEL), BF)] * 2,
        compiler_params=_params(("parallel", "parallel"), 32),
        name="kvproj",
    )(mem, wk, wv)


def _inproj_slab(h, wa_ref, wb_ref, p_out, j, part=0, n_parts=1):
    width = D_MODEL // n_parts
    lo = part * width
    if j < N_HEAD_SLABS:
        w = wa_ref[:, j * D_MODEL + lo:j * D_MODEL + lo + width]
    else:
        w = wb_ref[:, (j - N_HEAD_SLABS) * D_MODEL + lo:(j - N_HEAD_SLABS) * D_MODEL + lo + width]
    p_out[j, :, lo:lo + width] = _dot(h, _unpack_rows(w)).astype(p_out.dtype)


def _inproj_gate_lowrank(h, walr_ref):
    return _dot(h, walr_ref[...]).astype(BF)


def _inproj_gate(alr, wa2_ref, ba_ref, la_out):
    z = _dot(alr, wa2_ref[...]) + ba_ref[...]
    la_out[...] = (jnp.minimum(z, 0.0) - jnp.log(1.0 + jnp.exp(-jnp.abs(z)))) * (1.0 / GATE_TAU)


def _merge_term(br_ref, p, wbr_ref, n):
    return jax.nn.sigmoid(p[SLAB_MERGE + n].astype(F32)) * _dot(br_ref[n].astype(BF), _unpack_rows(wbr_ref[n]))


def _merge_finish(merged, x, wo_ref, fg_ref, final):
    x_new = x + _dot(merged.astype(BF), _unpack_rows(wo_ref[...]))
    return _rms(x_new, fg_ref[...]) if final else x_new


def _chunk_cumsum_matrix(tm):
    row = lax.broadcasted_iota(jnp.int32, (tm, tm), 0)
    col = lax.broadcasted_iota(jnp.int32, (tm, tm), 1)
    return (((row // CHUNK) == (col // CHUNK)) & (row >= col)).astype(BF)


def _window_matrices(tm):
    row = lax.broadcasted_iota(jnp.int32, (tm, tm), 0)
    col = lax.broadcasted_iota(jnp.int32, (tm, tm), 1)
    return jnp.stack([((row >= col) & (row - col < w)).astype(BF) for w in POOL_WINDOWS])


def _gla_cumdecay(cum_ref, la_ref):
    la = la_ref[...]
    la_hi = la.astype(BF)
    la_lo = (la - la_hi.astype(F32)).astype(BF)
    return _dot(cum_ref[...], la_hi) + _dot(cum_ref[...], la_lo)


def _gla_chunk(p, bcum, gg_ref, br_ref, st_ref, c, filler):
    crow = lax.broadcasted_iota(jnp.int32, (CHUNK, CHUNK), 0)
    ccol = lax.broadcasted_iota(jnp.int32, (CHUNK, CHUNK), 1)
    causal = crow >= ccol
    rows = slice(c * CHUNK, (c + 1) * CHUNK)
    heads = range(GLA_HEADS)
    q_dec, k_end, decay, att = [], [], [], []
    for h in heads:
        kc = slice(h * GLA_DK, (h + 1) * GLA_DK)
        kc2 = slice(GLA_KEY_WIDTH + h * GLA_DK, GLA_KEY_WIDTH + (h + 1) * GLA_DK)
        b = bcum[rows, kc]
        b_last = b[CHUNK - 1:CHUNK, :]
        q = p[SLAB_QK, rows, kc].astype(F32)
        k = p[SLAB_QK, rows, kc2].astype(F32)
        q_dec.append((q * (GLA_DK ** -0.5) * jnp.exp(b)).astype(BF))
        k_inv = (k * jnp.exp(-b)).astype(BF)
        k_end.append((k * jnp.exp(b_last - b)).astype(BF))
        decay.append(jnp.exp(b_last))
        att.append(_dot_nt(q_dec[h], k_inv))
    filler()
    o = []
    for h in heads:
        vc = slice(h * GLA_DV, (h + 1) * GLA_DV)
        a = jnp.where(causal, att[h], 0.0).astype(BF)
        o.append(_dot(a, p[SLAB_V, rows, vc]) + _dot_nt(q_dec[h], st_ref[h].astype(BF)))
    filler()
    for h in heads:
        vc = slice(h * GLA_DV, (h + 1) * GLA_DV)
        st_ref[h] = decay[h] * st_ref[h] + _dot_tn(p[SLAB_V, rows, vc], k_end[h])
    for h in heads:
        vc = slice(h * GLA_DV, (h + 1) * GLA_DV)
        g = p[SLAB_GLA_G, rows, vc].astype(F32)
        br_ref[0, rows, vc] = (_rms(o[h], gg_ref[:, vc]) * _silu(g)).astype(BF)


def _pool_branch(p, win_ref, pw_ref, ps_ref, br_ref, ubuf_ref, t, tm):
    u_bf = p[SLAB_U]
    u = u_bf.astype(F32)
    ubuf_ref[HIST_PAD:2 * HIST_PAD, :] = u[0:HIST_PAD]
    pos = t * tm + lax.broadcasted_iota(jnp.int32, (tm, 1), 0)
    for g, w in enumerate(POOL_WINDOWS):
        cs = slice(g * POOL_GROUP_DIM, (g + 1) * POOL_GROUP_DIM)
        ug = u[:, cs]
        head = ug[0:HIST_PAD]
        for j in range(1, w):
            head = head + ubuf_ref[HIST_PAD - j:2 * HIST_PAD - j, cs]
        s = jnp.concatenate([head, _dot(win_ref[g], u_bf[:, cs])[HIST_PAD:]], axis=0)
        cnt = jnp.minimum(w, pos + 1).astype(F32)
        diff = s / cnt - ug
        mixed = _dot(diff.astype(BF), pw_ref[g]) * ps_ref[:, cs]
        pg = p[SLAB_POOL_G, :, cs].astype(F32)
        br_ref[1, :, cs] = (mixed * _silu(pg)).astype(BF)
    ubuf_ref[0:HIST_PAD, :] = u[tm - HIST_PAD:tm]


def _xattn_probs(p, mk_ref):
    out = []
    for h in range(XA_HEADS):
        cs = slice(h * XA_HEAD_DIM, (h + 1) * XA_HEAD_DIM)
        s = _dot_nt(p[SLAB_XQ, :, cs], mk_ref[0, :, cs]) * (XA_HEAD_DIM ** -0.5)
        pr = jnp.exp(s - jnp.max(s, axis=-1, keepdims=True))
        out.append((pr.astype(BF), jnp.sum(pr, axis=-1, keepdims=True)))
    return out


def _xattn_branch(p, probs, mv_ref, br_ref):
    for h in range(XA_HEADS):
        cs = slice(h * XA_HEAD_DIM, (h + 1) * XA_HEAD_DIM)
        pr, denom = probs[h]
        o = _dot(pr, mv_ref[0, :, cs]) / denom
        xg = p[SLAB_XG, :, cs].astype(F32)
        br_ref[2, :, cs] = (o * _silu(xg)).astype(BF)


def _prompt_layer_kernel(xn_ref, xc_ref, g_ref, wa_ref, wb_ref, walr_ref, wa2_ref, ba_ref, cum_ref, win_ref,
                         mk_ref, mv_ref, pw_ref, ps_ref, gg_ref, wbr_ref, wo_ref, fg_ref,
                         out_ref, s_out_ref, hist_out_ref,
                         h_scr, p_scr, la_scr, br_scr, st_ref, ubuf_ref, *, tm, nt, final):
    s = pl.program_id(0)
    t = jnp.maximum(s - 1, 0) % nt
    slot_w = s % 2
    slot_r = 1 - slot_w

    @pl.when(s == 0)
    def _():
        p_scr[1] = jnp.zeros(p_scr.shape[1:], p_scr.dtype)
        la_scr[1] = jnp.zeros(la_scr.shape[1:], la_scr.dtype)

    @pl.when(t == 0)
    def _():
        st_ref[...] = jnp.zeros_like(st_ref)
        ubuf_ref[0:HIST_PAD, :] = jnp.zeros((HIST_PAD, D_MODEL), F32)

    @pl.when(s == 0)
    def _():
        h_scr[0] = _rms(xc_ref[...], g_ref[...]).astype(BF)

    p_next = p_scr.at[slot_w]
    p = p_scr.at[slot_r]

    pieces = [(j, part) for j in range(N_SLABS) for part in range(PROJ_PARTS)]

    def filler():
        if pieces:
            j, part = pieces.pop(0)
            _inproj_slab(h_scr[slot_w], wa_ref, wb_ref, p_next, j, part, PROJ_PARTS)

    alr = _inproj_gate_lowrank(h_scr[slot_w], walr_ref)
    bcum = _gla_cumdecay(cum_ref, la_scr.at[slot_r])
    filler()
    _inproj_gate(alr, wa2_ref, ba_ref, la_scr.at[slot_w])
    filler()
    probs = None
    for c in range(tm // CHUNK):
        _gla_chunk(p, bcum, gg_ref, br_scr, st_ref, c, filler)
        filler()
        if c == 0:
            probs = _xattn_probs(p, mk_ref)
            filler()
        elif c == 1:
            _xattn_branch(p, probs, mv_ref, br_scr)
            filler()
        elif c == 2:
            _pool_branch(p, win_ref, pw_ref, ps_ref, br_scr, ubuf_ref, t, tm)
            filler()
    h_scr[slot_r] = _rms(xn_ref[...], g_ref[...]).astype(BF)
    merged = _merge_term(br_scr, p, wbr_ref, 2)
    filler()
    merged = merged + _merge_term(br_scr, p, wbr_ref, 1)
    filler()
    merged = merged + _merge_term(br_scr, p, wbr_ref, 0)
    while pieces:
        filler()
    out_ref[...] = _merge_finish(merged, xc_ref[...], wo_ref, fg_ref, final)

    @pl.when((s > 0) & (t == nt - 1))
    def _():
        for h in range(GLA_HEADS):
            s_out_ref[0, 0, h] = st_ref[h].T
        hist_out_ref[0, 0] = ubuf_ref[1:HIST_PAD, :]


def _prompt_layer(x, ngain, w_head, w_tail, w_alr, wa2, ba, mk, mv, pool_w, pool_scale, gla_gain,
                  w_branch, w_out, fgain, carry, layer, depth, batch, seq, n_mem, tm, final):
    nt = seq // tm
    n_tiles = batch * nt
    assert tm // CHUNK >= 3, "the mixer stages are spread over the first three GLA chunks of a tile"
    n_alias = 0 if carry is None else len(carry)
    n_in = 18
    kern = _drop_alias_refs(functools.partial(_prompt_layer_kernel, tm=tm, nt=nt, final=final), n_in, n_alias)
    any_spec = pl.BlockSpec(memory_space=pl.ANY)

    def cur(s):
        return jnp.maximum(s - 1, 0)

    def seq_of(s):
        return cur(s) // nt

    return pl.pallas_call(
        kern,
        grid=(n_tiles + 1,),
        in_specs=[
            pl.BlockSpec((tm, D_MODEL), lambda s: (jnp.minimum(s + 1, n_tiles - 1), 0)),
            pl.BlockSpec((tm, D_MODEL), lambda s: (cur(s), 0)),
            _const_spec((1, D_MODEL)),
            _const_spec((D_MODEL // 2, N_HEAD_SLABS * D_MODEL)),
            _const_spec((D_MODEL // 2, (N_SLABS - N_HEAD_SLABS) * D_MODEL)),
            _const_spec((D_MODEL, GATE_RANK)),
            _const_spec((GATE_RANK, GLA_KEY_WIDTH)),
            _const_spec((1, GLA_KEY_WIDTH)),
            _const_spec((tm, tm)),
            _const_spec((len(POOL_WINDOWS), tm, tm)),
            pl.BlockSpec((1, n_mem, D_MODEL), lambda s: (layer, seq_of(s), 0)),
            pl.BlockSpec((1, n_mem, D_MODEL), lambda s: (layer, seq_of(s), 0)),
            _const_spec((len(POOL_WINDOWS), POOL_GROUP_DIM, POOL_GROUP_DIM)),
            _const_spec((1, D_MODEL)),
            _const_spec((1, D_MODEL)),
            _const_spec((N_BRANCH, D_MODEL // 2, D_MODEL)),
            _const_spec((D_MODEL // 2, D_MODEL)),
            _const_spec((1, D_MODEL)),
        ] + [any_spec] * n_alias,
        out_specs=[
            pl.BlockSpec((tm, D_MODEL), lambda s: (cur(s), 0)),
            pl.BlockSpec((1, 1, GLA_HEADS, GLA_DK, GLA_DV), lambda s: (layer, seq_of(s), 0, 0, 0)),
            pl.BlockSpec((1, 1, POOL_HIST, D_MODEL), lambda s: (layer, seq_of(s), 0, 0)),
        ],
        out_shape=[
            jax.ShapeDtypeStruct((n_tiles * tm, D_MODEL), F32),
            jax.ShapeDtypeStruct((depth, batch, GLA_HEADS, GLA_DK, GLA_DV), F32),
            jax.ShapeDtypeStruct((depth, batch, POOL_HIST, D_MODEL), F32),
        ],
        scratch_shapes=[
            pltpu.VMEM((2, tm, D_MODEL), BF),
            pltpu.VMEM((2, N_SLABS, tm, D_MODEL), BF),
            pltpu.VMEM((2, tm, GLA_KEY_WIDTH), F32),
            pltpu.VMEM((N_BRANCH, tm, D_MODEL), BF),
            pltpu.VMEM((GLA_HEADS, GLA_DV, GLA_DK), F32),
            pltpu.VMEM((2 * HIST_PAD, D_MODEL), F32),
        ],
        input_output_aliases={n_in + a: 1 + a for a in range(n_alias)},
        compiler_params=_params(("arbitrary",), 60),
        name="prompt_layer",
    )(x, x, ngain, w_head, w_tail, w_alr, wa2, ba, _chunk_cumsum_matrix(tm), _window_matrices(tm),
      mk, mv, pool_w, pool_scale, gla_gain,
      w_branch, w_out, fgain, *(carry or ()))


def _inproj_kernel(x_ref, g_ref, wa_ref, wb_ref, walr_ref, wa2_ref, ba_ref, p_ref, la_ref):
    h = _rms(x_ref[...], g_ref[...]).astype(BF)
    alr = _inproj_gate_lowrank(h, walr_ref)
    for j in range(N_SLABS):
        _inproj_slab(h, wa_ref, wb_ref, p_ref, j)
    _inproj_gate(alr, wa2_ref, ba_ref, la_ref)


def _inproj(x, gain, w_head, w_tail, w_alr, w_a2, b_a, tm, out_dtype):
    m_rows = x.shape[0]
    return pl.pallas_call(
        _inproj_kernel,
        grid=(m_rows // tm,),
        in_specs=[
            pl.BlockSpec((tm, D_MODEL), lambda i: (i, 0)),
            _const_spec((1, D_MODEL)),
            _const_spec((D_MODEL // 2, N_HEAD_SLABS * D_MODEL)),
            _const_spec((D_MODEL // 2, (N_SLABS - N_HEAD_SLABS) * D_MODEL)),
            _const_spec((D_MODEL, GATE_RANK)),
            _const_spec((GATE_RANK, GLA_KEY_WIDTH)),
            _const_spec((1, GLA_KEY_WIDTH)),
        ],
        out_specs=[
            pl.BlockSpec((N_SLABS, tm, D_MODEL), lambda i: (0, i, 0)),
            pl.BlockSpec((tm, GLA_KEY_WIDTH), lambda i: (i, 0)),
        ],
        out_shape=[
            jax.ShapeDtypeStruct((N_SLABS, m_rows, D_MODEL), out_dtype),
            jax.ShapeDtypeStruct((m_rows, GLA_KEY_WIDTH), F32),
        ],
        compiler_params=_params(("parallel",), 48),
        name="inproj",
    )(x, gain, w_head, w_tail, w_alr, w_a2, b_a)


def _sample_mix_kernel(p_ref, la_ref, s0_ref, hist_ref, ck_ref, cv_ref, pw_ref, ps_ref, gg_ref,
                       br_ref, s_out_ref, hist_out_ref, diff_ref, *, sb):
    r0 = pl.program_id(1) * sb
    erow = lax.broadcasted_iota(jnp.int32, (GLA_DK, GLA_DK), 0)
    ecol = lax.broadcasted_iota(jnp.int32, (GLA_DK, GLA_DK), 1)
    eye = erow == ecol

    def to_col(x):
        return jnp.sum(jnp.where(eye, jnp.broadcast_to(x, (GLA_DK, GLA_DK)), 0.0), axis=1, keepdims=True)

    for i in range(sb):
        r = pl.ds(r0 + i, 1)
        la = la_ref[r, :]
        qk = p_ref[SLAB_QK, r, :]
        vv = p_ref[SLAB_V, r, :]
        gla_g = p_ref[SLAB_GLA_G, r, :]
        for h in range(GLA_HEADS):
            kc = slice(h * GLA_DK, (h + 1) * GLA_DK)
            kc2 = slice(GLA_KEY_WIDTH + h * GLA_DK, GLA_KEY_WIDTH + (h + 1) * GLA_DK)
            vc = slice(h * GLA_DV, (h + 1) * GLA_DV)
            a_col = to_col(jnp.exp(la[:, kc]))
            q_col = to_col(qk[:, kc] * (GLA_DK ** -0.5))
            k_col = to_col(qk[:, kc2])
            s_new = a_col * s0_ref[0, i, h] + k_col * vv[:, vc]
            s_out_ref[0, i, h] = s_new
            o = jnp.sum(q_col * s_new, axis=0, keepdims=True)
            br_ref[0, r, vc] = _rms(o, gg_ref[:, vc]) * _silu(gla_g[:, vc])

        u = p_ref[SLAB_U, r, :]
        for g, w in enumerate(POOL_WINDOWS):
            cs = slice(g * POOL_GROUP_DIM, (g + 1) * POOL_GROUP_DIM)
            past = jnp.sum(hist_ref[0, i, POOL_HIST - (w - 1):POOL_HIST, cs], axis=0, keepdims=True)
            diff_ref[r, cs] = (u[:, cs] + past) / float(w) - u[:, cs]
        hist_out_ref[0, i, 0:POOL_HIST - 1, :] = hist_ref[0, i, 1:POOL_HIST, :]
        hist_out_ref[0, i, POOL_HIST - 1:POOL_HIST, :] = u

        xq = p_ref[SLAB_XQ, r, :]
        xg = p_ref[SLAB_XG, r, :]
        half_cols = [slice(h * XA_HEAD_DIM + j * LANES, h * XA_HEAD_DIM + (j + 1) * LANES)
                     for j in range(XA_HEAD_DIM // LANES) for h in range(XA_HEADS)]
        xq_rows = jnp.concatenate([xq[:, cs] for cs in half_cols], axis=0)
        part = jnp.sum(ck_ref[0, i] * xq_rows[None], axis=-1, keepdims=True)
        s = (part + pltpu.roll(part, XA_HEADS, axis=1)) * (XA_HEAD_DIM ** -0.5)
        p = jnp.exp(s - jnp.max(s, axis=0, keepdims=True))
        o = jnp.sum(p * cv_ref[0, i], axis=0) / jnp.sum(p, axis=0)
        halves = XA_HEAD_DIM // LANES
        o_row = jnp.concatenate([o[j * XA_HEADS + h:j * XA_HEADS + h + 1, :]
                                 for h in range(XA_HEADS) for j in range(halves)], axis=1)
        br_ref[2, r, :] = o_row * _silu(xg)

    @pl.when(pl.program_id(1) == pl.num_programs(1) - 1)
    def _():
        for g in range(len(POOL_WINDOWS)):
            cs = slice(g * POOL_GROUP_DIM, (g + 1) * POOL_GROUP_DIM)
            mixed = _dot(diff_ref[:, cs].astype(BF), pw_ref[g]) * ps_ref[:, cs]
            br_ref[1, :, cs] = mixed * _silu(p_ref[SLAB_POOL_G, :, cs])


def _cache_rows_view(c):
    depth, nb, n_mem = c.shape[:3]
    halves = XA_HEAD_DIM // LANES
    c = c.reshape(depth, nb, n_mem, XA_HEADS, halves, LANES)
    return c.transpose(0, 1, 2, 4, 3, 5).reshape(depth, nb, n_mem, halves * XA_HEADS, LANES)


def _sample_mix(p, la, s0, hist, ck, cv, pool_w, pool_scale, gla_gain, carry, layer, sb=4):
    nb = s0.shape[1]
    n_mem = ck.shape[2]
    ck, cv = _cache_rows_view(ck), _cache_rows_view(cv)
    rb = SUBLANES
    halves = rb // sb
    n_alias = 0 if carry is None else len(carry)
    n_in = 9
    kern = _drop_alias_refs(functools.partial(_sample_mix_kernel, sb=sb), n_in, n_alias)
    any_spec = pl.BlockSpec(memory_space=pl.ANY)
    return pl.pallas_call(
        kern,
        grid=(nb // rb, halves),
        in_specs=[
            pl.BlockSpec((N_MIX_SLABS, rb, D_MODEL), lambda i, j: (0, i, 0)),
            pl.BlockSpec((rb, GLA_KEY_WIDTH), lambda i, j: (i, 0)),
            pl.BlockSpec((1, sb, GLA_HEADS, GLA_DK, GLA_DV), lambda i, j: (layer, i * halves + j, 0, 0, 0)),
            pl.BlockSpec((1, sb, POOL_HIST, D_MODEL), lambda i, j: (layer, i * halves + j, 0, 0)),
            pl.BlockSpec((1, sb, n_mem, SUBLANES, LANES), lambda i, j: (layer, i * halves + j, 0, 0, 0)),
            pl.BlockSpec((1, sb, n_mem, SUBLANES, LANES), lambda i, j: (layer, i * halves + j, 0, 0, 0)),
            _const_spec((len(POOL_WINDOWS), POOL_GROUP_DIM, POOL_GROUP_DIM)),
            _const_spec((1, D_MODEL)),
            _const_spec((1, D_MODEL)),
        ] + [any_spec] * n_alias,
        out_specs=[
            pl.BlockSpec((N_BRANCH, rb, D_MODEL), lambda i, j: (0, i, 0)),
            pl.BlockSpec((1, sb, GLA_HEADS, GLA_DK, GLA_DV), lambda i, j: (layer, i * halves + j, 0, 0, 0)),
            pl.BlockSpec((1, sb, POOL_HIST, D_MODEL), lambda i, j: (layer, i * halves + j, 0, 0)),
        ],
        out_shape=[
            jax.ShapeDtypeStruct((N_BRANCH, nb, D_MODEL), F32),
            jax.ShapeDtypeStruct(s0.shape, F32),
            jax.ShapeDtypeStruct(hist.shape, F32),
        ],
        scratch_shapes=[pltpu.VMEM((rb, D_MODEL), F32)],
        input_output_aliases={n_in + a: 1 + a for a in range(n_alias)},
        compiler_params=_params(("parallel", "arbitrary"), 40),
        name="sample_mix",
    )(p, la, s0, hist, ck, cv, pool_w, pool_scale, gla_gain, *(carry or ()))


def _merge_kernel(br_ref, p_ref, x_ref, wbr_ref, wo_ref, fg_ref, out_ref, *, final):
    merged = _merge_term(br_ref, p_ref, wbr_ref, 0)
    for n in range(1, N_BRANCH):
        merged = merged + _merge_term(br_ref, p_ref, wbr_ref, n)
    out_ref[...] = _merge_finish(merged, x_ref[...], wo_ref, fg_ref, final)


def _merge_out(br, p, x, w_branch, w_out, final_gain, final):
    m_rows = x.shape[0]
    whole = lambda shape: pl.BlockSpec(shape, lambda i: (0,) * len(shape))
    return pl.pallas_call(
        functools.partial(_merge_kernel, final=final),
        grid=(1,),
        in_specs=[
            whole((N_BRANCH, m_rows, D_MODEL)),
            whole((N_SLABS, m_rows, D_MODEL)),
            whole((m_rows, D_MODEL)),
            _const_spec((N_BRANCH, D_MODEL // 2, D_MODEL)),
            _const_spec((D_MODEL // 2, D_MODEL)),
            _const_spec((1, D_MODEL)),
        ],
        out_specs=whole((m_rows, D_MODEL)),
        out_shape=jax.ShapeDtypeStruct((m_rows, D_MODEL), F32),
        compiler_params=_params(("arbitrary",), 40),
        name="merge_out",
    )(br, p, x, w_branch, w_out, final_gain)


def kernel(x_prompt, x_sample, mem_prompt, cache_mem_k, cache_mem_v, state_gla, state_pool, w_in, w_a2, b_a, gla_gain, pool_w, pool_scale, w_mk, w_mv, w_branch, w_out, norm_gain, final_gain):
    batch, seq, _ = x_prompt.shape
    nb = x_sample.shape[0]
    n_mem = mem_prompt.shape[1]
    depth = w_in.shape[0]
    tm = min(256, seq)

    xp = x_prompt.reshape(batch * seq, D_MODEL)
    xs = x_sample.reshape(nb, D_MODEL)
    mem = mem_prompt.reshape(batch * n_mem, D_MODEL)
    fgain = final_gain.reshape(1, D_MODEL)

    mk, mv, mk_bf, mv_bf = _kvproj(mem, w_mk.astype(BF), w_mv.astype(BF), batch, n_mem)

    carry_p, carry_s = None, None
    for l in range(depth):
        final = l == depth - 1
        w_head = _pack_rows(w_in[l, :, :ALR_START])
        w_tail = _pack_rows(w_in[l, :, ALR_START + GATE_RANK:])
        w_alr = w_in[l, :, ALR_START:ALR_START + GATE_RANK].astype(BF)
        wa2 = w_a2[l].astype(BF)
        ba = b_a[l].reshape(1, GLA_KEY_WIDTH)
        ngain = norm_gain[l].reshape(1, D_MODEL)
        ggain = gla_gain[l].reshape(1, D_MODEL)
        pscale = pool_scale[l].reshape(1, D_MODEL)
        pw = pool_w[l].astype(BF)
        wb = _pack_rows(w_branch[l])
        wo = _pack_rows(w_out[l])

        xp, s_all, hist_all = _prompt_layer(xp, ngain, w_head, w_tail, w_alr, wa2, ba, mk_bf, mv_bf, pw, pscale,
                                            ggain, wb, wo, fgain, carry_p, l, depth, batch, seq, n_mem, tm, final)
        carry_p = (s_all, hist_all)

        ps, las = _inproj(xs, ngain, w_head, w_tail, w_alr, wa2, ba, nb, F32)
        brs, s_new, hist_new = _sample_mix(ps, las, state_gla, state_pool, cache_mem_k, cache_mem_v,
                                           pw, pscale, ggain, carry_s, l)
        carry_s = (s_new, hist_new)
        xs = _merge_out(brs, ps, xs, wb, wo, fgain, final)

    return (xp.reshape(batch, seq, D_MODEL), xs.reshape(nb, 1, D_MODEL),
            mk, mv, carry_p[0], carry_p[1], carry_s[0], carry_s[1])
```

```python
import functools

import jax
import jax.numpy as jnp
from jax import lax
from jax.experimental import pallas as pl
from jax.experimental.pallas import tpu as pltpu

D_MODEL = 1024
GLA_HEADS = 4
GLA_DK = 128
GLA_DV = 256
GLA_KEY_WIDTH = GLA_HEADS * GLA_DK
GATE_RANK = 16
GATE_TAU = 16.0
CHUNK = 64
POOL_WINDOWS = (2, 4, 8, 16)
POOL_GROUP_DIM = 256
POOL_HIST = 15
HIST_PAD = 16
XA_HEADS = 4
XA_HEAD_DIM = 256
N_BRANCH = 3
EPS = 1e-6
SUBLANES = 8
LANES = 128

SLAB_QK, SLAB_V, SLAB_GLA_G, SLAB_U, SLAB_POOL_G, SLAB_XQ, SLAB_XG, SLAB_MERGE = 0, 1, 2, 3, 4, 5, 6, 7
N_SLABS = 10
N_MIX_SLABS = 7
N_HEAD_SLABS = 3
ALR_START = N_HEAD_SLABS * D_MODEL
PROJ_PARTS = 2

BF = jnp.bfloat16
F32 = jnp.float32
MIB = 1 << 20


def _dot(a, b):
    return jnp.dot(a, b, preferred_element_type=F32)


def _dot_nt(a, b):
    return lax.dot_general(a, b, (((1,), (1,)), ((), ())), preferred_element_type=F32)


def _dot_tn(a, b):
    return lax.dot_general(a, b, (((0,), (0,)), ((), ())), preferred_element_type=F32)


def _pack_rows(w):
    return pltpu.bitcast(w.astype(BF), jnp.uint32)


def _unpack_rows(w_words):
    return pltpu.bitcast(w_words, BF)


def _params(sem, vmem_mib):
    return pltpu.CompilerParams(dimension_semantics=sem, vmem_limit_bytes=vmem_mib * MIB)


def _pack_w_in_kernel(a_ref, b_ref, o_ref):
    j = pl.program_id(1)

    @pl.when(j < N_HEAD_SLABS)
    def _():
        o_ref[0] = _pack_rows(a_ref[0])

    @pl.when(j >= N_HEAD_SLABS)
    def _():
        o_ref[0] = _pack_rows(jnp.concatenate([a_ref[0, :, GATE_RANK:], b_ref[0, :, :GATE_RANK]], axis=1))


def _pack_w_in(w_in):
    depth = w_in.shape[0]
    return pl.pallas_call(
        _pack_w_in_kernel,
        grid=(depth, N_SLABS),
        in_specs=[
            pl.BlockSpec((1, D_MODEL, D_MODEL), lambda l, j: (l, 0, j)),
            pl.BlockSpec((1, D_MODEL, LANES), lambda l, j: (l, 0, (j + 1) * (D_MODEL // LANES))),
        ],
        out_specs=pl.BlockSpec((1, D_MODEL // 2, D_MODEL), lambda l, j: (l, 0, j)),
        out_shape=jax.ShapeDtypeStruct((depth, D_MODEL // 2, N_SLABS * D_MODEL), jnp.uint32),
        compiler_params=_params(("parallel", "parallel"), 32),
        name="pack_w_in",
    )(w_in, w_in)


def _pack_matrices_kernel(a_ref, o_ref):
    o_ref[0] = _pack_rows(a_ref[0])


def _pack_matrices(w):
    return pl.pallas_call(
        _pack_matrices_kernel,
        grid=(w.shape[0],),
        in_specs=[pl.BlockSpec((1, D_MODEL, D_MODEL), lambda r: (r, 0, 0))],
        out_specs=pl.BlockSpec((1, D_MODEL // 2, D_MODEL), lambda r: (r, 0, 0)),
        out_shape=jax.ShapeDtypeStruct((w.shape[0], D_MODEL // 2, D_MODEL), jnp.uint32),
        compiler_params=_params(("parallel",), 32),
        name="pack_matrices",
    )(w)


def _silu(x):
    return x * jax.nn.sigmoid(x)


def _rms(x, gain):
    ms = jnp.mean(x * x, axis=-1, keepdims=True)
    return x * lax.rsqrt(ms + EPS) * gain


def _const_spec(shape):
    zeros = (0,) * len(shape)
    return pl.BlockSpec(shape, lambda *_: zeros, pipeline_mode=pl.Buffered(1))


def _layer_spec(shape, layer):
    index = (layer,) + (0,) * len(shape)
    return pl.BlockSpec((1,) + tuple(shape), lambda *_: index, pipeline_mode=pl.Buffered(1))


def _drop_alias_refs(body, n_in, n_alias):
    def kern(*refs):
        return body(*refs[:n_in], *refs[n_in + n_alias:])
    return kern


def _kvproj_kernel(m_ref, wk_ref, wv_ref, k_ref, v_ref, kb_ref, vb_ref):
    m = m_ref[...].astype(BF)
    k = _dot(m, wk_ref[0])
    v = _dot(m, wv_ref[0])
    for h in range(XA_HEADS):
        cs = slice(h * XA_HEAD_DIM, (h + 1) * XA_HEAD_DIM)
        k_ref[0, 0, :, h, :] = k[:, cs]
        v_ref[0, 0, :, h, :] = v[:, cs]
    kb_ref[0] = k.astype(BF)
    vb_ref[0] = v.astype(BF)


def _kvproj(mem, wk, wv, batch, n_mem):
    depth = wk.shape[0]
    w_spec = pl.BlockSpec((1, D_MODEL, D_MODEL), lambda l, b: (l, 0, 0))
    out5 = pl.BlockSpec((1, 1, n_mem, XA_HEADS, XA_HEAD_DIM), lambda l, b: (l, b, 0, 0, 0))
    out_bf = pl.BlockSpec((1, n_mem, D_MODEL), lambda l, b: (l, b, 0))
    return pl.pallas_call(
        _kvproj_kernel,
        grid=(depth, batch),
        in_specs=[pl.BlockSpec((n_mem, D_MODEL), lambda l, b: (b, 0)), w_spec, w_spec],
        out_specs=[out5, out5, out_bf, out_bf],
        out_shape=[jax.ShapeDtypeStruct((depth, batch, n_mem, XA_HEADS, XA_HEAD_DIM), F32)] * 2
        + [jax.ShapeDtypeStruct((depth, batch * n_mem, D_MODEL), BF)] * 2,
        compiler_params=_params(("parallel", "parallel"), 32),
        name="kvproj",
    )(mem, wk, wv)


def _inproj_slab(h, w_ref, p_out, j, part=0, n_parts=1):
    width = D_MODEL // n_parts
    lo = part * width
    w = w_ref[0, :, j * D_MODEL + lo:j * D_MODEL + lo + width]
    p_out[j, :, lo:lo + width] = _dot(h, _unpack_rows(w)).astype(p_out.dtype)


def _inproj_gate_lowrank(h, walr_ref):
    return _dot(h, walr_ref[...].astype(BF)).astype(BF)


def _inproj_gate(alr, wa2_ref, ba_ref, la_out):
    z = _dot(alr, wa2_ref[...]) + ba_ref[...]
    la_out[...] = (jnp.minimum(z, 0.0) - jnp.log(1.0 + jnp.exp(-jnp.abs(z)))) * (1.0 / GATE_TAU)


def _merge_term(br_ref, p, wbr_ref, n):
    return jax.nn.sigmoid(p[SLAB_MERGE + n].astype(F32)) * _dot(br_ref[n].astype(BF), _unpack_rows(wbr_ref[0, n]))


def _merge_finish(merged, x, wo_ref, fg_ref, final):
    x_new = x + _dot(merged.astype(BF), _unpack_rows(wo_ref[0]))
    return _rms(x_new, fg_ref[...]) if final else x_new


def _chunk_cumsum_matrix(tm):
    row = lax.broadcasted_iota(jnp.int32, (tm, tm), 0)
    col = lax.broadcasted_iota(jnp.int32, (tm, tm), 1)
    return (((row // CHUNK) == (col // CHUNK)) & (row >= col)).astype(BF)


def _window_matrices(tm):
    row = lax.broadcasted_iota(jnp.int32, (tm, tm), 0)
    col = lax.broadcasted_iota(jnp.int32, (tm, tm), 1)
    return jnp.stack([((row >= col) & (row - col < w)).astype(BF) for w in POOL_WINDOWS])


def _gla_cumdecay(cum_ref, la_ref):
    la = la_ref[...]
    la_hi = la.astype(BF)
    la_lo = (la - la_hi.astype(F32)).astype(BF)
    return _dot(cum_ref[...], la_hi) + _dot(cum_ref[...], la_lo)


def _gla_chunk(p, bcum, gg_ref, br_ref, st_ref, c, filler):
    crow = lax.broadcasted_iota(jnp.int32, (CHUNK, CHUNK), 0)
    ccol = lax.broadcasted_iota(jnp.int32, (CHUNK, CHUNK), 1)
    causal = crow >= ccol
    rows = slice(c * CHUNK, (c + 1) * CHUNK)
    heads = range(GLA_HEADS)
    q_dec, k_end, decay, att = [], [], [], []
    for h in heads:
        kc = slice(h * GLA_DK, (h + 1) * GLA_DK)
        kc2 = slice(GLA_KEY_WIDTH + h * GLA_DK, GLA_KEY_WIDTH + (h + 1) * GLA_DK)
        b = bcum[rows, kc]
        b_last = b[CHUNK - 1:CHUNK, :]
        q = p[SLAB_QK, rows, kc].astype(F32)
        k = p[SLAB_QK, rows, kc2].astype(F32)
        q_dec.append((q * (GLA_DK ** -0.5) * jnp.exp(b)).astype(BF))
        k_inv = (k * jnp.exp(-b)).astype(BF)
        k_end.append((k * jnp.exp(b_last - b)).astype(BF))
        decay.append(jnp.exp(b_last))
        att.append(_dot_nt(q_dec[h], k_inv))
    filler()
    o = []
    for h in heads:
        vc = slice(h * GLA_DV, (h + 1) * GLA_DV)
        a = jnp.where(causal, att[h], 0.0).astype(BF)
        o.append(_dot(a, p[SLAB_V, rows, vc]) + _dot_nt(q_dec[h], st_ref[h].astype(BF)))
    filler()
    for h in heads:
        vc = slice(h * GLA_DV, (h + 1) * GLA_DV)
        st_ref[h] = decay[h] * st_ref[h] + _dot_tn(p[SLAB_V, rows, vc], k_end[h])
    for h in heads:
        vc = slice(h * GLA_DV, (h + 1) * GLA_DV)
        g = p[SLAB_GLA_G, rows, vc].astype(F32)
        br_ref[0, rows, vc] = (_rms(o[h], gg_ref[:, vc]) * _silu(g)).astype(BF)


def _pool_branch(p, win_ref, pw_ref, ps_ref, br_ref, ubuf_ref, t, tm):
    u_bf = p[SLAB_U]
    u = u_bf.astype(F32)
    ubuf_ref[HIST_PAD:2 * HIST_PAD, :] = u[0:HIST_PAD]
    pos = t * tm + lax.broadcasted_iota(jnp.int32, (tm, 1), 0)
    for g, w in enumerate(POOL_WINDOWS):
        cs = slice(g * POOL_GROUP_DIM, (g + 1) * POOL_GROUP_DIM)
        ug = u[:, cs]
        head = ug[0:HIST_PAD]
        for j in range(1, w):
            head = head + ubuf_ref[HIST_PAD - j:2 * HIST_PAD - j, cs]
        s = jnp.concatenate([head, _dot(win_ref[g], u_bf[:, cs])[HIST_PAD:]], axis=0)
        cnt = jnp.minimum(w, pos + 1).astype(F32)
        diff = s / cnt - ug
        mixed = _dot(diff.astype(BF), pw_ref[g]) * ps_ref[:, cs]
        pg = p[SLAB_POOL_G, :, cs].astype(F32)
        br_ref[1, :, cs] = (mixed * _silu(pg)).astype(BF)
    ubuf_ref[0:HIST_PAD, :] = u[tm - HIST_PAD:tm]


def _xattn_probs(p, mk_ref):
    out = []
    for h in range(XA_HEADS):
        cs = slice(h * XA_HEAD_DIM, (h + 1) * XA_HEAD_DIM)
        s = _dot_nt(p[SLAB_XQ, :, cs], mk_ref[0, :, cs]) * (XA_HEAD_DIM ** -0.5)
        pr = jnp.exp(s - jnp.max(s, axis=-1, keepdims=True))
        out.append((pr.astype(BF), jnp.sum(pr, axis=-1, keepdims=True)))
    return out


def _xattn_branch(p, probs, mv_ref, br_ref):
    for h in range(XA_HEADS):
        cs = slice(h * XA_HEAD_DIM, (h + 1) * XA_HEAD_DIM)
        pr, denom = probs[h]
        o = _dot(pr, mv_ref[0, :, cs]) / denom
        xg = p[SLAB_XG, :, cs].astype(F32)
        br_ref[2, :, cs] = (o * _silu(xg)).astype(BF)


def _prompt_layer_kernel(xn_ref, xc_ref, g_ref, w_ref, walr_ref, wa2_ref, ba_ref, cum_ref, win_ref,
                         mk_ref, mv_ref, pw_ref, ps_ref, gg_ref, wbr_ref, wo_ref, fg_ref,
                         out_ref, s_out_ref, hist_out_ref,
                         h_scr, p_scr, la_scr, br_scr, st_ref, ubuf_ref, *, tm, nt, final):
    s = pl.program_id(0)
    t = jnp.maximum(s - 1, 0) % nt
    slot_w = s % 2
    slot_r = 1 - slot_w

    @pl.when(s == 0)
    def _():
        p_scr[1] = jnp.zeros(p_scr.shape[1:], p_scr.dtype)
        la_scr[1] = jnp.zeros(la_scr.shape[1:], la_scr.dtype)

    @pl.when(t == 0)
    def _():
        st_ref[...] = jnp.zeros_like(st_ref)
        ubuf_ref[0:HIST_PAD, :] = jnp.zeros((HIST_PAD, D_MODEL), F32)

    @pl.when(s == 0)
    def _():
        h_scr[0] = _rms(xc_ref[...], g_ref[...]).astype(BF)

    p_next = p_scr.at[slot_w]
    p = p_scr.at[slot_r]

    pieces = [(j, part) for j in range(N_SLABS) for part in range(PROJ_PARTS)]

    def filler():
        if pieces:
            j, part = pieces.pop(0)
            _inproj_slab(h_scr[slot_w], w_ref, p_next, j, part, PROJ_PARTS)

    alr = _inproj_gate_lowrank(h_scr[slot_w], walr_ref)
    bcum = _gla_cumdecay(cum_ref, la_scr.at[slot_r])
    filler()
    _inproj_gate(alr, wa2_ref, ba_ref, la_scr.at[slot_w])
    filler()
    probs = None
    for c in range(tm // CHUNK):
        _gla_chunk(p, bcum, gg_ref, br_scr, st_ref, c, filler)
        filler()
        if c == 0:
            probs = _xattn_probs(p, mk_ref)
            filler()
        elif c == 1:
            _xattn_branch(p, probs, mv_ref, br_scr)
            filler()
        elif c == 2:
            _pool_branch(p, win_ref, pw_ref, ps_ref, br_scr, ubuf_ref, t, tm)
            filler()
    h_scr[slot_r] = _rms(xn_ref[...], g_ref[...]).astype(BF)
    merged = _merge_term(br_scr, p, wbr_ref, 2)
    filler()
    merged = merged + _merge_term(br_scr, p, wbr_ref, 1)
    filler()
    merged = merged + _merge_term(br_scr, p, wbr_ref, 0)
    while pieces:
        filler()
    out_ref[...] = _merge_finish(merged, xc_ref[...], wo_ref, fg_ref, final)

    @pl.when((s > 0) & (t == nt - 1))
    def _():
        for h in range(GLA_HEADS):
            s_out_ref[0, 0, h] = st_ref[h].T
        hist_out_ref[0, 0] = ubuf_ref[1:HIST_PAD, :]


def _prompt_layer(x, ngain, w_proj, w_alr, wa2, ba, mk, mv, pool_w, pool_scale, gla_gain,
                  w_branch, w_out, fgain, carry, layer, depth, batch, seq, n_mem, tm, final):
    nt = seq // tm
    n_tiles = batch * nt
    assert tm // CHUNK >= 3, "the mixer stages are spread over the first three GLA chunks of a tile"
    n_alias = 0 if carry is None else len(carry)
    n_in = 17
    kern = _drop_alias_refs(functools.partial(_prompt_layer_kernel, tm=tm, nt=nt, final=final), n_in, n_alias)
    any_spec = pl.BlockSpec(memory_space=pl.ANY)

    def cur(s):
        return jnp.maximum(s - 1, 0)

    def seq_of(s):
        return cur(s) // nt

    return pl.pallas_call(
        kern,
        grid=(n_tiles + 1,),
        in_specs=[
            pl.BlockSpec((tm, D_MODEL), lambda s: (jnp.minimum(s + 1, n_tiles - 1), 0)),
            pl.BlockSpec((tm, D_MODEL), lambda s: (cur(s), 0)),
            _const_spec((1, D_MODEL)),
            _layer_spec((D_MODEL // 2, N_SLABS * D_MODEL), layer),
            _const_spec((D_MODEL, GATE_RANK)),
            _const_spec((GATE_RANK, GLA_KEY_WIDTH)),
            _const_spec((1, GLA_KEY_WIDTH)),
            _const_spec((tm, tm)),
            _const_spec((len(POOL_WINDOWS), tm, tm)),
            pl.BlockSpec((1, n_mem, D_MODEL), lambda s: (layer, seq_of(s), 0)),
            pl.BlockSpec((1, n_mem, D_MODEL), lambda s: (layer, seq_of(s), 0)),
            _const_spec((len(POOL_WINDOWS), POOL_GROUP_DIM, POOL_GROUP_DIM)),
            _const_spec((1, D_MODEL)),
            _const_spec((1, D_MODEL)),
            _layer_spec((N_BRANCH, D_MODEL // 2, D_MODEL), layer),
            _layer_spec((D_MODEL // 2, D_MODEL), layer),
            _const_spec((1, D_MODEL)),
        ] + [any_spec] * n_alias,
        out_specs=[
            pl.BlockSpec((tm, D_MODEL), lambda s: (cur(s), 0)),
            pl.BlockSpec((1, 1, GLA_HEADS, GLA_DK, GLA_DV), lambda s: (layer, seq_of(s), 0, 0, 0)),
            pl.BlockSpec((1, 1, POOL_HIST, D_MODEL), lambda s: (layer, seq_of(s), 0, 0)),
        ],
        out_shape=[
            jax.ShapeDtypeStruct((n_tiles * tm, D_MODEL), F32),
            jax.ShapeDtypeStruct((depth, batch, GLA_HEADS, GLA_DK, GLA_DV), F32),
            jax.ShapeDtypeStruct((depth, batch, POOL_HIST, D_MODEL), F32),
        ],
        scratch_shapes=[
            pltpu.VMEM((2, tm, D_MODEL), BF),
            pltpu.VMEM((2, N_SLABS, tm, D_MODEL), BF),
            pltpu.VMEM((2, tm, GLA_KEY_WIDTH), F32),
            pltpu.VMEM((N_BRANCH, tm, D_MODEL), BF),
            pltpu.VMEM((GLA_HEADS, GLA_DV, GLA_DK), F32),
            pltpu.VMEM((2 * HIST_PAD, D_MODEL), F32),
        ],
        input_output_aliases={n_in + a: 1 + a for a in range(n_alias)},
        compiler_params=_params(("arbitrary",), 60),
        name="prompt_layer",
    )(x, x, ngain, w_proj, w_alr, wa2, ba, _chunk_cumsum_matrix(tm), _window_matrices(tm),
      mk, mv, pool_w, pool_scale, gla_gain,
      w_branch, w_out, fgain, *(carry or ()))


def _inproj_kernel(x_ref, g_ref, w_ref, walr_ref, wa2_ref, ba_ref, p_ref, la_ref):
    h = _rms(x_ref[...], g_ref[...]).astype(BF)
    alr = _inproj_gate_lowrank(h, walr_ref)
    for j in range(N_SLABS):
        _inproj_slab(h, w_ref, p_ref, j)
    _inproj_gate(alr, wa2_ref, ba_ref, la_ref)


def _inproj(x, gain, w_proj, w_alr, w_a2, b_a, layer, tm, out_dtype):
    m_rows = x.shape[0]
    return pl.pallas_call(
        _inproj_kernel,
        grid=(m_rows // tm,),
        in_specs=[
            pl.BlockSpec((tm, D_MODEL), lambda i: (i, 0)),
            _const_spec((1, D_MODEL)),
            _layer_spec((D_MODEL // 2, N_SLABS * D_MODEL), layer),
            _const_spec((D_MODEL, GATE_RANK)),
            _const_spec((GATE_RANK, GLA_KEY_WIDTH)),
            _const_spec((1, GLA_KEY_WIDTH)),
        ],
        out_specs=[
            pl.BlockSpec((N_SLABS, tm, D_MODEL), lambda i: (0, i, 0)),
            pl.BlockSpec((tm, GLA_KEY_WIDTH), lambda i: (i, 0)),
        ],
        out_shape=[
            jax.ShapeDtypeStruct((N_SLABS, m_rows, D_MODEL), out_dtype),
            jax.ShapeDtypeStruct((m_rows, GLA_KEY_WIDTH), F32),
        ],
        compiler_params=_params(("parallel",), 48),
        name="inproj",
    )(x, gain, w_proj, w_alr, w_a2, b_a)


def _sample_mix_kernel(p_ref, la_ref, s0_ref, hist_ref, ck_ref, cv_ref, pw_ref, ps_ref, gg_ref,
                       br_ref, s_out_ref, hist_out_ref, diff_ref, *, sb):
    r0 = pl.program_id(1) * sb
    erow = lax.broadcasted_iota(jnp.int32, (GLA_DK, GLA_DK), 0)
    ecol = lax.broadcasted_iota(jnp.int32, (GLA_DK, GLA_DK), 1)
    eye = erow == ecol

    def to_col(x):
        return jnp.sum(jnp.where(eye, jnp.broadcast_to(x, (GLA_DK, GLA_DK)), 0.0), axis=1, keepdims=True)

    for i in range(sb):
        r = pl.ds(r0 + i, 1)
        la = la_ref[r, :]
        qk = p_ref[SLAB_QK, r, :]
        vv = p_ref[SLAB_V, r, :]
        gla_g = p_ref[SLAB_GLA_G, r, :]
        for h in range(GLA_HEADS):
            kc = slice(h * GLA_DK, (h + 1) * GLA_DK)
            kc2 = slice(GLA_KEY_WIDTH + h * GLA_DK, GLA_KEY_WIDTH + (h + 1) * GLA_DK)
            vc = slice(h * GLA_DV, (h + 1) * GLA_DV)
            a_col = to_col(jnp.exp(la[:, kc]))
            q_col = to_col(qk[:, kc] * (GLA_DK ** -0.5))
            k_col = to_col(qk[:, kc2])
            s_new = a_col * s0_ref[0, i, h] + k_col * vv[:, vc]
            s_out_ref[0, i, h] = s_new
            o = jnp.sum(q_col * s_new, axis=0, keepdims=True)
            br_ref[0, r, vc] = _rms(o, gg_ref[:, vc]) * _silu(gla_g[:, vc])

        u = p_ref[SLAB_U, r, :]
        for g, w in enumerate(POOL_WINDOWS):
            cs = slice(g * POOL_GROUP_DIM, (g + 1) * POOL_GROUP_DIM)
            past = jnp.sum(hist_ref[0, i, POOL_HIST - (w - 1):POOL_HIST, cs], axis=0, keepdims=True)
            diff_ref[r, cs] = (u[:, cs] + past) / float(w) - u[:, cs]
        hist_out_ref[0, i, 0:POOL_HIST - 1, :] = hist_ref[0, i, 1:POOL_HIST, :]
        hist_out_ref[0, i, POOL_HIST - 1:POOL_HIST, :] = u

        xq = p_ref[SLAB_XQ, r, :]
        xg = p_ref[SLAB_XG, r, :]
        half_cols = [slice(h * XA_HEAD_DIM + j * LANES, h * XA_HEAD_DIM + (j + 1) * LANES)
                     for j in range(XA_HEAD_DIM // LANES) for h in range(XA_HEADS)]
        xq_rows = jnp.concatenate([xq[:, cs] for cs in half_cols], axis=0)
        part = jnp.sum(ck_ref[0, i] * xq_rows[None], axis=-1, keepdims=True)
        s = (part + pltpu.roll(part, XA_HEADS, axis=1)) * (XA_HEAD_DIM ** -0.5)
        p = jnp.exp(s - jnp.max(s, axis=0, keepdims=True))
        o = jnp.sum(p * cv_ref[0, i], axis=0) / jnp.sum(p, axis=0)
        halves = XA_HEAD_DIM // LANES
        o_row = jnp.concatenate([o[j * XA_HEADS + h:j * XA_HEADS + h + 1, :]
                                 for h in range(XA_HEADS) for j in range(halves)], axis=1)
        br_ref[2, r, :] = o_row * _silu(xg)

    @pl.when(pl.program_id(1) == pl.num_programs(1) - 1)
    def _():
        for g in range(len(POOL_WINDOWS)):
            cs = slice(g * POOL_GROUP_DIM, (g + 1) * POOL_GROUP_DIM)
            mixed = _dot(diff_ref[:, cs].astype(BF), pw_ref[g]) * ps_ref[:, cs]
            br_ref[1, :, cs] = mixed * _silu(p_ref[SLAB_POOL_G, :, cs])


def _cache_rows_view(c):
    depth, nb, n_mem = c.shape[:3]
    halves = XA_HEAD_DIM // LANES
    c = c.reshape(depth, nb, n_mem, XA_HEADS, halves, LANES)
    return c.transpose(0, 1, 2, 4, 3, 5).reshape(depth, nb, n_mem, halves * XA_HEADS, LANES)


def _sample_mix(p, la, s0, hist, ck, cv, pool_w, pool_scale, gla_gain, carry, layer, sb=4):
    nb = s0.shape[1]
    n_mem = ck.shape[2]
    ck, cv = _cache_rows_view(ck), _cache_rows_view(cv)
    rb = SUBLANES
    halves = rb // sb
    n_alias = 0 if carry is None else len(carry)
    n_in = 9
    kern = _drop_alias_refs(functools.partial(_sample_mix_kernel, sb=sb), n_in, n_alias)
    any_spec = pl.BlockSpec(memory_space=pl.ANY)
    return pl.pallas_call(
        kern,
        grid=(nb // rb, halves),
        in_specs=[
            pl.BlockSpec((N_MIX_SLABS, rb, D_MODEL), lambda i, j: (0, i, 0)),
            pl.BlockSpec((rb, GLA_KEY_WIDTH), lambda i, j: (i, 0)),
            pl.BlockSpec((1, sb, GLA_HEADS, GLA_DK, GLA_DV), lambda i, j: (layer, i * halves + j, 0, 0, 0)),
            pl.BlockSpec((1, sb, POOL_HIST, D_MODEL), lambda i, j: (layer, i * halves + j, 0, 0)),
            pl.BlockSpec((1, sb, n_mem, SUBLANES, LANES), lambda i, j: (layer, i * halves + j, 0, 0, 0)),
            pl.BlockSpec((1, sb, n_mem, SUBLANES, LANES), lambda i, j: (layer, i * halves + j, 0, 0, 0)),
            _const_spec((len(POOL_WINDOWS), POOL_GROUP_DIM, POOL_GROUP_DIM)),
            _const_spec((1, D_MODEL)),
            _const_spec((1, D_MODEL)),
        ] + [any_spec] * n_alias,
        out_specs=[
            pl.BlockSpec((N_BRANCH, rb, D_MODEL), lambda i, j: (0, i, 0)),
            pl.BlockSpec((1, sb, GLA_HEADS, GLA_DK, GLA_DV), lambda i, j: (layer, i * halves + j, 0, 0, 0)),
            pl.BlockSpec((1, sb, POOL_HIST, D_MODEL), lambda i, j: (layer, i * halves + j, 0, 0)),
        ],
        out_shape=[
            jax.ShapeDtypeStruct((N_BRANCH, nb, D_MODEL), F32),
            jax.ShapeDtypeStruct(s0.shape, F32),
            jax.ShapeDtypeStruct(hist.shape, F32),
        ],
        scratch_shapes=[pltpu.VMEM((rb, D_MODEL), F32)],
        input_output_aliases={n_in + a: 1 + a for a in range(n_alias)},
        compiler_params=_params(("parallel", "arbitrary"), 40),
        name="sample_mix",
    )(p, la, s0, hist, ck, cv, pool_w, pool_scale, gla_gain, *(carry or ()))


def _merge_kernel(br_ref, p_ref, x_ref, wbr_ref, wo_ref, fg_ref, out_ref, *, final):
    merged = _merge_term(br_ref, p_ref, wbr_ref, 0)
    for n in range(1, N_BRANCH):
        merged = merged + _merge_term(br_ref, p_ref, wbr_ref, n)
    out_ref[...] = _merge_finish(merged, x_ref[...], wo_ref, fg_ref, final)


def _merge_out(br, p, x, w_branch, w_out, final_gain, layer, final):
    m_rows = x.shape[0]
    whole = lambda shape: pl.BlockSpec(shape, lambda i: (0,) * len(shape))
    return pl.pallas_call(
        functools.partial(_merge_kernel, final=final),
        grid=(1,),
        in_specs=[
            whole((N_BRANCH, m_rows, D_MODEL)),
            whole((N_SLABS, m_rows, D_MODEL)),
            whole((m_rows, D_MODEL)),
            _layer_spec((N_BRANCH, D_MODEL // 2, D_MODEL), layer),
            _layer_spec((D_MODEL // 2, D_MODEL), layer),
            _const_spec((1, D_MODEL)),
        ],
        out_specs=whole((m_rows, D_MODEL)),
        out_shape=jax.ShapeDtypeStruct((m_rows, D_MODEL), F32),
        compiler_params=_params(("arbitrary",), 40),
        name="merge_out",
    )(br, p, x, w_branch, w_out, final_gain)


def kernel(x_prompt, x_sample, mem_prompt, cache_mem_k, cache_mem_v, state_gla, state_pool, w_in, w_a2, b_a, gla_gain, pool_w, pool_scale, w_mk, w_mv, w_branch, w_out, norm_gain, final_gain):
    batch, seq, _ = x_prompt.shape
    nb = x_sample.shape[0]
    n_mem = mem_prompt.shape[1]
    depth = w_in.shape[0]
    tm = min(256, seq)

    xp = x_prompt.reshape(batch * seq, D_MODEL)
    xs = x_sample.reshape(nb, D_MODEL)
    mem = mem_prompt.reshape(batch * n_mem, D_MODEL)
    fgain = final_gain.reshape(1, D_MODEL)

    mk, mv, mk_bf, mv_bf = _kvproj(mem, w_mk.astype(BF), w_mv.astype(BF), batch, n_mem)

    w_proj = _pack_w_in(w_in)
    wb = _pack_matrices(w_branch.reshape(depth * N_BRANCH, D_MODEL, D_MODEL))
    wb = wb.reshape(depth, N_BRANCH, D_MODEL // 2, D_MODEL)
    wo = _pack_matrices(w_out)

    carry_p, carry_s = None, None
    for l in range(depth):
        final = l == depth - 1
        w_alr = w_in[l, :, ALR_START:ALR_START + GATE_RANK]
        wa2 = w_a2[l].astype(BF)
        ba = b_a[l].reshape(1, GLA_KEY_WIDTH)
        ngain = norm_gain[l].reshape(1, D_MODEL)
        ggain = gla_gain[l].reshape(1, D_MODEL)
        pscale = pool_scale[l].reshape(1, D_MODEL)
        pw = pool_w[l].astype(BF)

        xp, s_all, hist_all = _prompt_layer(xp, ngain, w_proj, w_alr, wa2, ba, mk_bf, mv_bf, pw, pscale,
                                            ggain, wb, wo, fgain, carry_p, l, depth, batch, seq, n_mem, tm, final)
        carry_p = (s_all, hist_all)

        ps, las = _inproj(xs, ngain, w_proj, w_alr, wa2, ba, l, nb, F32)
        brs, s_new, hist_new = _sample_mix(ps, las, state_gla, state_pool, cache_mem_k, cache_mem_v,
                                           pw, pscale, ggain, carry_s, l)
        carry_s = (s_new, hist_new)
        xs = _merge_out(brs, ps, xs, wb, wo, fgain, l, final)

    return (xp.reshape(batch, seq, D_MODEL), xs.reshape(nb, 1, D_MODEL),
            mk, mv, carry_p[0], carry_p[1], carry_s[0], carry_s[1])
```

```python
import functools

import jax
import jax.numpy as jnp
from jax import lax
from jax.experimental import pallas as pl
from jax.experimental.pallas import tpu as pltpu

D_MODEL = 1024
GLA_HEADS = 4
GLA_DK = 128
GLA_DV = 256
GLA_KEY_WIDTH = GLA_HEADS * GLA_DK
GATE_RANK = 16
GATE_TAU = 16.0
CHUNK = 64
POOL_WINDOWS = (2, 4, 8, 16)
POOL_GROUP_DIM = 256
POOL_HIST = 15
HIST_PAD = 16
XA_HEADS = 4
XA_HEAD_DIM = 256
N_BRANCH = 3
EPS = 1e-6
SUBLANES = 8
LANES = 128

SLAB_QK, SLAB_V, SLAB_GLA_G, SLAB_U, SLAB_POOL_G, SLAB_XQ, SLAB_XG, SLAB_MERGE = 0, 1, 2, 3, 4, 5, 6, 7
N_SLABS = 10
N_MIX_SLABS = 7
N_HEAD_SLABS = 3
ALR_START = N_HEAD_SLABS * D_MODEL
FILLER_SITES = 20
PROJ_PARTS = 1

BF = jnp.bfloat16
F32 = jnp.float32
MIB = 1 << 20


def _dot(a, b):
    return jnp.dot(a, b, preferred_element_type=F32)


def _dot_nt(a, b):
    return lax.dot_general(a, b, (((1,), (1,)), ((), ())), preferred_element_type=F32)


def _dot_tn(a, b):
    return lax.dot_general(a, b, (((0,), (0,)), ((), ())), preferred_element_type=F32)


def _pack_rows(w):
    return pltpu.bitcast(w.astype(BF), jnp.uint32)


def _unpack_rows(w_words):
    return pltpu.bitcast(w_words, BF)


def _params(sem, vmem_mib):
    return pltpu.CompilerParams(dimension_semantics=sem, vmem_limit_bytes=vmem_mib * MIB)


def _pack_w_in_kernel(a_ref, b_ref, o_ref):
    j = pl.program_id(1)

    @pl.when(j < N_HEAD_SLABS)
    def _():
        o_ref[0] = _pack_rows(a_ref[0].T)

    @pl.when(j >= N_HEAD_SLABS)
    def _():
        o_ref[0] = _pack_rows(jnp.concatenate([a_ref[0, GATE_RANK:, :], b_ref[0]], axis=0).T)


def _pack_w_in(w_in_t):
    depth = w_in_t.shape[0]
    return pl.pallas_call(
        _pack_w_in_kernel,
        grid=(depth, N_SLABS),
        in_specs=[
            pl.BlockSpec((1, D_MODEL, D_MODEL), lambda l, j: (l, j, 0)),
            pl.BlockSpec((1, GATE_RANK, D_MODEL), lambda l, j: (l, (j + 1) * (D_MODEL // GATE_RANK), 0)),
        ],
        out_specs=pl.BlockSpec((1, D_MODEL // 2, D_MODEL), lambda l, j: (l, 0, j)),
        out_shape=jax.ShapeDtypeStruct((depth, D_MODEL // 2, N_SLABS * D_MODEL), jnp.uint32),
        compiler_params=_params(("parallel", "parallel"), 32),
        name="pack_w_in",
    )(w_in_t, w_in_t)


def _pack_matrices_kernel(a_ref, o_ref):
    o_ref[0] = _pack_rows(a_ref[0])


def _pack_matrices(w):
    return pl.pallas_call(
        _pack_matrices_kernel,
        grid=(w.shape[0],),
        in_specs=[pl.BlockSpec((1, D_MODEL, D_MODEL), lambda r: (r, 0, 0))],
        out_specs=pl.BlockSpec((1, D_MODEL // 2, D_MODEL), lambda r: (r, 0, 0)),
        out_shape=jax.ShapeDtypeStruct((w.shape[0], D_MODEL // 2, D_MODEL), jnp.uint32),
        compiler_params=_params(("parallel",), 32),
        name="pack_matrices",
    )(w)


def _silu(x):
    return x * jax.nn.sigmoid(x)


def _rms(x, gain):
    ms = jnp.mean(x * x, axis=-1, keepdims=True)
    return x * lax.rsqrt(ms + EPS) * gain


def _const_spec(shape):
    zeros = (0,) * len(shape)
    return pl.BlockSpec(shape, lambda *_: zeros, pipeline_mode=pl.Buffered(1))


def _layer_spec(shape, layer):
    index = (layer,) + (0,) * len(shape)
    return pl.BlockSpec((1,) + tuple(shape), lambda *_: index, pipeline_mode=pl.Buffered(1))


def _drop_alias_refs(body, n_in, n_alias):
    def kern(*refs):
        return body(*refs[:n_in], *refs[n_in + n_alias:])
    return kern


def _kvproj_kernel(m_ref, wk_ref, wv_ref, k_ref, v_ref, kb_ref, vb_ref):
    m = m_ref[...].astype(BF)
    k = _dot(m, wk_ref[0])
    v = _dot(m, wv_ref[0])
    for h in range(XA_HEADS):
        cs = slice(h * XA_HEAD_DIM, (h + 1) * XA_HEAD_DIM)
        k_ref[0, 0, :, h, :] = k[:, cs]
        v_ref[0, 0, :, h, :] = v[:, cs]
    kb_ref[0] = k.astype(BF)
    vb_ref[0] = v.astype(BF)


def _kvproj(mem, wk, wv, batch, n_mem):
    depth = wk.shape[0]
    w_spec = pl.BlockSpec((1, D_MODEL, D_MODEL), lambda l, b: (l, 0, 0))
    out5 = pl.BlockSpec((1, 1, n_mem, XA_HEADS, XA_HEAD_DIM), lambda l, b: (l, b, 0, 0, 0))
    out_bf = pl.BlockSpec((1, n_mem, D_MODEL), lambda l, b: (l, b, 0))
    return pl.pallas_call(
        _kvproj_kernel,
        grid=(depth, batch),
        in_specs=[pl.BlockSpec((n_mem, D_MODEL), lambda l, b: (b, 0)), w_spec, w_spec],
        out_specs=[out5, out5, out_bf, out_bf],
        out_shape=[jax.ShapeDtypeStruct((depth, batch, n_mem, XA_HEADS, XA_HEAD_DIM), F32)] * 2
        + [jax.ShapeDtypeStruct((depth, batch * n_mem, D_MODEL), BF)] * 2,
        compiler_params=_params(("parallel", "parallel"), 32),
        name="kvproj",
    )(mem, wk, wv)


def _inproj_slab(h, w_ref, p_out, j, part=0, n_parts=1):
    width = D_MODEL // n_parts
    lo = part * width
    w = w_ref[0, :, j * D_MODEL + lo:j * D_MODEL + lo + width]
    p_out[j, :, lo:lo + width] = _dot(h, _unpack_rows(w)).astype(p_out.dtype)


def _inproj_gate_lowrank(h, walr_ref):
    return _dot_nt(h, walr_ref[...].astype(BF)).astype(BF)


def _inproj_gate(alr, wa2_ref, ba_ref, la_out):
    z = _dot(alr, wa2_ref[...]) + ba_ref[...]
    la_out[...] = (jnp.minimum(z, 0.0) - jnp.log(1.0 + jnp.exp(-jnp.abs(z)))) * (1.0 / GATE_TAU)


def _merge_term(br_ref, p, wbr_ref, n):
    return jax.nn.sigmoid(p[SLAB_MERGE + n].astype(F32)) * _dot(br_ref[n].astype(BF), _unpack_rows(wbr_ref[0, n]))


def _merge_finish(merged, x, wo_ref, fg_ref, final):
    x_new = x + _dot(merged.astype(BF), _unpack_rows(wo_ref[0]))
    return _rms(x_new, fg_ref[...]) if final else x_new


def _chunk_cumsum_matrix(tm):
    row = lax.broadcasted_iota(jnp.int32, (tm, tm), 0)
    col = lax.broadcasted_iota(jnp.int32, (tm, tm), 1)
    return (((row // CHUNK) == (col // CHUNK)) & (row >= col)).astype(BF)


def _window_matrices(tm):
    row = lax.broadcasted_iota(jnp.int32, (tm, tm), 0)
    col = lax.broadcasted_iota(jnp.int32, (tm, tm), 1)
    return jnp.stack([((row >= col) & (row - col < w)).astype(BF) for w in POOL_WINDOWS])


def _gla_cumdecay(cum_ref, la_ref):
    la = la_ref[...]
    la_hi = la.astype(BF)
    la_lo = (la - la_hi.astype(F32)).astype(BF)
    return _dot(cum_ref[...], la_hi) + _dot(cum_ref[...], la_lo)


def _gla_chunk(p, bcum, gg_ref, br_ref, st_ref, c, filler):
    crow = lax.broadcasted_iota(jnp.int32, (CHUNK, CHUNK), 0)
    ccol = lax.broadcasted_iota(jnp.int32, (CHUNK, CHUNK), 1)
    causal = crow >= ccol
    rows = slice(c * CHUNK, (c + 1) * CHUNK)
    heads = range(GLA_HEADS)
    q_dec, k_end, decay, att = [], [], [], []
    for h in heads:
        kc = slice(h * GLA_DK, (h + 1) * GLA_DK)
        kc2 = slice(GLA_KEY_WIDTH + h * GLA_DK, GLA_KEY_WIDTH + (h + 1) * GLA_DK)
        b = bcum[rows, kc]
        b_last = b[CHUNK - 1:CHUNK, :]
        q = p[SLAB_QK, rows, kc].astype(F32)
        k = p[SLAB_QK, rows, kc2].astype(F32)
        q_dec.append((q * (GLA_DK ** -0.5) * jnp.exp(b)).astype(BF))
        k_inv = (k * jnp.exp(-b)).astype(BF)
        k_end.append((k * jnp.exp(b_last - b)).astype(BF))
        decay.append(jnp.exp(b_last))
        att.append(_dot_nt(q_dec[h], k_inv))
    filler()
    o = []
    for h in heads:
        vc = slice(h * GLA_DV, (h + 1) * GLA_DV)
        a = jnp.where(causal, att[h], 0.0).astype(BF)
        o.append(_dot(a, p[SLAB_V, rows, vc]) + _dot_nt(q_dec[h], st_ref[h].astype(BF)))
    filler()
    for h in heads:
        vc = slice(h * GLA_DV, (h + 1) * GLA_DV)
        st_ref[h] = decay[h] * st_ref[h] + _dot_tn(p[SLAB_V, rows, vc], k_end[h])
    for h in heads:
        vc = slice(h * GLA_DV, (h + 1) * GLA_DV)
        g = p[SLAB_GLA_G, rows, vc].astype(F32)
        br_ref[0, rows, vc] = (_rms(o[h], gg_ref[:, vc]) * _silu(g)).astype(BF)


def _pool_branch(p, win_ref, pw_ref, ps_ref, br_ref, ubuf_ref, t, tm):
    u_bf = p[SLAB_U]
    u = u_bf.astype(F32)
    ubuf_ref[HIST_PAD:2 * HIST_PAD, :] = u[0:HIST_PAD]
    pos = t * tm + lax.broadcasted_iota(jnp.int32, (tm, 1), 0)
    for g, w in enumerate(POOL_WINDOWS):
        cs = slice(g * POOL_GROUP_DIM, (g + 1) * POOL_GROUP_DIM)
        ug = u[:, cs]
        head = ug[0:HIST_PAD]
        for j in range(1, w):
            head = head + ubuf_ref[HIST_PAD - j:2 * HIST_PAD - j, cs]
        s = jnp.concatenate([head, _dot(win_ref[g], u_bf[:, cs])[HIST_PAD:]], axis=0)
        cnt = jnp.minimum(w, pos + 1).astype(F32)
        diff = s / cnt - ug
        mixed = _dot(diff.astype(BF), pw_ref[g]) * ps_ref[:, cs]
        pg = p[SLAB_POOL_G, :, cs].astype(F32)
        br_ref[1, :, cs] = (mixed * _silu(pg)).astype(BF)
    ubuf_ref[0:HIST_PAD, :] = u[tm - HIST_PAD:tm]


def _xattn_probs(p, mk_ref):
    out = []
    for h in range(XA_HEADS):
        cs = slice(h * XA_HEAD_DIM, (h + 1) * XA_HEAD_DIM)
        s = _dot_nt(p[SLAB_XQ, :, cs], mk_ref[0, :, cs]) * (XA_HEAD_DIM ** -0.5)
        pr = jnp.exp(s - jnp.max(s, axis=-1, keepdims=True))
        out.append((pr.astype(BF), jnp.sum(pr, axis=-1, keepdims=True)))
    return out


def _xattn_branch(p, probs, mv_ref, br_ref):
    for h in range(XA_HEADS):
        cs = slice(h * XA_HEAD_DIM, (h + 1) * XA_HEAD_DIM)
        pr, denom = probs[h]
        o = _dot(pr, mv_ref[0, :, cs]) / denom
        xg = p[SLAB_XG, :, cs].astype(F32)
        br_ref[2, :, cs] = (o * _silu(xg)).astype(BF)


def _prompt_layer_kernel(xn_ref, xc_ref, g_ref, w_ref, walr_ref, wa2_ref, ba_ref, cum_ref, win_ref,
                         mk_ref, mv_ref, pw_ref, ps_ref, gg_ref, wbr_ref, wo_ref, fg_ref,
                         out_ref, s_out_ref, hist_out_ref,
                         h_scr, p_scr, la_scr, br_scr, st_ref, ubuf_ref, *, tm, nt, final):
    s = pl.program_id(0)
    t = jnp.maximum(s - 1, 0) % nt
    slot_w = s % 2
    slot_r = 1 - slot_w

    @pl.when(s == 0)
    def _():
        p_scr[1] = jnp.zeros(p_scr.shape[1:], p_scr.dtype)
        la_scr[1] = jnp.zeros(la_scr.shape[1:], la_scr.dtype)

    @pl.when(t == 0)
    def _():
        st_ref[...] = jnp.zeros_like(st_ref)
        ubuf_ref[0:HIST_PAD, :] = jnp.zeros((HIST_PAD, D_MODEL), F32)

    @pl.when(s == 0)
    def _():
        h_scr[0] = _rms(xc_ref[...], g_ref[...]).astype(BF)

    p_next = p_scr.at[slot_w]
    p = p_scr.at[slot_r]

    pieces = [(j, part) for j in range(N_SLABS) for part in range(PROJ_PARTS)]
    n_pieces = len(pieces)
    calls = [0]

    def filler():
        calls[0] += 1
        while pieces and (n_pieces - len(pieces) + 1) * FILLER_SITES <= calls[0] * n_pieces:
            j, part = pieces.pop(0)
            _inproj_slab(h_scr[slot_w], w_ref, p_next, j, part, PROJ_PARTS)

    alr = _inproj_gate_lowrank(h_scr[slot_w], walr_ref)
    bcum = _gla_cumdecay(cum_ref, la_scr.at[slot_r])
    filler()
    _inproj_gate(alr, wa2_ref, ba_ref, la_scr.at[slot_w])
    filler()
    probs = None
    for c in range(tm // CHUNK):
        _gla_chunk(p, bcum, gg_ref, br_scr, st_ref, c, filler)
        filler()
        if c == 0:
            probs = _xattn_probs(p, mk_ref)
            filler()
        elif c == 1:
            _xattn_branch(p, probs, mv_ref, br_scr)
            filler()
        elif c == 2:
            _pool_branch(p, win_ref, pw_ref, ps_ref, br_scr, ubuf_ref, t, tm)
            filler()
    h_scr[slot_r] = _rms(xn_ref[...], g_ref[...]).astype(BF)
    merged = _merge_term(br_scr, p, wbr_ref, 2)
    filler()
    merged = merged + _merge_term(br_scr, p, wbr_ref, 1)
    filler()
    merged = merged + _merge_term(br_scr, p, wbr_ref, 0)
    while pieces:
        filler()
    out_ref[...] = _merge_finish(merged, xc_ref[...], wo_ref, fg_ref, final)

    @pl.when((s > 0) & (t == nt - 1))
    def _():
        for h in range(GLA_HEADS):
            s_out_ref[0, 0, h] = st_ref[h].T
        hist_out_ref[0, 0] = ubuf_ref[1:HIST_PAD, :]


def _prompt_layer(x, ngain, w_proj, w_alr, wa2, ba, mk, mv, pool_w, pool_scale, gla_gain,
                  w_branch, w_out, fgain, carry, layer, depth, batch, seq, n_mem, tm, final):
    nt = seq // tm
    n_tiles = batch * nt
    assert tm // CHUNK >= 3, "the mixer stages are spread over the first three GLA chunks of a tile"
    n_alias = 0 if carry is None else len(carry)
    n_in = 17
    kern = _drop_alias_refs(functools.partial(_prompt_layer_kernel, tm=tm, nt=nt, final=final), n_in, n_alias)
    any_spec = pl.BlockSpec(memory_space=pl.ANY)

    def cur(s):
        return jnp.maximum(s - 1, 0)

    def seq_of(s):
        return cur(s) // nt

    return pl.pallas_call(
        kern,
        grid=(n_tiles + 1,),
        in_specs=[
            pl.BlockSpec((tm, D_MODEL), lambda s: (jnp.minimum(s + 1, n_tiles - 1), 0)),
            pl.BlockSpec((tm, D_MODEL), lambda s: (cur(s), 0)),
            _const_spec((1, D_MODEL)),
            _layer_spec((D_MODEL // 2, N_SLABS * D_MODEL), layer),
            _const_spec((GATE_RANK, D_MODEL)),
            _const_spec((GATE_RANK, GLA_KEY_WIDTH)),
            _const_spec((1, GLA_KEY_WIDTH)),
            _const_spec((tm, tm)),
            _const_spec((len(POOL_WINDOWS), tm, tm)),
            pl.BlockSpec((1, n_mem, D_MODEL), lambda s: (layer, seq_of(s), 0)),
            pl.BlockSpec((1, n_mem, D_MODEL), lambda s: (layer, seq_of(s), 0)),
            _const_spec((len(POOL_WINDOWS), POOL_GROUP_DIM, POOL_GROUP_DIM)),
            _const_spec((1, D_MODEL)),
            _const_spec((1, D_MODEL)),
            _layer_spec((N_BRANCH, D_MODEL // 2, D_MODEL), layer),
            _layer_spec((D_MODEL // 2, D_MODEL), layer),
            _const_spec((1, D_MODEL)),
        ] + [any_spec] * n_alias,
        out_specs=[
            pl.BlockSpec((tm, D_MODEL), lambda s: (cur(s), 0)),
            pl.BlockSpec((1, 1, GLA_HEADS, GLA_DK, GLA_DV), lambda s: (layer, seq_of(s), 0, 0, 0)),
            pl.BlockSpec((1, 1, POOL_HIST, D_MODEL), lambda s: (layer, seq_of(s), 0, 0)),
        ],
        out_shape=[
            jax.ShapeDtypeStruct((n_tiles * tm, D_MODEL), F32),
            jax.ShapeDtypeStruct((depth, batch, GLA_HEADS, GLA_DK, GLA_DV), F32),
            jax.ShapeDtypeStruct((depth, batch, POOL_HIST, D_MODEL), F32),
        ],
        scratch_shapes=[
            pltpu.VMEM((2, tm, D_MODEL), BF),
            pltpu.VMEM((2, N_SLABS, tm, D_MODEL), BF),
            pltpu.VMEM((2, tm, GLA_KEY_WIDTH), F32),
            pltpu.VMEM((N_BRANCH, tm, D_MODEL), BF),
            pltpu.VMEM((GLA_HEADS, GLA_DV, GLA_DK), F32),
            pltpu.VMEM((2 * HIST_PAD, D_MODEL), F32),
        ],
        input_output_aliases={n_in + a: 1 + a for a in range(n_alias)},
        compiler_params=_params(("arbitrary",), 60),
        name="prompt_layer",
    )(x, x, ngain, w_proj, w_alr, wa2, ba, _chunk_cumsum_matrix(tm), _window_matrices(tm),
      mk, mv, pool_w, pool_scale, gla_gain,
      w_branch, w_out, fgain, *(carry or ()))


def _inproj_kernel(x_ref, g_ref, w_ref, walr_ref, wa2_ref, ba_ref, p_ref, la_ref):
    h = _rms(x_ref[...], g_ref[...]).astype(BF)
    alr = _inproj_gate_lowrank(h, walr_ref)
    for j in range(N_SLABS):
        _inproj_slab(h, w_ref, p_ref, j)
    _inproj_gate(alr, wa2_ref, ba_ref, la_ref)


def _inproj(x, gain, w_proj, w_alr, w_a2, b_a, layer, tm, out_dtype):
    m_rows = x.shape[0]
    return pl.pallas_call(
        _inproj_kernel,
        grid=(m_rows // tm,),
        in_specs=[
            pl.BlockSpec((tm, D_MODEL), lambda i: (i, 0)),
            _const_spec((1, D_MODEL)),
            _layer_spec((D_MODEL // 2, N_SLABS * D_MODEL), layer),
            _const_spec((GATE_RANK, D_MODEL)),
            _const_spec((GATE_RANK, GLA_KEY_WIDTH)),
            _const_spec((1, GLA_KEY_WIDTH)),
        ],
        out_specs=[
            pl.BlockSpec((N_SLABS, tm, D_MODEL), lambda i: (0, i, 0)),
            pl.BlockSpec((tm, GLA_KEY_WIDTH), lambda i: (i, 0)),
        ],
        out_shape=[
            jax.ShapeDtypeStruct((N_SLABS, m_rows, D_MODEL), out_dtype),
            jax.ShapeDtypeStruct((m_rows, GLA_KEY_WIDTH), F32),
        ],
        compiler_params=_params(("parallel",), 48),
        name="inproj",
    )(x, gain, w_proj, w_alr, w_a2, b_a)


def _sample_mix_kernel(p_ref, la_ref, s0_ref, hist_ref, ck_ref, cv_ref, pw_ref, ps_ref, gg_ref,
                       br_ref, s_out_ref, hist_out_ref, diff_ref, *, sb):
    r0 = pl.program_id(1) * sb
    erow = lax.broadcasted_iota(jnp.int32, (GLA_DK, GLA_DK), 0)
    ecol = lax.broadcasted_iota(jnp.int32, (GLA_DK, GLA_DK), 1)
    eye = erow == ecol

    def to_col(x):
        return jnp.sum(jnp.where(eye, jnp.broadcast_to(x, (GLA_DK, GLA_DK)), 0.0), axis=1, keepdims=True)

    for i in range(sb):
        r = pl.ds(r0 + i, 1)
        la = la_ref[r, :]
        qk = p_ref[SLAB_QK, r, :]
        vv = p_ref[SLAB_V, r, :]
        gla_g = p_ref[SLAB_GLA_G, r, :]
        for h in range(GLA_HEADS):
            kc = slice(h * GLA_DK, (h + 1) * GLA_DK)
            kc2 = slice(GLA_KEY_WIDTH + h * GLA_DK, GLA_KEY_WIDTH + (h + 1) * GLA_DK)
            vc = slice(h * GLA_DV, (h + 1) * GLA_DV)
            a_col = to_col(jnp.exp(la[:, kc]))
            q_col = to_col(qk[:, kc] * (GLA_DK ** -0.5))
            k_col = to_col(qk[:, kc2])
            s_new = a_col * s0_ref[0, i, h] + k_col * vv[:, vc]
            s_out_ref[0, i, h] = s_new
            o = jnp.sum(q_col * s_new, axis=0, keepdims=True)
            br_ref[0, r, vc] = _rms(o, gg_ref[:, vc]) * _silu(gla_g[:, vc])

        u = p_ref[SLAB_U, r, :]
        for g, w in enumerate(POOL_WINDOWS):
            cs = slice(g * POOL_GROUP_DIM, (g + 1) * POOL_GROUP_DIM)
            past = jnp.sum(hist_ref[0, i, POOL_HIST - (w - 1):POOL_HIST, cs], axis=0, keepdims=True)
            diff_ref[r, cs] = (u[:, cs] + past) / float(w) - u[:, cs]
        hist_out_ref[0, i, 0:POOL_HIST - 1, :] = hist_ref[0, i, 1:POOL_HIST, :]
        hist_out_ref[0, i, POOL_HIST - 1:POOL_HIST, :] = u

        xq = p_ref[SLAB_XQ, r, :]
        xg = p_ref[SLAB_XG, r, :]
        half_cols = [slice(h * XA_HEAD_DIM + j * LANES, h * XA_HEAD_DIM + (j + 1) * LANES)
                     for j in range(XA_HEAD_DIM // LANES) for h in range(XA_HEADS)]
        xq_rows = jnp.concatenate([xq[:, cs] for cs in half_cols], axis=0)
        part = jnp.sum(ck_ref[0, i] * xq_rows[None], axis=-1, keepdims=True)
        s = (part + pltpu.roll(part, XA_HEADS, axis=1)) * (XA_HEAD_DIM ** -0.5)
        p = jnp.exp(s - jnp.max(s, axis=0, keepdims=True))
        o = jnp.sum(p * cv_ref[0, i], axis=0) / jnp.sum(p, axis=0)
        halves = XA_HEAD_DIM // LANES
        o_row = jnp.concatenate([o[j * XA_HEADS + h:j * XA_HEADS + h + 1, :]
                                 for h in range(XA_HEADS) for j in range(halves)], axis=1)
        br_ref[2, r, :] = o_row * _silu(xg)

    @pl.when(pl.program_id(1) == pl.num_programs(1) - 1)
    def _():
        for g in range(len(POOL_WINDOWS)):
            cs = slice(g * POOL_GROUP_DIM, (g + 1) * POOL_GROUP_DIM)
            mixed = _dot(diff_ref[:, cs].astype(BF), pw_ref[g]) * ps_ref[:, cs]
            br_ref[1, :, cs] = mixed * _silu(p_ref[SLAB_POOL_G, :, cs])


def _cache_rows_view(c):
    depth, nb, n_mem = c.shape[:3]
    halves = XA_HEAD_DIM // LANES
    c = c.reshape(depth, nb, n_mem, XA_HEADS, halves, LANES)
    return c.transpose(0, 1, 2, 4, 3, 5).reshape(depth, nb, n_mem, halves * XA_HEADS, LANES)


def _sample_mix(p, la, s0, hist, ck, cv, pool_w, pool_scale, gla_gain, carry, layer, sb=4):
    nb = s0.shape[1]
    n_mem = ck.shape[2]
    ck, cv = _cache_rows_view(ck), _cache_rows_view(cv)
    rb = SUBLANES
    halves = rb // sb
    n_alias = 0 if carry is None else len(carry)
    n_in = 9
    kern = _drop_alias_refs(functools.partial(_sample_mix_kernel, sb=sb), n_in, n_alias)
    any_spec = pl.BlockSpec(memory_space=pl.ANY)
    return pl.pallas_call(
        kern,
        grid=(nb // rb, halves),
        in_specs=[
            pl.BlockSpec((N_MIX_SLABS, rb, D_MODEL), lambda i, j: (0, i, 0)),
            pl.BlockSpec((rb, GLA_KEY_WIDTH), lambda i, j: (i, 0)),
            pl.BlockSpec((1, sb, GLA_HEADS, GLA_DK, GLA_DV), lambda i, j: (layer, i * halves + j, 0, 0, 0)),
            pl.BlockSpec((1, sb, POOL_HIST, D_MODEL), lambda i, j: (layer, i * halves + j, 0, 0)),
            pl.BlockSpec((1, sb, n_mem, SUBLANES, LANES), lambda i, j: (layer, i * halves + j, 0, 0, 0)),
            pl.BlockSpec((1, sb, n_mem, SUBLANES, LANES), lambda i, j: (layer, i * halves + j, 0, 0, 0)),
            _const_spec((len(POOL_WINDOWS), POOL_GROUP_DIM, POOL_GROUP_DIM)),
            _const_spec((1, D_MODEL)),
            _const_spec((1, D_MODEL)),
        ] + [any_spec] * n_alias,
        out_specs=[
            pl.BlockSpec((N_BRANCH, rb, D_MODEL), lambda i, j: (0, i, 0)),
            pl.BlockSpec((1, sb, GLA_HEADS, GLA_DK, GLA_DV), lambda i, j: (layer, i * halves + j, 0, 0, 0)),
            pl.BlockSpec((1, sb, POOL_HIST, D_MODEL), lambda i, j: (layer, i * halves + j, 0, 0)),
        ],
        out_shape=[
            jax.ShapeDtypeStruct((N_BRANCH, nb, D_MODEL), F32),
            jax.ShapeDtypeStruct(s0.shape, F32),
            jax.ShapeDtypeStruct(hist.shape, F32),
        ],
        scratch_shapes=[pltpu.VMEM((rb, D_MODEL), F32)],
        input_output_aliases={n_in + a: 1 + a for a in range(n_alias)},
        compiler_params=_params(("parallel", "arbitrary"), 40),
        name="sample_mix",
    )(p, la, s0, hist, ck, cv, pool_w, pool_scale, gla_gain, *(carry or ()))


def _merge_kernel(br_ref, p_ref, x_ref, wbr_ref, wo_ref, fg_ref, out_ref, *, final):
    merged = _merge_term(br_ref, p_ref, wbr_ref, 0)
    for n in range(1, N_BRANCH):
        merged = merged + _merge_term(br_ref, p_ref, wbr_ref, n)
    out_ref[...] = _merge_finish(merged, x_ref[...], wo_ref, fg_ref, final)


def _merge_out(br, p, x, w_branch, w_out, final_gain, layer, final):
    m_rows = x.shape[0]
    whole = lambda shape: pl.BlockSpec(shape, lambda i: (0,) * len(shape))
    return pl.pallas_call(
        functools.partial(_merge_kernel, final=final),
        grid=(1,),
        in_specs=[
            whole((N_BRANCH, m_rows, D_MODEL)),
            whole((N_SLABS, m_rows, D_MODEL)),
            whole((m_rows, D_MODEL)),
            _layer_spec((N_BRANCH, D_MODEL // 2, D_MODEL), layer),
            _layer_spec((D_MODEL // 2, D_MODEL), layer),
            _const_spec((1, D_MODEL)),
        ],
        out_specs=whole((m_rows, D_MODEL)),
        out_shape=jax.ShapeDtypeStruct((m_rows, D_MODEL), F32),
        compiler_params=_params(("arbitrary",), 40),
        name="merge_out",
    )(br, p, x, w_branch, w_out, final_gain)


def kernel(x_prompt, x_sample, mem_prompt, cache_mem_k, cache_mem_v, state_gla, state_pool, w_in, w_a2, b_a, gla_gain, pool_w, pool_scale, w_mk, w_mv, w_branch, w_out, norm_gain, final_gain):
    batch, seq, _ = x_prompt.shape
    nb = x_sample.shape[0]
    n_mem = mem_prompt.shape[1]
    depth = w_in.shape[0]
    tm = min(256, seq)

    xp = x_prompt.reshape(batch * seq, D_MODEL)
    xs = x_sample.reshape(nb, D_MODEL)
    mem = mem_prompt.reshape(batch * n_mem, D_MODEL)
    fgain = final_gain.reshape(1, D_MODEL)

    mk, mv, mk_bf, mv_bf = _kvproj(mem, w_mk.astype(BF), w_mv.astype(BF), batch, n_mem)

    w_in_t = jnp.swapaxes(w_in, 1, 2)
    w_proj = _pack_w_in(w_in_t)
    wb = _pack_matrices(w_branch.reshape(depth * N_BRANCH, D_MODEL, D_MODEL))
    wb = wb.reshape(depth, N_BRANCH, D_MODEL // 2, D_MODEL)
    wo = _pack_matrices(w_out)

    carry_p, carry_s = None, None
    for l in range(depth):
        final = l == depth - 1
        w_alr = w_in_t[l, ALR_START:ALR_START + GATE_RANK, :]
        wa2 = w_a2[l].astype(BF)
        ba = b_a[l].reshape(1, GLA_KEY_WIDTH)
        ngain = norm_gain[l].reshape(1, D_MODEL)
        ggain = gla_gain[l].reshape(1, D_MODEL)
        pscale = pool_scale[l].reshape(1, D_MODEL)
        pw = pool_w[l].astype(BF)

        xp, s_all, hist_all = _prompt_layer(xp, ngain, w_proj, w_alr, wa2, ba, mk_bf, mv_bf, pw, pscale,
                                            ggain, wb, wo, fgain, carry_p, l, depth, batch, seq, n_mem, tm, final)
        carry_p = (s_all, hist_all)

        ps, las = _inproj(xs, ngain, w_proj, w_alr, wa2, ba, l, nb, F32)
        brs, s_new, hist_new = _sample_mix(ps, las, state_gla, state_pool, cache_mem_k, cache_mem_v,
                                           pw, pscale, ggain, carry_s, l)
        carry_s = (s_new, hist_new)
        xs = _merge_out(brs, ps, xs, wb, wo, fgain, l, final)

    return (xp.reshape(batch, seq, D_MODEL), xs.reshape(nb, 1, D_MODEL),
            mk, mv, carry_p[0], carry_p[1], carry_s[0], carry_s[1])
```

```python
import functools

import jax
import jax.numpy as jnp
from jax import lax
from jax.experimental import pallas as pl
from jax.experimental.pallas import tpu as pltpu

D_MODEL = 1024
GLA_HEADS = 4
GLA_DK = 128
GLA_DV = 256
GLA_KEY_WIDTH = GLA_HEADS * GLA_DK
GATE_RANK = 16
GATE_TAU = 16.0
CHUNK = 64
POOL_WINDOWS = (2, 4, 8, 16)
POOL_GROUP_DIM = 256
POOL_HIST = 15
HIST_PAD = 16
XA_HEADS = 4
XA_HEAD_DIM = 256
N_BRANCH = 3
EPS = 1e-6
SUBLANES = 8
LANES = 128

SLAB_QK, SLAB_V, SLAB_GLA_G, SLAB_U, SLAB_POOL_G, SLAB_XQ, SLAB_XG, SLAB_MERGE = 0, 1, 2, 3, 4, 5, 6, 7
N_SLABS = 10
N_MIX_SLABS = 7
N_HEAD_SLABS = 3
ALR_START = N_HEAD_SLABS * D_MODEL
FILLER_SITES = 20
PROJ_PARTS = 1

BF = jnp.bfloat16
F32 = jnp.float32
MIB = 1 << 20


def _dot(a, b):
    return jnp.dot(a, b, preferred_element_type=F32)


def _dot_nt(a, b):
    return lax.dot_general(a, b, (((1,), (1,)), ((), ())), preferred_element_type=F32)


def _dot_tn(a, b):
    return lax.dot_general(a, b, (((0,), (0,)), ((), ())), preferred_element_type=F32)


def _pack_rows(w):
    return pltpu.bitcast(w.astype(BF), jnp.uint32)


def _unpack_rows(w_words):
    return pltpu.bitcast(w_words, BF)


def _params(sem, vmem_mib):
    return pltpu.CompilerParams(dimension_semantics=sem, vmem_limit_bytes=vmem_mib * MIB)


def _pack_w_in_kernel(a_ref, b_ref, o_ref):
    j = pl.program_id(1)

    @pl.when(j < N_HEAD_SLABS)
    def _():
        o_ref[0] = _pack_rows(a_ref[0].T)

    @pl.when(j >= N_HEAD_SLABS)
    def _():
        o_ref[0] = _pack_rows(jnp.concatenate([a_ref[0, GATE_RANK:, :], b_ref[0]], axis=0).T)


def _pack_w_in(w_in_t):
    depth = w_in_t.shape[0]
    return pl.pallas_call(
        _pack_w_in_kernel,
        grid=(depth, N_SLABS),
        in_specs=[
            pl.BlockSpec((1, D_MODEL, D_MODEL), lambda l, j: (l, j, 0)),
            pl.BlockSpec((1, GATE_RANK, D_MODEL), lambda l, j: (l, (j + 1) * (D_MODEL // GATE_RANK), 0)),
        ],
        out_specs=pl.BlockSpec((1, D_MODEL // 2, D_MODEL), lambda l, j: (l, 0, j)),
        out_shape=jax.ShapeDtypeStruct((depth, D_MODEL // 2, N_SLABS * D_MODEL), jnp.uint32),
        compiler_params=_params(("parallel", "parallel"), 32),
        name="pack_w_in",
    )(w_in_t, w_in_t)


def _pack_matrices_kernel(a_ref, o_ref):
    o_ref[0] = _pack_rows(a_ref[0])


def _pack_matrices(w):
    return pl.pallas_call(
        _pack_matrices_kernel,
        grid=(w.shape[0],),
        in_specs=[pl.BlockSpec((1, D_MODEL, D_MODEL), lambda r: (r, 0, 0))],
        out_specs=pl.BlockSpec((1, D_MODEL // 2, D_MODEL), lambda r: (r, 0, 0)),
        out_shape=jax.ShapeDtypeStruct((w.shape[0], D_MODEL // 2, D_MODEL), jnp.uint32),
        compiler_params=_params(("parallel",), 32),
        name="pack_matrices",
    )(w)


def _silu(x):
    return x * jax.nn.sigmoid(x)


def _rms(x, gain):
    ms = jnp.mean(x * x, axis=-1, keepdims=True)
    return x * lax.rsqrt(ms + EPS) * gain


def _const_spec(shape):
    zeros = (0,) * len(shape)
    return pl.BlockSpec(shape, lambda *_: zeros, pipeline_mode=pl.Buffered(1))


def _layer_spec(shape, layer):
    index = (layer,) + (0,) * len(shape)
    return pl.BlockSpec((1,) + tuple(shape), lambda *_: index, pipeline_mode=pl.Buffered(1))


def _drop_alias_refs(body, n_in, n_alias):
    def kern(*refs):
        return body(*refs[:n_in], *refs[n_in + n_alias:])
    return kern


def _kvproj_kernel(m_ref, wk_ref, wv_ref, k_ref, v_ref, kb_ref, vb_ref):
    m = m_ref[...].astype(BF)
    k = _dot(m, wk_ref[0])
    v = _dot(m, wv_ref[0])
    for h in range(XA_HEADS):
        cs = slice(h * XA_HEAD_DIM, (h + 1) * XA_HEAD_DIM)
        k_ref[0, 0, :, h, :] = k[:, cs]
        v_ref[0, 0, :, h, :] = v[:, cs]
    kb_ref[0] = k.astype(BF)
    vb_ref[0] = v.astype(BF)


def _kvproj(mem, wk, wv, batch, n_mem):
    depth = wk.shape[0]
    w_spec = pl.BlockSpec((1, D_MODEL, D_MODEL), lambda l, b: (l, 0, 0))
    out5 = pl.BlockSpec((1, 1, n_mem, XA_HEADS, XA_HEAD_DIM), lambda l, b: (l, b, 0, 0, 0))
    out_bf = pl.BlockSpec((1, n_mem, D_MODEL), lambda l, b: (l, b, 0))
    return pl.pallas_call(
        _kvproj_kernel,
        grid=(depth, batch),
        in_specs=[pl.BlockSpec((n_mem, D_MODEL), lambda l, b: (b, 0)), w_spec, w_spec],
        out_specs=[out5, out5, out_bf, out_bf],
        out_shape=[jax.ShapeDtypeStruct((depth, batch, n_mem, XA_HEADS, XA_HEAD_DIM), F32)] * 2
        + [jax.ShapeDtypeStruct((depth, batch * n_mem, D_MODEL), BF)] * 2,
        compiler_params=_params(("parallel", "parallel"), 32),
        name="kvproj",
    )(mem, wk, wv)


def _inproj_slab(h, w_ref, p_out, j, part=0, n_parts=1):
    width = D_MODEL // n_parts
    lo = part * width
    w = w_ref[0, :, j * D_MODEL + lo:j * D_MODEL + lo + width]
    p_out[j, :, lo:lo + width] = _dot(h, _unpack_rows(w)).astype(p_out.dtype)


def _inproj_gate_lowrank(h, walr_ref):
    return _dot_nt(h, walr_ref[...].astype(BF)).astype(BF)


def _inproj_gate(alr, wa2_ref, ba_ref, la_out):
    z = _dot(alr, wa2_ref[...]) + ba_ref[...]
    la_out[...] = (jnp.minimum(z, 0.0) - jnp.log(1.0 + jnp.exp(-jnp.abs(z)))) * (1.0 / GATE_TAU)


def _merge_term(br_ref, p, wbr_ref, n):
    return jax.nn.sigmoid(p[SLAB_MERGE + n].astype(F32)) * _dot(br_ref[n].astype(BF), _unpack_rows(wbr_ref[0, n]))


def _merge_finish(merged, x, wo_ref, fg_ref, final):
    x_new = x + _dot(merged.astype(BF), _unpack_rows(wo_ref[0]))
    return _rms(x_new, fg_ref[...]) if final else x_new


def _chunk_cumsum_matrix(tm):
    row = lax.broadcasted_iota(jnp.int32, (tm, tm), 0)
    col = lax.broadcasted_iota(jnp.int32, (tm, tm), 1)
    return (((row // CHUNK) == (col // CHUNK)) & (row >= col)).astype(BF)


def _window_matrices(tm):
    row = lax.broadcasted_iota(jnp.int32, (tm, tm), 0)
    col = lax.broadcasted_iota(jnp.int32, (tm, tm), 1)
    return jnp.stack([((row >= col) & (row - col < w)).astype(BF) for w in POOL_WINDOWS])


def _gla_cumdecay(cum_ref, la_ref):
    la = la_ref[...]
    la_hi = la.astype(BF)
    la_lo = (la - la_hi.astype(F32)).astype(BF)
    return _dot(cum_ref[...], la_hi) + _dot(cum_ref[...], la_lo)


def _gla_chunk(p, bcum, gg_ref, br_ref, st_ref, c, filler):
    crow = lax.broadcasted_iota(jnp.int32, (CHUNK, CHUNK), 0)
    ccol = lax.broadcasted_iota(jnp.int32, (CHUNK, CHUNK), 1)
    causal = crow >= ccol
    rows = slice(c * CHUNK, (c + 1) * CHUNK)
    heads = range(GLA_HEADS)
    q_dec, k_end, decay, att = [], [], [], []
    for h in heads:
        kc = slice(h * GLA_DK, (h + 1) * GLA_DK)
        kc2 = slice(GLA_KEY_WIDTH + h * GLA_DK, GLA_KEY_WIDTH + (h + 1) * GLA_DK)
        b = bcum[rows, kc]
        b_last = b[CHUNK - 1:CHUNK, :]
        q = p[SLAB_QK, rows, kc].astype(F32)
        k = p[SLAB_QK, rows, kc2].astype(F32)
        q_dec.append((q * (GLA_DK ** -0.5) * jnp.exp(b)).astype(BF))
        k_inv = (k * jnp.exp(-b)).astype(BF)
        k_end.append((k * jnp.exp(b_last - b)).astype(BF))
        decay.append(jnp.exp(b_last))
        att.append(_dot_nt(q_dec[h], k_inv))
    filler()
    o = []
    for h in heads:
        vc = slice(h * GLA_DV, (h + 1) * GLA_DV)
        a = jnp.where(causal, att[h], 0.0).astype(BF)
        o.append(_dot(a, p[SLAB_V, rows, vc]) + _dot_nt(q_dec[h], st_ref[h].astype(BF)))
    filler()
    for h in heads:
        vc = slice(h * GLA_DV, (h + 1) * GLA_DV)
        st_ref[h] = decay[h] * st_ref[h] + _dot_tn(p[SLAB_V, rows, vc], k_end[h])
    for h in heads:
        vc = slice(h * GLA_DV, (h + 1) * GLA_DV)
        g = p[SLAB_GLA_G, rows, vc].astype(F32)
        br_ref[0, rows, vc] = (_rms(o[h], gg_ref[:, vc]) * _silu(g)).astype(BF)


def _pool_branch(p, win_ref, pw_ref, ps_ref, br_ref, ubuf_ref, t, tm):
    u_bf = p[SLAB_U]
    u = u_bf.astype(F32)
    ubuf_ref[HIST_PAD:2 * HIST_PAD, :] = u[0:HIST_PAD]
    pos = t * tm + lax.broadcasted_iota(jnp.int32, (tm, 1), 0)
    for g, w in enumerate(POOL_WINDOWS):
        cs = slice(g * POOL_GROUP_DIM, (g + 1) * POOL_GROUP_DIM)
        ug = u[:, cs]
        head = ug[0:HIST_PAD]
        for j in range(1, w):
            head = head + ubuf_ref[HIST_PAD - j:2 * HIST_PAD - j, cs]
        s = jnp.concatenate([head, _dot(win_ref[g], u_bf[:, cs])[HIST_PAD:]], axis=0)
        cnt = jnp.minimum(w, pos + 1).astype(F32)
        diff = s / cnt - ug
        mixed = _dot(diff.astype(BF), pw_ref[g]) * ps_ref[:, cs]
        pg = p[SLAB_POOL_G, :, cs].astype(F32)
        br_ref[1, :, cs] = (mixed * _silu(pg)).astype(BF)
    ubuf_ref[0:HIST_PAD, :] = u[tm - HIST_PAD:tm]


def _xattn_probs(p, mk_ref):
    out = []
    for h in range(XA_HEADS):
        cs = slice(h * XA_HEAD_DIM, (h + 1) * XA_HEAD_DIM)
        s = _dot_nt(p[SLAB_XQ, :, cs], mk_ref[0, :, cs]) * (XA_HEAD_DIM ** -0.5)
        pr = jnp.exp(s - jnp.max(s, axis=-1, keepdims=True))
        out.append((pr.astype(BF), jnp.sum(pr, axis=-1, keepdims=True)))
    return out


def _xattn_branch(p, probs, mv_ref, br_ref):
    for h in range(XA_HEADS):
        cs = slice(h * XA_HEAD_DIM, (h + 1) * XA_HEAD_DIM)
        pr, denom = probs[h]
        o = _dot(pr, mv_ref[0, :, cs]) / denom
        xg = p[SLAB_XG, :, cs].astype(F32)
        br_ref[2, :, cs] = (o * _silu(xg)).astype(BF)


def _prompt_layer_kernel(xn_ref, xc_ref, g_ref, w_ref, walr_ref, wa2_ref, ba_ref, cum_ref, win_ref,
                         mk_ref, mv_ref, pw_ref, ps_ref, gg_ref, wbr_ref, wo_ref, fg_ref,
                         out_ref, s_out_ref, hist_out_ref,
                         h_scr, p_scr, la_scr, br_scr, st_ref, ubuf_ref, *, tm, nt, final):
    s = pl.program_id(0)
    t = jnp.maximum(s - 1, 0) % nt
    slot_w = s % 2
    slot_r = 1 - slot_w

    @pl.when(s == 0)
    def _():
        p_scr[1] = jnp.zeros(p_scr.shape[1:], p_scr.dtype)
        la_scr[1] = jnp.zeros(la_scr.shape[1:], la_scr.dtype)

    @pl.when(t == 0)
    def _():
        st_ref[...] = jnp.zeros_like(st_ref)
        ubuf_ref[0:HIST_PAD, :] = jnp.zeros((HIST_PAD, D_MODEL), F32)

    @pl.when(s == 0)
    def _():
        h_scr[0] = _rms(xc_ref[...], g_ref[...]).astype(BF)

    p_next = p_scr.at[slot_w]
    p = p_scr.at[slot_r]

    pieces = [(j, part) for j in range(N_SLABS) for part in range(PROJ_PARTS)]
    n_pieces = len(pieces)
    calls = [0]

    def filler():
        calls[0] += 1
        while pieces and (n_pieces - len(pieces) + 1) * FILLER_SITES <= calls[0] * n_pieces:
            j, part = pieces.pop(0)
            _inproj_slab(h_scr[slot_w], w_ref, p_next, j, part, PROJ_PARTS)

    alr = _inproj_gate_lowrank(h_scr[slot_w], walr_ref)
    bcum = _gla_cumdecay(cum_ref, la_scr.at[slot_r])
    filler()
    _inproj_gate(alr, wa2_ref, ba_ref, la_scr.at[slot_w])
    filler()
    probs = None
    for c in range(tm // CHUNK):
        _gla_chunk(p, bcum, gg_ref, br_scr, st_ref, c, filler)
        filler()
        if c == 0:
            probs = _xattn_probs(p, mk_ref)
            filler()
        elif c == 1:
            _xattn_branch(p, probs, mv_ref, br_scr)
            filler()
        elif c == 2:
            _pool_branch(p, win_ref, pw_ref, ps_ref, br_scr, ubuf_ref, t, tm)
            filler()
    h_scr[slot_r] = _rms(xn_ref[...], g_ref[...]).astype(BF)
    merged = _merge_term(br_scr, p, wbr_ref, 2)
    filler()
    merged = merged + _merge_term(br_scr, p, wbr_ref, 1)
    filler()
    merged = merged + _merge_term(br_scr, p, wbr_ref, 0)
    while pieces:
        filler()
    out_ref[...] = _merge_finish(merged, xc_ref[...], wo_ref, fg_ref, final)

    @pl.when((s > 0) & (t == nt - 1))
    def _():
        for h in range(GLA_HEADS):
            s_out_ref[0, 0, h] = st_ref[h].T
        hist_out_ref[0, 0] = ubuf_ref[1:HIST_PAD, :]


def _prompt_layer(x, ngain, w_proj, w_alr, wa2, ba, mk, mv, pool_w, pool_scale, gla_gain,
                  w_branch, w_out, fgain, carry, layer, depth, batch, seq, n_mem, tm, final):
    nt = seq // tm
    n_tiles = batch * nt
    assert tm // CHUNK >= 3, "the mixer stages are spread over the first three GLA chunks of a tile"
    n_alias = 0 if carry is None else len(carry)
    n_in = 17
    kern = _drop_alias_refs(functools.partial(_prompt_layer_kernel, tm=tm, nt=nt, final=final), n_in, n_alias)
    any_spec = pl.BlockSpec(memory_space=pl.ANY)

    def cur(s):
        return jnp.maximum(s - 1, 0)

    def seq_of(s):
        return cur(s) // nt

    return pl.pallas_call(
        kern,
        grid=(n_tiles + 1,),
        in_specs=[
            pl.BlockSpec((tm, D_MODEL), lambda s: (jnp.minimum(s + 1, n_tiles - 1), 0)),
            pl.BlockSpec((tm, D_MODEL), lambda s: (cur(s), 0)),
            _const_spec((1, D_MODEL)),
            _layer_spec((D_MODEL // 2, N_SLABS * D_MODEL), layer),
            _const_spec((GATE_RANK, D_MODEL)),
            _const_spec((GATE_RANK, GLA_KEY_WIDTH)),
            _const_spec((1, GLA_KEY_WIDTH)),
            _const_spec((tm, tm)),
            _const_spec((len(POOL_WINDOWS), tm, tm)),
            pl.BlockSpec((1, n_mem, D_MODEL), lambda s: (layer, seq_of(s), 0)),
            pl.BlockSpec((1, n_mem, D_MODEL), lambda s: (layer, seq_of(s), 0)),
            _const_spec((len(POOL_WINDOWS), POOL_GROUP_DIM, POOL_GROUP_DIM)),
            _const_spec((1, D_MODEL)),
            _const_spec((1, D_MODEL)),
            _layer_spec((N_BRANCH, D_MODEL // 2, D_MODEL), layer),
            _layer_spec((D_MODEL // 2, D_MODEL), layer),
            _const_spec((1, D_MODEL)),
        ] + [any_spec] * n_alias,
        out_specs=[
            pl.BlockSpec((tm, D_MODEL), lambda s: (cur(s), 0)),
            pl.BlockSpec((1, 1, GLA_HEADS, GLA_DK, GLA_DV), lambda s: (layer, seq_of(s), 0, 0, 0)),
            pl.BlockSpec((1, 1, POOL_HIST, D_MODEL), lambda s: (layer, seq_of(s), 0, 0)),
        ],
        out_shape=[
            jax.ShapeDtypeStruct((n_tiles * tm, D_MODEL), F32),
            jax.ShapeDtypeStruct((depth, batch, GLA_HEADS, GLA_DK, GLA_DV), F32),
            jax.ShapeDtypeStruct((depth, batch, POOL_HIST, D_MODEL), F32),
        ],
        scratch_shapes=[
            pltpu.VMEM((2, tm, D_MODEL), BF),
            pltpu.VMEM((2, N_SLABS, tm, D_MODEL), BF),
            pltpu.VMEM((2, tm, GLA_KEY_WIDTH), F32),
            pltpu.VMEM((N_BRANCH, tm, D_MODEL), BF),
            pltpu.VMEM((GLA_HEADS, GLA_DV, GLA_DK), F32),
            pltpu.VMEM((2 * HIST_PAD, D_MODEL), F32),
        ],
        input_output_aliases={n_in + a: 1 + a for a in range(n_alias)},
        compiler_params=_params(("arbitrary",), 60),
        name="prompt_layer",
    )(x, x, ngain, w_proj, w_alr, wa2, ba, _chunk_cumsum_matrix(tm), _window_matrices(tm),
      mk, mv, pool_w, pool_scale, gla_gain,
      w_branch, w_out, fgain, *(carry or ()))


def _inproj_kernel(x_ref, g_ref, w_ref, walr_ref, wa2_ref, ba_ref, p_ref, la_ref):
    h = _rms(x_ref[...], g_ref[...]).astype(BF)
    alr = _inproj_gate_lowrank(h, walr_ref)
    for j in range(N_SLABS):
        _inproj_slab(h, w_ref, p_ref, j)
    _inproj_gate(alr, wa2_ref, ba_ref, la_ref)


def _inproj(x, gain, w_proj, w_alr, w_a2, b_a, layer, tm, out_dtype):
    m_rows = x.shape[0]
    return pl.pallas_call(
        _inproj_kernel,
        grid=(m_rows // tm,),
        in_specs=[
            pl.BlockSpec((tm, D_MODEL), lambda i: (i, 0)),
            _const_spec((1, D_MODEL)),
            _layer_spec((D_MODEL // 2, N_SLABS * D_MODEL), layer),
            _const_spec((GATE_RANK, D_MODEL)),
            _const_spec((GATE_RANK, GLA_KEY_WIDTH)),
            _const_spec((1, GLA_KEY_WIDTH)),
        ],
        out_specs=[
            pl.BlockSpec((N_SLABS, tm, D_MODEL), lambda i: (0, i, 0)),
            pl.BlockSpec((tm, GLA_KEY_WIDTH), lambda i: (i, 0)),
        ],
        out_shape=[
            jax.ShapeDtypeStruct((N_SLABS, m_rows, D_MODEL), out_dtype),
            jax.ShapeDtypeStruct((m_rows, GLA_KEY_WIDTH), F32),
        ],
        compiler_params=_params(("parallel",), 48),
        name="inproj",
    )(x, gain, w_proj, w_alr, w_a2, b_a)


def _sample_mix_kernel(p_ref, la_ref, s0_ref, hist_ref, ck_ref, cv_ref, pw_ref, ps_ref, gg_ref,
                       br_ref, s_out_ref, hist_out_ref, diff_ref, *, sb):
    r0 = pl.program_id(1) * sb
    erow = lax.broadcasted_iota(jnp.int32, (GLA_DK, GLA_DK), 0)
    ecol = lax.broadcasted_iota(jnp.int32, (GLA_DK, GLA_DK), 1)
    eye = erow == ecol

    def to_col(x):
        return jnp.sum(jnp.where(eye, jnp.broadcast_to(x, (GLA_DK, GLA_DK)), 0.0), axis=1, keepdims=True)

    for i in range(sb):
        r = pl.ds(r0 + i, 1)
        la = la_ref[r, :]
        qk = p_ref[SLAB_QK, r, :]
        vv = p_ref[SLAB_V, r, :]
        gla_g = p_ref[SLAB_GLA_G, r, :]
        for h in range(GLA_HEADS):
            kc = slice(h * GLA_DK, (h + 1) * GLA_DK)
            kc2 = slice(GLA_KEY_WIDTH + h * GLA_DK, GLA_KEY_WIDTH + (h + 1) * GLA_DK)
            vc = slice(h * GLA_DV, (h + 1) * GLA_DV)
            a_col = to_col(jnp.exp(la[:, kc]))
            q_col = to_col(qk[:, kc] * (GLA_DK ** -0.5))
            k_col = to_col(qk[:, kc2])
            s_new = a_col * s0_ref[0, i, h] + k_col * vv[:, vc]
            s_out_ref[0, i, h] = s_new
            o = jnp.sum(q_col * s_new, axis=0, keepdims=True)
            br_ref[0, r, vc] = _rms(o, gg_ref[:, vc]) * _silu(gla_g[:, vc])

        u = p_ref[SLAB_U, r, :]
        for g, w in enumerate(POOL_WINDOWS):
            cs = slice(g * POOL_GROUP_DIM, (g + 1) * POOL_GROUP_DIM)
            past = jnp.sum(hist_ref[0, i, POOL_HIST - (w - 1):POOL_HIST, cs], axis=0, keepdims=True)
            diff_ref[r, cs] = (u[:, cs] + past) / float(w) - u[:, cs]
        hist_out_ref[0, i, 0:POOL_HIST - 1, :] = hist_ref[0, i, 1:POOL_HIST, :]
        hist_out_ref[0, i, POOL_HIST - 1:POOL_HIST, :] = u

        xq = p_ref[SLAB_XQ, r, :]
        xg = p_ref[SLAB_XG, r, :]
        half_cols = [slice(h * XA_HEAD_DIM + j * LANES, h * XA_HEAD_DIM + (j + 1) * LANES)
                     for j in range(XA_HEAD_DIM // LANES) for h in range(XA_HEADS)]
        xq_rows = jnp.concatenate([xq[:, cs] for cs in half_cols], axis=0)
        n_mem = ck_ref.shape[2]
        prod = (ck_ref[0, i] * xq_rows[None]).reshape(n_mem * SUBLANES, LANES).astype(BF)
        part = _dot(prod, jnp.ones((LANES, LANES), BF)).reshape(n_mem, SUBLANES, LANES)
        s = (part + pltpu.roll(part, XA_HEADS, axis=1)) * (XA_HEAD_DIM ** -0.5)
        p = jnp.exp(s - jnp.max(s, axis=0, keepdims=True))
        o = jnp.sum(p * cv_ref[0, i], axis=0) / jnp.sum(p, axis=0)
        halves = XA_HEAD_DIM // LANES
        o_row = jnp.concatenate([o[j * XA_HEADS + h:j * XA_HEADS + h + 1, :]
                                 for h in range(XA_HEADS) for j in range(halves)], axis=1)
        br_ref[2, r, :] = o_row * _silu(xg)

    @pl.when(pl.program_id(1) == pl.num_programs(1) - 1)
    def _():
        for g in range(len(POOL_WINDOWS)):
            cs = slice(g * POOL_GROUP_DIM, (g + 1) * POOL_GROUP_DIM)
            mixed = _dot(diff_ref[:, cs].astype(BF), pw_ref[g]) * ps_ref[:, cs]
            br_ref[1, :, cs] = mixed * _silu(p_ref[SLAB_POOL_G, :, cs])


def _cache_rows_view(c):
    depth, nb, n_mem = c.shape[:3]
    halves = XA_HEAD_DIM // LANES
    c = c.reshape(depth, nb, n_mem, XA_HEADS, halves, LANES)
    return c.transpose(0, 1, 2, 4, 3, 5).reshape(depth, nb, n_mem, halves * XA_HEADS, LANES)


def _sample_mix(p, la, s0, hist, ck, cv, pool_w, pool_scale, gla_gain, carry, layer, sb=4):
    nb = s0.shape[1]
    n_mem = ck.shape[2]
    ck, cv = _cache_rows_view(ck), _cache_rows_view(cv)
    rb = SUBLANES
    halves = rb // sb
    n_alias = 0 if carry is None else len(carry)
    n_in = 9
    kern = _drop_alias_refs(functools.partial(_sample_mix_kernel, sb=sb), n_in, n_alias)
    any_spec = pl.BlockSpec(memory_space=pl.ANY)
    return pl.pallas_call(
        kern,
        grid=(nb // rb, halves),
        in_specs=[
            pl.BlockSpec((N_MIX_SLABS, rb, D_MODEL), lambda i, j: (0, i, 0)),
            pl.BlockSpec((rb, GLA_KEY_WIDTH), lambda i, j: (i, 0)),
            pl.BlockSpec((1, sb, GLA_HEADS, GLA_DK, GLA_DV), lambda i, j: (layer, i * halves + j, 0, 0, 0)),
            pl.BlockSpec((1, sb, POOL_HIST, D_MODEL), lambda i, j: (layer, i * halves + j, 0, 0)),
            pl.BlockSpec((1, sb, n_mem, SUBLANES, LANES), lambda i, j: (layer, i * halves + j, 0, 0, 0)),
            pl.BlockSpec((1, sb, n_mem, SUBLANES, LANES), lambda i, j: (layer, i * halves + j, 0, 0, 0)),
            _const_spec((len(POOL_WINDOWS), POOL_GROUP_DIM, POOL_GROUP_DIM)),
            _const_spec((1, D_MODEL)),
            _const_spec((1, D_MODEL)),
        ] + [any_spec] * n_alias,
        out_specs=[
            pl.BlockSpec((N_BRANCH, rb, D_MODEL), lambda i, j: (0, i, 0)),
            pl.BlockSpec((1, sb, GLA_HEADS, GLA_DK, GLA_DV), lambda i, j: (layer, i * halves + j, 0, 0, 0)),
            pl.BlockSpec((1, sb, POOL_HIST, D_MODEL), lambda i, j: (layer, i * halves + j, 0, 0)),
        ],
        out_shape=[
            jax.ShapeDtypeStruct((N_BRANCH, nb, D_MODEL), F32),
            jax.ShapeDtypeStruct(s0.shape, F32),
            jax.ShapeDtypeStruct(hist.shape, F32),
        ],
        scratch_shapes=[pltpu.VMEM((rb, D_MODEL), F32)],
        input_output_aliases={n_in + a: 1 + a for a in range(n_alias)},
        compiler_params=_params(("parallel", "arbitrary"), 40),
        name="sample_mix",
    )(p, la, s0, hist, ck, cv, pool_w, pool_scale, gla_gain, *(carry or ()))


def _merge_kernel(br_ref, p_ref, x_ref, wbr_ref, wo_ref, fg_ref, out_ref, *, final):
    merged = _merge_term(br_ref, p_ref, wbr_ref, 0)
    for n in range(1, N_BRANCH):
        merged = merged + _merge_term(br_ref, p_ref, wbr_ref, n)
    out_ref[...] = _merge_finish(merged, x_ref[...], wo_ref, fg_ref, final)


def _merge_out(br, p, x, w_branch, w_out, final_gain, layer, final):
    m_rows = x.shape[0]
    whole = lambda shape: pl.BlockSpec(shape, lambda i: (0,) * len(shape))
    return pl.pallas_call(
        functools.partial(_merge_kernel, final=final),
        grid=(1,),
        in_specs=[
            whole((N_BRANCH, m_rows, D_MODEL)),
            whole((N_SLABS, m_rows, D_MODEL)),
            whole((m_rows, D_MODEL)),
            _layer_spec((N_BRANCH, D_MODEL // 2, D_MODEL), layer),
            _layer_spec((D_MODEL // 2, D_MODEL), layer),
            _const_spec((1, D_MODEL)),
        ],
        out_specs=whole((m_rows, D_MODEL)),
        out_shape=jax.ShapeDtypeStruct((m_rows, D_MODEL), F32),
        compiler_params=_params(("arbitrary",), 40),
        name="merge_out",
    )(br, p, x, w_branch, w_out, final_gain)


def kernel(x_prompt, x_sample, mem_prompt, cache_mem_k, cache_mem_v, state_gla, state_pool, w_in, w_a2, b_a, gla_gain, pool_w, pool_scale, w_mk, w_mv, w_branch, w_out, norm_gain, final_gain):
    batch, seq, _ = x_prompt.shape
    nb = x_sample.shape[0]
    n_mem = mem_prompt.shape[1]
    depth = w_in.shape[0]
    tm = min(256, seq)

    xp = x_prompt.reshape(batch * seq, D_MODEL)
    xs = x_sample.reshape(nb, D_MODEL)
    mem = mem_prompt.reshape(batch * n_mem, D_MODEL)
    fgain = final_gain.reshape(1, D_MODEL)

    mk, mv, mk_bf, mv_bf = _kvproj(mem, w_mk.astype(BF), w_mv.astype(BF), batch, n_mem)

    w_in_t = jnp.swapaxes(w_in, 1, 2)
    w_proj = _pack_w_in(w_in_t)
    wb = _pack_matrices(w_branch.reshape(depth * N_BRANCH, D_MODEL, D_MODEL))
    wb = wb.reshape(depth, N_BRANCH, D_MODEL // 2, D_MODEL)
    wo = _pack_matrices(w_out)

    carry_p, carry_s = None, None
    for l in range(depth):
        final = l == depth - 1
        w_alr = w_in_t[l, ALR_START:ALR_START + GATE_RANK, :]
        wa2 = w_a2[l].astype(BF)
        ba = b_a[l].reshape(1, GLA_KEY_WIDTH)
        ngain = norm_gain[l].reshape(1, D_MODEL)
        ggain = gla_gain[l].reshape(1, D_MODEL)
        pscale = pool_scale[l].reshape(1, D_MODEL)
        pw = pool_w[l].astype(BF)

        xp, s_all, hist_all = _prompt_layer(xp, ngain, w_proj, w_alr, wa2, ba, mk_bf, mv_bf, pw, pscale,
                                            ggain, wb, wo, fgain, carry_p, l, depth, batch, seq, n_mem, tm, final)
        carry_p = (s_all, hist_all)

        ps, las = _inproj(xs, ngain, w_proj, w_alr, wa2, ba, l, nb, F32)
        brs, s_new, hist_new = _sample_mix(ps, las, state_gla, state_pool, cache_mem_k, cache_mem_v,
                                           pw, pscale, ggain, carry_s, l)
        carry_s = (s_new, hist_new)
        xs = _merge_out(brs, ps, xs, wb, wo, fgain, l, final)

    return (xp.reshape(batch, seq, D_MODEL), xs.reshape(nb, 1, D_MODEL),
            mk, mv, carry_p[0], carry_p[1], carry_s[0], carry_s[1])
```

```python
import functools

import jax
import jax.numpy as jnp
from jax import lax
from jax.experimental import pallas as pl
from jax.experimental.pallas import tpu as pltpu

D_MODEL = 1024
GLA_HEADS = 4
GLA_DK = 128
GLA_DV = 256
GLA_KEY_WIDTH = GLA_HEADS * GLA_DK
GATE_RANK = 16
GATE_TAU = 16.0
CHUNK = 64
POOL_WINDOWS = (2, 4, 8, 16)
POOL_GROUP_DIM = 256
POOL_HIST = 15
HIST_PAD = 16
XA_HEADS = 4
XA_HEAD_DIM = 256
N_BRANCH = 3
EPS = 1e-6
SUBLANES = 8
LANES = 128

SLAB_QK, SLAB_V, SLAB_GLA_G, SLAB_U, SLAB_POOL_G, SLAB_XQ, SLAB_XG, SLAB_MERGE = 0, 1, 2, 3, 4, 5, 6, 7
N_SLABS = 10
N_MIX_SLABS = 7
N_HEAD_SLABS = 3
ALR_START = N_HEAD_SLABS * D_MODEL
FILLER_SITES = 20
PROJ_PARTS = 1

BF = jnp.bfloat16
F32 = jnp.float32
MIB = 1 << 20


def _dot(a, b):
    return jnp.dot(a, b, preferred_element_type=F32)


def _dot_nt(a, b):
    return lax.dot_general(a, b, (((1,), (1,)), ((), ())), preferred_element_type=F32)


def _dot_tn(a, b):
    return lax.dot_general(a, b, (((0,), (0,)), ((), ())), preferred_element_type=F32)


def _pack_rows(w):
    return pltpu.bitcast(w.astype(BF), jnp.uint32)


def _unpack_rows(w_words):
    return pltpu.bitcast(w_words, BF)


def _params(sem, vmem_mib):
    return pltpu.CompilerParams(dimension_semantics=sem, vmem_limit_bytes=vmem_mib * MIB)


def _pack_w_in_kernel(a_ref, b_ref, o_ref):
    j = pl.program_id(1)

    @pl.when(j < N_HEAD_SLABS)
    def _():
        o_ref[0] = _pack_rows(a_ref[0].T)

    @pl.when(j >= N_HEAD_SLABS)
    def _():
        o_ref[0] = _pack_rows(jnp.concatenate([a_ref[0, GATE_RANK:, :], b_ref[0]], axis=0).T)


def _pack_w_in(w_in_t):
    depth = w_in_t.shape[0]
    return pl.pallas_call(
        _pack_w_in_kernel,
        grid=(depth, N_SLABS),
        in_specs=[
            pl.BlockSpec((1, D_MODEL, D_MODEL), lambda l, j: (l, j, 0)),
            pl.BlockSpec((1, GATE_RANK, D_MODEL), lambda l, j: (l, (j + 1) * (D_MODEL // GATE_RANK), 0)),
        ],
        out_specs=pl.BlockSpec((1, D_MODEL // 2, D_MODEL), lambda l, j: (l, 0, j)),
        out_shape=jax.ShapeDtypeStruct((depth, D_MODEL // 2, N_SLABS * D_MODEL), jnp.uint32),
        compiler_params=_params(("parallel", "parallel"), 32),
        name="pack_w_in",
    )(w_in_t, w_in_t)


def _pack_matrices_kernel(a_ref, o_ref):
    o_ref[0] = _pack_rows(a_ref[0])


def _pack_matrices(w):
    return pl.pallas_call(
        _pack_matrices_kernel,
        grid=(w.shape[0],),
        in_specs=[pl.BlockSpec((1, D_MODEL, D_MODEL), lambda r: (r, 0, 0))],
        out_specs=pl.BlockSpec((1, D_MODEL // 2, D_MODEL), lambda r: (r, 0, 0)),
        out_shape=jax.ShapeDtypeStruct((w.shape[0], D_MODEL // 2, D_MODEL), jnp.uint32),
        compiler_params=_params(("parallel",), 32),
        name="pack_matrices",
    )(w)


def _silu(x):
    return x * jax.nn.sigmoid(x)


def _rms(x, gain):
    ms = jnp.mean(x * x, axis=-1, keepdims=True)
    return x * lax.rsqrt(ms + EPS) * gain


def _const_spec(shape):
    zeros = (0,) * len(shape)
    return pl.BlockSpec(shape, lambda *_: zeros, pipeline_mode=pl.Buffered(1))


def _layer_spec(shape, layer):
    index = (layer,) + (0,) * len(shape)
    return pl.BlockSpec((1,) + tuple(shape), lambda *_: index, pipeline_mode=pl.Buffered(1))


def _drop_alias_refs(body, n_in, n_alias):
    def kern(*refs):
        return body(*refs[:n_in], *refs[n_in + n_alias:])
    return kern


def _kvproj_kernel(m_ref, wk_ref, wv_ref, k_ref, v_ref, kb_ref, vb_ref):
    m = m_ref[...].astype(BF)
    k = _dot(m, wk_ref[0])
    v = _dot(m, wv_ref[0])
    for h in range(XA_HEADS):
        cs = slice(h * XA_HEAD_DIM, (h + 1) * XA_HEAD_DIM)
        k_ref[0, 0, :, h, :] = k[:, cs]
        v_ref[0, 0, :, h, :] = v[:, cs]
    kb_ref[0] = k.astype(BF)
    vb_ref[0] = v.astype(BF)


def _kvproj(mem, wk, wv, batch, n_mem):
    depth = wk.shape[0]
    w_spec = pl.BlockSpec((1, D_MODEL, D_MODEL), lambda l, b: (l, 0, 0))
    out5 = pl.BlockSpec((1, 1, n_mem, XA_HEADS, XA_HEAD_DIM), lambda l, b: (l, b, 0, 0, 0))
    out_bf = pl.BlockSpec((1, n_mem, D_MODEL), lambda l, b: (l, b, 0))
    return pl.pallas_call(
        _kvproj_kernel,
        grid=(depth, batch),
        in_specs=[pl.BlockSpec((n_mem, D_MODEL), lambda l, b: (b, 0)), w_spec, w_spec],
        out_specs=[out5, out5, out_bf, out_bf],
        out_shape=[jax.ShapeDtypeStruct((depth, batch, n_mem, XA_HEADS, XA_HEAD_DIM), F32)] * 2
        + [jax.ShapeDtypeStruct((depth, batch * n_mem, D_MODEL), BF)] * 2,
        compiler_params=_params(("parallel", "parallel"), 32),
        name="kvproj",
    )(mem, wk, wv)


def _inproj_slab(h, w_ref, p_out, j, part=0, n_parts=1):
    width = D_MODEL // n_parts
    lo = part * width
    w = w_ref[0, :, j * D_MODEL + lo:j * D_MODEL + lo + width]
    p_out[j, :, lo:lo + width] = _dot(h, _unpack_rows(w)).astype(p_out.dtype)


def _inproj_gate_lowrank(h, walr_ref):
    return _dot_nt(h, walr_ref[...].astype(BF)).astype(BF)


def _inproj_gate(alr, wa2_ref, ba_ref, la_out):
    z = _dot(alr, wa2_ref[...]) + ba_ref[...]
    la_out[...] = (jnp.minimum(z, 0.0) - jnp.log(1.0 + jnp.exp(-jnp.abs(z)))) * (1.0 / GATE_TAU)


def _merge_term(br_ref, p, wbr_ref, n):
    return jax.nn.sigmoid(p[SLAB_MERGE + n].astype(F32)) * _dot(br_ref[n].astype(BF), _unpack_rows(wbr_ref[0, n]))


def _merge_finish(merged, x, wo_ref, fg_ref, final):
    x_new = x + _dot(merged.astype(BF), _unpack_rows(wo_ref[0]))
    return _rms(x_new, fg_ref[...]) if final else x_new


def _chunk_cumsum_matrix(tm):
    row = lax.broadcasted_iota(jnp.int32, (tm, tm), 0)
    col = lax.broadcasted_iota(jnp.int32, (tm, tm), 1)
    return (((row // CHUNK) == (col // CHUNK)) & (row >= col)).astype(BF)


def _window_matrices(tm):
    row = lax.broadcasted_iota(jnp.int32, (tm, tm), 0)
    col = lax.broadcasted_iota(jnp.int32, (tm, tm), 1)
    return jnp.stack([((row >= col) & (row - col < w)).astype(BF) for w in POOL_WINDOWS])


def _gla_cumdecay(cum_ref, la_ref):
    la = la_ref[...]
    la_hi = la.astype(BF)
    la_lo = (la - la_hi.astype(F32)).astype(BF)
    return _dot(cum_ref[...], la_hi) + _dot(cum_ref[...], la_lo)


def _gla_chunk(p, bcum, gg_ref, br_ref, st_ref, c, filler):
    crow = lax.broadcasted_iota(jnp.int32, (CHUNK, CHUNK), 0)
    ccol = lax.broadcasted_iota(jnp.int32, (CHUNK, CHUNK), 1)
    causal = crow >= ccol
    rows = slice(c * CHUNK, (c + 1) * CHUNK)
    heads = range(GLA_HEADS)
    q_dec, k_end, decay, att = [], [], [], []
    for h in heads:
        kc = slice(h * GLA_DK, (h + 1) * GLA_DK)
        kc2 = slice(GLA_KEY_WIDTH + h * GLA_DK, GLA_KEY_WIDTH + (h + 1) * GLA_DK)
        b = bcum[rows, kc]
        b_last = b[CHUNK - 1:CHUNK, :]
        q = p[SLAB_QK, rows, kc].astype(F32)
        k = p[SLAB_QK, rows, kc2].astype(F32)
        q_dec.append((q * (GLA_DK ** -0.5) * jnp.exp(b)).astype(BF))
        k_inv = (k * jnp.exp(-b)).astype(BF)
        k_end.append((k * jnp.exp(b_last - b)).astype(BF))
        decay.append(jnp.exp(b_last))
        att.append(_dot_nt(q_dec[h], k_inv))
    filler()
    o = []
    for h in heads:
        vc = slice(h * GLA_DV, (h + 1) * GLA_DV)
        a = jnp.where(causal, att[h], 0.0).astype(BF)
        o.append(_dot(a, p[SLAB_V, rows, vc]) + _dot_nt(q_dec[h], st_ref[h].astype(BF)))
    filler()
    for h in heads:
        vc = slice(h * GLA_DV, (h + 1) * GLA_DV)
        st_ref[h] = decay[h] * st_ref[h] + _dot_tn(p[SLAB_V, rows, vc], k_end[h])
    for h in heads:
        vc = slice(h * GLA_DV, (h + 1) * GLA_DV)
        g = p[SLAB_GLA_G, rows, vc].astype(F32)
        br_ref[0, rows, vc] = (_rms(o[h], gg_ref[:, vc]) * _silu(g)).astype(BF)


def _pool_branch(p, win_ref, pw_ref, ps_ref, br_ref, ubuf_ref, t, tm):
    u_bf = p[SLAB_U]
    u = u_bf.astype(F32)
    ubuf_ref[HIST_PAD:2 * HIST_PAD, :] = u[0:HIST_PAD]
    pos = t * tm + lax.broadcasted_iota(jnp.int32, (tm, 1), 0)
    for g, w in enumerate(POOL_WINDOWS):
        cs = slice(g * POOL_GROUP_DIM, (g + 1) * POOL_GROUP_DIM)
        ug = u[:, cs]
        head = ug[0:HIST_PAD]
        for j in range(1, w):
            head = head + ubuf_ref[HIST_PAD - j:2 * HIST_PAD - j, cs]
        s = jnp.concatenate([head, _dot(win_ref[g], u_bf[:, cs])[HIST_PAD:]], axis=0)
        cnt = jnp.minimum(w, pos + 1).astype(F32)
        diff = s / cnt - ug
        mixed = _dot(diff.astype(BF), pw_ref[g]) * ps_ref[:, cs]
        pg = p[SLAB_POOL_G, :, cs].astype(F32)
        br_ref[1, :, cs] = (mixed * _silu(pg)).astype(BF)
    ubuf_ref[0:HIST_PAD, :] = u[tm - HIST_PAD:tm]


def _xattn_probs(p, mk_ref):
    out = []
    for h in range(XA_HEADS):
        cs = slice(h * XA_HEAD_DIM, (h + 1) * XA_HEAD_DIM)
        s = _dot_nt(p[SLAB_XQ, :, cs], mk_ref[0, :, cs]) * (XA_HEAD_DIM ** -0.5)
        pr = jnp.exp(s - jnp.max(s, axis=-1, keepdims=True))
        out.append((pr.astype(BF), jnp.sum(pr, axis=-1, keepdims=True)))
    return out


def _xattn_branch(p, probs, mv_ref, br_ref):
    for h in range(XA_HEADS):
        cs = slice(h * XA_HEAD_DIM, (h + 1) * XA_HEAD_DIM)
        pr, denom = probs[h]
        o = _dot(pr, mv_ref[0, :, cs]) / denom
        xg = p[SLAB_XG, :, cs].astype(F32)
        br_ref[2, :, cs] = (o * _silu(xg)).astype(BF)


def _prompt_layer_kernel(xn_ref, xc_ref, g_ref, w_ref, walr_ref, wa2_ref, ba_ref, cum_ref, win_ref,
                         mk_ref, mv_ref, pw_ref, ps_ref, gg_ref, wbr_ref, wo_ref, fg_ref,
                         out_ref, s_out_ref, hist_out_ref,
                         h_scr, p_scr, la_scr, br_scr, st_ref, ubuf_ref, *, tm, nt, final):
    s = pl.program_id(0)
    t = jnp.maximum(s - 1, 0) % nt
    slot_w = s % 2
    slot_r = 1 - slot_w

    @pl.when(s == 0)
    def _():
        p_scr[1] = jnp.zeros(p_scr.shape[1:], p_scr.dtype)
        la_scr[1] = jnp.zeros(la_scr.shape[1:], la_scr.dtype)

    @pl.when(t == 0)
    def _():
        st_ref[...] = jnp.zeros_like(st_ref)
        ubuf_ref[0:HIST_PAD, :] = jnp.zeros((HIST_PAD, D_MODEL), F32)

    @pl.when(s == 0)
    def _():
        h_scr[0] = _rms(xc_ref[...], g_ref[...]).astype(BF)

    p_next = p_scr.at[slot_w]
    p = p_scr.at[slot_r]

    pieces = [(j, part) for j in range(N_SLABS) for part in range(PROJ_PARTS)]
    n_pieces = len(pieces)
    calls = [0]

    def filler():
        calls[0] += 1
        while pieces and (n_pieces - len(pieces) + 1) * FILLER_SITES <= calls[0] * n_pieces:
            j, part = pieces.pop(0)
            _inproj_slab(h_scr[slot_w], w_ref, p_next, j, part, PROJ_PARTS)

    alr = _inproj_gate_lowrank(h_scr[slot_w], walr_ref)
    bcum = _gla_cumdecay(cum_ref, la_scr.at[slot_r])
    filler()
    _inproj_gate(alr, wa2_ref, ba_ref, la_scr.at[slot_w])
    filler()
    probs = None
    for c in range(tm // CHUNK):
        _gla_chunk(p, bcum, gg_ref, br_scr, st_ref, c, filler)
        filler()
        if c == 0:
            probs = _xattn_probs(p, mk_ref)
            filler()
        elif c == 1:
            _xattn_branch(p, probs, mv_ref, br_scr)
            filler()
        elif c == 2:
            _pool_branch(p, win_ref, pw_ref, ps_ref, br_scr, ubuf_ref, t, tm)
            filler()
    h_scr[slot_r] = _rms(xn_ref[...], g_ref[...]).astype(BF)
    merged = _merge_term(br_scr, p, wbr_ref, 2)
    filler()
    merged = merged + _merge_term(br_scr, p, wbr_ref, 1)
    filler()
    merged = merged + _merge_term(br_scr, p, wbr_ref, 0)
    while pieces:
        filler()
    out_ref[...] = _merge_finish(merged, xc_ref[...], wo_ref, fg_ref, final)

    @pl.when((s > 0) & (t == nt - 1))
    def _():
        for h in range(GLA_HEADS):
            s_out_ref[0, 0, h] = st_ref[h].T
        hist_out_ref[0, 0] = ubuf_ref[1:HIST_PAD, :]


def _prompt_layer(x, ngain, w_proj, w_alr, wa2, ba, mk, mv, pool_w, pool_scale, gla_gain,
                  w_branch, w_out, fgain, carry, layer, depth, batch, seq, n_mem, tm, final):
    nt = seq // tm
    n_tiles = batch * nt
    assert tm // CHUNK >= 3, "the mixer stages are spread over the first three GLA chunks of a tile"
    n_alias = 0 if carry is None else len(carry)
    n_in = 17
    kern = _drop_alias_refs(functools.partial(_prompt_layer_kernel, tm=tm, nt=nt, final=final), n_in, n_alias)
    any_spec = pl.BlockSpec(memory_space=pl.ANY)

    def cur(s):
        return jnp.maximum(s - 1, 0)

    def seq_of(s):
        return cur(s) // nt

    return pl.pallas_call(
        kern,
        grid=(n_tiles + 1,),
        in_specs=[
            pl.BlockSpec((tm, D_MODEL), lambda s: (jnp.minimum(s + 1, n_tiles - 1), 0)),
            pl.BlockSpec((tm, D_MODEL), lambda s: (cur(s), 0)),
            _const_spec((1, D_MODEL)),
            _layer_spec((D_MODEL // 2, N_SLABS * D_MODEL), layer),
            _const_spec((GATE_RANK, D_MODEL)),
            _const_spec((GATE_RANK, GLA_KEY_WIDTH)),
            _const_spec((1, GLA_KEY_WIDTH)),
            _const_spec((tm, tm)),
            _const_spec((len(POOL_WINDOWS), tm, tm)),
            pl.BlockSpec((1, n_mem, D_MODEL), lambda s: (layer, seq_of(s), 0)),
            pl.BlockSpec((1, n_mem, D_MODEL), lambda s: (layer, seq_of(s), 0)),
            _const_spec((len(POOL_WINDOWS), POOL_GROUP_DIM, POOL_GROUP_DIM)),
            _const_spec((1, D_MODEL)),
            _const_spec((1, D_MODEL)),
            _layer_spec((N_BRANCH, D_MODEL // 2, D_MODEL), layer),
            _layer_spec((D_MODEL // 2, D_MODEL), layer),
            _const_spec((1, D_MODEL)),
        ] + [any_spec] * n_alias,
        out_specs=[
            pl.BlockSpec((tm, D_MODEL), lambda s: (cur(s), 0)),
            pl.BlockSpec((1, 1, GLA_HEADS, GLA_DK, GLA_DV), lambda s: (layer, seq_of(s), 0, 0, 0)),
            pl.BlockSpec((1, 1, POOL_HIST, D_MODEL), lambda s: (layer, seq_of(s), 0, 0)),
        ],
        out_shape=[
            jax.ShapeDtypeStruct((n_tiles * tm, D_MODEL), F32),
            jax.ShapeDtypeStruct((depth, batch, GLA_HEADS, GLA_DK, GLA_DV), F32),
            jax.ShapeDtypeStruct((depth, batch, POOL_HIST, D_MODEL), F32),
        ],
        scratch_shapes=[
            pltpu.VMEM((2, tm, D_MODEL), BF),
            pltpu.VMEM((2, N_SLABS, tm, D_MODEL), BF),
            pltpu.VMEM((2, tm, GLA_KEY_WIDTH), F32),
            pltpu.VMEM((N_BRANCH, tm, D_MODEL), BF),
            pltpu.VMEM((GLA_HEADS, GLA_DV, GLA_DK), F32),
            pltpu.VMEM((2 * HIST_PAD, D_MODEL), F32),
        ],
        input_output_aliases={n_in + a: 1 + a for a in range(n_alias)},
        compiler_params=_params(("arbitrary",), 60),
        name="prompt_layer",
    )(x, x, ngain, w_proj, w_alr, wa2, ba, _chunk_cumsum_matrix(tm), _window_matrices(tm),
      mk, mv, pool_w, pool_scale, gla_gain,
      w_branch, w_out, fgain, *(carry or ()))


def _inproj_kernel(x_ref, g_ref, w_ref, walr_ref, wa2_ref, ba_ref, p_ref, la_ref):
    h = _rms(x_ref[...], g_ref[...]).astype(BF)
    alr = _inproj_gate_lowrank(h, walr_ref)
    for j in range(N_SLABS):
        _inproj_slab(h, w_ref, p_ref, j)
    _inproj_gate(alr, wa2_ref, ba_ref, la_ref)


def _inproj(x, gain, w_proj, w_alr, w_a2, b_a, layer, tm, out_dtype):
    m_rows = x.shape[0]
    return pl.pallas_call(
        _inproj_kernel,
        grid=(m_rows // tm,),
        in_specs=[
            pl.BlockSpec((tm, D_MODEL), lambda i: (i, 0)),
            _const_spec((1, D_MODEL)),
            _layer_spec((D_MODEL // 2, N_SLABS * D_MODEL), layer),
            _const_spec((GATE_RANK, D_MODEL)),
            _const_spec((GATE_RANK, GLA_KEY_WIDTH)),
            _const_spec((1, GLA_KEY_WIDTH)),
        ],
        out_specs=[
            pl.BlockSpec((N_SLABS, tm, D_MODEL), lambda i: (0, i, 0)),
            pl.BlockSpec((tm, GLA_KEY_WIDTH), lambda i: (i, 0)),
        ],
        out_shape=[
            jax.ShapeDtypeStruct((N_SLABS, m_rows, D_MODEL), out_dtype),
            jax.ShapeDtypeStruct((m_rows, GLA_KEY_WIDTH), F32),
        ],
        compiler_params=_params(("parallel",), 48),
        name="inproj",
    )(x, gain, w_proj, w_alr, w_a2, b_a)


def _sample_mix_kernel(p_ref, la_ref, s0_ref, hist_ref, ck_ref, cv_ref, pw_ref, ps_ref, gg_ref,
                       br_ref, s_out_ref, hist_out_ref, diff_ref, *, sb):
    r0 = pl.program_id(1) * sb
    erow = lax.broadcasted_iota(jnp.int32, (GLA_DK, GLA_DK), 0)
    ecol = lax.broadcasted_iota(jnp.int32, (GLA_DK, GLA_DK), 1)
    eye = erow == ecol

    def to_col(x):
        return jnp.sum(jnp.where(eye, jnp.broadcast_to(x, (GLA_DK, GLA_DK)), 0.0), axis=1, keepdims=True)

    for i in range(sb):
        r = pl.ds(r0 + i, 1)
        la = la_ref[r, :]
        qk = p_ref[SLAB_QK, r, :]
        vv = p_ref[SLAB_V, r, :]
        gla_g = p_ref[SLAB_GLA_G, r, :]
        for h in range(GLA_HEADS):
            kc = slice(h * GLA_DK, (h + 1) * GLA_DK)
            kc2 = slice(GLA_KEY_WIDTH + h * GLA_DK, GLA_KEY_WIDTH + (h + 1) * GLA_DK)
            vc = slice(h * GLA_DV, (h + 1) * GLA_DV)
            a_col = to_col(jnp.exp(la[:, kc]))
            q_col = to_col(qk[:, kc] * (GLA_DK ** -0.5))
            k_col = to_col(qk[:, kc2])
            s_new = a_col * s0_ref[0, i, h] + k_col * vv[:, vc]
            s_out_ref[0, i, h] = s_new
            o = jnp.sum(q_col * s_new, axis=0, keepdims=True)
            br_ref[0, r, vc] = _rms(o, gg_ref[:, vc]) * _silu(gla_g[:, vc])

        u = p_ref[SLAB_U, r, :]
        for g, w in enumerate(POOL_WINDOWS):
            cs = slice(g * POOL_GROUP_DIM, (g + 1) * POOL_GROUP_DIM)
            past = jnp.sum(hist_ref[0, i, POOL_HIST - (w - 1):POOL_HIST, cs], axis=0, keepdims=True)
            diff_ref[r, cs] = (u[:, cs] + past) / float(w) - u[:, cs]
        hist_out_ref[0, i, 0:POOL_HIST - 1, :] = hist_ref[0, i, 1:POOL_HIST, :]
        hist_out_ref[0, i, POOL_HIST - 1:POOL_HIST, :] = u

        xq = p_ref[SLAB_XQ, r, :]
        xg = p_ref[SLAB_XG, r, :]
        half_cols = [slice(h * XA_HEAD_DIM + j * LANES, h * XA_HEAD_DIM + (j + 1) * LANES)
                     for j in range(XA_HEAD_DIM // LANES) for h in range(XA_HEADS)]
        xq_rows = jnp.concatenate([xq[:, cs] for cs in half_cols], axis=0)
        n_mem = ck_ref.shape[2]
        prod = (ck_ref[0, i] * xq_rows[None]).reshape(n_mem * SUBLANES, LANES).astype(BF)
        part = _dot(prod, jnp.ones((LANES, LANES), BF)).reshape(n_mem, SUBLANES, LANES)
        s = (part + pltpu.roll(part, XA_HEADS, axis=1)) * (XA_HEAD_DIM ** -0.5)
        p = jnp.exp(s - jnp.max(s, axis=0, keepdims=True))
        o = jnp.sum(p * cv_ref[0, i], axis=0) / jnp.sum(p, axis=0)
        halves = XA_HEAD_DIM // LANES
        o_row = jnp.concatenate([o[j * XA_HEADS + h:j * XA_HEADS + h + 1, :]
                                 for h in range(XA_HEADS) for j in range(halves)], axis=1)
        br_ref[2, r, :] = o_row * _silu(xg)

    @pl.when(pl.program_id(1) == pl.num_programs(1) - 1)
    def _():
        for g in range(len(POOL_WINDOWS)):
            cs = slice(g * POOL_GROUP_DIM, (g + 1) * POOL_GROUP_DIM)
            mixed = _dot(diff_ref[:, cs].astype(BF), pw_ref[g]) * ps_ref[:, cs]
            br_ref[1, :, cs] = mixed * _silu(p_ref[SLAB_POOL_G, :, cs])


def _cache_rows_view(c):
    depth, nb, n_mem = c.shape[:3]
    halves = XA_HEAD_DIM // LANES
    c = c.reshape(depth, nb, n_mem, XA_HEADS, halves, LANES)
    return c.transpose(0, 1, 2, 4, 3, 5).reshape(depth, nb, n_mem, halves * XA_HEADS, LANES)


def _sample_mix(p, la, s0, hist, ck, cv, pool_w, pool_scale, gla_gain, carry, layer, sb=8):
    nb = s0.shape[1]
    n_mem = ck.shape[2]
    ck, cv = _cache_rows_view(ck), _cache_rows_view(cv)
    rb = SUBLANES
    halves = rb // sb
    n_alias = 0 if carry is None else len(carry)
    n_in = 9
    kern = _drop_alias_refs(functools.partial(_sample_mix_kernel, sb=sb), n_in, n_alias)
    any_spec = pl.BlockSpec(memory_space=pl.ANY)
    return pl.pallas_call(
        kern,
        grid=(nb // rb, halves),
        in_specs=[
            pl.BlockSpec((N_MIX_SLABS, rb, D_MODEL), lambda i, j: (0, i, 0)),
            pl.BlockSpec((rb, GLA_KEY_WIDTH), lambda i, j: (i, 0)),
            pl.BlockSpec((1, sb, GLA_HEADS, GLA_DK, GLA_DV), lambda i, j: (layer, i * halves + j, 0, 0, 0)),
            pl.BlockSpec((1, sb, POOL_HIST, D_MODEL), lambda i, j: (layer, i * halves + j, 0, 0)),
            pl.BlockSpec((1, sb, n_mem, SUBLANES, LANES), lambda i, j: (layer, i * halves + j, 0, 0, 0)),
            pl.BlockSpec((1, sb, n_mem, SUBLANES, LANES), lambda i, j: (layer, i * halves + j, 0, 0, 0)),
            _const_spec((len(POOL_WINDOWS), POOL_GROUP_DIM, POOL_GROUP_DIM)),
            _const_spec((1, D_MODEL)),
            _const_spec((1, D_MODEL)),
        ] + [any_spec] * n_alias,
        out_specs=[
            pl.BlockSpec((N_BRANCH, rb, D_MODEL), lambda i, j: (0, i, 0)),
            pl.BlockSpec((1, sb, GLA_HEADS, GLA_DK, GLA_DV), lambda i, j: (layer, i * halves + j, 0, 0, 0)),
            pl.BlockSpec((1, sb, POOL_HIST, D_MODEL), lambda i, j: (layer, i * halves + j, 0, 0)),
        ],
        out_shape=[
            jax.ShapeDtypeStruct((N_BRANCH, nb, D_MODEL), F32),
            jax.ShapeDtypeStruct(s0.shape, F32),
            jax.ShapeDtypeStruct(hist.shape, F32),
        ],
        scratch_shapes=[pltpu.VMEM((rb, D_MODEL), F32)],
        input_output_aliases={n_in + a: 1 + a for a in range(n_alias)},
        compiler_params=_params(("parallel", "arbitrary"), 58),
        name="sample_mix",
    )(p, la, s0, hist, ck, cv, pool_w, pool_scale, gla_gain, *(carry or ()))


def _merge_kernel(br_ref, p_ref, x_ref, wbr_ref, wo_ref, fg_ref, out_ref, *, final):
    merged = _merge_term(br_ref, p_ref, wbr_ref, 0)
    for n in range(1, N_BRANCH):
        merged = merged + _merge_term(br_ref, p_ref, wbr_ref, n)
    out_ref[...] = _merge_finish(merged, x_ref[...], wo_ref, fg_ref, final)


def _merge_out(br, p, x, w_branch, w_out, final_gain, layer, final):
    m_rows = x.shape[0]
    whole = lambda shape: pl.BlockSpec(shape, lambda i: (0,) * len(shape))
    return pl.pallas_call(
        functools.partial(_merge_kernel, final=final),
        grid=(1,),
        in_specs=[
            whole((N_BRANCH, m_rows, D_MODEL)),
            whole((N_SLABS, m_rows, D_MODEL)),
            whole((m_rows, D_MODEL)),
            _layer_spec((N_BRANCH, D_MODEL // 2, D_MODEL), layer),
            _layer_spec((D_MODEL // 2, D_MODEL), layer),
            _const_spec((1, D_MODEL)),
        ],
        out_specs=whole((m_rows, D_MODEL)),
        out_shape=jax.ShapeDtypeStruct((m_rows, D_MODEL), F32),
        compiler_params=_params(("arbitrary",), 40),
        name="merge_out",
    )(br, p, x, w_branch, w_out, final_gain)


def kernel(x_prompt, x_sample, mem_prompt, cache_mem_k, cache_mem_v, state_gla, state_pool, w_in, w_a2, b_a, gla_gain, pool_w, pool_scale, w_mk, w_mv, w_branch, w_out, norm_gain, final_gain):
    batch, seq, _ = x_prompt.shape
    nb = x_sample.shape[0]
    n_mem = mem_prompt.shape[1]
    depth = w_in.shape[0]
    tm = min(256, seq)

    xp = x_prompt.reshape(batch * seq, D_MODEL)
    xs = x_sample.reshape(nb, D_MODEL)
    mem = mem_prompt.reshape(batch * n_mem, D_MODEL)
    fgain = final_gain.reshape(1, D_MODEL)

    mk, mv, mk_bf, mv_bf = _kvproj(mem, w_mk.astype(BF), w_mv.astype(BF), batch, n_mem)

    w_in_t = jnp.swapaxes(w_in, 1, 2)
    w_proj = _pack_w_in(w_in_t)
    wb = _pack_matrices(w_branch.reshape(depth * N_BRANCH, D_MODEL, D_MODEL))
    wb = wb.reshape(depth, N_BRANCH, D_MODEL // 2, D_MODEL)
    wo = _pack_matrices(w_out)

    carry_p, carry_s = None, None
    for l in range(depth):
        final = l == depth - 1
        w_alr = w_in_t[l, ALR_START:ALR_START + GATE_RANK, :]
        wa2 = w_a2[l].astype(BF)
        ba = b_a[l].reshape(1, GLA_KEY_WIDTH)
        ngain = norm_gain[l].reshape(1, D_MODEL)
        ggain = gla_gain[l].reshape(1, D_MODEL)
        pscale = pool_scale[l].reshape(1, D_MODEL)
        pw = pool_w[l].astype(BF)

        xp, s_all, hist_all = _prompt_layer(xp, ngain, w_proj, w_alr, wa2, ba, mk_bf, mv_bf, pw, pscale,
                                            ggain, wb, wo, fgain, carry_p, l, depth, batch, seq, n_mem, tm, final)
        carry_p = (s_all, hist_all)

        ps, las = _inproj(xs, ngain, w_proj, w_alr, wa2, ba, l, nb, F32)
        brs, s_new, hist_new = _sample_mix(ps, las, state_gla, state_pool, cache_mem_k, cache_mem_v,
                                           pw, pscale, ggain, carry_s, l)
        carry_s = (s_new, hist_new)
        xs = _merge_out(brs, ps, xs, wb, wo, fgain, l, final)

    return (xp.reshape(batch, seq, D_MODEL), xs.reshape(nb, 1, D_MODEL),
            mk, mv, carry_p[0], carry_p[1], carry_s[0], carry_s[1])
```

```python
import functools

import jax
import jax.numpy as jnp
from jax import lax
from jax.experimental import pallas as pl
from jax.experimental.pallas import tpu as pltpu

D_MODEL = 1024
GLA_HEADS = 4
GLA_DK = 128
GLA_DV = 256
GLA_KEY_WIDTH = GLA_HEADS * GLA_DK
GATE_RANK = 16
GATE_TAU = 16.0
CHUNK = 64
POOL_WINDOWS = (2, 4, 8, 16)
POOL_GROUP_DIM = 256
POOL_HIST = 15
HIST_PAD = 16
XA_HEADS = 4
XA_HEAD_DIM = 256
N_BRANCH = 3
EPS = 1e-6
SUBLANES = 8
LANES = 128

SLAB_QK, SLAB_V, SLAB_GLA_G, SLAB_U, SLAB_POOL_G, SLAB_XQ, SLAB_XG, SLAB_MERGE = 0, 1, 2, 3, 4, 5, 6, 7
N_SLABS = 10
N_MIX_SLABS = 7
N_HEAD_SLABS = 3
ALR_START = N_HEAD_SLABS * D_MODEL

BF = jnp.bfloat16
F32 = jnp.float32
MIB = 1 << 20


def _dot(a, b):
    return jnp.dot(a, b, preferred_element_type=F32)


def _dot_nt(a, b):
    return lax.dot_general(a, b, (((1,), (1,)), ((), ())), preferred_element_type=F32)


def _dot_tn(a, b):
    return lax.dot_general(a, b, (((0,), (0,)), ((), ())), preferred_element_type=F32)


def _pack_rows(w):
    return pltpu.bitcast(w.astype(BF), jnp.uint32)


def _unpack_rows(w_words):
    return pltpu.bitcast(w_words, BF)


def _params(sem, vmem_mib):
    return pltpu.CompilerParams(dimension_semantics=sem, vmem_limit_bytes=vmem_mib * MIB)


def _pack_w_in_kernel(a_ref, b_ref, o_ref):
    j = pl.program_id(1)

    @pl.when(j < N_HEAD_SLABS)
    def _():
        o_ref[0] = _pack_rows(a_ref[0].T)

    @pl.when(j >= N_HEAD_SLABS)
    def _():
        o_ref[0] = _pack_rows(jnp.concatenate([a_ref[0, GATE_RANK:, :], b_ref[0]], axis=0).T)


def _pack_w_in(w_in_t):
    depth = w_in_t.shape[0]
    return pl.pallas_call(
        _pack_w_in_kernel,
        grid=(depth, N_SLABS),
        in_specs=[
            pl.BlockSpec((1, D_MODEL, D_MODEL), lambda l, j: (l, j, 0)),
            pl.BlockSpec((1, GATE_RANK, D_MODEL), lambda l, j: (l, (j + 1) * (D_MODEL // GATE_RANK), 0)),
        ],
        out_specs=pl.BlockSpec((1, D_MODEL // 2, D_MODEL), lambda l, j: (l, 0, j)),
        out_shape=jax.ShapeDtypeStruct((depth, D_MODEL // 2, N_SLABS * D_MODEL), jnp.uint32),
        compiler_params=_params(("parallel", "parallel"), 32),
        name="pack_w_in",
    )(w_in_t, w_in_t)


def _pack_matrices_kernel(a_ref, o_ref):
    o_ref[0] = _pack_rows(a_ref[0])


def _pack_matrices(w):
    return pl.pallas_call(
        _pack_matrices_kernel,
        grid=(w.shape[0],),
        in_specs=[pl.BlockSpec((1, D_MODEL, D_MODEL), lambda r: (r, 0, 0))],
        out_specs=pl.BlockSpec((1, D_MODEL // 2, D_MODEL), lambda r: (r, 0, 0)),
        out_shape=jax.ShapeDtypeStruct((w.shape[0], D_MODEL // 2, D_MODEL), jnp.uint32),
        compiler_params=_params(("parallel",), 32),
        name="pack_matrices",
    )(w)


def _silu(x):
    return x * jax.nn.sigmoid(x)


def _rms(x, gain):
    ms = jnp.mean(x * x, axis=-1, keepdims=True)
    return x * lax.rsqrt(ms + EPS) * gain


def _const_spec(shape):
    zeros = (0,) * len(shape)
    return pl.BlockSpec(shape, lambda *_: zeros, pipeline_mode=pl.Buffered(1))


def _layer_spec(shape, layer):
    index = (layer,) + (0,) * len(shape)
    return pl.BlockSpec((1,) + tuple(shape), lambda *_: index, pipeline_mode=pl.Buffered(1))


def _drop_alias_refs(body, n_in, n_alias):
    def kern(*refs):
        return body(*refs[:n_in], *refs[n_in + n_alias:])
    return kern


def _kvproj_kernel(m_ref, wk_ref, wv_ref, k_ref, v_ref, kb_ref, vb_ref):
    m = m_ref[...].astype(BF)
    k = _dot(m, wk_ref[0])
    v = _dot(m, wv_ref[0])
    for h in range(XA_HEADS):
        cs = slice(h * XA_HEAD_DIM, (h + 1) * XA_HEAD_DIM)
        k_ref[0, 0, :, h, :] = k[:, cs]
        v_ref[0, 0, :, h, :] = v[:, cs]
    kb_ref[0] = k.astype(BF)
    vb_ref[0] = v.astype(BF)


def _kvproj(mem, wk, wv, batch, n_mem):
    depth = wk.shape[0]
    w_spec = pl.BlockSpec((1, D_MODEL, D_MODEL), lambda l, b: (l, 0, 0))
    out5 = pl.BlockSpec((1, 1, n_mem, XA_HEADS, XA_HEAD_DIM), lambda l, b: (l, b, 0, 0, 0))
    out_bf = pl.BlockSpec((1, n_mem, D_MODEL), lambda l, b: (l, b, 0))
    return pl.pallas_call(
        _kvproj_kernel,
        grid=(depth, batch),
        in_specs=[pl.BlockSpec((n_mem, D_MODEL), lambda l, b: (b, 0)), w_spec, w_spec],
        out_specs=[out5, out5, out_bf, out_bf],
        out_shape=[jax.ShapeDtypeStruct((depth, batch, n_mem, XA_HEADS, XA_HEAD_DIM), F32)] * 2
        + [jax.ShapeDtypeStruct((depth, batch * n_mem, D_MODEL), BF)] * 2,
        compiler_params=_params(("parallel", "parallel"), 32),
        name="kvproj",
    )(mem, wk, wv)


def _inproj_slab(h, w_ref, p_out, j):
    w = w_ref[0, :, j * D_MODEL:(j + 1) * D_MODEL]
    p_out[j] = _dot(h, _unpack_rows(w)).astype(p_out.dtype)


def _inproj_gate_lowrank(h, walr_ref):
    return _dot_nt(h, walr_ref[...].astype(BF)).astype(BF)


def _inproj_gate(alr, wa2_ref, ba_ref, la_out):
    z = _dot(alr, wa2_ref[...]) + ba_ref[...]
    la_out[...] = (jnp.minimum(z, 0.0) - jnp.log(1.0 + jnp.exp(-jnp.abs(z)))) * (1.0 / GATE_TAU)


def _merge_term(br_ref, p, wbr_ref, n):
    return jax.nn.sigmoid(p[SLAB_MERGE + n].astype(F32)) * _dot(br_ref[n].astype(BF), _unpack_rows(wbr_ref[0, n]))


def _merge_finish(merged, x, wo_ref, fg_ref, final):
    x_new = x + _dot(merged.astype(BF), _unpack_rows(wo_ref[0]))
    return _rms(x_new, fg_ref[...]) if final else x_new


def _chunk_cumsum_matrix(tm):
    row = lax.broadcasted_iota(jnp.int32, (tm, tm), 0)
    col = lax.broadcasted_iota(jnp.int32, (tm, tm), 1)
    return (((row // CHUNK) == (col // CHUNK)) & (row >= col)).astype(BF)


def _window_matrices(tm):
    row = lax.broadcasted_iota(jnp.int32, (tm, tm), 0)
    col = lax.broadcasted_iota(jnp.int32, (tm, tm), 1)
    return jnp.stack([((row >= col) & (row - col < w)).astype(BF) for w in POOL_WINDOWS])


def _gla_cumdecay(cum_ref, la_ref):
    la = la_ref[...]
    la_hi = la.astype(BF)
    la_lo = (la - la_hi.astype(F32)).astype(BF)
    return _dot(cum_ref[...], la_hi) + _dot(cum_ref[...], la_lo)


def _gla_chunk(p, bcum, gg_ref, br_ref, st_ref, c, filler):
    crow = lax.broadcasted_iota(jnp.int32, (CHUNK, CHUNK), 0)
    ccol = lax.broadcasted_iota(jnp.int32, (CHUNK, CHUNK), 1)
    causal = crow >= ccol
    rows = slice(c * CHUNK, (c + 1) * CHUNK)
    heads = range(GLA_HEADS)
    q_dec, k_end, decay, att = [], [], [], []
    for h in heads:
        kc = slice(h * GLA_DK, (h + 1) * GLA_DK)
        kc2 = slice(GLA_KEY_WIDTH + h * GLA_DK, GLA_KEY_WIDTH + (h + 1) * GLA_DK)
        b = bcum[rows, kc]
        b_last = b[CHUNK - 1:CHUNK, :]
        q = p[SLAB_QK, rows, kc].astype(F32)
        k = p[SLAB_QK, rows, kc2].astype(F32)
        q_dec.append((q * (GLA_DK ** -0.5) * jnp.exp(b)).astype(BF))
        k_inv = (k * jnp.exp(-b)).astype(BF)
        k_end.append((k * jnp.exp(b_last - b)).astype(BF))
        decay.append(jnp.exp(b_last))
        att.append(_dot_nt(q_dec[h], k_inv))
    filler()
    o = []
    for h in heads:
        vc = slice(h * GLA_DV, (h + 1) * GLA_DV)
        a = jnp.where(causal, att[h], 0.0).astype(BF)
        o.append(_dot(a, p[SLAB_V, rows, vc]) + _dot_nt(q_dec[h], st_ref[h].astype(BF)))
    for h in heads:
        vc = slice(h * GLA_DV, (h + 1) * GLA_DV)
        st_ref[h] = decay[h] * st_ref[h] + _dot_tn(p[SLAB_V, rows, vc], k_end[h])
    for h in heads:
        vc = slice(h * GLA_DV, (h + 1) * GLA_DV)
        g = p[SLAB_GLA_G, rows, vc].astype(F32)
        br_ref[0, rows, vc] = (_rms(o[h], gg_ref[:, vc]) * _silu(g)).astype(BF)


def _pool_branch(p, win_ref, pw_ref, ps_ref, br_ref, ubuf_ref, t, tm):
    u_bf = p[SLAB_U]
    u = u_bf.astype(F32)
    ubuf_ref[HIST_PAD:2 * HIST_PAD, :] = u[0:HIST_PAD]
    pos = t * tm + lax.broadcasted_iota(jnp.int32, (tm, 1), 0)
    for g, w in enumerate(POOL_WINDOWS):
        cs = slice(g * POOL_GROUP_DIM, (g + 1) * POOL_GROUP_DIM)
        ug = u[:, cs]
        head = ug[0:HIST_PAD]
        for j in range(1, w):
            head = head + ubuf_ref[HIST_PAD - j:2 * HIST_PAD - j, cs]
        s = jnp.concatenate([head, _dot(win_ref[g], u_bf[:, cs])[HIST_PAD:]], axis=0)
        cnt = jnp.minimum(w, pos + 1).astype(F32)
        diff = s / cnt - ug
        mixed = _dot(diff.astype(BF), pw_ref[g]) * ps_ref[:, cs]
        pg = p[SLAB_POOL_G, :, cs].astype(F32)
        br_ref[1, :, cs] = (mixed * _silu(pg)).astype(BF)
    ubuf_ref[0:HIST_PAD, :] = u[tm - HIST_PAD:tm]


def _xattn_probs(p, mk_ref):
    out = []
    for h in range(XA_HEADS):
        cs = slice(h * XA_HEAD_DIM, (h + 1) * XA_HEAD_DIM)
        s = _dot_nt(p[SLAB_XQ, :, cs], mk_ref[0, :, cs]) * (XA_HEAD_DIM ** -0.5)
        pr = jnp.exp(s - jnp.max(s, axis=-1, keepdims=True))
        out.append((pr.astype(BF), jnp.sum(pr, axis=-1, keepdims=True)))
    return out


def _xattn_branch(p, probs, mv_ref, br_ref):
    for h in range(XA_HEADS):
        cs = slice(h * XA_HEAD_DIM, (h + 1) * XA_HEAD_DIM)
        pr, denom = probs[h]
        o = _dot(pr, mv_ref[0, :, cs]) / denom
        xg = p[SLAB_XG, :, cs].astype(F32)
        br_ref[2, :, cs] = (o * _silu(xg)).astype(BF)


def _prompt_layer_kernel(xn_ref, xc_ref, g_ref, w_ref, walr_ref, wa2_ref, ba_ref, cum_ref, win_ref,
                         mk_ref, mv_ref, pw_ref, ps_ref, gg_ref, wbr_ref, wo_ref, fg_ref,
                         out_ref, s_out_ref, hist_out_ref,
                         h_scr, p_scr, la_scr, br_scr, st_ref, ubuf_ref, *, tm, nt, final):
    s = pl.program_id(0)
    t = jnp.maximum(s - 1, 0) % nt
    slot_w = s % 2
    slot_r = 1 - slot_w

    @pl.when(s == 0)
    def _():
        p_scr[1] = jnp.zeros(p_scr.shape[1:], p_scr.dtype)
        la_scr[1] = jnp.zeros(la_scr.shape[1:], la_scr.dtype)

    @pl.when(t == 0)
    def _():
        st_ref[...] = jnp.zeros_like(st_ref)
        ubuf_ref[0:HIST_PAD, :] = jnp.zeros((HIST_PAD, D_MODEL), F32)

    @pl.when(s == 0)
    def _():
        h_scr[0] = _rms(xc_ref[...], g_ref[...]).astype(BF)

    p_next = p_scr.at[slot_w]
    p = p_scr.at[slot_r]

    slabs = list(range(N_SLABS))

    def filler():
        _inproj_slab(h_scr[slot_w], w_ref, p_next, slabs.pop(0))

    alr = _inproj_gate_lowrank(h_scr[slot_w], walr_ref)
    bcum = _gla_cumdecay(cum_ref, la_scr.at[slot_r])
    filler()
    _inproj_gate(alr, wa2_ref, ba_ref, la_scr.at[slot_w])
    probs = None
    for c in range(tm // CHUNK):
        _gla_chunk(p, bcum, gg_ref, br_scr, st_ref, c, filler)
        filler()
        if c == 0:
            probs = _xattn_probs(p, mk_ref)
        elif c == 1:
            _xattn_branch(p, probs, mv_ref, br_scr)
        elif c == 2:
            _pool_branch(p, win_ref, pw_ref, ps_ref, br_scr, ubuf_ref, t, tm)
    h_scr[slot_r] = _rms(xn_ref[...], g_ref[...]).astype(BF)
    merged = _merge_term(br_scr, p, wbr_ref, 2)
    merged = merged + _merge_term(br_scr, p, wbr_ref, 1)
    merged = merged + _merge_term(br_scr, p, wbr_ref, 0)
    filler()
    assert not slabs
    out_ref[...] = _merge_finish(merged, xc_ref[...], wo_ref, fg_ref, final)

    @pl.when((s > 0) & (t == nt - 1))
    def _():
        for h in range(GLA_HEADS):
            s_out_ref[0, 0, h] = st_ref[h].T
        hist_out_ref[0, 0] = ubuf_ref[1:HIST_PAD, :]


def _prompt_layer(x, ngain, w_proj, w_alr, wa2, ba, mk, mv, pool_w, pool_scale, gla_gain,
                  w_branch, w_out, fgain, carry, layer, depth, batch, seq, n_mem, tm, final):
    nt = seq // tm
    n_tiles = batch * nt
    assert 2 * (tm // CHUNK) + 2 == N_SLABS, "one projection slab per dependent matmul stage of a step"
    n_alias = 0 if carry is None else len(carry)
    n_in = 17
    kern = _drop_alias_refs(functools.partial(_prompt_layer_kernel, tm=tm, nt=nt, final=final), n_in, n_alias)
    any_spec = pl.BlockSpec(memory_space=pl.ANY)

    def cur(s):
        return jnp.maximum(s - 1, 0)

    def seq_of(s):
        return cur(s) // nt

    return pl.pallas_call(
        kern,
        grid=(n_tiles + 1,),
        in_specs=[
            pl.BlockSpec((tm, D_MODEL), lambda s: (jnp.minimum(s + 1, n_tiles - 1), 0)),
            pl.BlockSpec((tm, D_MODEL), lambda s: (cur(s), 0)),
            _const_spec((1, D_MODEL)),
            _layer_spec((D_MODEL // 2, N_SLABS * D_MODEL), layer),
            _const_spec((GATE_RANK, D_MODEL)),
            _const_spec((GATE_RANK, GLA_KEY_WIDTH)),
            _const_spec((1, GLA_KEY_WIDTH)),
            _const_spec((tm, tm)),
            _const_spec((len(POOL_WINDOWS), tm, tm)),
            pl.BlockSpec((1, n_mem, D_MODEL), lambda s: (layer, seq_of(s), 0)),
            pl.BlockSpec((1, n_mem, D_MODEL), lambda s: (layer, seq_of(s), 0)),
            _const_spec((len(POOL_WINDOWS), POOL_GROUP_DIM, POOL_GROUP_DIM)),
            _const_spec((1, D_MODEL)),
            _const_spec((1, D_MODEL)),
            _layer_spec((N_BRANCH, D_MODEL // 2, D_MODEL), layer),
            _layer_spec((D_MODEL // 2, D_MODEL), layer),
            _const_spec((1, D_MODEL)),
        ] + [any_spec] * n_alias,
        out_specs=[
            pl.BlockSpec((tm, D_MODEL), lambda s: (cur(s), 0)),
            pl.BlockSpec((1, 1, GLA_HEADS, GLA_DK, GLA_DV), lambda s: (layer, seq_of(s), 0, 0, 0)),
            pl.BlockSpec((1, 1, POOL_HIST, D_MODEL), lambda s: (layer, seq_of(s), 0, 0)),
        ],
        out_shape=[
            jax.ShapeDtypeStruct((n_tiles * tm, D_MODEL), F32),
            jax.ShapeDtypeStruct((depth, batch, GLA_HEADS, GLA_DK, GLA_DV), F32),
            jax.ShapeDtypeStruct((depth, batch, POOL_HIST, D_MODEL), F32),
        ],
        scratch_shapes=[
            pltpu.VMEM((2, tm, D_MODEL), BF),
            pltpu.VMEM((2, N_SLABS, tm, D_MODEL), BF),
            pltpu.VMEM((2, tm, GLA_KEY_WIDTH), F32),
            pltpu.VMEM((N_BRANCH, tm, D_MODEL), BF),
            pltpu.VMEM((GLA_HEADS, GLA_DV, GLA_DK), F32),
            pltpu.VMEM((2 * HIST_PAD, D_MODEL), F32),
        ],
        input_output_aliases={n_in + a: 1 + a for a in range(n_alias)},
        compiler_params=_params(("arbitrary",), 60),
        name="prompt_layer",
    )(x, x, ngain, w_proj, w_alr, wa2, ba, _chunk_cumsum_matrix(tm), _window_matrices(tm),
      mk, mv, pool_w, pool_scale, gla_gain,
      w_branch, w_out, fgain, *(carry or ()))


def _inproj_kernel(x_ref, g_ref, w_ref, walr_ref, wa2_ref, ba_ref, p_ref, la_ref):
    h = _rms(x_ref[...], g_ref[...]).astype(BF)
    alr = _inproj_gate_lowrank(h, walr_ref)
    for j in range(N_SLABS):
        _inproj_slab(h, w_ref, p_ref, j)
    _inproj_gate(alr, wa2_ref, ba_ref, la_ref)


def _inproj(x, gain, w_proj, w_alr, w_a2, b_a, layer, tm, out_dtype):
    m_rows = x.shape[0]
    return pl.pallas_call(
        _inproj_kernel,
        grid=(m_rows // tm,),
        in_specs=[
            pl.BlockSpec((tm, D_MODEL), lambda i: (i, 0)),
            _const_spec((1, D_MODEL)),
            _layer_spec((D_MODEL // 2, N_SLABS * D_MODEL), layer),
            _const_spec((GATE_RANK, D_MODEL)),
            _const_spec((GATE_RANK, GLA_KEY_WIDTH)),
            _const_spec((1, GLA_KEY_WIDTH)),
        ],
        out_specs=[
            pl.BlockSpec((N_SLABS, tm, D_MODEL), lambda i: (0, i, 0)),
            pl.BlockSpec((tm, GLA_KEY_WIDTH), lambda i: (i, 0)),
        ],
        out_shape=[
            jax.ShapeDtypeStruct((N_SLABS, m_rows, D_MODEL), out_dtype),
            jax.ShapeDtypeStruct((m_rows, GLA_KEY_WIDTH), F32),
        ],
        compiler_params=_params(("parallel",), 48),
        name="inproj",
    )(x, gain, w_proj, w_alr, w_a2, b_a)


def _sample_mix_kernel(p_ref, la_ref, s0_ref, hist_ref, ck_ref, cv_ref, pw_ref, ps_ref, gg_ref,
                       br_ref, s_out_ref, hist_out_ref, diff_ref, *, sb):
    r0 = pl.program_id(1) * sb
    erow = lax.broadcasted_iota(jnp.int32, (GLA_DK, GLA_DK), 0)
    ecol = lax.broadcasted_iota(jnp.int32, (GLA_DK, GLA_DK), 1)
    eye = erow == ecol

    def to_col(x):
        return jnp.sum(jnp.where(eye, jnp.broadcast_to(x, (GLA_DK, GLA_DK)), 0.0), axis=1, keepdims=True)

    for i in range(sb):
        r = pl.ds(r0 + i, 1)
        la = la_ref[r, :]
        qk = p_ref[SLAB_QK, r, :]
        vv = p_ref[SLAB_V, r, :]
        gla_g = p_ref[SLAB_GLA_G, r, :]
        for h in range(GLA_HEADS):
            kc = slice(h * GLA_DK, (h + 1) * GLA_DK)
            kc2 = slice(GLA_KEY_WIDTH + h * GLA_DK, GLA_KEY_WIDTH + (h + 1) * GLA_DK)
            vc = slice(h * GLA_DV, (h + 1) * GLA_DV)
            a_col = to_col(jnp.exp(la[:, kc]))
            q_col = to_col(qk[:, kc] * (GLA_DK ** -0.5))
            k_col = to_col(qk[:, kc2])
            s_new = a_col * s0_ref[0, i, h] + k_col * vv[:, vc]
            s_out_ref[0, i, h] = s_new
            o = jnp.sum(q_col * s_new, axis=0, keepdims=True)
            br_ref[0, r, vc] = _rms(o, gg_ref[:, vc]) * _silu(gla_g[:, vc])

        u = p_ref[SLAB_U, r, :]
        for g, w in enumerate(POOL_WINDOWS):
            cs = slice(g * POOL_GROUP_DIM, (g + 1) * POOL_GROUP_DIM)
            past = jnp.sum(hist_ref[0, POOL_HIST - (w - 1):POOL_HIST, r, cs], axis=0)
            diff_ref[r, cs] = (u[:, cs] + past) / float(w) - u[:, cs]
        hist_out_ref[0, 0:POOL_HIST - 1, r, :] = hist_ref[0, 1:POOL_HIST, r, :]
        hist_out_ref[0, POOL_HIST - 1, r, :] = u

        xq = p_ref[SLAB_XQ, r, :]
        xg = p_ref[SLAB_XG, r, :]
        half_cols = [slice(h * XA_HEAD_DIM + j * LANES, h * XA_HEAD_DIM + (j + 1) * LANES)
                     for j in range(XA_HEAD_DIM // LANES) for h in range(XA_HEADS)]
        xq_rows = jnp.concatenate([xq[:, cs] for cs in half_cols], axis=0)
        n_mem = ck_ref.shape[2]
        prod = (ck_ref[0, i] * xq_rows[None]).reshape(n_mem * SUBLANES, LANES).astype(BF)
        part = _dot(prod, jnp.ones((LANES, LANES), BF)).reshape(n_mem, SUBLANES, LANES)
        s = (part + pltpu.roll(part, XA_HEADS, axis=1)) * (XA_HEAD_DIM ** -0.5)
        p = jnp.exp(s - jnp.max(s, axis=0, keepdims=True))
        o = jnp.sum(p * cv_ref[0, i], axis=0) / jnp.sum(p, axis=0)
        halves = XA_HEAD_DIM // LANES
        o_row = jnp.concatenate([o[j * XA_HEADS + h:j * XA_HEADS + h + 1, :]
                                 for h in range(XA_HEADS) for j in range(halves)], axis=1)
        br_ref[2, r, :] = o_row * _silu(xg)

    @pl.when(pl.program_id(1) == pl.num_programs(1) - 1)
    def _():
        for g in range(len(POOL_WINDOWS)):
            cs = slice(g * POOL_GROUP_DIM, (g + 1) * POOL_GROUP_DIM)
            mixed = _dot(diff_ref[:, cs].astype(BF), pw_ref[g]) * ps_ref[:, cs]
            br_ref[1, :, cs] = mixed * _silu(p_ref[SLAB_POOL_G, :, cs])


def _cache_rows_view(c):
    depth, nb, n_mem = c.shape[:3]
    halves = XA_HEAD_DIM // LANES
    c = c.reshape(depth, nb, n_mem, XA_HEADS, halves, LANES)
    return c.transpose(0, 1, 2, 4, 3, 5).reshape(depth, nb, n_mem, halves * XA_HEADS, LANES)


def _sample_mix(p, la, s0, hist, ck, cv, pool_w, pool_scale, gla_gain, carry, layer, sb=4):
    nb = s0.shape[1]
    n_mem = ck.shape[2]
    ck, cv = _cache_rows_view(ck), _cache_rows_view(cv)
    rb = SUBLANES
    halves = rb // sb
    n_alias = 0 if carry is None else len(carry)
    n_in = 9
    kern = _drop_alias_refs(functools.partial(_sample_mix_kernel, sb=sb), n_in, n_alias)
    any_spec = pl.BlockSpec(memory_space=pl.ANY)
    return pl.pallas_call(
        kern,
        grid=(nb // rb, halves),
        in_specs=[
            pl.BlockSpec((N_MIX_SLABS, rb, D_MODEL), lambda i, j: (0, i, 0)),
            pl.BlockSpec((rb, GLA_KEY_WIDTH), lambda i, j: (i, 0)),
            pl.BlockSpec((1, sb, GLA_HEADS, GLA_DK, GLA_DV), lambda i, j: (layer, i * halves + j, 0, 0, 0)),
            pl.BlockSpec((1, POOL_HIST, rb, D_MODEL), lambda i, j: (layer, 0, i, 0)),
            pl.BlockSpec((1, sb, n_mem, SUBLANES, LANES), lambda i, j: (layer, i * halves + j, 0, 0, 0)),
            pl.BlockSpec((1, sb, n_mem, SUBLANES, LANES), lambda i, j: (layer, i * halves + j, 0, 0, 0)),
            _const_spec((len(POOL_WINDOWS), POOL_GROUP_DIM, POOL_GROUP_DIM)),
            _const_spec((1, D_MODEL)),
            _const_spec((1, D_MODEL)),
        ] + [any_spec] * n_alias,
        out_specs=[
            pl.BlockSpec((N_BRANCH, rb, D_MODEL), lambda i, j: (0, i, 0)),
            pl.BlockSpec((1, sb, GLA_HEADS, GLA_DK, GLA_DV), lambda i, j: (layer, i * halves + j, 0, 0, 0)),
            pl.BlockSpec((1, POOL_HIST, rb, D_MODEL), lambda i, j: (layer, 0, i, 0)),
        ],
        out_shape=[
            jax.ShapeDtypeStruct((N_BRANCH, nb, D_MODEL), F32),
            jax.ShapeDtypeStruct(s0.shape, F32),
            jax.ShapeDtypeStruct(hist.shape, F32),
        ],
        scratch_shapes=[pltpu.VMEM((rb, D_MODEL), F32)],
        input_output_aliases={n_in + a: 1 + a for a in range(n_alias)},
        compiler_params=_params(("parallel", "arbitrary"), 40),
        name="sample_mix",
    )(p, la, s0, hist, ck, cv, pool_w, pool_scale, gla_gain, *(carry or ()))


def _merge_kernel(br_ref, p_ref, x_ref, wbr_ref, wo_ref, fg_ref, out_ref, *, final):
    merged = _merge_term(br_ref, p_ref, wbr_ref, 0)
    for n in range(1, N_BRANCH):
        merged = merged + _merge_term(br_ref, p_ref, wbr_ref, n)
    out_ref[...] = _merge_finish(merged, x_ref[...], wo_ref, fg_ref, final)


def _merge_out(br, p, x, w_branch, w_out, final_gain, layer, final):
    m_rows = x.shape[0]
    whole = lambda shape: pl.BlockSpec(shape, lambda i: (0,) * len(shape))
    return pl.pallas_call(
        functools.partial(_merge_kernel, final=final),
        grid=(1,),
        in_specs=[
            whole((N_BRANCH, m_rows, D_MODEL)),
            whole((N_SLABS, m_rows, D_MODEL)),
            whole((m_rows, D_MODEL)),
            _layer_spec((N_BRANCH, D_MODEL // 2, D_MODEL), layer),
            _layer_spec((D_MODEL // 2, D_MODEL), layer),
            _const_spec((1, D_MODEL)),
        ],
        out_specs=whole((m_rows, D_MODEL)),
        out_shape=jax.ShapeDtypeStruct((m_rows, D_MODEL), F32),
        compiler_params=_params(("arbitrary",), 40),
        name="merge_out",
    )(br, p, x, w_branch, w_out, final_gain)


def kernel(x_prompt, x_sample, mem_prompt, cache_mem_k, cache_mem_v, state_gla, state_pool, w_in, w_a2, b_a, gla_gain, pool_w, pool_scale, w_mk, w_mv, w_branch, w_out, norm_gain, final_gain):
    batch, seq, _ = x_prompt.shape
    nb = x_sample.shape[0]
    n_mem = mem_prompt.shape[1]
    depth = w_in.shape[0]
    tm = min(256, seq)

    xp = x_prompt.reshape(batch * seq, D_MODEL)
    xs = x_sample.reshape(nb, D_MODEL)
    mem = mem_prompt.reshape(batch * n_mem, D_MODEL)
    fgain = final_gain.reshape(1, D_MODEL)

    mk, mv, mk_bf, mv_bf = _kvproj(mem, w_mk.astype(BF), w_mv.astype(BF), batch, n_mem)

    w_in_t = jnp.swapaxes(w_in, 1, 2)
    pool_rows = jnp.swapaxes(state_pool, 1, 2)
    w_proj = _pack_w_in(w_in_t)
    wb = _pack_matrices(w_branch.reshape(depth * N_BRANCH, D_MODEL, D_MODEL))
    wb = wb.reshape(depth, N_BRANCH, D_MODEL // 2, D_MODEL)
    wo = _pack_matrices(w_out)

    carry_p, carry_s = None, None
    for l in range(depth):
        final = l == depth - 1
        w_alr = w_in_t[l, ALR_START:ALR_START + GATE_RANK, :]
        wa2 = w_a2[l].astype(BF)
        ba = b_a[l].reshape(1, GLA_KEY_WIDTH)
        ngain = norm_gain[l].reshape(1, D_MODEL)
        ggain = gla_gain[l].reshape(1, D_MODEL)
        pscale = pool_scale[l].reshape(1, D_MODEL)
        pw = pool_w[l].astype(BF)

        xp, s_all, hist_all = _prompt_layer(xp, ngain, w_proj, w_alr, wa2, ba, mk_bf, mv_bf, pw, pscale,
                                            ggain, wb, wo, fgain, carry_p, l, depth, batch, seq, n_mem, tm, final)
        carry_p = (s_all, hist_all)

        ps, las = _inproj(xs, ngain, w_proj, w_alr, wa2, ba, l, nb, F32)
        brs, s_new, hist_new = _sample_mix(ps, las, state_gla, pool_rows, cache_mem_k, cache_mem_v,
                                           pw, pscale, ggain, carry_s, l)
        carry_s = (s_new, hist_new)
        xs = _merge_out(brs, ps, xs, wb, wo, fgain, l, final)

    return (xp.reshape(batch, seq, D_MODEL), xs.reshape(nb, 1, D_MODEL),
            mk, mv, carry_p[0], carry_p[1], carry_s[0], jnp.swapaxes(carry_s[1], 1, 2))
```

```python
import functools

import jax
import jax.numpy as jnp
from jax import lax
from jax.experimental import pallas as pl
from jax.experimental.pallas import tpu as pltpu

D_MODEL = 1024
GLA_HEADS = 4
GLA_DK = 128
GLA_DV = 256
GLA_KEY_WIDTH = GLA_HEADS * GLA_DK
GATE_RANK = 16
GATE_TAU = 16.0
CHUNK = 64
POOL_WINDOWS = (2, 4, 8, 16)
POOL_GROUP_DIM = 256
POOL_HIST = 15
HIST_PAD = 16
XA_HEADS = 4
XA_HEAD_DIM = 256
N_BRANCH = 3
EPS = 1e-6
SUBLANES = 8
LANES = 128

SLAB_QK, SLAB_V, SLAB_GLA_G, SLAB_U, SLAB_POOL_G, SLAB_XQ, SLAB_XG, SLAB_MERGE = 0, 1, 2, 3, 4, 5, 6, 7
N_SLABS = 10
N_MIX_SLABS = 7
N_HEAD_SLABS = 3
ALR_START = N_HEAD_SLABS * D_MODEL

BF = jnp.bfloat16
F32 = jnp.float32
MIB = 1 << 20


def _dot(a, b):
    return jnp.dot(a, b, preferred_element_type=F32)


def _dot_nt(a, b):
    return lax.dot_general(a, b, (((1,), (1,)), ((), ())), preferred_element_type=F32)


def _dot_tn(a, b):
    return lax.dot_general(a, b, (((0,), (0,)), ((), ())), preferred_element_type=F32)


def _pack_rows(w):
    return pltpu.bitcast(w.astype(BF), jnp.uint32)


def _unpack_rows(w_words):
    return pltpu.bitcast(w_words, BF)


def _params(sem, vmem_mib):
    return pltpu.CompilerParams(dimension_semantics=sem, vmem_limit_bytes=vmem_mib * MIB)


def _pack_w_in_kernel(a_ref, b_ref, o_ref):
    j = pl.program_id(1)

    @pl.when(j < N_HEAD_SLABS)
    def _():
        o_ref[0] = _pack_rows(a_ref[0].T)

    @pl.when(j >= N_HEAD_SLABS)
    def _():
        o_ref[0] = _pack_rows(jnp.concatenate([a_ref[0, GATE_RANK:, :], b_ref[0]], axis=0).T)


def _pack_w_in(w_in_t):
    depth = w_in_t.shape[0]
    return pl.pallas_call(
        _pack_w_in_kernel,
        grid=(depth, N_SLABS),
        in_specs=[
            pl.BlockSpec((1, D_MODEL, D_MODEL), lambda l, j: (l, j, 0)),
            pl.BlockSpec((1, GATE_RANK, D_MODEL), lambda l, j: (l, (j + 1) * (D_MODEL // GATE_RANK), 0)),
        ],
        out_specs=pl.BlockSpec((1, D_MODEL // 2, D_MODEL), lambda l, j: (l, 0, j)),
        out_shape=jax.ShapeDtypeStruct((depth, D_MODEL // 2, N_SLABS * D_MODEL), jnp.uint32),
        compiler_params=_params(("parallel", "parallel"), 32),
        name="pack_w_in",
    )(w_in_t, w_in_t)


def _pack_matrices_kernel(a_ref, o_ref):
    o_ref[0] = _pack_rows(a_ref[0])


def _pack_matrices(w):
    return pl.pallas_call(
        _pack_matrices_kernel,
        grid=(w.shape[0],),
        in_specs=[pl.BlockSpec((1, D_MODEL, D_MODEL), lambda r: (r, 0, 0))],
        out_specs=pl.BlockSpec((1, D_MODEL // 2, D_MODEL), lambda r: (r, 0, 0)),
        out_shape=jax.ShapeDtypeStruct((w.shape[0], D_MODEL // 2, D_MODEL), jnp.uint32),
        compiler_params=_params(("parallel",), 32),
        name="pack_matrices",
    )(w)


def _silu(x):
    return x * jax.nn.sigmoid(x)


def _rms(x, gain):
    ms = jnp.mean(x * x, axis=-1, keepdims=True)
    return x * lax.rsqrt(ms + EPS) * gain


def _const_spec(shape):
    zeros = (0,) * len(shape)
    return pl.BlockSpec(shape, lambda *_: zeros, pipeline_mode=pl.Buffered(1))


def _layer_spec(shape, layer):
    index = (layer,) + (0,) * len(shape)
    return pl.BlockSpec((1,) + tuple(shape), lambda *_: index, pipeline_mode=pl.Buffered(1))


def _drop_alias_refs(body, n_in, n_alias):
    def kern(*refs):
        return body(*refs[:n_in], *refs[n_in + n_alias:])
    return kern


def _kvproj_kernel(m_ref, wk_ref, wv_ref, k_ref, v_ref, kb_ref, vb_ref):
    m = m_ref[...].astype(BF)
    k = _dot(m, wk_ref[0])
    v = _dot(m, wv_ref[0])
    for h in range(XA_HEADS):
        cs = slice(h * XA_HEAD_DIM, (h + 1) * XA_HEAD_DIM)
        k_ref[0, 0, :, h, :] = k[:, cs]
        v_ref[0, 0, :, h, :] = v[:, cs]
    kb_ref[0] = k.astype(BF)
    vb_ref[0] = v.astype(BF)


def _kvproj(mem, wk, wv, batch, n_mem):
    depth = wk.shape[0]
    w_spec = pl.BlockSpec((1, D_MODEL, D_MODEL), lambda l, b: (l, 0, 0))
    out5 = pl.BlockSpec((1, 1, n_mem, XA_HEADS, XA_HEAD_DIM), lambda l, b: (l, b, 0, 0, 0))
    out_bf = pl.BlockSpec((1, n_mem, D_MODEL), lambda l, b: (l, b, 0))
    return pl.pallas_call(
        _kvproj_kernel,
        grid=(depth, batch),
        in_specs=[pl.BlockSpec((n_mem, D_MODEL), lambda l, b: (b, 0)), w_spec, w_spec],
        out_specs=[out5, out5, out_bf, out_bf],
        out_shape=[jax.ShapeDtypeStruct((depth, batch, n_mem, XA_HEADS, XA_HEAD_DIM), F32)] * 2
        + [jax.ShapeDtypeStruct((depth, batch * n_mem, D_MODEL), BF)] * 2,
        compiler_params=_params(("parallel", "parallel"), 32),
        name="kvproj",
    )(mem, wk, wv)


def _inproj_slab(h, w_ref, p_out, j):
    w = w_ref[0, :, j * D_MODEL:(j + 1) * D_MODEL]
    p_out[j] = _dot(h, _unpack_rows(w)).astype(p_out.dtype)


def _inproj_gate_lowrank(h, walr_ref):
    return _dot_nt(h, walr_ref[...].astype(BF)).astype(BF)


def _inproj_gate(alr, wa2_ref, ba_ref, la_out):
    z = _dot(alr, wa2_ref[...]) + ba_ref[...]
    la_out[...] = (jnp.minimum(z, 0.0) - jnp.log(1.0 + jnp.exp(-jnp.abs(z)))) * (1.0 / GATE_TAU)


def _merge_term(br_ref, p, wbr_ref, n):
    return jax.nn.sigmoid(p[SLAB_MERGE + n].astype(F32)) * _dot(br_ref[n].astype(BF), _unpack_rows(wbr_ref[0, n]))


def _merge_finish(merged, x, wo_ref, fg_ref, final):
    x_new = x + _dot(merged.astype(BF), _unpack_rows(wo_ref[0]))
    return _rms(x_new, fg_ref[...]) if final else x_new


def _chunk_cumsum_matrix(tm):
    row = lax.broadcasted_iota(jnp.int32, (tm, tm), 0)
    col = lax.broadcasted_iota(jnp.int32, (tm, tm), 1)
    return (((row // CHUNK) == (col // CHUNK)) & (row >= col)).astype(BF)


def _window_matrices(tm):
    row = lax.broadcasted_iota(jnp.int32, (tm, tm), 0)
    col = lax.broadcasted_iota(jnp.int32, (tm, tm), 1)
    return jnp.stack([((row >= col) & (row - col < w)).astype(BF) for w in POOL_WINDOWS])


def _gla_cumdecay(cum_ref, la_ref):
    la = la_ref[...]
    la_hi = la.astype(BF)
    la_lo = (la - la_hi.astype(F32)).astype(BF)
    return _dot(cum_ref[...], la_hi) + _dot(cum_ref[...], la_lo)


def _gla_chunk(p, bcum, gg_ref, br_ref, st_ref, c, filler):
    crow = lax.broadcasted_iota(jnp.int32, (CHUNK, CHUNK), 0)
    ccol = lax.broadcasted_iota(jnp.int32, (CHUNK, CHUNK), 1)
    causal = crow >= ccol
    rows = slice(c * CHUNK, (c + 1) * CHUNK)
    heads = range(GLA_HEADS)
    q_dec, k_end, decay, att = [], [], [], []
    for h in heads:
        kc = slice(h * GLA_DK, (h + 1) * GLA_DK)
        kc2 = slice(GLA_KEY_WIDTH + h * GLA_DK, GLA_KEY_WIDTH + (h + 1) * GLA_DK)
        b = bcum[rows, kc]
        b_last = b[CHUNK - 1:CHUNK, :]
        q = p[SLAB_QK, rows, kc].astype(F32)
        k = p[SLAB_QK, rows, kc2].astype(F32)
        q_dec.append((q * (GLA_DK ** -0.5) * jnp.exp(b)).astype(BF))
        k_inv = (k * jnp.exp(-b)).astype(BF)
        k_end.append((k * jnp.exp(b_last - b)).astype(BF))
        decay.append(jnp.exp(b_last))
        att.append(_dot_nt(q_dec[h], k_inv))
    filler()
    o = []
    for h in heads:
        vc = slice(h * GLA_DV, (h + 1) * GLA_DV)
        a = jnp.where(causal, att[h], 0.0).astype(BF)
        o.append(_dot(a, p[SLAB_V, rows, vc]) + _dot_nt(q_dec[h], st_ref[h].astype(BF)))
    for h in heads:
        vc = slice(h * GLA_DV, (h + 1) * GLA_DV)
        st_ref[h] = decay[h] * st_ref[h] + _dot_tn(p[SLAB_V, rows, vc], k_end[h])
    for h in heads:
        vc = slice(h * GLA_DV, (h + 1) * GLA_DV)
        g = p[SLAB_GLA_G, rows, vc].astype(F32)
        br_ref[0, rows, vc] = (_rms(o[h], gg_ref[:, vc]) * _silu(g)).astype(BF)


def _pool_window_sums(p, win_ref):
    sums = []
    for g in range(len(POOL_WINDOWS)):
        cs = slice(g * POOL_GROUP_DIM, (g + 1) * POOL_GROUP_DIM)
        sums.append(_dot(win_ref[g], p[SLAB_U, :, cs]))
    return sums


def _pool_branch(p, sums, pw_ref, ps_ref, br_ref, ubuf_ref, t, tm):
    u = p[SLAB_U].astype(F32)
    ubuf_ref[HIST_PAD:2 * HIST_PAD, :] = u[0:HIST_PAD]
    pos = t * tm + lax.broadcasted_iota(jnp.int32, (tm, 1), 0)
    for g, w in enumerate(POOL_WINDOWS):
        cs = slice(g * POOL_GROUP_DIM, (g + 1) * POOL_GROUP_DIM)
        ug = u[:, cs]
        head = ug[0:HIST_PAD]
        for j in range(1, w):
            head = head + ubuf_ref[HIST_PAD - j:2 * HIST_PAD - j, cs]
        s = jnp.concatenate([head, sums[g][HIST_PAD:]], axis=0)
        cnt = jnp.minimum(w, pos + 1).astype(F32)
        diff = s / cnt - ug
        mixed = _dot(diff.astype(BF), pw_ref[g]) * ps_ref[:, cs]
        pg = p[SLAB_POOL_G, :, cs].astype(F32)
        br_ref[1, :, cs] = (mixed * _silu(pg)).astype(BF)
    ubuf_ref[0:HIST_PAD, :] = u[tm - HIST_PAD:tm]


def _xattn_probs(p, mk_ref):
    out = []
    for h in range(XA_HEADS):
        cs = slice(h * XA_HEAD_DIM, (h + 1) * XA_HEAD_DIM)
        s = _dot_nt(p[SLAB_XQ, :, cs], mk_ref[0, :, cs]) * (XA_HEAD_DIM ** -0.5)
        pr = jnp.exp(s - jnp.max(s, axis=-1, keepdims=True))
        out.append((pr.astype(BF), jnp.sum(pr, axis=-1, keepdims=True)))
    return out


def _xattn_branch(p, probs, mv_ref, br_ref):
    for h in range(XA_HEADS):
        cs = slice(h * XA_HEAD_DIM, (h + 1) * XA_HEAD_DIM)
        pr, denom = probs[h]
        o = _dot(pr, mv_ref[0, :, cs]) / denom
        xg = p[SLAB_XG, :, cs].astype(F32)
        br_ref[2, :, cs] = (o * _silu(xg)).astype(BF)


def _prompt_layer_kernel(xn_ref, xc_ref, g_ref, w_ref, walr_ref, wa2_ref, ba_ref, cum_ref, win_ref,
                         mk_ref, mv_ref, pw_ref, ps_ref, gg_ref, wbr_ref, wo_ref, fg_ref,
                         out_ref, s_out_ref, hist_out_ref,
                         h_scr, p_scr, la_scr, br_scr, st_ref, ubuf_ref, *, tm, nt, final):
    s = pl.program_id(0)
    t = jnp.maximum(s - 1, 0) % nt
    slot_w = s % 2
    slot_r = 1 - slot_w

    @pl.when(s == 0)
    def _():
        p_scr[1] = jnp.zeros(p_scr.shape[1:], p_scr.dtype)
        la_scr[1] = jnp.zeros(la_scr.shape[1:], la_scr.dtype)

    @pl.when(t == 0)
    def _():
        st_ref[...] = jnp.zeros_like(st_ref)
        ubuf_ref[0:HIST_PAD, :] = jnp.zeros((HIST_PAD, D_MODEL), F32)

    @pl.when(s == 0)
    def _():
        h_scr[0] = _rms(xc_ref[...], g_ref[...]).astype(BF)

    p_next = p_scr.at[slot_w]
    p = p_scr.at[slot_r]

    slabs = list(range(N_SLABS))

    def filler():
        _inproj_slab(h_scr[slot_w], w_ref, p_next, slabs.pop(0))

    alr = _inproj_gate_lowrank(h_scr[slot_w], walr_ref)
    bcum = _gla_cumdecay(cum_ref, la_scr.at[slot_r])
    filler()
    _inproj_gate(alr, wa2_ref, ba_ref, la_scr.at[slot_w])
    probs, sums = None, None
    for c in range(tm // CHUNK):
        _gla_chunk(p, bcum, gg_ref, br_scr, st_ref, c, filler)
        filler()
        if c == 0:
            probs = _xattn_probs(p, mk_ref)
        elif c == 1:
            _xattn_branch(p, probs, mv_ref, br_scr)
            sums = _pool_window_sums(p, win_ref)
        elif c == 2:
            _pool_branch(p, sums, pw_ref, ps_ref, br_scr, ubuf_ref, t, tm)
    h_scr[slot_r] = _rms(xn_ref[...], g_ref[...]).astype(BF)
    merged = _merge_term(br_scr, p, wbr_ref, 2)
    merged = merged + _merge_term(br_scr, p, wbr_ref, 1)
    merged = merged + _merge_term(br_scr, p, wbr_ref, 0)
    filler()
    assert not slabs
    out_ref[...] = _merge_finish(merged, xc_ref[...], wo_ref, fg_ref, final)

    @pl.when((s > 0) & (t == nt - 1))
    def _():
        for h in range(GLA_HEADS):
            s_out_ref[0, 0, h] = st_ref[h].T
        hist_out_ref[0, 0] = ubuf_ref[1:HIST_PAD, :]


def _prompt_layer(x, ngain, w_proj, w_alr, wa2, ba, mk, mv, pool_w, pool_scale, gla_gain,
                  w_branch, w_out, fgain, carry, layer, depth, batch, seq, n_mem, tm, final):
    nt = seq // tm
    n_tiles = batch * nt
    assert 2 * (tm // CHUNK) + 2 == N_SLABS, "one projection slab per dependent matmul stage of a step"
    n_alias = 0 if carry is None else len(carry)
    n_in = 17
    kern = _drop_alias_refs(functools.partial(_prompt_layer_kernel, tm=tm, nt=nt, final=final), n_in, n_alias)
    any_spec = pl.BlockSpec(memory_space=pl.ANY)

    def cur(s):
        return jnp.maximum(s - 1, 0)

    def seq_of(s):
        return cur(s) // nt

    return pl.pallas_call(
        kern,
        grid=(n_tiles + 1,),
        in_specs=[
            pl.BlockSpec((tm, D_MODEL), lambda s: (jnp.minimum(s + 1, n_tiles - 1), 0)),
            pl.BlockSpec((tm, D_MODEL), lambda s: (cur(s), 0)),
            _const_spec((1, D_MODEL)),
            _layer_spec((D_MODEL // 2, N_SLABS * D_MODEL), layer),
            _const_spec((GATE_RANK, D_MODEL)),
            _const_spec((GATE_RANK, GLA_KEY_WIDTH)),
            _const_spec((1, GLA_KEY_WIDTH)),
            _const_spec((tm, tm)),
            _const_spec((len(POOL_WINDOWS), tm, tm)),
            pl.BlockSpec((1, n_mem, D_MODEL), lambda s: (layer, seq_of(s), 0)),
            pl.BlockSpec((1, n_mem, D_MODEL), lambda s: (layer, seq_of(s), 0)),
            _const_spec((len(POOL_WINDOWS), POOL_GROUP_DIM, POOL_GROUP_DIM)),
            _const_spec((1, D_MODEL)),
            _const_spec((1, D_MODEL)),
            _layer_spec((N_BRANCH, D_MODEL // 2, D_MODEL), layer),
            _layer_spec((D_MODEL // 2, D_MODEL), layer),
            _const_spec((1, D_MODEL)),
        ] + [any_spec] * n_alias,
        out_specs=[
            pl.BlockSpec((tm, D_MODEL), lambda s: (cur(s), 0)),
            pl.BlockSpec((1, 1, GLA_HEADS, GLA_DK, GLA_DV), lambda s: (layer, seq_of(s), 0, 0, 0)),
            pl.BlockSpec((1, 1, POOL_HIST, D_MODEL), lambda s: (layer, seq_of(s), 0, 0)),
        ],
        out_shape=[
            jax.ShapeDtypeStruct((n_tiles * tm, D_MODEL), F32),
            jax.ShapeDtypeStruct((depth, batch, GLA_HEADS, GLA_DK, GLA_DV), F32),
            jax.ShapeDtypeStruct((depth, batch, POOL_HIST, D_MODEL), F32),
        ],
        scratch_shapes=[
            pltpu.VMEM((2, tm, D_MODEL), BF),
            pltpu.VMEM((2, N_SLABS, tm, D_MODEL), BF),
            pltpu.VMEM((2, tm, GLA_KEY_WIDTH), F32),
            pltpu.VMEM((N_BRANCH, tm, D_MODEL), BF),
            pltpu.VMEM((GLA_HEADS, GLA_DV, GLA_DK), F32),
            pltpu.VMEM((2 * HIST_PAD, D_MODEL), F32),
        ],
        input_output_aliases={n_in + a: 1 + a for a in range(n_alias)},
        compiler_params=_params(("arbitrary",), 60),
        name="prompt_layer",
    )(x, x, ngain, w_proj, w_alr, wa2, ba, _chunk_cumsum_matrix(tm), _window_matrices(tm),
      mk, mv, pool_w, pool_scale, gla_gain,
      w_branch, w_out, fgain, *(carry or ()))


def _inproj_kernel(x_ref, g_ref, w_ref, walr_ref, wa2_ref, ba_ref, p_ref, la_ref):
    h = _rms(x_ref[...], g_ref[...]).astype(BF)
    alr = _inproj_gate_lowrank(h, walr_ref)
    for j in range(N_SLABS):
        _inproj_slab(h, w_ref, p_ref, j)
    _inproj_gate(alr, wa2_ref, ba_ref, la_ref)


def _inproj(x, gain, w_proj, w_alr, w_a2, b_a, layer, tm, out_dtype):
    m_rows = x.shape[0]
    return pl.pallas_call(
        _inproj_kernel,
        grid=(m_rows // tm,),
        in_specs=[
            pl.BlockSpec((tm, D_MODEL), lambda i: (i, 0)),
            _const_spec((1, D_MODEL)),
            _layer_spec((D_MODEL // 2, N_SLABS * D_MODEL), layer),
            _const_spec((GATE_RANK, D_MODEL)),
            _const_spec((GATE_RANK, GLA_KEY_WIDTH)),
            _const_spec((1, GLA_KEY_WIDTH)),
        ],
        out_specs=[
            pl.BlockSpec((N_SLABS, tm, D_MODEL), lambda i: (0, i, 0)),
            pl.BlockSpec((tm, GLA_KEY_WIDTH), lambda i: (i, 0)),
        ],
        out_shape=[
            jax.ShapeDtypeStruct((N_SLABS, m_rows, D_MODEL), out_dtype),
            jax.ShapeDtypeStruct((m_rows, GLA_KEY_WIDTH), F32),
        ],
        compiler_params=_params(("parallel",), 48),
        name="inproj",
    )(x, gain, w_proj, w_alr, w_a2, b_a)


def _sample_mix_kernel(p_ref, la_ref, s0_ref, hist_ref, ck_ref, cv_ref, pw_ref, ps_ref, gg_ref,
                       br_ref, s_out_ref, hist_out_ref, diff_ref, *, sb):
    r0 = pl.program_id(1) * sb
    erow = lax.broadcasted_iota(jnp.int32, (GLA_DK, GLA_DK), 0)
    ecol = lax.broadcasted_iota(jnp.int32, (GLA_DK, GLA_DK), 1)
    eye = erow == ecol

    def to_col(x):
        return jnp.sum(jnp.where(eye, jnp.broadcast_to(x, (GLA_DK, GLA_DK)), 0.0), axis=1, keepdims=True)

    for i in range(sb):
        r = pl.ds(r0 + i, 1)
        la = la_ref[r, :]
        qk = p_ref[SLAB_QK, r, :]
        vv = p_ref[SLAB_V, r, :]
        gla_g = p_ref[SLAB_GLA_G, r, :]
        for h in range(GLA_HEADS):
            kc = slice(h * GLA_DK, (h + 1) * GLA_DK)
            kc2 = slice(GLA_KEY_WIDTH + h * GLA_DK, GLA_KEY_WIDTH + (h + 1) * GLA_DK)
            vc = slice(h * GLA_DV, (h + 1) * GLA_DV)
            a_col = to_col(jnp.exp(la[:, kc]))
            q_col = to_col(qk[:, kc] * (GLA_DK ** -0.5))
            k_col = to_col(qk[:, kc2])
            s_new = a_col * s0_ref[0, i, h] + k_col * vv[:, vc]
            s_out_ref[0, i, h] = s_new
            o = jnp.sum(q_col * s_new, axis=0, keepdims=True)
            br_ref[0, r, vc] = _rms(o, gg_ref[:, vc]) * _silu(gla_g[:, vc])

        u = p_ref[SLAB_U, r, :]
        for g, w in enumerate(POOL_WINDOWS):
            cs = slice(g * POOL_GROUP_DIM, (g + 1) * POOL_GROUP_DIM)
            past = jnp.sum(hist_ref[0, POOL_HIST - (w - 1):POOL_HIST, r, cs], axis=0)
            diff_ref[r, cs] = (u[:, cs] + past) / float(w) - u[:, cs]
        hist_out_ref[0, 0:POOL_HIST - 1, r, :] = hist_ref[0, 1:POOL_HIST, r, :]
        hist_out_ref[0, POOL_HIST - 1, r, :] = u

        xq = p_ref[SLAB_XQ, r, :]
        xg = p_ref[SLAB_XG, r, :]
        half_cols = [slice(h * XA_HEAD_DIM + j * LANES, h * XA_HEAD_DIM + (j + 1) * LANES)
                     for j in range(XA_HEAD_DIM // LANES) for h in range(XA_HEADS)]
        xq_rows = jnp.concatenate([xq[:, cs] for cs in half_cols], axis=0)
        n_mem = ck_ref.shape[2]
        prod = (ck_ref[0, i] * xq_rows[None]).reshape(n_mem * SUBLANES, LANES).astype(BF)
        part = _dot(prod, jnp.ones((LANES, LANES), BF)).reshape(n_mem, SUBLANES, LANES)
        s = (part + pltpu.roll(part, XA_HEADS, axis=1)) * (XA_HEAD_DIM ** -0.5)
        p = jnp.exp(s - jnp.max(s, axis=0, keepdims=True))
        o = jnp.sum(p * cv_ref[0, i], axis=0) / jnp.sum(p, axis=0)
        halves = XA_HEAD_DIM // LANES
        o_row = jnp.concatenate([o[j * XA_HEADS + h:j * XA_HEADS + h + 1, :]
                                 for h in range(XA_HEADS) for j in range(halves)], axis=1)
        br_ref[2, r, :] = o_row * _silu(xg)

    @pl.when(pl.program_id(1) == pl.num_programs(1) - 1)
    def _():
        for g in range(len(POOL_WINDOWS)):
            cs = slice(g * POOL_GROUP_DIM, (g + 1) * POOL_GROUP_DIM)
            mixed = _dot(diff_ref[:, cs].astype(BF), pw_ref[g]) * ps_ref[:, cs]
            br_ref[1, :, cs] = mixed * _silu(p_ref[SLAB_POOL_G, :, cs])


def _cache_rows_view(c):
    depth, nb, n_mem = c.shape[:3]
    halves = XA_HEAD_DIM // LANES
    c = c.reshape(depth, nb, n_mem, XA_HEADS, halves, LANES)
    return c.transpose(0, 1, 2, 4, 3, 5).reshape(depth, nb, n_mem, halves * XA_HEADS, LANES)


def _sample_mix(p, la, s0, hist, ck, cv, pool_w, pool_scale, gla_gain, carry, layer, sb=4):
    nb = s0.shape[1]
    n_mem = ck.shape[2]
    ck, cv = _cache_rows_view(ck), _cache_rows_view(cv)
    rb = SUBLANES
    halves = rb // sb
    n_alias = 0 if carry is None else len(carry)
    n_in = 9
    kern = _drop_alias_refs(functools.partial(_sample_mix_kernel, sb=sb), n_in, n_alias)
    any_spec = pl.BlockSpec(memory_space=pl.ANY)
    return pl.pallas_call(
        kern,
        grid=(nb // rb, halves),
        in_specs=[
            pl.BlockSpec((N_MIX_SLABS, rb, D_MODEL), lambda i, j: (0, i, 0)),
            pl.BlockSpec((rb, GLA_KEY_WIDTH), lambda i, j: (i, 0)),
            pl.BlockSpec((1, sb, GLA_HEADS, GLA_DK, GLA_DV), lambda i, j: (layer, i * halves + j, 0, 0, 0)),
            pl.BlockSpec((1, POOL_HIST, rb, D_MODEL), lambda i, j: (layer, 0, i, 0)),
            pl.BlockSpec((1, sb, n_mem, SUBLANES, LANES), lambda i, j: (layer, i * halves + j, 0, 0, 0)),
            pl.BlockSpec((1, sb, n_mem, SUBLANES, LANES), lambda i, j: (layer, i * halves + j, 0, 0, 0)),
            _const_spec((len(POOL_WINDOWS), POOL_GROUP_DIM, POOL_GROUP_DIM)),
            _const_spec((1, D_MODEL)),
            _const_spec((1, D_MODEL)),
        ] + [any_spec] * n_alias,
        out_specs=[
            pl.BlockSpec((N_BRANCH, rb, D_MODEL), lambda i, j: (0, i, 0)),
            pl.BlockSpec((1, sb, GLA_HEADS, GLA_DK, GLA_DV), lambda i, j: (layer, i * halves + j, 0, 0, 0)),
            pl.BlockSpec((1, POOL_HIST, rb, D_MODEL), lambda i, j: (layer, 0, i, 0)),
        ],
        out_shape=[
            jax.ShapeDtypeStruct((N_BRANCH, nb, D_MODEL), F32),
            jax.ShapeDtypeStruct(s0.shape, F32),
            jax.ShapeDtypeStruct(hist.shape, F32),
        ],
        scratch_shapes=[pltpu.VMEM((rb, D_MODEL), F32)],
        input_output_aliases={n_in + a: 1 + a for a in range(n_alias)},
        compiler_params=_params(("parallel", "arbitrary"), 40),
        name="sample_mix",
    )(p, la, s0, hist, ck, cv, pool_w, pool_scale, gla_gain, *(carry or ()))


def _merge_kernel(br_ref, p_ref, x_ref, wbr_ref, wo_ref, fg_ref, out_ref, *, final):
    merged = _merge_term(br_ref, p_ref, wbr_ref, 0)
    for n in range(1, N_BRANCH):
        merged = merged + _merge_term(br_ref, p_ref, wbr_ref, n)
    out_ref[...] = _merge_finish(merged, x_ref[...], wo_ref, fg_ref, final)


def _merge_out(br, p, x, w_branch, w_out, final_gain, layer, final):
    m_rows = x.shape[0]
    whole = lambda shape: pl.BlockSpec(shape, lambda i: (0,) * len(shape))
    return pl.pallas_call(
        functools.partial(_merge_kernel, final=final),
        grid=(1,),
        in_specs=[
            whole((N_BRANCH, m_rows, D_MODEL)),
            whole((N_SLABS, m_rows, D_MODEL)),
            whole((m_rows, D_MODEL)),
            _layer_spec((N_BRANCH, D_MODEL // 2, D_MODEL), layer),
            _layer_spec((D_MODEL // 2, D_MODEL), layer),
            _const_spec((1, D_MODEL)),
        ],
        out_specs=whole((m_rows, D_MODEL)),
        out_shape=jax.ShapeDtypeStruct((m_rows, D_MODEL), F32),
        compiler_params=_params(("arbitrary",), 40),
        name="merge_out",
    )(br, p, x, w_branch, w_out, final_gain)


def kernel(x_prompt, x_sample, mem_prompt, cache_mem_k, cache_mem_v, state_gla, state_pool, w_in, w_a2, b_a, gla_gain, pool_w, pool_scale, w_mk, w_mv, w_branch, w_out, norm_gain, final_gain):
    batch, seq, _ = x_prompt.shape
    nb = x_sample.shape[0]
    n_mem = mem_prompt.shape[1]
    depth = w_in.shape[0]
    tm = min(256, seq)

    xp = x_prompt.reshape(batch * seq, D_MODEL)
    xs = x_sample.reshape(nb, D_MODEL)
    mem = mem_prompt.reshape(batch * n_mem, D_MODEL)
    fgain = final_gain.reshape(1, D_MODEL)

    mk, mv, mk_bf, mv_bf = _kvproj(mem, w_mk.astype(BF), w_mv.astype(BF), batch, n_mem)

    w_in_t = jnp.swapaxes(w_in, 1, 2)
    pool_rows = jnp.swapaxes(state_pool, 1, 2)
    w_proj = _pack_w_in(w_in_t)
    wb = _pack_matrices(w_branch.reshape(depth * N_BRANCH, D_MODEL, D_MODEL))
    wb = wb.reshape(depth, N_BRANCH, D_MODEL // 2, D_MODEL)
    wo = _pack_matrices(w_out)

    carry_p, carry_s = None, None
    for l in range(depth):
        final = l == depth - 1
        w_alr = w_in_t[l, ALR_START:ALR_START + GATE_RANK, :]
        wa2 = w_a2[l].astype(BF)
        ba = b_a[l].reshape(1, GLA_KEY_WIDTH)
        ngain = norm_gain[l].reshape(1, D_MODEL)
        ggain = gla_gain[l].reshape(1, D_MODEL)
        pscale = pool_scale[l].reshape(1, D_MODEL)
        pw = pool_w[l].astype(BF)

        xp, s_all, hist_all = _prompt_layer(xp, ngain, w_proj, w_alr, wa2, ba, mk_bf, mv_bf, pw, pscale,
                                            ggain, wb, wo, fgain, carry_p, l, depth, batch, seq, n_mem, tm, final)
        carry_p = (s_all, hist_all)

        ps, las = _inproj(xs, ngain, w_proj, w_alr, wa2, ba, l, nb, F32)
        brs, s_new, hist_new = _sample_mix(ps, las, state_gla, pool_rows, cache_mem_k, cache_mem_v,
                                           pw, pscale, ggain, carry_s, l)
        carry_s = (s_new, hist_new)
        xs = _merge_out(brs, ps, xs, wb, wo, fgain, l, final)

    return (xp.reshape(batch, seq, D_MODEL), xs.reshape(nb, 1, D_MODEL),
            mk, mv, carry_p[0], carry_p[1], carry_s[0], jnp.swapaxes(carry_s[1], 1, 2))
```

```python
import functools

import jax
import jax.numpy as jnp
from jax import lax
from jax.experimental import pallas as pl
from jax.experimental.pallas import tpu as pltpu

D_MODEL = 1024
GLA_HEADS = 4
GLA_DK = 128
GLA_DV = 256
GLA_KEY_WIDTH = GLA_HEADS * GLA_DK
GATE_RANK = 16
GATE_TAU = 16.0
CHUNK = 128
POOL_WINDOWS = (2, 4, 8, 16)
POOL_GROUP_DIM = 256
POOL_HIST = 15
HIST_PAD = 16
XA_HEADS = 4
XA_HEAD_DIM = 256
N_BRANCH = 3
EPS = 1e-6
SUBLANES = 8
LANES = 128

SLAB_QK, SLAB_V, SLAB_GLA_G, SLAB_U, SLAB_POOL_G, SLAB_XQ, SLAB_XG, SLAB_MERGE = 0, 1, 2, 3, 4, 5, 6, 7
N_SLABS = 10
N_MIX_SLABS = 7
N_HEAD_SLABS = 3
ALR_START = N_HEAD_SLABS * D_MODEL

BF = jnp.bfloat16
F32 = jnp.float32
MIB = 1 << 20


def _dot(a, b):
    return jnp.dot(a, b, preferred_element_type=F32)


def _dot_nt(a, b):
    return lax.dot_general(a, b, (((1,), (1,)), ((), ())), preferred_element_type=F32)


def _dot_tn(a, b):
    return lax.dot_general(a, b, (((0,), (0,)), ((), ())), preferred_element_type=F32)


def _pack_rows(w):
    return pltpu.bitcast(w.astype(BF), jnp.uint32)


def _unpack_rows(w_words):
    return pltpu.bitcast(w_words, BF)


def _params(sem, vmem_mib):
    return pltpu.CompilerParams(dimension_semantics=sem, vmem_limit_bytes=vmem_mib * MIB)


def _pack_w_in_kernel(a_ref, b_ref, o_ref):
    j = pl.program_id(1)

    @pl.when(j < N_HEAD_SLABS)
    def _():
        o_ref[0] = _pack_rows(a_ref[0].T)

    @pl.when(j >= N_HEAD_SLABS)
    def _():
        o_ref[0] = _pack_rows(jnp.concatenate([a_ref[0, GATE_RANK:, :], b_ref[0]], axis=0).T)


def _pack_w_in(w_in_t):
    depth = w_in_t.shape[0]
    return pl.pallas_call(
        _pack_w_in_kernel,
        grid=(depth, N_SLABS),
        in_specs=[
            pl.BlockSpec((1, D_MODEL, D_MODEL), lambda l, j: (l, j, 0)),
            pl.BlockSpec((1, GATE_RANK, D_MODEL), lambda l, j: (l, (j + 1) * (D_MODEL // GATE_RANK), 0)),
        ],
        out_specs=pl.BlockSpec((1, D_MODEL // 2, D_MODEL), lambda l, j: (l, 0, j)),
        out_shape=jax.ShapeDtypeStruct((depth, D_MODEL // 2, N_SLABS * D_MODEL), jnp.uint32),
        compiler_params=_params(("parallel", "parallel"), 32),
        name="pack_w_in",
    )(w_in_t, w_in_t)


def _pack_matrices_kernel(a_ref, o_ref):
    o_ref[0] = _pack_rows(a_ref[0])


def _pack_matrices(w):
    return pl.pallas_call(
        _pack_matrices_kernel,
        grid=(w.shape[0],),
        in_specs=[pl.BlockSpec((1, D_MODEL, D_MODEL), lambda r: (r, 0, 0))],
        out_specs=pl.BlockSpec((1, D_MODEL // 2, D_MODEL), lambda r: (r, 0, 0)),
        out_shape=jax.ShapeDtypeStruct((w.shape[0], D_MODEL // 2, D_MODEL), jnp.uint32),
        compiler_params=_params(("parallel",), 32),
        name="pack_matrices",
    )(w)


def _silu(x):
    return x * jax.nn.sigmoid(x)


def _rms(x, gain):
    ms = jnp.mean(x * x, axis=-1, keepdims=True)
    return x * lax.rsqrt(ms + EPS) * gain


def _const_spec(shape):
    zeros = (0,) * len(shape)
    return pl.BlockSpec(shape, lambda *_: zeros, pipeline_mode=pl.Buffered(1))


def _layer_spec(shape, layer):
    index = (layer,) + (0,) * len(shape)
    return pl.BlockSpec((1,) + tuple(shape), lambda *_: index, pipeline_mode=pl.Buffered(1))


def _drop_alias_refs(body, n_in, n_alias):
    def kern(*refs):
        return body(*refs[:n_in], *refs[n_in + n_alias:])
    return kern


def _kvproj_kernel(m_ref, wk_ref, wv_ref, k_ref, v_ref, kb_ref, vb_ref):
    m = m_ref[...].astype(BF)
    k = _dot(m, wk_ref[0])
    v = _dot(m, wv_ref[0])
    for h in range(XA_HEADS):
        cs = slice(h * XA_HEAD_DIM, (h + 1) * XA_HEAD_DIM)
        k_ref[0, 0, :, h, :] = k[:, cs]
        v_ref[0, 0, :, h, :] = v[:, cs]
    kb_ref[0] = k.astype(BF)
    vb_ref[0] = v.astype(BF)


def _kvproj(mem, wk, wv, batch, n_mem):
    depth = wk.shape[0]
    w_spec = pl.BlockSpec((1, D_MODEL, D_MODEL), lambda l, b: (l, 0, 0))
    out5 = pl.BlockSpec((1, 1, n_mem, XA_HEADS, XA_HEAD_DIM), lambda l, b: (l, b, 0, 0, 0))
    out_bf = pl.BlockSpec((1, n_mem, D_MODEL), lambda l, b: (l, b, 0))
    return pl.pallas_call(
        _kvproj_kernel,
        grid=(depth, batch),
        in_specs=[pl.BlockSpec((n_mem, D_MODEL), lambda l, b: (b, 0)), w_spec, w_spec],
        out_specs=[out5, out5, out_bf, out_bf],
        out_shape=[jax.ShapeDtypeStruct((depth, batch, n_mem, XA_HEADS, XA_HEAD_DIM), F32)] * 2
        + [jax.ShapeDtypeStruct((depth, batch * n_mem, D_MODEL), BF)] * 2,
        compiler_params=_params(("parallel", "parallel"), 32),
        name="kvproj",
    )(mem, wk, wv)


def _inproj_slab(h, w_ref, p_out, j):
    w = w_ref[0, :, j * D_MODEL:(j + 1) * D_MODEL]
    p_out[j] = _dot(h, _unpack_rows(w)).astype(p_out.dtype)


def _inproj_gate_lowrank(h, walr_ref):
    return _dot_nt(h, walr_ref[...].astype(BF)).astype(BF)


def _inproj_gate(alr, wa2_ref, ba_ref, la_out):
    z = _dot(alr, wa2_ref[...]) + ba_ref[...]
    la_out[...] = (jnp.minimum(z, 0.0) - jnp.log(1.0 + jnp.exp(-jnp.abs(z)))) * (1.0 / GATE_TAU)


def _merge_term(br_ref, p, wbr_ref, n):
    return jax.nn.sigmoid(p[SLAB_MERGE + n].astype(F32)) * _dot(br_ref[n].astype(BF), _unpack_rows(wbr_ref[0, n]))


def _merge_finish(merged, x, wo_ref, fg_ref, final):
    x_new = x + _dot(merged.astype(BF), _unpack_rows(wo_ref[0]))
    return _rms(x_new, fg_ref[...]) if final else x_new


def _chunk_cumsum_matrix(tm):
    row = lax.broadcasted_iota(jnp.int32, (tm, tm), 0)
    col = lax.broadcasted_iota(jnp.int32, (tm, tm), 1)
    return (((row // CHUNK) == (col // CHUNK)) & (row >= col)).astype(BF)


def _window_matrices(tm):
    row = lax.broadcasted_iota(jnp.int32, (tm, tm), 0)
    col = lax.broadcasted_iota(jnp.int32, (tm, tm), 1)
    return jnp.stack([((row >= col) & (row - col < w)).astype(BF) for w in POOL_WINDOWS])


def _gla_cumdecay(cum_ref, la_ref):
    la = la_ref[...]
    la_hi = la.astype(BF)
    la_lo = (la - la_hi.astype(F32)).astype(BF)
    return _dot(cum_ref[...], la_hi) + _dot(cum_ref[...], la_lo)


def _gla_chunk(p, bcum, gg_ref, br_ref, st_ref, c, filler):
    crow = lax.broadcasted_iota(jnp.int32, (CHUNK, CHUNK), 0)
    ccol = lax.broadcasted_iota(jnp.int32, (CHUNK, CHUNK), 1)
    causal = crow >= ccol
    rows = slice(c * CHUNK, (c + 1) * CHUNK)
    heads = range(GLA_HEADS)
    q_dec, k_end, decay, att = [], [], [], []
    for h in heads:
        kc = slice(h * GLA_DK, (h + 1) * GLA_DK)
        kc2 = slice(GLA_KEY_WIDTH + h * GLA_DK, GLA_KEY_WIDTH + (h + 1) * GLA_DK)
        b = bcum[rows, kc]
        b_mid = b[CHUNK // 2 - 1:CHUNK // 2, :]
        b_last = b[CHUNK - 1:CHUNK, :]
        q = p[SLAB_QK, rows, kc].astype(F32) * (GLA_DK ** -0.5)
        k = p[SLAB_QK, rows, kc2].astype(F32)
        q_dec.append((q * jnp.exp(b)).astype(BF))
        q_mid = (q * jnp.exp(b - b_mid)).astype(BF)
        k_mid = (k * jnp.exp(b_mid - b)).astype(BF)
        k_end.append((k * jnp.exp(b_last - b)).astype(BF))
        decay.append(jnp.exp(b_last))
        att.append(_dot_nt(q_mid, k_mid))
    filler()
    o = []
    for h in heads:
        vc = slice(h * GLA_DV, (h + 1) * GLA_DV)
        a = jnp.where(causal, att[h], 0.0).astype(BF)
        o.append(_dot(a, p[SLAB_V, rows, vc]) + _dot_nt(q_dec[h], st_ref[h].astype(BF)))
    for h in heads:
        vc = slice(h * GLA_DV, (h + 1) * GLA_DV)
        st_ref[h] = decay[h] * st_ref[h] + _dot_tn(p[SLAB_V, rows, vc], k_end[h])
    for h in heads:
        vc = slice(h * GLA_DV, (h + 1) * GLA_DV)
        g = p[SLAB_GLA_G, rows, vc].astype(F32)
        br_ref[0, rows, vc] = (_rms(o[h], gg_ref[:, vc]) * _silu(g)).astype(BF)


def _pool_window_sums(p, win_ref):
    sums = []
    for g in range(len(POOL_WINDOWS)):
        cs = slice(g * POOL_GROUP_DIM, (g + 1) * POOL_GROUP_DIM)
        sums.append(_dot(win_ref[g], p[SLAB_U, :, cs]))
    return sums


def _pool_branch(p, sums, pw_ref, ps_ref, br_ref, ubuf_ref, t, tm):
    u = p[SLAB_U].astype(F32)
    ubuf_ref[HIST_PAD:2 * HIST_PAD, :] = u[0:HIST_PAD]
    pos = t * tm + lax.broadcasted_iota(jnp.int32, (tm, 1), 0)
    for g, w in enumerate(POOL_WINDOWS):
        cs = slice(g * POOL_GROUP_DIM, (g + 1) * POOL_GROUP_DIM)
        ug = u[:, cs]
        head = ug[0:HIST_PAD]
        for j in range(1, w):
            head = head + ubuf_ref[HIST_PAD - j:2 * HIST_PAD - j, cs]
        s = jnp.concatenate([head, sums[g][HIST_PAD:]], axis=0)
        cnt = jnp.minimum(w, pos + 1).astype(F32)
        diff = s / cnt - ug
        mixed = _dot(diff.astype(BF), pw_ref[g]) * ps_ref[:, cs]
        pg = p[SLAB_POOL_G, :, cs].astype(F32)
        br_ref[1, :, cs] = (mixed * _silu(pg)).astype(BF)
    ubuf_ref[0:HIST_PAD, :] = u[tm - HIST_PAD:tm]


def _xattn_probs(p, mk_ref):
    out = []
    for h in range(XA_HEADS):
        cs = slice(h * XA_HEAD_DIM, (h + 1) * XA_HEAD_DIM)
        s = _dot_nt(p[SLAB_XQ, :, cs], mk_ref[0, :, cs]) * (XA_HEAD_DIM ** -0.5)
        pr = jnp.exp(s - jnp.max(s, axis=-1, keepdims=True))
        out.append((pr.astype(BF), jnp.sum(pr, axis=-1, keepdims=True)))
    return out


def _xattn_branch(p, probs, mv_ref, br_ref):
    for h in range(XA_HEADS):
        cs = slice(h * XA_HEAD_DIM, (h + 1) * XA_HEAD_DIM)
        pr, denom = probs[h]
        o = _dot(pr, mv_ref[0, :, cs]) / denom
        xg = p[SLAB_XG, :, cs].astype(F32)
        br_ref[2, :, cs] = (o * _silu(xg)).astype(BF)


def _prompt_layer_kernel(xn_ref, xc_ref, g_ref, w_ref, walr_ref, wa2_ref, ba_ref, cum_ref, win_ref,
                         mk_ref, mv_ref, pw_ref, ps_ref, gg_ref, wbr_ref, wo_ref, fg_ref,
                         out_ref, s_out_ref, hist_out_ref,
                         h_scr, p_scr, la_scr, br_scr, st_ref, ubuf_ref, *, tm, nt, final):
    s = pl.program_id(0)
    t = jnp.maximum(s - 1, 0) % nt
    slot_w = s % 2
    slot_r = 1 - slot_w

    @pl.when(s == 0)
    def _():
        p_scr[1] = jnp.zeros(p_scr.shape[1:], p_scr.dtype)
        la_scr[1] = jnp.zeros(la_scr.shape[1:], la_scr.dtype)

    @pl.when(t == 0)
    def _():
        st_ref[...] = jnp.zeros_like(st_ref)
        ubuf_ref[0:HIST_PAD, :] = jnp.zeros((HIST_PAD, D_MODEL), F32)

    @pl.when(s == 0)
    def _():
        h_scr[0] = _rms(xc_ref[...], g_ref[...]).astype(BF)

    p_next = p_scr.at[slot_w]
    p = p_scr.at[slot_r]

    slabs = list(range(N_SLABS))

    def filler(n=1):
        for _ in range(n):
            _inproj_slab(h_scr[slot_w], w_ref, p_next, slabs.pop(0))

    alr = _inproj_gate_lowrank(h_scr[slot_w], walr_ref)
    bcum = _gla_cumdecay(cum_ref, la_scr.at[slot_r])
    filler()
    _inproj_gate(alr, wa2_ref, ba_ref, la_scr.at[slot_w])
    per_site = (N_SLABS - 2) // (2 * (tm // CHUNK))
    probs, sums = None, None
    for c in range(tm // CHUNK):
        _gla_chunk(p, bcum, gg_ref, br_scr, st_ref, c, functools.partial(filler, per_site))
        filler(per_site)
        if c == 0:
            probs = _xattn_probs(p, mk_ref)
            sums = _pool_window_sums(p, win_ref)
        elif c == 1:
            _xattn_branch(p, probs, mv_ref, br_scr)
            _pool_branch(p, sums, pw_ref, ps_ref, br_scr, ubuf_ref, t, tm)
    h_scr[slot_r] = _rms(xn_ref[...], g_ref[...]).astype(BF)
    merged = _merge_term(br_scr, p, wbr_ref, 2)
    merged = merged + _merge_term(br_scr, p, wbr_ref, 1)
    merged = merged + _merge_term(br_scr, p, wbr_ref, 0)
    filler()
    assert not slabs
    out_ref[...] = _merge_finish(merged, xc_ref[...], wo_ref, fg_ref, final)

    @pl.when((s > 0) & (t == nt - 1))
    def _():
        for h in range(GLA_HEADS):
            s_out_ref[0, 0, h] = st_ref[h].T
        hist_out_ref[0, 0] = ubuf_ref[1:HIST_PAD, :]


def _prompt_layer(x, ngain, w_proj, w_alr, wa2, ba, mk, mv, pool_w, pool_scale, gla_gain,
                  w_branch, w_out, fgain, carry, layer, depth, batch, seq, n_mem, tm, final):
    nt = seq // tm
    n_tiles = batch * nt
    assert tm // CHUNK == 2 and (N_SLABS - 2) % (2 * (tm // CHUNK)) == 0, "slab placement assumes two GLA chunks"
    n_alias = 0 if carry is None else len(carry)
    n_in = 17
    kern = _drop_alias_refs(functools.partial(_prompt_layer_kernel, tm=tm, nt=nt, final=final), n_in, n_alias)
    any_spec = pl.BlockSpec(memory_space=pl.ANY)

    def cur(s):
        return jnp.maximum(s - 1, 0)

    def seq_of(s):
        return cur(s) // nt

    return pl.pallas_call(
        kern,
        grid=(n_tiles + 1,),
        in_specs=[
            pl.BlockSpec((tm, D_MODEL), lambda s: (jnp.minimum(s + 1, n_tiles - 1), 0)),
            pl.BlockSpec((tm, D_MODEL), lambda s: (cur(s), 0)),
            _const_spec((1, D_MODEL)),
            _layer_spec((D_MODEL // 2, N_SLABS * D_MODEL), layer),
            _const_spec((GATE_RANK, D_MODEL)),
            _const_spec((GATE_RANK, GLA_KEY_WIDTH)),
            _const_spec((1, GLA_KEY_WIDTH)),
            _const_spec((tm, tm)),
            _const_spec((len(POOL_WINDOWS), tm, tm)),
            pl.BlockSpec((1, n_mem, D_MODEL), lambda s: (layer, seq_of(s), 0)),
            pl.BlockSpec((1, n_mem, D_MODEL), lambda s: (layer, seq_of(s), 0)),
            _const_spec((len(POOL_WINDOWS), POOL_GROUP_DIM, POOL_GROUP_DIM)),
            _const_spec((1, D_MODEL)),
            _const_spec((1, D_MODEL)),
            _layer_spec((N_BRANCH, D_MODEL // 2, D_MODEL), layer),
            _layer_spec((D_MODEL // 2, D_MODEL), layer),
            _const_spec((1, D_MODEL)),
        ] + [any_spec] * n_alias,
        out_specs=[
            pl.BlockSpec((tm, D_MODEL), lambda s: (cur(s), 0)),
            pl.BlockSpec((1, 1, GLA_HEADS, GLA_DK, GLA_DV), lambda s: (layer, seq_of(s), 0, 0, 0)),
            pl.BlockSpec((1, 1, POOL_HIST, D_MODEL), lambda s: (layer, seq_of(s), 0, 0)),
        ],
        out_shape=[
            jax.ShapeDtypeStruct((n_tiles * tm, D_MODEL), F32),
            jax.ShapeDtypeStruct((depth, batch, GLA_HEADS, GLA_DK, GLA_DV), F32),
            jax.ShapeDtypeStruct((depth, batch, POOL_HIST, D_MODEL), F32),
        ],
        scratch_shapes=[
            pltpu.VMEM((2, tm, D_MODEL), BF),
            pltpu.VMEM((2, N_SLABS, tm, D_MODEL), BF),
            pltpu.VMEM((2, tm, GLA_KEY_WIDTH), F32),
            pltpu.VMEM((N_BRANCH, tm, D_MODEL), BF),
            pltpu.VMEM((GLA_HEADS, GLA_DV, GLA_DK), F32),
            pltpu.VMEM((2 * HIST_PAD, D_MODEL), F32),
        ],
        input_output_aliases={n_in + a: 1 + a for a in range(n_alias)},
        compiler_params=_params(("arbitrary",), 60),
        name="prompt_layer",
    )(x, x, ngain, w_proj, w_alr, wa2, ba, _chunk_cumsum_matrix(tm), _window_matrices(tm),
      mk, mv, pool_w, pool_scale, gla_gain,
      w_branch, w_out, fgain, *(carry or ()))


def _inproj_kernel(x_ref, g_ref, w_ref, walr_ref, wa2_ref, ba_ref, p_ref, la_ref):
    h = _rms(x_ref[...], g_ref[...]).astype(BF)
    alr = _inproj_gate_lowrank(h, walr_ref)
    for j in range(N_SLABS):
        _inproj_slab(h, w_ref, p_ref, j)
    _inproj_gate(alr, wa2_ref, ba_ref, la_ref)


def _inproj(x, gain, w_proj, w_alr, w_a2, b_a, layer, tm, out_dtype):
    m_rows = x.shape[0]
    return pl.pallas_call(
        _inproj_kernel,
        grid=(m_rows // tm,),
        in_specs=[
            pl.BlockSpec((tm, D_MODEL), lambda i: (i, 0)),
            _const_spec((1, D_MODEL)),
            _layer_spec((D_MODEL // 2, N_SLABS * D_MODEL), layer),
            _const_spec((GATE_RANK, D_MODEL)),
            _const_spec((GATE_RANK, GLA_KEY_WIDTH)),
            _const_spec((1, GLA_KEY_WIDTH)),
        ],
        out_specs=[
            pl.BlockSpec((N_SLABS, tm, D_MODEL), lambda i: (0, i, 0)),
            pl.BlockSpec((tm, GLA_KEY_WIDTH), lambda i: (i, 0)),
        ],
        out_shape=[
            jax.ShapeDtypeStruct((N_SLABS, m_rows, D_MODEL), out_dtype),
            jax.ShapeDtypeStruct((m_rows, GLA_KEY_WIDTH), F32),
        ],
        compiler_params=_params(("parallel",), 48),
        name="inproj",
    )(x, gain, w_proj, w_alr, w_a2, b_a)


def _sample_mix_kernel(p_ref, la_ref, s0_ref, hist_ref, ck_ref, cv_ref, pw_ref, ps_ref, gg_ref,
                       br_ref, s_out_ref, hist_out_ref, diff_ref, *, sb):
    r0 = pl.program_id(1) * sb
    erow = lax.broadcasted_iota(jnp.int32, (GLA_DK, GLA_DK), 0)
    ecol = lax.broadcasted_iota(jnp.int32, (GLA_DK, GLA_DK), 1)
    eye = erow == ecol

    def to_col(x):
        return jnp.sum(jnp.where(eye, jnp.broadcast_to(x, (GLA_DK, GLA_DK)), 0.0), axis=1, keepdims=True)

    for i in range(sb):
        r = pl.ds(r0 + i, 1)
        la = la_ref[r, :]
        qk = p_ref[SLAB_QK, r, :]
        vv = p_ref[SLAB_V, r, :]
        gla_g = p_ref[SLAB_GLA_G, r, :]
        for h in range(GLA_HEADS):
            kc = slice(h * GLA_DK, (h + 1) * GLA_DK)
            kc2 = slice(GLA_KEY_WIDTH + h * GLA_DK, GLA_KEY_WIDTH + (h + 1) * GLA_DK)
            vc = slice(h * GLA_DV, (h + 1) * GLA_DV)
            a_col = to_col(jnp.exp(la[:, kc]))
            q_col = to_col(qk[:, kc] * (GLA_DK ** -0.5))
            k_col = to_col(qk[:, kc2])
            s_new = a_col * s0_ref[0, i, h] + k_col * vv[:, vc]
            s_out_ref[0, i, h] = s_new
            o = jnp.sum(q_col * s_new, axis=0, keepdims=True)
            br_ref[0, r, vc] = _rms(o, gg_ref[:, vc]) * _silu(gla_g[:, vc])

        u = p_ref[SLAB_U, r, :]
        for g, w in enumerate(POOL_WINDOWS):
            cs = slice(g * POOL_GROUP_DIM, (g + 1) * POOL_GROUP_DIM)
            past = jnp.sum(hist_ref[0, POOL_HIST - (w - 1):POOL_HIST, r, cs], axis=0)
            diff_ref[r, cs] = (u[:, cs] + past) / float(w) - u[:, cs]
        hist_out_ref[0, 0:POOL_HIST - 1, r, :] = hist_ref[0, 1:POOL_HIST, r, :]
        hist_out_ref[0, POOL_HIST - 1, r, :] = u

        xq = p_ref[SLAB_XQ, r, :]
        xg = p_ref[SLAB_XG, r, :]
        half_cols = [slice(h * XA_HEAD_DIM + j * LANES, h * XA_HEAD_DIM + (j + 1) * LANES)
                     for j in range(XA_HEAD_DIM // LANES) for h in range(XA_HEADS)]
        xq_rows = jnp.concatenate([xq[:, cs] for cs in half_cols], axis=0)
        n_mem = ck_ref.shape[2]
        prod = (ck_ref[0, i] * xq_rows[None]).reshape(n_mem * SUBLANES, LANES).astype(BF)
        part = _dot(prod, jnp.ones((LANES, LANES), BF)).reshape(n_mem, SUBLANES, LANES)
        s = (part + pltpu.roll(part, XA_HEADS, axis=1)) * (XA_HEAD_DIM ** -0.5)
        p = jnp.exp(s - jnp.max(s, axis=0, keepdims=True))
        o = jnp.sum(p * cv_ref[0, i], axis=0) / jnp.sum(p, axis=0)
        halves = XA_HEAD_DIM // LANES
        o_row = jnp.concatenate([o[j * XA_HEADS + h:j * XA_HEADS + h + 1, :]
                                 for h in range(XA_HEADS) for j in range(halves)], axis=1)
        br_ref[2, r, :] = o_row * _silu(xg)

    @pl.when(pl.program_id(1) == pl.num_programs(1) - 1)
    def _():
        for g in range(len(POOL_WINDOWS)):
            cs = slice(g * POOL_GROUP_DIM, (g + 1) * POOL_GROUP_DIM)
            mixed = _dot(diff_ref[:, cs].astype(BF), pw_ref[g]) * ps_ref[:, cs]
            br_ref[1, :, cs] = mixed * _silu(p_ref[SLAB_POOL_G, :, cs])


def _cache_rows_view(c):
    depth, nb, n_mem = c.shape[:3]
    halves = XA_HEAD_DIM // LANES
    c = c.reshape(depth, nb, n_mem, XA_HEADS, halves, LANES)
    return c.transpose(0, 1, 2, 4, 3, 5).reshape(depth, nb, n_mem, halves * XA_HEADS, LANES)


def _sample_mix(p, la, s0, hist, ck, cv, pool_w, pool_scale, gla_gain, carry, layer, sb=4):
    nb = s0.shape[1]
    n_mem = ck.shape[2]
    ck, cv = _cache_rows_view(ck), _cache_rows_view(cv)
    rb = SUBLANES
    halves = rb // sb
    n_alias = 0 if carry is None else len(carry)
    n_in = 9
    kern = _drop_alias_refs(functools.partial(_sample_mix_kernel, sb=sb), n_in, n_alias)
    any_spec = pl.BlockSpec(memory_space=pl.ANY)
    return pl.pallas_call(
        kern,
        grid=(nb // rb, halves),
        in_specs=[
            pl.BlockSpec((N_MIX_SLABS, rb, D_MODEL), lambda i, j: (0, i, 0)),
            pl.BlockSpec((rb, GLA_KEY_WIDTH), lambda i, j: (i, 0)),
            pl.BlockSpec((1, sb, GLA_HEADS, GLA_DK, GLA_DV), lambda i, j: (layer, i * halves + j, 0, 0, 0)),
            pl.BlockSpec((1, POOL_HIST, rb, D_MODEL), lambda i, j: (layer, 0, i, 0)),
            pl.BlockSpec((1, sb, n_mem, SUBLANES, LANES), lambda i, j: (layer, i * halves + j, 0, 0, 0)),
            pl.BlockSpec((1, sb, n_mem, SUBLANES, LANES), lambda i, j: (layer, i * halves + j, 0, 0, 0)),
            _const_spec((len(POOL_WINDOWS), POOL_GROUP_DIM, POOL_GROUP_DIM)),
            _const_spec((1, D_MODEL)),
            _const_spec((1, D_MODEL)),
        ] + [any_spec] * n_alias,
        out_specs=[
            pl.BlockSpec((N_BRANCH, rb, D_MODEL), lambda i, j: (0, i, 0)),
            pl.BlockSpec((1, sb, GLA_HEADS, GLA_DK, GLA_DV), lambda i, j: (layer, i * halves + j, 0, 0, 0)),
            pl.BlockSpec((1, POOL_HIST, rb, D_MODEL), lambda i, j: (layer, 0, i, 0)),
        ],
        out_shape=[
            jax.ShapeDtypeStruct((N_BRANCH, nb, D_MODEL), F32),
            jax.ShapeDtypeStruct(s0.shape, F32),
            jax.ShapeDtypeStruct(hist.shape, F32),
        ],
        scratch_shapes=[pltpu.VMEM((rb, D_MODEL), F32)],
        input_output_aliases={n_in + a: 1 + a for a in range(n_alias)},
        compiler_params=_params(("parallel", "arbitrary"), 40),
        name="sample_mix",
    )(p, la, s0, hist, ck, cv, pool_w, pool_scale, gla_gain, *(carry or ()))


def _merge_kernel(br_ref, p_ref, x_ref, wbr_ref, wo_ref, fg_ref, out_ref, *, final):
    merged = _merge_term(br_ref, p_ref, wbr_ref, 0)
    for n in range(1, N_BRANCH):
        merged = merged + _merge_term(br_ref, p_ref, wbr_ref, n)
    out_ref[...] = _merge_finish(merged, x_ref[...], wo_ref, fg_ref, final)


def _merge_out(br, p, x, w_branch, w_out, final_gain, layer, final):
    m_rows = x.shape[0]
    whole = lambda shape: pl.BlockSpec(shape, lambda i: (0,) * len(shape))
    return pl.pallas_call(
        functools.partial(_merge_kernel, final=final),
        grid=(1,),
        in_specs=[
            whole((N_BRANCH, m_rows, D_MODEL)),
            whole((N_SLABS, m_rows, D_MODEL)),
            whole((m_rows, D_MODEL)),
            _layer_spec((N_BRANCH, D_MODEL // 2, D_MODEL), layer),
            _layer_spec((D_MODEL // 2, D_MODEL), layer),
            _const_spec((1, D_MODEL)),
        ],
        out_specs=whole((m_rows, D_MODEL)),
        out_shape=jax.ShapeDtypeStruct((m_rows, D_MODEL), F32),
        compiler_params=_params(("arbitrary",), 40),
        name="merge_out",
    )(br, p, x, w_branch, w_out, final_gain)


def kernel(x_prompt, x_sample, mem_prompt, cache_mem_k, cache_mem_v, state_gla, state_pool, w_in, w_a2, b_a, gla_gain, pool_w, pool_scale, w_mk, w_mv, w_branch, w_out, norm_gain, final_gain):
    batch, seq, _ = x_prompt.shape
    nb = x_sample.shape[0]
    n_mem = mem_prompt.shape[1]
    depth = w_in.shape[0]
    tm = min(256, seq)

    xp = x_prompt.reshape(batch * seq, D_MODEL)
    xs = x_sample.reshape(nb, D_MODEL)
    mem = mem_prompt.reshape(batch * n_mem, D_MODEL)
    fgain = final_gain.reshape(1, D_MODEL)

    mk, mv, mk_bf, mv_bf = _kvproj(mem, w_mk.astype(BF), w_mv.astype(BF), batch, n_mem)

    w_in_t = jnp.swapaxes(w_in, 1, 2)
    pool_rows = jnp.swapaxes(state_pool, 1, 2)
    w_proj = _pack_w_in(w_in_t)
    wb = _pack_matrices(w_branch.reshape(depth * N_BRANCH, D_MODEL, D_MODEL))
    wb = wb.reshape(depth, N_BRANCH, D_MODEL // 2, D_MODEL)
    wo = _pack_matrices(w_out)

    carry_p, carry_s = None, None
    for l in range(depth):
        final = l == depth - 1
        w_alr = w_in_t[l, ALR_START:ALR_START + GATE_RANK, :]
        wa2 = w_a2[l].astype(BF)
        ba = b_a[l].reshape(1, GLA_KEY_WIDTH)
        ngain = norm_gain[l].reshape(1, D_MODEL)
        ggain = gla_gain[l].reshape(1, D_MODEL)
        pscale = pool_scale[l].reshape(1, D_MODEL)
        pw = pool_w[l].astype(BF)

        xp, s_all, hist_all = _prompt_layer(xp, ngain, w_proj, w_alr, wa2, ba, mk_bf, mv_bf, pw, pscale,
                                            ggain, wb, wo, fgain, carry_p, l, depth, batch, seq, n_mem, tm, final)
        carry_p = (s_all, hist_all)

        ps, las = _inproj(xs, ngain, w_proj, w_alr, wa2, ba, l, nb, F32)
        brs, s_new, hist_new = _sample_mix(ps, las, state_gla, pool_rows, cache_mem_k, cache_mem_v,
                                           pw, pscale, ggain, carry_s, l)
        carry_s = (s_new, hist_new)
        xs = _merge_out(brs, ps, xs, wb, wo, fgain, l, final)

    return (xp.reshape(batch, seq, D_MODEL), xs.reshape(nb, 1, D_MODEL),
            mk, mv, carry_p[0], carry_p[1], carry_s[0], jnp.swapaxes(carry_s[1], 1, 2))
```

```python
import functools

import jax
import jax.numpy as jnp
from jax import lax
from jax.experimental import pallas as pl
from jax.experimental.pallas import tpu as pltpu

D_MODEL = 1024
GLA_HEADS = 4
GLA_DK = 128
GLA_DV = 256
GLA_KEY_WIDTH = GLA_HEADS * GLA_DK
GATE_RANK = 16
GATE_TAU = 16.0
CHUNK = 128
POOL_WINDOWS = (2, 4, 8, 16)
POOL_GROUP_DIM = 256
POOL_HIST = 15
HIST_PAD = 16
XA_HEADS = 4
XA_HEAD_DIM = 256
N_BRANCH = 3
EPS = 1e-6
SUBLANES = 8
LANES = 128

SLAB_QK, SLAB_V, SLAB_GLA_G, SLAB_U, SLAB_POOL_G, SLAB_XQ, SLAB_XG, SLAB_MERGE = 0, 1, 2, 3, 4, 5, 6, 7
N_SLABS = 10
N_MIX_SLABS = 7
N_HEAD_SLABS = 3
ALR_START = N_HEAD_SLABS * D_MODEL

BF = jnp.bfloat16
F32 = jnp.float32
MIB = 1 << 20


def _dot(a, b):
    return jnp.dot(a, b, preferred_element_type=F32)


def _dot_nt(a, b):
    return lax.dot_general(a, b, (((1,), (1,)), ((), ())), preferred_element_type=F32)


def _dot_tn(a, b):
    return lax.dot_general(a, b, (((0,), (0,)), ((), ())), preferred_element_type=F32)


def _pack_rows(w):
    return pltpu.bitcast(w.astype(BF), jnp.uint32)


def _unpack_rows(w_words):
    return pltpu.bitcast(w_words, BF)


def _params(sem, vmem_mib):
    return pltpu.CompilerParams(dimension_semantics=sem, vmem_limit_bytes=vmem_mib * MIB)


def _pack_w_in_kernel(a_ref, b_ref, o_ref):
    j = pl.program_id(1)

    @pl.when(j < N_HEAD_SLABS)
    def _():
        o_ref[0] = _pack_rows(a_ref[0].T)

    @pl.when(j >= N_HEAD_SLABS)
    def _():
        o_ref[0] = _pack_rows(jnp.concatenate([a_ref[0, GATE_RANK:, :], b_ref[0]], axis=0).T)


def _pack_w_in(w_in_t):
    depth = w_in_t.shape[0]
    return pl.pallas_call(
        _pack_w_in_kernel,
        grid=(depth, N_SLABS),
        in_specs=[
            pl.BlockSpec((1, D_MODEL, D_MODEL), lambda l, j: (l, j, 0)),
            pl.BlockSpec((1, GATE_RANK, D_MODEL), lambda l, j: (l, (j + 1) * (D_MODEL // GATE_RANK), 0)),
        ],
        out_specs=pl.BlockSpec((1, D_MODEL // 2, D_MODEL), lambda l, j: (l, 0, j)),
        out_shape=jax.ShapeDtypeStruct((depth, D_MODEL // 2, N_SLABS * D_MODEL), jnp.uint32),
        compiler_params=_params(("parallel", "parallel"), 32),
        name="pack_w_in",
    )(w_in_t, w_in_t)


def _pack_matrices_kernel(a_ref, o_ref):
    o_ref[0] = _pack_rows(a_ref[0])


def _pack_matrices(w):
    return pl.pallas_call(
        _pack_matrices_kernel,
        grid=(w.shape[0],),
        in_specs=[pl.BlockSpec((1, D_MODEL, D_MODEL), lambda r: (r, 0, 0))],
        out_specs=pl.BlockSpec((1, D_MODEL // 2, D_MODEL), lambda r: (r, 0, 0)),
        out_shape=jax.ShapeDtypeStruct((w.shape[0], D_MODEL // 2, D_MODEL), jnp.uint32),
        compiler_params=_params(("parallel",), 32),
        name="pack_matrices",
    )(w)


def _silu(x):
    return x * jax.nn.sigmoid(x)


def _rms(x, gain):
    ms = jnp.mean(x * x, axis=-1, keepdims=True)
    return x * lax.rsqrt(ms + EPS) * gain


def _const_spec(shape):
    zeros = (0,) * len(shape)
    return pl.BlockSpec(shape, lambda *_: zeros, pipeline_mode=pl.Buffered(1))


def _layer_spec(shape, layer):
    index = (layer,) + (0,) * len(shape)
    return pl.BlockSpec((1,) + tuple(shape), lambda *_: index, pipeline_mode=pl.Buffered(1))


def _drop_alias_refs(body, n_in, n_alias):
    def kern(*refs):
        return body(*refs[:n_in], *refs[n_in + n_alias:])
    return kern


def _kvproj_kernel(m_ref, wk_ref, wv_ref, k_ref, v_ref, kb_ref, vb_ref):
    m = m_ref[...].astype(BF)
    k = _dot(m, wk_ref[0])
    v = _dot(m, wv_ref[0])
    for h in range(XA_HEADS):
        cs = slice(h * XA_HEAD_DIM, (h + 1) * XA_HEAD_DIM)
        k_ref[0, 0, :, h, :] = k[:, cs]
        v_ref[0, 0, :, h, :] = v[:, cs]
    kb_ref[0] = k.astype(BF)
    vb_ref[0] = v.astype(BF)


def _kvproj(mem, wk, wv, batch, n_mem):
    depth = wk.shape[0]
    w_spec = pl.BlockSpec((1, D_MODEL, D_MODEL), lambda l, b: (l, 0, 0))
    out5 = pl.BlockSpec((1, 1, n_mem, XA_HEADS, XA_HEAD_DIM), lambda l, b: (l, b, 0, 0, 0))
    out_bf = pl.BlockSpec((1, n_mem, D_MODEL), lambda l, b: (l, b, 0))
    return pl.pallas_call(
        _kvproj_kernel,
        grid=(depth, batch),
        in_specs=[pl.BlockSpec((n_mem, D_MODEL), lambda l, b: (b, 0)), w_spec, w_spec],
        out_specs=[out5, out5, out_bf, out_bf],
        out_shape=[jax.ShapeDtypeStruct((depth, batch, n_mem, XA_HEADS, XA_HEAD_DIM), F32)] * 2
        + [jax.ShapeDtypeStruct((depth, batch * n_mem, D_MODEL), BF)] * 2,
        compiler_params=_params(("parallel", "parallel"), 32),
        name="kvproj",
    )(mem, wk, wv)


def _inproj_slab(h, w_ref, p_out, j):
    w = w_ref[0, :, j * D_MODEL:(j + 1) * D_MODEL]
    p_out[j] = _dot(h, _unpack_rows(w)).astype(p_out.dtype)


def _inproj_gate_lowrank(h, walr_ref):
    return _dot_nt(h, walr_ref[...].astype(BF)).astype(BF)


def _inproj_gate(alr, wa2_ref, ba_ref, la_out):
    z = _dot(alr, wa2_ref[...]) + ba_ref[...]
    la_out[...] = (jnp.minimum(z, 0.0) - jnp.log(1.0 + jnp.exp(-jnp.abs(z)))) * (1.0 / GATE_TAU)


def _merge_term(br_ref, p, wbr_ref, n):
    return jax.nn.sigmoid(p[SLAB_MERGE + n].astype(F32)) * _dot(br_ref[n].astype(BF), _unpack_rows(wbr_ref[0, n]))


def _merge_finish(merged, x, wo_ref, fg_ref, final):
    x_new = x + _dot(merged.astype(BF), _unpack_rows(wo_ref[0]))
    return _rms(x_new, fg_ref[...]) if final else x_new


def _chunk_cumsum_matrix(tm):
    row = lax.broadcasted_iota(jnp.int32, (tm, tm), 0)
    col = lax.broadcasted_iota(jnp.int32, (tm, tm), 1)
    return (((row // CHUNK) == (col // CHUNK)) & (row >= col)).astype(BF)


def _window_matrices(tm):
    row = lax.broadcasted_iota(jnp.int32, (tm, tm), 0)
    col = lax.broadcasted_iota(jnp.int32, (tm, tm), 1)
    return jnp.stack([((row >= col) & (row - col < w)).astype(BF) for w in POOL_WINDOWS])


def _gla_cumdecay(cum_ref, la_ref):
    la = la_ref[...]
    la_hi = la.astype(BF)
    la_lo = (la - la_hi.astype(F32)).astype(BF)
    return _dot(cum_ref[...], la_hi) + _dot(cum_ref[...], la_lo)


def _gla_chunk(p, bcum, gg_ref, br_ref, st_ref, c, filler):
    crow = lax.broadcasted_iota(jnp.int32, (CHUNK, CHUNK), 0)
    ccol = lax.broadcasted_iota(jnp.int32, (CHUNK, CHUNK), 1)
    causal = crow >= ccol
    rows = slice(c * CHUNK, (c + 1) * CHUNK)
    heads = range(GLA_HEADS)
    q_dec, k_end, decay, att = [], [], [], []
    for h in heads:
        kc = slice(h * GLA_DK, (h + 1) * GLA_DK)
        kc2 = slice(GLA_KEY_WIDTH + h * GLA_DK, GLA_KEY_WIDTH + (h + 1) * GLA_DK)
        b = bcum[rows, kc]
        b_mid = b[CHUNK // 2 - 1:CHUNK // 2, :]
        b_last = b[CHUNK - 1:CHUNK, :]
        q = p[SLAB_QK, rows, kc].astype(F32) * (GLA_DK ** -0.5)
        k = p[SLAB_QK, rows, kc2].astype(F32)
        q_dec.append((q * jnp.exp(b)).astype(BF))
        q_mid = (q * jnp.exp(b - b_mid)).astype(BF)
        k_mid = (k * jnp.exp(b_mid - b)).astype(BF)
        k_end.append((k * jnp.exp(b_last - b)).astype(BF))
        decay.append(jnp.exp(b_last))
        att.append(_dot_nt(q_mid, k_mid))
    filler()
    o = []
    for h in heads:
        vc = slice(h * GLA_DV, (h + 1) * GLA_DV)
        a = jnp.where(causal, att[h], 0.0).astype(BF)
        o.append(_dot(a, p[SLAB_V, rows, vc]) + _dot_nt(q_dec[h], st_ref[h].astype(BF)))
    for h in heads:
        vc = slice(h * GLA_DV, (h + 1) * GLA_DV)
        st_ref[h] = decay[h] * st_ref[h] + _dot_tn(p[SLAB_V, rows, vc], k_end[h])
    for h in heads:
        vc = slice(h * GLA_DV, (h + 1) * GLA_DV)
        g = p[SLAB_GLA_G, rows, vc].astype(F32)
        br_ref[0, rows, vc] = (_rms(o[h], gg_ref[:, vc]) * _silu(g)).astype(BF)


def _pool_window_sums(p, win_ref):
    sums = []
    for g in range(len(POOL_WINDOWS)):
        cs = slice(g * POOL_GROUP_DIM, (g + 1) * POOL_GROUP_DIM)
        sums.append(_dot(win_ref[g], p[SLAB_U, :, cs]))
    return sums


def _pool_branch(p, sums, pw_ref, ps_ref, br_ref, ubuf_ref, t, tm):
    u = p[SLAB_U].astype(F32)
    ubuf_ref[HIST_PAD:2 * HIST_PAD, :] = u[0:HIST_PAD]
    pos = t * tm + lax.broadcasted_iota(jnp.int32, (tm, 1), 0)
    for g, w in enumerate(POOL_WINDOWS):
        cs = slice(g * POOL_GROUP_DIM, (g + 1) * POOL_GROUP_DIM)
        ug = u[:, cs]
        head = ug[0:HIST_PAD]
        for j in range(1, w):
            head = head + ubuf_ref[HIST_PAD - j:2 * HIST_PAD - j, cs]
        s = jnp.concatenate([head, sums[g][HIST_PAD:]], axis=0)
        cnt = jnp.minimum(w, pos + 1).astype(F32)
        diff = s / cnt - ug
        mixed = _dot(diff.astype(BF), pw_ref[g]) * ps_ref[:, cs]
        pg = p[SLAB_POOL_G, :, cs].astype(F32)
        br_ref[1, :, cs] = (mixed * _silu(pg)).astype(BF)
    ubuf_ref[0:HIST_PAD, :] = u[tm - HIST_PAD:tm]


def _xattn_probs(p, mk_ref):
    out = []
    for h in range(XA_HEADS):
        cs = slice(h * XA_HEAD_DIM, (h + 1) * XA_HEAD_DIM)
        s = _dot_nt(p[SLAB_XQ, :, cs], mk_ref[0, :, cs]) * (XA_HEAD_DIM ** -0.5)
        pr = jnp.exp(s - jnp.max(s, axis=-1, keepdims=True))
        out.append((pr.astype(BF), jnp.sum(pr, axis=-1, keepdims=True)))
    return out


def _xattn_branch(p, probs, mv_ref, br_ref):
    for h in range(XA_HEADS):
        cs = slice(h * XA_HEAD_DIM, (h + 1) * XA_HEAD_DIM)
        pr, denom = probs[h]
        o = _dot(pr, mv_ref[0, :, cs]) / denom
        xg = p[SLAB_XG, :, cs].astype(F32)
        br_ref[2, :, cs] = (o * _silu(xg)).astype(BF)


def _prompt_layer_kernel(xn_ref, xc_ref, g_ref, w_ref, walr_ref, wa2_ref, ba_ref, cum_ref, win_ref,
                         mk_ref, mv_ref, pw_ref, ps_ref, gg_ref, wbr_ref, wo_ref, fg_ref,
                         out_ref, s_out_ref, hist_out_ref,
                         h_scr, p_scr, la_scr, br_scr, st_ref, ubuf_ref, *, tm, nt, final):
    s = pl.program_id(0)
    t = jnp.maximum(s - 1, 0) % nt
    slot_w = s % 2
    slot_r = 1 - slot_w

    @pl.when(s == 0)
    def _():
        p_scr[1] = jnp.zeros(p_scr.shape[1:], p_scr.dtype)
        la_scr[1] = jnp.zeros(la_scr.shape[1:], la_scr.dtype)

    @pl.when(t == 0)
    def _():
        st_ref[...] = jnp.zeros_like(st_ref)
        ubuf_ref[0:HIST_PAD, :] = jnp.zeros((HIST_PAD, D_MODEL), F32)

    @pl.when(s == 0)
    def _():
        h_scr[0] = _rms(xc_ref[...], g_ref[...]).astype(BF)

    p_next = p_scr.at[slot_w]
    p = p_scr.at[slot_r]

    slabs = list(range(N_SLABS))

    def filler(n=1):
        for _ in range(n):
            _inproj_slab(h_scr[slot_w], w_ref, p_next, slabs.pop(0))

    alr = _inproj_gate_lowrank(h_scr[slot_w], walr_ref)
    bcum = _gla_cumdecay(cum_ref, la_scr.at[slot_r])
    filler()
    per_site = (N_SLABS - 2) // (2 * (tm // CHUNK))

    def mid_filler(c):
        filler(per_site)
        if c == 0:
            _inproj_gate(alr, wa2_ref, ba_ref, la_scr.at[slot_w])

    probs, sums = None, None
    for c in range(tm // CHUNK):
        _gla_chunk(p, bcum, gg_ref, br_scr, st_ref, c, functools.partial(mid_filler, c))
        filler(per_site)
        if c == 0:
            probs = _xattn_probs(p, mk_ref)
            sums = _pool_window_sums(p, win_ref)
        elif c == 1:
            _xattn_branch(p, probs, mv_ref, br_scr)
            _pool_branch(p, sums, pw_ref, ps_ref, br_scr, ubuf_ref, t, tm)
    h_scr[slot_r] = _rms(xn_ref[...], g_ref[...]).astype(BF)
    merged = _merge_term(br_scr, p, wbr_ref, 0)
    merged = merged + _merge_term(br_scr, p, wbr_ref, 2)
    merged = merged + _merge_term(br_scr, p, wbr_ref, 1)
    filler()
    assert not slabs
    out_ref[...] = _merge_finish(merged, xc_ref[...], wo_ref, fg_ref, final)

    @pl.when((s > 0) & (t == nt - 1))
    def _():
        for h in range(GLA_HEADS):
            s_out_ref[0, 0, h] = st_ref[h].T
        hist_out_ref[0, 0] = ubuf_ref[1:HIST_PAD, :]


def _prompt_layer(x, ngain, w_proj, w_alr, wa2, ba, mk, mv, pool_w, pool_scale, gla_gain,
                  w_branch, w_out, fgain, carry, layer, depth, batch, seq, n_mem, tm, final):
    nt = seq // tm
    n_tiles = batch * nt
    assert tm // CHUNK == 2 and (N_SLABS - 2) % (2 * (tm // CHUNK)) == 0, "slab placement assumes two GLA chunks"
    n_alias = 0 if carry is None else len(carry)
    n_in = 17
    kern = _drop_alias_refs(functools.partial(_prompt_layer_kernel, tm=tm, nt=nt, final=final), n_in, n_alias)
    any_spec = pl.BlockSpec(memory_space=pl.ANY)

    def cur(s):
        return jnp.maximum(s - 1, 0)

    def seq_of(s):
        return cur(s) // nt

    return pl.pallas_call(
        kern,
        grid=(n_tiles + 1,),
        in_specs=[
            pl.BlockSpec((tm, D_MODEL), lambda s: (jnp.minimum(s + 1, n_tiles - 1), 0)),
            pl.BlockSpec((tm, D_MODEL), lambda s: (cur(s), 0)),
            _const_spec((1, D_MODEL)),
            _layer_spec((D_MODEL // 2, N_SLABS * D_MODEL), layer),
            _const_spec((GATE_RANK, D_MODEL)),
            _const_spec((GATE_RANK, GLA_KEY_WIDTH)),
            _const_spec((1, GLA_KEY_WIDTH)),
            _const_spec((tm, tm)),
            _const_spec((len(POOL_WINDOWS), tm, tm)),
            pl.BlockSpec((1, n_mem, D_MODEL), lambda s: (layer, seq_of(s), 0)),
            pl.BlockSpec((1, n_mem, D_MODEL), lambda s: (layer, seq_of(s), 0)),
            _const_spec((len(POOL_WINDOWS), POOL_GROUP_DIM, POOL_GROUP_DIM)),
            _const_spec((1, D_MODEL)),
            _const_spec((1, D_MODEL)),
            _layer_spec((N_BRANCH, D_MODEL // 2, D_MODEL), layer),
            _layer_spec((D_MODEL // 2, D_MODEL), layer),
            _const_spec((1, D_MODEL)),
        ] + [any_spec] * n_alias,
        out_specs=[
            pl.BlockSpec((tm, D_MODEL), lambda s: (cur(s), 0)),
            pl.BlockSpec((1, 1, GLA_HEADS, GLA_DK, GLA_DV), lambda s: (layer, seq_of(s), 0, 0, 0)),
            pl.BlockSpec((1, 1, POOL_HIST, D_MODEL), lambda s: (layer, seq_of(s), 0, 0)),
        ],
        out_shape=[
            jax.ShapeDtypeStruct((n_tiles * tm, D_MODEL), F32),
            jax.ShapeDtypeStruct((depth, batch, GLA_HEADS, GLA_DK, GLA_DV), F32),
            jax.ShapeDtypeStruct((depth, batch, POOL_HIST, D_MODEL), F32),
        ],
        scratch_shapes=[
            pltpu.VMEM((2, tm, D_MODEL), BF),
            pltpu.VMEM((2, N_SLABS, tm, D_MODEL), BF),
            pltpu.VMEM((2, tm, GLA_KEY_WIDTH), F32),
            pltpu.VMEM((N_BRANCH, tm, D_MODEL), BF),
            pltpu.VMEM((GLA_HEADS, GLA_DV, GLA_DK), F32),
            pltpu.VMEM((2 * HIST_PAD, D_MODEL), F32),
        ],
        input_output_aliases={n_in + a: 1 + a for a in range(n_alias)},
        compiler_params=_params(("arbitrary",), 60),
        name="prompt_layer",
    )(x, x, ngain, w_proj, w_alr, wa2, ba, _chunk_cumsum_matrix(tm), _window_matrices(tm),
      mk, mv, pool_w, pool_scale, gla_gain,
      w_branch, w_out, fgain, *(carry or ()))


def _inproj_kernel(x_ref, g_ref, w_ref, walr_ref, wa2_ref, ba_ref, p_ref, la_ref):
    h = _rms(x_ref[...], g_ref[...]).astype(BF)
    alr = _inproj_gate_lowrank(h, walr_ref)
    for j in range(N_SLABS):
        _inproj_slab(h, w_ref, p_ref, j)
    _inproj_gate(alr, wa2_ref, ba_ref, la_ref)


def _inproj(x, gain, w_proj, w_alr, w_a2, b_a, layer, tm, out_dtype):
    m_rows = x.shape[0]
    return pl.pallas_call(
        _inproj_kernel,
        grid=(m_rows // tm,),
        in_specs=[
            pl.BlockSpec((tm, D_MODEL), lambda i: (i, 0)),
            _const_spec((1, D_MODEL)),
            _layer_spec((D_MODEL // 2, N_SLABS * D_MODEL), layer),
            _const_spec((GATE_RANK, D_MODEL)),
            _const_spec((GATE_RANK, GLA_KEY_WIDTH)),
            _const_spec((1, GLA_KEY_WIDTH)),
        ],
        out_specs=[
            pl.BlockSpec((N_SLABS, tm, D_MODEL), lambda i: (0, i, 0)),
            pl.BlockSpec((tm, GLA_KEY_WIDTH), lambda i: (i, 0)),
        ],
        out_shape=[
            jax.ShapeDtypeStruct((N_SLABS, m_rows, D_MODEL), out_dtype),
            jax.ShapeDtypeStruct((m_rows, GLA_KEY_WIDTH), F32),
        ],
        compiler_params=_params(("parallel",), 48),
        name="inproj",
    )(x, gain, w_proj, w_alr, w_a2, b_a)


def _sample_mix_kernel(p_ref, la_ref, s0_ref, hist_ref, ck_ref, cv_ref, pw_ref, ps_ref, gg_ref,
                       br_ref, s_out_ref, hist_out_ref, diff_ref, *, sb):
    r0 = pl.program_id(1) * sb
    erow = lax.broadcasted_iota(jnp.int32, (GLA_DK, GLA_DK), 0)
    ecol = lax.broadcasted_iota(jnp.int32, (GLA_DK, GLA_DK), 1)
    eye = erow == ecol

    def to_col(x):
        return jnp.sum(jnp.where(eye, jnp.broadcast_to(x, (GLA_DK, GLA_DK)), 0.0), axis=1, keepdims=True)

    for i in range(sb):
        r = pl.ds(r0 + i, 1)
        la = la_ref[r, :]
        qk = p_ref[SLAB_QK, r, :]
        vv = p_ref[SLAB_V, r, :]
        gla_g = p_ref[SLAB_GLA_G, r, :]
        for h in range(GLA_HEADS):
            kc = slice(h * GLA_DK, (h + 1) * GLA_DK)
            kc2 = slice(GLA_KEY_WIDTH + h * GLA_DK, GLA_KEY_WIDTH + (h + 1) * GLA_DK)
            vc = slice(h * GLA_DV, (h + 1) * GLA_DV)
            a_col = to_col(jnp.exp(la[:, kc]))
            q_col = to_col(qk[:, kc] * (GLA_DK ** -0.5))
            k_col = to_col(qk[:, kc2])
            s_new = a_col * s0_ref[0, i, h] + k_col * vv[:, vc]
            s_out_ref[0, i, h] = s_new
            o = jnp.sum(q_col * s_new, axis=0, keepdims=True)
            br_ref[0, r, vc] = _rms(o, gg_ref[:, vc]) * _silu(gla_g[:, vc])

        u = p_ref[SLAB_U, r, :]
        for g, w in enumerate(POOL_WINDOWS):
            cs = slice(g * POOL_GROUP_DIM, (g + 1) * POOL_GROUP_DIM)
            past = jnp.sum(hist_ref[0, POOL_HIST - (w - 1):POOL_HIST, r, cs], axis=0)
            diff_ref[r, cs] = (u[:, cs] + past) / float(w) - u[:, cs]
        hist_out_ref[0, 0:POOL_HIST - 1, r, :] = hist_ref[0, 1:POOL_HIST, r, :]
        hist_out_ref[0, POOL_HIST - 1, r, :] = u

        xq = p_ref[SLAB_XQ, r, :]
        xg = p_ref[SLAB_XG, r, :]
        half_cols = [slice(h * XA_HEAD_DIM + j * LANES, h * XA_HEAD_DIM + (j + 1) * LANES)
                     for j in range(XA_HEAD_DIM // LANES) for h in range(XA_HEADS)]
        xq_rows = jnp.concatenate([xq[:, cs] for cs in half_cols], axis=0)
        n_mem = ck_ref.shape[2]
        prod = (ck_ref[0, i] * xq_rows[None]).reshape(n_mem * SUBLANES, LANES).astype(BF)
        part = _dot(prod, jnp.ones((LANES, LANES), BF)).reshape(n_mem, SUBLANES, LANES)
        s = (part + pltpu.roll(part, XA_HEADS, axis=1)) * (XA_HEAD_DIM ** -0.5)
        p = jnp.exp(s - jnp.max(s, axis=0, keepdims=True))
        o = jnp.sum(p * cv_ref[0, i], axis=0) / jnp.sum(p, axis=0)
        halves = XA_HEAD_DIM // LANES
        o_row = jnp.concatenate([o[j * XA_HEADS + h:j * XA_HEADS + h + 1, :]
                                 for h in range(XA_HEADS) for j in range(halves)], axis=1)
        br_ref[2, r, :] = o_row * _silu(xg)

    @pl.when(pl.program_id(1) == pl.num_programs(1) - 1)
    def _():
        for g in range(len(POOL_WINDOWS)):
            cs = slice(g * POOL_GROUP_DIM, (g + 1) * POOL_GROUP_DIM)
            mixed = _dot(diff_ref[:, cs].astype(BF), pw_ref[g]) * ps_ref[:, cs]
            br_ref[1, :, cs] = mixed * _silu(p_ref[SLAB_POOL_G, :, cs])


def _cache_rows_view(c):
    depth, nb, n_mem = c.shape[:3]
    halves = XA_HEAD_DIM // LANES
    c = c.reshape(depth, nb, n_mem, XA_HEADS, halves, LANES)
    return c.transpose(0, 1, 2, 4, 3, 5).reshape(depth, nb, n_mem, halves * XA_HEADS, LANES)


def _sample_mix(p, la, s0, hist, ck, cv, pool_w, pool_scale, gla_gain, carry, layer, sb=4):
    nb = s0.shape[1]
    n_mem = ck.shape[2]
    ck, cv = _cache_rows_view(ck), _cache_rows_view(cv)
    rb = SUBLANES
    halves = rb // sb
    n_alias = 0 if carry is None else len(carry)
    n_in = 9
    kern = _drop_alias_refs(functools.partial(_sample_mix_kernel, sb=sb), n_in, n_alias)
    any_spec = pl.BlockSpec(memory_space=pl.ANY)
    return pl.pallas_call(
        kern,
        grid=(nb // rb, halves),
        in_specs=[
            pl.BlockSpec((N_MIX_SLABS, rb, D_MODEL), lambda i, j: (0, i, 0)),
            pl.BlockSpec((rb, GLA_KEY_WIDTH), lambda i, j: (i, 0)),
            pl.BlockSpec((1, sb, GLA_HEADS, GLA_DK, GLA_DV), lambda i, j: (layer, i * halves + j, 0, 0, 0)),
            pl.BlockSpec((1, POOL_HIST, rb, D_MODEL), lambda i, j: (layer, 0, i, 0)),
            pl.BlockSpec((1, sb, n_mem, SUBLANES, LANES), lambda i, j: (layer, i * halves + j, 0, 0, 0)),
            pl.BlockSpec((1, sb, n_mem, SUBLANES, LANES), lambda i, j: (layer, i * halves + j, 0, 0, 0)),
            _const_spec((len(POOL_WINDOWS), POOL_GROUP_DIM, POOL_GROUP_DIM)),
            _const_spec((1, D_MODEL)),
            _const_spec((1, D_MODEL)),
        ] + [any_spec] * n_alias,
        out_specs=[
            pl.BlockSpec((N_BRANCH, rb, D_MODEL), lambda i, j: (0, i, 0)),
            pl.BlockSpec((1, sb, GLA_HEADS, GLA_DK, GLA_DV), lambda i, j: (layer, i * halves + j, 0, 0, 0)),
            pl.BlockSpec((1, POOL_HIST, rb, D_MODEL), lambda i, j: (layer, 0, i, 0)),
        ],
        out_shape=[
            jax.ShapeDtypeStruct((N_BRANCH, nb, D_MODEL), F32),
            jax.ShapeDtypeStruct(s0.shape, F32),
            jax.ShapeDtypeStruct(hist.shape, F32),
        ],
        scratch_shapes=[pltpu.VMEM((rb, D_MODEL), F32)],
        input_output_aliases={n_in + a: 1 + a for a in range(n_alias)},
        compiler_params=_params(("parallel", "arbitrary"), 40),
        name="sample_mix",
    )(p, la, s0, hist, ck, cv, pool_w, pool_scale, gla_gain, *(carry or ()))


def _merge_kernel(br_ref, p_ref, x_ref, wbr_ref, wo_ref, fg_ref, out_ref, *, final):
    merged = _merge_term(br_ref, p_ref, wbr_ref, 0)
    for n in range(1, N_BRANCH):
        merged = merged + _merge_term(br_ref, p_ref, wbr_ref, n)
    out_ref[...] = _merge_finish(merged, x_ref[...], wo_ref, fg_ref, final)


def _merge_out(br, p, x, w_branch, w_out, final_gain, layer, final):
    m_rows = x.shape[0]
    whole = lambda shape: pl.BlockSpec(shape, lambda i: (0,) * len(shape))
    return pl.pallas_call(
        functools.partial(_merge_kernel, final=final),
        grid=(1,),
        in_specs=[
            whole((N_BRANCH, m_rows, D_MODEL)),
            whole((N_SLABS, m_rows, D_MODEL)),
            whole((m_rows, D_MODEL)),
            _layer_spec((N_BRANCH, D_MODEL // 2, D_MODEL), layer),
            _layer_spec((D_MODEL // 2, D_MODEL), layer),
            _const_spec((1, D_MODEL)),
        ],
        out_specs=whole((m_rows, D_MODEL)),
        out_shape=jax.ShapeDtypeStruct((m_rows, D_MODEL), F32),
        compiler_params=_params(("arbitrary",), 40),
        name="merge_out",
    )(br, p, x, w_branch, w_out, final_gain)


def kernel(x_prompt, x_sample, mem_prompt, cache_mem_k, cache_mem_v, state_gla, state_pool, w_in, w_a2, b_a, gla_gain, pool_w, pool_scale, w_mk, w_mv, w_branch, w_out, norm_gain, final_gain):
    batch, seq, _ = x_prompt.shape
    nb = x_sample.shape[0]
    n_mem = mem_prompt.shape[1]
    depth = w_in.shape[0]
    tm = min(256, seq)

    xp = x_prompt.reshape(batch * seq, D_MODEL)
    xs = x_sample.reshape(nb, D_MODEL)
    mem = mem_prompt.reshape(batch * n_mem, D_MODEL)
    fgain = final_gain.reshape(1, D_MODEL)

    mk, mv, mk_bf, mv_bf = _kvproj(mem, w_mk.astype(BF), w_mv.astype(BF), batch, n_mem)

    w_in_t = jnp.swapaxes(w_in, 1, 2)
    pool_rows = jnp.swapaxes(state_pool, 1, 2)
    w_proj = _pack_w_in(w_in_t)
    wb = _pack_matrices(w_branch.reshape(depth * N_BRANCH, D_MODEL, D_MODEL))
    wb = wb.reshape(depth, N_BRANCH, D_MODEL // 2, D_MODEL)
    wo = _pack_matrices(w_out)

    carry_p, carry_s = None, None
    for l in range(depth):
        final = l == depth - 1
        w_alr = w_in_t[l, ALR_START:ALR_START + GATE_RANK, :]
        wa2 = w_a2[l].astype(BF)
        ba = b_a[l].reshape(1, GLA_KEY_WIDTH)
        ngain = norm_gain[l].reshape(1, D_MODEL)
        ggain = gla_gain[l].reshape(1, D_MODEL)
        pscale = pool_scale[l].reshape(1, D_MODEL)
        pw = pool_w[l].astype(BF)

        xp, s_all, hist_all = _prompt_layer(xp, ngain, w_proj, w_alr, wa2, ba, mk_bf, mv_bf, pw, pscale,
                                            ggain, wb, wo, fgain, carry_p, l, depth, batch, seq, n_mem, tm, final)
        carry_p = (s_all, hist_all)

        ps, las = _inproj(xs, ngain, w_proj, w_alr, wa2, ba, l, nb, F32)
        brs, s_new, hist_new = _sample_mix(ps, las, state_gla, pool_rows, cache_mem_k, cache_mem_v,
                                           pw, pscale, ggain, carry_s, l)
        carry_s = (s_new, hist_new)
        xs = _merge_out(brs, ps, xs, wb, wo, fgain, l, final)

    return (xp.reshape(batch, seq, D_MODEL), xs.reshape(nb, 1, D_MODEL),
            mk, mv, carry_p[0], carry_p[1], carry_s[0], jnp.swapaxes(carry_s[1], 1, 2))
```

```python
import functools
import math

import jax
import jax.numpy as jnp
from jax import lax
from jax.experimental import pallas as pl
from jax.experimental.pallas import tpu as pltpu

D_MODEL = 1024
GLA_HEADS = 4
GLA_DK = 128
GLA_DV = 256
GLA_KEY_WIDTH = GLA_HEADS * GLA_DK
GATE_RANK = 16
GATE_TAU = 16.0
CHUNK = 128
POOL_WINDOWS = (2, 4, 8, 16)
POOL_GROUP_DIM = 256
POOL_HIST = 15
HIST_PAD = 16
XA_HEADS = 4
XA_HEAD_DIM = 256
N_BRANCH = 3
EPS = 1e-6
SUBLANES = 8
LANES = 128

SLAB_QK, SLAB_V, SLAB_GLA_G, SLAB_U, SLAB_POOL_G, SLAB_XQ, SLAB_XG, SLAB_MERGE = 0, 1, 2, 3, 4, 5, 6, 7
N_SLABS = 10
N_MIX_SLABS = 7
N_HEAD_SLABS = 3
ALR_START = N_HEAD_SLABS * D_MODEL

BF = jnp.bfloat16
F32 = jnp.float32
MIB = 1 << 20

PROMPT_TILE = 256
SAMPLE_BLOCK = 4
VMEM_COMPILER_SCRATCH = 8 * MIB


def _dot(a, b):
    return jnp.dot(a, b, preferred_element_type=F32)


def _dot_nt(a, b):
    return lax.dot_general(a, b, (((1,), (1,)), ((), ())), preferred_element_type=F32)


def _dot_tn(a, b):
    return lax.dot_general(a, b, (((0,), (0,)), ((), ())), preferred_element_type=F32)


def _pack_rows(w):
    return pltpu.bitcast(w.astype(BF), jnp.uint32)


def _unpack_rows(w_words):
    return pltpu.bitcast(w_words, BF)


def _params(sem, *buffers):
    need = sum(math.prod(shape) * jnp.dtype(dtype).itemsize * copies for shape, dtype, copies in buffers)
    return pltpu.CompilerParams(dimension_semantics=sem, vmem_limit_bytes=need + VMEM_COMPILER_SCRATCH)


def _pack_w_in_kernel(a_ref, b_ref, o_ref):
    j = pl.program_id(1)

    @pl.when(j < N_HEAD_SLABS)
    def _():
        o_ref[0] = _pack_rows(a_ref[0].T)

    @pl.when(j >= N_HEAD_SLABS)
    def _():
        o_ref[0] = _pack_rows(jnp.concatenate([a_ref[0, GATE_RANK:, :], b_ref[0]], axis=0).T)


def _pack_w_in(w_in_t):
    depth = w_in_t.shape[0]
    return pl.pallas_call(
        _pack_w_in_kernel,
        grid=(depth, N_SLABS),
        in_specs=[
            pl.BlockSpec((1, D_MODEL, D_MODEL), lambda l, j: (l, j, 0)),
            pl.BlockSpec((1, GATE_RANK, D_MODEL), lambda l, j: (l, (j + 1) * (D_MODEL // GATE_RANK), 0)),
        ],
        out_specs=pl.BlockSpec((1, D_MODEL // 2, D_MODEL), lambda l, j: (l, 0, j)),
        out_shape=jax.ShapeDtypeStruct((depth, D_MODEL // 2, N_SLABS * D_MODEL), jnp.uint32),
        compiler_params=_params(("parallel", "parallel"),
                                ((D_MODEL, D_MODEL), F32, 2), ((GATE_RANK, D_MODEL), F32, 2),
                                ((D_MODEL // 2, D_MODEL), jnp.uint32, 2)),
        name="pack_w_in",
    )(w_in_t, w_in_t)


def _pack_matrices_kernel(a_ref, o_ref):
    o_ref[0] = _pack_rows(a_ref[0])


def _pack_matrices(w):
    return pl.pallas_call(
        _pack_matrices_kernel,
        grid=(w.shape[0],),
        in_specs=[pl.BlockSpec((1, D_MODEL, D_MODEL), lambda r: (r, 0, 0))],
        out_specs=pl.BlockSpec((1, D_MODEL // 2, D_MODEL), lambda r: (r, 0, 0)),
        out_shape=jax.ShapeDtypeStruct((w.shape[0], D_MODEL // 2, D_MODEL), jnp.uint32),
        compiler_params=_params(("parallel",), ((D_MODEL, D_MODEL), F32, 2), ((D_MODEL // 2, D_MODEL), jnp.uint32, 2)),
        name="pack_matrices",
    )(w)


def _silu(x):
    return x * jax.nn.sigmoid(x)


def _rms(x, gain):
    ms = jnp.mean(x * x, axis=-1, keepdims=True)
    return x * lax.rsqrt(ms + EPS) * gain


def _const_spec(shape):
    zeros = (0,) * len(shape)
    return pl.BlockSpec(shape, lambda *_: zeros, pipeline_mode=pl.Buffered(1))


def _layer_spec(shape, layer):
    index = (layer,) + (0,) * len(shape)
    return pl.BlockSpec((1,) + tuple(shape), lambda *_: index, pipeline_mode=pl.Buffered(1))


def _drop_alias_refs(body, n_in, n_alias):
    def kern(*refs):
        return body(*refs[:n_in], *refs[n_in + n_alias:])
    return kern


def _kvproj_kernel(m_ref, wk_ref, wv_ref, k_ref, v_ref, kb_ref, vb_ref):
    m = m_ref[...].astype(BF)
    k = _dot(m, wk_ref[0])
    v = _dot(m, wv_ref[0])
    for h in range(XA_HEADS):
        cs = slice(h * XA_HEAD_DIM, (h + 1) * XA_HEAD_DIM)
        k_ref[0, 0, :, h, :] = k[:, cs]
        v_ref[0, 0, :, h, :] = v[:, cs]
    kb_ref[0] = k.astype(BF)
    vb_ref[0] = v.astype(BF)


def _kvproj(mem, wk, wv, batch, n_mem):
    depth = wk.shape[0]
    w_spec = pl.BlockSpec((1, D_MODEL, D_MODEL), lambda l, b: (l, 0, 0))
    out5 = pl.BlockSpec((1, 1, n_mem, XA_HEADS, XA_HEAD_DIM), lambda l, b: (l, b, 0, 0, 0))
    out_bf = pl.BlockSpec((1, n_mem, D_MODEL), lambda l, b: (l, b, 0))
    return pl.pallas_call(
        _kvproj_kernel,
        grid=(depth, batch),
        in_specs=[pl.BlockSpec((n_mem, D_MODEL), lambda l, b: (b, 0)), w_spec, w_spec],
        out_specs=[out5, out5, out_bf, out_bf],
        out_shape=[jax.ShapeDtypeStruct((depth, batch, n_mem, XA_HEADS, XA_HEAD_DIM), F32)] * 2
        + [jax.ShapeDtypeStruct((depth, batch * n_mem, D_MODEL), BF)] * 2,
        compiler_params=_params(("parallel", "parallel"),
                                ((n_mem, D_MODEL), F32, 2), ((D_MODEL, D_MODEL), BF, 4),
                                ((n_mem, D_MODEL), F32, 4), ((n_mem, D_MODEL), BF, 4)),
        name="kvproj",
    )(mem, wk, wv)


def _inproj_slab(h, w_ref, p_out, j):
    w = w_ref[0, :, j * D_MODEL:(j + 1) * D_MODEL]
    p_out[j] = _dot(h, _unpack_rows(w)).astype(p_out.dtype)


def _inproj_gate_lowrank(h, walr_ref):
    return _dot_nt(h, walr_ref[...].astype(BF)).astype(BF)


def _inproj_gate(alr, wa2_ref, ba_ref, la_out):
    z = _dot(alr, wa2_ref[...]) + ba_ref[...]
    la_out[...] = (jnp.minimum(z, 0.0) - jnp.log(1.0 + jnp.exp(-jnp.abs(z)))) * (1.0 / GATE_TAU)


def _merge_term(br_ref, p, wbr_ref, n):
    return jax.nn.sigmoid(p[SLAB_MERGE + n].astype(F32)) * _dot(br_ref[n].astype(BF), _unpack_rows(wbr_ref[0, n]))


def _merge_finish(merged, x, wo_ref, fg_ref, final):
    x_new = x + _dot(merged.astype(BF), _unpack_rows(wo_ref[0]))
    return _rms(x_new, fg_ref[...]) if final else x_new


def _chunk_cumsum_matrix(tm):
    row = lax.broadcasted_iota(jnp.int32, (tm, tm), 0)
    col = lax.broadcasted_iota(jnp.int32, (tm, tm), 1)
    return (((row // CHUNK) == (col // CHUNK)) & (row >= col)).astype(BF)


def _window_matrices(tm):
    row = lax.broadcasted_iota(jnp.int32, (tm, tm), 0)
    col = lax.broadcasted_iota(jnp.int32, (tm, tm), 1)
    return jnp.stack([((row >= col) & (row - col < w)).astype(BF) for w in POOL_WINDOWS])


def _gla_cumdecay(cum_ref, la_ref):
    la = la_ref[...]
    la_hi = la.astype(BF)
    la_lo = (la - la_hi.astype(F32)).astype(BF)
    return _dot(cum_ref[...], la_hi) + _dot(cum_ref[...], la_lo)


def _gla_chunk(p, bcum, gg_ref, br_ref, st_ref, c, filler):
    crow = lax.broadcasted_iota(jnp.int32, (CHUNK, CHUNK), 0)
    ccol = lax.broadcasted_iota(jnp.int32, (CHUNK, CHUNK), 1)
    causal = crow >= ccol
    rows = slice(c * CHUNK, (c + 1) * CHUNK)
    heads = range(GLA_HEADS)
    q_dec, k_end, decay, att = [], [], [], []
    for h in heads:
        kc = slice(h * GLA_DK, (h + 1) * GLA_DK)
        kc2 = slice(GLA_KEY_WIDTH + h * GLA_DK, GLA_KEY_WIDTH + (h + 1) * GLA_DK)
        b = bcum[rows, kc]
        b_mid = b[CHUNK // 2 - 1:CHUNK // 2, :]
        b_last = b[CHUNK - 1:CHUNK, :]
        q = p[SLAB_QK, rows, kc].astype(F32) * (GLA_DK ** -0.5)
        k = p[SLAB_QK, rows, kc2].astype(F32)
        q_dec.append((q * jnp.exp(b)).astype(BF))
        q_mid = (q * jnp.exp(b - b_mid)).astype(BF)
        k_mid = (k * jnp.exp(b_mid - b)).astype(BF)
        k_end.append((k * jnp.exp(b_last - b)).astype(BF))
        decay.append(jnp.exp(b_last))
        att.append(_dot_nt(q_mid, k_mid))
    filler()
    o = []
    for h in heads:
        vc = slice(h * GLA_DV, (h + 1) * GLA_DV)
        a = jnp.where(causal, att[h], 0.0).astype(BF)
        o.append(_dot(a, p[SLAB_V, rows, vc]) + _dot_nt(q_dec[h], st_ref[h].astype(BF)))
    for h in heads:
        vc = slice(h * GLA_DV, (h + 1) * GLA_DV)
        st_ref[h] = decay[h] * st_ref[h] + _dot_tn(p[SLAB_V, rows, vc], k_end[h])
    for h in heads:
        vc = slice(h * GLA_DV, (h + 1) * GLA_DV)
        g = p[SLAB_GLA_G, rows, vc].astype(F32)
        br_ref[0, rows, vc] = (_rms(o[h], gg_ref[:, vc]) * _silu(g)).astype(BF)


def _pool_window_sums(p, win_ref):
    sums = []
    for g in range(len(POOL_WINDOWS)):
        cs = slice(g * POOL_GROUP_DIM, (g + 1) * POOL_GROUP_DIM)
        sums.append(_dot(win_ref[g], p[SLAB_U, :, cs]))
    return sums


def _pool_branch(p, sums, pw_ref, ps_ref, br_ref, ubuf_ref, t, tm):
    u = p[SLAB_U].astype(F32)
    ubuf_ref[HIST_PAD:2 * HIST_PAD, :] = u[0:HIST_PAD]
    pos = t * tm + lax.broadcasted_iota(jnp.int32, (tm, 1), 0)
    for g, w in enumerate(POOL_WINDOWS):
        cs = slice(g * POOL_GROUP_DIM, (g + 1) * POOL_GROUP_DIM)
        ug = u[:, cs]
        head = ug[0:HIST_PAD]
        for j in range(1, w):
            head = head + ubuf_ref[HIST_PAD - j:2 * HIST_PAD - j, cs]
        s = jnp.concatenate([head, sums[g][HIST_PAD:]], axis=0)
        cnt = jnp.minimum(w, pos + 1).astype(F32)
        diff = s / cnt - ug
        mixed = _dot(diff.astype(BF), pw_ref[g]) * ps_ref[:, cs]
        pg = p[SLAB_POOL_G, :, cs].astype(F32)
        br_ref[1, :, cs] = (mixed * _silu(pg)).astype(BF)
    ubuf_ref[0:HIST_PAD, :] = u[tm - HIST_PAD:tm]


def _xattn_probs(p, mk_ref):
    out = []
    for h in range(XA_HEADS):
        cs = slice(h * XA_HEAD_DIM, (h + 1) * XA_HEAD_DIM)
        s = _dot_nt(p[SLAB_XQ, :, cs], mk_ref[0, :, cs]) * (XA_HEAD_DIM ** -0.5)
        pr = jnp.exp(s - jnp.max(s, axis=-1, keepdims=True))
        out.append((pr.astype(BF), jnp.sum(pr, axis=-1, keepdims=True)))
    return out


def _xattn_branch(p, probs, mv_ref, br_ref):
    for h in range(XA_HEADS):
        cs = slice(h * XA_HEAD_DIM, (h + 1) * XA_HEAD_DIM)
        pr, denom = probs[h]
        o = _dot(pr, mv_ref[0, :, cs]) / denom
        xg = p[SLAB_XG, :, cs].astype(F32)
        br_ref[2, :, cs] = (o * _silu(xg)).astype(BF)


def _prompt_layer_kernel(xn_ref, xc_ref, g_ref, w_ref, walr_ref, wa2_ref, ba_ref, cum_ref, win_ref,
                         mk_ref, mv_ref, pw_ref, ps_ref, gg_ref, wbr_ref, wo_ref, fg_ref,
                         out_ref, s_out_ref, hist_out_ref,
                         h_scr, p_scr, la_scr, br_scr, st_ref, ubuf_ref, *, tm, nt, final):
    s = pl.program_id(0)
    t = jnp.maximum(s - 1, 0) % nt
    slot_w = s % 2
    slot_r = 1 - slot_w

    @pl.when(s == 0)
    def _():
        p_scr[1] = jnp.zeros(p_scr.shape[1:], p_scr.dtype)
        la_scr[1] = jnp.zeros(la_scr.shape[1:], la_scr.dtype)

    @pl.when(t == 0)
    def _():
        st_ref[...] = jnp.zeros_like(st_ref)
        ubuf_ref[0:HIST_PAD, :] = jnp.zeros((HIST_PAD, D_MODEL), F32)

    @pl.when(s == 0)
    def _():
        h_scr[0] = _rms(xc_ref[...], g_ref[...]).astype(BF)

    p_next = p_scr.at[slot_w]
    p = p_scr.at[slot_r]

    slabs = list(range(N_SLABS))

    def filler(n=1):
        for _ in range(n):
            _inproj_slab(h_scr[slot_w], w_ref, p_next, slabs.pop(0))

    alr = _inproj_gate_lowrank(h_scr[slot_w], walr_ref)
    bcum = _gla_cumdecay(cum_ref, la_scr.at[slot_r])
    filler()
    _inproj_gate(alr, wa2_ref, ba_ref, la_scr.at[slot_w])
    per_site = (N_SLABS - 2) // (2 * (tm // CHUNK))
    probs, sums = None, None
    for c in range(tm // CHUNK):
        _gla_chunk(p, bcum, gg_ref, br_scr, st_ref, c, functools.partial(filler, per_site))
        filler(per_site)
        if c == 0:
            probs = _xattn_probs(p, mk_ref)
            sums = _pool_window_sums(p, win_ref)
        elif c == 1:
            _xattn_branch(p, probs, mv_ref, br_scr)
            _pool_branch(p, sums, pw_ref, ps_ref, br_scr, ubuf_ref, t, tm)
    h_scr[slot_r] = _rms(xn_ref[...], g_ref[...]).astype(BF)
    merged = _merge_term(br_scr, p, wbr_ref, 2)
    merged = merged + _merge_term(br_scr, p, wbr_ref, 1)
    merged = merged + _merge_term(br_scr, p, wbr_ref, 0)
    filler()
    assert not slabs
    out_ref[...] = _merge_finish(merged, xc_ref[...], wo_ref, fg_ref, final)

    @pl.when((s > 0) & (t == nt - 1))
    def _():
        for h in range(GLA_HEADS):
            s_out_ref[0, 0, h] = st_ref[h].T
        hist_out_ref[0, 0] = ubuf_ref[1:HIST_PAD, :]


def _prompt_layer(x, ngain, w_proj, w_alr, wa2, ba, mk, mv, pool_w, pool_scale, gla_gain,
                  w_branch, w_out, fgain, carry, layer, depth, batch, seq, n_mem, tm, final):
    nt = seq // tm
    n_tiles = batch * nt
    assert tm // CHUNK == 2 and (N_SLABS - 2) % (2 * (tm // CHUNK)) == 0, "slab placement assumes two GLA chunks"
    n_alias = 0 if carry is None else len(carry)
    n_in = 17
    kern = _drop_alias_refs(functools.partial(_prompt_layer_kernel, tm=tm, nt=nt, final=final), n_in, n_alias)
    any_spec = pl.BlockSpec(memory_space=pl.ANY)
    scratch = [
        ((2, tm, D_MODEL), BF, 1),
        ((2, N_SLABS, tm, D_MODEL), BF, 1),
        ((2, tm, GLA_KEY_WIDTH), F32, 1),
        ((N_BRANCH, tm, D_MODEL), BF, 1),
        ((GLA_HEADS, GLA_DV, GLA_DK), F32, 1),
        ((2 * HIST_PAD, D_MODEL), F32, 1),
    ]

    def cur(s):
        return jnp.maximum(s - 1, 0)

    def seq_of(s):
        return cur(s) // nt

    return pl.pallas_call(
        kern,
        grid=(n_tiles + 1,),
        in_specs=[
            pl.BlockSpec((tm, D_MODEL), lambda s: (jnp.minimum(s + 1, n_tiles - 1), 0)),
            pl.BlockSpec((tm, D_MODEL), lambda s: (cur(s), 0)),
            _const_spec((1, D_MODEL)),
            _layer_spec((D_MODEL // 2, N_SLABS * D_MODEL), layer),
            _const_spec((GATE_RANK, D_MODEL)),
            _const_spec((GATE_RANK, GLA_KEY_WIDTH)),
            _const_spec((1, GLA_KEY_WIDTH)),
            _const_spec((tm, tm)),
            _const_spec((len(POOL_WINDOWS), tm, tm)),
            pl.BlockSpec((1, n_mem, D_MODEL), lambda s: (layer, seq_of(s), 0)),
            pl.BlockSpec((1, n_mem, D_MODEL), lambda s: (layer, seq_of(s), 0)),
            _const_spec((len(POOL_WINDOWS), POOL_GROUP_DIM, POOL_GROUP_DIM)),
            _const_spec((1, D_MODEL)),
            _const_spec((1, D_MODEL)),
            _layer_spec((N_BRANCH, D_MODEL // 2, D_MODEL), layer),
            _layer_spec((D_MODEL // 2, D_MODEL), layer),
            _const_spec((1, D_MODEL)),
        ] + [any_spec] * n_alias,
        out_specs=[
            pl.BlockSpec((tm, D_MODEL), lambda s: (cur(s), 0)),
            pl.BlockSpec((1, 1, GLA_HEADS, GLA_DK, GLA_DV), lambda s: (layer, seq_of(s), 0, 0, 0)),
            pl.BlockSpec((1, 1, POOL_HIST, D_MODEL), lambda s: (layer, seq_of(s), 0, 0)),
        ],
        out_shape=[
            jax.ShapeDtypeStruct((n_tiles * tm, D_MODEL), F32),
            jax.ShapeDtypeStruct((depth, batch, GLA_HEADS, GLA_DK, GLA_DV), F32),
            jax.ShapeDtypeStruct((depth, batch, POOL_HIST, D_MODEL), F32),
        ],
        scratch_shapes=[pltpu.VMEM(shape, dtype) for shape, dtype, _ in scratch],
        input_output_aliases={n_in + a: 1 + a for a in range(n_alias)},
        compiler_params=_params(
            ("arbitrary",), *scratch,
            ((tm, D_MODEL), F32, 6),
            ((D_MODEL // 2, (N_SLABS + N_BRANCH + 1) * D_MODEL), jnp.uint32, 1),
            ((n_mem, D_MODEL), BF, 4),
            ((tm, tm), BF, 1 + len(POOL_WINDOWS)),
            ((len(POOL_WINDOWS), POOL_GROUP_DIM, POOL_GROUP_DIM), BF, 1),
            ((GLA_HEADS, GLA_DK, GLA_DV), F32, 2)),
        name="prompt_layer",
    )(x, x, ngain, w_proj, w_alr, wa2, ba, _chunk_cumsum_matrix(tm), _window_matrices(tm),
      mk, mv, pool_w, pool_scale, gla_gain,
      w_branch, w_out, fgain, *(carry or ()))


def _inproj_kernel(x_ref, g_ref, w_ref, walr_ref, wa2_ref, ba_ref, p_ref, la_ref):
    h = _rms(x_ref[...], g_ref[...]).astype(BF)
    alr = _inproj_gate_lowrank(h, walr_ref)
    for j in range(N_SLABS):
        _inproj_slab(h, w_ref, p_ref, j)
    _inproj_gate(alr, wa2_ref, ba_ref, la_ref)


def _inproj(x, gain, w_proj, w_alr, w_a2, b_a, layer, tm, out_dtype):
    m_rows = x.shape[0]
    return pl.pallas_call(
        _inproj_kernel,
        grid=(m_rows // tm,),
        in_specs=[
            pl.BlockSpec((tm, D_MODEL), lambda i: (i, 0)),
            _const_spec((1, D_MODEL)),
            _layer_spec((D_MODEL // 2, N_SLABS * D_MODEL), layer),
            _const_spec((GATE_RANK, D_MODEL)),
            _const_spec((GATE_RANK, GLA_KEY_WIDTH)),
            _const_spec((1, GLA_KEY_WIDTH)),
        ],
        out_specs=[
            pl.BlockSpec((N_SLABS, tm, D_MODEL), lambda i: (0, i, 0)),
            pl.BlockSpec((tm, GLA_KEY_WIDTH), lambda i: (i, 0)),
        ],
        out_shape=[
            jax.ShapeDtypeStruct((N_SLABS, m_rows, D_MODEL), out_dtype),
            jax.ShapeDtypeStruct((m_rows, GLA_KEY_WIDTH), F32),
        ],
        compiler_params=_params(("parallel",),
                                ((tm, D_MODEL), F32, 2), ((D_MODEL // 2, N_SLABS * D_MODEL), jnp.uint32, 1),
                                ((N_SLABS, tm, D_MODEL), out_dtype, 2), ((tm, GLA_KEY_WIDTH), F32, 2)),
        name="inproj",
    )(x, gain, w_proj, w_alr, w_a2, b_a)


def _sample_mix_kernel(p_ref, la_ref, s0_ref, hist_ref, ck_ref, cv_ref, pw_ref, ps_ref, gg_ref,
                       br_ref, s_out_ref, hist_out_ref, diff_ref, *, sb):
    r0 = pl.program_id(1) * sb
    erow = lax.broadcasted_iota(jnp.int32, (GLA_DK, GLA_DK), 0)
    ecol = lax.broadcasted_iota(jnp.int32, (GLA_DK, GLA_DK), 1)
    eye = erow == ecol

    def to_col(x):
        return jnp.sum(jnp.where(eye, jnp.broadcast_to(x, (GLA_DK, GLA_DK)), 0.0), axis=1, keepdims=True)

    for i in range(sb):
        r = pl.ds(r0 + i, 1)
        la = la_ref[r, :]
        qk = p_ref[SLAB_QK, r, :]
        vv = p_ref[SLAB_V, r, :]
        gla_g = p_ref[SLAB_GLA_G, r, :]
        for h in range(GLA_HEADS):
            kc = slice(h * GLA_DK, (h + 1) * GLA_DK)
            kc2 = slice(GLA_KEY_WIDTH + h * GLA_DK, GLA_KEY_WIDTH + (h + 1) * GLA_DK)
            vc = slice(h * GLA_DV, (h + 1) * GLA_DV)
            a_col = to_col(jnp.exp(la[:, kc]))
            q_col = to_col(qk[:, kc] * (GLA_DK ** -0.5))
            k_col = to_col(qk[:, kc2])
            s_new = a_col * s0_ref[0, i, h] + k_col * vv[:, vc]
            s_out_ref[0, i, h] = s_new
            o = jnp.sum(q_col * s_new, axis=0, keepdims=True)
            br_ref[0, r, vc] = _rms(o, gg_ref[:, vc]) * _silu(gla_g[:, vc])

        u = p_ref[SLAB_U, r, :]
        for g, w in enumerate(POOL_WINDOWS):
            cs = slice(g * POOL_GROUP_DIM, (g + 1) * POOL_GROUP_DIM)
            past = jnp.sum(hist_ref[0, POOL_HIST - (w - 1):POOL_HIST, r, cs], axis=0)
            diff_ref[r, cs] = (u[:, cs] + past) / float(w) - u[:, cs]
        hist_out_ref[0, 0:POOL_HIST - 1, r, :] = hist_ref[0, 1:POOL_HIST, r, :]
        hist_out_ref[0, POOL_HIST - 1, r, :] = u

        xq = p_ref[SLAB_XQ, r, :]
        xg = p_ref[SLAB_XG, r, :]
        half_cols = [slice(h * XA_HEAD_DIM + j * LANES, h * XA_HEAD_DIM + (j + 1) * LANES)
                     for j in range(XA_HEAD_DIM // LANES) for h in range(XA_HEADS)]
        xq_rows = jnp.concatenate([xq[:, cs] for cs in half_cols], axis=0)
        n_mem = ck_ref.shape[2]
        prod = (ck_ref[0, i] * xq_rows[None]).reshape(n_mem * SUBLANES, LANES).astype(BF)
        part = _dot(prod, jnp.ones((LANES, LANES), BF)).reshape(n_mem, SUBLANES, LANES)
        s = (part + pltpu.roll(part, XA_HEADS, axis=1)) * (XA_HEAD_DIM ** -0.5)
        p = jnp.exp(s - jnp.max(s, axis=0, keepdims=True))
        o = jnp.sum(p * cv_ref[0, i], axis=0) / jnp.sum(p, axis=0)
        halves = XA_HEAD_DIM // LANES
        o_row = jnp.concatenate([o[j * XA_HEADS + h:j * XA_HEADS + h + 1, :]
                                 for h in range(XA_HEADS) for j in range(halves)], axis=1)
        br_ref[2, r, :] = o_row * _silu(xg)

    @pl.when(pl.program_id(1) == pl.num_programs(1) - 1)
    def _():
        for g in range(len(POOL_WINDOWS)):
            cs = slice(g * POOL_GROUP_DIM, (g + 1) * POOL_GROUP_DIM)
            mixed = _dot(diff_ref[:, cs].astype(BF), pw_ref[g]) * ps_ref[:, cs]
            br_ref[1, :, cs] = mixed * _silu(p_ref[SLAB_POOL_G, :, cs])


def _cache_rows_view(c):
    depth, nb, n_mem = c.shape[:3]
    halves = XA_HEAD_DIM // LANES
    c = c.reshape(depth, nb, n_mem, XA_HEADS, halves, LANES)
    return c.transpose(0, 1, 2, 4, 3, 5).reshape(depth, nb, n_mem, halves * XA_HEADS, LANES)


def _sample_mix(p, la, s0, hist, ck, cv, pool_w, pool_scale, gla_gain, carry, layer, sb=SAMPLE_BLOCK):
    nb = s0.shape[1]
    n_mem = ck.shape[2]
    ck, cv = _cache_rows_view(ck), _cache_rows_view(cv)
    rb = SUBLANES
    halves = rb // sb
    n_alias = 0 if carry is None else len(carry)
    n_in = 9
    kern = _drop_alias_refs(functools.partial(_sample_mix_kernel, sb=sb), n_in, n_alias)
    any_spec = pl.BlockSpec(memory_space=pl.ANY)
    return pl.pallas_call(
        kern,
        grid=(nb // rb, halves),
        in_specs=[
            pl.BlockSpec((N_MIX_SLABS, rb, D_MODEL), lambda i, j: (0, i, 0)),
            pl.BlockSpec((rb, GLA_KEY_WIDTH), lambda i, j: (i, 0)),
            pl.BlockSpec((1, sb, GLA_HEADS, GLA_DK, GLA_DV), lambda i, j: (layer, i * halves + j, 0, 0, 0)),
            pl.BlockSpec((1, POOL_HIST, rb, D_MODEL), lambda i, j: (layer, 0, i, 0)),
            pl.BlockSpec((1, sb, n_mem, SUBLANES, LANES), lambda i, j: (layer, i * halves + j, 0, 0, 0)),
            pl.BlockSpec((1, sb, n_mem, SUBLANES, LANES), lambda i, j: (layer, i * halves + j, 0, 0, 0)),
            _const_spec((len(POOL_WINDOWS), POOL_GROUP_DIM, POOL_GROUP_DIM)),
            _const_spec((1, D_MODEL)),
            _const_spec((1, D_MODEL)),
        ] + [any_spec] * n_alias,
        out_specs=[
            pl.BlockSpec((N_BRANCH, rb, D_MODEL), lambda i, j: (0, i, 0)),
            pl.BlockSpec((1, sb, GLA_HEADS, GLA_DK, GLA_DV), lambda i, j: (layer, i * halves + j, 0, 0, 0)),
            pl.BlockSpec((1, POOL_HIST, rb, D_MODEL), lambda i, j: (layer, 0, i, 0)),
        ],
        out_shape=[
            jax.ShapeDtypeStruct((N_BRANCH, nb, D_MODEL), F32),
            jax.ShapeDtypeStruct(s0.shape, F32),
            jax.ShapeDtypeStruct(hist.shape, F32),
        ],
        scratch_shapes=[pltpu.VMEM((rb, D_MODEL), F32)],
        input_output_aliases={n_in + a: 1 + a for a in range(n_alias)},
        compiler_params=_params(
            ("parallel", "arbitrary"),
            ((N_MIX_SLABS + N_BRANCH + 1, rb, D_MODEL), F32, 2),
            ((sb, GLA_HEADS, GLA_DK, GLA_DV), F32, 4),
            ((POOL_HIST, rb, D_MODEL), F32, 4),
            ((sb, n_mem, SUBLANES, LANES), F32, 4),
            ((len(POOL_WINDOWS), POOL_GROUP_DIM, POOL_GROUP_DIM), BF, 1)),
        name="sample_mix",
    )(p, la, s0, hist, ck, cv, pool_w, pool_scale, gla_gain, *(carry or ()))


def _merge_kernel(br_ref, p_ref, x_ref, wbr_ref, wo_ref, fg_ref, out_ref, *, final):
    merged = _merge_term(br_ref, p_ref, wbr_ref, 0)
    for n in range(1, N_BRANCH):
        merged = merged + _merge_term(br_ref, p_ref, wbr_ref, n)
    out_ref[...] = _merge_finish(merged, x_ref[...], wo_ref, fg_ref, final)


def _merge_out(br, p, x, w_branch, w_out, final_gain, layer, final):
    m_rows = x.shape[0]
    whole = lambda shape: pl.BlockSpec(shape, lambda i: (0,) * len(shape))
    return pl.pallas_call(
        functools.partial(_merge_kernel, final=final),
        grid=(1,),
        in_specs=[
            whole((N_BRANCH, m_rows, D_MODEL)),
            whole((N_SLABS, m_rows, D_MODEL)),
            whole((m_rows, D_MODEL)),
            _layer_spec((N_BRANCH, D_MODEL // 2, D_MODEL), layer),
            _layer_spec((D_MODEL // 2, D_MODEL), layer),
            _const_spec((1, D_MODEL)),
        ],
        out_specs=whole((m_rows, D_MODEL)),
        out_shape=jax.ShapeDtypeStruct((m_rows, D_MODEL), F32),
        compiler_params=_params(("arbitrary",),
                                ((N_BRANCH + N_SLABS + 2, m_rows, D_MODEL), F32, 2),
                                ((D_MODEL // 2, (N_BRANCH + 1) * D_MODEL), jnp.uint32, 1)),
        name="merge_out",
    )(br, p, x, w_branch, w_out, final_gain)


def kernel(x_prompt, x_sample, mem_prompt, cache_mem_k, cache_mem_v, state_gla, state_pool, w_in, w_a2, b_a, gla_gain, pool_w, pool_scale, w_mk, w_mv, w_branch, w_out, norm_gain, final_gain):
    batch, seq, _ = x_prompt.shape
    nb = x_sample.shape[0]
    n_mem = mem_prompt.shape[1]
    depth = w_in.shape[0]
    tm = PROMPT_TILE
    assert seq % tm == 0 and nb % SUBLANES == 0 and x_prompt.shape[2] == D_MODEL

    xp = x_prompt.reshape(batch * seq, D_MODEL)
    xs = x_sample.reshape(nb, D_MODEL)
    mem = mem_prompt.reshape(batch * n_mem, D_MODEL)
    fgain = final_gain.reshape(1, D_MODEL)

    mk, mv, mk_bf, mv_bf = _kvproj(mem, w_mk.astype(BF), w_mv.astype(BF), batch, n_mem)

    w_in_t = jnp.swapaxes(w_in, 1, 2)
    pool_rows = jnp.swapaxes(state_pool, 1, 2)
    w_proj = _pack_w_in(w_in_t)
    wb = _pack_matrices(w_branch.reshape(depth * N_BRANCH, D_MODEL, D_MODEL))
    wb = wb.reshape(depth, N_BRANCH, D_MODEL // 2, D_MODEL)
    wo = _pack_matrices(w_out)

    carry_p, carry_s = None, None
    for l in range(depth):
        final = l == depth - 1
        w_alr = w_in_t[l, ALR_START:ALR_START + GATE_RANK, :]
        wa2 = w_a2[l].astype(BF)
        ba = b_a[l].reshape(1, GLA_KEY_WIDTH)
        ngain = norm_gain[l].reshape(1, D_MODEL)
        ggain = gla_gain[l].reshape(1, D_MODEL)
        pscale = pool_scale[l].reshape(1, D_MODEL)
        pw = pool_w[l].astype(BF)

        xp, s_all, hist_all = _prompt_layer(xp, ngain, w_proj, w_alr, wa2, ba, mk_bf, mv_bf, pw, pscale,
                                            ggain, wb, wo, fgain, carry_p, l, depth, batch, seq, n_mem, tm, final)
        carry_p = (s_all, hist_all)

        ps, las = _inproj(xs, ngain, w_proj, w_alr, wa2, ba, l, nb, F32)
        brs, s_new, hist_new = _sample_mix(ps, las, state_gla, pool_rows, cache_mem_k, cache_mem_v,
                                           pw, pscale, ggain, carry_s, l)
        carry_s = (s_new, hist_new)
        xs = _merge_out(brs, ps, xs, wb, wo, fgain, l, final)

    return (xp.reshape(batch, seq, D_MODEL), xs.reshape(nb, 1, D_MODEL),
            mk, mv, carry_p[0], carry_p[1], carry_s[0], jnp.swapaxes(carry_s[1], 1, 2))
```

```python
import functools
import math

import jax
import jax.numpy as jnp
from jax import lax
from jax.experimental import pallas as pl
from jax.experimental.pallas import tpu as pltpu

D_MODEL = 1024
GLA_HEADS = 4
GLA_DK = 128
GLA_DV = 256
GLA_KEY_WIDTH = GLA_HEADS * GLA_DK
GATE_RANK = 16
GATE_TAU = 16.0
CHUNK = 128
POOL_WINDOWS = (2, 4, 8, 16)
POOL_GROUP_DIM = 256
POOL_HIST = 15
HIST_PAD = 16
XA_HEADS = 4
XA_HEAD_DIM = 256
N_BRANCH = 3
EPS = 1e-6
SUBLANES = 8
LANES = 128

SLAB_QK, SLAB_V, SLAB_GLA_G, SLAB_U, SLAB_POOL_G, SLAB_XQ, SLAB_XG, SLAB_MERGE = 0, 1, 2, 3, 4, 5, 6, 7
N_SLABS = 10
N_MIX_SLABS = 7
N_HEAD_SLABS = 3
ALR_START = N_HEAD_SLABS * D_MODEL

BF = jnp.bfloat16
F32 = jnp.float32
MIB = 1 << 20

PROMPT_TILE = 256
SAMPLE_BLOCK = 4
VMEM_COMPILER_SCRATCH = 8 * MIB


def _dot(a, b):
    return jnp.dot(a, b, preferred_element_type=F32)


def _dot_nt(a, b):
    return lax.dot_general(a, b, (((1,), (1,)), ((), ())), preferred_element_type=F32)


def _dot_tn(a, b):
    return lax.dot_general(a, b, (((0,), (0,)), ((), ())), preferred_element_type=F32)


def _pack_rows(w):
    return pltpu.bitcast(w.astype(BF), jnp.uint32)


def _unpack_rows(w_words):
    return pltpu.bitcast(w_words, BF)


def _params(sem, *buffers):
    need = sum(math.prod(shape) * jnp.dtype(dtype).itemsize * copies for shape, dtype, copies in buffers)
    return pltpu.CompilerParams(dimension_semantics=sem, vmem_limit_bytes=need + VMEM_COMPILER_SCRATCH)


def _pack_w_in_kernel(a_ref, b_ref, o_ref):
    j = pl.program_id(1)

    @pl.when(j < N_HEAD_SLABS)
    def _():
        o_ref[0] = _pack_rows(a_ref[0].T)

    @pl.when(j >= N_HEAD_SLABS)
    def _():
        o_ref[0] = _pack_rows(jnp.concatenate([a_ref[0, GATE_RANK:, :], b_ref[0]], axis=0).T)


def _pack_w_in(w_in_t):
    depth = w_in_t.shape[0]
    return pl.pallas_call(
        _pack_w_in_kernel,
        grid=(depth, N_SLABS),
        in_specs=[
            pl.BlockSpec((1, D_MODEL, D_MODEL), lambda l, j: (l, j, 0)),
            pl.BlockSpec((1, GATE_RANK, D_MODEL), lambda l, j: (l, (j + 1) * (D_MODEL // GATE_RANK), 0)),
        ],
        out_specs=pl.BlockSpec((1, D_MODEL // 2, D_MODEL), lambda l, j: (l, 0, j)),
        out_shape=jax.ShapeDtypeStruct((depth, D_MODEL // 2, N_SLABS * D_MODEL), jnp.uint32),
        compiler_params=_params(("parallel", "parallel"),
                                ((D_MODEL, D_MODEL), F32, 2), ((GATE_RANK, D_MODEL), F32, 2),
                                ((D_MODEL // 2, D_MODEL), jnp.uint32, 2)),
        name="pack_w_in",
    )(w_in_t, w_in_t)


def _pack_matrices_kernel(a_ref, o_ref):
    o_ref[0] = _pack_rows(a_ref[0])


def _pack_matrices(w):
    return pl.pallas_call(
        _pack_matrices_kernel,
        grid=(w.shape[0],),
        in_specs=[pl.BlockSpec((1, D_MODEL, D_MODEL), lambda r: (r, 0, 0))],
        out_specs=pl.BlockSpec((1, D_MODEL // 2, D_MODEL), lambda r: (r, 0, 0)),
        out_shape=jax.ShapeDtypeStruct((w.shape[0], D_MODEL // 2, D_MODEL), jnp.uint32),
        compiler_params=_params(("parallel",), ((D_MODEL, D_MODEL), F32, 2), ((D_MODEL // 2, D_MODEL), jnp.uint32, 2)),
        name="pack_matrices",
    )(w)


def _silu(x):
    return x * jax.nn.sigmoid(x)


def _rms(x, gain):
    ms = jnp.mean(x * x, axis=-1, keepdims=True)
    return x * lax.rsqrt(ms + EPS) * gain


def _const_spec(shape):
    zeros = (0,) * len(shape)
    return pl.BlockSpec(shape, lambda *_: zeros, pipeline_mode=pl.Buffered(1))


def _layer_spec(shape, layer):
    index = (layer,) + (0,) * len(shape)
    return pl.BlockSpec((1,) + tuple(shape), lambda *_: index, pipeline_mode=pl.Buffered(1))


def _drop_alias_refs(body, n_in, n_alias):
    def kern(*refs):
        return body(*refs[:n_in], *refs[n_in + n_alias:])
    return kern


def _kvproj_kernel(m_ref, wk_ref, wv_ref, k_ref, v_ref, kb_ref, vb_ref):
    m = m_ref[...].astype(BF)
    k = _dot(m, wk_ref[0])
    v = _dot(m, wv_ref[0])
    for h in range(XA_HEADS):
        cs = slice(h * XA_HEAD_DIM, (h + 1) * XA_HEAD_DIM)
        k_ref[0, 0, :, h, :] = k[:, cs]
        v_ref[0, 0, :, h, :] = v[:, cs]
    kb_ref[0] = k.astype(BF)
    vb_ref[0] = v.astype(BF)


def _kvproj(mem, wk, wv, batch, n_mem):
    depth = wk.shape[0]
    w_spec = pl.BlockSpec((1, D_MODEL, D_MODEL), lambda l, b: (l, 0, 0))
    out5 = pl.BlockSpec((1, 1, n_mem, XA_HEADS, XA_HEAD_DIM), lambda l, b: (l, b, 0, 0, 0))
    out_bf = pl.BlockSpec((1, n_mem, D_MODEL), lambda l, b: (l, b, 0))
    return pl.pallas_call(
        _kvproj_kernel,
        grid=(depth, batch),
        in_specs=[pl.BlockSpec((n_mem, D_MODEL), lambda l, b: (b, 0)), w_spec, w_spec],
        out_specs=[out5, out5, out_bf, out_bf],
        out_shape=[jax.ShapeDtypeStruct((depth, batch, n_mem, XA_HEADS, XA_HEAD_DIM), F32)] * 2
        + [jax.ShapeDtypeStruct((depth, batch * n_mem, D_MODEL), BF)] * 2,
        compiler_params=_params(("parallel", "parallel"),
                                ((n_mem, D_MODEL), F32, 2), ((D_MODEL, D_MODEL), BF, 4),
                                ((n_mem, D_MODEL), F32, 4), ((n_mem, D_MODEL), BF, 4)),
        name="kvproj",
    )(mem, wk, wv)


def _inproj_slab(h, w_ref, p_out, j):
    w = w_ref[0, :, j * D_MODEL:(j + 1) * D_MODEL]
    p_out[j] = _dot(h, _unpack_rows(w)).astype(p_out.dtype)


def _inproj_gate_lowrank(h, walr_ref):
    return _dot_nt(h, walr_ref[...].astype(BF)).astype(BF)


def _inproj_gate(alr, wa2_ref, ba_ref, la_out):
    z = _dot(alr, wa2_ref[...]) + ba_ref[...]
    la_out[...] = (jnp.minimum(z, 0.0) - jnp.log(1.0 + jnp.exp(-jnp.abs(z)))) * (1.0 / GATE_TAU)


def _merge_term(br_ref, p, wbr_ref, n):
    return jax.nn.sigmoid(p[SLAB_MERGE + n].astype(F32)) * _dot(br_ref[n].astype(BF), _unpack_rows(wbr_ref[0, n]))


def _merge_finish(merged, x, wo_ref, fg_ref, final):
    x_new = x + _dot(merged.astype(BF), _unpack_rows(wo_ref[0]))
    return _rms(x_new, fg_ref[...]) if final else x_new


def _chunk_cumsum_matrix(tm):
    row = lax.broadcasted_iota(jnp.int32, (tm, tm), 0)
    col = lax.broadcasted_iota(jnp.int32, (tm, tm), 1)
    return (((row // CHUNK) == (col // CHUNK)) & (row >= col)).astype(BF)


def _window_matrices(tm):
    row = lax.broadcasted_iota(jnp.int32, (tm, tm), 0)
    col = lax.broadcasted_iota(jnp.int32, (tm, tm), 1)
    return jnp.stack([((row >= col) & (row - col < w)).astype(BF) for w in POOL_WINDOWS])


def _gla_cumdecay(cum_ref, la_ref):
    la = la_ref[...]
    la_hi = la.astype(BF)
    la_lo = (la - la_hi.astype(F32)).astype(BF)
    return _dot(cum_ref[...], la_hi) + _dot(cum_ref[...], la_lo)


def _gla_chunk(p, bcum, gg_ref, br_ref, st_ref, c, filler):
    crow = lax.broadcasted_iota(jnp.int32, (CHUNK, CHUNK), 0)
    ccol = lax.broadcasted_iota(jnp.int32, (CHUNK, CHUNK), 1)
    causal = crow >= ccol
    rows = slice(c * CHUNK, (c + 1) * CHUNK)
    heads = range(GLA_HEADS)
    q_dec, k_end, decay, att = [], [], [], []
    for h in heads:
        kc = slice(h * GLA_DK, (h + 1) * GLA_DK)
        kc2 = slice(GLA_KEY_WIDTH + h * GLA_DK, GLA_KEY_WIDTH + (h + 1) * GLA_DK)
        b = bcum[rows, kc]
        b_mid = b[CHUNK // 2 - 1:CHUNK // 2, :]
        b_last = b[CHUNK - 1:CHUNK, :]
        q = p[SLAB_QK, rows, kc].astype(F32) * (GLA_DK ** -0.5)
        k = p[SLAB_QK, rows, kc2].astype(F32)
        q_dec.append((q * jnp.exp(b)).astype(BF))
        q_mid = (q * jnp.exp(b - b_mid)).astype(BF)
        k_mid = (k * jnp.exp(b_mid - b)).astype(BF)
        k_end.append((k * jnp.exp(b_last - b)).astype(BF))
        decay.append(jnp.exp(b_last))
        att.append(_dot_nt(q_mid, k_mid))
    filler()
    o = []
    for h in heads:
        vc = slice(h * GLA_DV, (h + 1) * GLA_DV)
        a = jnp.where(causal, att[h], 0.0).astype(BF)
        s_before = st_ref[h].T.astype(BF)
        o.append(_dot(jnp.concatenate([a, q_dec[h]], axis=1),
                      jnp.concatenate([p[SLAB_V, rows, vc], s_before], axis=0)))
    for h in heads:
        vc = slice(h * GLA_DV, (h + 1) * GLA_DV)
        st_ref[h] = decay[h] * st_ref[h] + _dot_tn(p[SLAB_V, rows, vc], k_end[h])
    for h in heads:
        vc = slice(h * GLA_DV, (h + 1) * GLA_DV)
        g = p[SLAB_GLA_G, rows, vc].astype(F32)
        br_ref[0, rows, vc] = (_rms(o[h], gg_ref[:, vc]) * _silu(g)).astype(BF)


def _pool_window_sums(p, win_ref):
    sums = []
    for g in range(len(POOL_WINDOWS)):
        cs = slice(g * POOL_GROUP_DIM, (g + 1) * POOL_GROUP_DIM)
        sums.append(_dot(win_ref[g], p[SLAB_U, :, cs]))
    return sums


def _pool_branch(p, sums, pw_ref, ps_ref, br_ref, ubuf_ref, t, tm):
    u = p[SLAB_U].astype(F32)
    ubuf_ref[HIST_PAD:2 * HIST_PAD, :] = u[0:HIST_PAD]
    pos = t * tm + lax.broadcasted_iota(jnp.int32, (tm, 1), 0)
    for g, w in enumerate(POOL_WINDOWS):
        cs = slice(g * POOL_GROUP_DIM, (g + 1) * POOL_GROUP_DIM)
        ug = u[:, cs]
        head = ug[0:HIST_PAD]
        for j in range(1, w):
            head = head + ubuf_ref[HIST_PAD - j:2 * HIST_PAD - j, cs]
        s = jnp.concatenate([head, sums[g][HIST_PAD:]], axis=0)
        cnt = jnp.minimum(w, pos + 1).astype(F32)
        diff = s / cnt - ug
        mixed = _dot(diff.astype(BF), pw_ref[g]) * ps_ref[:, cs]
        pg = p[SLAB_POOL_G, :, cs].astype(F32)
        br_ref[1, :, cs] = (mixed * _silu(pg)).astype(BF)
    ubuf_ref[0:HIST_PAD, :] = u[tm - HIST_PAD:tm]


def _xattn_probs(p, mk_ref):
    out = []
    for h in range(XA_HEADS):
        cs = slice(h * XA_HEAD_DIM, (h + 1) * XA_HEAD_DIM)
        s = _dot_nt(p[SLAB_XQ, :, cs], mk_ref[0, :, cs]) * (XA_HEAD_DIM ** -0.5)
        pr = jnp.exp(s - jnp.max(s, axis=-1, keepdims=True))
        out.append((pr.astype(BF), jnp.sum(pr, axis=-1, keepdims=True)))
    return out


def _xattn_branch(p, probs, mv_ref, br_ref):
    for h in range(XA_HEADS):
        cs = slice(h * XA_HEAD_DIM, (h + 1) * XA_HEAD_DIM)
        pr, denom = probs[h]
        o = _dot(pr, mv_ref[0, :, cs]) / denom
        xg = p[SLAB_XG, :, cs].astype(F32)
        br_ref[2, :, cs] = (o * _silu(xg)).astype(BF)


def _prompt_layer_kernel(xn_ref, xc_ref, g_ref, w_ref, walr_ref, wa2_ref, ba_ref, cum_ref, win_ref,
                         mk_ref, mv_ref, pw_ref, ps_ref, gg_ref, wbr_ref, wo_ref, fg_ref,
                         out_ref, s_out_ref, hist_out_ref,
                         h_scr, p_scr, la_scr, br_scr, st_ref, ubuf_ref, *, tm, nt, final):
    s = pl.program_id(0)
    t = jnp.maximum(s - 1, 0) % nt
    slot_w = s % 2
    slot_r = 1 - slot_w

    @pl.when(s == 0)
    def _():
        p_scr[1] = jnp.zeros(p_scr.shape[1:], p_scr.dtype)
        la_scr[1] = jnp.zeros(la_scr.shape[1:], la_scr.dtype)

    @pl.when(t == 0)
    def _():
        st_ref[...] = jnp.zeros_like(st_ref)
        ubuf_ref[0:HIST_PAD, :] = jnp.zeros((HIST_PAD, D_MODEL), F32)

    @pl.when(s == 0)
    def _():
        h_scr[0] = _rms(xc_ref[...], g_ref[...]).astype(BF)

    p_next = p_scr.at[slot_w]
    p = p_scr.at[slot_r]

    slabs = list(range(N_SLABS))

    def filler(n=1):
        for _ in range(n):
            _inproj_slab(h_scr[slot_w], w_ref, p_next, slabs.pop(0))

    alr = _inproj_gate_lowrank(h_scr[slot_w], walr_ref)
    bcum = _gla_cumdecay(cum_ref, la_scr.at[slot_r])
    filler()
    _inproj_gate(alr, wa2_ref, ba_ref, la_scr.at[slot_w])
    per_site = (N_SLABS - 2) // (2 * (tm // CHUNK))
    probs, sums = None, None
    for c in range(tm // CHUNK):
        _gla_chunk(p, bcum, gg_ref, br_scr, st_ref, c, functools.partial(filler, per_site))
        filler(per_site)
        if c == 0:
            probs = _xattn_probs(p, mk_ref)
            sums = _pool_window_sums(p, win_ref)
        elif c == 1:
            _xattn_branch(p, probs, mv_ref, br_scr)
            _pool_branch(p, sums, pw_ref, ps_ref, br_scr, ubuf_ref, t, tm)
    h_scr[slot_r] = _rms(xn_ref[...], g_ref[...]).astype(BF)
    merged = _merge_term(br_scr, p, wbr_ref, 2)
    merged = merged + _merge_term(br_scr, p, wbr_ref, 1)
    merged = merged + _merge_term(br_scr, p, wbr_ref, 0)
    filler()
    assert not slabs
    out_ref[...] = _merge_finish(merged, xc_ref[...], wo_ref, fg_ref, final)

    @pl.when((s > 0) & (t == nt - 1))
    def _():
        for h in range(GLA_HEADS):
            s_out_ref[0, 0, h] = st_ref[h].T
        hist_out_ref[0, 0] = ubuf_ref[1:HIST_PAD, :]


def _prompt_layer(x, ngain, w_proj, w_alr, wa2, ba, mk, mv, pool_w, pool_scale, gla_gain,
                  w_branch, w_out, fgain, carry, layer, depth, batch, seq, n_mem, tm, final):
    nt = seq // tm
    n_tiles = batch * nt
    assert tm // CHUNK == 2 and (N_SLABS - 2) % (2 * (tm // CHUNK)) == 0, "slab placement assumes two GLA chunks"
    n_alias = 0 if carry is None else len(carry)
    n_in = 17
    kern = _drop_alias_refs(functools.partial(_prompt_layer_kernel, tm=tm, nt=nt, final=final), n_in, n_alias)
    any_spec = pl.BlockSpec(memory_space=pl.ANY)
    scratch = [
        ((2, tm, D_MODEL), BF, 1),
        ((2, N_SLABS, tm, D_MODEL), BF, 1),
        ((2, tm, GLA_KEY_WIDTH), F32, 1),
        ((N_BRANCH, tm, D_MODEL), BF, 1),
        ((GLA_HEADS, GLA_DV, GLA_DK), F32, 1),
        ((2 * HIST_PAD, D_MODEL), F32, 1),
    ]

    def cur(s):
        return jnp.maximum(s - 1, 0)

    def seq_of(s):
        return cur(s) // nt

    return pl.pallas_call(
        kern,
        grid=(n_tiles + 1,),
        in_specs=[
            pl.BlockSpec((tm, D_MODEL), lambda s: (jnp.minimum(s + 1, n_tiles - 1), 0)),
            pl.BlockSpec((tm, D_MODEL), lambda s: (cur(s), 0)),
            _const_spec((1, D_MODEL)),
            _layer_spec((D_MODEL // 2, N_SLABS * D_MODEL), layer),
            _const_spec((GATE_RANK, D_MODEL)),
            _const_spec((GATE_RANK, GLA_KEY_WIDTH)),
            _const_spec((1, GLA_KEY_WIDTH)),
            _const_spec((tm, tm)),
            _const_spec((len(POOL_WINDOWS), tm, tm)),
            pl.BlockSpec((1, n_mem, D_MODEL), lambda s: (layer, seq_of(s), 0)),
            pl.BlockSpec((1, n_mem, D_MODEL), lambda s: (layer, seq_of(s), 0)),
            _const_spec((len(POOL_WINDOWS), POOL_GROUP_DIM, POOL_GROUP_DIM)),
            _const_spec((1, D_MODEL)),
            _const_spec((1, D_MODEL)),
            _layer_spec((N_BRANCH, D_MODEL // 2, D_MODEL), layer),
            _layer_spec((D_MODEL // 2, D_MODEL), layer),
            _const_spec((1, D_MODEL)),
        ] + [any_spec] * n_alias,
        out_specs=[
            pl.BlockSpec((tm, D_MODEL), lambda s: (cur(s), 0)),
            pl.BlockSpec((1, 1, GLA_HEADS, GLA_DK, GLA_DV), lambda s: (layer, seq_of(s), 0, 0, 0)),
            pl.BlockSpec((1, 1, POOL_HIST, D_MODEL), lambda s: (layer, seq_of(s), 0, 0)),
        ],
        out_shape=[
            jax.ShapeDtypeStruct((n_tiles * tm, D_MODEL), F32),
            jax.ShapeDtypeStruct((depth, batch, GLA_HEADS, GLA_DK, GLA_DV), F32),
            jax.ShapeDtypeStruct((depth, batch, POOL_HIST, D_MODEL), F32),
        ],
        scratch_shapes=[pltpu.VMEM(shape, dtype) for shape, dtype, _ in scratch],
        input_output_aliases={n_in + a: 1 + a for a in range(n_alias)},
        compiler_params=_params(
            ("arbitrary",), *scratch,
            ((tm, D_MODEL), F32, 6),
            ((D_MODEL // 2, (N_SLABS + N_BRANCH + 1) * D_MODEL), jnp.uint32, 1),
            ((n_mem, D_MODEL), BF, 4),
            ((tm, tm), BF, 1 + len(POOL_WINDOWS)),
            ((len(POOL_WINDOWS), POOL_GROUP_DIM, POOL_GROUP_DIM), BF, 1),
            ((GLA_HEADS, GLA_DK, GLA_DV), F32, 2)),
        name="prompt_layer",
    )(x, x, ngain, w_proj, w_alr, wa2, ba, _chunk_cumsum_matrix(tm), _window_matrices(tm),
      mk, mv, pool_w, pool_scale, gla_gain,
      w_branch, w_out, fgain, *(carry or ()))


def _inproj_kernel(x_ref, g_ref, w_ref, walr_ref, wa2_ref, ba_ref, p_ref, la_ref):
    h = _rms(x_ref[...], g_ref[...]).astype(BF)
    alr = _inproj_gate_lowrank(h, walr_ref)
    for j in range(N_SLABS):
        _inproj_slab(h, w_ref, p_ref, j)
    _inproj_gate(alr, wa2_ref, ba_ref, la_ref)


def _inproj(x, gain, w_proj, w_alr, w_a2, b_a, layer, tm, out_dtype):
    m_rows = x.shape[0]
    return pl.pallas_call(
        _inproj_kernel,
        grid=(m_rows // tm,),
        in_specs=[
            pl.BlockSpec((tm, D_MODEL), lambda i: (i, 0)),
            _const_spec((1, D_MODEL)),
            _layer_spec((D_MODEL // 2, N_SLABS * D_MODEL), layer),
            _const_spec((GATE_RANK, D_MODEL)),
            _const_spec((GATE_RANK, GLA_KEY_WIDTH)),
            _const_spec((1, GLA_KEY_WIDTH)),
        ],
        out_specs=[
            pl.BlockSpec((N_SLABS, tm, D_MODEL), lambda i: (0, i, 0)),
            pl.BlockSpec((tm, GLA_KEY_WIDTH), lambda i: (i, 0)),
        ],
        out_shape=[
            jax.ShapeDtypeStruct((N_SLABS, m_rows, D_MODEL), out_dtype),
            jax.ShapeDtypeStruct((m_rows, GLA_KEY_WIDTH), F32),
        ],
        compiler_params=_params(("parallel",),
                                ((tm, D_MODEL), F32, 2), ((D_MODEL // 2, N_SLABS * D_MODEL), jnp.uint32, 1),
                                ((N_SLABS, tm, D_MODEL), out_dtype, 2), ((tm, GLA_KEY_WIDTH), F32, 2)),
        name="inproj",
    )(x, gain, w_proj, w_alr, w_a2, b_a)


def _sample_mix_kernel(p_ref, la_ref, s0_ref, hist_ref, ck_ref, cv_ref, pw_ref, ps_ref, gg_ref,
                       br_ref, s_out_ref, hist_out_ref, diff_ref, *, sb):
    r0 = pl.program_id(1) * sb
    erow = lax.broadcasted_iota(jnp.int32, (GLA_DK, GLA_DK), 0)
    ecol = lax.broadcasted_iota(jnp.int32, (GLA_DK, GLA_DK), 1)
    eye = erow == ecol

    def to_col(x):
        return jnp.sum(jnp.where(eye, jnp.broadcast_to(x, (GLA_DK, GLA_DK)), 0.0), axis=1, keepdims=True)

    for i in range(sb):
        r = pl.ds(r0 + i, 1)
        la = la_ref[r, :]
        qk = p_ref[SLAB_QK, r, :]
        vv = p_ref[SLAB_V, r, :]
        gla_g = p_ref[SLAB_GLA_G, r, :]
        for h in range(GLA_HEADS):
            kc = slice(h * GLA_DK, (h + 1) * GLA_DK)
            kc2 = slice(GLA_KEY_WIDTH + h * GLA_DK, GLA_KEY_WIDTH + (h + 1) * GLA_DK)
            vc = slice(h * GLA_DV, (h + 1) * GLA_DV)
            a_col = to_col(jnp.exp(la[:, kc]))
            q_col = to_col(qk[:, kc] * (GLA_DK ** -0.5))
            k_col = to_col(qk[:, kc2])
            s_new = a_col * s0_ref[0, i, h] + k_col * vv[:, vc]
            s_out_ref[0, i, h] = s_new
            o = jnp.sum(q_col * s_new, axis=0, keepdims=True)
            br_ref[0, r, vc] = _rms(o, gg_ref[:, vc]) * _silu(gla_g[:, vc])

        u = p_ref[SLAB_U, r, :]
        for g, w in enumerate(POOL_WINDOWS):
            cs = slice(g * POOL_GROUP_DIM, (g + 1) * POOL_GROUP_DIM)
            past = jnp.sum(hist_ref[0, POOL_HIST - (w - 1):POOL_HIST, r, cs], axis=0)
            diff_ref[r, cs] = (u[:, cs] + past) / float(w) - u[:, cs]
        hist_out_ref[0, 0:POOL_HIST - 1, r, :] = hist_ref[0, 1:POOL_HIST, r, :]
        hist_out_ref[0, POOL_HIST - 1, r, :] = u

        xq = p_ref[SLAB_XQ, r, :]
        xg = p_ref[SLAB_XG, r, :]
        half_cols = [slice(h * XA_HEAD_DIM + j * LANES, h * XA_HEAD_DIM + (j + 1) * LANES)
                     for j in range(XA_HEAD_DIM // LANES) for h in range(XA_HEADS)]
        xq_rows = jnp.concatenate([xq[:, cs] for cs in half_cols], axis=0)
        n_mem = ck_ref.shape[2]
        prod = (ck_ref[0, i] * xq_rows[None]).reshape(n_mem * SUBLANES, LANES).astype(BF)
        part = _dot(prod, jnp.ones((LANES, LANES), BF)).reshape(n_mem, SUBLANES, LANES)
        s = (part + pltpu.roll(part, XA_HEADS, axis=1)) * (XA_HEAD_DIM ** -0.5)
        p = jnp.exp(s - jnp.max(s, axis=0, keepdims=True))
        o = jnp.sum(p * cv_ref[0, i], axis=0) / jnp.sum(p, axis=0)
        halves = XA_HEAD_DIM // LANES
        o_row = jnp.concatenate([o[j * XA_HEADS + h:j * XA_HEADS + h + 1, :]
                                 for h in range(XA_HEADS) for j in range(halves)], axis=1)
        br_ref[2, r, :] = o_row * _silu(xg)

    @pl.when(pl.program_id(1) == pl.num_programs(1) - 1)
    def _():
        for g in range(len(POOL_WINDOWS)):
            cs = slice(g * POOL_GROUP_DIM, (g + 1) * POOL_GROUP_DIM)
            mixed = _dot(diff_ref[:, cs].astype(BF), pw_ref[g]) * ps_ref[:, cs]
            br_ref[1, :, cs] = mixed * _silu(p_ref[SLAB_POOL_G, :, cs])


def _cache_rows_view(c):
    depth, nb, n_mem = c.shape[:3]
    halves = XA_HEAD_DIM // LANES
    c = c.reshape(depth, nb, n_mem, XA_HEADS, halves, LANES)
    return c.transpose(0, 1, 2, 4, 3, 5).reshape(depth, nb, n_mem, halves * XA_HEADS, LANES)


def _sample_mix(p, la, s0, hist, ck, cv, pool_w, pool_scale, gla_gain, carry, layer, sb=SAMPLE_BLOCK):
    nb = s0.shape[1]
    n_mem = ck.shape[2]
    ck, cv = _cache_rows_view(ck), _cache_rows_view(cv)
    rb = SUBLANES
    halves = rb // sb
    n_alias = 0 if carry is None else len(carry)
    n_in = 9
    kern = _drop_alias_refs(functools.partial(_sample_mix_kernel, sb=sb), n_in, n_alias)
    any_spec = pl.BlockSpec(memory_space=pl.ANY)
    return pl.pallas_call(
        kern,
        grid=(nb // rb, halves),
        in_specs=[
            pl.BlockSpec((N_MIX_SLABS, rb, D_MODEL), lambda i, j: (0, i, 0)),
            pl.BlockSpec((rb, GLA_KEY_WIDTH), lambda i, j: (i, 0)),
            pl.BlockSpec((1, sb, GLA_HEADS, GLA_DK, GLA_DV), lambda i, j: (layer, i * halves + j, 0, 0, 0)),
            pl.BlockSpec((1, POOL_HIST, rb, D_MODEL), lambda i, j: (layer, 0, i, 0)),
            pl.BlockSpec((1, sb, n_mem, SUBLANES, LANES), lambda i, j: (layer, i * halves + j, 0, 0, 0)),
            pl.BlockSpec((1, sb, n_mem, SUBLANES, LANES), lambda i, j: (layer, i * halves + j, 0, 0, 0)),
            _const_spec((len(POOL_WINDOWS), POOL_GROUP_DIM, POOL_GROUP_DIM)),
            _const_spec((1, D_MODEL)),
            _const_spec((1, D_MODEL)),
        ] + [any_spec] * n_alias,
        out_specs=[
            pl.BlockSpec((N_BRANCH, rb, D_MODEL), lambda i, j: (0, i, 0)),
            pl.BlockSpec((1, sb, GLA_HEADS, GLA_DK, GLA_DV), lambda i, j: (layer, i * halves + j, 0, 0, 0)),
            pl.BlockSpec((1, POOL_HIST, rb, D_MODEL), lambda i, j: (layer, 0, i, 0)),
        ],
        out_shape=[
            jax.ShapeDtypeStruct((N_BRANCH, nb, D_MODEL), F32),
            jax.ShapeDtypeStruct(s0.shape, F32),
            jax.ShapeDtypeStruct(hist.shape, F32),
        ],
        scratch_shapes=[pltpu.VMEM((rb, D_MODEL), F32)],
        input_output_aliases={n_in + a: 1 + a for a in range(n_alias)},
        compiler_params=_params(
            ("parallel", "arbitrary"),
            ((N_MIX_SLABS + N_BRANCH + 1, rb, D_MODEL), F32, 2),
            ((sb, GLA_HEADS, GLA_DK, GLA_DV), F32, 4),
            ((POOL_HIST, rb, D_MODEL), F32, 4),
            ((sb, n_mem, SUBLANES, LANES), F32, 4),
            ((len(POOL_WINDOWS), POOL_GROUP_DIM, POOL_GROUP_DIM), BF, 1)),
        name="sample_mix",
    )(p, la, s0, hist, ck, cv, pool_w, pool_scale, gla_gain, *(carry or ()))


def _merge_kernel(br_ref, p_ref, x_ref, wbr_ref, wo_ref, fg_ref, out_ref, *, final):
    merged = _merge_term(br_ref, p_ref, wbr_ref, 0)
    for n in range(1, N_BRANCH):
        merged = merged + _merge_term(br_ref, p_ref, wbr_ref, n)
    out_ref[...] = _merge_finish(merged, x_ref[...], wo_ref, fg_ref, final)


def _merge_out(br, p, x, w_branch, w_out, final_gain, layer, final):
    m_rows = x.shape[0]
    whole = lambda shape: pl.BlockSpec(shape, lambda i: (0,) * len(shape))
    return pl.pallas_call(
        functools.partial(_merge_kernel, final=final),
        grid=(1,),
        in_specs=[
            whole((N_BRANCH, m_rows, D_MODEL)),
            whole((N_SLABS, m_rows, D_MODEL)),
            whole((m_rows, D_MODEL)),
            _layer_spec((N_BRANCH, D_MODEL // 2, D_MODEL), layer),
            _layer_spec((D_MODEL // 2, D_MODEL), layer),
            _const_spec((1, D_MODEL)),
        ],
        out_specs=whole((m_rows, D_MODEL)),
        out_shape=jax.ShapeDtypeStruct((m_rows, D_MODEL), F32),
        compiler_params=_params(("arbitrary",),
                                ((N_BRANCH + N_SLABS + 2, m_rows, D_MODEL), F32, 2),
                                ((D_MODEL // 2, (N_BRANCH + 1) * D_MODEL), jnp.uint32, 1)),
        name="merge_out",
    )(br, p, x, w_branch, w_out, final_gain)


def kernel(x_prompt, x_sample, mem_prompt, cache_mem_k, cache_mem_v, state_gla, state_pool, w_in, w_a2, b_a, gla_gain, pool_w, pool_scale, w_mk, w_mv, w_branch, w_out, norm_gain, final_gain):
    batch, seq, _ = x_prompt.shape
    nb = x_sample.shape[0]
    n_mem = mem_prompt.shape[1]
    depth = w_in.shape[0]
    tm = PROMPT_TILE
    assert seq % tm == 0 and nb % SUBLANES == 0 and x_prompt.shape[2] == D_MODEL

    xp = x_prompt.reshape(batch * seq, D_MODEL)
    xs = x_sample.reshape(nb, D_MODEL)
    mem = mem_prompt.reshape(batch * n_mem, D_MODEL)
    fgain = final_gain.reshape(1, D_MODEL)

    mk, mv, mk_bf, mv_bf = _kvproj(mem, w_mk.astype(BF), w_mv.astype(BF), batch, n_mem)

    w_in_t = jnp.swapaxes(w_in, 1, 2)
    pool_rows = jnp.swapaxes(state_pool, 1, 2)
    w_proj = _pack_w_in(w_in_t)
    wb = _pack_matrices(w_branch.reshape(depth * N_BRANCH, D_MODEL, D_MODEL))
    wb = wb.reshape(depth, N_BRANCH, D_MODEL // 2, D_MODEL)
    wo = _pack_matrices(w_out)

    carry_p, carry_s = None, None
    for l in range(depth):
        final = l == depth - 1
        w_alr = w_in_t[l, ALR_START:ALR_START + GATE_RANK, :]
        wa2 = w_a2[l].astype(BF)
        ba = b_a[l].reshape(1, GLA_KEY_WIDTH)
        ngain = norm_gain[l].reshape(1, D_MODEL)
        ggain = gla_gain[l].reshape(1, D_MODEL)
        pscale = pool_scale[l].reshape(1, D_MODEL)
        pw = pool_w[l].astype(BF)

        xp, s_all, hist_all = _prompt_layer(xp, ngain, w_proj, w_alr, wa2, ba, mk_bf, mv_bf, pw, pscale,
                                            ggain, wb, wo, fgain, carry_p, l, depth, batch, seq, n_mem, tm, final)
        carry_p = (s_all, hist_all)

        ps, las = _inproj(xs, ngain, w_proj, w_alr, wa2, ba, l, nb, F32)
        brs, s_new, hist_new = _sample_mix(ps, las, state_gla, pool_rows, cache_mem_k, cache_mem_v,
                                           pw, pscale, ggain, carry_s, l)
        carry_s = (s_new, hist_new)
        xs = _merge_out(brs, ps, xs, wb, wo, fgain, l, final)

    return (xp.reshape(batch, seq, D_MODEL), xs.reshape(nb, 1, D_MODEL),
            mk, mv, carry_p[0], carry_p[1], carry_s[0], jnp.swapaxes(carry_s[1], 1, 2))
```

```python
import functools
import math

import jax
import jax.numpy as jnp
from jax import lax
from jax.experimental import pallas as pl
from jax.experimental.pallas import tpu as pltpu

D_MODEL = 1024
GLA_HEADS = 4
GLA_DK = 128
GLA_DV = 256
GLA_KEY_WIDTH = GLA_HEADS * GLA_DK
GATE_RANK = 16
GATE_TAU = 16.0
CHUNK = 128
POOL_WINDOWS = (2, 4, 8, 16)
POOL_GROUP_DIM = 256
POOL_HIST = 15
HIST_PAD = 16
XA_HEADS = 4
XA_HEAD_DIM = 256
N_BRANCH = 3
EPS = 1e-6
SUBLANES = 8
LANES = 128

SLAB_QK, SLAB_V, SLAB_GLA_G, SLAB_U, SLAB_POOL_G, SLAB_XQ, SLAB_XG, SLAB_MERGE = 0, 1, 2, 3, 4, 5, 6, 7
N_SLABS = 10
N_MIX_SLABS = 7
N_HEAD_SLABS = 3
ALR_START = N_HEAD_SLABS * D_MODEL

BF = jnp.bfloat16
F32 = jnp.float32
MIB = 1 << 20

PROMPT_TILE = 256
SAMPLE_BLOCK = 4
VMEM_COMPILER_SCRATCH = 6 * MIB


def _dot(a, b):
    return jnp.dot(a, b, preferred_element_type=F32)


def _dot_nt(a, b):
    return lax.dot_general(a, b, (((1,), (1,)), ((), ())), preferred_element_type=F32)


def _dot_tn(a, b):
    return lax.dot_general(a, b, (((0,), (0,)), ((), ())), preferred_element_type=F32)


def _pack_rows(w):
    return pltpu.bitcast(w.astype(BF), jnp.uint32)


def _unpack_rows(w_words):
    return pltpu.bitcast(w_words, BF)


def _params(sem, *buffers):
    need = sum(math.prod(shape) * jnp.dtype(dtype).itemsize * copies for shape, dtype, copies in buffers)
    return pltpu.CompilerParams(dimension_semantics=sem, vmem_limit_bytes=need + VMEM_COMPILER_SCRATCH)


def _pack_w_in_kernel(a_ref, b_ref, o_ref):
    j = pl.program_id(1)

    @pl.when(j < N_HEAD_SLABS)
    def _():
        o_ref[0] = _pack_rows(a_ref[0].T)

    @pl.when(j >= N_HEAD_SLABS)
    def _():
        o_ref[0] = _pack_rows(jnp.concatenate([a_ref[0, GATE_RANK:, :], b_ref[0]], axis=0).T)


def _pack_w_in(w_in_t):
    depth = w_in_t.shape[0]
    return pl.pallas_call(
        _pack_w_in_kernel,
        grid=(depth, N_SLABS),
        in_specs=[
            pl.BlockSpec((1, D_MODEL, D_MODEL), lambda l, j: (l, j, 0)),
            pl.BlockSpec((1, GATE_RANK, D_MODEL), lambda l, j: (l, (j + 1) * (D_MODEL // GATE_RANK), 0)),
        ],
        out_specs=pl.BlockSpec((1, D_MODEL // 2, D_MODEL), lambda l, j: (l, 0, j)),
        out_shape=jax.ShapeDtypeStruct((depth, D_MODEL // 2, N_SLABS * D_MODEL), jnp.uint32),
        compiler_params=_params(("parallel", "parallel"),
                                ((D_MODEL, D_MODEL), F32, 2), ((GATE_RANK, D_MODEL), F32, 2),
                                ((D_MODEL // 2, D_MODEL), jnp.uint32, 2)),
        name="pack_w_in",
    )(w_in_t, w_in_t)


def _pack_matrices_kernel(a_ref, o_ref):
    o_ref[0] = _pack_rows(a_ref[0])


def _pack_matrices(w):
    return pl.pallas_call(
        _pack_matrices_kernel,
        grid=(w.shape[0],),
        in_specs=[pl.BlockSpec((1, D_MODEL, D_MODEL), lambda r: (r, 0, 0))],
        out_specs=pl.BlockSpec((1, D_MODEL // 2, D_MODEL), lambda r: (r, 0, 0)),
        out_shape=jax.ShapeDtypeStruct((w.shape[0], D_MODEL // 2, D_MODEL), jnp.uint32),
        compiler_params=_params(("parallel",), ((D_MODEL, D_MODEL), F32, 2), ((D_MODEL // 2, D_MODEL), jnp.uint32, 2)),
        name="pack_matrices",
    )(w)


def _silu(x):
    return x * jax.nn.sigmoid(x)


def _rms(x, gain):
    ms = jnp.mean(x * x, axis=-1, keepdims=True)
    return x * lax.rsqrt(ms + EPS) * gain


def _const_spec(shape):
    zeros = (0,) * len(shape)
    return pl.BlockSpec(shape, lambda *_: zeros, pipeline_mode=pl.Buffered(1))


def _layer_spec(shape, layer):
    index = (layer,) + (0,) * len(shape)
    return pl.BlockSpec((1,) + tuple(shape), lambda *_: index, pipeline_mode=pl.Buffered(1))


def _drop_alias_refs(body, n_in, n_alias):
    def kern(*refs):
        return body(*refs[:n_in], *refs[n_in + n_alias:])
    return kern


def _kvproj_kernel(m_ref, wk_ref, wv_ref, k_ref, v_ref, kb_ref, vb_ref):
    m = m_ref[...].astype(BF)
    k = _dot(m, wk_ref[0])
    v = _dot(m, wv_ref[0])
    for h in range(XA_HEADS):
        cs = slice(h * XA_HEAD_DIM, (h + 1) * XA_HEAD_DIM)
        k_ref[0, 0, :, h, :] = k[:, cs]
        v_ref[0, 0, :, h, :] = v[:, cs]
    kb_ref[0] = k.astype(BF)
    vb_ref[0] = v.astype(BF)


def _kvproj(mem, wk, wv, batch, n_mem):
    depth = wk.shape[0]
    w_spec = pl.BlockSpec((1, D_MODEL, D_MODEL), lambda l, b: (l, 0, 0))
    out5 = pl.BlockSpec((1, 1, n_mem, XA_HEADS, XA_HEAD_DIM), lambda l, b: (l, b, 0, 0, 0))
    out_bf = pl.BlockSpec((1, n_mem, D_MODEL), lambda l, b: (l, b, 0))
    return pl.pallas_call(
        _kvproj_kernel,
        grid=(depth, batch),
        in_specs=[pl.BlockSpec((n_mem, D_MODEL), lambda l, b: (b, 0)), w_spec, w_spec],
        out_specs=[out5, out5, out_bf, out_bf],
        out_shape=[jax.ShapeDtypeStruct((depth, batch, n_mem, XA_HEADS, XA_HEAD_DIM), F32)] * 2
        + [jax.ShapeDtypeStruct((depth, batch * n_mem, D_MODEL), BF)] * 2,
        compiler_params=_params(("parallel", "parallel"),
                                ((n_mem, D_MODEL), F32, 2), ((D_MODEL, D_MODEL), BF, 4),
                                ((n_mem, D_MODEL), F32, 4), ((n_mem, D_MODEL), BF, 4)),
        name="kvproj",
    )(mem, wk, wv)


def _inproj_slab(h, w_ref, p_out, j):
    w = w_ref[0, :, j * D_MODEL:(j + 1) * D_MODEL]
    p_out[j] = _dot(h, _unpack_rows(w)).astype(p_out.dtype)


def _inproj_gate_lowrank(h, walr_ref):
    return _dot_nt(h, walr_ref[...].astype(BF)).astype(BF)


def _inproj_gate(alr, wa2_ref, ba_ref, la_out):
    z = _dot(alr, wa2_ref[...]) + ba_ref[...]
    la_out[...] = (jnp.minimum(z, 0.0) - jnp.log(1.0 + jnp.exp(-jnp.abs(z)))) * (1.0 / GATE_TAU)


def _merge_term(br_ref, p, wbr_ref, n):
    return jax.nn.sigmoid(p[SLAB_MERGE + n].astype(F32)) * _dot(br_ref[n].astype(BF), _unpack_rows(wbr_ref[0, n]))


def _merge_finish(merged, x, wo_ref, fg_ref, final):
    x_new = x + _dot(merged.astype(BF), _unpack_rows(wo_ref[0]))
    return _rms(x_new, fg_ref[...]) if final else x_new


def _chunk_cumsum_matrix(tm):
    row = lax.broadcasted_iota(jnp.int32, (tm, tm), 0)
    col = lax.broadcasted_iota(jnp.int32, (tm, tm), 1)
    return (((row // CHUNK) == (col // CHUNK)) & (row >= col)).astype(BF)


def _window_matrices(tm):
    row = lax.broadcasted_iota(jnp.int32, (tm, tm), 0)
    col = lax.broadcasted_iota(jnp.int32, (tm, tm), 1)
    return jnp.stack([((row >= col) & (row - col < w)).astype(BF) for w in POOL_WINDOWS])


def _gla_cumdecay(cum_ref, la_ref):
    la = la_ref[...]
    la_hi = la.astype(BF)
    la_lo = (la - la_hi.astype(F32)).astype(BF)
    return _dot(cum_ref[...], la_hi) + _dot(cum_ref[...], la_lo)


def _gla_chunk(p, bcum, gg_ref, br_ref, st_ref, c, filler):
    crow = lax.broadcasted_iota(jnp.int32, (CHUNK, CHUNK), 0)
    ccol = lax.broadcasted_iota(jnp.int32, (CHUNK, CHUNK), 1)
    causal = crow >= ccol
    rows = slice(c * CHUNK, (c + 1) * CHUNK)
    heads = range(GLA_HEADS)
    q_dec, k_end, decay, att = [], [], [], []
    for h in heads:
        kc = slice(h * GLA_DK, (h + 1) * GLA_DK)
        kc2 = slice(GLA_KEY_WIDTH + h * GLA_DK, GLA_KEY_WIDTH + (h + 1) * GLA_DK)
        b = bcum[rows, kc]
        b_mid = b[CHUNK // 2 - 1:CHUNK // 2, :]
        b_last = b[CHUNK - 1:CHUNK, :]
        q = p[SLAB_QK, rows, kc].astype(F32) * (GLA_DK ** -0.5)
        k = p[SLAB_QK, rows, kc2].astype(F32)
        q_dec.append((q * jnp.exp(b)).astype(BF))
        q_mid = (q * jnp.exp(b - b_mid)).astype(BF)
        k_mid = (k * jnp.exp(b_mid - b)).astype(BF)
        k_end.append((k * jnp.exp(b_last - b)).astype(BF))
        decay.append(jnp.exp(b_last))
        att.append(_dot_nt(q_mid, k_mid))
    filler()
    o = []
    for h in heads:
        vc = slice(h * GLA_DV, (h + 1) * GLA_DV)
        a = jnp.where(causal, att[h], 0.0).astype(BF)
        o.append(_dot(a, p[SLAB_V, rows, vc]) + _dot_nt(q_dec[h], st_ref[h].astype(BF)))
    for h in heads:
        vc = slice(h * GLA_DV, (h + 1) * GLA_DV)
        st_ref[h] = decay[h] * st_ref[h] + _dot_tn(p[SLAB_V, rows, vc], k_end[h])
    for h in heads:
        vc = slice(h * GLA_DV, (h + 1) * GLA_DV)
        g = p[SLAB_GLA_G, rows, vc].astype(F32)
        br_ref[0, rows, vc] = (_rms(o[h], gg_ref[:, vc]) * _silu(g)).astype(BF)


def _pool_window_sums(p, win_ref):
    sums = []
    for g in range(len(POOL_WINDOWS)):
        cs = slice(g * POOL_GROUP_DIM, (g + 1) * POOL_GROUP_DIM)
        sums.append(_dot(win_ref[g], p[SLAB_U, :, cs]))
    return sums


def _pool_branch(p, sums, pw_ref, ps_ref, br_ref, ubuf_ref, t, tm):
    u = p[SLAB_U].astype(F32)
    ubuf_ref[HIST_PAD:2 * HIST_PAD, :] = u[0:HIST_PAD]
    pos = t * tm + lax.broadcasted_iota(jnp.int32, (tm, 1), 0)
    for g, w in enumerate(POOL_WINDOWS):
        cs = slice(g * POOL_GROUP_DIM, (g + 1) * POOL_GROUP_DIM)
        ug = u[:, cs]
        head = ug[0:HIST_PAD]
        for j in range(1, w):
            head = head + ubuf_ref[HIST_PAD - j:2 * HIST_PAD - j, cs]
        s = jnp.concatenate([head, sums[g][HIST_PAD:]], axis=0)
        cnt = jnp.minimum(w, pos + 1).astype(F32)
        diff = s / cnt - ug
        mixed = _dot(diff.astype(BF), pw_ref[g]) * ps_ref[:, cs]
        pg = p[SLAB_POOL_G, :, cs].astype(F32)
        br_ref[1, :, cs] = (mixed * _silu(pg)).astype(BF)
    ubuf_ref[0:HIST_PAD, :] = u[tm - HIST_PAD:tm]


def _xattn_probs(p, mk_ref):
    out = []
    for h in range(XA_HEADS):
        cs = slice(h * XA_HEAD_DIM, (h + 1) * XA_HEAD_DIM)
        s = _dot_nt(p[SLAB_XQ, :, cs], mk_ref[0, :, cs]) * (XA_HEAD_DIM ** -0.5)
        pr = jnp.exp(s - jnp.max(s, axis=-1, keepdims=True))
        out.append((pr.astype(BF), jnp.sum(pr, axis=-1, keepdims=True)))
    return out


def _xattn_branch(p, probs, mv_ref, br_ref):
    for h in range(XA_HEADS):
        cs = slice(h * XA_HEAD_DIM, (h + 1) * XA_HEAD_DIM)
        pr, denom = probs[h]
        o = _dot(pr, mv_ref[0, :, cs]) / denom
        xg = p[SLAB_XG, :, cs].astype(F32)
        br_ref[2, :, cs] = (o * _silu(xg)).astype(BF)


def _prompt_layer_kernel(xn_ref, xc_ref, g_ref, w_ref, walr_ref, wa2_ref, ba_ref, cum_ref, win_ref,
                         mk_ref, mv_ref, pw_ref, ps_ref, gg_ref, wbr_ref, wo_ref, fg_ref,
                         sxq_ref, sxg_ref, sk_ref, sv_ref,
                         out_ref, s_out_ref, hist_out_ref, sx_out_ref,
                         h_scr, p_scr, la_scr, br_scr, st_ref, ubuf_ref, *, tm, nt, final, n_fused):
    s = pl.program_id(0)
    t = jnp.maximum(s - 1, 0) % nt
    slot_w = s % 2
    slot_r = 1 - slot_w

    @pl.when(s == 0)
    def _():
        p_scr[1] = jnp.zeros(p_scr.shape[1:], p_scr.dtype)
        la_scr[1] = jnp.zeros(la_scr.shape[1:], la_scr.dtype)

    @pl.when(t == 0)
    def _():
        st_ref[...] = jnp.zeros_like(st_ref)
        ubuf_ref[0:HIST_PAD, :] = jnp.zeros((HIST_PAD, D_MODEL), F32)

    @pl.when(s == 0)
    def _():
        h_scr[0] = _rms(xc_ref[...], g_ref[...]).astype(BF)

    p_next = p_scr.at[slot_w]
    p = p_scr.at[slot_r]

    slabs = list(range(N_SLABS))

    def filler(n=1):
        for _ in range(n):
            _inproj_slab(h_scr[slot_w], w_ref, p_next, slabs.pop(0))

    alr = _inproj_gate_lowrank(h_scr[slot_w], walr_ref)
    bcum = _gla_cumdecay(cum_ref, la_scr.at[slot_r])
    filler()
    _inproj_gate(alr, wa2_ref, ba_ref, la_scr.at[slot_w])
    per_site = (N_SLABS - 2) // (2 * (tm // CHUNK))
    probs, sums = None, None
    for c in range(tm // CHUNK):
        _gla_chunk(p, bcum, gg_ref, br_scr, st_ref, c, functools.partial(filler, per_site))
        filler(per_site)
        if c == 0:
            probs = _xattn_probs(p, mk_ref)
            sums = _pool_window_sums(p, win_ref)
            srow = pl.ds(jnp.clip(s - 1, 0, n_fused - 1) % SUBLANES, 1)
            sx_out_ref[srow, :] = _sample_xattn_row(sxq_ref[0, srow, :], sxg_ref[0, srow, :],
                                                    sk_ref[0, 0], sv_ref[0, 0])
        elif c == 1:
            _xattn_branch(p, probs, mv_ref, br_scr)
            _pool_branch(p, sums, pw_ref, ps_ref, br_scr, ubuf_ref, t, tm)
    h_scr[slot_r] = _rms(xn_ref[...], g_ref[...]).astype(BF)
    merged = _merge_term(br_scr, p, wbr_ref, 2)
    merged = merged + _merge_term(br_scr, p, wbr_ref, 1)
    merged = merged + _merge_term(br_scr, p, wbr_ref, 0)
    filler()
    assert not slabs
    out_ref[...] = _merge_finish(merged, xc_ref[...], wo_ref, fg_ref, final)

    @pl.when((s > 0) & (t == nt - 1))
    def _():
        for h in range(GLA_HEADS):
            s_out_ref[0, 0, h] = st_ref[h].T
        hist_out_ref[0, 0] = ubuf_ref[1:HIST_PAD, :]


def _prompt_layer(x, ngain, w_proj, w_alr, wa2, ba, mk, mv, pool_w, pool_scale, gla_gain,
                  w_branch, w_out, fgain, p_sample, ck, cv, n_fused, carry, layer, depth, batch, seq, n_mem, tm, final):
    nt = seq // tm
    n_tiles = batch * nt
    assert tm // CHUNK == 2 and (N_SLABS - 2) % (2 * (tm // CHUNK)) == 0, "slab placement assumes two GLA chunks"
    assert 0 < n_fused <= n_tiles and n_fused % SUBLANES == 0
    n_alias = 0 if carry is None else len(carry)
    n_in = 21
    kern = _drop_alias_refs(functools.partial(_prompt_layer_kernel, tm=tm, nt=nt, final=final, n_fused=n_fused),
                            n_in, n_alias)
    any_spec = pl.BlockSpec(memory_space=pl.ANY)
    scratch = [
        ((2, tm, D_MODEL), BF, 1),
        ((2, N_SLABS, tm, D_MODEL), BF, 1),
        ((2, tm, GLA_KEY_WIDTH), F32, 1),
        ((N_BRANCH, tm, D_MODEL), BF, 1),
        ((GLA_HEADS, GLA_DV, GLA_DK), F32, 1),
        ((2 * HIST_PAD, D_MODEL), F32, 1),
    ]

    def cur(s):
        return jnp.maximum(s - 1, 0)

    def seq_of(s):
        return cur(s) // nt

    def sample_of(s):
        return jnp.clip(s - 1, 0, n_fused - 1)

    sample_rows = lambda slab: pl.BlockSpec((1, SUBLANES, D_MODEL), lambda s: (slab, sample_of(s) // SUBLANES, 0))
    sample_cache = pl.BlockSpec((1, 1, n_mem, SUBLANES, LANES), lambda s: (layer, sample_of(s), 0, 0, 0))

    return pl.pallas_call(
        kern,
        grid=(n_tiles + 1,),
        in_specs=[
            pl.BlockSpec((tm, D_MODEL), lambda s: (jnp.minimum(s + 1, n_tiles - 1), 0)),
            pl.BlockSpec((tm, D_MODEL), lambda s: (cur(s), 0)),
            _const_spec((1, D_MODEL)),
            _layer_spec((D_MODEL // 2, N_SLABS * D_MODEL), layer),
            _const_spec((GATE_RANK, D_MODEL)),
            _const_spec((GATE_RANK, GLA_KEY_WIDTH)),
            _const_spec((1, GLA_KEY_WIDTH)),
            _const_spec((tm, tm)),
            _const_spec((len(POOL_WINDOWS), tm, tm)),
            pl.BlockSpec((1, n_mem, D_MODEL), lambda s: (layer, seq_of(s), 0)),
            pl.BlockSpec((1, n_mem, D_MODEL), lambda s: (layer, seq_of(s), 0)),
            _const_spec((len(POOL_WINDOWS), POOL_GROUP_DIM, POOL_GROUP_DIM)),
            _const_spec((1, D_MODEL)),
            _const_spec((1, D_MODEL)),
            _layer_spec((N_BRANCH, D_MODEL // 2, D_MODEL), layer),
            _layer_spec((D_MODEL // 2, D_MODEL), layer),
            _const_spec((1, D_MODEL)),
            sample_rows(SLAB_XQ), sample_rows(SLAB_XG), sample_cache, sample_cache,
        ] + [any_spec] * n_alias,
        out_specs=[
            pl.BlockSpec((tm, D_MODEL), lambda s: (cur(s), 0)),
            pl.BlockSpec((1, 1, GLA_HEADS, GLA_DK, GLA_DV), lambda s: (layer, seq_of(s), 0, 0, 0)),
            pl.BlockSpec((1, 1, POOL_HIST, D_MODEL), lambda s: (layer, seq_of(s), 0, 0)),
            pl.BlockSpec((SUBLANES, D_MODEL), lambda s: (sample_of(s) // SUBLANES, 0)),
        ],
        out_shape=[
            jax.ShapeDtypeStruct((n_tiles * tm, D_MODEL), F32),
            jax.ShapeDtypeStruct((depth, batch, GLA_HEADS, GLA_DK, GLA_DV), F32),
            jax.ShapeDtypeStruct((depth, batch, POOL_HIST, D_MODEL), F32),
            jax.ShapeDtypeStruct((n_fused, D_MODEL), F32),
        ],
        scratch_shapes=[pltpu.VMEM(shape, dtype) for shape, dtype, _ in scratch],
        input_output_aliases={n_in + a: 1 + a for a in range(n_alias)},
        compiler_params=_params(
            ("arbitrary",), *scratch,
            ((tm, D_MODEL), F32, 6),
            ((D_MODEL // 2, (N_SLABS + N_BRANCH + 1) * D_MODEL), jnp.uint32, 1),
            ((n_mem, D_MODEL), BF, 4),
            ((tm, tm), BF, 1 + len(POOL_WINDOWS)),
            ((len(POOL_WINDOWS), POOL_GROUP_DIM, POOL_GROUP_DIM), BF, 1),
            ((GLA_HEADS, GLA_DK, GLA_DV), F32, 2),
            ((n_mem, SUBLANES, LANES), F32, 4), ((SUBLANES, D_MODEL), F32, 6)),
        name="prompt_layer",
    )(x, x, ngain, w_proj, w_alr, wa2, ba, _chunk_cumsum_matrix(tm), _window_matrices(tm),
      mk, mv, pool_w, pool_scale, gla_gain,
      w_branch, w_out, fgain, p_sample, p_sample, ck, cv, *(carry or ()))


def _inproj_kernel(x_ref, g_ref, w_ref, walr_ref, wa2_ref, ba_ref, p_ref, la_ref):
    h = _rms(x_ref[...], g_ref[...]).astype(BF)
    alr = _inproj_gate_lowrank(h, walr_ref)
    for j in range(N_SLABS):
        _inproj_slab(h, w_ref, p_ref, j)
    _inproj_gate(alr, wa2_ref, ba_ref, la_ref)


def _inproj(x, gain, w_proj, w_alr, w_a2, b_a, layer, tm, out_dtype):
    m_rows = x.shape[0]
    return pl.pallas_call(
        _inproj_kernel,
        grid=(m_rows // tm,),
        in_specs=[
            pl.BlockSpec((tm, D_MODEL), lambda i: (i, 0)),
            _const_spec((1, D_MODEL)),
            _layer_spec((D_MODEL // 2, N_SLABS * D_MODEL), layer),
            _const_spec((GATE_RANK, D_MODEL)),
            _const_spec((GATE_RANK, GLA_KEY_WIDTH)),
            _const_spec((1, GLA_KEY_WIDTH)),
        ],
        out_specs=[
            pl.BlockSpec((N_SLABS, tm, D_MODEL), lambda i: (0, i, 0)),
            pl.BlockSpec((tm, GLA_KEY_WIDTH), lambda i: (i, 0)),
        ],
        out_shape=[
            jax.ShapeDtypeStruct((N_SLABS, m_rows, D_MODEL), out_dtype),
            jax.ShapeDtypeStruct((m_rows, GLA_KEY_WIDTH), F32),
        ],
        compiler_params=_params(("parallel",),
                                ((tm, D_MODEL), F32, 2), ((D_MODEL // 2, N_SLABS * D_MODEL), jnp.uint32, 1),
                                ((N_SLABS, tm, D_MODEL), out_dtype, 2), ((tm, GLA_KEY_WIDTH), F32, 2)),
        name="inproj",
    )(x, gain, w_proj, w_alr, w_a2, b_a)


def _sample_xattn_row(xq, xg, keys, values):
    n_mem = keys.shape[0]
    halves = XA_HEAD_DIM // LANES
    half_cols = [slice(h * XA_HEAD_DIM + j * LANES, h * XA_HEAD_DIM + (j + 1) * LANES)
                 for j in range(halves) for h in range(XA_HEADS)]
    xq_rows = jnp.concatenate([xq[:, cs] for cs in half_cols], axis=0)
    prod = (keys * xq_rows[None]).reshape(n_mem * SUBLANES, LANES).astype(BF)
    part = _dot(prod, jnp.ones((LANES, LANES), BF)).reshape(n_mem, SUBLANES, LANES)
    s = (part + pltpu.roll(part, XA_HEADS, axis=1)) * (XA_HEAD_DIM ** -0.5)
    p = jnp.exp(s - jnp.max(s, axis=0, keepdims=True))
    o = jnp.sum(p * values, axis=0) / jnp.sum(p, axis=0)
    o_row = jnp.concatenate([o[j * XA_HEADS + h:j * XA_HEADS + h + 1, :]
                             for h in range(XA_HEADS) for j in range(halves)], axis=1)
    return o_row * _silu(xg)


def _sample_mix_kernel(p_ref, la_ref, s0_ref, hist_ref, ck_ref, cv_ref, sx_ref, pw_ref, ps_ref, gg_ref,
                       br_ref, s_out_ref, hist_out_ref, diff_ref, *, sb, fused_blocks):
    r0 = pl.program_id(1) * sb
    erow = lax.broadcasted_iota(jnp.int32, (GLA_DK, GLA_DK), 0)
    ecol = lax.broadcasted_iota(jnp.int32, (GLA_DK, GLA_DK), 1)
    eye = erow == ecol

    def to_col(x):
        return jnp.sum(jnp.where(eye, jnp.broadcast_to(x, (GLA_DK, GLA_DK)), 0.0), axis=1, keepdims=True)

    for i in range(sb):
        r = pl.ds(r0 + i, 1)
        la = la_ref[r, :]
        qk = p_ref[SLAB_QK, r, :]
        vv = p_ref[SLAB_V, r, :]
        gla_g = p_ref[SLAB_GLA_G, r, :]
        for h in range(GLA_HEADS):
            kc = slice(h * GLA_DK, (h + 1) * GLA_DK)
            kc2 = slice(GLA_KEY_WIDTH + h * GLA_DK, GLA_KEY_WIDTH + (h + 1) * GLA_DK)
            vc = slice(h * GLA_DV, (h + 1) * GLA_DV)
            a_col = to_col(jnp.exp(la[:, kc]))
            q_col = to_col(qk[:, kc] * (GLA_DK ** -0.5))
            k_col = to_col(qk[:, kc2])
            s_new = a_col * s0_ref[0, i, h] + k_col * vv[:, vc]
            s_out_ref[0, i, h] = s_new
            o = jnp.sum(q_col * s_new, axis=0, keepdims=True)
            br_ref[0, r, vc] = _rms(o, gg_ref[:, vc]) * _silu(gla_g[:, vc])

        u = p_ref[SLAB_U, r, :]
        for g, w in enumerate(POOL_WINDOWS):
            cs = slice(g * POOL_GROUP_DIM, (g + 1) * POOL_GROUP_DIM)
            past = jnp.sum(hist_ref[0, POOL_HIST - (w - 1):POOL_HIST, r, cs], axis=0)
            diff_ref[r, cs] = (u[:, cs] + past) / float(w) - u[:, cs]
        hist_out_ref[0, 0:POOL_HIST - 1, r, :] = hist_ref[0, 1:POOL_HIST, r, :]
        hist_out_ref[0, POOL_HIST - 1, r, :] = u

    block = pl.program_id(0) * pl.num_programs(1) + pl.program_id(1)

    @pl.when(block < fused_blocks)
    def _():
        for i in range(sb):
            r = pl.ds(r0 + i, 1)
            br_ref[2, r, :] = sx_ref[r, :]

    @pl.when(block >= fused_blocks)
    def _():
        for i in range(sb):
            r = pl.ds(r0 + i, 1)
            br_ref[2, r, :] = _sample_xattn_row(p_ref[SLAB_XQ, r, :], p_ref[SLAB_XG, r, :], ck_ref[0, i], cv_ref[0, i])

    @pl.when(pl.program_id(1) == pl.num_programs(1) - 1)
    def _():
        for g in range(len(POOL_WINDOWS)):
            cs = slice(g * POOL_GROUP_DIM, (g + 1) * POOL_GROUP_DIM)
            mixed = _dot(diff_ref[:, cs].astype(BF), pw_ref[g]) * ps_ref[:, cs]
            br_ref[1, :, cs] = mixed * _silu(p_ref[SLAB_POOL_G, :, cs])


def _cache_rows_view(c):
    depth, nb, n_mem = c.shape[:3]
    halves = XA_HEAD_DIM // LANES
    c = c.reshape(depth, nb, n_mem, XA_HEADS, halves, LANES)
    return c.transpose(0, 1, 2, 4, 3, 5).reshape(depth, nb, n_mem, halves * XA_HEADS, LANES)


def _sample_mix(p, la, s0, hist, ck, cv, sx, pool_w, pool_scale, gla_gain, carry, layer, sb=SAMPLE_BLOCK):
    nb = s0.shape[1]
    n_mem = ck.shape[2]
    rb = SUBLANES
    halves = rb // sb
    fused_blocks = sx.shape[0] // sb
    first_cache_block = min(fused_blocks, nb // sb - 1)
    cache_spec = pl.BlockSpec((1, sb, n_mem, SUBLANES, LANES),
                              lambda i, j: (layer, jnp.maximum(i * halves + j, first_cache_block), 0, 0, 0))
    n_alias = 0 if carry is None else len(carry)
    n_in = 10
    kern = _drop_alias_refs(functools.partial(_sample_mix_kernel, sb=sb, fused_blocks=fused_blocks), n_in, n_alias)
    any_spec = pl.BlockSpec(memory_space=pl.ANY)
    return pl.pallas_call(
        kern,
        grid=(nb // rb, halves),
        in_specs=[
            pl.BlockSpec((N_MIX_SLABS, rb, D_MODEL), lambda i, j: (0, i, 0)),
            pl.BlockSpec((rb, GLA_KEY_WIDTH), lambda i, j: (i, 0)),
            pl.BlockSpec((1, sb, GLA_HEADS, GLA_DK, GLA_DV), lambda i, j: (layer, i * halves + j, 0, 0, 0)),
            pl.BlockSpec((1, POOL_HIST, rb, D_MODEL), lambda i, j: (layer, 0, i, 0)),
            cache_spec, cache_spec,
            pl.BlockSpec((rb, D_MODEL), lambda i, j: (jnp.minimum(i, sx.shape[0] // rb - 1), 0)),
            _const_spec((len(POOL_WINDOWS), POOL_GROUP_DIM, POOL_GROUP_DIM)),
            _const_spec((1, D_MODEL)),
            _const_spec((1, D_MODEL)),
        ] + [any_spec] * n_alias,
        out_specs=[
            pl.BlockSpec((N_BRANCH, rb, D_MODEL), lambda i, j: (0, i, 0)),
            pl.BlockSpec((1, sb, GLA_HEADS, GLA_DK, GLA_DV), lambda i, j: (layer, i * halves + j, 0, 0, 0)),
            pl.BlockSpec((1, POOL_HIST, rb, D_MODEL), lambda i, j: (layer, 0, i, 0)),
        ],
        out_shape=[
            jax.ShapeDtypeStruct((N_BRANCH, nb, D_MODEL), F32),
            jax.ShapeDtypeStruct(s0.shape, F32),
            jax.ShapeDtypeStruct(hist.shape, F32),
        ],
        scratch_shapes=[pltpu.VMEM((rb, D_MODEL), F32)],
        input_output_aliases={n_in + a: 1 + a for a in range(n_alias)},
        compiler_params=_params(
            ("parallel", "arbitrary"),
            ((N_MIX_SLABS + N_BRANCH + 2, rb, D_MODEL), F32, 2),
            ((sb, GLA_HEADS, GLA_DK, GLA_DV), F32, 4),
            ((POOL_HIST, rb, D_MODEL), F32, 4),
            ((sb, n_mem, SUBLANES, LANES), F32, 4),
            ((len(POOL_WINDOWS), POOL_GROUP_DIM, POOL_GROUP_DIM), BF, 1)),
        name="sample_mix",
    )(p, la, s0, hist, ck, cv, sx, pool_w, pool_scale, gla_gain, *(carry or ()))


def _merge_kernel(br_ref, p_ref, x_ref, wbr_ref, wo_ref, fg_ref, out_ref, *, final):
    merged = _merge_term(br_ref, p_ref, wbr_ref, 0)
    for n in range(1, N_BRANCH):
        merged = merged + _merge_term(br_ref, p_ref, wbr_ref, n)
    out_ref[...] = _merge_finish(merged, x_ref[...], wo_ref, fg_ref, final)


def _merge_out(br, p, x, w_branch, w_out, final_gain, layer, final):
    m_rows = x.shape[0]
    whole = lambda shape: pl.BlockSpec(shape, lambda i: (0,) * len(shape))
    return pl.pallas_call(
        functools.partial(_merge_kernel, final=final),
        grid=(1,),
        in_specs=[
            whole((N_BRANCH, m_rows, D_MODEL)),
            whole((N_SLABS, m_rows, D_MODEL)),
            whole((m_rows, D_MODEL)),
            _layer_spec((N_BRANCH, D_MODEL // 2, D_MODEL), layer),
            _layer_spec((D_MODEL // 2, D_MODEL), layer),
            _const_spec((1, D_MODEL)),
        ],
        out_specs=whole((m_rows, D_MODEL)),
        out_shape=jax.ShapeDtypeStruct((m_rows, D_MODEL), F32),
        compiler_params=_params(("arbitrary",),
                                ((N_BRANCH + N_SLABS + 2, m_rows, D_MODEL), F32, 2),
                                ((D_MODEL // 2, (N_BRANCH + 1) * D_MODEL), jnp.uint32, 1)),
        name="merge_out",
    )(br, p, x, w_branch, w_out, final_gain)


def kernel(x_prompt, x_sample, mem_prompt, cache_mem_k, cache_mem_v, state_gla, state_pool, w_in, w_a2, b_a, gla_gain, pool_w, pool_scale, w_mk, w_mv, w_branch, w_out, norm_gain, final_gain):
    batch, seq, _ = x_prompt.shape
    nb = x_sample.shape[0]
    n_mem = mem_prompt.shape[1]
    depth = w_in.shape[0]
    tm = PROMPT_TILE
    assert seq % tm == 0 and nb % SUBLANES == 0 and x_prompt.shape[2] == D_MODEL

    xp = x_prompt.reshape(batch * seq, D_MODEL)
    xs = x_sample.reshape(nb, D_MODEL)
    mem = mem_prompt.reshape(batch * n_mem, D_MODEL)
    fgain = final_gain.reshape(1, D_MODEL)

    mk, mv, mk_bf, mv_bf = _kvproj(mem, w_mk.astype(BF), w_mv.astype(BF), batch, n_mem)

    w_in_t = jnp.swapaxes(w_in, 1, 2)
    pool_rows = jnp.swapaxes(state_pool, 1, 2)
    cache_k, cache_v = _cache_rows_view(cache_mem_k), _cache_rows_view(cache_mem_v)
    n_fused = min(batch * (seq // tm), nb) // SUBLANES * SUBLANES
    w_proj = _pack_w_in(w_in_t)
    wb = _pack_matrices(w_branch.reshape(depth * N_BRANCH, D_MODEL, D_MODEL))
    wb = wb.reshape(depth, N_BRANCH, D_MODEL // 2, D_MODEL)
    wo = _pack_matrices(w_out)

    carry_p, carry_s = None, None
    for l in range(depth):
        final = l == depth - 1
        w_alr = w_in_t[l, ALR_START:ALR_START + GATE_RANK, :]
        wa2 = w_a2[l].astype(BF)
        ba = b_a[l].reshape(1, GLA_KEY_WIDTH)
        ngain = norm_gain[l].reshape(1, D_MODEL)
        ggain = gla_gain[l].reshape(1, D_MODEL)
        pscale = pool_scale[l].reshape(1, D_MODEL)
        pw = pool_w[l].astype(BF)

        ps, las = _inproj(xs, ngain, w_proj, w_alr, wa2, ba, l, nb, F32)

        xp, s_all, hist_all, sx = _prompt_layer(xp, ngain, w_proj, w_alr, wa2, ba, mk_bf, mv_bf, pw, pscale,
                                                ggain, wb, wo, fgain, ps, cache_k, cache_v, n_fused,
                                                carry_p, l, depth, batch, seq, n_mem, tm, final)
        carry_p = (s_all, hist_all)

        brs, s_new, hist_new = _sample_mix(ps, las, state_gla, pool_rows, cache_k, cache_v, sx,
                                           pw, pscale, ggain, carry_s, l)
        carry_s = (s_new, hist_new)
        xs = _merge_out(brs, ps, xs, wb, wo, fgain, l, final)

    return (xp.reshape(batch, seq, D_MODEL), xs.reshape(nb, 1, D_MODEL),
            mk, mv, carry_p[0], carry_p[1], carry_s[0], jnp.swapaxes(carry_s[1], 1, 2))
```

```python
import functools
import math

import jax
import jax.numpy as jnp
from jax import lax
from jax.experimental import pallas as pl
from jax.experimental.pallas import tpu as pltpu

D_MODEL = 1024
GLA_HEADS = 4
GLA_DK = 128
GLA_DV = 256
GLA_KEY_WIDTH = GLA_HEADS * GLA_DK
GATE_RANK = 16
GATE_TAU = 16.0
CHUNK = 128
POOL_WINDOWS = (2, 4, 8, 16)
POOL_GROUP_DIM = 256
POOL_HIST = 15
HIST_PAD = 16
XA_HEADS = 4
XA_HEAD_DIM = 256
N_BRANCH = 3
EPS = 1e-6
SUBLANES = 8
LANES = 128

SLAB_QK, SLAB_V, SLAB_GLA_G, SLAB_U, SLAB_POOL_G, SLAB_XQ, SLAB_XG, SLAB_MERGE = 0, 1, 2, 3, 4, 5, 6, 7
N_SLABS = 10
N_MIX_SLABS = 7
N_HEAD_SLABS = 3
ALR_START = N_HEAD_SLABS * D_MODEL

BF = jnp.bfloat16
F32 = jnp.float32
MIB = 1 << 20

PROMPT_TILE = 256
SAMPLE_BLOCK = 4
CACHE_RING = 3
VMEM_COMPILER_SCRATCH = 8 * MIB


def _dot(a, b):
    return jnp.dot(a, b, preferred_element_type=F32)


def _dot_nt(a, b):
    return lax.dot_general(a, b, (((1,), (1,)), ((), ())), preferred_element_type=F32)


def _dot_tn(a, b):
    return lax.dot_general(a, b, (((0,), (0,)), ((), ())), preferred_element_type=F32)


def _pack_rows(w):
    return pltpu.bitcast(w.astype(BF), jnp.uint32)


def _unpack_rows(w_words):
    return pltpu.bitcast(w_words, BF)


def _params(sem, *buffers):
    need = sum(math.prod(shape) * jnp.dtype(dtype).itemsize * copies for shape, dtype, copies in buffers)
    return pltpu.CompilerParams(dimension_semantics=sem, vmem_limit_bytes=need + VMEM_COMPILER_SCRATCH)


def _pack_w_in_kernel(a_ref, b_ref, o_ref):
    j = pl.program_id(1)

    @pl.when(j < N_HEAD_SLABS)
    def _():
        o_ref[0] = _pack_rows(a_ref[0].T)

    @pl.when(j >= N_HEAD_SLABS)
    def _():
        o_ref[0] = _pack_rows(jnp.concatenate([a_ref[0, GATE_RANK:, :], b_ref[0]], axis=0).T)


def _pack_w_in(w_in_t):
    depth = w_in_t.shape[0]
    return pl.pallas_call(
        _pack_w_in_kernel,
        grid=(depth, N_SLABS),
        in_specs=[
            pl.BlockSpec((1, D_MODEL, D_MODEL), lambda l, j: (l, j, 0)),
            pl.BlockSpec((1, GATE_RANK, D_MODEL), lambda l, j: (l, (j + 1) * (D_MODEL // GATE_RANK), 0)),
        ],
        out_specs=pl.BlockSpec((1, D_MODEL // 2, D_MODEL), lambda l, j: (l, 0, j)),
        out_shape=jax.ShapeDtypeStruct((depth, D_MODEL // 2, N_SLABS * D_MODEL), jnp.uint32),
        compiler_params=_params(("parallel", "parallel"),
                                ((D_MODEL, D_MODEL), F32, 2), ((GATE_RANK, D_MODEL), F32, 2),
                                ((D_MODEL // 2, D_MODEL), jnp.uint32, 2)),
        name="pack_w_in",
    )(w_in_t, w_in_t)


def _pack_matrices_kernel(a_ref, o_ref):
    o_ref[0] = _pack_rows(a_ref[0])


def _pack_matrices(w):
    return pl.pallas_call(
        _pack_matrices_kernel,
        grid=(w.shape[0],),
        in_specs=[pl.BlockSpec((1, D_MODEL, D_MODEL), lambda r: (r, 0, 0))],
        out_specs=pl.BlockSpec((1, D_MODEL // 2, D_MODEL), lambda r: (r, 0, 0)),
        out_shape=jax.ShapeDtypeStruct((w.shape[0], D_MODEL // 2, D_MODEL), jnp.uint32),
        compiler_params=_params(("parallel",), ((D_MODEL, D_MODEL), F32, 2), ((D_MODEL // 2, D_MODEL), jnp.uint32, 2)),
        name="pack_matrices",
    )(w)


def _silu(x):
    return x * jax.nn.sigmoid(x)


def _rms(x, gain):
    ms = jnp.mean(x * x, axis=-1, keepdims=True)
    return x * lax.rsqrt(ms + EPS) * gain


def _const_spec(shape):
    zeros = (0,) * len(shape)
    return pl.BlockSpec(shape, lambda *_: zeros, pipeline_mode=pl.Buffered(1))


def _layer_spec(shape, layer):
    index = (layer,) + (0,) * len(shape)
    return pl.BlockSpec((1,) + tuple(shape), lambda *_: index, pipeline_mode=pl.Buffered(1))


def _drop_alias_refs(body, n_in, n_alias):
    def kern(*refs):
        return body(*refs[:n_in], *refs[n_in + n_alias:])
    return kern


def _kvproj_kernel(m_ref, wk_ref, wv_ref, k_ref, v_ref, kb_ref, vb_ref):
    m = m_ref[...].astype(BF)
    k = _dot(m, wk_ref[0])
    v = _dot(m, wv_ref[0])
    for h in range(XA_HEADS):
        cs = slice(h * XA_HEAD_DIM, (h + 1) * XA_HEAD_DIM)
        k_ref[0, 0, :, h, :] = k[:, cs]
        v_ref[0, 0, :, h, :] = v[:, cs]
    kb_ref[0] = k.astype(BF)
    vb_ref[0] = v.astype(BF)


def _kvproj(mem, wk, wv, batch, n_mem):
    depth = wk.shape[0]
    w_spec = pl.BlockSpec((1, D_MODEL, D_MODEL), lambda l, b: (l, 0, 0))
    out5 = pl.BlockSpec((1, 1, n_mem, XA_HEADS, XA_HEAD_DIM), lambda l, b: (l, b, 0, 0, 0))
    out_bf = pl.BlockSpec((1, n_mem, D_MODEL), lambda l, b: (l, b, 0))
    return pl.pallas_call(
        _kvproj_kernel,
        grid=(depth, batch),
        in_specs=[pl.BlockSpec((n_mem, D_MODEL), lambda l, b: (b, 0)), w_spec, w_spec],
        out_specs=[out5, out5, out_bf, out_bf],
        out_shape=[jax.ShapeDtypeStruct((depth, batch, n_mem, XA_HEADS, XA_HEAD_DIM), F32)] * 2
        + [jax.ShapeDtypeStruct((depth, batch * n_mem, D_MODEL), BF)] * 2,
        compiler_params=_params(("parallel", "parallel"),
                                ((n_mem, D_MODEL), F32, 2), ((D_MODEL, D_MODEL), BF, 4),
                                ((n_mem, D_MODEL), F32, 4), ((n_mem, D_MODEL), BF, 4)),
        name="kvproj",
    )(mem, wk, wv)


def _inproj_slab(h, w_ref, p_out, j):
    w = w_ref[0, :, j * D_MODEL:(j + 1) * D_MODEL]
    p_out[j] = _dot(h, _unpack_rows(w)).astype(p_out.dtype)


def _inproj_gate_lowrank(h, walr_ref):
    return _dot_nt(h, walr_ref[...].astype(BF)).astype(BF)


def _inproj_gate(alr, wa2_ref, ba_ref, la_out):
    z = _dot(alr, wa2_ref[...]) + ba_ref[...]
    la_out[...] = (jnp.minimum(z, 0.0) - jnp.log(1.0 + jnp.exp(-jnp.abs(z)))) * (1.0 / GATE_TAU)


def _merge_term(br_ref, p, wbr_ref, n):
    return jax.nn.sigmoid(p[SLAB_MERGE + n].astype(F32)) * _dot(br_ref[n].astype(BF), _unpack_rows(wbr_ref[0, n]))


def _merge_finish(merged, x, wo_ref, fg_ref, final):
    x_new = x + _dot(merged.astype(BF), _unpack_rows(wo_ref[0]))
    return _rms(x_new, fg_ref[...]) if final else x_new


def _chunk_cumsum_matrix(tm):
    row = lax.broadcasted_iota(jnp.int32, (tm, tm), 0)
    col = lax.broadcasted_iota(jnp.int32, (tm, tm), 1)
    return (((row // CHUNK) == (col // CHUNK)) & (row >= col)).astype(BF)


def _window_matrices(tm):
    row = lax.broadcasted_iota(jnp.int32, (tm, tm), 0)
    col = lax.broadcasted_iota(jnp.int32, (tm, tm), 1)
    return jnp.stack([((row >= col) & (row - col < w)).astype(BF) for w in POOL_WINDOWS])


def _gla_cumdecay(cum_ref, la_ref):
    la = la_ref[...]
    la_hi = la.astype(BF)
    la_lo = (la - la_hi.astype(F32)).astype(BF)
    return _dot(cum_ref[...], la_hi) + _dot(cum_ref[...], la_lo)


def _gla_chunk(p, bcum, gg_ref, br_ref, st_ref, c, filler):
    crow = lax.broadcasted_iota(jnp.int32, (CHUNK, CHUNK), 0)
    ccol = lax.broadcasted_iota(jnp.int32, (CHUNK, CHUNK), 1)
    causal = crow >= ccol
    rows = slice(c * CHUNK, (c + 1) * CHUNK)
    heads = range(GLA_HEADS)
    q_dec, k_end, decay, att = [], [], [], []
    for h in heads:
        kc = slice(h * GLA_DK, (h + 1) * GLA_DK)
        kc2 = slice(GLA_KEY_WIDTH + h * GLA_DK, GLA_KEY_WIDTH + (h + 1) * GLA_DK)
        b = bcum[rows, kc]
        b_mid = b[CHUNK // 2 - 1:CHUNK // 2, :]
        b_last = b[CHUNK - 1:CHUNK, :]
        q = p[SLAB_QK, rows, kc].astype(F32) * (GLA_DK ** -0.5)
        k = p[SLAB_QK, rows, kc2].astype(F32)
        q_dec.append((q * jnp.exp(b)).astype(BF))
        q_mid = (q * jnp.exp(b - b_mid)).astype(BF)
        k_mid = (k * jnp.exp(b_mid - b)).astype(BF)
        k_end.append((k * jnp.exp(b_last - b)).astype(BF))
        decay.append(jnp.exp(b_last))
        att.append(_dot_nt(q_mid, k_mid))
    filler()
    o = []
    for h in heads:
        vc = slice(h * GLA_DV, (h + 1) * GLA_DV)
        a = jnp.where(causal, att[h], 0.0).astype(BF)
        o.append(_dot(a, p[SLAB_V, rows, vc]) + _dot_nt(q_dec[h], st_ref[h].astype(BF)))
    for h in heads:
        vc = slice(h * GLA_DV, (h + 1) * GLA_DV)
        st_ref[h] = decay[h] * st_ref[h] + _dot_tn(p[SLAB_V, rows, vc], k_end[h])
    for h in heads:
        vc = slice(h * GLA_DV, (h + 1) * GLA_DV)
        g = p[SLAB_GLA_G, rows, vc].astype(F32)
        br_ref[0, rows, vc] = (_rms(o[h], gg_ref[:, vc]) * _silu(g)).astype(BF)


def _pool_window_sums(p, win_ref):
    sums = []
    for g in range(len(POOL_WINDOWS)):
        cs = slice(g * POOL_GROUP_DIM, (g + 1) * POOL_GROUP_DIM)
        sums.append(_dot(win_ref[g], p[SLAB_U, :, cs]))
    return sums


def _pool_branch(p, sums, pw_ref, ps_ref, br_ref, ubuf_ref, t, tm):
    u = p[SLAB_U].astype(F32)
    ubuf_ref[HIST_PAD:2 * HIST_PAD, :] = u[0:HIST_PAD]
    pos = t * tm + lax.broadcasted_iota(jnp.int32, (tm, 1), 0)
    for g, w in enumerate(POOL_WINDOWS):
        cs = slice(g * POOL_GROUP_DIM, (g + 1) * POOL_GROUP_DIM)
        ug = u[:, cs]
        head = ug[0:HIST_PAD]
        for j in range(1, w):
            head = head + ubuf_ref[HIST_PAD - j:2 * HIST_PAD - j, cs]
        s = jnp.concatenate([head, sums[g][HIST_PAD:]], axis=0)
        cnt = jnp.minimum(w, pos + 1).astype(F32)
        diff = s / cnt - ug
        mixed = _dot(diff.astype(BF), pw_ref[g]) * ps_ref[:, cs]
        pg = p[SLAB_POOL_G, :, cs].astype(F32)
        br_ref[1, :, cs] = (mixed * _silu(pg)).astype(BF)
    ubuf_ref[0:HIST_PAD, :] = u[tm - HIST_PAD:tm]


def _xattn_probs(p, mk_ref):
    out = []
    for h in range(XA_HEADS):
        cs = slice(h * XA_HEAD_DIM, (h + 1) * XA_HEAD_DIM)
        s = _dot_nt(p[SLAB_XQ, :, cs], mk_ref[0, :, cs]) * (XA_HEAD_DIM ** -0.5)
        pr = jnp.exp(s - jnp.max(s, axis=-1, keepdims=True))
        out.append((pr.astype(BF), jnp.sum(pr, axis=-1, keepdims=True)))
    return out


def _xattn_branch(p, probs, mv_ref, br_ref):
    for h in range(XA_HEADS):
        cs = slice(h * XA_HEAD_DIM, (h + 1) * XA_HEAD_DIM)
        pr, denom = probs[h]
        o = _dot(pr, mv_ref[0, :, cs]) / denom
        xg = p[SLAB_XG, :, cs].astype(F32)
        br_ref[2, :, cs] = (o * _silu(xg)).astype(BF)


def _prompt_layer_kernel(xn_ref, xc_ref, g_ref, w_ref, walr_ref, wa2_ref, ba_ref, cum_ref, win_ref,
                         mk_ref, mv_ref, pw_ref, ps_ref, gg_ref, wbr_ref, wo_ref, fg_ref,
                         out_ref, s_out_ref, hist_out_ref,
                         h_scr, p_scr, la_scr, br_scr, st_ref, ubuf_ref, *, tm, nt, final):
    s = pl.program_id(0)
    t = jnp.maximum(s - 1, 0) % nt
    slot_w = s % 2
    slot_r = 1 - slot_w

    @pl.when(s == 0)
    def _():
        p_scr[1] = jnp.zeros(p_scr.shape[1:], p_scr.dtype)
        la_scr[1] = jnp.zeros(la_scr.shape[1:], la_scr.dtype)

    @pl.when(t == 0)
    def _():
        st_ref[...] = jnp.zeros_like(st_ref)
        ubuf_ref[0:HIST_PAD, :] = jnp.zeros((HIST_PAD, D_MODEL), F32)

    @pl.when(s == 0)
    def _():
        h_scr[0] = _rms(xc_ref[...], g_ref[...]).astype(BF)

    p_next = p_scr.at[slot_w]
    p = p_scr.at[slot_r]

    slabs = list(range(N_SLABS))

    def filler(n=1):
        for _ in range(n):
            _inproj_slab(h_scr[slot_w], w_ref, p_next, slabs.pop(0))

    alr = _inproj_gate_lowrank(h_scr[slot_w], walr_ref)
    bcum = _gla_cumdecay(cum_ref, la_scr.at[slot_r])
    filler()
    _inproj_gate(alr, wa2_ref, ba_ref, la_scr.at[slot_w])
    per_site = (N_SLABS - 2) // (2 * (tm // CHUNK))
    probs, sums = None, None
    for c in range(tm // CHUNK):
        _gla_chunk(p, bcum, gg_ref, br_scr, st_ref, c, functools.partial(filler, per_site))
        filler(per_site)
        if c == 0:
            probs = _xattn_probs(p, mk_ref)
            sums = _pool_window_sums(p, win_ref)
        elif c == 1:
            _xattn_branch(p, probs, mv_ref, br_scr)
            _pool_branch(p, sums, pw_ref, ps_ref, br_scr, ubuf_ref, t, tm)
    h_scr[slot_r] = _rms(xn_ref[...], g_ref[...]).astype(BF)
    merged = _merge_term(br_scr, p, wbr_ref, 2)
    merged = merged + _merge_term(br_scr, p, wbr_ref, 1)
    merged = merged + _merge_term(br_scr, p, wbr_ref, 0)
    filler()
    assert not slabs
    out_ref[...] = _merge_finish(merged, xc_ref[...], wo_ref, fg_ref, final)

    @pl.when((s > 0) & (t == nt - 1))
    def _():
        for h in range(GLA_HEADS):
            s_out_ref[0, 0, h] = st_ref[h].T
        hist_out_ref[0, 0] = ubuf_ref[1:HIST_PAD, :]


def _prompt_layer(x, ngain, w_proj, w_alr, wa2, ba, mk, mv, pool_w, pool_scale, gla_gain,
                  w_branch, w_out, fgain, carry, layer, depth, batch, seq, n_mem, tm, final):
    nt = seq // tm
    n_tiles = batch * nt
    assert tm // CHUNK == 2 and (N_SLABS - 2) % (2 * (tm // CHUNK)) == 0, "slab placement assumes two GLA chunks"
    n_alias = 0 if carry is None else len(carry)
    n_in = 17
    kern = _drop_alias_refs(functools.partial(_prompt_layer_kernel, tm=tm, nt=nt, final=final), n_in, n_alias)
    any_spec = pl.BlockSpec(memory_space=pl.ANY)
    scratch = [
        ((2, tm, D_MODEL), BF, 1),
        ((2, N_SLABS, tm, D_MODEL), BF, 1),
        ((2, tm, GLA_KEY_WIDTH), F32, 1),
        ((N_BRANCH, tm, D_MODEL), BF, 1),
        ((GLA_HEADS, GLA_DV, GLA_DK), F32, 1),
        ((2 * HIST_PAD, D_MODEL), F32, 1),
    ]

    def cur(s):
        return jnp.maximum(s - 1, 0)

    def seq_of(s):
        return cur(s) // nt

    return pl.pallas_call(
        kern,
        grid=(n_tiles + 1,),
        in_specs=[
            pl.BlockSpec((tm, D_MODEL), lambda s: (jnp.minimum(s + 1, n_tiles - 1), 0)),
            pl.BlockSpec((tm, D_MODEL), lambda s: (cur(s), 0)),
            _const_spec((1, D_MODEL)),
            _layer_spec((D_MODEL // 2, N_SLABS * D_MODEL), layer),
            _const_spec((GATE_RANK, D_MODEL)),
            _const_spec((GATE_RANK, GLA_KEY_WIDTH)),
            _const_spec((1, GLA_KEY_WIDTH)),
            _const_spec((tm, tm)),
            _const_spec((len(POOL_WINDOWS), tm, tm)),
            pl.BlockSpec((1, n_mem, D_MODEL), lambda s: (layer, seq_of(s), 0)),
            pl.BlockSpec((1, n_mem, D_MODEL), lambda s: (layer, seq_of(s), 0)),
            _const_spec((len(POOL_WINDOWS), POOL_GROUP_DIM, POOL_GROUP_DIM)),
            _const_spec((1, D_MODEL)),
            _const_spec((1, D_MODEL)),
            _layer_spec((N_BRANCH, D_MODEL // 2, D_MODEL), layer),
            _layer_spec((D_MODEL // 2, D_MODEL), layer),
            _const_spec((1, D_MODEL)),
        ] + [any_spec] * n_alias,
        out_specs=[
            pl.BlockSpec((tm, D_MODEL), lambda s: (cur(s), 0)),
            pl.BlockSpec((1, 1, GLA_HEADS, GLA_DK, GLA_DV), lambda s: (layer, seq_of(s), 0, 0, 0)),
            pl.BlockSpec((1, 1, POOL_HIST, D_MODEL), lambda s: (layer, seq_of(s), 0, 0)),
        ],
        out_shape=[
            jax.ShapeDtypeStruct((n_tiles * tm, D_MODEL), F32),
            jax.ShapeDtypeStruct((depth, batch, GLA_HEADS, GLA_DK, GLA_DV), F32),
            jax.ShapeDtypeStruct((depth, batch, POOL_HIST, D_MODEL), F32),
        ],
        scratch_shapes=[pltpu.VMEM(shape, dtype) for shape, dtype, _ in scratch],
        input_output_aliases={n_in + a: 1 + a for a in range(n_alias)},
        compiler_params=_params(
            ("arbitrary",), *scratch,
            ((tm, D_MODEL), F32, 6),
            ((D_MODEL // 2, (N_SLABS + N_BRANCH + 1) * D_MODEL), jnp.uint32, 1),
            ((n_mem, D_MODEL), BF, 4),
            ((tm, tm), BF, 1 + len(POOL_WINDOWS)),
            ((len(POOL_WINDOWS), POOL_GROUP_DIM, POOL_GROUP_DIM), BF, 1),
            ((GLA_HEADS, GLA_DK, GLA_DV), F32, 2)),
        name="prompt_layer",
    )(x, x, ngain, w_proj, w_alr, wa2, ba, _chunk_cumsum_matrix(tm), _window_matrices(tm),
      mk, mv, pool_w, pool_scale, gla_gain,
      w_branch, w_out, fgain, *(carry or ()))


def _inproj_kernel(x_ref, g_ref, w_ref, walr_ref, wa2_ref, ba_ref, p_ref, la_ref):
    h = _rms(x_ref[...], g_ref[...]).astype(BF)
    alr = _inproj_gate_lowrank(h, walr_ref)
    for j in range(N_SLABS):
        _inproj_slab(h, w_ref, p_ref, j)
    _inproj_gate(alr, wa2_ref, ba_ref, la_ref)


def _inproj(x, gain, w_proj, w_alr, w_a2, b_a, layer, tm, out_dtype):
    m_rows = x.shape[0]
    return pl.pallas_call(
        _inproj_kernel,
        grid=(m_rows // tm,),
        in_specs=[
            pl.BlockSpec((tm, D_MODEL), lambda i: (i, 0)),
            _const_spec((1, D_MODEL)),
            _layer_spec((D_MODEL // 2, N_SLABS * D_MODEL), layer),
            _const_spec((GATE_RANK, D_MODEL)),
            _const_spec((GATE_RANK, GLA_KEY_WIDTH)),
            _const_spec((1, GLA_KEY_WIDTH)),
        ],
        out_specs=[
            pl.BlockSpec((N_SLABS, tm, D_MODEL), lambda i: (0, i, 0)),
            pl.BlockSpec((tm, GLA_KEY_WIDTH), lambda i: (i, 0)),
        ],
        out_shape=[
            jax.ShapeDtypeStruct((N_SLABS, m_rows, D_MODEL), out_dtype),
            jax.ShapeDtypeStruct((m_rows, GLA_KEY_WIDTH), F32),
        ],
        compiler_params=_params(("parallel",),
                                ((tm, D_MODEL), F32, 2), ((D_MODEL // 2, N_SLABS * D_MODEL), jnp.uint32, 1),
                                ((N_SLABS, tm, D_MODEL), out_dtype, 2), ((tm, GLA_KEY_WIDTH), F32, 2)),
        name="inproj",
    )(x, gain, w_proj, w_alr, w_a2, b_a)


def _sample_mix_kernel(p_ref, la_ref, s0_ref, hist_ref, ck_hbm, cv_hbm, pw_ref, ps_ref, gg_ref,
                       br_ref, s_out_ref, hist_out_ref, diff_ref, kbuf, vbuf, cache_sem, *, sb, layer):
    n_steps = pl.num_programs(0) * pl.num_programs(1)
    step = pl.program_id(0) * pl.num_programs(1) + pl.program_id(1)

    def cache_copy(which, t):
        src, buf = ((ck_hbm, kbuf), (cv_hbm, vbuf))[which]
        slot = t % CACHE_RING
        return pltpu.make_async_copy(src.at[layer, pl.ds(t * sb, sb)], buf.at[slot], cache_sem.at[which, slot])

    @pl.when(step == 0)
    def _():
        for t in range(CACHE_RING - 1):
            cache_copy(0, t).start()
            cache_copy(1, t).start()

    @pl.when(step + CACHE_RING - 1 < n_steps)
    def _():
        cache_copy(0, step + CACHE_RING - 1).start()
        cache_copy(1, step + CACHE_RING - 1).start()

    cache_copy(0, step).wait()
    cache_copy(1, step).wait()
    ck_ref = kbuf.at[step % CACHE_RING]
    cv_ref = vbuf.at[step % CACHE_RING]

    r0 = pl.program_id(1) * sb
    erow = lax.broadcasted_iota(jnp.int32, (GLA_DK, GLA_DK), 0)
    ecol = lax.broadcasted_iota(jnp.int32, (GLA_DK, GLA_DK), 1)
    eye = erow == ecol

    def to_col(x):
        return jnp.sum(jnp.where(eye, jnp.broadcast_to(x, (GLA_DK, GLA_DK)), 0.0), axis=1, keepdims=True)

    for i in range(sb):
        r = pl.ds(r0 + i, 1)
        la = la_ref[r, :]
        qk = p_ref[SLAB_QK, r, :]
        vv = p_ref[SLAB_V, r, :]
        gla_g = p_ref[SLAB_GLA_G, r, :]
        for h in range(GLA_HEADS):
            kc = slice(h * GLA_DK, (h + 1) * GLA_DK)
            kc2 = slice(GLA_KEY_WIDTH + h * GLA_DK, GLA_KEY_WIDTH + (h + 1) * GLA_DK)
            vc = slice(h * GLA_DV, (h + 1) * GLA_DV)
            a_col = to_col(jnp.exp(la[:, kc]))
            q_col = to_col(qk[:, kc] * (GLA_DK ** -0.5))
            k_col = to_col(qk[:, kc2])
            s_new = a_col * s0_ref[0, i, h] + k_col * vv[:, vc]
            s_out_ref[0, i, h] = s_new
            o = jnp.sum(q_col * s_new, axis=0, keepdims=True)
            br_ref[0, r, vc] = _rms(o, gg_ref[:, vc]) * _silu(gla_g[:, vc])

        u = p_ref[SLAB_U, r, :]
        for g, w in enumerate(POOL_WINDOWS):
            cs = slice(g * POOL_GROUP_DIM, (g + 1) * POOL_GROUP_DIM)
            past = jnp.sum(hist_ref[0, POOL_HIST - (w - 1):POOL_HIST, r, cs], axis=0)
            diff_ref[r, cs] = (u[:, cs] + past) / float(w) - u[:, cs]
        hist_out_ref[0, 0:POOL_HIST - 1, r, :] = hist_ref[0, 1:POOL_HIST, r, :]
        hist_out_ref[0, POOL_HIST - 1, r, :] = u

        xq = p_ref[SLAB_XQ, r, :]
        xg = p_ref[SLAB_XG, r, :]
        half_cols = [slice(h * XA_HEAD_DIM + j * LANES, h * XA_HEAD_DIM + (j + 1) * LANES)
                     for j in range(XA_HEAD_DIM // LANES) for h in range(XA_HEADS)]
        xq_rows = jnp.concatenate([xq[:, cs] for cs in half_cols], axis=0)
        n_mem = ck_ref.shape[1]
        prod = (ck_ref[i] * xq_rows[None]).reshape(n_mem * SUBLANES, LANES).astype(BF)
        part = _dot(prod, jnp.ones((LANES, LANES), BF)).reshape(n_mem, SUBLANES, LANES)
        s = (part + pltpu.roll(part, XA_HEADS, axis=1)) * (XA_HEAD_DIM ** -0.5)
        p = jnp.exp(s - jnp.max(s, axis=0, keepdims=True))
        o = jnp.sum(p * cv_ref[i], axis=0) / jnp.sum(p, axis=0)
        halves = XA_HEAD_DIM // LANES
        o_row = jnp.concatenate([o[j * XA_HEADS + h:j * XA_HEADS + h + 1, :]
                                 for h in range(XA_HEADS) for j in range(halves)], axis=1)
        br_ref[2, r, :] = o_row * _silu(xg)

    @pl.when(pl.program_id(1) == pl.num_programs(1) - 1)
    def _():
        for g in range(len(POOL_WINDOWS)):
            cs = slice(g * POOL_GROUP_DIM, (g + 1) * POOL_GROUP_DIM)
            mixed = _dot(diff_ref[:, cs].astype(BF), pw_ref[g]) * ps_ref[:, cs]
            br_ref[1, :, cs] = mixed * _silu(p_ref[SLAB_POOL_G, :, cs])


def _cache_rows_view(c):
    depth, nb, n_mem = c.shape[:3]
    halves = XA_HEAD_DIM // LANES
    c = c.reshape(depth, nb, n_mem, XA_HEADS, halves, LANES)
    return c.transpose(0, 1, 2, 4, 3, 5).reshape(depth, nb, n_mem, halves * XA_HEADS, LANES)


def _sample_mix(p, la, s0, hist, ck, cv, pool_w, pool_scale, gla_gain, carry, layer, sb=SAMPLE_BLOCK):
    nb = s0.shape[1]
    n_mem = ck.shape[2]
    ck, cv = _cache_rows_view(ck), _cache_rows_view(cv)
    rb = SUBLANES
    halves = rb // sb
    n_alias = 0 if carry is None else len(carry)
    n_in = 9
    kern = _drop_alias_refs(functools.partial(_sample_mix_kernel, sb=sb, layer=layer), n_in, n_alias)
    any_spec = pl.BlockSpec(memory_space=pl.ANY)
    cache_ring = (CACHE_RING, sb, n_mem, SUBLANES, LANES)
    return pl.pallas_call(
        kern,
        grid=(nb // rb, halves),
        in_specs=[
            pl.BlockSpec((N_MIX_SLABS, rb, D_MODEL), lambda i, j: (0, i, 0)),
            pl.BlockSpec((rb, GLA_KEY_WIDTH), lambda i, j: (i, 0)),
            pl.BlockSpec((1, sb, GLA_HEADS, GLA_DK, GLA_DV), lambda i, j: (layer, i * halves + j, 0, 0, 0)),
            pl.BlockSpec((1, POOL_HIST, rb, D_MODEL), lambda i, j: (layer, 0, i, 0)),
            any_spec, any_spec,
            _const_spec((len(POOL_WINDOWS), POOL_GROUP_DIM, POOL_GROUP_DIM)),
            _const_spec((1, D_MODEL)),
            _const_spec((1, D_MODEL)),
        ] + [any_spec] * n_alias,
        out_specs=[
            pl.BlockSpec((N_BRANCH, rb, D_MODEL), lambda i, j: (0, i, 0)),
            pl.BlockSpec((1, sb, GLA_HEADS, GLA_DK, GLA_DV), lambda i, j: (layer, i * halves + j, 0, 0, 0)),
            pl.BlockSpec((1, POOL_HIST, rb, D_MODEL), lambda i, j: (layer, 0, i, 0)),
        ],
        out_shape=[
            jax.ShapeDtypeStruct((N_BRANCH, nb, D_MODEL), F32),
            jax.ShapeDtypeStruct(s0.shape, F32),
            jax.ShapeDtypeStruct(hist.shape, F32),
        ],
        scratch_shapes=[pltpu.VMEM((rb, D_MODEL), F32), pltpu.VMEM(cache_ring, F32), pltpu.VMEM(cache_ring, F32),
                        pltpu.SemaphoreType.DMA((2, CACHE_RING))],
        input_output_aliases={n_in + a: 1 + a for a in range(n_alias)},
        compiler_params=_params(
            ("arbitrary", "arbitrary"),
            ((N_MIX_SLABS + N_BRANCH + 1, rb, D_MODEL), F32, 2),
            ((sb, GLA_HEADS, GLA_DK, GLA_DV), F32, 4),
            ((POOL_HIST, rb, D_MODEL), F32, 4),
            (cache_ring, F32, 2),
            ((len(POOL_WINDOWS), POOL_GROUP_DIM, POOL_GROUP_DIM), BF, 1)),
        name="sample_mix",
    )(p, la, s0, hist, ck, cv, pool_w, pool_scale, gla_gain, *(carry or ()))


def _merge_kernel(br_ref, p_ref, x_ref, wbr_ref, wo_ref, fg_ref, out_ref, *, final):
    merged = _merge_term(br_ref, p_ref, wbr_ref, 0)
    for n in range(1, N_BRANCH):
        merged = merged + _merge_term(br_ref, p_ref, wbr_ref, n)
    out_ref[...] = _merge_finish(merged, x_ref[...], wo_ref, fg_ref, final)


def _merge_out(br, p, x, w_branch, w_out, final_gain, layer, final):
    m_rows = x.shape[0]
    whole = lambda shape: pl.BlockSpec(shape, lambda i: (0,) * len(shape))
    return pl.pallas_call(
        functools.partial(_merge_kernel, final=final),
        grid=(1,),
        in_specs=[
            whole((N_BRANCH, m_rows, D_MODEL)),
            whole((N_SLABS, m_rows, D_MODEL)),
            whole((m_rows, D_MODEL)),
            _layer_spec((N_BRANCH, D_MODEL // 2, D_MODEL), layer),
            _layer_spec((D_MODEL // 2, D_MODEL), layer),
            _const_spec((1, D_MODEL)),
        ],
        out_specs=whole((m_rows, D_MODEL)),
        out_shape=jax.ShapeDtypeStruct((m_rows, D_MODEL), F32),
        compiler_params=_params(("arbitrary",),
                                ((N_BRANCH + N_SLABS + 2, m_rows, D_MODEL), F32, 2),
                                ((D_MODEL // 2, (N_BRANCH + 1) * D_MODEL), jnp.uint32, 1)),
        name="merge_out",
    )(br, p, x, w_branch, w_out, final_gain)


def kernel(x_prompt, x_sample, mem_prompt, cache_mem_k, cache_mem_v, state_gla, state_pool, w_in, w_a2, b_a, gla_gain, pool_w, pool_scale, w_mk, w_mv, w_branch, w_out, norm_gain, final_gain):
    batch, seq, _ = x_prompt.shape
    nb = x_sample.shape[0]
    n_mem = mem_prompt.shape[1]
    depth = w_in.shape[0]
    tm = PROMPT_TILE
    assert seq % tm == 0 and nb % SUBLANES == 0 and x_prompt.shape[2] == D_MODEL

    xp = x_prompt.reshape(batch * seq, D_MODEL)
    xs = x_sample.reshape(nb, D_MODEL)
    mem = mem_prompt.reshape(batch * n_mem, D_MODEL)
    fgain = final_gain.reshape(1, D_MODEL)

    mk, mv, mk_bf, mv_bf = _kvproj(mem, w_mk.astype(BF), w_mv.astype(BF), batch, n_mem)

    w_in_t = jnp.swapaxes(w_in, 1, 2)
    pool_rows = jnp.swapaxes(state_pool, 1, 2)
    w_proj = _pack_w_in(w_in_t)
    wb = _pack_matrices(w_branch.reshape(depth * N_BRANCH, D_MODEL, D_MODEL))
    wb = wb.reshape(depth, N_BRANCH, D_MODEL // 2, D_MODEL)
    wo = _pack_matrices(w_out)

    carry_p, carry_s = None, None
    for l in range(depth):
        final = l == depth - 1
        w_alr = w_in_t[l, ALR_START:ALR_START + GATE_RANK, :]
        wa2 = w_a2[l].astype(BF)
        ba = b_a[l].reshape(1, GLA_KEY_WIDTH)
        ngain = norm_gain[l].reshape(1, D_MODEL)
        ggain = gla_gain[l].reshape(1, D_MODEL)
        pscale = pool_scale[l].reshape(1, D_MODEL)
        pw = pool_w[l].astype(BF)

        xp, s_all, hist_all = _prompt_layer(xp, ngain, w_proj, w_alr, wa2, ba, mk_bf, mv_bf, pw, pscale,
                                            ggain, wb, wo, fgain, carry_p, l, depth, batch, seq, n_mem, tm, final)
        carry_p = (s_all, hist_all)

        ps, las = _inproj(xs, ngain, w_proj, w_alr, wa2, ba, l, nb, F32)
        brs, s_new, hist_new = _sample_mix(ps, las, state_gla, pool_rows, cache_mem_k, cache_mem_v,
                                           pw, pscale, ggain, carry_s, l)
        carry_s = (s_new, hist_new)
        xs = _merge_out(brs, ps, xs, wb, wo, fgain, l, final)

    return (xp.reshape(batch, seq, D_MODEL), xs.reshape(nb, 1, D_MODEL),
            mk, mv, carry_p[0], carry_p[1], carry_s[0], jnp.swapaxes(carry_s[1], 1, 2))
```

```python
import functools
import math

import jax
import jax.numpy as jnp
from jax import lax
from jax.experimental import pallas as pl
from jax.experimental.pallas import tpu as pltpu

D_MODEL = 1024
GLA_HEADS = 4
GLA_DK = 128
GLA_DV = 256
GLA_KEY_WIDTH = GLA_HEADS * GLA_DK
GATE_RANK = 16
GATE_TAU = 16.0
CHUNK = 128
POOL_WINDOWS = (2, 4, 8, 16)
POOL_GROUP_DIM = 256
POOL_HIST = 15
HIST_PAD = 16
XA_HEADS = 4
XA_HEAD_DIM = 256
N_BRANCH = 3
EPS = 1e-6
SUBLANES = 8
LANES = 128

SLAB_QK, SLAB_V, SLAB_GLA_G, SLAB_U, SLAB_POOL_G, SLAB_XQ, SLAB_XG, SLAB_MERGE = 0, 1, 2, 3, 4, 5, 6, 7
N_SLABS = 10
N_MIX_SLABS = 7
N_HEAD_SLABS = 3
ALR_START = N_HEAD_SLABS * D_MODEL

BF = jnp.bfloat16
F32 = jnp.float32
MIB = 1 << 20

PROMPT_TILE = 256
SAMPLE_BLOCK = 4
CACHE_RING = 4
VMEM_COMPILER_SCRATCH = 8 * MIB


def _dot(a, b):
    return jnp.dot(a, b, preferred_element_type=F32)


def _dot_nt(a, b):
    return lax.dot_general(a, b, (((1,), (1,)), ((), ())), preferred_element_type=F32)


def _dot_tn(a, b):
    return lax.dot_general(a, b, (((0,), (0,)), ((), ())), preferred_element_type=F32)


def _pack_rows(w):
    return pltpu.bitcast(w.astype(BF), jnp.uint32)


def _unpack_rows(w_words):
    return pltpu.bitcast(w_words, BF)


def _params(sem, *buffers):
    need = sum(math.prod(shape) * jnp.dtype(dtype).itemsize * copies for shape, dtype, copies in buffers)
    return pltpu.CompilerParams(dimension_semantics=sem, vmem_limit_bytes=need + VMEM_COMPILER_SCRATCH)


def _pack_w_in_kernel(a_ref, b_ref, o_ref):
    j = pl.program_id(1)

    @pl.when(j < N_HEAD_SLABS)
    def _():
        o_ref[0] = _pack_rows(a_ref[0].T)

    @pl.when(j >= N_HEAD_SLABS)
    def _():
        o_ref[0] = _pack_rows(jnp.concatenate([a_ref[0, GATE_RANK:, :], b_ref[0]], axis=0).T)


def _pack_w_in(w_in_t):
    depth = w_in_t.shape[0]
    return pl.pallas_call(
        _pack_w_in_kernel,
        grid=(depth, N_SLABS),
        in_specs=[
            pl.BlockSpec((1, D_MODEL, D_MODEL), lambda l, j: (l, j, 0)),
            pl.BlockSpec((1, GATE_RANK, D_MODEL), lambda l, j: (l, (j + 1) * (D_MODEL // GATE_RANK), 0)),
        ],
        out_specs=pl.BlockSpec((1, D_MODEL // 2, D_MODEL), lambda l, j: (l, 0, j)),
        out_shape=jax.ShapeDtypeStruct((depth, D_MODEL // 2, N_SLABS * D_MODEL), jnp.uint32),
        compiler_params=_params(("parallel", "parallel"),
                                ((D_MODEL, D_MODEL), F32, 2), ((GATE_RANK, D_MODEL), F32, 2),
                                ((D_MODEL // 2, D_MODEL), jnp.uint32, 2)),
        name="pack_w_in",
    )(w_in_t, w_in_t)


def _pack_matrices_kernel(a_ref, o_ref):
    o_ref[0] = _pack_rows(a_ref[0])


def _pack_matrices(w):
    return pl.pallas_call(
        _pack_matrices_kernel,
        grid=(w.shape[0],),
        in_specs=[pl.BlockSpec((1, D_MODEL, D_MODEL), lambda r: (r, 0, 0))],
        out_specs=pl.BlockSpec((1, D_MODEL // 2, D_MODEL), lambda r: (r, 0, 0)),
        out_shape=jax.ShapeDtypeStruct((w.shape[0], D_MODEL // 2, D_MODEL), jnp.uint32),
        compiler_params=_params(("parallel",), ((D_MODEL, D_MODEL), F32, 2), ((D_MODEL // 2, D_MODEL), jnp.uint32, 2)),
        name="pack_matrices",
    )(w)


def _silu(x):
    return x * jax.nn.sigmoid(x)


def _rms(x, gain):
    ms = jnp.mean(x * x, axis=-1, keepdims=True)
    return x * lax.rsqrt(ms + EPS) * gain


def _const_spec(shape):
    zeros = (0,) * len(shape)
    return pl.BlockSpec(shape, lambda *_: zeros, pipeline_mode=pl.Buffered(1))


def _layer_spec(shape, layer):
    index = (layer,) + (0,) * len(shape)
    return pl.BlockSpec((1,) + tuple(shape), lambda *_: index, pipeline_mode=pl.Buffered(1))


def _drop_alias_refs(body, n_in, n_alias):
    def kern(*refs):
        return body(*refs[:n_in], *refs[n_in + n_alias:])
    return kern


def _kvproj_kernel(m_ref, wk_ref, wv_ref, k_ref, v_ref, kb_ref, vb_ref):
    m = m_ref[...].astype(BF)
    k = _dot(m, wk_ref[0])
    v = _dot(m, wv_ref[0])
    for h in range(XA_HEADS):
        cs = slice(h * XA_HEAD_DIM, (h + 1) * XA_HEAD_DIM)
        k_ref[0, 0, :, h, :] = k[:, cs]
        v_ref[0, 0, :, h, :] = v[:, cs]
    kb_ref[0] = k.astype(BF)
    vb_ref[0] = v.astype(BF)


def _kvproj(mem, wk, wv, batch, n_mem):
    depth = wk.shape[0]
    w_spec = pl.BlockSpec((1, D_MODEL, D_MODEL), lambda l, b: (l, 0, 0))
    out5 = pl.BlockSpec((1, 1, n_mem, XA_HEADS, XA_HEAD_DIM), lambda l, b: (l, b, 0, 0, 0))
    out_bf = pl.BlockSpec((1, n_mem, D_MODEL), lambda l, b: (l, b, 0))
    return pl.pallas_call(
        _kvproj_kernel,
        grid=(depth, batch),
        in_specs=[pl.BlockSpec((n_mem, D_MODEL), lambda l, b: (b, 0)), w_spec, w_spec],
        out_specs=[out5, out5, out_bf, out_bf],
        out_shape=[jax.ShapeDtypeStruct((depth, batch, n_mem, XA_HEADS, XA_HEAD_DIM), F32)] * 2
        + [jax.ShapeDtypeStruct((depth, batch * n_mem, D_MODEL), BF)] * 2,
        compiler_params=_params(("parallel", "parallel"),
                                ((n_mem, D_MODEL), F32, 2), ((D_MODEL, D_MODEL), BF, 4),
                                ((n_mem, D_MODEL), F32, 4), ((n_mem, D_MODEL), BF, 4)),
        name="kvproj",
    )(mem, wk, wv)


def _inproj_slab(h, w_ref, p_out, j):
    w = w_ref[0, :, j * D_MODEL:(j + 1) * D_MODEL]
    p_out[j] = _dot(h, _unpack_rows(w)).astype(p_out.dtype)


def _inproj_gate_lowrank(h, walr_ref):
    return _dot_nt(h, walr_ref[...].astype(BF)).astype(BF)


def _inproj_gate(alr, wa2_ref, ba_ref, la_out):
    z = _dot(alr, wa2_ref[...]) + ba_ref[...]
    la_out[...] = (jnp.minimum(z, 0.0) - jnp.log(1.0 + jnp.exp(-jnp.abs(z)))) * (1.0 / GATE_TAU)


def _merge_term(br_ref, p, wbr_ref, n):
    return jax.nn.sigmoid(p[SLAB_MERGE + n].astype(F32)) * _dot(br_ref[n].astype(BF), _unpack_rows(wbr_ref[0, n]))


def _merge_finish(merged, x, wo_ref, fg_ref, final):
    x_new = x + _dot(merged.astype(BF), _unpack_rows(wo_ref[0]))
    return _rms(x_new, fg_ref[...]) if final else x_new


def _chunk_cumsum_matrix(tm):
    row = lax.broadcasted_iota(jnp.int32, (tm, tm), 0)
    col = lax.broadcasted_iota(jnp.int32, (tm, tm), 1)
    return (((row // CHUNK) == (col // CHUNK)) & (row >= col)).astype(BF)


def _window_matrices(tm):
    row = lax.broadcasted_iota(jnp.int32, (tm, tm), 0)
    col = lax.broadcasted_iota(jnp.int32, (tm, tm), 1)
    return jnp.stack([((row >= col) & (row - col < w)).astype(BF) for w in POOL_WINDOWS])


def _gla_cumdecay(cum_ref, la_ref):
    la = la_ref[...]
    la_hi = la.astype(BF)
    la_lo = (la - la_hi.astype(F32)).astype(BF)
    return _dot(cum_ref[...], la_hi) + _dot(cum_ref[...], la_lo)


def _gla_chunk(p, bcum, gg_ref, br_ref, st_ref, c, filler):
    crow = lax.broadcasted_iota(jnp.int32, (CHUNK, CHUNK), 0)
    ccol = lax.broadcasted_iota(jnp.int32, (CHUNK, CHUNK), 1)
    causal = crow >= ccol
    rows = slice(c * CHUNK, (c + 1) * CHUNK)
    heads = range(GLA_HEADS)
    q_dec, k_end, decay, att = [], [], [], []
    for h in heads:
        kc = slice(h * GLA_DK, (h + 1) * GLA_DK)
        kc2 = slice(GLA_KEY_WIDTH + h * GLA_DK, GLA_KEY_WIDTH + (h + 1) * GLA_DK)
        b = bcum[rows, kc]
        b_mid = b[CHUNK // 2 - 1:CHUNK // 2, :]
        b_last = b[CHUNK - 1:CHUNK, :]
        q = p[SLAB_QK, rows, kc].astype(F32) * (GLA_DK ** -0.5)
        k = p[SLAB_QK, rows, kc2].astype(F32)
        q_dec.append((q * jnp.exp(b)).astype(BF))
        q_mid = (q * jnp.exp(b - b_mid)).astype(BF)
        k_mid = (k * jnp.exp(b_mid - b)).astype(BF)
        k_end.append((k * jnp.exp(b_last - b)).astype(BF))
        decay.append(jnp.exp(b_last))
        att.append(_dot_nt(q_mid, k_mid))
    filler()
    o = []
    for h in heads:
        vc = slice(h * GLA_DV, (h + 1) * GLA_DV)
        a = jnp.where(causal, att[h], 0.0).astype(BF)
        o.append(_dot(a, p[SLAB_V, rows, vc]) + _dot_nt(q_dec[h], st_ref[h].astype(BF)))
    for h in heads:
        vc = slice(h * GLA_DV, (h + 1) * GLA_DV)
        st_ref[h] = decay[h] * st_ref[h] + _dot_tn(p[SLAB_V, rows, vc], k_end[h])
    for h in heads:
        vc = slice(h * GLA_DV, (h + 1) * GLA_DV)
        g = p[SLAB_GLA_G, rows, vc].astype(F32)
        br_ref[0, rows, vc] = (_rms(o[h], gg_ref[:, vc]) * _silu(g)).astype(BF)


def _pool_window_sums(p, win_ref):
    sums = []
    for g in range(len(POOL_WINDOWS)):
        cs = slice(g * POOL_GROUP_DIM, (g + 1) * POOL_GROUP_DIM)
        sums.append(_dot(win_ref[g], p[SLAB_U, :, cs]))
    return sums


def _pool_branch(p, sums, pw_ref, ps_ref, br_ref, ubuf_ref, t, tm):
    u = p[SLAB_U].astype(F32)
    ubuf_ref[HIST_PAD:2 * HIST_PAD, :] = u[0:HIST_PAD]
    pos = t * tm + lax.broadcasted_iota(jnp.int32, (tm, 1), 0)
    for g, w in enumerate(POOL_WINDOWS):
        cs = slice(g * POOL_GROUP_DIM, (g + 1) * POOL_GROUP_DIM)
        ug = u[:, cs]
        head = ug[0:HIST_PAD]
        for j in range(1, w):
            head = head + ubuf_ref[HIST_PAD - j:2 * HIST_PAD - j, cs]
        s = jnp.concatenate([head, sums[g][HIST_PAD:]], axis=0)
        cnt = jnp.minimum(w, pos + 1).astype(F32)
        diff = s / cnt - ug
        mixed = _dot(diff.astype(BF), pw_ref[g]) * ps_ref[:, cs]
        pg = p[SLAB_POOL_G, :, cs].astype(F32)
        br_ref[1, :, cs] = (mixed * _silu(pg)).astype(BF)
    ubuf_ref[0:HIST_PAD, :] = u[tm - HIST_PAD:tm]


def _xattn_probs(p, mk_ref):
    out = []
    for h in range(XA_HEADS):
        cs = slice(h * XA_HEAD_DIM, (h + 1) * XA_HEAD_DIM)
        s = _dot_nt(p[SLAB_XQ, :, cs], mk_ref[0, :, cs]) * (XA_HEAD_DIM ** -0.5)
        pr = jnp.exp(s - jnp.max(s, axis=-1, keepdims=True))
        out.append((pr.astype(BF), jnp.sum(pr, axis=-1, keepdims=True)))
    return out


def _xattn_branch(p, probs, mv_ref, br_ref):
    for h in range(XA_HEADS):
        cs = slice(h * XA_HEAD_DIM, (h + 1) * XA_HEAD_DIM)
        pr, denom = probs[h]
        o = _dot(pr, mv_ref[0, :, cs]) / denom
        xg = p[SLAB_XG, :, cs].astype(F32)
        br_ref[2, :, cs] = (o * _silu(xg)).astype(BF)


def _prompt_layer_kernel(xn_ref, xc_ref, g_ref, w_ref, walr_ref, wa2_ref, ba_ref, cum_ref, win_ref,
                         mk_ref, mv_ref, pw_ref, ps_ref, gg_ref, wbr_ref, wo_ref, fg_ref,
                         out_ref, s_out_ref, hist_out_ref,
                         h_scr, p_scr, la_scr, br_scr, st_ref, ubuf_ref, *, tm, nt, final):
    s = pl.program_id(0)
    t = jnp.maximum(s - 1, 0) % nt
    slot_w = s % 2
    slot_r = 1 - slot_w

    @pl.when(s == 0)
    def _():
        p_scr[1] = jnp.zeros(p_scr.shape[1:], p_scr.dtype)
        la_scr[1] = jnp.zeros(la_scr.shape[1:], la_scr.dtype)

    @pl.when(t == 0)
    def _():
        st_ref[...] = jnp.zeros_like(st_ref)
        ubuf_ref[0:HIST_PAD, :] = jnp.zeros((HIST_PAD, D_MODEL), F32)

    @pl.when(s == 0)
    def _():
        h_scr[0] = _rms(xc_ref[...], g_ref[...]).astype(BF)

    p_next = p_scr.at[slot_w]
    p = p_scr.at[slot_r]

    slabs = list(range(N_SLABS))

    def filler(n=1):
        for _ in range(n):
            _inproj_slab(h_scr[slot_w], w_ref, p_next, slabs.pop(0))

    alr = _inproj_gate_lowrank(h_scr[slot_w], walr_ref)
    bcum = _gla_cumdecay(cum_ref, la_scr.at[slot_r])
    filler()
    _inproj_gate(alr, wa2_ref, ba_ref, la_scr.at[slot_w])
    per_site = (N_SLABS - 2) // (2 * (tm // CHUNK))
    probs, sums = None, None
    for c in range(tm // CHUNK):
        _gla_chunk(p, bcum, gg_ref, br_scr, st_ref, c, functools.partial(filler, per_site))
        filler(per_site)
        if c == 0:
            probs = _xattn_probs(p, mk_ref)
            sums = _pool_window_sums(p, win_ref)
        elif c == 1:
            _xattn_branch(p, probs, mv_ref, br_scr)
            _pool_branch(p, sums, pw_ref, ps_ref, br_scr, ubuf_ref, t, tm)
    h_scr[slot_r] = _rms(xn_ref[...], g_ref[...]).astype(BF)
    merged = _merge_term(br_scr, p, wbr_ref, 2)
    merged = merged + _merge_term(br_scr, p, wbr_ref, 1)
    merged = merged + _merge_term(br_scr, p, wbr_ref, 0)
    filler()
    assert not slabs
    out_ref[...] = _merge_finish(merged, xc_ref[...], wo_ref, fg_ref, final)

    @pl.when((s > 0) & (t == nt - 1))
    def _():
        for h in range(GLA_HEADS):
            s_out_ref[0, 0, h] = st_ref[h].T
        hist_out_ref[0, 0] = ubuf_ref[1:HIST_PAD, :]


def _prompt_layer(x, ngain, w_proj, w_alr, wa2, ba, mk, mv, pool_w, pool_scale, gla_gain,
                  w_branch, w_out, fgain, carry, layer, depth, batch, seq, n_mem, tm, final):
    nt = seq // tm
    n_tiles = batch * nt
    assert tm // CHUNK == 2 and (N_SLABS - 2) % (2 * (tm // CHUNK)) == 0, "slab placement assumes two GLA chunks"
    n_alias = 0 if carry is None else len(carry)
    n_in = 17
    kern = _drop_alias_refs(functools.partial(_prompt_layer_kernel, tm=tm, nt=nt, final=final), n_in, n_alias)
    any_spec = pl.BlockSpec(memory_space=pl.ANY)
    scratch = [
        ((2, tm, D_MODEL), BF, 1),
        ((2, N_SLABS, tm, D_MODEL), BF, 1),
        ((2, tm, GLA_KEY_WIDTH), F32, 1),
        ((N_BRANCH, tm, D_MODEL), BF, 1),
        ((GLA_HEADS, GLA_DV, GLA_DK), F32, 1),
        ((2 * HIST_PAD, D_MODEL), F32, 1),
    ]

    def cur(s):
        return jnp.maximum(s - 1, 0)

    def seq_of(s):
        return cur(s) // nt

    return pl.pallas_call(
        kern,
        grid=(n_tiles + 1,),
        in_specs=[
            pl.BlockSpec((tm, D_MODEL), lambda s: (jnp.minimum(s + 1, n_tiles - 1), 0)),
            pl.BlockSpec((tm, D_MODEL), lambda s: (cur(s), 0)),
            _const_spec((1, D_MODEL)),
            _layer_spec((D_MODEL // 2, N_SLABS * D_MODEL), layer),
            _const_spec((GATE_RANK, D_MODEL)),
            _const_spec((GATE_RANK, GLA_KEY_WIDTH)),
            _const_spec((1, GLA_KEY_WIDTH)),
            _const_spec((tm, tm)),
            _const_spec((len(POOL_WINDOWS), tm, tm)),
            pl.BlockSpec((1, n_mem, D_MODEL), lambda s: (layer, seq_of(s), 0)),
            pl.BlockSpec((1, n_mem, D_MODEL), lambda s: (layer, seq_of(s), 0)),
            _const_spec((len(POOL_WINDOWS), POOL_GROUP_DIM, POOL_GROUP_DIM)),
            _const_spec((1, D_MODEL)),
            _const_spec((1, D_MODEL)),
            _layer_spec((N_BRANCH, D_MODEL // 2, D_MODEL), layer),
            _layer_spec((D_MODEL // 2, D_MODEL), layer),
            _const_spec((1, D_MODEL)),
        ] + [any_spec] * n_alias,
        out_specs=[
            pl.BlockSpec((tm, D_MODEL), lambda s: (cur(s), 0)),
            pl.BlockSpec((1, 1, GLA_HEADS, GLA_DK, GLA_DV), lambda s: (layer, seq_of(s), 0, 0, 0)),
            pl.BlockSpec((1, 1, POOL_HIST, D_MODEL), lambda s: (layer, seq_of(s), 0, 0)),
        ],
        out_shape=[
            jax.ShapeDtypeStruct((n_tiles * tm, D_MODEL), F32),
            jax.ShapeDtypeStruct((depth, batch, GLA_HEADS, GLA_DK, GLA_DV), F32),
            jax.ShapeDtypeStruct((depth, batch, POOL_HIST, D_MODEL), F32),
        ],
        scratch_shapes=[pltpu.VMEM(shape, dtype) for shape, dtype, _ in scratch],
        input_output_aliases={n_in + a: 1 + a for a in range(n_alias)},
        compiler_params=_params(
            ("arbitrary",), *scratch,
            ((tm, D_MODEL), F32, 6),
            ((D_MODEL // 2, (N_SLABS + N_BRANCH + 1) * D_MODEL), jnp.uint32, 1),
            ((n_mem, D_MODEL), BF, 4),
            ((tm, tm), BF, 1 + len(POOL_WINDOWS)),
            ((len(POOL_WINDOWS), POOL_GROUP_DIM, POOL_GROUP_DIM), BF, 1),
            ((GLA_HEADS, GLA_DK, GLA_DV), F32, 2)),
        name="prompt_layer",
    )(x, x, ngain, w_proj, w_alr, wa2, ba, _chunk_cumsum_matrix(tm), _window_matrices(tm),
      mk, mv, pool_w, pool_scale, gla_gain,
      w_branch, w_out, fgain, *(carry or ()))


def _inproj_kernel(x_ref, g_ref, w_ref, walr_ref, wa2_ref, ba_ref, p_ref, la_ref):
    h = _rms(x_ref[...], g_ref[...]).astype(BF)
    alr = _inproj_gate_lowrank(h, walr_ref)
    for j in range(N_SLABS):
        _inproj_slab(h, w_ref, p_ref, j)
    _inproj_gate(alr, wa2_ref, ba_ref, la_ref)


def _inproj(x, gain, w_proj, w_alr, w_a2, b_a, layer, tm, out_dtype):
    m_rows = x.shape[0]
    return pl.pallas_call(
        _inproj_kernel,
        grid=(m_rows // tm,),
        in_specs=[
            pl.BlockSpec((tm, D_MODEL), lambda i: (i, 0)),
            _const_spec((1, D_MODEL)),
            _layer_spec((D_MODEL // 2, N_SLABS * D_MODEL), layer),
            _const_spec((GATE_RANK, D_MODEL)),
            _const_spec((GATE_RANK, GLA_KEY_WIDTH)),
            _const_spec((1, GLA_KEY_WIDTH)),
        ],
        out_specs=[
            pl.BlockSpec((N_SLABS, tm, D_MODEL), lambda i: (0, i, 0)),
            pl.BlockSpec((tm, GLA_KEY_WIDTH), lambda i: (i, 0)),
        ],
        out_shape=[
            jax.ShapeDtypeStruct((N_SLABS, m_rows, D_MODEL), out_dtype),
            jax.ShapeDtypeStruct((m_rows, GLA_KEY_WIDTH), F32),
        ],
        compiler_params=_params(("parallel",),
                                ((tm, D_MODEL), F32, 2), ((D_MODEL // 2, N_SLABS * D_MODEL), jnp.uint32, 1),
                                ((N_SLABS, tm, D_MODEL), out_dtype, 2), ((tm, GLA_KEY_WIDTH), F32, 2)),
        name="inproj",
    )(x, gain, w_proj, w_alr, w_a2, b_a)


def _sample_mix_kernel(p_ref, la_ref, s0_hbm, hist_ref, ck_hbm, cv_hbm, pw_ref, ps_ref, gg_ref,
                       br_ref, s_out_ref, hist_out_ref, diff_ref, sbuf, kbuf, vbuf, ring_sem, *, sb, layer, n_steps):
    step = pl.program_id(0) * pl.num_programs(1) + pl.program_id(1)
    streams = ((s0_hbm, sbuf), (ck_hbm, kbuf), (cv_hbm, vbuf))

    def ring_copy(which, t):
        src, buf = streams[which]
        slot = t % CACHE_RING
        return pltpu.make_async_copy(src.at[layer, pl.ds(t * sb, sb)], buf.at[slot], ring_sem.at[which, slot])

    @pl.when(step == 0)
    def _():
        for t in range(min(CACHE_RING - 1, n_steps)):
            for which in range(len(streams)):
                ring_copy(which, t).start()

    @pl.when(step + CACHE_RING - 1 < n_steps)
    def _():
        for which in range(len(streams)):
            ring_copy(which, step + CACHE_RING - 1).start()

    for which in range(len(streams)):
        ring_copy(which, step).wait()
    s0_ref = sbuf.at[step % CACHE_RING]
    ck_ref = kbuf.at[step % CACHE_RING]
    cv_ref = vbuf.at[step % CACHE_RING]

    r0 = pl.program_id(1) * sb
    erow = lax.broadcasted_iota(jnp.int32, (GLA_DK, GLA_DK), 0)
    ecol = lax.broadcasted_iota(jnp.int32, (GLA_DK, GLA_DK), 1)
    eye = erow == ecol

    def to_col(x):
        return jnp.sum(jnp.where(eye, jnp.broadcast_to(x, (GLA_DK, GLA_DK)), 0.0), axis=1, keepdims=True)

    for i in range(sb):
        r = pl.ds(r0 + i, 1)
        la = la_ref[r, :]
        qk = p_ref[SLAB_QK, r, :]
        vv = p_ref[SLAB_V, r, :]
        gla_g = p_ref[SLAB_GLA_G, r, :]
        for h in range(GLA_HEADS):
            kc = slice(h * GLA_DK, (h + 1) * GLA_DK)
            kc2 = slice(GLA_KEY_WIDTH + h * GLA_DK, GLA_KEY_WIDTH + (h + 1) * GLA_DK)
            vc = slice(h * GLA_DV, (h + 1) * GLA_DV)
            a_col = to_col(jnp.exp(la[:, kc]))
            q_col = to_col(qk[:, kc] * (GLA_DK ** -0.5))
            k_col = to_col(qk[:, kc2])
            s_new = a_col * s0_ref[i, h] + k_col * vv[:, vc]
            s_out_ref[0, i, h] = s_new
            o = jnp.sum(q_col * s_new, axis=0, keepdims=True)
            br_ref[0, r, vc] = _rms(o, gg_ref[:, vc]) * _silu(gla_g[:, vc])

        u = p_ref[SLAB_U, r, :]
        for g, w in enumerate(POOL_WINDOWS):
            cs = slice(g * POOL_GROUP_DIM, (g + 1) * POOL_GROUP_DIM)
            past = jnp.sum(hist_ref[0, POOL_HIST - (w - 1):POOL_HIST, r, cs], axis=0)
            diff_ref[r, cs] = (u[:, cs] + past) / float(w) - u[:, cs]
        hist_out_ref[0, 0:POOL_HIST - 1, r, :] = hist_ref[0, 1:POOL_HIST, r, :]
        hist_out_ref[0, POOL_HIST - 1, r, :] = u

        xq = p_ref[SLAB_XQ, r, :]
        xg = p_ref[SLAB_XG, r, :]
        half_cols = [slice(h * XA_HEAD_DIM + j * LANES, h * XA_HEAD_DIM + (j + 1) * LANES)
                     for j in range(XA_HEAD_DIM // LANES) for h in range(XA_HEADS)]
        xq_rows = jnp.concatenate([xq[:, cs] for cs in half_cols], axis=0)
        n_mem = ck_ref.shape[1]
        prod = (ck_ref[i] * xq_rows[None]).reshape(n_mem * SUBLANES, LANES).astype(BF)
        part = _dot(prod, jnp.ones((LANES, LANES), BF)).reshape(n_mem, SUBLANES, LANES)
        s = (part + pltpu.roll(part, XA_HEADS, axis=1)) * (XA_HEAD_DIM ** -0.5)
        p = jnp.exp(s - jnp.max(s, axis=0, keepdims=True))
        o = jnp.sum(p * cv_ref[i], axis=0) / jnp.sum(p, axis=0)
        halves = XA_HEAD_DIM // LANES
        o_row = jnp.concatenate([o[j * XA_HEADS + h:j * XA_HEADS + h + 1, :]
                                 for h in range(XA_HEADS) for j in range(halves)], axis=1)
        br_ref[2, r, :] = o_row * _silu(xg)

    @pl.when(pl.program_id(1) == pl.num_programs(1) - 1)
    def _():
        for g in range(len(POOL_WINDOWS)):
            cs = slice(g * POOL_GROUP_DIM, (g + 1) * POOL_GROUP_DIM)
            mixed = _dot(diff_ref[:, cs].astype(BF), pw_ref[g]) * ps_ref[:, cs]
            br_ref[1, :, cs] = mixed * _silu(p_ref[SLAB_POOL_G, :, cs])


def _cache_rows_view(c):
    depth, nb, n_mem = c.shape[:3]
    halves = XA_HEAD_DIM // LANES
    c = c.reshape(depth, nb, n_mem, XA_HEADS, halves, LANES)
    return c.transpose(0, 1, 2, 4, 3, 5).reshape(depth, nb, n_mem, halves * XA_HEADS, LANES)


def _sample_mix(p, la, s0, hist, ck, cv, pool_w, pool_scale, gla_gain, carry, layer, sb=SAMPLE_BLOCK):
    nb = s0.shape[1]
    n_mem = ck.shape[2]
    ck, cv = _cache_rows_view(ck), _cache_rows_view(cv)
    rb = SUBLANES
    halves = rb // sb
    n_alias = 0 if carry is None else len(carry)
    n_in = 9
    kern = _drop_alias_refs(functools.partial(_sample_mix_kernel, sb=sb, layer=layer, n_steps=nb // sb),
                            n_in, n_alias)
    any_spec = pl.BlockSpec(memory_space=pl.ANY)
    cache_ring = (CACHE_RING, sb, n_mem, SUBLANES, LANES)
    state_ring = (CACHE_RING, sb, GLA_HEADS, GLA_DK, GLA_DV)
    return pl.pallas_call(
        kern,
        grid=(nb // rb, halves),
        in_specs=[
            pl.BlockSpec((N_MIX_SLABS, rb, D_MODEL), lambda i, j: (0, i, 0)),
            pl.BlockSpec((rb, GLA_KEY_WIDTH), lambda i, j: (i, 0)),
            any_spec,
            pl.BlockSpec((1, POOL_HIST, rb, D_MODEL), lambda i, j: (layer, 0, i, 0)),
            any_spec, any_spec,
            _const_spec((len(POOL_WINDOWS), POOL_GROUP_DIM, POOL_GROUP_DIM)),
            _const_spec((1, D_MODEL)),
            _const_spec((1, D_MODEL)),
        ] + [any_spec] * n_alias,
        out_specs=[
            pl.BlockSpec((N_BRANCH, rb, D_MODEL), lambda i, j: (0, i, 0)),
            pl.BlockSpec((1, sb, GLA_HEADS, GLA_DK, GLA_DV), lambda i, j: (layer, i * halves + j, 0, 0, 0)),
            pl.BlockSpec((1, POOL_HIST, rb, D_MODEL), lambda i, j: (layer, 0, i, 0)),
        ],
        out_shape=[
            jax.ShapeDtypeStruct((N_BRANCH, nb, D_MODEL), F32),
            jax.ShapeDtypeStruct(s0.shape, F32),
            jax.ShapeDtypeStruct(hist.shape, F32),
        ],
        scratch_shapes=[pltpu.VMEM((rb, D_MODEL), F32), pltpu.VMEM(state_ring, F32),
                        pltpu.VMEM(cache_ring, F32), pltpu.VMEM(cache_ring, F32),
                        pltpu.SemaphoreType.DMA((3, CACHE_RING))],
        input_output_aliases={n_in + a: 1 + a for a in range(n_alias)},
        compiler_params=_params(
            ("arbitrary", "arbitrary"),
            ((N_MIX_SLABS + N_BRANCH + 1, rb, D_MODEL), F32, 2),
            (state_ring, F32, 1), ((sb, GLA_HEADS, GLA_DK, GLA_DV), F32, 2),
            ((POOL_HIST, rb, D_MODEL), F32, 4),
            (cache_ring, F32, 2),
            ((len(POOL_WINDOWS), POOL_GROUP_DIM, POOL_GROUP_DIM), BF, 1)),
        name="sample_mix",
    )(p, la, s0, hist, ck, cv, pool_w, pool_scale, gla_gain, *(carry or ()))


def _merge_kernel(br_ref, p_ref, x_ref, wbr_ref, wo_ref, fg_ref, out_ref, *, final):
    merged = _merge_term(br_ref, p_ref, wbr_ref, 0)
    for n in range(1, N_BRANCH):
        merged = merged + _merge_term(br_ref, p_ref, wbr_ref, n)
    out_ref[...] = _merge_finish(merged, x_ref[...], wo_ref, fg_ref, final)


def _merge_out(br, p, x, w_branch, w_out, final_gain, layer, final):
    m_rows = x.shape[0]
    whole = lambda shape: pl.BlockSpec(shape, lambda i: (0,) * len(shape))
    return pl.pallas_call(
        functools.partial(_merge_kernel, final=final),
        grid=(1,),
        in_specs=[
            whole((N_BRANCH, m_rows, D_MODEL)),
            whole((N_SLABS, m_rows, D_MODEL)),
            whole((m_rows, D_MODEL)),
            _layer_spec((N_BRANCH, D_MODEL // 2, D_MODEL), layer),
            _layer_spec((D_MODEL // 2, D_MODEL), layer),
            _const_spec((1, D_MODEL)),
        ],
        out_specs=whole((m_rows, D_MODEL)),
        out_shape=jax.ShapeDtypeStruct((m_rows, D_MODEL), F32),
        compiler_params=_params(("arbitrary",),
                                ((N_BRANCH + N_SLABS + 2, m_rows, D_MODEL), F32, 2),
                                ((D_MODEL // 2, (N_BRANCH + 1) * D_MODEL), jnp.uint32, 1)),
        name="merge_out",
    )(br, p, x, w_branch, w_out, final_gain)


def kernel(x_prompt, x_sample, mem_prompt, cache_mem_k, cache_mem_v, state_gla, state_pool, w_in, w_a2, b_a, gla_gain, pool_w, pool_scale, w_mk, w_mv, w_branch, w_out, norm_gain, final_gain):
    batch, seq, _ = x_prompt.shape
    nb = x_sample.shape[0]
    n_mem = mem_prompt.shape[1]
    depth = w_in.shape[0]
    tm = PROMPT_TILE
    assert seq % tm == 0 and nb % SUBLANES == 0 and x_prompt.shape[2] == D_MODEL

    xp = x_prompt.reshape(batch * seq, D_MODEL)
    xs = x_sample.reshape(nb, D_MODEL)
    mem = mem_prompt.reshape(batch * n_mem, D_MODEL)
    fgain = final_gain.reshape(1, D_MODEL)

    mk, mv, mk_bf, mv_bf = _kvproj(mem, w_mk.astype(BF), w_mv.astype(BF), batch, n_mem)

    w_in_t = jnp.swapaxes(w_in, 1, 2)
    pool_rows = jnp.swapaxes(state_pool, 1, 2)
    w_proj = _pack_w_in(w_in_t)
    wb = _pack_matrices(w_branch.reshape(depth * N_BRANCH, D_MODEL, D_MODEL))
    wb = wb.reshape(depth, N_BRANCH, D_MODEL // 2, D_MODEL)
    wo = _pack_matrices(w_out)

    carry_p, carry_s = None, None
    for l in range(depth):
        final = l == depth - 1
        w_alr = w_in_t[l, ALR_START:ALR_START + GATE_RANK, :]
        wa2 = w_a2[l].astype(BF)
        ba = b_a[l].reshape(1, GLA_KEY_WIDTH)
        ngain = norm_gain[l].reshape(1, D_MODEL)
        ggain = gla_gain[l].reshape(1, D_MODEL)
        pscale = pool_scale[l].reshape(1, D_MODEL)
        pw = pool_w[l].astype(BF)

        xp, s_all, hist_all = _prompt_layer(xp, ngain, w_proj, w_alr, wa2, ba, mk_bf, mv_bf, pw, pscale,
                                            ggain, wb, wo, fgain, carry_p, l, depth, batch, seq, n_mem, tm, final)
        carry_p = (s_all, hist_all)

        ps, las = _inproj(xs, ngain, w_proj, w_alr, wa2, ba, l, nb, F32)
        brs, s_new, hist_new = _sample_mix(ps, las, state_gla, pool_rows, cache_mem_k, cache_mem_v,
                                           pw, pscale, ggain, carry_s, l)
        carry_s = (s_new, hist_new)
        xs = _merge_out(brs, ps, xs, wb, wo, fgain, l, final)

    return (xp.reshape(batch, seq, D_MODEL), xs.reshape(nb, 1, D_MODEL),
            mk, mv, carry_p[0], carry_p[1], carry_s[0], jnp.swapaxes(carry_s[1], 1, 2))
```

```python
import functools
import math

import jax
import jax.numpy as jnp
from jax import lax
from jax.experimental import pallas as pl
from jax.experimental.pallas import tpu as pltpu

D_MODEL = 1024
GLA_HEADS = 4
GLA_DK = 128
GLA_DV = 256
GLA_KEY_WIDTH = GLA_HEADS * GLA_DK
GATE_RANK = 16
GATE_TAU = 16.0
CHUNK = 128
POOL_WINDOWS = (2, 4, 8, 16)
POOL_GROUP_DIM = 256
POOL_HIST = 15
HIST_PAD = 16
XA_HEADS = 4
XA_HEAD_DIM = 256
N_BRANCH = 3
EPS = 1e-6
SUBLANES = 8
LANES = 128

SLAB_QK, SLAB_V, SLAB_GLA_G, SLAB_U, SLAB_POOL_G, SLAB_XQ, SLAB_XG, SLAB_MERGE = 0, 1, 2, 3, 4, 5, 6, 7
N_SLABS = 10
N_MIX_SLABS = 7
N_HEAD_SLABS = 3
ALR_START = N_HEAD_SLABS * D_MODEL

BF = jnp.bfloat16
F32 = jnp.float32
MIB = 1 << 20

PROMPT_TILE = 256
SAMPLE_BLOCK = 4
CACHE_RING = 3
VMEM_COMPILER_SCRATCH = 8 * MIB


def _dot(a, b):
    return jnp.dot(a, b, preferred_element_type=F32)


def _dot_nt(a, b):
    return lax.dot_general(a, b, (((1,), (1,)), ((), ())), preferred_element_type=F32)


def _dot_tn(a, b):
    return lax.dot_general(a, b, (((0,), (0,)), ((), ())), preferred_element_type=F32)


def _pack_rows(w):
    return pltpu.bitcast(w.astype(BF), jnp.uint32)


def _unpack_rows(w_words):
    return pltpu.bitcast(w_words, BF)


def _params(sem, *buffers):
    need = sum(math.prod(shape) * jnp.dtype(dtype).itemsize * copies for shape, dtype, copies in buffers)
    return pltpu.CompilerParams(dimension_semantics=sem, vmem_limit_bytes=need + VMEM_COMPILER_SCRATCH)


def _pack_w_in_kernel(a_ref, b_ref, o_ref):
    j = pl.program_id(1)

    @pl.when(j < N_HEAD_SLABS)
    def _():
        o_ref[0] = _pack_rows(a_ref[0].T)

    @pl.when(j >= N_HEAD_SLABS)
    def _():
        o_ref[0] = _pack_rows(jnp.concatenate([a_ref[0, GATE_RANK:, :], b_ref[0]], axis=0).T)


def _pack_w_in(w_in_t):
    depth = w_in_t.shape[0]
    return pl.pallas_call(
        _pack_w_in_kernel,
        grid=(depth, N_SLABS),
        in_specs=[
            pl.BlockSpec((1, D_MODEL, D_MODEL), lambda l, j: (l, j, 0)),
            pl.BlockSpec((1, GATE_RANK, D_MODEL), lambda l, j: (l, (j + 1) * (D_MODEL // GATE_RANK), 0)),
        ],
        out_specs=pl.BlockSpec((1, D_MODEL // 2, D_MODEL), lambda l, j: (l, 0, j)),
        out_shape=jax.ShapeDtypeStruct((depth, D_MODEL // 2, N_SLABS * D_MODEL), jnp.uint32),
        compiler_params=_params(("parallel", "parallel"),
                                ((D_MODEL, D_MODEL), F32, 2), ((GATE_RANK, D_MODEL), F32, 2),
                                ((D_MODEL // 2, D_MODEL), jnp.uint32, 2)),
        name="pack_w_in",
    )(w_in_t, w_in_t)


def _pack_matrices_kernel(a_ref, o_ref):
    o_ref[0] = _pack_rows(a_ref[0])


def _pack_matrices(w):
    return pl.pallas_call(
        _pack_matrices_kernel,
        grid=(w.shape[0],),
        in_specs=[pl.BlockSpec((1, D_MODEL, D_MODEL), lambda r: (r, 0, 0))],
        out_specs=pl.BlockSpec((1, D_MODEL // 2, D_MODEL), lambda r: (r, 0, 0)),
        out_shape=jax.ShapeDtypeStruct((w.shape[0], D_MODEL // 2, D_MODEL), jnp.uint32),
        compiler_params=_params(("parallel",), ((D_MODEL, D_MODEL), F32, 2), ((D_MODEL // 2, D_MODEL), jnp.uint32, 2)),
        name="pack_matrices",
    )(w)


def _silu(x):
    return x * jax.nn.sigmoid(x)


def _rms(x, gain):
    ms = jnp.mean(x * x, axis=-1, keepdims=True)
    return x * lax.rsqrt(ms + EPS) * gain


def _const_spec(shape):
    zeros = (0,) * len(shape)
    return pl.BlockSpec(shape, lambda *_: zeros, pipeline_mode=pl.Buffered(1))


def _layer_spec(shape, layer):
    index = (layer,) + (0,) * len(shape)
    return pl.BlockSpec((1,) + tuple(shape), lambda *_: index, pipeline_mode=pl.Buffered(1))


def _drop_alias_refs(body, n_in, n_alias):
    def kern(*refs):
        return body(*refs[:n_in], *refs[n_in + n_alias:])
    return kern


def _kvproj_kernel(m_ref, wk_ref, wv_ref, k_ref, v_ref, kb_ref, vb_ref):
    m = m_ref[...].astype(BF)
    k = _dot(m, wk_ref[0])
    v = _dot(m, wv_ref[0])
    for h in range(XA_HEADS):
        cs = slice(h * XA_HEAD_DIM, (h + 1) * XA_HEAD_DIM)
        k_ref[0, 0, :, h, :] = k[:, cs]
        v_ref[0, 0, :, h, :] = v[:, cs]
    kb_ref[0] = k.astype(BF)
    vb_ref[0] = v.astype(BF)


def _kvproj(mem, wk, wv, batch, n_mem):
    depth = wk.shape[0]
    w_spec = pl.BlockSpec((1, D_MODEL, D_MODEL), lambda l, b: (l, 0, 0))
    out5 = pl.BlockSpec((1, 1, n_mem, XA_HEADS, XA_HEAD_DIM), lambda l, b: (l, b, 0, 0, 0))
    out_bf = pl.BlockSpec((1, n_mem, D_MODEL), lambda l, b: (l, b, 0))
    return pl.pallas_call(
        _kvproj_kernel,
        grid=(depth, batch),
        in_specs=[pl.BlockSpec((n_mem, D_MODEL), lambda l, b: (b, 0)), w_spec, w_spec],
        out_specs=[out5, out5, out_bf, out_bf],
        out_shape=[jax.ShapeDtypeStruct((depth, batch, n_mem, XA_HEADS, XA_HEAD_DIM), F32)] * 2
        + [jax.ShapeDtypeStruct((depth, batch * n_mem, D_MODEL), BF)] * 2,
        compiler_params=_params(("parallel", "parallel"),
                                ((n_mem, D_MODEL), F32, 2), ((D_MODEL, D_MODEL), BF, 4),
                                ((n_mem, D_MODEL), F32, 4), ((n_mem, D_MODEL), BF, 4)),
        name="kvproj",
    )(mem, wk, wv)


def _inproj_slab(h, w_ref, p_out, j):
    w = w_ref[0, :, j * D_MODEL:(j + 1) * D_MODEL]
    p_out[j] = _dot(h, _unpack_rows(w)).astype(p_out.dtype)


def _inproj_gate_lowrank(h, walr_ref):
    return _dot_nt(h, walr_ref[...].astype(BF)).astype(BF)


def _inproj_gate(alr, wa2_ref, ba_ref, la_out):
    z = _dot(alr, wa2_ref[...]) + ba_ref[...]
    la_out[...] = (jnp.minimum(z, 0.0) - jnp.log(1.0 + jnp.exp(-jnp.abs(z)))) * (1.0 / GATE_TAU)


def _merge_term(br_ref, p, wbr_ref, n):
    return jax.nn.sigmoid(p[SLAB_MERGE + n].astype(F32)) * _dot(br_ref[n].astype(BF), _unpack_rows(wbr_ref[0, n]))


def _merge_finish(merged, x, wo_ref, fg_ref, final):
    x_new = x + _dot(merged.astype(BF), _unpack_rows(wo_ref[0]))
    return _rms(x_new, fg_ref[...]) if final else x_new


def _chunk_cumsum_matrix(tm):
    row = lax.broadcasted_iota(jnp.int32, (tm, tm), 0)
    col = lax.broadcasted_iota(jnp.int32, (tm, tm), 1)
    return (((row // CHUNK) == (col // CHUNK)) & (row >= col)).astype(BF)


def _window_matrices(tm):
    row = lax.broadcasted_iota(jnp.int32, (tm, tm), 0)
    col = lax.broadcasted_iota(jnp.int32, (tm, tm), 1)
    return jnp.stack([((row >= col) & (row - col < w)).astype(BF) for w in POOL_WINDOWS])


def _gla_cumdecay(cum_ref, la_ref):
    la = la_ref[...]
    la_hi = la.astype(BF)
    la_lo = (la - la_hi.astype(F32)).astype(BF)
    return _dot(cum_ref[...], la_hi) + _dot(cum_ref[...], la_lo)


def _gla_chunk(p, bcum, gg_ref, br_ref, st_ref, c, filler):
    crow = lax.broadcasted_iota(jnp.int32, (CHUNK, CHUNK), 0)
    ccol = lax.broadcasted_iota(jnp.int32, (CHUNK, CHUNK), 1)
    causal = crow >= ccol
    rows = slice(c * CHUNK, (c + 1) * CHUNK)
    heads = range(GLA_HEADS)
    q_dec, k_end, decay, att = [], [], [], []
    for h in heads:
        kc = slice(h * GLA_DK, (h + 1) * GLA_DK)
        kc2 = slice(GLA_KEY_WIDTH + h * GLA_DK, GLA_KEY_WIDTH + (h + 1) * GLA_DK)
        b = bcum[rows, kc]
        b_mid = b[CHUNK // 2 - 1:CHUNK // 2, :]
        b_last = b[CHUNK - 1:CHUNK, :]
        q = p[SLAB_QK, rows, kc].astype(F32) * (GLA_DK ** -0.5)
        k = p[SLAB_QK, rows, kc2].astype(F32)
        q_dec.append((q * jnp.exp(b)).astype(BF))
        q_mid = (q * jnp.exp(b - b_mid)).astype(BF)
        k_mid = (k * jnp.exp(b_mid - b)).astype(BF)
        k_end.append((k * jnp.exp(b_last - b)).astype(BF))
        decay.append(jnp.exp(b_last))
        att.append(_dot_nt(q_mid, k_mid))
    filler()
    o = []
    for h in heads:
        vc = slice(h * GLA_DV, (h + 1) * GLA_DV)
        a = jnp.where(causal, att[h], 0.0).astype(BF)
        o.append(_dot(a, p[SLAB_V, rows, vc]) + _dot_nt(q_dec[h], st_ref[h].astype(BF)))
    for h in heads:
        vc = slice(h * GLA_DV, (h + 1) * GLA_DV)
        st_ref[h] = decay[h] * st_ref[h] + _dot_tn(p[SLAB_V, rows, vc], k_end[h])
    for h in heads:
        vc = slice(h * GLA_DV, (h + 1) * GLA_DV)
        g = p[SLAB_GLA_G, rows, vc].astype(F32)
        br_ref[0, rows, vc] = (_rms(o[h], gg_ref[:, vc]) * _silu(g)).astype(BF)


def _pool_window_sums(p, win_ref):
    sums = []
    for g in range(len(POOL_WINDOWS)):
        cs = slice(g * POOL_GROUP_DIM, (g + 1) * POOL_GROUP_DIM)
        sums.append(_dot(win_ref[g], p[SLAB_U, :, cs]))
    return sums


def _pool_branch(p, sums, pw_ref, ps_ref, br_ref, ubuf_ref, t, tm):
    u = p[SLAB_U].astype(F32)
    ubuf_ref[HIST_PAD:2 * HIST_PAD, :] = u[0:HIST_PAD]
    pos = t * tm + lax.broadcasted_iota(jnp.int32, (tm, 1), 0)
    for g, w in enumerate(POOL_WINDOWS):
        cs = slice(g * POOL_GROUP_DIM, (g + 1) * POOL_GROUP_DIM)
        ug = u[:, cs]
        head = ug[0:HIST_PAD]
        for j in range(1, w):
            head = head + ubuf_ref[HIST_PAD - j:2 * HIST_PAD - j, cs]
        s = jnp.concatenate([head, sums[g][HIST_PAD:]], axis=0)
        cnt = jnp.minimum(w, pos + 1).astype(F32)
        diff = s / cnt - ug
        mixed = _dot(diff.astype(BF), pw_ref[g]) * ps_ref[:, cs]
        pg = p[SLAB_POOL_G, :, cs].astype(F32)
        br_ref[1, :, cs] = (mixed * _silu(pg)).astype(BF)
    ubuf_ref[0:HIST_PAD, :] = u[tm - HIST_PAD:tm]


def _xattn_probs(p, mk_ref):
    out = []
    for h in range(XA_HEADS):
        cs = slice(h * XA_HEAD_DIM, (h + 1) * XA_HEAD_DIM)
        s = _dot_nt(p[SLAB_XQ, :, cs], mk_ref[0, :, cs]) * (XA_HEAD_DIM ** -0.5)
        pr = jnp.exp(s - jnp.max(s, axis=-1, keepdims=True))
        out.append((pr.astype(BF), jnp.sum(pr, axis=-1, keepdims=True)))
    return out


def _xattn_branch(p, probs, mv_ref, br_ref):
    for h in range(XA_HEADS):
        cs = slice(h * XA_HEAD_DIM, (h + 1) * XA_HEAD_DIM)
        pr, denom = probs[h]
        o = _dot(pr, mv_ref[0, :, cs]) / denom
        xg = p[SLAB_XG, :, cs].astype(F32)
        br_ref[2, :, cs] = (o * _silu(xg)).astype(BF)


def _prompt_layer_kernel(xn_ref, xc_ref, g_ref, w_ref, walr_ref, wa2_ref, ba_ref, cum_ref, win_ref,
                         mk_ref, mv_ref, pw_ref, ps_ref, gg_ref, wbr_ref, wo_ref, fg_ref,
                         out_ref, s_out_ref, hist_out_ref,
                         h_scr, p_scr, la_scr, br_scr, st_ref, ubuf_ref, *, tm, nt, final):
    s = pl.program_id(0)
    t = jnp.maximum(s - 1, 0) % nt
    slot_w = s % 2
    slot_r = 1 - slot_w

    @pl.when(s == 0)
    def _():
        p_scr[1] = jnp.zeros(p_scr.shape[1:], p_scr.dtype)
        la_scr[1] = jnp.zeros(la_scr.shape[1:], la_scr.dtype)

    @pl.when(t == 0)
    def _():
        st_ref[...] = jnp.zeros_like(st_ref)
        ubuf_ref[0:HIST_PAD, :] = jnp.zeros((HIST_PAD, D_MODEL), F32)

    @pl.when(s == 0)
    def _():
        h_scr[0] = _rms(xc_ref[...], g_ref[...]).astype(BF)

    p_next = p_scr.at[slot_w]
    p = p_scr.at[slot_r]

    slabs = list(range(N_SLABS))

    def filler(n=1):
        for _ in range(n):
            _inproj_slab(h_scr[slot_w], w_ref, p_next, slabs.pop(0))

    alr = _inproj_gate_lowrank(h_scr[slot_w], walr_ref)
    bcum = _gla_cumdecay(cum_ref, la_scr.at[slot_r])
    filler()
    _inproj_gate(alr, wa2_ref, ba_ref, la_scr.at[slot_w])
    per_site = (N_SLABS - 2) // (2 * (tm // CHUNK))
    probs, sums = None, None
    for c in range(tm // CHUNK):
        _gla_chunk(p, bcum, gg_ref, br_scr, st_ref, c, functools.partial(filler, per_site))
        filler(per_site)
        if c == 0:
            probs = _xattn_probs(p, mk_ref)
            sums = _pool_window_sums(p, win_ref)
        elif c == 1:
            _xattn_branch(p, probs, mv_ref, br_scr)
            _pool_branch(p, sums, pw_ref, ps_ref, br_scr, ubuf_ref, t, tm)
    h_scr[slot_r] = _rms(xn_ref[...], g_ref[...]).astype(BF)
    merged = _merge_term(br_scr, p, wbr_ref, 2)
    merged = merged + _merge_term(br_scr, p, wbr_ref, 1)
    merged = merged + _merge_term(br_scr, p, wbr_ref, 0)
    filler()
    assert not slabs
    out_ref[...] = _merge_finish(merged, xc_ref[...], wo_ref, fg_ref, final)

    @pl.when((s > 0) & (t == nt - 1))
    def _():
        for h in range(GLA_HEADS):
            s_out_ref[0, 0, h] = st_ref[h].T
        hist_out_ref[0, 0] = ubuf_ref[1:HIST_PAD, :]


def _prompt_layer(x, ngain, w_proj, w_alr, wa2, ba, mk, mv, pool_w, pool_scale, gla_gain,
                  w_branch, w_out, fgain, carry, layer, depth, batch, seq, n_mem, tm, final):
    nt = seq // tm
    n_tiles = batch * nt
    assert tm // CHUNK == 2 and (N_SLABS - 2) % (2 * (tm // CHUNK)) == 0, "slab placement assumes two GLA chunks"
    n_alias = 0 if carry is None else len(carry)
    n_in = 17
    kern = _drop_alias_refs(functools.partial(_prompt_layer_kernel, tm=tm, nt=nt, final=final), n_in, n_alias)
    any_spec = pl.BlockSpec(memory_space=pl.ANY)
    scratch = [
        ((2, tm, D_MODEL), BF, 1),
        ((2, N_SLABS, tm, D_MODEL), BF, 1),
        ((2, tm, GLA_KEY_WIDTH), F32, 1),
        ((N_BRANCH, tm, D_MODEL), BF, 1),
        ((GLA_HEADS, GLA_DV, GLA_DK), F32, 1),
        ((2 * HIST_PAD, D_MODEL), F32, 1),
    ]

    def cur(s):
        return jnp.maximum(s - 1, 0)

    def seq_of(s):
        return cur(s) // nt

    return pl.pallas_call(
        kern,
        grid=(n_tiles + 1,),
        in_specs=[
            pl.BlockSpec((tm, D_MODEL), lambda s: (jnp.minimum(s + 1, n_tiles - 1), 0)),
            pl.BlockSpec((tm, D_MODEL), lambda s: (cur(s), 0)),
            _const_spec((1, D_MODEL)),
            _layer_spec((D_MODEL // 2, N_SLABS * D_MODEL), layer),
            _const_spec((GATE_RANK, D_MODEL)),
            _const_spec((GATE_RANK, GLA_KEY_WIDTH)),
            _const_spec((1, GLA_KEY_WIDTH)),
            _const_spec((tm, tm)),
            _const_spec((len(POOL_WINDOWS), tm, tm)),
            pl.BlockSpec((1, n_mem, D_MODEL), lambda s: (layer, seq_of(s), 0)),
            pl.BlockSpec((1, n_mem, D_MODEL), lambda s: (layer, seq_of(s), 0)),
            _const_spec((len(POOL_WINDOWS), POOL_GROUP_DIM, POOL_GROUP_DIM)),
            _const_spec((1, D_MODEL)),
            _const_spec((1, D_MODEL)),
            _layer_spec((N_BRANCH, D_MODEL // 2, D_MODEL), layer),
            _layer_spec((D_MODEL // 2, D_MODEL), layer),
            _const_spec((1, D_MODEL)),
        ] + [any_spec] * n_alias,
        out_specs=[
            pl.BlockSpec((tm, D_MODEL), lambda s: (cur(s), 0)),
            pl.BlockSpec((1, 1, GLA_HEADS, GLA_DK, GLA_DV), lambda s: (layer, seq_of(s), 0, 0, 0)),
            pl.BlockSpec((1, 1, POOL_HIST, D_MODEL), lambda s: (layer, seq_of(s), 0, 0)),
        ],
        out_shape=[
            jax.ShapeDtypeStruct((n_tiles * tm, D_MODEL), F32),
            jax.ShapeDtypeStruct((depth, batch, GLA_HEADS, GLA_DK, GLA_DV), F32),
            jax.ShapeDtypeStruct((depth, batch, POOL_HIST, D_MODEL), F32),
        ],
        scratch_shapes=[pltpu.VMEM(shape, dtype) for shape, dtype, _ in scratch],
        input_output_aliases={n_in + a: 1 + a for a in range(n_alias)},
        compiler_params=_params(
            ("arbitrary",), *scratch,
            ((tm, D_MODEL), F32, 6),
            ((D_MODEL // 2, (N_SLABS + N_BRANCH + 1) * D_MODEL), jnp.uint32, 1),
            ((n_mem, D_MODEL), BF, 4),
            ((tm, tm), BF, 1 + len(POOL_WINDOWS)),
            ((len(POOL_WINDOWS), POOL_GROUP_DIM, POOL_GROUP_DIM), BF, 1),
            ((GLA_HEADS, GLA_DK, GLA_DV), F32, 2)),
        name="prompt_layer",
    )(x, x, ngain, w_proj, w_alr, wa2, ba, _chunk_cumsum_matrix(tm), _window_matrices(tm),
      mk, mv, pool_w, pool_scale, gla_gain,
      w_branch, w_out, fgain, *(carry or ()))


def _inproj_kernel(x_ref, g_ref, w_ref, walr_ref, wa2_ref, ba_ref, p_ref, la_ref):
    h = _rms(x_ref[...], g_ref[...]).astype(BF)
    alr = _inproj_gate_lowrank(h, walr_ref)
    for j in range(N_SLABS):
        _inproj_slab(h, w_ref, p_ref, j)
    _inproj_gate(alr, wa2_ref, ba_ref, la_ref)


def _inproj(x, gain, w_proj, w_alr, w_a2, b_a, layer, tm, out_dtype):
    m_rows = x.shape[0]
    return pl.pallas_call(
        _inproj_kernel,
        grid=(m_rows // tm,),
        in_specs=[
            pl.BlockSpec((tm, D_MODEL), lambda i: (i, 0)),
            _const_spec((1, D_MODEL)),
            _layer_spec((D_MODEL // 2, N_SLABS * D_MODEL), layer),
            _const_spec((GATE_RANK, D_MODEL)),
            _const_spec((GATE_RANK, GLA_KEY_WIDTH)),
            _const_spec((1, GLA_KEY_WIDTH)),
        ],
        out_specs=[
            pl.BlockSpec((N_SLABS, tm, D_MODEL), lambda i: (0, i, 0)),
            pl.BlockSpec((tm, GLA_KEY_WIDTH), lambda i: (i, 0)),
        ],
        out_shape=[
            jax.ShapeDtypeStruct((N_SLABS, m_rows, D_MODEL), out_dtype),
            jax.ShapeDtypeStruct((m_rows, GLA_KEY_WIDTH), F32),
        ],
        compiler_params=_params(("parallel",),
                                ((tm, D_MODEL), F32, 2), ((D_MODEL // 2, N_SLABS * D_MODEL), jnp.uint32, 1),
                                ((N_SLABS, tm, D_MODEL), out_dtype, 2), ((tm, GLA_KEY_WIDTH), F32, 2)),
        name="inproj",
    )(x, gain, w_proj, w_alr, w_a2, b_a)


def _sample_mix_kernel(p_ref, la_ref, s0_hbm, hist_ref, ck_hbm, cv_hbm, pw_ref, ps_ref, gg_ref,
                       br_ref, s_out_ref, hist_out_ref, diff_ref, sbuf, kbuf, vbuf, ring_sem, *, sb, layer, n_steps):
    step = pl.program_id(0) * pl.num_programs(1) + pl.program_id(1)
    streams = ((s0_hbm, sbuf), (ck_hbm, kbuf), (cv_hbm, vbuf))

    def ring_copy(which, t):
        src, buf = streams[which]
        slot = t % CACHE_RING
        return pltpu.make_async_copy(src.at[layer, pl.ds(t * sb, sb)], buf.at[slot], ring_sem.at[which, slot])

    @pl.when(step == 0)
    def _():
        for t in range(min(CACHE_RING - 1, n_steps)):
            for which in range(len(streams)):
                ring_copy(which, t).start()

    @pl.when(step + CACHE_RING - 1 < n_steps)
    def _():
        for which in range(len(streams)):
            ring_copy(which, step + CACHE_RING - 1).start()

    for which in range(len(streams)):
        ring_copy(which, step).wait()
    s0_ref = sbuf.at[step % CACHE_RING]
    ck_ref = kbuf.at[step % CACHE_RING]
    cv_ref = vbuf.at[step % CACHE_RING]

    r0 = pl.program_id(1) * sb
    erow = lax.broadcasted_iota(jnp.int32, (GLA_DK, GLA_DK), 0)
    ecol = lax.broadcasted_iota(jnp.int32, (GLA_DK, GLA_DK), 1)
    eye = erow == ecol

    def to_col(x):
        return jnp.sum(jnp.where(eye, jnp.broadcast_to(x, (GLA_DK, GLA_DK)), 0.0), axis=1, keepdims=True)

    for i in range(sb):
        r = pl.ds(r0 + i, 1)
        la = la_ref[r, :]
        qk = p_ref[SLAB_QK, r, :]
        vv = p_ref[SLAB_V, r, :]
        gla_g = p_ref[SLAB_GLA_G, r, :]
        for h in range(GLA_HEADS):
            kc = slice(h * GLA_DK, (h + 1) * GLA_DK)
            kc2 = slice(GLA_KEY_WIDTH + h * GLA_DK, GLA_KEY_WIDTH + (h + 1) * GLA_DK)
            vc = slice(h * GLA_DV, (h + 1) * GLA_DV)
            a_col = to_col(jnp.exp(la[:, kc]))
            q_col = to_col(qk[:, kc] * (GLA_DK ** -0.5))
            k_col = to_col(qk[:, kc2])
            s_new = a_col * s0_ref[i, h] + k_col * vv[:, vc]
            s_out_ref[0, i, h] = s_new
            o = jnp.sum(q_col * s_new, axis=0, keepdims=True)
            br_ref[0, r, vc] = _rms(o, gg_ref[:, vc]) * _silu(gla_g[:, vc])

        u = p_ref[SLAB_U, r, :]
        for g, w in enumerate(POOL_WINDOWS):
            cs = slice(g * POOL_GROUP_DIM, (g + 1) * POOL_GROUP_DIM)
            past = jnp.sum(hist_ref[0, POOL_HIST - (w - 1):POOL_HIST, r, cs], axis=0)
            diff_ref[r, cs] = (u[:, cs] + past) / float(w) - u[:, cs]
        hist_out_ref[0, 0:POOL_HIST - 1, r, :] = hist_ref[0, 1:POOL_HIST, r, :]
        hist_out_ref[0, POOL_HIST - 1, r, :] = u

        xq = p_ref[SLAB_XQ, r, :]
        xg = p_ref[SLAB_XG, r, :]
        half_cols = [slice(h * XA_HEAD_DIM + j * LANES, h * XA_HEAD_DIM + (j + 1) * LANES)
                     for j in range(XA_HEAD_DIM // LANES) for h in range(XA_HEADS)]
        xq_rows = jnp.concatenate([xq[:, cs] for cs in half_cols], axis=0)
        n_mem = ck_ref.shape[1]
        prod = (ck_ref[i] * xq_rows[None]).reshape(n_mem * SUBLANES, LANES).astype(BF)
        part = _dot(prod, jnp.ones((LANES, LANES), BF)).reshape(n_mem, SUBLANES, LANES)
        s = (part + pltpu.roll(part, XA_HEADS, axis=1)) * (XA_HEAD_DIM ** -0.5)
        p = jnp.exp(s - jnp.max(s, axis=0, keepdims=True))
        o = jnp.sum(p * cv_ref[i], axis=0) / jnp.sum(p, axis=0)
        halves = XA_HEAD_DIM // LANES
        o_row = jnp.concatenate([o[j * XA_HEADS + h:j * XA_HEADS + h + 1, :]
                                 for h in range(XA_HEADS) for j in range(halves)], axis=1)
        br_ref[2, r, :] = o_row * _silu(xg)

    @pl.when(pl.program_id(1) == pl.num_programs(1) - 1)
    def _():
        for g in range(len(POOL_WINDOWS)):
            cs = slice(g * POOL_GROUP_DIM, (g + 1) * POOL_GROUP_DIM)
            mixed = _dot(diff_ref[:, cs].astype(BF), pw_ref[g]) * ps_ref[:, cs]
            br_ref[1, :, cs] = mixed * _silu(p_ref[SLAB_POOL_G, :, cs])


def _cache_rows_view(c):
    depth, nb, n_mem = c.shape[:3]
    halves = XA_HEAD_DIM // LANES
    c = c.reshape(depth, nb, n_mem, XA_HEADS, halves, LANES)
    return c.transpose(0, 1, 2, 4, 3, 5).reshape(depth, nb, n_mem, halves * XA_HEADS, LANES)


def _sample_mix(p, la, s0, hist, ck, cv, pool_w, pool_scale, gla_gain, carry, layer, sb=SAMPLE_BLOCK):
    nb = s0.shape[1]
    n_mem = ck.shape[2]
    ck, cv = _cache_rows_view(ck), _cache_rows_view(cv)
    rb = SUBLANES
    halves = rb // sb
    n_alias = 0 if carry is None else len(carry)
    n_in = 9
    kern = _drop_alias_refs(functools.partial(_sample_mix_kernel, sb=sb, layer=layer, n_steps=nb // sb),
                            n_in, n_alias)
    any_spec = pl.BlockSpec(memory_space=pl.ANY)
    cache_ring = (CACHE_RING, sb, n_mem, SUBLANES, LANES)
    state_ring = (CACHE_RING, sb, GLA_HEADS, GLA_DK, GLA_DV)
    return pl.pallas_call(
        kern,
        grid=(nb // rb, halves),
        in_specs=[
            pl.BlockSpec((N_MIX_SLABS, rb, D_MODEL), lambda i, j: (0, i, 0)),
            pl.BlockSpec((rb, GLA_KEY_WIDTH), lambda i, j: (i, 0)),
            any_spec,
            pl.BlockSpec((1, POOL_HIST, rb, D_MODEL), lambda i, j: (layer, 0, i, 0)),
            any_spec, any_spec,
            _const_spec((len(POOL_WINDOWS), POOL_GROUP_DIM, POOL_GROUP_DIM)),
            _const_spec((1, D_MODEL)),
            _const_spec((1, D_MODEL)),
        ] + [any_spec] * n_alias,
        out_specs=[
            pl.BlockSpec((N_BRANCH, rb, D_MODEL), lambda i, j: (0, i, 0)),
            pl.BlockSpec((1, sb, GLA_HEADS, GLA_DK, GLA_DV), lambda i, j: (layer, i * halves + j, 0, 0, 0)),
            pl.BlockSpec((1, POOL_HIST, rb, D_MODEL), lambda i, j: (layer, 0, i, 0)),
        ],
        out_shape=[
            jax.ShapeDtypeStruct((N_BRANCH, nb, D_MODEL), F32),
            jax.ShapeDtypeStruct(s0.shape, F32),
            jax.ShapeDtypeStruct(hist.shape, F32),
        ],
        scratch_shapes=[pltpu.VMEM((rb, D_MODEL), F32), pltpu.VMEM(state_ring, F32),
                        pltpu.VMEM(cache_ring, F32), pltpu.VMEM(cache_ring, F32),
                        pltpu.SemaphoreType.DMA((3, CACHE_RING))],
        input_output_aliases={n_in + a: 1 + a for a in range(n_alias)},
        compiler_params=_params(
            ("arbitrary", "arbitrary"),
            ((N_MIX_SLABS + N_BRANCH + 1, rb, D_MODEL), F32, 2),
            (state_ring, F32, 1), ((sb, GLA_HEADS, GLA_DK, GLA_DV), F32, 2),
            ((POOL_HIST, rb, D_MODEL), F32, 4),
            (cache_ring, F32, 2),
            ((len(POOL_WINDOWS), POOL_GROUP_DIM, POOL_GROUP_DIM), BF, 1)),
        name="sample_mix",
    )(p, la, s0, hist, ck, cv, pool_w, pool_scale, gla_gain, *(carry or ()))


def _merge_kernel(br_ref, p_ref, x_ref, wbr_ref, wo_ref, fg_ref, out_ref, *, final):
    merged = _merge_term(br_ref, p_ref, wbr_ref, 0)
    for n in range(1, N_BRANCH):
        merged = merged + _merge_term(br_ref, p_ref, wbr_ref, n)
    out_ref[...] = _merge_finish(merged, x_ref[...], wo_ref, fg_ref, final)


def _merge_out(br, p, x, w_branch, w_out, final_gain, layer, final):
    m_rows = x.shape[0]
    whole = lambda shape: pl.BlockSpec(shape, lambda i: (0,) * len(shape))
    return pl.pallas_call(
        functools.partial(_merge_kernel, final=final),
        grid=(1,),
        in_specs=[
            whole((N_BRANCH, m_rows, D_MODEL)),
            whole((N_SLABS, m_rows, D_MODEL)),
            whole((m_rows, D_MODEL)),
            _layer_spec((N_BRANCH, D_MODEL // 2, D_MODEL), layer),
            _layer_spec((D_MODEL // 2, D_MODEL), layer),
            _const_spec((1, D_MODEL)),
        ],
        out_specs=whole((m_rows, D_MODEL)),
        out_shape=jax.ShapeDtypeStruct((m_rows, D_MODEL), F32),
        compiler_params=_params(("arbitrary",),
                                ((N_BRANCH + N_SLABS + 2, m_rows, D_MODEL), F32, 2),
                                ((D_MODEL // 2, (N_BRANCH + 1) * D_MODEL), jnp.uint32, 1)),
        name="merge_out",
    )(br, p, x, w_branch, w_out, final_gain)


def kernel(x_prompt, x_sample, mem_prompt, cache_mem_k, cache_mem_v, state_gla, state_pool, w_in, w_a2, b_a, gla_gain, pool_w, pool_scale, w_mk, w_mv, w_branch, w_out, norm_gain, final_gain):
    batch, seq, _ = x_prompt.shape
    nb = x_sample.shape[0]
    n_mem = mem_prompt.shape[1]
    depth = w_in.shape[0]
    tm = PROMPT_TILE
    assert seq % tm == 0 and nb % SUBLANES == 0 and x_prompt.shape[2] == D_MODEL

    xp = x_prompt.reshape(batch * seq, D_MODEL)
    xs = x_sample.reshape(nb, D_MODEL)
    mem = mem_prompt.reshape(batch * n_mem, D_MODEL)
    fgain = final_gain.reshape(1, D_MODEL)

    mk, mv, mk_bf, mv_bf = _kvproj(mem, w_mk.astype(BF), w_mv.astype(BF), batch, n_mem)

    w_in_t = jnp.swapaxes(w_in, 1, 2)
    pool_rows = jnp.swapaxes(state_pool, 1, 2)
    w_proj = _pack_w_in(w_in_t)
    wb = _pack_matrices(w_branch.reshape(depth * N_BRANCH, D_MODEL, D_MODEL))
    wb = wb.reshape(depth, N_BRANCH, D_MODEL // 2, D_MODEL)
    wo = _pack_matrices(w_out)

    carry_p, carry_s = None, None
    for l in range(depth):
        final = l == depth - 1
        w_alr = w_in_t[l, ALR_START:ALR_START + GATE_RANK, :]
        wa2 = w_a2[l].astype(BF)
        ba = b_a[l].reshape(1, GLA_KEY_WIDTH)
        ngain = norm_gain[l].reshape(1, D_MODEL)
        ggain = gla_gain[l].reshape(1, D_MODEL)
        pscale = pool_scale[l].reshape(1, D_MODEL)
        pw = pool_w[l].astype(BF)

        xp, s_all, hist_all = _prompt_layer(xp, ngain, w_proj, w_alr, wa2, ba, mk_bf, mv_bf, pw, pscale,
                                            ggain, wb, wo, fgain, carry_p, l, depth, batch, seq, n_mem, tm, final)
        carry_p = (s_all, hist_all)

        ps, las = _inproj(xs, ngain, w_proj, w_alr, wa2, ba, l, nb, F32)
        brs, s_new, hist_new = _sample_mix(ps, las, state_gla, pool_rows, cache_mem_k, cache_mem_v,
                                           pw, pscale, ggain, carry_s, l)
        carry_s = (s_new, hist_new)
        xs = _merge_out(brs, ps, xs, wb, wo, fgain, l, final)

    return (xp.reshape(batch, seq, D_MODEL), xs.reshape(nb, 1, D_MODEL),
            mk, mv, carry_p[0], carry_p[1], carry_s[0], jnp.swapaxes(carry_s[1], 1, 2))
```

```python
import functools
import math

import jax
import jax.numpy as jnp
from jax import lax
from jax.experimental import pallas as pl
from jax.experimental.pallas import tpu as pltpu

D_MODEL = 1024
GLA_HEADS = 4
GLA_DK = 128
GLA_DV = 256
GLA_KEY_WIDTH = GLA_HEADS * GLA_DK
GATE_RANK = 16
GATE_TAU = 16.0
CHUNK = 128
POOL_WINDOWS = (2, 4, 8, 16)
POOL_GROUP_DIM = 256
POOL_HIST = 15
HIST_PAD = 16
XA_HEADS = 4
XA_HEAD_DIM = 256
N_BRANCH = 3
EPS = 1e-6
SUBLANES = 8
LANES = 128

SLAB_QK, SLAB_V, SLAB_GLA_G, SLAB_U, SLAB_POOL_G, SLAB_XQ, SLAB_XG, SLAB_MERGE = 0, 1, 2, 3, 4, 5, 6, 7
N_SLABS = 10
N_MIX_SLABS = 7
N_HEAD_SLABS = 3
ALR_START = N_HEAD_SLABS * D_MODEL

BF = jnp.bfloat16
F32 = jnp.float32
MIB = 1 << 20

PROMPT_TILE = 256
SAMPLE_BLOCK = 4
CACHE_RING = 3
VMEM_COMPILER_SCRATCH = 8 * MIB


def _dot(a, b):
    return jnp.dot(a, b, preferred_element_type=F32)


def _dot_nt(a, b):
    return lax.dot_general(a, b, (((1,), (1,)), ((), ())), preferred_element_type=F32)


def _dot_tn(a, b):
    return lax.dot_general(a, b, (((0,), (0,)), ((), ())), preferred_element_type=F32)


def _pack_rows(w):
    return pltpu.bitcast(w.astype(BF), jnp.uint32)


def _unpack_rows(w_words):
    return pltpu.bitcast(w_words, BF)


def _params(sem, *buffers):
    need = sum(math.prod(shape) * jnp.dtype(dtype).itemsize * copies for shape, dtype, copies in buffers)
    return pltpu.CompilerParams(dimension_semantics=sem, vmem_limit_bytes=need + VMEM_COMPILER_SCRATCH)


def _pack_w_in_kernel(a_ref, b_ref, o_ref):
    j = pl.program_id(1)

    @pl.when(j < N_HEAD_SLABS)
    def _():
        o_ref[0] = _pack_rows(a_ref[0].T)

    @pl.when(j >= N_HEAD_SLABS)
    def _():
        o_ref[0] = _pack_rows(jnp.concatenate([a_ref[0, GATE_RANK:, :], b_ref[0]], axis=0).T)


def _pack_w_in(w_in_t):
    depth = w_in_t.shape[0]
    return pl.pallas_call(
        _pack_w_in_kernel,
        grid=(depth, N_SLABS),
        in_specs=[
            pl.BlockSpec((1, D_MODEL, D_MODEL), lambda l, j: (l, j, 0)),
            pl.BlockSpec((1, GATE_RANK, D_MODEL), lambda l, j: (l, (j + 1) * (D_MODEL // GATE_RANK), 0)),
        ],
        out_specs=pl.BlockSpec((1, D_MODEL // 2, D_MODEL), lambda l, j: (l, 0, j)),
        out_shape=jax.ShapeDtypeStruct((depth, D_MODEL // 2, N_SLABS * D_MODEL), jnp.uint32),
        compiler_params=_params(("parallel", "parallel"),
                                ((D_MODEL, D_MODEL), F32, 2), ((GATE_RANK, D_MODEL), F32, 2),
                                ((D_MODEL // 2, D_MODEL), jnp.uint32, 2)),
        name="pack_w_in",
    )(w_in_t, w_in_t)


def _pack_matrices_kernel(a_ref, o_ref):
    o_ref[0] = _pack_rows(a_ref[0])


def _pack_matrices(w):
    return pl.pallas_call(
        _pack_matrices_kernel,
        grid=(w.shape[0],),
        in_specs=[pl.BlockSpec((1, D_MODEL, D_MODEL), lambda r: (r, 0, 0))],
        out_specs=pl.BlockSpec((1, D_MODEL // 2, D_MODEL), lambda r: (r, 0, 0)),
        out_shape=jax.ShapeDtypeStruct((w.shape[0], D_MODEL // 2, D_MODEL), jnp.uint32),
        compiler_params=_params(("parallel",), ((D_MODEL, D_MODEL), F32, 2), ((D_MODEL // 2, D_MODEL), jnp.uint32, 2)),
        name="pack_matrices",
    )(w)


def _silu(x):
    return x * jax.nn.sigmoid(x)


def _rms(x, gain):
    ms = jnp.mean(x * x, axis=-1, keepdims=True)
    return x * lax.rsqrt(ms + EPS) * gain


def _const_spec(shape):
    zeros = (0,) * len(shape)
    return pl.BlockSpec(shape, lambda *_: zeros, pipeline_mode=pl.Buffered(1))


def _layer_spec(shape, layer):
    index = (layer,) + (0,) * len(shape)
    return pl.BlockSpec((1,) + tuple(shape), lambda *_: index, pipeline_mode=pl.Buffered(1))


def _drop_alias_refs(body, n_in, n_alias):
    def kern(*refs):
        return body(*refs[:n_in], *refs[n_in + n_alias:])
    return kern


def _kvproj_kernel(m_ref, wk_ref, wv_ref, k_ref, v_ref, kb_ref, vb_ref):
    m = m_ref[...].astype(BF)
    k = _dot(m, wk_ref[0])
    v = _dot(m, wv_ref[0])
    for h in range(XA_HEADS):
        cs = slice(h * XA_HEAD_DIM, (h + 1) * XA_HEAD_DIM)
        k_ref[0, 0, :, h, :] = k[:, cs]
        v_ref[0, 0, :, h, :] = v[:, cs]
    kb_ref[0] = k.astype(BF)
    vb_ref[0] = v.astype(BF)


def _kvproj(mem, wk, wv, batch, n_mem):
    depth = wk.shape[0]
    w_spec = pl.BlockSpec((1, D_MODEL, D_MODEL), lambda l, b: (l, 0, 0))
    out5 = pl.BlockSpec((1, 1, n_mem, XA_HEADS, XA_HEAD_DIM), lambda l, b: (l, b, 0, 0, 0))
    out_bf = pl.BlockSpec((1, n_mem, D_MODEL), lambda l, b: (l, b, 0))
    return pl.pallas_call(
        _kvproj_kernel,
        grid=(depth, batch),
        in_specs=[pl.BlockSpec((n_mem, D_MODEL), lambda l, b: (b, 0)), w_spec, w_spec],
        out_specs=[out5, out5, out_bf, out_bf],
        out_shape=[jax.ShapeDtypeStruct((depth, batch, n_mem, XA_HEADS, XA_HEAD_DIM), F32)] * 2
        + [jax.ShapeDtypeStruct((depth, batch * n_mem, D_MODEL), BF)] * 2,
        compiler_params=_params(("parallel", "parallel"),
                                ((n_mem, D_MODEL), F32, 2), ((D_MODEL, D_MODEL), BF, 4),
                                ((n_mem, D_MODEL), F32, 4), ((n_mem, D_MODEL), BF, 4)),
        name="kvproj",
    )(mem, wk, wv)


def _inproj_slab(h, w_ref, p_out, j):
    w = w_ref[0, :, j * D_MODEL:(j + 1) * D_MODEL]
    p_out[j] = _dot(h, _unpack_rows(w)).astype(p_out.dtype)


def _inproj_gate_lowrank(h, walr_ref):
    return _dot_nt(h, walr_ref[...].astype(BF)).astype(BF)


def _inproj_gate(alr, wa2_ref, ba_ref):
    z = _dot(alr, wa2_ref[...]) + ba_ref[...]
    return (jnp.minimum(z, 0.0) - jnp.log(1.0 + jnp.exp(-jnp.abs(z)))) * (1.0 / GATE_TAU)


def _merge_term(br_ref, p, wbr_ref, n):
    return jax.nn.sigmoid(p[SLAB_MERGE + n].astype(F32)) * _dot(br_ref[n].astype(BF), _unpack_rows(wbr_ref[0, n]))


def _merge_finish(merged, x, wo_ref, fg_ref, final):
    x_new = x + _dot(merged.astype(BF), _unpack_rows(wo_ref[0]))
    return _rms(x_new, fg_ref[...]) if final else x_new


def _chunk_cumsum_matrix(tm):
    row = lax.broadcasted_iota(jnp.int32, (tm, tm), 0)
    col = lax.broadcasted_iota(jnp.int32, (tm, tm), 1)
    return (((row // CHUNK) == (col // CHUNK)) & (row >= col)).astype(BF)


def _window_matrices(tm):
    row = lax.broadcasted_iota(jnp.int32, (tm, tm), 0)
    col = lax.broadcasted_iota(jnp.int32, (tm, tm), 1)
    return jnp.stack([((row >= col) & (row - col < w)).astype(BF) for w in POOL_WINDOWS])


def _gla_cumdecay(cum_ref, la):
    la_hi = la.astype(BF)
    la_lo = (la - la_hi.astype(F32)).astype(BF)
    return _dot(cum_ref[...], la_hi) + _dot(cum_ref[...], la_lo)


def _gla_chunk(p, bcum, gg_ref, br_ref, st_ref, c, filler):
    crow = lax.broadcasted_iota(jnp.int32, (CHUNK, CHUNK), 0)
    ccol = lax.broadcasted_iota(jnp.int32, (CHUNK, CHUNK), 1)
    causal = crow >= ccol
    rows = slice(c * CHUNK, (c + 1) * CHUNK)
    heads = range(GLA_HEADS)
    q_dec, k_end, decay, att = [], [], [], []
    for h in heads:
        kc = slice(h * GLA_DK, (h + 1) * GLA_DK)
        kc2 = slice(GLA_KEY_WIDTH + h * GLA_DK, GLA_KEY_WIDTH + (h + 1) * GLA_DK)
        b = bcum[rows, kc]
        b_mid = b[CHUNK // 2 - 1:CHUNK // 2, :]
        b_last = b[CHUNK - 1:CHUNK, :]
        q = p[SLAB_QK, rows, kc].astype(F32) * (GLA_DK ** -0.5)
        k = p[SLAB_QK, rows, kc2].astype(F32)
        q_dec.append((q * jnp.exp(b)).astype(BF))
        q_mid = (q * jnp.exp(b - b_mid)).astype(BF)
        k_mid = (k * jnp.exp(b_mid - b)).astype(BF)
        k_end.append((k * jnp.exp(b_last - b)).astype(BF))
        decay.append(jnp.exp(b_last))
        att.append(_dot_nt(q_mid, k_mid))
    filler()
    o = []
    for h in heads:
        vc = slice(h * GLA_DV, (h + 1) * GLA_DV)
        a = jnp.where(causal, att[h], 0.0).astype(BF)
        o.append(_dot(a, p[SLAB_V, rows, vc]) + _dot_nt(q_dec[h], st_ref[h].astype(BF)))
    for h in heads:
        vc = slice(h * GLA_DV, (h + 1) * GLA_DV)
        st_ref[h] = decay[h] * st_ref[h] + _dot_tn(p[SLAB_V, rows, vc], k_end[h])
    for h in heads:
        vc = slice(h * GLA_DV, (h + 1) * GLA_DV)
        g = p[SLAB_GLA_G, rows, vc].astype(F32)
        br_ref[0, rows, vc] = (_rms(o[h], gg_ref[:, vc]) * _silu(g)).astype(BF)


def _pool_window_sums(p, win_ref):
    sums = []
    for g in range(len(POOL_WINDOWS)):
        cs = slice(g * POOL_GROUP_DIM, (g + 1) * POOL_GROUP_DIM)
        sums.append(_dot(win_ref[g], p[SLAB_U, :, cs]))
    return sums


def _pool_branch(p, sums, pw_ref, ps_ref, br_ref, ubuf_ref, t, tm):
    u = p[SLAB_U].astype(F32)
    ubuf_ref[HIST_PAD:2 * HIST_PAD, :] = u[0:HIST_PAD]
    pos = t * tm + lax.broadcasted_iota(jnp.int32, (tm, 1), 0)
    for g, w in enumerate(POOL_WINDOWS):
        cs = slice(g * POOL_GROUP_DIM, (g + 1) * POOL_GROUP_DIM)
        ug = u[:, cs]
        head = ug[0:HIST_PAD]
        for j in range(1, w):
            head = head + ubuf_ref[HIST_PAD - j:2 * HIST_PAD - j, cs]
        s = jnp.concatenate([head, sums[g][HIST_PAD:]], axis=0)
        cnt = jnp.minimum(w, pos + 1).astype(F32)
        diff = s / cnt - ug
        mixed = _dot(diff.astype(BF), pw_ref[g]) * ps_ref[:, cs]
        pg = p[SLAB_POOL_G, :, cs].astype(F32)
        br_ref[1, :, cs] = (mixed * _silu(pg)).astype(BF)
    ubuf_ref[0:HIST_PAD, :] = u[tm - HIST_PAD:tm]


def _xattn_probs(p, mk_ref):
    out = []
    for h in range(XA_HEADS):
        cs = slice(h * XA_HEAD_DIM, (h + 1) * XA_HEAD_DIM)
        s = _dot_nt(p[SLAB_XQ, :, cs], mk_ref[0, :, cs]) * (XA_HEAD_DIM ** -0.5)
        pr = jnp.exp(s - jnp.max(s, axis=-1, keepdims=True))
        out.append((pr.astype(BF), jnp.sum(pr, axis=-1, keepdims=True)))
    return out


def _xattn_branch(p, probs, mv_ref, br_ref):
    for h in range(XA_HEADS):
        cs = slice(h * XA_HEAD_DIM, (h + 1) * XA_HEAD_DIM)
        pr, denom = probs[h]
        o = _dot(pr, mv_ref[0, :, cs]) / denom
        xg = p[SLAB_XG, :, cs].astype(F32)
        br_ref[2, :, cs] = (o * _silu(xg)).astype(BF)


def _prompt_layer_kernel(xn_ref, xc_ref, g_ref, w_ref, walr_ref, wa2_ref, ba_ref, cum_ref, win_ref,
                         mk_ref, mv_ref, pw_ref, ps_ref, gg_ref, wbr_ref, wo_ref, fg_ref,
                         out_ref, s_out_ref, hist_out_ref,
                         h_scr, p_scr, la_scr, br_scr, st_ref, ubuf_ref, *, tm, nt, final):
    s = pl.program_id(0)
    t = jnp.maximum(s - 1, 0) % nt
    slot_w = s % 2
    slot_r = 1 - slot_w

    @pl.when(s == 0)
    def _():
        p_scr[1] = jnp.zeros(p_scr.shape[1:], p_scr.dtype)
        la_scr[1] = jnp.zeros(la_scr.shape[1:], la_scr.dtype)

    @pl.when(t == 0)
    def _():
        st_ref[...] = jnp.zeros_like(st_ref)
        ubuf_ref[0:HIST_PAD, :] = jnp.zeros((HIST_PAD, D_MODEL), F32)

    @pl.when(s == 0)
    def _():
        h_scr[0] = _rms(xc_ref[...], g_ref[...]).astype(BF)

    p_next = p_scr.at[slot_w]
    p = p_scr.at[slot_r]

    slabs = list(range(N_SLABS))

    def filler(n=1):
        for _ in range(n):
            _inproj_slab(h_scr[slot_w], w_ref, p_next, slabs.pop(0))

    alr = _inproj_gate_lowrank(h_scr[slot_w], walr_ref)
    bcum = la_scr[slot_r]
    filler()
    la_next = _inproj_gate(alr, wa2_ref, ba_ref)
    per_site = (N_SLABS - 2) // (2 * (tm // CHUNK))
    probs, sums = None, None
    for c in range(tm // CHUNK):
        _gla_chunk(p, bcum, gg_ref, br_scr, st_ref, c, functools.partial(filler, per_site))
        filler(per_site)
        if c == 0:
            la_scr[slot_w] = _gla_cumdecay(cum_ref, la_next)
            probs = _xattn_probs(p, mk_ref)
            sums = _pool_window_sums(p, win_ref)
        elif c == 1:
            _xattn_branch(p, probs, mv_ref, br_scr)
            _pool_branch(p, sums, pw_ref, ps_ref, br_scr, ubuf_ref, t, tm)
    h_scr[slot_r] = _rms(xn_ref[...], g_ref[...]).astype(BF)
    merged = _merge_term(br_scr, p, wbr_ref, 2)
    merged = merged + _merge_term(br_scr, p, wbr_ref, 1)
    merged = merged + _merge_term(br_scr, p, wbr_ref, 0)
    filler()
    assert not slabs
    out_ref[...] = _merge_finish(merged, xc_ref[...], wo_ref, fg_ref, final)

    @pl.when((s > 0) & (t == nt - 1))
    def _():
        for h in range(GLA_HEADS):
            s_out_ref[0, 0, h] = st_ref[h].T
        hist_out_ref[0, 0] = ubuf_ref[1:HIST_PAD, :]


def _prompt_layer(x, ngain, w_proj, w_alr, wa2, ba, mk, mv, pool_w, pool_scale, gla_gain,
                  w_branch, w_out, fgain, carry, layer, depth, batch, seq, n_mem, tm, final):
    nt = seq // tm
    n_tiles = batch * nt
    assert tm // CHUNK == 2 and (N_SLABS - 2) % (2 * (tm // CHUNK)) == 0, "slab placement assumes two GLA chunks"
    n_alias = 0 if carry is None else len(carry)
    n_in = 17
    kern = _drop_alias_refs(functools.partial(_prompt_layer_kernel, tm=tm, nt=nt, final=final), n_in, n_alias)
    any_spec = pl.BlockSpec(memory_space=pl.ANY)
    scratch = [
        ((2, tm, D_MODEL), BF, 1),
        ((2, N_SLABS, tm, D_MODEL), BF, 1),
        ((2, tm, GLA_KEY_WIDTH), F32, 1),
        ((N_BRANCH, tm, D_MODEL), BF, 1),
        ((GLA_HEADS, GLA_DV, GLA_DK), F32, 1),
        ((2 * HIST_PAD, D_MODEL), F32, 1),
    ]

    def cur(s):
        return jnp.maximum(s - 1, 0)

    def seq_of(s):
        return cur(s) // nt

    return pl.pallas_call(
        kern,
        grid=(n_tiles + 1,),
        in_specs=[
            pl.BlockSpec((tm, D_MODEL), lambda s: (jnp.minimum(s + 1, n_tiles - 1), 0)),
            pl.BlockSpec((tm, D_MODEL), lambda s: (cur(s), 0)),
            _const_spec((1, D_MODEL)),
            _layer_spec((D_MODEL // 2, N_SLABS * D_MODEL), layer),
            _const_spec((GATE_RANK, D_MODEL)),
            _const_spec((GATE_RANK, GLA_KEY_WIDTH)),
            _const_spec((1, GLA_KEY_WIDTH)),
            _const_spec((tm, tm)),
            _const_spec((len(POOL_WINDOWS), tm, tm)),
            pl.BlockSpec((1, n_mem, D_MODEL), lambda s: (layer, seq_of(s), 0)),
            pl.BlockSpec((1, n_mem, D_MODEL), lambda s: (layer, seq_of(s), 0)),
            _const_spec((len(POOL_WINDOWS), POOL_GROUP_DIM, POOL_GROUP_DIM)),
            _const_spec((1, D_MODEL)),
            _const_spec((1, D_MODEL)),
            _layer_spec((N_BRANCH, D_MODEL // 2, D_MODEL), layer),
            _layer_spec((D_MODEL // 2, D_MODEL), layer),
            _const_spec((1, D_MODEL)),
        ] + [any_spec] * n_alias,
        out_specs=[
            pl.BlockSpec((tm, D_MODEL), lambda s: (cur(s), 0)),
            pl.BlockSpec((1, 1, GLA_HEADS, GLA_DK, GLA_DV), lambda s: (layer, seq_of(s), 0, 0, 0)),
            pl.BlockSpec((1, 1, POOL_HIST, D_MODEL), lambda s: (layer, seq_of(s), 0, 0)),
        ],
        out_shape=[
            jax.ShapeDtypeStruct((n_tiles * tm, D_MODEL), F32),
            jax.ShapeDtypeStruct((depth, batch, GLA_HEADS, GLA_DK, GLA_DV), F32),
            jax.ShapeDtypeStruct((depth, batch, POOL_HIST, D_MODEL), F32),
        ],
        scratch_shapes=[pltpu.VMEM(shape, dtype) for shape, dtype, _ in scratch],
        input_output_aliases={n_in + a: 1 + a for a in range(n_alias)},
        compiler_params=_params(
            ("arbitrary",), *scratch,
            ((tm, D_MODEL), F32, 6),
            ((D_MODEL // 2, (N_SLABS + N_BRANCH + 1) * D_MODEL), jnp.uint32, 1),
            ((n_mem, D_MODEL), BF, 4),
            ((tm, tm), BF, 1 + len(POOL_WINDOWS)),
            ((len(POOL_WINDOWS), POOL_GROUP_DIM, POOL_GROUP_DIM), BF, 1),
            ((GLA_HEADS, GLA_DK, GLA_DV), F32, 2)),
        name="prompt_layer",
    )(x, x, ngain, w_proj, w_alr, wa2, ba, _chunk_cumsum_matrix(tm), _window_matrices(tm),
      mk, mv, pool_w, pool_scale, gla_gain,
      w_branch, w_out, fgain, *(carry or ()))


def _inproj_kernel(x_ref, g_ref, w_ref, walr_ref, wa2_ref, ba_ref, p_ref, la_ref):
    h = _rms(x_ref[...], g_ref[...]).astype(BF)
    alr = _inproj_gate_lowrank(h, walr_ref)
    for j in range(N_SLABS):
        _inproj_slab(h, w_ref, p_ref, j)
    la_ref[...] = _inproj_gate(alr, wa2_ref, ba_ref)


def _inproj(x, gain, w_proj, w_alr, w_a2, b_a, layer, tm, out_dtype):
    m_rows = x.shape[0]
    return pl.pallas_call(
        _inproj_kernel,
        grid=(m_rows // tm,),
        in_specs=[
            pl.BlockSpec((tm, D_MODEL), lambda i: (i, 0)),
            _const_spec((1, D_MODEL)),
            _layer_spec((D_MODEL // 2, N_SLABS * D_MODEL), layer),
            _const_spec((GATE_RANK, D_MODEL)),
            _const_spec((GATE_RANK, GLA_KEY_WIDTH)),
            _const_spec((1, GLA_KEY_WIDTH)),
        ],
        out_specs=[
            pl.BlockSpec((N_SLABS, tm, D_MODEL), lambda i: (0, i, 0)),
            pl.BlockSpec((tm, GLA_KEY_WIDTH), lambda i: (i, 0)),
        ],
        out_shape=[
            jax.ShapeDtypeStruct((N_SLABS, m_rows, D_MODEL), out_dtype),
            jax.ShapeDtypeStruct((m_rows, GLA_KEY_WIDTH), F32),
        ],
        compiler_params=_params(("parallel",),
                                ((tm, D_MODEL), F32, 2), ((D_MODEL // 2, N_SLABS * D_MODEL), jnp.uint32, 1),
                                ((N_SLABS, tm, D_MODEL), out_dtype, 2), ((tm, GLA_KEY_WIDTH), F32, 2)),
        name="inproj",
    )(x, gain, w_proj, w_alr, w_a2, b_a)


def _sample_mix_kernel(p_ref, la_ref, s0_ref, hist_ref, ck_hbm, cv_hbm, pw_ref, ps_ref, gg_ref,
                       br_ref, s_out_ref, hist_out_ref, diff_ref, kbuf, vbuf, cache_sem, *, sb, layer):
    n_steps = pl.num_programs(0) * pl.num_programs(1)
    step = pl.program_id(0) * pl.num_programs(1) + pl.program_id(1)

    def cache_copy(which, t):
        src, buf = ((ck_hbm, kbuf), (cv_hbm, vbuf))[which]
        slot = t % CACHE_RING
        return pltpu.make_async_copy(src.at[layer, pl.ds(t * sb, sb)], buf.at[slot], cache_sem.at[which, slot])

    @pl.when(step == 0)
    def _():
        for t in range(CACHE_RING - 1):
            cache_copy(0, t).start()
            cache_copy(1, t).start()

    @pl.when(step + CACHE_RING - 1 < n_steps)
    def _():
        cache_copy(0, step + CACHE_RING - 1).start()
        cache_copy(1, step + CACHE_RING - 1).start()

    cache_copy(0, step).wait()
    cache_copy(1, step).wait()
    ck_ref = kbuf.at[step % CACHE_RING]
    cv_ref = vbuf.at[step % CACHE_RING]

    r0 = pl.program_id(1) * sb
    erow = lax.broadcasted_iota(jnp.int32, (GLA_DK, GLA_DK), 0)
    ecol = lax.broadcasted_iota(jnp.int32, (GLA_DK, GLA_DK), 1)
    eye = erow == ecol

    def to_col(x):
        return jnp.sum(jnp.where(eye, jnp.broadcast_to(x, (GLA_DK, GLA_DK)), 0.0), axis=1, keepdims=True)

    for i in range(sb):
        r = pl.ds(r0 + i, 1)
        la = la_ref[r, :]
        qk = p_ref[SLAB_QK, r, :]
        vv = p_ref[SLAB_V, r, :]
        gla_g = p_ref[SLAB_GLA_G, r, :]
        for h in range(GLA_HEADS):
            kc = slice(h * GLA_DK, (h + 1) * GLA_DK)
            kc2 = slice(GLA_KEY_WIDTH + h * GLA_DK, GLA_KEY_WIDTH + (h + 1) * GLA_DK)
            vc = slice(h * GLA_DV, (h + 1) * GLA_DV)
            a_col = to_col(jnp.exp(la[:, kc]))
            q_col = to_col(qk[:, kc] * (GLA_DK ** -0.5))
            k_col = to_col(qk[:, kc2])
            s_new = a_col * s0_ref[0, i, h] + k_col * vv[:, vc]
            s_out_ref[0, i, h] = s_new
            o = jnp.sum(q_col * s_new, axis=0, keepdims=True)
            br_ref[0, r, vc] = _rms(o, gg_ref[:, vc]) * _silu(gla_g[:, vc])

        u = p_ref[SLAB_U, r, :]
        for g, w in enumerate(POOL_WINDOWS):
            cs = slice(g * POOL_GROUP_DIM, (g + 1) * POOL_GROUP_DIM)
            past = jnp.sum(hist_ref[0, POOL_HIST - (w - 1):POOL_HIST, r, cs], axis=0)
            diff_ref[r, cs] = (u[:, cs] + past) / float(w) - u[:, cs]
        hist_out_ref[0, 0:POOL_HIST - 1, r, :] = hist_ref[0, 1:POOL_HIST, r, :]
        hist_out_ref[0, POOL_HIST - 1, r, :] = u

        xq = p_ref[SLAB_XQ, r, :]
        xg = p_ref[SLAB_XG, r, :]
        half_cols = [slice(h * XA_HEAD_DIM + j * LANES, h * XA_HEAD_DIM + (j + 1) * LANES)
                     for j in range(XA_HEAD_DIM // LANES) for h in range(XA_HEADS)]
        xq_rows = jnp.concatenate([xq[:, cs] for cs in half_cols], axis=0)
        n_mem = ck_ref.shape[1]
        prod = (ck_ref[i] * xq_rows[None]).reshape(n_mem * SUBLANES, LANES).astype(BF)
        part = _dot(prod, jnp.ones((LANES, LANES), BF)).reshape(n_mem, SUBLANES, LANES)
        s = (part + pltpu.roll(part, XA_HEADS, axis=1)) * (XA_HEAD_DIM ** -0.5)
        p = jnp.exp(s - jnp.max(s, axis=0, keepdims=True))
        o = jnp.sum(p * cv_ref[i], axis=0) / jnp.sum(p, axis=0)
        halves = XA_HEAD_DIM // LANES
        o_row = jnp.concatenate([o[j * XA_HEADS + h:j * XA_HEADS + h + 1, :]
                                 for h in range(XA_HEADS) for j in range(halves)], axis=1)
        br_ref[2, r, :] = o_row * _silu(xg)

    @pl.when(pl.program_id(1) == pl.num_programs(1) - 1)
    def _():
        for g in range(len(POOL_WINDOWS)):
            cs = slice(g * POOL_GROUP_DIM, (g + 1) * POOL_GROUP_DIM)
            mixed = _dot(diff_ref[:, cs].astype(BF), pw_ref[g]) * ps_ref[:, cs]
            br_ref[1, :, cs] = mixed * _silu(p_ref[SLAB_POOL_G, :, cs])


def _cache_rows_view(c):
    depth, nb, n_mem = c.shape[:3]
    halves = XA_HEAD_DIM // LANES
    c = c.reshape(depth, nb, n_mem, XA_HEADS, halves, LANES)
    return c.transpose(0, 1, 2, 4, 3, 5).reshape(depth, nb, n_mem, halves * XA_HEADS, LANES)


def _sample_mix(p, la, s0, hist, ck, cv, pool_w, pool_scale, gla_gain, carry, layer, sb=SAMPLE_BLOCK):
    nb = s0.shape[1]
    n_mem = ck.shape[2]
    ck, cv = _cache_rows_view(ck), _cache_rows_view(cv)
    rb = SUBLANES
    halves = rb // sb
    n_alias = 0 if carry is None else len(carry)
    n_in = 9
    kern = _drop_alias_refs(functools.partial(_sample_mix_kernel, sb=sb, layer=layer), n_in, n_alias)
    any_spec = pl.BlockSpec(memory_space=pl.ANY)
    cache_ring = (CACHE_RING, sb, n_mem, SUBLANES, LANES)
    return pl.pallas_call(
        kern,
        grid=(nb // rb, halves),
        in_specs=[
            pl.BlockSpec((N_MIX_SLABS, rb, D_MODEL), lambda i, j: (0, i, 0)),
            pl.BlockSpec((rb, GLA_KEY_WIDTH), lambda i, j: (i, 0)),
            pl.BlockSpec((1, sb, GLA_HEADS, GLA_DK, GLA_DV), lambda i, j: (layer, i * halves + j, 0, 0, 0)),
            pl.BlockSpec((1, POOL_HIST, rb, D_MODEL), lambda i, j: (layer, 0, i, 0)),
            any_spec, any_spec,
            _const_spec((len(POOL_WINDOWS), POOL_GROUP_DIM, POOL_GROUP_DIM)),
            _const_spec((1, D_MODEL)),
            _const_spec((1, D_MODEL)),
        ] + [any_spec] * n_alias,
        out_specs=[
            pl.BlockSpec((N_BRANCH, rb, D_MODEL), lambda i, j: (0, i, 0)),
            pl.BlockSpec((1, sb, GLA_HEADS, GLA_DK, GLA_DV), lambda i, j: (layer, i * halves + j, 0, 0, 0)),
            pl.BlockSpec((1, POOL_HIST, rb, D_MODEL), lambda i, j: (layer, 0, i, 0)),
        ],
        out_shape=[
            jax.ShapeDtypeStruct((N_BRANCH, nb, D_MODEL), F32),
            jax.ShapeDtypeStruct(s0.shape, F32),
            jax.ShapeDtypeStruct(hist.shape, F32),
        ],
        scratch_shapes=[pltpu.VMEM((rb, D_MODEL), F32), pltpu.VMEM(cache_ring, F32), pltpu.VMEM(cache_ring, F32),
                        pltpu.SemaphoreType.DMA((2, CACHE_RING))],
        input_output_aliases={n_in + a: 1 + a for a in range(n_alias)},
        compiler_params=_params(
            ("arbitrary", "arbitrary"),
            ((N_MIX_SLABS + N_BRANCH + 1, rb, D_MODEL), F32, 2),
            ((sb, GLA_HEADS, GLA_DK, GLA_DV), F32, 4),
            ((POOL_HIST, rb, D_MODEL), F32, 4),
            (cache_ring, F32, 2),
            ((len(POOL_WINDOWS), POOL_GROUP_DIM, POOL_GROUP_DIM), BF, 1)),
        name="sample_mix",
    )(p, la, s0, hist, ck, cv, pool_w, pool_scale, gla_gain, *(carry or ()))


def _merge_kernel(br_ref, p_ref, x_ref, wbr_ref, wo_ref, fg_ref, out_ref, *, final):
    merged = _merge_term(br_ref, p_ref, wbr_ref, 0)
    for n in range(1, N_BRANCH):
        merged = merged + _merge_term(br_ref, p_ref, wbr_ref, n)
    out_ref[...] = _merge_finish(merged, x_ref[...], wo_ref, fg_ref, final)


def _merge_out(br, p, x, w_branch, w_out, final_gain, layer, final):
    m_rows = x.shape[0]
    whole = lambda shape: pl.BlockSpec(shape, lambda i: (0,) * len(shape))
    return pl.pallas_call(
        functools.partial(_merge_kernel, final=final),
        grid=(1,),
        in_specs=[
            whole((N_BRANCH, m_rows, D_MODEL)),
            whole((N_SLABS, m_rows, D_MODEL)),
            whole((m_rows, D_MODEL)),
            _layer_spec((N_BRANCH, D_MODEL // 2, D_MODEL), layer),
            _layer_spec((D_MODEL // 2, D_MODEL), layer),
            _const_spec((1, D_MODEL)),
        ],
        out_specs=whole((m_rows, D_MODEL)),
        out_shape=jax.ShapeDtypeStruct((m_rows, D_MODEL), F32),
        compiler_params=_params(("arbitrary",),
                                ((N_BRANCH + N_SLABS + 2, m_rows, D_MODEL), F32, 2),
                                ((D_MODEL // 2, (N_BRANCH + 1) * D_MODEL), jnp.uint32, 1)),
        name="merge_out",
    )(br, p, x, w_branch, w_out, final_gain)


def kernel(x_prompt, x_sample, mem_prompt, cache_mem_k, cache_mem_v, state_gla, state_pool, w_in, w_a2, b_a, gla_gain, pool_w, pool_scale, w_mk, w_mv, w_branch, w_out, norm_gain, final_gain):
    batch, seq, _ = x_prompt.shape
    nb = x_sample.shape[0]
    n_mem = mem_prompt.shape[1]
    depth = w_in.shape[0]
    tm = PROMPT_TILE
    assert seq % tm == 0 and nb % SUBLANES == 0 and x_prompt.shape[2] == D_MODEL

    xp = x_prompt.reshape(batch * seq, D_MODEL)
    xs = x_sample.reshape(nb, D_MODEL)
    mem = mem_prompt.reshape(batch * n_mem, D_MODEL)
    fgain = final_gain.reshape(1, D_MODEL)

    mk, mv, mk_bf, mv_bf = _kvproj(mem, w_mk.astype(BF), w_mv.astype(BF), batch, n_mem)

    w_in_t = jnp.swapaxes(w_in, 1, 2)
    pool_rows = jnp.swapaxes(state_pool, 1, 2)
    w_proj = _pack_w_in(w_in_t)
    wb = _pack_matrices(w_branch.reshape(depth * N_BRANCH, D_MODEL, D_MODEL))
    wb = wb.reshape(depth, N_BRANCH, D_MODEL // 2, D_MODEL)
    wo = _pack_matrices(w_out)

    carry_p, carry_s = None, None
    for l in range(depth):
        final = l == depth - 1
        w_alr = w_in_t[l, ALR_START:ALR_START + GATE_RANK, :]
        wa2 = w_a2[l].astype(BF)
        ba = b_a[l].reshape(1, GLA_KEY_WIDTH)
        ngain = norm_gain[l].reshape(1, D_MODEL)
        ggain = gla_gain[l].reshape(1, D_MODEL)
        pscale = pool_scale[l].reshape(1, D_MODEL)
        pw = pool_w[l].astype(BF)

        xp, s_all, hist_all = _prompt_layer(xp, ngain, w_proj, w_alr, wa2, ba, mk_bf, mv_bf, pw, pscale,
                                            ggain, wb, wo, fgain, carry_p, l, depth, batch, seq, n_mem, tm, final)
        carry_p = (s_all, hist_all)

        ps, las = _inproj(xs, ngain, w_proj, w_alr, wa2, ba, l, nb, F32)
        brs, s_new, hist_new = _sample_mix(ps, las, state_gla, pool_rows, cache_mem_k, cache_mem_v,
                                           pw, pscale, ggain, carry_s, l)
        carry_s = (s_new, hist_new)
        xs = _merge_out(brs, ps, xs, wb, wo, fgain, l, final)

    return (xp.reshape(batch, seq, D_MODEL), xs.reshape(nb, 1, D_MODEL),
            mk, mv, carry_p[0], carry_p[1], carry_s[0], jnp.swapaxes(carry_s[1], 1, 2))
```

```python
import functools
import math

import jax
import jax.numpy as jnp
from jax import lax
from jax.experimental import pallas as pl
from jax.experimental.pallas import tpu as pltpu

D_MODEL = 1024
GLA_HEADS = 4
GLA_DK = 128
GLA_DV = 256
GLA_KEY_WIDTH = GLA_HEADS * GLA_DK
GATE_RANK = 16
GATE_TAU = 16.0
CHUNK = 256
SCORE_BLOCK = 128
POOL_WINDOWS = (2, 4, 8, 16)
POOL_GROUP_DIM = 256
POOL_HIST = 15
HIST_PAD = 16
XA_HEADS = 4
XA_HEAD_DIM = 256
N_BRANCH = 3
EPS = 1e-6
SUBLANES = 8
LANES = 128

SLAB_QK, SLAB_V, SLAB_GLA_G, SLAB_U, SLAB_POOL_G, SLAB_XQ, SLAB_XG, SLAB_MERGE = 0, 1, 2, 3, 4, 5, 6, 7
N_SLABS = 10
N_MIX_SLABS = 7
N_HEAD_SLABS = 3
ALR_START = N_HEAD_SLABS * D_MODEL

BF = jnp.bfloat16
F32 = jnp.float32
MIB = 1 << 20

PROMPT_TILE = 256
SAMPLE_BLOCK = 4
CACHE_RING = 3
VMEM_COMPILER_SCRATCH = 8 * MIB


def _dot(a, b):
    return jnp.dot(a, b, preferred_element_type=F32)


def _dot_nt(a, b):
    return lax.dot_general(a, b, (((1,), (1,)), ((), ())), preferred_element_type=F32)


def _dot_tn(a, b):
    return lax.dot_general(a, b, (((0,), (0,)), ((), ())), preferred_element_type=F32)


def _pack_rows(w):
    return pltpu.bitcast(w.astype(BF), jnp.uint32)


def _unpack_rows(w_words):
    return pltpu.bitcast(w_words, BF)


def _params(sem, *buffers):
    need = sum(math.prod(shape) * jnp.dtype(dtype).itemsize * copies for shape, dtype, copies in buffers)
    return pltpu.CompilerParams(dimension_semantics=sem, vmem_limit_bytes=need + VMEM_COMPILER_SCRATCH)


def _pack_w_in_kernel(a_ref, b_ref, o_ref):
    j = pl.program_id(1)

    @pl.when(j < N_HEAD_SLABS)
    def _():
        o_ref[0] = _pack_rows(a_ref[0].T)

    @pl.when(j >= N_HEAD_SLABS)
    def _():
        o_ref[0] = _pack_rows(jnp.concatenate([a_ref[0, GATE_RANK:, :], b_ref[0]], axis=0).T)


def _pack_w_in(w_in_t):
    depth = w_in_t.shape[0]
    return pl.pallas_call(
        _pack_w_in_kernel,
        grid=(depth, N_SLABS),
        in_specs=[
            pl.BlockSpec((1, D_MODEL, D_MODEL), lambda l, j: (l, j, 0)),
            pl.BlockSpec((1, GATE_RANK, D_MODEL), lambda l, j: (l, (j + 1) * (D_MODEL // GATE_RANK), 0)),
        ],
        out_specs=pl.BlockSpec((1, D_MODEL // 2, D_MODEL), lambda l, j: (l, 0, j)),
        out_shape=jax.ShapeDtypeStruct((depth, D_MODEL // 2, N_SLABS * D_MODEL), jnp.uint32),
        compiler_params=_params(("parallel", "parallel"),
                                ((D_MODEL, D_MODEL), F32, 2), ((GATE_RANK, D_MODEL), F32, 2),
                                ((D_MODEL // 2, D_MODEL), jnp.uint32, 2)),
        name="pack_w_in",
    )(w_in_t, w_in_t)


def _pack_matrices_kernel(a_ref, o_ref):
    o_ref[0] = _pack_rows(a_ref[0])


def _pack_matrices(w):
    return pl.pallas_call(
        _pack_matrices_kernel,
        grid=(w.shape[0],),
        in_specs=[pl.BlockSpec((1, D_MODEL, D_MODEL), lambda r: (r, 0, 0))],
        out_specs=pl.BlockSpec((1, D_MODEL // 2, D_MODEL), lambda r: (r, 0, 0)),
        out_shape=jax.ShapeDtypeStruct((w.shape[0], D_MODEL // 2, D_MODEL), jnp.uint32),
        compiler_params=_params(("parallel",), ((D_MODEL, D_MODEL), F32, 2), ((D_MODEL // 2, D_MODEL), jnp.uint32, 2)),
        name="pack_matrices",
    )(w)


def _silu(x):
    return x * jax.nn.sigmoid(x)


def _rms(x, gain):
    ms = jnp.mean(x * x, axis=-1, keepdims=True)
    return x * lax.rsqrt(ms + EPS) * gain


def _const_spec(shape):
    zeros = (0,) * len(shape)
    return pl.BlockSpec(shape, lambda *_: zeros, pipeline_mode=pl.Buffered(1))


def _layer_spec(shape, layer):
    index = (layer,) + (0,) * len(shape)
    return pl.BlockSpec((1,) + tuple(shape), lambda *_: index, pipeline_mode=pl.Buffered(1))


def _drop_alias_refs(body, n_in, n_alias):
    def kern(*refs):
        return body(*refs[:n_in], *refs[n_in + n_alias:])
    return kern


def _kvproj_kernel(m_ref, wk_ref, wv_ref, k_ref, v_ref, kb_ref, vb_ref):
    m = m_ref[...].astype(BF)
    k = _dot(m, wk_ref[0])
    v = _dot(m, wv_ref[0])
    for h in range(XA_HEADS):
        cs = slice(h * XA_HEAD_DIM, (h + 1) * XA_HEAD_DIM)
        k_ref[0, 0, :, h, :] = k[:, cs]
        v_ref[0, 0, :, h, :] = v[:, cs]
    kb_ref[0] = k.astype(BF)
    vb_ref[0] = v.astype(BF)


def _kvproj(mem, wk, wv, batch, n_mem):
    depth = wk.shape[0]
    w_spec = pl.BlockSpec((1, D_MODEL, D_MODEL), lambda l, b: (l, 0, 0))
    out5 = pl.BlockSpec((1, 1, n_mem, XA_HEADS, XA_HEAD_DIM), lambda l, b: (l, b, 0, 0, 0))
    out_bf = pl.BlockSpec((1, n_mem, D_MODEL), lambda l, b: (l, b, 0))
    return pl.pallas_call(
        _kvproj_kernel,
        grid=(depth, batch),
        in_specs=[pl.BlockSpec((n_mem, D_MODEL), lambda l, b: (b, 0)), w_spec, w_spec],
        out_specs=[out5, out5, out_bf, out_bf],
        out_shape=[jax.ShapeDtypeStruct((depth, batch, n_mem, XA_HEADS, XA_HEAD_DIM), F32)] * 2
        + [jax.ShapeDtypeStruct((depth, batch * n_mem, D_MODEL), BF)] * 2,
        compiler_params=_params(("parallel", "parallel"),
                                ((n_mem, D_MODEL), F32, 2), ((D_MODEL, D_MODEL), BF, 4),
                                ((n_mem, D_MODEL), F32, 4), ((n_mem, D_MODEL), BF, 4)),
        name="kvproj",
    )(mem, wk, wv)


def _inproj_slab(h, w_ref, p_out, j):
    w = w_ref[0, :, j * D_MODEL:(j + 1) * D_MODEL]
    p_out[j] = _dot(h, _unpack_rows(w)).astype(p_out.dtype)


def _inproj_gate_lowrank(h, walr_ref):
    return _dot_nt(h, walr_ref[...].astype(BF)).astype(BF)


def _inproj_gate(alr, wa2_ref, ba_ref, la_out):
    z = _dot(alr, wa2_ref[...]) + ba_ref[...]
    la_out[...] = (jnp.minimum(z, 0.0) - jnp.log(1.0 + jnp.exp(-jnp.abs(z)))) * (1.0 / GATE_TAU)


def _merge_term(br_ref, p, wbr_ref, n):
    return jax.nn.sigmoid(p[SLAB_MERGE + n].astype(F32)) * _dot(br_ref[n].astype(BF), _unpack_rows(wbr_ref[0, n]))


def _merge_finish(merged, x, wo_ref, fg_ref, final):
    x_new = x + _dot(merged.astype(BF), _unpack_rows(wo_ref[0]))
    return _rms(x_new, fg_ref[...]) if final else x_new


def _chunk_cumsum_matrix(tm):
    row = lax.broadcasted_iota(jnp.int32, (tm, tm), 0)
    col = lax.broadcasted_iota(jnp.int32, (tm, tm), 1)
    return (((row // CHUNK) == (col // CHUNK)) & (row >= col)).astype(BF)


def _window_matrices(tm):
    row = lax.broadcasted_iota(jnp.int32, (tm, tm), 0)
    col = lax.broadcasted_iota(jnp.int32, (tm, tm), 1)
    return jnp.stack([((row >= col) & (row - col < w)).astype(BF) for w in POOL_WINDOWS])


def _gla_cumdecay(cum_ref, la_ref):
    la = la_ref[...]
    la_hi = la.astype(BF)
    la_lo = (la - la_hi.astype(F32)).astype(BF)
    return _dot(cum_ref[...], la_hi) + _dot(cum_ref[...], la_lo)


def _gla_chunk(p, bcum, gg_ref, br_ref, st_ref, c, filler):
    assert CHUNK == 2 * SCORE_BLOCK
    crow = lax.broadcasted_iota(jnp.int32, (SCORE_BLOCK, SCORE_BLOCK), 0)
    ccol = lax.broadcasted_iota(jnp.int32, (SCORE_BLOCK, SCORE_BLOCK), 1)
    causal = crow >= ccol
    rows = slice(c * CHUNK, (c + 1) * CHUNK)
    top, bot = slice(0, SCORE_BLOCK), slice(SCORE_BLOCK, CHUNK)
    heads = range(GLA_HEADS)
    q_dec, k_end, decay, att = [], [], [], []
    for h in heads:
        kc = slice(h * GLA_DK, (h + 1) * GLA_DK)
        kc2 = slice(GLA_KEY_WIDTH + h * GLA_DK, GLA_KEY_WIDTH + (h + 1) * GLA_DK)
        b = bcum[rows, kc]
        b_last = b[CHUNK - 1:CHUNK, :]
        q = p[SLAB_QK, rows, kc].astype(F32) * (GLA_DK ** -0.5)
        k = p[SLAB_QK, rows, kc2].astype(F32)
        q_dec.append((q * jnp.exp(b)).astype(BF))
        k_end.append((k * jnp.exp(b_last - b)).astype(BF))
        decay.append(jnp.exp(b_last))
        diag = []
        for blk in (top, bot):
            b_mid = b[blk][SCORE_BLOCK // 2 - 1:SCORE_BLOCK // 2, :]
            q_mid = (q[blk] * jnp.exp(b[blk] - b_mid)).astype(BF)
            k_mid = (k[blk] * jnp.exp(b_mid - b[blk])).astype(BF)
            diag.append(_dot_nt(q_mid, k_mid))
        b_edge = b[SCORE_BLOCK - 1:SCORE_BLOCK, :]
        q_low = (q[bot] * jnp.exp(b[bot] - b_edge)).astype(BF)
        k_low = (k[top] * jnp.exp(b_edge - b[top])).astype(BF)
        att.append((diag[0], _dot_nt(q_low, k_low), diag[1]))
    filler()
    o = []
    for h in heads:
        vc = slice(h * GLA_DV, (h + 1) * GLA_DV)
        d0, low, d1 = att[h]
        a = jnp.concatenate([
            jnp.concatenate([jnp.where(causal, d0, 0.0), jnp.zeros_like(d0)], axis=1),
            jnp.concatenate([low, jnp.where(causal, d1, 0.0)], axis=1)], axis=0).astype(BF)
        o.append(_dot(a, p[SLAB_V, rows, vc]) + _dot_nt(q_dec[h], st_ref[h].astype(BF)))
    for h in heads:
        vc = slice(h * GLA_DV, (h + 1) * GLA_DV)
        st_ref[h] = decay[h] * st_ref[h] + _dot_tn(p[SLAB_V, rows, vc], k_end[h])
    for h in heads:
        vc = slice(h * GLA_DV, (h + 1) * GLA_DV)
        g = p[SLAB_GLA_G, rows, vc].astype(F32)
        br_ref[0, rows, vc] = (_rms(o[h], gg_ref[:, vc]) * _silu(g)).astype(BF)


def _pool_window_sums(p, win_ref):
    sums = []
    for g in range(len(POOL_WINDOWS)):
        cs = slice(g * POOL_GROUP_DIM, (g + 1) * POOL_GROUP_DIM)
        sums.append(_dot(win_ref[g], p[SLAB_U, :, cs]))
    return sums


def _pool_branch(p, sums, pw_ref, ps_ref, br_ref, ubuf_ref, t, tm):
    u = p[SLAB_U].astype(F32)
    ubuf_ref[HIST_PAD:2 * HIST_PAD, :] = u[0:HIST_PAD]
    pos = t * tm + lax.broadcasted_iota(jnp.int32, (tm, 1), 0)
    for g, w in enumerate(POOL_WINDOWS):
        cs = slice(g * POOL_GROUP_DIM, (g + 1) * POOL_GROUP_DIM)
        ug = u[:, cs]
        head = ug[0:HIST_PAD]
        for j in range(1, w):
            head = head + ubuf_ref[HIST_PAD - j:2 * HIST_PAD - j, cs]
        s = jnp.concatenate([head, sums[g][HIST_PAD:]], axis=0)
        cnt = jnp.minimum(w, pos + 1).astype(F32)
        diff = s / cnt - ug
        mixed = _dot(diff.astype(BF), pw_ref[g]) * ps_ref[:, cs]
        pg = p[SLAB_POOL_G, :, cs].astype(F32)
        br_ref[1, :, cs] = (mixed * _silu(pg)).astype(BF)
    ubuf_ref[0:HIST_PAD, :] = u[tm - HIST_PAD:tm]


def _xattn_probs(p, mk_ref):
    out = []
    for h in range(XA_HEADS):
        cs = slice(h * XA_HEAD_DIM, (h + 1) * XA_HEAD_DIM)
        s = _dot_nt(p[SLAB_XQ, :, cs], mk_ref[0, :, cs]) * (XA_HEAD_DIM ** -0.5)
        pr = jnp.exp(s - jnp.max(s, axis=-1, keepdims=True))
        out.append((pr.astype(BF), jnp.sum(pr, axis=-1, keepdims=True)))
    return out


def _xattn_branch(p, probs, mv_ref, br_ref):
    for h in range(XA_HEADS):
        cs = slice(h * XA_HEAD_DIM, (h + 1) * XA_HEAD_DIM)
        pr, denom = probs[h]
        o = _dot(pr, mv_ref[0, :, cs]) / denom
        xg = p[SLAB_XG, :, cs].astype(F32)
        br_ref[2, :, cs] = (o * _silu(xg)).astype(BF)


def _prompt_layer_kernel(xn_ref, xc_ref, g_ref, w_ref, walr_ref, wa2_ref, ba_ref, cum_ref, win_ref,
                         mk_ref, mv_ref, pw_ref, ps_ref, gg_ref, wbr_ref, wo_ref, fg_ref,
                         out_ref, s_out_ref, hist_out_ref,
                         h_scr, p_scr, la_scr, br_scr, st_ref, ubuf_ref, *, tm, nt, final):
    s = pl.program_id(0)
    t = jnp.maximum(s - 1, 0) % nt
    slot_w = s % 2
    slot_r = 1 - slot_w

    @pl.when(s == 0)
    def _():
        p_scr[1] = jnp.zeros(p_scr.shape[1:], p_scr.dtype)
        la_scr[1] = jnp.zeros(la_scr.shape[1:], la_scr.dtype)

    @pl.when(t == 0)
    def _():
        st_ref[...] = jnp.zeros_like(st_ref)
        ubuf_ref[0:HIST_PAD, :] = jnp.zeros((HIST_PAD, D_MODEL), F32)

    @pl.when(s == 0)
    def _():
        h_scr[0] = _rms(xc_ref[...], g_ref[...]).astype(BF)

    p_next = p_scr.at[slot_w]
    p = p_scr.at[slot_r]

    slabs = list(range(N_SLABS))

    def filler(n=1):
        for _ in range(n):
            _inproj_slab(h_scr[slot_w], w_ref, p_next, slabs.pop(0))

    alr = _inproj_gate_lowrank(h_scr[slot_w], walr_ref)
    bcum = _gla_cumdecay(cum_ref, la_scr.at[slot_r])
    filler()
    _inproj_gate(alr, wa2_ref, ba_ref, la_scr.at[slot_w])
    per_site = (N_SLABS - 2) // 2
    probs = _xattn_probs(p, mk_ref)
    sums = _pool_window_sums(p, win_ref)
    _gla_chunk(p, bcum, gg_ref, br_scr, st_ref, 0, functools.partial(filler, per_site))
    filler(per_site)
    _xattn_branch(p, probs, mv_ref, br_scr)
    _pool_branch(p, sums, pw_ref, ps_ref, br_scr, ubuf_ref, t, tm)
    h_scr[slot_r] = _rms(xn_ref[...], g_ref[...]).astype(BF)
    merged = _merge_term(br_scr, p, wbr_ref, 2)
    merged = merged + _merge_term(br_scr, p, wbr_ref, 1)
    merged = merged + _merge_term(br_scr, p, wbr_ref, 0)
    filler()
    assert not slabs
    out_ref[...] = _merge_finish(merged, xc_ref[...], wo_ref, fg_ref, final)

    @pl.when((s > 0) & (t == nt - 1))
    def _():
        for h in range(GLA_HEADS):
            s_out_ref[0, 0, h] = st_ref[h].T
        hist_out_ref[0, 0] = ubuf_ref[1:HIST_PAD, :]


def _prompt_layer(x, ngain, w_proj, w_alr, wa2, ba, mk, mv, pool_w, pool_scale, gla_gain,
                  w_branch, w_out, fgain, carry, layer, depth, batch, seq, n_mem, tm, final):
    nt = seq // tm
    n_tiles = batch * nt
    assert tm == CHUNK and N_SLABS % 2 == 0, "one GLA chunk per token tile"
    n_alias = 0 if carry is None else len(carry)
    n_in = 17
    kern = _drop_alias_refs(functools.partial(_prompt_layer_kernel, tm=tm, nt=nt, final=final), n_in, n_alias)
    any_spec = pl.BlockSpec(memory_space=pl.ANY)
    scratch = [
        ((2, tm, D_MODEL), BF, 1),
        ((2, N_SLABS, tm, D_MODEL), BF, 1),
        ((2, tm, GLA_KEY_WIDTH), F32, 1),
        ((N_BRANCH, tm, D_MODEL), BF, 1),
        ((GLA_HEADS, GLA_DV, GLA_DK), F32, 1),
        ((2 * HIST_PAD, D_MODEL), F32, 1),
    ]

    def cur(s):
        return jnp.maximum(s - 1, 0)

    def seq_of(s):
        return cur(s) // nt

    return pl.pallas_call(
        kern,
        grid=(n_tiles + 1,),
        in_specs=[
            pl.BlockSpec((tm, D_MODEL), lambda s: (jnp.minimum(s + 1, n_tiles - 1), 0)),
            pl.BlockSpec((tm, D_MODEL), lambda s: (cur(s), 0)),
            _const_spec((1, D_MODEL)),
            _layer_spec((D_MODEL // 2, N_SLABS * D_MODEL), layer),
            _const_spec((GATE_RANK, D_MODEL)),
            _const_spec((GATE_RANK, GLA_KEY_WIDTH)),
            _const_spec((1, GLA_KEY_WIDTH)),
            _const_spec((tm, tm)),
            _const_spec((len(POOL_WINDOWS), tm, tm)),
            pl.BlockSpec((1, n_mem, D_MODEL), lambda s: (layer, seq_of(s), 0)),
            pl.BlockSpec((1, n_mem, D_MODEL), lambda s: (layer, seq_of(s), 0)),
            _const_spec((len(POOL_WINDOWS), POOL_GROUP_DIM, POOL_GROUP_DIM)),
            _const_spec((1, D_MODEL)),
            _const_spec((1, D_MODEL)),
            _layer_spec((N_BRANCH, D_MODEL // 2, D_MODEL), layer),
            _layer_spec((D_MODEL // 2, D_MODEL), layer),
            _const_spec((1, D_MODEL)),
        ] + [any_spec] * n_alias,
        out_specs=[
            pl.BlockSpec((tm, D_MODEL), lambda s: (cur(s), 0)),
            pl.BlockSpec((1, 1, GLA_HEADS, GLA_DK, GLA_DV), lambda s: (layer, seq_of(s), 0, 0, 0)),
            pl.BlockSpec((1, 1, POOL_HIST, D_MODEL), lambda s: (layer, seq_of(s), 0, 0)),
        ],
        out_shape=[
            jax.ShapeDtypeStruct((n_tiles * tm, D_MODEL), F32),
            jax.ShapeDtypeStruct((depth, batch, GLA_HEADS, GLA_DK, GLA_DV), F32),
            jax.ShapeDtypeStruct((depth, batch, POOL_HIST, D_MODEL), F32),
        ],
        scratch_shapes=[pltpu.VMEM(shape, dtype) for shape, dtype, _ in scratch],
        input_output_aliases={n_in + a: 1 + a for a in range(n_alias)},
        compiler_params=_params(
            ("arbitrary",), *scratch,
            ((tm, D_MODEL), F32, 6),
            ((D_MODEL // 2, (N_SLABS + N_BRANCH + 1) * D_MODEL), jnp.uint32, 1),
            ((n_mem, D_MODEL), BF, 4),
            ((tm, tm), BF, 1 + len(POOL_WINDOWS)),
            ((len(POOL_WINDOWS), POOL_GROUP_DIM, POOL_GROUP_DIM), BF, 1),
            ((GLA_HEADS, GLA_DK, GLA_DV), F32, 2)),
        name="prompt_layer",
    )(x, x, ngain, w_proj, w_alr, wa2, ba, _chunk_cumsum_matrix(tm), _window_matrices(tm),
      mk, mv, pool_w, pool_scale, gla_gain,
      w_branch, w_out, fgain, *(carry or ()))


def _inproj_kernel(x_ref, g_ref, w_ref, walr_ref, wa2_ref, ba_ref, p_ref, la_ref):
    h = _rms(x_ref[...], g_ref[...]).astype(BF)
    alr = _inproj_gate_lowrank(h, walr_ref)
    for j in range(N_SLABS):
        _inproj_slab(h, w_ref, p_ref, j)
    _inproj_gate(alr, wa2_ref, ba_ref, la_ref)


def _inproj(x, gain, w_proj, w_alr, w_a2, b_a, layer, tm, out_dtype):
    m_rows = x.shape[0]
    return pl.pallas_call(
        _inproj_kernel,
        grid=(m_rows // tm,),
        in_specs=[
            pl.BlockSpec((tm, D_MODEL), lambda i: (i, 0)),
            _const_spec((1, D_MODEL)),
            _layer_spec((D_MODEL // 2, N_SLABS * D_MODEL), layer),
            _const_spec((GATE_RANK, D_MODEL)),
            _const_spec((GATE_RANK, GLA_KEY_WIDTH)),
            _const_spec((1, GLA_KEY_WIDTH)),
        ],
        out_specs=[
            pl.BlockSpec((N_SLABS, tm, D_MODEL), lambda i: (0, i, 0)),
            pl.BlockSpec((tm, GLA_KEY_WIDTH), lambda i: (i, 0)),
        ],
        out_shape=[
            jax.ShapeDtypeStruct((N_SLABS, m_rows, D_MODEL), out_dtype),
            jax.ShapeDtypeStruct((m_rows, GLA_KEY_WIDTH), F32),
        ],
        compiler_params=_params(("parallel",),
                                ((tm, D_MODEL), F32, 2), ((D_MODEL // 2, N_SLABS * D_MODEL), jnp.uint32, 1),
                                ((N_SLABS, tm, D_MODEL), out_dtype, 2), ((tm, GLA_KEY_WIDTH), F32, 2)),
        name="inproj",
    )(x, gain, w_proj, w_alr, w_a2, b_a)


def _sample_mix_kernel(p_ref, la_ref, s0_ref, hist_ref, ck_hbm, cv_hbm, pw_ref, ps_ref, gg_ref,
                       br_ref, s_out_ref, hist_out_ref, diff_ref, kbuf, vbuf, cache_sem, *, sb, layer):
    n_steps = pl.num_programs(0) * pl.num_programs(1)
    step = pl.program_id(0) * pl.num_programs(1) + pl.program_id(1)

    def cache_copy(which, t):
        src, buf = ((ck_hbm, kbuf), (cv_hbm, vbuf))[which]
        slot = t % CACHE_RING
        return pltpu.make_async_copy(src.at[layer, pl.ds(t * sb, sb)], buf.at[slot], cache_sem.at[which, slot])

    @pl.when(step == 0)
    def _():
        for t in range(CACHE_RING - 1):
            cache_copy(0, t).start()
            cache_copy(1, t).start()

    @pl.when(step + CACHE_RING - 1 < n_steps)
    def _():
        cache_copy(0, step + CACHE_RING - 1).start()
        cache_copy(1, step + CACHE_RING - 1).start()

    cache_copy(0, step).wait()
    cache_copy(1, step).wait()
    ck_ref = kbuf.at[step % CACHE_RING]
    cv_ref = vbuf.at[step % CACHE_RING]

    r0 = pl.program_id(1) * sb
    erow = lax.broadcasted_iota(jnp.int32, (GLA_DK, GLA_DK), 0)
    ecol = lax.broadcasted_iota(jnp.int32, (GLA_DK, GLA_DK), 1)
    eye = erow == ecol

    def to_col(x):
        return jnp.sum(jnp.where(eye, jnp.broadcast_to(x, (GLA_DK, GLA_DK)), 0.0), axis=1, keepdims=True)

    for i in range(sb):
        r = pl.ds(r0 + i, 1)
        la = la_ref[r, :]
        qk = p_ref[SLAB_QK, r, :]
        vv = p_ref[SLAB_V, r, :]
        gla_g = p_ref[SLAB_GLA_G, r, :]
        for h in range(GLA_HEADS):
            kc = slice(h * GLA_DK, (h + 1) * GLA_DK)
            kc2 = slice(GLA_KEY_WIDTH + h * GLA_DK, GLA_KEY_WIDTH + (h + 1) * GLA_DK)
            vc = slice(h * GLA_DV, (h + 1) * GLA_DV)
            a_col = to_col(jnp.exp(la[:, kc]))
            q_col = to_col(qk[:, kc] * (GLA_DK ** -0.5))
            k_col = to_col(qk[:, kc2])
            s_new = a_col * s0_ref[0, i, h] + k_col * vv[:, vc]
            s_out_ref[0, i, h] = s_new
            o = jnp.sum(q_col * s_new, axis=0, keepdims=True)
            br_ref[0, r, vc] = _rms(o, gg_ref[:, vc]) * _silu(gla_g[:, vc])

        u = p_ref[SLAB_U, r, :]
        for g, w in enumerate(POOL_WINDOWS):
            cs = slice(g * POOL_GROUP_DIM, (g + 1) * POOL_GROUP_DIM)
            past = jnp.sum(hist_ref[0, POOL_HIST - (w - 1):POOL_HIST, r, cs], axis=0)
            diff_ref[r, cs] = (u[:, cs] + past) / float(w) - u[:, cs]
        hist_out_ref[0, 0:POOL_HIST - 1, r, :] = hist_ref[0, 1:POOL_HIST, r, :]
        hist_out_ref[0, POOL_HIST - 1, r, :] = u

        xq = p_ref[SLAB_XQ, r, :]
        xg = p_ref[SLAB_XG, r, :]
        half_cols = [slice(h * XA_HEAD_DIM + j * LANES, h * XA_HEAD_DIM + (j + 1) * LANES)
                     for j in range(XA_HEAD_DIM // LANES) for h in range(XA_HEADS)]
        xq_rows = jnp.concatenate([xq[:, cs] for cs in half_cols], axis=0)
        n_mem = ck_ref.shape[1]
        prod = (ck_ref[i] * xq_rows[None]).reshape(n_mem * SUBLANES, LANES).astype(BF)
        part = _dot(prod, jnp.ones((LANES, LANES), BF)).reshape(n_mem, SUBLANES, LANES)
        s = (part + pltpu.roll(part, XA_HEADS, axis=1)) * (XA_HEAD_DIM ** -0.5)
        p = jnp.exp(s - jnp.max(s, axis=0, keepdims=True))
        o = jnp.sum(p * cv_ref[i], axis=0) / jnp.sum(p, axis=0)
        halves = XA_HEAD_DIM // LANES
        o_row = jnp.concatenate([o[j * XA_HEADS + h:j * XA_HEADS + h + 1, :]
                                 for h in range(XA_HEADS) for j in range(halves)], axis=1)
        br_ref[2, r, :] = o_row * _silu(xg)

    @pl.when(pl.program_id(1) == pl.num_programs(1) - 1)
    def _():
        for g in range(len(POOL_WINDOWS)):
            cs = slice(g * POOL_GROUP_DIM, (g + 1) * POOL_GROUP_DIM)
            mixed = _dot(diff_ref[:, cs].astype(BF), pw_ref[g]) * ps_ref[:, cs]
            br_ref[1, :, cs] = mixed * _silu(p_ref[SLAB_POOL_G, :, cs])


def _cache_rows_view(c):
    depth, nb, n_mem = c.shape[:3]
    halves = XA_HEAD_DIM // LANES
    c = c.reshape(depth, nb, n_mem, XA_HEADS, halves, LANES)
    return c.transpose(0, 1, 2, 4, 3, 5).reshape(depth, nb, n_mem, halves * XA_HEADS, LANES)


def _sample_mix(p, la, s0, hist, ck, cv, pool_w, pool_scale, gla_gain, carry, layer, sb=SAMPLE_BLOCK):
    nb = s0.shape[1]
    n_mem = ck.shape[2]
    ck, cv = _cache_rows_view(ck), _cache_rows_view(cv)
    rb = SUBLANES
    halves = rb // sb
    n_alias = 0 if carry is None else len(carry)
    n_in = 9
    kern = _drop_alias_refs(functools.partial(_sample_mix_kernel, sb=sb, layer=layer), n_in, n_alias)
    any_spec = pl.BlockSpec(memory_space=pl.ANY)
    cache_ring = (CACHE_RING, sb, n_mem, SUBLANES, LANES)
    return pl.pallas_call(
        kern,
        grid=(nb // rb, halves),
        in_specs=[
            pl.BlockSpec((N_MIX_SLABS, rb, D_MODEL), lambda i, j: (0, i, 0)),
            pl.BlockSpec((rb, GLA_KEY_WIDTH), lambda i, j: (i, 0)),
            pl.BlockSpec((1, sb, GLA_HEADS, GLA_DK, GLA_DV), lambda i, j: (layer, i * halves + j, 0, 0, 0)),
            pl.BlockSpec((1, POOL_HIST, rb, D_MODEL), lambda i, j: (layer, 0, i, 0)),
            any_spec, any_spec,
            _const_spec((len(POOL_WINDOWS), POOL_GROUP_DIM, POOL_GROUP_DIM)),
            _const_spec((1, D_MODEL)),
            _const_spec((1, D_MODEL)),
        ] + [any_spec] * n_alias,
        out_specs=[
            pl.BlockSpec((N_BRANCH, rb, D_MODEL), lambda i, j: (0, i, 0)),
            pl.BlockSpec((1, sb, GLA_HEADS, GLA_DK, GLA_DV), lambda i, j: (layer, i * halves + j, 0, 0, 0)),
            pl.BlockSpec((1, POOL_HIST, rb, D_MODEL), lambda i, j: (layer, 0, i, 0)),
        ],
        out_shape=[
            jax.ShapeDtypeStruct((N_BRANCH, nb, D_MODEL), F32),
            jax.ShapeDtypeStruct(s0.shape, F32),
            jax.ShapeDtypeStruct(hist.shape, F32),
        ],
        scratch_shapes=[pltpu.VMEM((rb, D_MODEL), F32), pltpu.VMEM(cache_ring, F32), pltpu.VMEM(cache_ring, F32),
                        pltpu.SemaphoreType.DMA((2, CACHE_RING))],
        input_output_aliases={n_in + a: 1 + a for a in range(n_alias)},
        compiler_params=_params(
            ("arbitrary", "arbitrary"),
            ((N_MIX_SLABS + N_BRANCH + 1, rb, D_MODEL), F32, 2),
            ((sb, GLA_HEADS, GLA_DK, GLA_DV), F32, 4),
            ((POOL_HIST, rb, D_MODEL), F32, 4),
            (cache_ring, F32, 2),
            ((len(POOL_WINDOWS), POOL_GROUP_DIM, POOL_GROUP_DIM), BF, 1)),
        name="sample_mix",
    )(p, la, s0, hist, ck, cv, pool_w, pool_scale, gla_gain, *(carry or ()))


def _merge_kernel(br_ref, p_ref, x_ref, wbr_ref, wo_ref, fg_ref, out_ref, *, final):
    merged = _merge_term(br_ref, p_ref, wbr_ref, 0)
    for n in range(1, N_BRANCH):
        merged = merged + _merge_term(br_ref, p_ref, wbr_ref, n)
    out_ref[...] = _merge_finish(merged, x_ref[...], wo_ref, fg_ref, final)


def _merge_out(br, p, x, w_branch, w_out, final_gain, layer, final):
    m_rows = x.shape[0]
    whole = lambda shape: pl.BlockSpec(shape, lambda i: (0,) * len(shape))
    return pl.pallas_call(
        functools.partial(_merge_kernel, final=final),
        grid=(1,),
        in_specs=[
            whole((N_BRANCH, m_rows, D_MODEL)),
            whole((N_SLABS, m_rows, D_MODEL)),
            whole((m_rows, D_MODEL)),
            _layer_spec((N_BRANCH, D_MODEL // 2, D_MODEL), layer),
            _layer_spec((D_MODEL // 2, D_MODEL), layer),
            _const_spec((1, D_MODEL)),
        ],
        out_specs=whole((m_rows, D_MODEL)),
        out_shape=jax.ShapeDtypeStruct((m_rows, D_MODEL), F32),
        compiler_params=_params(("arbitrary",),
                                ((N_BRANCH + N_SLABS + 2, m_rows, D_MODEL), F32, 2),
                                ((D_MODEL // 2, (N_BRANCH + 1) * D_MODEL), jnp.uint32, 1)),
        name="merge_out",
    )(br, p, x, w_branch, w_out, final_gain)


def kernel(x_prompt, x_sample, mem_prompt, cache_mem_k, cache_mem_v, state_gla, state_pool, w_in, w_a2, b_a, gla_gain, pool_w, pool_scale, w_mk, w_mv, w_branch, w_out, norm_gain, final_gain):
    batch, seq, _ = x_prompt.shape
    nb = x_sample.shape[0]
    n_mem = mem_prompt.shape[1]
    depth = w_in.shape[0]
    tm = PROMPT_TILE
    assert seq % tm == 0 and nb % SUBLANES == 0 and x_prompt.shape[2] == D_MODEL

    xp = x_prompt.reshape(batch * seq, D_MODEL)
    xs = x_sample.reshape(nb, D_MODEL)
    mem = mem_prompt.reshape(batch * n_mem, D_MODEL)
    fgain = final_gain.reshape(1, D_MODEL)

    mk, mv, mk_bf, mv_bf = _kvproj(mem, w_mk.astype(BF), w_mv.astype(BF), batch, n_mem)

    w_in_t = jnp.swapaxes(w_in, 1, 2)
    pool_rows = jnp.swapaxes(state_pool, 1, 2)
    w_proj = _pack_w_in(w_in_t)
    wb = _pack_matrices(w_branch.reshape(depth * N_BRANCH, D_MODEL, D_MODEL))
    wb = wb.reshape(depth, N_BRANCH, D_MODEL // 2, D_MODEL)
    wo = _pack_matrices(w_out)

    carry_p, carry_s = None, None
    for l in range(depth):
        final = l == depth - 1
        w_alr = w_in_t[l, ALR_START:ALR_START + GATE_RANK, :]
        wa2 = w_a2[l].astype(BF)
        ba = b_a[l].reshape(1, GLA_KEY_WIDTH)
        ngain = norm_gain[l].reshape(1, D_MODEL)
        ggain = gla_gain[l].reshape(1, D_MODEL)
        pscale = pool_scale[l].reshape(1, D_MODEL)
        pw = pool_w[l].astype(BF)

        xp, s_all, hist_all = _prompt_layer(xp, ngain, w_proj, w_alr, wa2, ba, mk_bf, mv_bf, pw, pscale,
                                            ggain, wb, wo, fgain, carry_p, l, depth, batch, seq, n_mem, tm, final)
        carry_p = (s_all, hist_all)

        ps, las = _inproj(xs, ngain, w_proj, w_alr, wa2, ba, l, nb, F32)
        brs, s_new, hist_new = _sample_mix(ps, las, state_gla, pool_rows, cache_mem_k, cache_mem_v,
                                           pw, pscale, ggain, carry_s, l)
        carry_s = (s_new, hist_new)
        xs = _merge_out(brs, ps, xs, wb, wo, fgain, l, final)

    return (xp.reshape(batch, seq, D_MODEL), xs.reshape(nb, 1, D_MODEL),
            mk, mv, carry_p[0], carry_p[1], carry_s[0], jnp.swapaxes(carry_s[1], 1, 2))
```

```python
import functools
import math

import jax
import jax.numpy as jnp
from jax import lax
from jax.experimental import pallas as pl
from jax.experimental.pallas import tpu as pltpu

D_MODEL = 1024
GLA_HEADS = 4
GLA_DK = 128
GLA_DV = 256
GLA_KEY_WIDTH = GLA_HEADS * GLA_DK
GATE_RANK = 16
GATE_TAU = 16.0
CHUNK = 128
POOL_WINDOWS = (2, 4, 8, 16)
POOL_GROUP_DIM = 256
POOL_HIST = 15
HIST_PAD = 16
XA_HEADS = 4
XA_HEAD_DIM = 256
N_BRANCH = 3
EPS = 1e-6
SUBLANES = 8
LANES = 128

SLAB_QK, SLAB_V, SLAB_GLA_G, SLAB_U, SLAB_POOL_G, SLAB_XQ, SLAB_XG, SLAB_MERGE = 0, 1, 2, 3, 4, 5, 6, 7
N_SLABS = 10
N_MIX_SLABS = 7
N_HEAD_SLABS = 3
ALR_START = N_HEAD_SLABS * D_MODEL

BF = jnp.bfloat16
F32 = jnp.float32
MIB = 1 << 20

PROMPT_TILE = 256
SAMPLE_BLOCK = 4
CACHE_RING = 3
VMEM_COMPILER_SCRATCH = 8 * MIB


def _dot(a, b):
    return jnp.dot(a, b, preferred_element_type=F32)


def _dot_nt(a, b):
    return lax.dot_general(a, b, (((1,), (1,)), ((), ())), preferred_element_type=F32)


def _dot_tn(a, b):
    return lax.dot_general(a, b, (((0,), (0,)), ((), ())), preferred_element_type=F32)


def _pack_rows(w):
    return pltpu.bitcast(w.astype(BF), jnp.uint32)


def _unpack_rows(w_words):
    return pltpu.bitcast(w_words, BF)


def _params(sem, *buffers):
    need = sum(math.prod(shape) * jnp.dtype(dtype).itemsize * copies for shape, dtype, copies in buffers)
    return pltpu.CompilerParams(dimension_semantics=sem, vmem_limit_bytes=need + VMEM_COMPILER_SCRATCH)


def _pack_w_in_kernel(a_ref, b_ref, o_ref):
    j = pl.program_id(1)

    @pl.when(j < N_HEAD_SLABS)
    def _():
        o_ref[0] = _pack_rows(a_ref[0].T)

    @pl.when(j >= N_HEAD_SLABS)
    def _():
        o_ref[0] = _pack_rows(jnp.concatenate([a_ref[0, GATE_RANK:, :], b_ref[0]], axis=0).T)


def _pack_w_in(w_in_t):
    depth = w_in_t.shape[0]
    return pl.pallas_call(
        _pack_w_in_kernel,
        grid=(depth, N_SLABS),
        in_specs=[
            pl.BlockSpec((1, D_MODEL, D_MODEL), lambda l, j: (l, j, 0)),
            pl.BlockSpec((1, GATE_RANK, D_MODEL), lambda l, j: (l, (j + 1) * (D_MODEL // GATE_RANK), 0)),
        ],
        out_specs=pl.BlockSpec((1, D_MODEL // 2, D_MODEL), lambda l, j: (l, 0, j)),
        out_shape=jax.ShapeDtypeStruct((depth, D_MODEL // 2, N_SLABS * D_MODEL), jnp.uint32),
        compiler_params=_params(("parallel", "parallel"),
                                ((D_MODEL, D_MODEL), F32, 2), ((GATE_RANK, D_MODEL), F32, 2),
                                ((D_MODEL // 2, D_MODEL), jnp.uint32, 2)),
        name="pack_w_in",
    )(w_in_t, w_in_t)


def _pack_matrices_kernel(a_ref, o_ref):
    o_ref[0] = _pack_rows(a_ref[0])


def _pack_matrices(w):
    return pl.pallas_call(
        _pack_matrices_kernel,
        grid=(w.shape[0],),
        in_specs=[pl.BlockSpec((1, D_MODEL, D_MODEL), lambda r: (r, 0, 0))],
        out_specs=pl.BlockSpec((1, D_MODEL // 2, D_MODEL), lambda r: (r, 0, 0)),
        out_shape=jax.ShapeDtypeStruct((w.shape[0], D_MODEL // 2, D_MODEL), jnp.uint32),
        compiler_params=_params(("parallel",), ((D_MODEL, D_MODEL), F32, 2), ((D_MODEL // 2, D_MODEL), jnp.uint32, 2)),
        name="pack_matrices",
    )(w)


def _silu(x):
    return x * jax.nn.sigmoid(x)


def _rms(x, gain):
    ms = jnp.mean(x * x, axis=-1, keepdims=True)
    return x * lax.rsqrt(ms + EPS) * gain


def _const_spec(shape):
    zeros = (0,) * len(shape)
    return pl.BlockSpec(shape, lambda *_: zeros, pipeline_mode=pl.Buffered(1))


def _layer_spec(shape, layer):
    index = (layer,) + (0,) * len(shape)
    return pl.BlockSpec((1,) + tuple(shape), lambda *_: index, pipeline_mode=pl.Buffered(1))


def _drop_alias_refs(body, n_in, n_alias):
    def kern(*refs):
        return body(*refs[:n_in], *refs[n_in + n_alias:])
    return kern


def _kvproj_kernel(m_ref, wk_ref, wv_ref, k_ref, v_ref, kb_ref, vb_ref):
    m = m_ref[...].astype(BF)
    k = _dot(m, _unpack_rows(wk_ref[0]))
    v = _dot(m, _unpack_rows(wv_ref[0]))
    for h in range(XA_HEADS):
        cs = slice(h * XA_HEAD_DIM, (h + 1) * XA_HEAD_DIM)
        k_ref[0, 0, :, h, :] = k[:, cs]
        v_ref[0, 0, :, h, :] = v[:, cs]
    kb_ref[0] = k.astype(BF)
    vb_ref[0] = v.astype(BF)


def _kvproj(mem, wk, wv, batch, n_mem):
    depth = wk.shape[0]
    w_spec = pl.BlockSpec((1, D_MODEL // 2, D_MODEL), lambda l, b: (l, 0, 0))
    out5 = pl.BlockSpec((1, 1, n_mem, XA_HEADS, XA_HEAD_DIM), lambda l, b: (l, b, 0, 0, 0))
    out_bf = pl.BlockSpec((1, n_mem, D_MODEL), lambda l, b: (l, b, 0))
    return pl.pallas_call(
        _kvproj_kernel,
        grid=(depth, batch),
        in_specs=[pl.BlockSpec((n_mem, D_MODEL), lambda l, b: (b, 0)), w_spec, w_spec],
        out_specs=[out5, out5, out_bf, out_bf],
        out_shape=[jax.ShapeDtypeStruct((depth, batch, n_mem, XA_HEADS, XA_HEAD_DIM), F32)] * 2
        + [jax.ShapeDtypeStruct((depth, batch * n_mem, D_MODEL), BF)] * 2,
        compiler_params=_params(("parallel", "parallel"),
                                ((n_mem, D_MODEL), F32, 2), ((D_MODEL // 2, D_MODEL), jnp.uint32, 4),
                                ((n_mem, D_MODEL), F32, 4), ((n_mem, D_MODEL), BF, 4)),
        name="kvproj",
    )(mem, wk, wv)


def _inproj_slab(h, w_ref, p_out, j):
    w = w_ref[0, :, j * D_MODEL:(j + 1) * D_MODEL]
    p_out[j] = _dot(h, _unpack_rows(w)).astype(p_out.dtype)


def _inproj_gate_lowrank(h, walr_ref):
    return _dot_nt(h, walr_ref[...].astype(BF)).astype(BF)


def _inproj_gate(alr, wa2_ref, ba_ref, la_out):
    z = _dot(alr, wa2_ref[...]) + ba_ref[...]
    la_out[...] = (jnp.minimum(z, 0.0) - jnp.log(1.0 + jnp.exp(-jnp.abs(z)))) * (1.0 / GATE_TAU)


def _merge_term(br_ref, p, wbr_ref, n):
    return jax.nn.sigmoid(p[SLAB_MERGE + n].astype(F32)) * _dot(br_ref[n].astype(BF), _unpack_rows(wbr_ref[0, n]))


def _merge_finish(merged, x, wo_ref, fg_ref, final):
    x_new = x + _dot(merged.astype(BF), _unpack_rows(wo_ref[0]))
    return _rms(x_new, fg_ref[...]) if final else x_new


def _chunk_cumsum_matrix(tm):
    row = lax.broadcasted_iota(jnp.int32, (tm, tm), 0)
    col = lax.broadcasted_iota(jnp.int32, (tm, tm), 1)
    return (((row // CHUNK) == (col // CHUNK)) & (row >= col)).astype(BF)


def _window_matrices(tm):
    row = lax.broadcasted_iota(jnp.int32, (tm, tm), 0)
    col = lax.broadcasted_iota(jnp.int32, (tm, tm), 1)
    return jnp.stack([((row >= col) & (row - col < w)).astype(BF) for w in POOL_WINDOWS])


def _gla_cumdecay(cum_ref, la_ref):
    la = la_ref[...]
    la_hi = la.astype(BF)
    la_lo = (la - la_hi.astype(F32)).astype(BF)
    return _dot(cum_ref[...], la_hi) + _dot(cum_ref[...], la_lo)


def _gla_chunk(p, bcum, gg_ref, br_ref, st_ref, c, filler):
    crow = lax.broadcasted_iota(jnp.int32, (CHUNK, CHUNK), 0)
    ccol = lax.broadcasted_iota(jnp.int32, (CHUNK, CHUNK), 1)
    causal = crow >= ccol
    rows = slice(c * CHUNK, (c + 1) * CHUNK)
    heads = range(GLA_HEADS)
    q_dec, k_end, decay, att = [], [], [], []
    for h in heads:
        kc = slice(h * GLA_DK, (h + 1) * GLA_DK)
        kc2 = slice(GLA_KEY_WIDTH + h * GLA_DK, GLA_KEY_WIDTH + (h + 1) * GLA_DK)
        b = bcum[rows, kc]
        b_mid = b[CHUNK // 2 - 1:CHUNK // 2, :]
        b_last = b[CHUNK - 1:CHUNK, :]
        q = p[SLAB_QK, rows, kc].astype(F32) * (GLA_DK ** -0.5)
        k = p[SLAB_QK, rows, kc2].astype(F32)
        q_dec.append((q * jnp.exp(b)).astype(BF))
        q_mid = (q * jnp.exp(b - b_mid)).astype(BF)
        k_mid = (k * jnp.exp(b_mid - b)).astype(BF)
        k_end.append((k * jnp.exp(b_last - b)).astype(BF))
        decay.append(jnp.exp(b_last))
        att.append(_dot_nt(q_mid, k_mid))
    filler()
    o = []
    for h in heads:
        vc = slice(h * GLA_DV, (h + 1) * GLA_DV)
        a = jnp.where(causal, att[h], 0.0).astype(BF)
        o.append(_dot(a, p[SLAB_V, rows, vc]) + _dot_nt(q_dec[h], st_ref[h].astype(BF)))
    for h in heads:
        vc = slice(h * GLA_DV, (h + 1) * GLA_DV)
        st_ref[h] = decay[h] * st_ref[h] + _dot_tn(p[SLAB_V, rows, vc], k_end[h])
    for h in heads:
        vc = slice(h * GLA_DV, (h + 1) * GLA_DV)
        g = p[SLAB_GLA_G, rows, vc].astype(F32)
        br_ref[0, rows, vc] = (_rms(o[h], gg_ref[:, vc]) * _silu(g)).astype(BF)


def _pool_window_sums(p, win_ref):
    sums = []
    for g in range(len(POOL_WINDOWS)):
        cs = slice(g * POOL_GROUP_DIM, (g + 1) * POOL_GROUP_DIM)
        sums.append(_dot(win_ref[g], p[SLAB_U, :, cs]))
    return sums


def _pool_branch(p, sums, pw_ref, ps_ref, br_ref, ubuf_ref, t, tm):
    u = p[SLAB_U].astype(F32)
    ubuf_ref[HIST_PAD:2 * HIST_PAD, :] = u[0:HIST_PAD]
    pos = t * tm + lax.broadcasted_iota(jnp.int32, (tm, 1), 0)
    for g, w in enumerate(POOL_WINDOWS):
        cs = slice(g * POOL_GROUP_DIM, (g + 1) * POOL_GROUP_DIM)
        ug = u[:, cs]
        head = ug[0:HIST_PAD]
        for j in range(1, w):
            head = head + ubuf_ref[HIST_PAD - j:2 * HIST_PAD - j, cs]
        s = jnp.concatenate([head, sums[g][HIST_PAD:]], axis=0)
        cnt = jnp.minimum(w, pos + 1).astype(F32)
        diff = s / cnt - ug
        mixed = _dot(diff.astype(BF), pw_ref[g]) * ps_ref[:, cs]
        pg = p[SLAB_POOL_G, :, cs].astype(F32)
        br_ref[1, :, cs] = (mixed * _silu(pg)).astype(BF)
    ubuf_ref[0:HIST_PAD, :] = u[tm - HIST_PAD:tm]


def _xattn_probs(p, mk_ref):
    out = []
    for h in range(XA_HEADS):
        cs = slice(h * XA_HEAD_DIM, (h + 1) * XA_HEAD_DIM)
        s = _dot_nt(p[SLAB_XQ, :, cs], mk_ref[0, :, cs]) * (XA_HEAD_DIM ** -0.5)
        pr = jnp.exp(s - jnp.max(s, axis=-1, keepdims=True))
        out.append((pr.astype(BF), jnp.sum(pr, axis=-1, keepdims=True)))
    return out


def _xattn_branch(p, probs, mv_ref, br_ref):
    for h in range(XA_HEADS):
        cs = slice(h * XA_HEAD_DIM, (h + 1) * XA_HEAD_DIM)
        pr, denom = probs[h]
        o = _dot(pr, mv_ref[0, :, cs]) / denom
        xg = p[SLAB_XG, :, cs].astype(F32)
        br_ref[2, :, cs] = (o * _silu(xg)).astype(BF)


def _prompt_layer_kernel(xn_ref, xc_ref, g_ref, w_ref, walr_ref, wa2_ref, ba_ref, cum_ref, win_ref,
                         mk_ref, mv_ref, pw_ref, ps_ref, gg_ref, wbr_ref, wo_ref, fg_ref,
                         out_ref, s_out_ref, hist_out_ref,
                         h_scr, p_scr, la_scr, br_scr, st_ref, ubuf_ref, *, tm, nt, final):
    s = pl.program_id(0)
    t = jnp.maximum(s - 1, 0) % nt
    slot_w = s % 2
    slot_r = 1 - slot_w

    @pl.when(s == 0)
    def _():
        p_scr[1] = jnp.zeros(p_scr.shape[1:], p_scr.dtype)
        la_scr[1] = jnp.zeros(la_scr.shape[1:], la_scr.dtype)

    @pl.when(t == 0)
    def _():
        st_ref[...] = jnp.zeros_like(st_ref)
        ubuf_ref[0:HIST_PAD, :] = jnp.zeros((HIST_PAD, D_MODEL), F32)

    @pl.when(s == 0)
    def _():
        h_scr[0] = _rms(xc_ref[...], g_ref[...]).astype(BF)

    p_next = p_scr.at[slot_w]
    p = p_scr.at[slot_r]

    slabs = list(range(N_SLABS))

    def filler(n=1):
        for _ in range(n):
            _inproj_slab(h_scr[slot_w], w_ref, p_next, slabs.pop(0))

    alr = _inproj_gate_lowrank(h_scr[slot_w], walr_ref)
    bcum = _gla_cumdecay(cum_ref, la_scr.at[slot_r])
    filler()
    _inproj_gate(alr, wa2_ref, ba_ref, la_scr.at[slot_w])
    per_site = (N_SLABS - 2) // (2 * (tm // CHUNK))
    probs, sums = None, None
    for c in range(tm // CHUNK):
        _gla_chunk(p, bcum, gg_ref, br_scr, st_ref, c, functools.partial(filler, per_site))
        filler(per_site)
        if c == 0:
            probs = _xattn_probs(p, mk_ref)
            sums = _pool_window_sums(p, win_ref)
        elif c == 1:
            _xattn_branch(p, probs, mv_ref, br_scr)
            _pool_branch(p, sums, pw_ref, ps_ref, br_scr, ubuf_ref, t, tm)
    h_scr[slot_r] = _rms(xn_ref[...], g_ref[...]).astype(BF)
    merged = _merge_term(br_scr, p, wbr_ref, 2)
    merged = merged + _merge_term(br_scr, p, wbr_ref, 1)
    merged = merged + _merge_term(br_scr, p, wbr_ref, 0)
    filler()
    assert not slabs
    out_ref[...] = _merge_finish(merged, xc_ref[...], wo_ref, fg_ref, final)

    @pl.when((s > 0) & (t == nt - 1))
    def _():
        for h in range(GLA_HEADS):
            s_out_ref[0, 0, h] = st_ref[h].T
        hist_out_ref[0, 0] = ubuf_ref[1:HIST_PAD, :]


def _prompt_layer(x, ngain, w_proj, w_alr, wa2, ba, mk, mv, pool_w, pool_scale, gla_gain,
                  w_branch, w_out, fgain, carry, layer, depth, batch, seq, n_mem, tm, final):
    nt = seq // tm
    n_tiles = batch * nt
    assert tm // CHUNK == 2 and (N_SLABS - 2) % (2 * (tm // CHUNK)) == 0, "slab placement assumes two GLA chunks"
    n_alias = 0 if carry is None else len(carry)
    n_in = 17
    kern = _drop_alias_refs(functools.partial(_prompt_layer_kernel, tm=tm, nt=nt, final=final), n_in, n_alias)
    any_spec = pl.BlockSpec(memory_space=pl.ANY)
    scratch = [
        ((2, tm, D_MODEL), BF, 1),
        ((2, N_SLABS, tm, D_MODEL), BF, 1),
        ((2, tm, GLA_KEY_WIDTH), F32, 1),
        ((N_BRANCH, tm, D_MODEL), BF, 1),
        ((GLA_HEADS, GLA_DV, GLA_DK), F32, 1),
        ((2 * HIST_PAD, D_MODEL), F32, 1),
    ]

    def cur(s):
        return jnp.maximum(s - 1, 0)

    def seq_of(s):
        return cur(s) // nt

    return pl.pallas_call(
        kern,
        grid=(n_tiles + 1,),
        in_specs=[
            pl.BlockSpec((tm, D_MODEL), lambda s: (jnp.minimum(s + 1, n_tiles - 1), 0)),
            pl.BlockSpec((tm, D_MODEL), lambda s: (cur(s), 0)),
            _const_spec((1, D_MODEL)),
            _layer_spec((D_MODEL // 2, N_SLABS * D_MODEL), layer),
            _const_spec((GATE_RANK, D_MODEL)),
            _const_spec((GATE_RANK, GLA_KEY_WIDTH)),
            _const_spec((1, GLA_KEY_WIDTH)),
            _const_spec((tm, tm)),
            _const_spec((len(POOL_WINDOWS), tm, tm)),
            pl.BlockSpec((1, n_mem, D_MODEL), lambda s: (layer, seq_of(s), 0)),
            pl.BlockSpec((1, n_mem, D_MODEL), lambda s: (layer, seq_of(s), 0)),
            _const_spec((len(POOL_WINDOWS), POOL_GROUP_DIM, POOL_GROUP_DIM)),
            _const_spec((1, D_MODEL)),
            _const_spec((1, D_MODEL)),
            _layer_spec((N_BRANCH, D_MODEL // 2, D_MODEL), layer),
            _layer_spec((D_MODEL // 2, D_MODEL), layer),
            _const_spec((1, D_MODEL)),
        ] + [any_spec] * n_alias,
        out_specs=[
            pl.BlockSpec((tm, D_MODEL), lambda s: (cur(s), 0)),
            pl.BlockSpec((1, 1, GLA_HEADS, GLA_DK, GLA_DV), lambda s: (layer, seq_of(s), 0, 0, 0)),
            pl.BlockSpec((1, 1, POOL_HIST, D_MODEL), lambda s: (layer, seq_of(s), 0, 0)),
        ],
        out_shape=[
            jax.ShapeDtypeStruct((n_tiles * tm, D_MODEL), F32),
            jax.ShapeDtypeStruct((depth, batch, GLA_HEADS, GLA_DK, GLA_DV), F32),
            jax.ShapeDtypeStruct((depth, batch, POOL_HIST, D_MODEL), F32),
        ],
        scratch_shapes=[pltpu.VMEM(shape, dtype) for shape, dtype, _ in scratch],
        input_output_aliases={n_in + a: 1 + a for a in range(n_alias)},
        compiler_params=_params(
            ("arbitrary",), *scratch,
            ((tm, D_MODEL), F32, 6),
            ((D_MODEL // 2, (N_SLABS + N_BRANCH + 1) * D_MODEL), jnp.uint32, 1),
            ((n_mem, D_MODEL), BF, 4),
            ((tm, tm), BF, 1 + len(POOL_WINDOWS)),
            ((len(POOL_WINDOWS), POOL_GROUP_DIM, POOL_GROUP_DIM), BF, 1),
            ((GLA_HEADS, GLA_DK, GLA_DV), F32, 2)),
        name="prompt_layer",
    )(x, x, ngain, w_proj, w_alr, wa2, ba, _chunk_cumsum_matrix(tm), _window_matrices(tm),
      mk, mv, pool_w, pool_scale, gla_gain,
      w_branch, w_out, fgain, *(carry or ()))


def _inproj_kernel(x_ref, g_ref, w_ref, walr_ref, wa2_ref, ba_ref, p_ref, la_ref):
    h = _rms(x_ref[...], g_ref[...]).astype(BF)
    alr = _inproj_gate_lowrank(h, walr_ref)
    for j in range(N_SLABS):
        _inproj_slab(h, w_ref, p_ref, j)
    _inproj_gate(alr, wa2_ref, ba_ref, la_ref)


def _inproj(x, gain, w_proj, w_alr, w_a2, b_a, layer, tm, out_dtype):
    m_rows = x.shape[0]
    return pl.pallas_call(
        _inproj_kernel,
        grid=(m_rows // tm,),
        in_specs=[
            pl.BlockSpec((tm, D_MODEL), lambda i: (i, 0)),
            _const_spec((1, D_MODEL)),
            _layer_spec((D_MODEL // 2, N_SLABS * D_MODEL), layer),
            _const_spec((GATE_RANK, D_MODEL)),
            _const_spec((GATE_RANK, GLA_KEY_WIDTH)),
            _const_spec((1, GLA_KEY_WIDTH)),
        ],
        out_specs=[
            pl.BlockSpec((N_SLABS, tm, D_MODEL), lambda i: (0, i, 0)),
            pl.BlockSpec((tm, GLA_KEY_WIDTH), lambda i: (i, 0)),
        ],
        out_shape=[
            jax.ShapeDtypeStruct((N_SLABS, m_rows, D_MODEL), out_dtype),
            jax.ShapeDtypeStruct((m_rows, GLA_KEY_WIDTH), F32),
        ],
        compiler_params=_params(("parallel",),
                                ((tm, D_MODEL), F32, 2), ((D_MODEL // 2, N_SLABS * D_MODEL), jnp.uint32, 1),
                                ((N_SLABS, tm, D_MODEL), out_dtype, 2), ((tm, GLA_KEY_WIDTH), F32, 2)),
        name="inproj",
    )(x, gain, w_proj, w_alr, w_a2, b_a)


def _sample_mix_kernel(p_ref, la_ref, s0_ref, hist_ref, ck_hbm, cv_hbm, pw_ref, ps_ref, gg_ref,
                       br_ref, s_out_ref, hist_out_ref, diff_ref, kbuf, vbuf, cache_sem, *, sb, layer):
    n_steps = pl.num_programs(0) * pl.num_programs(1)
    step = pl.program_id(0) * pl.num_programs(1) + pl.program_id(1)

    def cache_copy(which, t):
        src, buf = ((ck_hbm, kbuf), (cv_hbm, vbuf))[which]
        slot = t % CACHE_RING
        return pltpu.make_async_copy(src.at[layer, pl.ds(t * sb, sb)], buf.at[slot], cache_sem.at[which, slot])

    @pl.when(step == 0)
    def _():
        for t in range(CACHE_RING - 1):
            cache_copy(0, t).start()
            cache_copy(1, t).start()

    @pl.when(step + CACHE_RING - 1 < n_steps)
    def _():
        cache_copy(0, step + CACHE_RING - 1).start()
        cache_copy(1, step + CACHE_RING - 1).start()

    cache_copy(0, step).wait()
    cache_copy(1, step).wait()
    ck_ref = kbuf.at[step % CACHE_RING]
    cv_ref = vbuf.at[step % CACHE_RING]

    r0 = pl.program_id(1) * sb
    erow = lax.broadcasted_iota(jnp.int32, (GLA_DK, GLA_DK), 0)
    ecol = lax.broadcasted_iota(jnp.int32, (GLA_DK, GLA_DK), 1)
    eye = erow == ecol

    def to_col(x):
        return jnp.sum(jnp.where(eye, jnp.broadcast_to(x, (GLA_DK, GLA_DK)), 0.0), axis=1, keepdims=True)

    for i in range(sb):
        r = pl.ds(r0 + i, 1)
        la = la_ref[r, :]
        qk = p_ref[SLAB_QK, r, :]
        vv = p_ref[SLAB_V, r, :]
        gla_g = p_ref[SLAB_GLA_G, r, :]
        for h in range(GLA_HEADS):
            kc = slice(h * GLA_DK, (h + 1) * GLA_DK)
            kc2 = slice(GLA_KEY_WIDTH + h * GLA_DK, GLA_KEY_WIDTH + (h + 1) * GLA_DK)
            vc = slice(h * GLA_DV, (h + 1) * GLA_DV)
            a_col = to_col(jnp.exp(la[:, kc]))
            q_col = to_col(qk[:, kc] * (GLA_DK ** -0.5))
            k_col = to_col(qk[:, kc2])
            s_new = a_col * s0_ref[0, i, h] + k_col * vv[:, vc]
            s_out_ref[0, i, h] = s_new
            o = jnp.sum(q_col * s_new, axis=0, keepdims=True)
            br_ref[0, r, vc] = _rms(o, gg_ref[:, vc]) * _silu(gla_g[:, vc])

        u = p_ref[SLAB_U, r, :]
        for g, w in enumerate(POOL_WINDOWS):
            cs = slice(g * POOL_GROUP_DIM, (g + 1) * POOL_GROUP_DIM)
            past = jnp.sum(hist_ref[0, POOL_HIST - (w - 1):POOL_HIST, r, cs], axis=0)
            diff_ref[r, cs] = (u[:, cs] + past) / float(w) - u[:, cs]
        hist_out_ref[0, 0:POOL_HIST - 1, r, :] = hist_ref[0, 1:POOL_HIST, r, :]
        hist_out_ref[0, POOL_HIST - 1, r, :] = u

        xq = p_ref[SLAB_XQ, r, :]
        xg = p_ref[SLAB_XG, r, :]
        half_cols = [slice(h * XA_HEAD_DIM + j * LANES, h * XA_HEAD_DIM + (j + 1) * LANES)
                     for j in range(XA_HEAD_DIM // LANES) for h in range(XA_HEADS)]
        xq_rows = jnp.concatenate([xq[:, cs] for cs in half_cols], axis=0)
        n_mem = ck_ref.shape[1]
        prod = (ck_ref[i] * xq_rows[None]).reshape(n_mem * SUBLANES, LANES).astype(BF)
        part = _dot(prod, jnp.ones((LANES, LANES), BF)).reshape(n_mem, SUBLANES, LANES)
        s = (part + pltpu.roll(part, XA_HEADS, axis=1)) * (XA_HEAD_DIM ** -0.5)
        p = jnp.exp(s - jnp.max(s, axis=0, keepdims=True))
        o = jnp.sum(p * cv_ref[i], axis=0) / jnp.sum(p, axis=0)
        halves = XA_HEAD_DIM // LANES
        o_row = jnp.concatenate([o[j * XA_HEADS + h:j * XA_HEADS + h + 1, :]
                                 for h in range(XA_HEADS) for j in range(halves)], axis=1)
        br_ref[2, r, :] = o_row * _silu(xg)

    @pl.when(pl.program_id(1) == pl.num_programs(1) - 1)
    def _():
        for g in range(len(POOL_WINDOWS)):
            cs = slice(g * POOL_GROUP_DIM, (g + 1) * POOL_GROUP_DIM)
            mixed = _dot(diff_ref[:, cs].astype(BF), pw_ref[g]) * ps_ref[:, cs]
            br_ref[1, :, cs] = mixed * _silu(p_ref[SLAB_POOL_G, :, cs])


def _cache_rows_view(c):
    depth, nb, n_mem = c.shape[:3]
    halves = XA_HEAD_DIM // LANES
    c = c.reshape(depth, nb, n_mem, XA_HEADS, halves, LANES)
    return c.transpose(0, 1, 2, 4, 3, 5).reshape(depth, nb, n_mem, halves * XA_HEADS, LANES)


def _sample_mix(p, la, s0, hist, ck, cv, pool_w, pool_scale, gla_gain, carry, layer, sb=SAMPLE_BLOCK):
    nb = s0.shape[1]
    n_mem = ck.shape[2]
    ck, cv = _cache_rows_view(ck), _cache_rows_view(cv)
    rb = SUBLANES
    halves = rb // sb
    n_alias = 0 if carry is None else len(carry)
    n_in = 9
    kern = _drop_alias_refs(functools.partial(_sample_mix_kernel, sb=sb, layer=layer), n_in, n_alias)
    any_spec = pl.BlockSpec(memory_space=pl.ANY)
    cache_ring = (CACHE_RING, sb, n_mem, SUBLANES, LANES)
    return pl.pallas_call(
        kern,
        grid=(nb // rb, halves),
        in_specs=[
            pl.BlockSpec((N_MIX_SLABS, rb, D_MODEL), lambda i, j: (0, i, 0)),
            pl.BlockSpec((rb, GLA_KEY_WIDTH), lambda i, j: (i, 0)),
            pl.BlockSpec((1, sb, GLA_HEADS, GLA_DK, GLA_DV), lambda i, j: (layer, i * halves + j, 0, 0, 0)),
            pl.BlockSpec((1, POOL_HIST, rb, D_MODEL), lambda i, j: (layer, 0, i, 0)),
            any_spec, any_spec,
            _const_spec((len(POOL_WINDOWS), POOL_GROUP_DIM, POOL_GROUP_DIM)),
            _const_spec((1, D_MODEL)),
            _const_spec((1, D_MODEL)),
        ] + [any_spec] * n_alias,
        out_specs=[
            pl.BlockSpec((N_BRANCH, rb, D_MODEL), lambda i, j: (0, i, 0)),
            pl.BlockSpec((1, sb, GLA_HEADS, GLA_DK, GLA_DV), lambda i, j: (layer, i * halves + j, 0, 0, 0)),
            pl.BlockSpec((1, POOL_HIST, rb, D_MODEL), lambda i, j: (layer, 0, i, 0)),
        ],
        out_shape=[
            jax.ShapeDtypeStruct((N_BRANCH, nb, D_MODEL), F32),
            jax.ShapeDtypeStruct(s0.shape, F32),
            jax.ShapeDtypeStruct(hist.shape, F32),
        ],
        scratch_shapes=[pltpu.VMEM((rb, D_MODEL), F32), pltpu.VMEM(cache_ring, F32), pltpu.VMEM(cache_ring, F32),
                        pltpu.SemaphoreType.DMA((2, CACHE_RING))],
        input_output_aliases={n_in + a: 1 + a for a in range(n_alias)},
        compiler_params=_params(
            ("arbitrary", "arbitrary"),
            ((N_MIX_SLABS + N_BRANCH + 1, rb, D_MODEL), F32, 2),
            ((sb, GLA_HEADS, GLA_DK, GLA_DV), F32, 4),
            ((POOL_HIST, rb, D_MODEL), F32, 4),
            (cache_ring, F32, 2),
            ((len(POOL_WINDOWS), POOL_GROUP_DIM, POOL_GROUP_DIM), BF, 1)),
        name="sample_mix",
    )(p, la, s0, hist, ck, cv, pool_w, pool_scale, gla_gain, *(carry or ()))


def _merge_kernel(br_ref, p_ref, x_ref, wbr_ref, wo_ref, fg_ref, out_ref, *, final):
    merged = _merge_term(br_ref, p_ref, wbr_ref, 0)
    for n in range(1, N_BRANCH):
        merged = merged + _merge_term(br_ref, p_ref, wbr_ref, n)
    out_ref[...] = _merge_finish(merged, x_ref[...], wo_ref, fg_ref, final)


def _merge_out(br, p, x, w_branch, w_out, final_gain, layer, final):
    m_rows = x.shape[0]
    whole = lambda shape: pl.BlockSpec(shape, lambda i: (0,) * len(shape))
    return pl.pallas_call(
        functools.partial(_merge_kernel, final=final),
        grid=(1,),
        in_specs=[
            whole((N_BRANCH, m_rows, D_MODEL)),
            whole((N_SLABS, m_rows, D_MODEL)),
            whole((m_rows, D_MODEL)),
            _layer_spec((N_BRANCH, D_MODEL // 2, D_MODEL), layer),
            _layer_spec((D_MODEL // 2, D_MODEL), layer),
            _const_spec((1, D_MODEL)),
        ],
        out_specs=whole((m_rows, D_MODEL)),
        out_shape=jax.ShapeDtypeStruct((m_rows, D_MODEL), F32),
        compiler_params=_params(("arbitrary",),
                                ((N_BRANCH + N_SLABS + 2, m_rows, D_MODEL), F32, 2),
                                ((D_MODEL // 2, (N_BRANCH + 1) * D_MODEL), jnp.uint32, 1)),
        name="merge_out",
    )(br, p, x, w_branch, w_out, final_gain)


def kernel(x_prompt, x_sample, mem_prompt, cache_mem_k, cache_mem_v, state_gla, state_pool, w_in, w_a2, b_a, gla_gain, pool_w, pool_scale, w_mk, w_mv, w_branch, w_out, norm_gain, final_gain):
    batch, seq, _ = x_prompt.shape
    nb = x_sample.shape[0]
    n_mem = mem_prompt.shape[1]
    depth = w_in.shape[0]
    tm = PROMPT_TILE
    assert seq % tm == 0 and nb % SUBLANES == 0 and x_prompt.shape[2] == D_MODEL

    xp = x_prompt.reshape(batch * seq, D_MODEL)
    xs = x_sample.reshape(nb, D_MODEL)
    mem = mem_prompt.reshape(batch * n_mem, D_MODEL)
    fgain = final_gain.reshape(1, D_MODEL)

    mk, mv, mk_bf, mv_bf = _kvproj(mem, _pack_matrices(w_mk), _pack_matrices(w_mv), batch, n_mem)

    w_in_t = jnp.swapaxes(w_in, 1, 2)
    pool_rows = jnp.swapaxes(state_pool, 1, 2)
    w_proj = _pack_w_in(w_in_t)
    wb = _pack_matrices(w_branch.reshape(depth * N_BRANCH, D_MODEL, D_MODEL))
    wb = wb.reshape(depth, N_BRANCH, D_MODEL // 2, D_MODEL)
    wo = _pack_matrices(w_out)

    carry_p, carry_s = None, None
    for l in range(depth):
        final = l == depth - 1
        w_alr = w_in_t[l, ALR_START:ALR_START + GATE_RANK, :]
        wa2 = w_a2[l].astype(BF)
        ba = b_a[l].reshape(1, GLA_KEY_WIDTH)
        ngain = norm_gain[l].reshape(1, D_MODEL)
        ggain = gla_gain[l].reshape(1, D_MODEL)
        pscale = pool_scale[l].reshape(1, D_MODEL)
        pw = pool_w[l].astype(BF)

        xp, s_all, hist_all = _prompt_layer(xp, ngain, w_proj, w_alr, wa2, ba, mk_bf, mv_bf, pw, pscale,
                                            ggain, wb, wo, fgain, carry_p, l, depth, batch, seq, n_mem, tm, final)
        carry_p = (s_all, hist_all)

        ps, las = _inproj(xs, ngain, w_proj, w_alr, wa2, ba, l, nb, F32)
        brs, s_new, hist_new = _sample_mix(ps, las, state_gla, pool_rows, cache_mem_k, cache_mem_v,
                                           pw, pscale, ggain, carry_s, l)
        carry_s = (s_new, hist_new)
        xs = _merge_out(brs, ps, xs, wb, wo, fgain, l, final)

    return (xp.reshape(batch, seq, D_MODEL), xs.reshape(nb, 1, D_MODEL),
            mk, mv, carry_p[0], carry_p[1], carry_s[0], jnp.swapaxes(carry_s[1], 1, 2))
```

```python
import functools
import math

import jax
import jax.numpy as jnp
from jax import lax
from jax.experimental import pallas as pl
from jax.experimental.pallas import tpu as pltpu

D_MODEL = 1024
GLA_HEADS = 4
GLA_DK = 128
GLA_DV = 256
GLA_KEY_WIDTH = GLA_HEADS * GLA_DK
GATE_RANK = 16
GATE_TAU = 16.0
CHUNK = 128
POOL_WINDOWS = (2, 4, 8, 16)
POOL_GROUP_DIM = 256
POOL_HIST = 15
HIST_PAD = 16
XA_HEADS = 4
XA_HEAD_DIM = 256
N_BRANCH = 3
EPS = 1e-6
SUBLANES = 8
LANES = 128

SLAB_QK, SLAB_V, SLAB_GLA_G, SLAB_U, SLAB_POOL_G, SLAB_XQ, SLAB_XG, SLAB_MERGE = 0, 1, 2, 3, 4, 5, 6, 7
N_SLABS = 10
N_MIX_SLABS = 7
N_HEAD_SLABS = 3
ALR_START = N_HEAD_SLABS * D_MODEL

BF = jnp.bfloat16
F32 = jnp.float32
MIB = 1 << 20

PROMPT_TILE = 256
SAMPLE_BLOCK = 4
CACHE_RING = 3
VMEM_COMPILER_SCRATCH = 8 * MIB


def _dot(a, b):
    return jnp.dot(a, b, preferred_element_type=F32)


def _dot_nt(a, b):
    return lax.dot_general(a, b, (((1,), (1,)), ((), ())), preferred_element_type=F32)


def _dot_tn(a, b):
    return lax.dot_general(a, b, (((0,), (0,)), ((), ())), preferred_element_type=F32)


def _pack_rows(w):
    return pltpu.bitcast(w.astype(BF), jnp.uint32)


def _unpack_rows(w_words):
    return pltpu.bitcast(w_words, BF)


def _params(sem, *buffers):
    need = sum(math.prod(shape) * jnp.dtype(dtype).itemsize * copies for shape, dtype, copies in buffers)
    return pltpu.CompilerParams(dimension_semantics=sem, vmem_limit_bytes=need + VMEM_COMPILER_SCRATCH)


def _pack_w_in_kernel(a_ref, b_ref, o_ref):
    j = pl.program_id(1)

    @pl.when(j < N_HEAD_SLABS)
    def _():
        o_ref[0] = _pack_rows(a_ref[0].T)

    @pl.when(j >= N_HEAD_SLABS)
    def _():
        o_ref[0] = _pack_rows(jnp.concatenate([a_ref[0, GATE_RANK:, :], b_ref[0]], axis=0).T)


def _pack_w_in(w_in_t):
    depth = w_in_t.shape[0]
    return pl.pallas_call(
        _pack_w_in_kernel,
        grid=(depth, N_SLABS),
        in_specs=[
            pl.BlockSpec((1, D_MODEL, D_MODEL), lambda l, j: (l, j, 0)),
            pl.BlockSpec((1, GATE_RANK, D_MODEL), lambda l, j: (l, (j + 1) * (D_MODEL // GATE_RANK), 0)),
        ],
        out_specs=pl.BlockSpec((1, D_MODEL // 2, D_MODEL), lambda l, j: (l, 0, j)),
        out_shape=jax.ShapeDtypeStruct((depth, D_MODEL // 2, N_SLABS * D_MODEL), jnp.uint32),
        compiler_params=_params(("parallel", "parallel"),
                                ((D_MODEL, D_MODEL), F32, 2), ((GATE_RANK, D_MODEL), F32, 2),
                                ((D_MODEL // 2, D_MODEL), jnp.uint32, 2)),
        name="pack_w_in",
    )(w_in_t, w_in_t)


def _pack_matrices_kernel(a_ref, o_ref):
    o_ref[0] = _pack_rows(a_ref[0])


def _pack_matrices(w):
    return pl.pallas_call(
        _pack_matrices_kernel,
        grid=(w.shape[0],),
        in_specs=[pl.BlockSpec((1, D_MODEL, D_MODEL), lambda r: (r, 0, 0))],
        out_specs=pl.BlockSpec((1, D_MODEL // 2, D_MODEL), lambda r: (r, 0, 0)),
        out_shape=jax.ShapeDtypeStruct((w.shape[0], D_MODEL // 2, D_MODEL), jnp.uint32),
        compiler_params=_params(("parallel",), ((D_MODEL, D_MODEL), F32, 2), ((D_MODEL // 2, D_MODEL), jnp.uint32, 2)),
        name="pack_matrices",
    )(w)


def _silu(x):
    return x * jax.nn.sigmoid(x)


def _rms(x, gain):
    ms = jnp.mean(x * x, axis=-1, keepdims=True)
    return x * lax.rsqrt(ms + EPS) * gain


def _const_spec(shape):
    zeros = (0,) * len(shape)
    return pl.BlockSpec(shape, lambda *_: zeros, pipeline_mode=pl.Buffered(1))


def _layer_spec(shape, layer):
    index = (layer,) + (0,) * len(shape)
    return pl.BlockSpec((1,) + tuple(shape), lambda *_: index, pipeline_mode=pl.Buffered(1))


def _drop_alias_refs(body, n_in, n_alias):
    def kern(*refs):
        return body(*refs[:n_in], *refs[n_in + n_alias:])
    return kern


def _kvproj_kernel(m_ref, wk_ref, wv_ref, k_ref, v_ref, kb_ref, vb_ref):
    m = m_ref[...].astype(BF)
    k = _dot(m, wk_ref[0])
    v = _dot(m, wv_ref[0])
    for h in range(XA_HEADS):
        cs = slice(h * XA_HEAD_DIM, (h + 1) * XA_HEAD_DIM)
        k_ref[0, 0, :, h, :] = k[:, cs]
        v_ref[0, 0, :, h, :] = v[:, cs]
    kb_ref[0] = k.astype(BF)
    vb_ref[0] = v.astype(BF)


def _kvproj(mem, wk, wv, batch, n_mem):
    depth = wk.shape[0]
    w_spec = pl.BlockSpec((1, D_MODEL, D_MODEL), lambda l, b: (l, 0, 0))
    out5 = pl.BlockSpec((1, 1, n_mem, XA_HEADS, XA_HEAD_DIM), lambda l, b: (l, b, 0, 0, 0))
    out_bf = pl.BlockSpec((1, n_mem, D_MODEL), lambda l, b: (l, b, 0))
    return pl.pallas_call(
        _kvproj_kernel,
        grid=(depth, batch),
        in_specs=[pl.BlockSpec((n_mem, D_MODEL), lambda l, b: (b, 0)), w_spec, w_spec],
        out_specs=[out5, out5, out_bf, out_bf],
        out_shape=[jax.ShapeDtypeStruct((depth, batch, n_mem, XA_HEADS, XA_HEAD_DIM), F32)] * 2
        + [jax.ShapeDtypeStruct((depth, batch * n_mem, D_MODEL), BF)] * 2,
        compiler_params=_params(("parallel", "parallel"),
                                ((n_mem, D_MODEL), F32, 2), ((D_MODEL, D_MODEL), BF, 4),
                                ((n_mem, D_MODEL), F32, 4), ((n_mem, D_MODEL), BF, 4)),
        name="kvproj",
    )(mem, wk, wv)


def _inproj_slab(h, w_ref, p_out, j):
    w = w_ref[0, :, j * D_MODEL:(j + 1) * D_MODEL]
    p_out[j] = _dot(h, _unpack_rows(w)).astype(p_out.dtype)


def _inproj_gate_lowrank(h, walr_ref):
    return _dot_nt(h, walr_ref[...].astype(BF)).astype(BF)


def _inproj_gate(alr, wa2_ref, ba_ref, la_out):
    z = _dot(alr, wa2_ref[...]) + ba_ref[...]
    la_out[...] = (jnp.minimum(z, 0.0) - jnp.log(1.0 + jnp.exp(-jnp.abs(z)))) * (1.0 / GATE_TAU)


def _merge_term(br_ref, p, wbr_ref, n):
    return jax.nn.sigmoid(p[SLAB_MERGE + n].astype(F32)) * _dot(br_ref[n].astype(BF), _unpack_rows(wbr_ref[0, n]))


def _merge_finish(merged, x, wo_ref, fg_ref, final):
    x_new = x + _dot(merged.astype(BF), _unpack_rows(wo_ref[0]))
    return _rms(x_new, fg_ref[...]) if final else x_new


def _chunk_cumsum_matrix(tm):
    row = lax.broadcasted_iota(jnp.int32, (tm, tm), 0)
    col = lax.broadcasted_iota(jnp.int32, (tm, tm), 1)
    return (((row // CHUNK) == (col // CHUNK)) & (row >= col)).astype(BF)


def _window_matrices(tm):
    row = lax.broadcasted_iota(jnp.int32, (tm, tm), 0)
    col = lax.broadcasted_iota(jnp.int32, (tm, tm), 1)
    return jnp.stack([((row >= col) & (row - col < w)).astype(BF) for w in POOL_WINDOWS])


def _gla_cumdecay(cum_ref, la_ref):
    la = la_ref[...]
    la_hi = la.astype(BF)
    la_lo = (la - la_hi.astype(F32)).astype(BF)
    return _dot(cum_ref[...], la_hi) + _dot(cum_ref[...], la_lo)


def _gla_chunk(p, bcum, gg_ref, br_ref, st_ref, c, filler):
    crow = lax.broadcasted_iota(jnp.int32, (CHUNK, CHUNK), 0)
    ccol = lax.broadcasted_iota(jnp.int32, (CHUNK, CHUNK), 1)
    causal = crow >= ccol
    rows = slice(c * CHUNK, (c + 1) * CHUNK)
    heads = range(GLA_HEADS)
    q_dec, k_end, decay, att = [], [], [], []
    for h in heads:
        kc = slice(h * GLA_DK, (h + 1) * GLA_DK)
        kc2 = slice(GLA_KEY_WIDTH + h * GLA_DK, GLA_KEY_WIDTH + (h + 1) * GLA_DK)
        b = bcum[rows, kc]
        b_mid = b[CHUNK // 2 - 1:CHUNK // 2, :]
        b_last = b[CHUNK - 1:CHUNK, :]
        q = p[SLAB_QK, rows, kc].astype(F32) * (GLA_DK ** -0.5)
        k = p[SLAB_QK, rows, kc2].astype(F32)
        q_dec.append((q * jnp.exp(b)).astype(BF))
        q_mid = (q * jnp.exp(b - b_mid)).astype(BF)
        k_mid = (k * jnp.exp(b_mid - b)).astype(BF)
        k_end.append((k * jnp.exp(b_last - b)).astype(BF))
        decay.append(jnp.exp(b_last))
        att.append(_dot_nt(q_mid, k_mid))
    filler()
    o = []
    for h in heads:
        vc = slice(h * GLA_DV, (h + 1) * GLA_DV)
        a = jnp.where(causal, att[h], 0.0).astype(BF)
        s_before = st_ref[h].T.astype(BF)
        o.append(_dot(jnp.concatenate([a, q_dec[h]], axis=1),
                      jnp.concatenate([p[SLAB_V, rows, vc], s_before], axis=0)))
    for h in heads:
        vc = slice(h * GLA_DV, (h + 1) * GLA_DV)
        st_ref[h] = decay[h] * st_ref[h] + _dot_tn(k_end[h], p[SLAB_V, rows, vc]).T
    for h in heads:
        vc = slice(h * GLA_DV, (h + 1) * GLA_DV)
        g = p[SLAB_GLA_G, rows, vc].astype(F32)
        br_ref[0, rows, vc] = (_rms(o[h], gg_ref[:, vc]) * _silu(g)).astype(BF)


def _pool_window_sums(p, win_ref):
    sums = []
    for g in range(len(POOL_WINDOWS)):
        cs = slice(g * POOL_GROUP_DIM, (g + 1) * POOL_GROUP_DIM)
        sums.append(_dot(win_ref[g], p[SLAB_U, :, cs]))
    return sums


def _pool_branch(p, sums, pw_ref, ps_ref, br_ref, ubuf_ref, t, tm):
    u = p[SLAB_U].astype(F32)
    ubuf_ref[HIST_PAD:2 * HIST_PAD, :] = u[0:HIST_PAD]
    pos = t * tm + lax.broadcasted_iota(jnp.int32, (tm, 1), 0)
    for g, w in enumerate(POOL_WINDOWS):
        cs = slice(g * POOL_GROUP_DIM, (g + 1) * POOL_GROUP_DIM)
        ug = u[:, cs]
        head = ug[0:HIST_PAD]
        for j in range(1, w):
            head = head + ubuf_ref[HIST_PAD - j:2 * HIST_PAD - j, cs]
        s = jnp.concatenate([head, sums[g][HIST_PAD:]], axis=0)
        cnt = jnp.minimum(w, pos + 1).astype(F32)
        diff = s / cnt - ug
        mixed = _dot(diff.astype(BF), pw_ref[g]) * ps_ref[:, cs]
        pg = p[SLAB_POOL_G, :, cs].astype(F32)
        br_ref[1, :, cs] = (mixed * _silu(pg)).astype(BF)
    ubuf_ref[0:HIST_PAD, :] = u[tm - HIST_PAD:tm]


def _xattn_probs(p, mk_ref):
    out = []
    for h in range(XA_HEADS):
        cs = slice(h * XA_HEAD_DIM, (h + 1) * XA_HEAD_DIM)
        s = _dot_nt(p[SLAB_XQ, :, cs], mk_ref[0, :, cs]) * (XA_HEAD_DIM ** -0.5)
        pr = jnp.exp(s - jnp.max(s, axis=-1, keepdims=True))
        out.append((pr.astype(BF), jnp.sum(pr, axis=-1, keepdims=True)))
    return out


def _xattn_branch(p, probs, mv_ref, br_ref):
    for h in range(XA_HEADS):
        cs = slice(h * XA_HEAD_DIM, (h + 1) * XA_HEAD_DIM)
        pr, denom = probs[h]
        o = _dot(pr, mv_ref[0, :, cs]) / denom
        xg = p[SLAB_XG, :, cs].astype(F32)
        br_ref[2, :, cs] = (o * _silu(xg)).astype(BF)


def _prompt_layer_kernel(xn_ref, xc_ref, g_ref, w_ref, walr_ref, wa2_ref, ba_ref, cum_ref, win_ref,
                         mk_ref, mv_ref, pw_ref, ps_ref, gg_ref, wbr_ref, wo_ref, fg_ref,
                         out_ref, s_out_ref, hist_out_ref,
                         h_scr, p_scr, la_scr, br_scr, st_ref, ubuf_ref, *, tm, nt, final):
    s = pl.program_id(0)
    t = jnp.maximum(s - 1, 0) % nt
    slot_w = s % 2
    slot_r = 1 - slot_w

    @pl.when(s == 0)
    def _():
        p_scr[1] = jnp.zeros(p_scr.shape[1:], p_scr.dtype)
        la_scr[1] = jnp.zeros(la_scr.shape[1:], la_scr.dtype)

    @pl.when(t == 0)
    def _():
        st_ref[...] = jnp.zeros_like(st_ref)
        ubuf_ref[0:HIST_PAD, :] = jnp.zeros((HIST_PAD, D_MODEL), F32)

    @pl.when(s == 0)
    def _():
        h_scr[0] = _rms(xc_ref[...], g_ref[...]).astype(BF)

    p_next = p_scr.at[slot_w]
    p = p_scr.at[slot_r]

    slabs = list(range(N_SLABS))

    def filler(n=1):
        for _ in range(n):
            _inproj_slab(h_scr[slot_w], w_ref, p_next, slabs.pop(0))

    alr = _inproj_gate_lowrank(h_scr[slot_w], walr_ref)
    bcum = _gla_cumdecay(cum_ref, la_scr.at[slot_r])
    filler()
    _inproj_gate(alr, wa2_ref, ba_ref, la_scr.at[slot_w])
    per_site = (N_SLABS - 2) // (2 * (tm // CHUNK))
    probs, sums = None, None
    for c in range(tm // CHUNK):
        _gla_chunk(p, bcum, gg_ref, br_scr, st_ref, c, functools.partial(filler, per_site))
        filler(per_site)
        if c == 0:
            probs = _xattn_probs(p, mk_ref)
            sums = _pool_window_sums(p, win_ref)
        elif c == 1:
            _xattn_branch(p, probs, mv_ref, br_scr)
            _pool_branch(p, sums, pw_ref, ps_ref, br_scr, ubuf_ref, t, tm)
    h_scr[slot_r] = _rms(xn_ref[...], g_ref[...]).astype(BF)
    merged = _merge_term(br_scr, p, wbr_ref, 2)
    merged = merged + _merge_term(br_scr, p, wbr_ref, 1)
    merged = merged + _merge_term(br_scr, p, wbr_ref, 0)
    filler()
    assert not slabs
    out_ref[...] = _merge_finish(merged, xc_ref[...], wo_ref, fg_ref, final)

    @pl.when((s > 0) & (t == nt - 1))
    def _():
        for h in range(GLA_HEADS):
            s_out_ref[0, 0, h] = st_ref[h].T
        hist_out_ref[0, 0] = ubuf_ref[1:HIST_PAD, :]


def _prompt_layer(x, ngain, w_proj, w_alr, wa2, ba, mk, mv, pool_w, pool_scale, gla_gain,
                  w_branch, w_out, fgain, carry, layer, depth, batch, seq, n_mem, tm, final):
    nt = seq // tm
    n_tiles = batch * nt
    assert tm // CHUNK == 2 and (N_SLABS - 2) % (2 * (tm // CHUNK)) == 0, "slab placement assumes two GLA chunks"
    n_alias = 0 if carry is None else len(carry)
    n_in = 17
    kern = _drop_alias_refs(functools.partial(_prompt_layer_kernel, tm=tm, nt=nt, final=final), n_in, n_alias)
    any_spec = pl.BlockSpec(memory_space=pl.ANY)
    scratch = [
        ((2, tm, D_MODEL), BF, 1),
        ((2, N_SLABS, tm, D_MODEL), BF, 1),
        ((2, tm, GLA_KEY_WIDTH), F32, 1),
        ((N_BRANCH, tm, D_MODEL), BF, 1),
        ((GLA_HEADS, GLA_DV, GLA_DK), F32, 1),
        ((2 * HIST_PAD, D_MODEL), F32, 1),
    ]

    def cur(s):
        return jnp.maximum(s - 1, 0)

    def seq_of(s):
        return cur(s) // nt

    return pl.pallas_call(
        kern,
        grid=(n_tiles + 1,),
        in_specs=[
            pl.BlockSpec((tm, D_MODEL), lambda s: (jnp.minimum(s + 1, n_tiles - 1), 0)),
            pl.BlockSpec((tm, D_MODEL), lambda s: (cur(s), 0)),
            _const_spec((1, D_MODEL)),
            _layer_spec((D_MODEL // 2, N_SLABS * D_MODEL), layer),
            _const_spec((GATE_RANK, D_MODEL)),
            _const_spec((GATE_RANK, GLA_KEY_WIDTH)),
            _const_spec((1, GLA_KEY_WIDTH)),
            _const_spec((tm, tm)),
            _const_spec((len(POOL_WINDOWS), tm, tm)),
            pl.BlockSpec((1, n_mem, D_MODEL), lambda s: (layer, seq_of(s), 0)),
            pl.BlockSpec((1, n_mem, D_MODEL), lambda s: (layer, seq_of(s), 0)),
            _const_spec((len(POOL_WINDOWS), POOL_GROUP_DIM, POOL_GROUP_DIM)),
            _const_spec((1, D_MODEL)),
            _const_spec((1, D_MODEL)),
            _layer_spec((N_BRANCH, D_MODEL // 2, D_MODEL), layer),
            _layer_spec((D_MODEL // 2, D_MODEL), layer),
            _const_spec((1, D_MODEL)),
        ] + [any_spec] * n_alias,
        out_specs=[
            pl.BlockSpec((tm, D_MODEL), lambda s: (cur(s), 0)),
            pl.BlockSpec((1, 1, GLA_HEADS, GLA_DK, GLA_DV), lambda s: (layer, seq_of(s), 0, 0, 0)),
            pl.BlockSpec((1, 1, POOL_HIST, D_MODEL), lambda s: (layer, seq_of(s), 0, 0)),
        ],
        out_shape=[
            jax.ShapeDtypeStruct((n_tiles * tm, D_MODEL), F32),
            jax.ShapeDtypeStruct((depth, batch, GLA_HEADS, GLA_DK, GLA_DV), F32),
            jax.ShapeDtypeStruct((depth, batch, POOL_HIST, D_MODEL), F32),
        ],
        scratch_shapes=[pltpu.VMEM(shape, dtype) for shape, dtype, _ in scratch],
        input_output_aliases={n_in + a: 1 + a for a in range(n_alias)},
        compiler_params=_params(
            ("arbitrary",), *scratch,
            ((tm, D_MODEL), F32, 6),
            ((D_MODEL // 2, (N_SLABS + N_BRANCH + 1) * D_MODEL), jnp.uint32, 1),
            ((n_mem, D_MODEL), BF, 4),
            ((tm, tm), BF, 1 + len(POOL_WINDOWS)),
            ((len(POOL_WINDOWS), POOL_GROUP_DIM, POOL_GROUP_DIM), BF, 1),
            ((GLA_HEADS, GLA_DK, GLA_DV), F32, 2)),
        name="prompt_layer",
    )(x, x, ngain, w_proj, w_alr, wa2, ba, _chunk_cumsum_matrix(tm), _window_matrices(tm),
      mk, mv, pool_w, pool_scale, gla_gain,
      w_branch, w_out, fgain, *(carry or ()))


def _inproj_kernel(x_ref, g_ref, w_ref, walr_ref, wa2_ref, ba_ref, p_ref, la_ref):
    h = _rms(x_ref[...], g_ref[...]).astype(BF)
    alr = _inproj_gate_lowrank(h, walr_ref)
    for j in range(N_SLABS):
        _inproj_slab(h, w_ref, p_ref, j)
    _inproj_gate(alr, wa2_ref, ba_ref, la_ref)


def _inproj(x, gain, w_proj, w_alr, w_a2, b_a, layer, tm, out_dtype):
    m_rows = x.shape[0]
    return pl.pallas_call(
        _inproj_kernel,
        grid=(m_rows // tm,),
        in_specs=[
            pl.BlockSpec((tm, D_MODEL), lambda i: (i, 0)),
            _const_spec((1, D_MODEL)),
            _layer_spec((D_MODEL // 2, N_SLABS * D_MODEL), layer),
            _const_spec((GATE_RANK, D_MODEL)),
            _const_spec((GATE_RANK, GLA_KEY_WIDTH)),
            _const_spec((1, GLA_KEY_WIDTH)),
        ],
        out_specs=[
            pl.BlockSpec((N_SLABS, tm, D_MODEL), lambda i: (0, i, 0)),
            pl.BlockSpec((tm, GLA_KEY_WIDTH), lambda i: (i, 0)),
        ],
        out_shape=[
            jax.ShapeDtypeStruct((N_SLABS, m_rows, D_MODEL), out_dtype),
            jax.ShapeDtypeStruct((m_rows, GLA_KEY_WIDTH), F32),
        ],
        compiler_params=_params(("parallel",),
                                ((tm, D_MODEL), F32, 2), ((D_MODEL // 2, N_SLABS * D_MODEL), jnp.uint32, 1),
                                ((N_SLABS, tm, D_MODEL), out_dtype, 2), ((tm, GLA_KEY_WIDTH), F32, 2)),
        name="inproj",
    )(x, gain, w_proj, w_alr, w_a2, b_a)


def _sample_mix_kernel(p_ref, la_ref, s0_ref, hist_ref, ck_hbm, cv_hbm, pw_ref, ps_ref, gg_ref,
                       br_ref, s_out_ref, hist_out_ref, diff_ref, kbuf, vbuf, cache_sem, *, sb, layer):
    n_steps = pl.num_programs(0) * pl.num_programs(1)
    step = pl.program_id(0) * pl.num_programs(1) + pl.program_id(1)

    def cache_copy(which, t):
        src, buf = ((ck_hbm, kbuf), (cv_hbm, vbuf))[which]
        slot = t % CACHE_RING
        return pltpu.make_async_copy(src.at[layer, pl.ds(t * sb, sb)], buf.at[slot], cache_sem.at[which, slot])

    @pl.when(step == 0)
    def _():
        for t in range(CACHE_RING - 1):
            cache_copy(0, t).start()
            cache_copy(1, t).start()

    @pl.when(step + CACHE_RING - 1 < n_steps)
    def _():
        cache_copy(0, step + CACHE_RING - 1).start()
        cache_copy(1, step + CACHE_RING - 1).start()

    cache_copy(0, step).wait()
    cache_copy(1, step).wait()
    ck_ref = kbuf.at[step % CACHE_RING]
    cv_ref = vbuf.at[step % CACHE_RING]

    r0 = pl.program_id(1) * sb
    erow = lax.broadcasted_iota(jnp.int32, (GLA_DK, GLA_DK), 0)
    ecol = lax.broadcasted_iota(jnp.int32, (GLA_DK, GLA_DK), 1)
    eye = erow == ecol

    def to_col(x):
        return jnp.sum(jnp.where(eye, jnp.broadcast_to(x, (GLA_DK, GLA_DK)), 0.0), axis=1, keepdims=True)

    for i in range(sb):
        r = pl.ds(r0 + i, 1)
        la = la_ref[r, :]
        qk = p_ref[SLAB_QK, r, :]
        vv = p_ref[SLAB_V, r, :]
        gla_g = p_ref[SLAB_GLA_G, r, :]
        for h in range(GLA_HEADS):
            kc = slice(h * GLA_DK, (h + 1) * GLA_DK)
            kc2 = slice(GLA_KEY_WIDTH + h * GLA_DK, GLA_KEY_WIDTH + (h + 1) * GLA_DK)
            vc = slice(h * GLA_DV, (h + 1) * GLA_DV)
            a_col = to_col(jnp.exp(la[:, kc]))
            q_col = to_col(qk[:, kc] * (GLA_DK ** -0.5))
            k_col = to_col(qk[:, kc2])
            s_new = a_col * s0_ref[0, i, h] + k_col * vv[:, vc]
            s_out_ref[0, i, h] = s_new
            o = jnp.sum(q_col * s_new, axis=0, keepdims=True)
            br_ref[0, r, vc] = _rms(o, gg_ref[:, vc]) * _silu(gla_g[:, vc])

        u = p_ref[SLAB_U, r, :]
        for g, w in enumerate(POOL_WINDOWS):
            cs = slice(g * POOL_GROUP_DIM, (g + 1) * POOL_GROUP_DIM)
            past = jnp.sum(hist_ref[0, POOL_HIST - (w - 1):POOL_HIST, r, cs], axis=0)
            diff_ref[r, cs] = (u[:, cs] + past) / float(w) - u[:, cs]
        hist_out_ref[0, 0:POOL_HIST - 1, r, :] = hist_ref[0, 1:POOL_HIST, r, :]
        hist_out_ref[0, POOL_HIST - 1, r, :] = u

        xq = p_ref[SLAB_XQ, r, :]
        xg = p_ref[SLAB_XG, r, :]
        half_cols = [slice(h * XA_HEAD_DIM + j * LANES, h * XA_HEAD_DIM + (j + 1) * LANES)
                     for j in range(XA_HEAD_DIM // LANES) for h in range(XA_HEADS)]
        xq_rows = jnp.concatenate([xq[:, cs] for cs in half_cols], axis=0)
        n_mem = ck_ref.shape[1]
        prod = (ck_ref[i] * xq_rows[None]).reshape(n_mem * SUBLANES, LANES).astype(BF)
        part = _dot(prod, jnp.ones((LANES, LANES), BF)).reshape(n_mem, SUBLANES, LANES)
        s = (part + pltpu.roll(part, XA_HEADS, axis=1)) * (XA_HEAD_DIM ** -0.5)
        p = jnp.exp(s - jnp.max(s, axis=0, keepdims=True))
        o = jnp.sum(p * cv_ref[i], axis=0) / jnp.sum(p, axis=0)
        halves = XA_HEAD_DIM // LANES
        o_row = jnp.concatenate([o[j * XA_HEADS + h:j * XA_HEADS + h + 1, :]
                                 for h in range(XA_HEADS) for j in range(halves)], axis=1)
        br_ref[2, r, :] = o_row * _silu(xg)

    @pl.when(pl.program_id(1) == pl.num_programs(1) - 1)
    def _():
        for g in range(len(POOL_WINDOWS)):
            cs = slice(g * POOL_GROUP_DIM, (g + 1) * POOL_GROUP_DIM)
            mixed = _dot(diff_ref[:, cs].astype(BF), pw_ref[g]) * ps_ref[:, cs]
            br_ref[1, :, cs] = mixed * _silu(p_ref[SLAB_POOL_G, :, cs])


def _cache_rows_view(c):
    depth, nb, n_mem = c.shape[:3]
    halves = XA_HEAD_DIM // LANES
    c = c.reshape(depth, nb, n_mem, XA_HEADS, halves, LANES)
    return c.transpose(0, 1, 2, 4, 3, 5).reshape(depth, nb, n_mem, halves * XA_HEADS, LANES)


def _sample_mix(p, la, s0, hist, ck, cv, pool_w, pool_scale, gla_gain, carry, layer, sb=SAMPLE_BLOCK):
    nb = s0.shape[1]
    n_mem = ck.shape[2]
    ck, cv = _cache_rows_view(ck), _cache_rows_view(cv)
    rb = SUBLANES
    halves = rb // sb
    n_alias = 0 if carry is None else len(carry)
    n_in = 9
    kern = _drop_alias_refs(functools.partial(_sample_mix_kernel, sb=sb, layer=layer), n_in, n_alias)
    any_spec = pl.BlockSpec(memory_space=pl.ANY)
    cache_ring = (CACHE_RING, sb, n_mem, SUBLANES, LANES)
    return pl.pallas_call(
        kern,
        grid=(nb // rb, halves),
        in_specs=[
            pl.BlockSpec((N_MIX_SLABS, rb, D_MODEL), lambda i, j: (0, i, 0)),
            pl.BlockSpec((rb, GLA_KEY_WIDTH), lambda i, j: (i, 0)),
            pl.BlockSpec((1, sb, GLA_HEADS, GLA_DK, GLA_DV), lambda i, j: (layer, i * halves + j, 0, 0, 0)),
            pl.BlockSpec((1, POOL_HIST, rb, D_MODEL), lambda i, j: (layer, 0, i, 0)),
            any_spec, any_spec,
            _const_spec((len(POOL_WINDOWS), POOL_GROUP_DIM, POOL_GROUP_DIM)),
            _const_spec((1, D_MODEL)),
            _const_spec((1, D_MODEL)),
        ] + [any_spec] * n_alias,
        out_specs=[
            pl.BlockSpec((N_BRANCH, rb, D_MODEL), lambda i, j: (0, i, 0)),
            pl.BlockSpec((1, sb, GLA_HEADS, GLA_DK, GLA_DV), lambda i, j: (layer, i * halves + j, 0, 0, 0)),
            pl.BlockSpec((1, POOL_HIST, rb, D_MODEL), lambda i, j: (layer, 0, i, 0)),
        ],
        out_shape=[
            jax.ShapeDtypeStruct((N_BRANCH, nb, D_MODEL), F32),
            jax.ShapeDtypeStruct(s0.shape, F32),
            jax.ShapeDtypeStruct(hist.shape, F32),
        ],
        scratch_shapes=[pltpu.VMEM((rb, D_MODEL), F32), pltpu.VMEM(cache_ring, F32), pltpu.VMEM(cache_ring, F32),
                        pltpu.SemaphoreType.DMA((2, CACHE_RING))],
        input_output_aliases={n_in + a: 1 + a for a in range(n_alias)},
        compiler_params=_params(
            ("arbitrary", "arbitrary"),
            ((N_MIX_SLABS + N_BRANCH + 1, rb, D_MODEL), F32, 2),
            ((sb, GLA_HEADS, GLA_DK, GLA_DV), F32, 4),
            ((POOL_HIST, rb, D_MODEL), F32, 4),
            (cache_ring, F32, 2),
            ((len(POOL_WINDOWS), POOL_GROUP_DIM, POOL_GROUP_DIM), BF, 1)),
        name="sample_mix",
    )(p, la, s0, hist, ck, cv, pool_w, pool_scale, gla_gain, *(carry or ()))


def _merge_kernel(br_ref, p_ref, x_ref, wbr_ref, wo_ref, fg_ref, out_ref, *, final):
    merged = _merge_term(br_ref, p_ref, wbr_ref, 0)
    for n in range(1, N_BRANCH):
        merged = merged + _merge_term(br_ref, p_ref, wbr_ref, n)
    out_ref[...] = _merge_finish(merged, x_ref[...], wo_ref, fg_ref, final)


def _merge_out(br, p, x, w_branch, w_out, final_gain, layer, final):
    m_rows = x.shape[0]
    whole = lambda shape: pl.BlockSpec(shape, lambda i: (0,) * len(shape))
    return pl.pallas_call(
        functools.partial(_merge_kernel, final=final),
        grid=(1,),
        in_specs=[
            whole((N_BRANCH, m_rows, D_MODEL)),
            whole((N_SLABS, m_rows, D_MODEL)),
            whole((m_rows, D_MODEL)),
            _layer_spec((N_BRANCH, D_MODEL // 2, D_MODEL), layer),
            _layer_spec((D_MODEL // 2, D_MODEL), layer),
            _const_spec((1, D_MODEL)),
        ],
        out_specs=whole((m_rows, D_MODEL)),
        out_shape=jax.ShapeDtypeStruct((m_rows, D_MODEL), F32),
        compiler_params=_params(("arbitrary",),
                                ((N_BRANCH + N_SLABS + 2, m_rows, D_MODEL), F32, 2),
                                ((D_MODEL // 2, (N_BRANCH + 1) * D_MODEL), jnp.uint32, 1)),
        name="merge_out",
    )(br, p, x, w_branch, w_out, final_gain)


def kernel(x_prompt, x_sample, mem_prompt, cache_mem_k, cache_mem_v, state_gla, state_pool, w_in, w_a2, b_a, gla_gain, pool_w, pool_scale, w_mk, w_mv, w_branch, w_out, norm_gain, final_gain):
    batch, seq, _ = x_prompt.shape
    nb = x_sample.shape[0]
    n_mem = mem_prompt.shape[1]
    depth = w_in.shape[0]
    tm = PROMPT_TILE
    assert seq % tm == 0 and nb % SUBLANES == 0 and x_prompt.shape[2] == D_MODEL

    xp = x_prompt.reshape(batch * seq, D_MODEL)
    xs = x_sample.reshape(nb, D_MODEL)
    mem = mem_prompt.reshape(batch * n_mem, D_MODEL)
    fgain = final_gain.reshape(1, D_MODEL)

    mk, mv, mk_bf, mv_bf = _kvproj(mem, w_mk.astype(BF), w_mv.astype(BF), batch, n_mem)

    w_in_t = jnp.swapaxes(w_in, 1, 2)
    pool_rows = jnp.swapaxes(state_pool, 1, 2)
    w_proj = _pack_w_in(w_in_t)
    wb = _pack_matrices(w_branch.reshape(depth * N_BRANCH, D_MODEL, D_MODEL))
    wb = wb.reshape(depth, N_BRANCH, D_MODEL // 2, D_MODEL)
    wo = _pack_matrices(w_out)

    carry_p, carry_s = None, None
    for l in range(depth):
        final = l == depth - 1
        w_alr = w_in_t[l, ALR_START:ALR_START + GATE_RANK, :]
        wa2 = w_a2[l].astype(BF)
        ba = b_a[l].reshape(1, GLA_KEY_WIDTH)
        ngain = norm_gain[l].reshape(1, D_MODEL)
        ggain = gla_gain[l].reshape(1, D_MODEL)
        pscale = pool_scale[l].reshape(1, D_MODEL)
        pw = pool_w[l].astype(BF)

        xp, s_all, hist_all = _prompt_layer(xp, ngain, w_proj, w_alr, wa2, ba, mk_bf, mv_bf, pw, pscale,
                                            ggain, wb, wo, fgain, carry_p, l, depth, batch, seq, n_mem, tm, final)
        carry_p = (s_all, hist_all)

        ps, las = _inproj(xs, ngain, w_proj, w_alr, wa2, ba, l, nb, F32)
        brs, s_new, hist_new = _sample_mix(ps, las, state_gla, pool_rows, cache_mem_k, cache_mem_v,
                                           pw, pscale, ggain, carry_s, l)
        carry_s = (s_new, hist_new)
        xs = _merge_out(brs, ps, xs, wb, wo, fgain, l, final)

    return (xp.reshape(batch, seq, D_MODEL), xs.reshape(nb, 1, D_MODEL),
            mk, mv, carry_p[0], carry_p[1], carry_s[0], jnp.swapaxes(carry_s[1], 1, 2))
```

```python
import functools
import math

import jax
import jax.numpy as jnp
from jax import lax
from jax.experimental import pallas as pl
from jax.experimental.pallas import tpu as pltpu

D_MODEL = 1024
GLA_HEADS = 4
GLA_DK = 128
GLA_DV = 256
GLA_KEY_WIDTH = GLA_HEADS * GLA_DK
GATE_RANK = 16
GATE_TAU = 16.0
CHUNK = 128
POOL_WINDOWS = (2, 4, 8, 16)
POOL_GROUP_DIM = 256
POOL_HIST = 15
HIST_PAD = 16
XA_HEADS = 4
XA_HEAD_DIM = 256
N_BRANCH = 3
EPS = 1e-6
SUBLANES = 8
LANES = 128

SLAB_QK, SLAB_V, SLAB_GLA_G, SLAB_U, SLAB_POOL_G, SLAB_XQ, SLAB_XG, SLAB_MERGE = 0, 1, 2, 3, 4, 5, 6, 7
N_SLABS = 10
N_MIX_SLABS = 7
N_HEAD_SLABS = 3
ALR_START = N_HEAD_SLABS * D_MODEL

BF = jnp.bfloat16
F32 = jnp.float32
MIB = 1 << 20

PROMPT_TILE = 256
SAMPLE_BLOCK = 4
CACHE_RING = 3
VMEM_COMPILER_SCRATCH = 8 * MIB


def _dot(a, b):
    return jnp.dot(a, b, preferred_element_type=F32)


def _dot_nt(a, b):
    return lax.dot_general(a, b, (((1,), (1,)), ((), ())), preferred_element_type=F32)


def _dot_tn(a, b):
    return lax.dot_general(a, b, (((0,), (0,)), ((), ())), preferred_element_type=F32)


def _pack_rows(w):
    return pltpu.bitcast(w.astype(BF), jnp.uint32)


def _unpack_rows(w_words):
    return pltpu.bitcast(w_words, BF)


def _params(sem, *buffers):
    need = sum(math.prod(shape) * jnp.dtype(dtype).itemsize * copies for shape, dtype, copies in buffers)
    return pltpu.CompilerParams(dimension_semantics=sem, vmem_limit_bytes=need + VMEM_COMPILER_SCRATCH)


def _pack_w_in_kernel(a_ref, b_ref, o_ref):
    j = pl.program_id(1)

    @pl.when(j < N_HEAD_SLABS)
    def _():
        o_ref[0] = _pack_rows(a_ref[0].T)

    @pl.when(j >= N_HEAD_SLABS)
    def _():
        o_ref[0] = _pack_rows(jnp.concatenate([a_ref[0, GATE_RANK:, :], b_ref[0]], axis=0).T)


def _pack_w_in(w_in_t):
    depth = w_in_t.shape[0]
    return pl.pallas_call(
        _pack_w_in_kernel,
        grid=(depth, N_SLABS),
        in_specs=[
            pl.BlockSpec((1, D_MODEL, D_MODEL), lambda l, j: (l, j, 0)),
            pl.BlockSpec((1, GATE_RANK, D_MODEL), lambda l, j: (l, (j + 1) * (D_MODEL // GATE_RANK), 0)),
        ],
        out_specs=pl.BlockSpec((1, D_MODEL // 2, D_MODEL), lambda l, j: (l, 0, j)),
        out_shape=jax.ShapeDtypeStruct((depth, D_MODEL // 2, N_SLABS * D_MODEL), jnp.uint32),
        compiler_params=_params(("parallel", "parallel"),
                                ((D_MODEL, D_MODEL), F32, 2), ((GATE_RANK, D_MODEL), F32, 2),
                                ((D_MODEL // 2, D_MODEL), jnp.uint32, 2)),
        name="pack_w_in",
    )(w_in_t, w_in_t)


def _pack_matrices_kernel(a_ref, o_ref):
    o_ref[0] = _pack_rows(a_ref[0])


def _pack_matrices(w):
    return pl.pallas_call(
        _pack_matrices_kernel,
        grid=(w.shape[0],),
        in_specs=[pl.BlockSpec((1, D_MODEL, D_MODEL), lambda r: (r, 0, 0))],
        out_specs=pl.BlockSpec((1, D_MODEL // 2, D_MODEL), lambda r: (r, 0, 0)),
        out_shape=jax.ShapeDtypeStruct((w.shape[0], D_MODEL // 2, D_MODEL), jnp.uint32),
        compiler_params=_params(("parallel",), ((D_MODEL, D_MODEL), F32, 2), ((D_MODEL // 2, D_MODEL), jnp.uint32, 2)),
        name="pack_matrices",
    )(w)


def _silu(x):
    return x * jax.nn.sigmoid(x)


def _rms(x, gain):
    ms = jnp.mean(x * x, axis=-1, keepdims=True)
    return x * lax.rsqrt(ms + EPS) * gain


def _const_spec(shape):
    zeros = (0,) * len(shape)
    return pl.BlockSpec(shape, lambda *_: zeros, pipeline_mode=pl.Buffered(1))


def _layer_spec(shape, layer):
    index = (layer,) + (0,) * len(shape)
    return pl.BlockSpec((1,) + tuple(shape), lambda *_: index, pipeline_mode=pl.Buffered(1))


def _drop_alias_refs(body, n_in, n_alias):
    def kern(*refs):
        return body(*refs[:n_in], *refs[n_in + n_alias:])
    return kern


def _kvproj_kernel(m_ref, wk_ref, wv_ref, k_ref, v_ref, kb_ref, vb_ref):
    m = m_ref[...].astype(BF)
    k = _dot(m, wk_ref[0])
    v = _dot(m, wv_ref[0])
    for h in range(XA_HEADS):
        cs = slice(h * XA_HEAD_DIM, (h + 1) * XA_HEAD_DIM)
        k_ref[0, 0, :, h, :] = k[:, cs]
        v_ref[0, 0, :, h, :] = v[:, cs]
    kb_ref[0] = k.astype(BF)
    vb_ref[0] = v.astype(BF)


def _kvproj(mem, wk, wv, batch, n_mem):
    depth = wk.shape[0]
    w_spec = pl.BlockSpec((1, D_MODEL, D_MODEL), lambda l, b: (l, 0, 0))
    out5 = pl.BlockSpec((1, 1, n_mem, XA_HEADS, XA_HEAD_DIM), lambda l, b: (l, b, 0, 0, 0))
    out_bf = pl.BlockSpec((1, n_mem, D_MODEL), lambda l, b: (l, b, 0))
    return pl.pallas_call(
        _kvproj_kernel,
        grid=(depth, batch),
        in_specs=[pl.BlockSpec((n_mem, D_MODEL), lambda l, b: (b, 0)), w_spec, w_spec],
        out_specs=[out5, out5, out_bf, out_bf],
        out_shape=[jax.ShapeDtypeStruct((depth, batch, n_mem, XA_HEADS, XA_HEAD_DIM), F32)] * 2
        + [jax.ShapeDtypeStruct((depth, batch * n_mem, D_MODEL), BF)] * 2,
        compiler_params=_params(("parallel", "parallel"),
                                ((n_mem, D_MODEL), F32, 2), ((D_MODEL, D_MODEL), BF, 4),
                                ((n_mem, D_MODEL), F32, 4), ((n_mem, D_MODEL), BF, 4)),
        name="kvproj",
    )(mem, wk, wv)


def _inproj_slab(h, w_ref, p_out, j):
    w = w_ref[0, :, j * D_MODEL:(j + 1) * D_MODEL]
    p_out[j] = _dot(h, _unpack_rows(w)).astype(p_out.dtype)


def _inproj_gate_lowrank(h, walr_ref):
    return _dot_nt(h, walr_ref[...].astype(BF)).astype(BF)


def _inproj_gate(alr, wa2_ref, ba_ref, la_out):
    z = _dot(alr, wa2_ref[...]) + ba_ref[...]
    la_out[...] = (jnp.minimum(z, 0.0) - jnp.log(1.0 + jnp.exp(-jnp.abs(z)))) * (1.0 / GATE_TAU)


def _merge_term(br_ref, p, wbr_ref, n):
    return jax.nn.sigmoid(p[SLAB_MERGE + n].astype(F32)) * _dot(br_ref[n].astype(BF), _unpack_rows(wbr_ref[0, n]))


def _merge_finish(merged, x, wo_ref, fg_ref, final):
    x_new = x + _dot(merged.astype(BF), _unpack_rows(wo_ref[0]))
    return _rms(x_new, fg_ref[...]) if final else x_new


def _chunk_cumsum_matrix(tm):
    row = lax.broadcasted_iota(jnp.int32, (tm, tm), 0)
    col = lax.broadcasted_iota(jnp.int32, (tm, tm), 1)
    return (((row // CHUNK) == (col // CHUNK)) & (row >= col)).astype(BF)


def _window_matrices(tm):
    row = lax.broadcasted_iota(jnp.int32, (tm, tm), 0)
    col = lax.broadcasted_iota(jnp.int32, (tm, tm), 1)
    return jnp.stack([((row >= col) & (row - col < w)).astype(BF) for w in POOL_WINDOWS])


def _gate_lowrank_t(h, walr_ref):
    return _dot_nt(walr_ref[...].astype(BF), h).astype(BF)


def _gate_t(alr_t, wa2_ref, bat_ref):
    z_t = _dot_tn(wa2_ref[...], alr_t) + bat_ref[...]
    return (jnp.minimum(z_t, 0.0) - jnp.log(1.0 + jnp.exp(-jnp.abs(z_t)))) * (1.0 / GATE_TAU)


def _gla_cumdecay(cum_ref, lat_ref):
    la_t = lat_ref[...]
    la_hi = la_t.astype(BF)
    la_lo = (la_t - la_hi.astype(F32)).astype(BF)
    return _dot_nt(cum_ref[...], la_hi) + _dot_nt(cum_ref[...], la_lo)


def _gla_chunk(p, bcum, gg_ref, br_ref, st_ref, c, filler):
    crow = lax.broadcasted_iota(jnp.int32, (CHUNK, CHUNK), 0)
    ccol = lax.broadcasted_iota(jnp.int32, (CHUNK, CHUNK), 1)
    causal = crow >= ccol
    rows = slice(c * CHUNK, (c + 1) * CHUNK)
    heads = range(GLA_HEADS)
    q_dec, k_end, decay, att = [], [], [], []
    for h in heads:
        kc = slice(h * GLA_DK, (h + 1) * GLA_DK)
        kc2 = slice(GLA_KEY_WIDTH + h * GLA_DK, GLA_KEY_WIDTH + (h + 1) * GLA_DK)
        b = bcum[rows, kc]
        b_mid = b[CHUNK // 2 - 1:CHUNK // 2, :]
        b_last = b[CHUNK - 1:CHUNK, :]
        q = p[SLAB_QK, rows, kc].astype(F32) * (GLA_DK ** -0.5)
        k = p[SLAB_QK, rows, kc2].astype(F32)
        q_dec.append((q * jnp.exp(b)).astype(BF))
        q_mid = (q * jnp.exp(b - b_mid)).astype(BF)
        k_mid = (k * jnp.exp(b_mid - b)).astype(BF)
        k_end.append((k * jnp.exp(b_last - b)).astype(BF))
        decay.append(jnp.exp(b_last))
        att.append(_dot_nt(q_mid, k_mid))
    filler()
    o = []
    for h in heads:
        vc = slice(h * GLA_DV, (h + 1) * GLA_DV)
        a = jnp.where(causal, att[h], 0.0).astype(BF)
        s_before = st_ref[h].T.astype(BF)
        o.append(_dot(jnp.concatenate([a, q_dec[h]], axis=1),
                      jnp.concatenate([p[SLAB_V, rows, vc], s_before], axis=0)))
    for h in heads:
        vc = slice(h * GLA_DV, (h + 1) * GLA_DV)
        st_ref[h] = decay[h] * st_ref[h] + _dot_tn(k_end[h], p[SLAB_V, rows, vc]).T
    for h in heads:
        vc = slice(h * GLA_DV, (h + 1) * GLA_DV)
        g = p[SLAB_GLA_G, rows, vc].astype(F32)
        br_ref[0, rows, vc] = (_rms(o[h], gg_ref[:, vc]) * _silu(g)).astype(BF)


def _pool_window_sums(p, win_ref):
    sums = []
    for g in range(len(POOL_WINDOWS)):
        cs = slice(g * POOL_GROUP_DIM, (g + 1) * POOL_GROUP_DIM)
        sums.append(_dot(win_ref[g], p[SLAB_U, :, cs]))
    return sums


def _pool_branch(p, sums, pw_ref, ps_ref, br_ref, ubuf_ref, t, tm):
    u = p[SLAB_U].astype(F32)
    ubuf_ref[HIST_PAD:2 * HIST_PAD, :] = u[0:HIST_PAD]
    pos = t * tm + lax.broadcasted_iota(jnp.int32, (tm, 1), 0)
    for g, w in enumerate(POOL_WINDOWS):
        cs = slice(g * POOL_GROUP_DIM, (g + 1) * POOL_GROUP_DIM)
        ug = u[:, cs]
        head = ug[0:HIST_PAD]
        for j in range(1, w):
            head = head + ubuf_ref[HIST_PAD - j:2 * HIST_PAD - j, cs]
        s = jnp.concatenate([head, sums[g][HIST_PAD:]], axis=0)
        cnt = jnp.minimum(w, pos + 1).astype(F32)
        diff = s / cnt - ug
        mixed = _dot(diff.astype(BF), pw_ref[g]) * ps_ref[:, cs]
        pg = p[SLAB_POOL_G, :, cs].astype(F32)
        br_ref[1, :, cs] = (mixed * _silu(pg)).astype(BF)
    ubuf_ref[0:HIST_PAD, :] = u[tm - HIST_PAD:tm]


def _xattn_probs(p, mk_ref):
    out = []
    for h in range(XA_HEADS):
        cs = slice(h * XA_HEAD_DIM, (h + 1) * XA_HEAD_DIM)
        s = _dot_nt(p[SLAB_XQ, :, cs], mk_ref[0, :, cs]) * (XA_HEAD_DIM ** -0.5)
        pr = jnp.exp(s - jnp.max(s, axis=-1, keepdims=True))
        out.append((pr.astype(BF), jnp.sum(pr, axis=-1, keepdims=True)))
    return out


def _xattn_branch(p, probs, mv_ref, br_ref):
    for h in range(XA_HEADS):
        cs = slice(h * XA_HEAD_DIM, (h + 1) * XA_HEAD_DIM)
        pr, denom = probs[h]
        o = _dot(pr, mv_ref[0, :, cs]) / denom
        xg = p[SLAB_XG, :, cs].astype(F32)
        br_ref[2, :, cs] = (o * _silu(xg)).astype(BF)


def _prompt_layer_kernel(xn_ref, xc_ref, g_ref, w_ref, walr_ref, wa2_ref, ba_ref, cum_ref, win_ref,
                         mk_ref, mv_ref, pw_ref, ps_ref, gg_ref, wbr_ref, wo_ref, fg_ref,
                         out_ref, s_out_ref, hist_out_ref,
                         h_scr, p_scr, la_scr, br_scr, st_ref, ubuf_ref, *, tm, nt, final):
    s = pl.program_id(0)
    t = jnp.maximum(s - 1, 0) % nt
    slot_w = s % 2
    slot_r = 1 - slot_w

    @pl.when(s == 0)
    def _():
        p_scr[1] = jnp.zeros(p_scr.shape[1:], p_scr.dtype)
        la_scr[1] = jnp.zeros(la_scr.shape[1:], la_scr.dtype)

    @pl.when(t == 0)
    def _():
        st_ref[...] = jnp.zeros_like(st_ref)
        ubuf_ref[0:HIST_PAD, :] = jnp.zeros((HIST_PAD, D_MODEL), F32)

    @pl.when(s == 0)
    def _():
        h_scr[0] = _rms(xc_ref[...], g_ref[...]).astype(BF)

    p_next = p_scr.at[slot_w]
    p = p_scr.at[slot_r]

    slabs = list(range(N_SLABS))

    def filler(n=1):
        for _ in range(n):
            _inproj_slab(h_scr[slot_w], w_ref, p_next, slabs.pop(0))

    alr_t = _gate_lowrank_t(h_scr[slot_w], walr_ref)
    bcum = _gla_cumdecay(cum_ref, la_scr.at[slot_r])
    filler()
    la_scr[slot_w] = _gate_t(alr_t, wa2_ref, ba_ref)
    per_site = (N_SLABS - 2) // (2 * (tm // CHUNK))
    probs, sums = None, None
    for c in range(tm // CHUNK):
        _gla_chunk(p, bcum, gg_ref, br_scr, st_ref, c, functools.partial(filler, per_site))
        filler(per_site)
        if c == 0:
            probs = _xattn_probs(p, mk_ref)
            sums = _pool_window_sums(p, win_ref)
        elif c == 1:
            _xattn_branch(p, probs, mv_ref, br_scr)
            _pool_branch(p, sums, pw_ref, ps_ref, br_scr, ubuf_ref, t, tm)
    h_scr[slot_r] = _rms(xn_ref[...], g_ref[...]).astype(BF)
    merged = _merge_term(br_scr, p, wbr_ref, 2)
    merged = merged + _merge_term(br_scr, p, wbr_ref, 1)
    merged = merged + _merge_term(br_scr, p, wbr_ref, 0)
    filler()
    assert not slabs
    out_ref[...] = _merge_finish(merged, xc_ref[...], wo_ref, fg_ref, final)

    @pl.when((s > 0) & (t == nt - 1))
    def _():
        for h in range(GLA_HEADS):
            s_out_ref[0, 0, h] = st_ref[h].T
        hist_out_ref[0, 0] = ubuf_ref[1:HIST_PAD, :]


def _prompt_layer(x, ngain, w_proj, w_alr, wa2, ba, mk, mv, pool_w, pool_scale, gla_gain,
                  w_branch, w_out, fgain, carry, layer, depth, batch, seq, n_mem, tm, final):
    nt = seq // tm
    n_tiles = batch * nt
    assert tm // CHUNK == 2 and (N_SLABS - 2) % (2 * (tm // CHUNK)) == 0, "slab placement assumes two GLA chunks"
    n_alias = 0 if carry is None else len(carry)
    n_in = 17
    kern = _drop_alias_refs(functools.partial(_prompt_layer_kernel, tm=tm, nt=nt, final=final), n_in, n_alias)
    any_spec = pl.BlockSpec(memory_space=pl.ANY)
    scratch = [
        ((2, tm, D_MODEL), BF, 1),
        ((2, N_SLABS, tm, D_MODEL), BF, 1),
        ((2, GLA_KEY_WIDTH, tm), F32, 1),
        ((N_BRANCH, tm, D_MODEL), BF, 1),
        ((GLA_HEADS, GLA_DV, GLA_DK), F32, 1),
        ((2 * HIST_PAD, D_MODEL), F32, 1),
    ]

    def cur(s):
        return jnp.maximum(s - 1, 0)

    def seq_of(s):
        return cur(s) // nt

    return pl.pallas_call(
        kern,
        grid=(n_tiles + 1,),
        in_specs=[
            pl.BlockSpec((tm, D_MODEL), lambda s: (jnp.minimum(s + 1, n_tiles - 1), 0)),
            pl.BlockSpec((tm, D_MODEL), lambda s: (cur(s), 0)),
            _const_spec((1, D_MODEL)),
            _layer_spec((D_MODEL // 2, N_SLABS * D_MODEL), layer),
            _const_spec((GATE_RANK, D_MODEL)),
            _const_spec((GATE_RANK, GLA_KEY_WIDTH)),
            _const_spec((GLA_KEY_WIDTH, 1)),
            _const_spec((tm, tm)),
            _const_spec((len(POOL_WINDOWS), tm, tm)),
            pl.BlockSpec((1, n_mem, D_MODEL), lambda s: (layer, seq_of(s), 0)),
            pl.BlockSpec((1, n_mem, D_MODEL), lambda s: (layer, seq_of(s), 0)),
            _const_spec((len(POOL_WINDOWS), POOL_GROUP_DIM, POOL_GROUP_DIM)),
            _const_spec((1, D_MODEL)),
            _const_spec((1, D_MODEL)),
            _layer_spec((N_BRANCH, D_MODEL // 2, D_MODEL), layer),
            _layer_spec((D_MODEL // 2, D_MODEL), layer),
            _const_spec((1, D_MODEL)),
        ] + [any_spec] * n_alias,
        out_specs=[
            pl.BlockSpec((tm, D_MODEL), lambda s: (cur(s), 0)),
            pl.BlockSpec((1, 1, GLA_HEADS, GLA_DK, GLA_DV), lambda s: (layer, seq_of(s), 0, 0, 0)),
            pl.BlockSpec((1, 1, POOL_HIST, D_MODEL), lambda s: (layer, seq_of(s), 0, 0)),
        ],
        out_shape=[
            jax.ShapeDtypeStruct((n_tiles * tm, D_MODEL), F32),
            jax.ShapeDtypeStruct((depth, batch, GLA_HEADS, GLA_DK, GLA_DV), F32),
            jax.ShapeDtypeStruct((depth, batch, POOL_HIST, D_MODEL), F32),
        ],
        scratch_shapes=[pltpu.VMEM(shape, dtype) for shape, dtype, _ in scratch],
        input_output_aliases={n_in + a: 1 + a for a in range(n_alias)},
        compiler_params=_params(
            ("arbitrary",), *scratch,
            ((tm, D_MODEL), F32, 6),
            ((D_MODEL // 2, (N_SLABS + N_BRANCH + 1) * D_MODEL), jnp.uint32, 1),
            ((n_mem, D_MODEL), BF, 4),
            ((tm, tm), BF, 1 + len(POOL_WINDOWS)),
            ((len(POOL_WINDOWS), POOL_GROUP_DIM, POOL_GROUP_DIM), BF, 1),
            ((GLA_HEADS, GLA_DK, GLA_DV), F32, 2)),
        name="prompt_layer",
    )(x, x, ngain, w_proj, w_alr, wa2, ba.reshape(GLA_KEY_WIDTH, 1), _chunk_cumsum_matrix(tm), _window_matrices(tm),
      mk, mv, pool_w, pool_scale, gla_gain,
      w_branch, w_out, fgain, *(carry or ()))


def _inproj_kernel(x_ref, g_ref, w_ref, walr_ref, wa2_ref, ba_ref, p_ref, la_ref):
    h = _rms(x_ref[...], g_ref[...]).astype(BF)
    alr = _inproj_gate_lowrank(h, walr_ref)
    for j in range(N_SLABS):
        _inproj_slab(h, w_ref, p_ref, j)
    _inproj_gate(alr, wa2_ref, ba_ref, la_ref)


def _inproj(x, gain, w_proj, w_alr, w_a2, b_a, layer, tm, out_dtype):
    m_rows = x.shape[0]
    return pl.pallas_call(
        _inproj_kernel,
        grid=(m_rows // tm,),
        in_specs=[
            pl.BlockSpec((tm, D_MODEL), lambda i: (i, 0)),
            _const_spec((1, D_MODEL)),
            _layer_spec((D_MODEL // 2, N_SLABS * D_MODEL), layer),
            _const_spec((GATE_RANK, D_MODEL)),
            _const_spec((GATE_RANK, GLA_KEY_WIDTH)),
            _const_spec((1, GLA_KEY_WIDTH)),
        ],
        out_specs=[
            pl.BlockSpec((N_SLABS, tm, D_MODEL), lambda i: (0, i, 0)),
            pl.BlockSpec((tm, GLA_KEY_WIDTH), lambda i: (i, 0)),
        ],
        out_shape=[
            jax.ShapeDtypeStruct((N_SLABS, m_rows, D_MODEL), out_dtype),
            jax.ShapeDtypeStruct((m_rows, GLA_KEY_WIDTH), F32),
        ],
        compiler_params=_params(("parallel",),
                                ((tm, D_MODEL), F32, 2), ((D_MODEL // 2, N_SLABS * D_MODEL), jnp.uint32, 1),
                                ((N_SLABS, tm, D_MODEL), out_dtype, 2), ((tm, GLA_KEY_WIDTH), F32, 2)),
        name="inproj",
    )(x, gain, w_proj, w_alr, w_a2, b_a)


def _sample_mix_kernel(p_ref, la_ref, s0_ref, hist_ref, ck_hbm, cv_hbm, pw_ref, ps_ref, gg_ref,
                       br_ref, s_out_ref, hist_out_ref, diff_ref, kbuf, vbuf, cache_sem, *, sb, layer):
    n_steps = pl.num_programs(0) * pl.num_programs(1)
    step = pl.program_id(0) * pl.num_programs(1) + pl.program_id(1)

    def cache_copy(which, t):
        src, buf = ((ck_hbm, kbuf), (cv_hbm, vbuf))[which]
        slot = t % CACHE_RING
        return pltpu.make_async_copy(src.at[layer, pl.ds(t * sb, sb)], buf.at[slot], cache_sem.at[which, slot])

    @pl.when(step == 0)
    def _():
        for t in range(CACHE_RING - 1):
            cache_copy(0, t).start()
            cache_copy(1, t).start()

    @pl.when(step + CACHE_RING - 1 < n_steps)
    def _():
        cache_copy(0, step + CACHE_RING - 1).start()
        cache_copy(1, step + CACHE_RING - 1).start()

    cache_copy(0, step).wait()
    cache_copy(1, step).wait()
    ck_ref = kbuf.at[step % CACHE_RING]
    cv_ref = vbuf.at[step % CACHE_RING]

    r0 = pl.program_id(1) * sb
    erow = lax.broadcasted_iota(jnp.int32, (GLA_DK, GLA_DK), 0)
    ecol = lax.broadcasted_iota(jnp.int32, (GLA_DK, GLA_DK), 1)
    eye = erow == ecol

    def to_col(x):
        return jnp.sum(jnp.where(eye, jnp.broadcast_to(x, (GLA_DK, GLA_DK)), 0.0), axis=1, keepdims=True)

    for i in range(sb):
        r = pl.ds(r0 + i, 1)
        la = la_ref[r, :]
        qk = p_ref[SLAB_QK, r, :]
        vv = p_ref[SLAB_V, r, :]
        gla_g = p_ref[SLAB_GLA_G, r, :]
        for h in range(GLA_HEADS):
            kc = slice(h * GLA_DK, (h + 1) * GLA_DK)
            kc2 = slice(GLA_KEY_WIDTH + h * GLA_DK, GLA_KEY_WIDTH + (h + 1) * GLA_DK)
            vc = slice(h * GLA_DV, (h + 1) * GLA_DV)
            a_col = to_col(jnp.exp(la[:, kc]))
            q_col = to_col(qk[:, kc] * (GLA_DK ** -0.5))
            k_col = to_col(qk[:, kc2])
            s_new = a_col * s0_ref[0, i, h] + k_col * vv[:, vc]
            s_out_ref[0, i, h] = s_new
            o = jnp.sum(q_col * s_new, axis=0, keepdims=True)
            br_ref[0, r, vc] = _rms(o, gg_ref[:, vc]) * _silu(gla_g[:, vc])

        u = p_ref[SLAB_U, r, :]
        for g, w in enumerate(POOL_WINDOWS):
            cs = slice(g * POOL_GROUP_DIM, (g + 1) * POOL_GROUP_DIM)
            past = jnp.sum(hist_ref[0, POOL_HIST - (w - 1):POOL_HIST, r, cs], axis=0)
            diff_ref[r, cs] = (u[:, cs] + past) / float(w) - u[:, cs]
        hist_out_ref[0, 0:POOL_HIST - 1, r, :] = hist_ref[0, 1:POOL_HIST, r, :]
        hist_out_ref[0, POOL_HIST - 1, r, :] = u

        xq = p_ref[SLAB_XQ, r, :]
        xg = p_ref[SLAB_XG, r, :]
        half_cols = [slice(h * XA_HEAD_DIM + j * LANES, h * XA_HEAD_DIM + (j + 1) * LANES)
                     for j in range(XA_HEAD_DIM // LANES) for h in range(XA_HEADS)]
        xq_rows = jnp.concatenate([xq[:, cs] for cs in half_cols], axis=0)
        n_mem = ck_ref.shape[1]
        prod = (ck_ref[i] * xq_rows[None]).reshape(n_mem * SUBLANES, LANES).astype(BF)
        part = _dot(prod, jnp.ones((LANES, LANES), BF)).reshape(n_mem, SUBLANES, LANES)
        s = (part + pltpu.roll(part, XA_HEADS, axis=1)) * (XA_HEAD_DIM ** -0.5)
        p = jnp.exp(s - jnp.max(s, axis=0, keepdims=True))
        o = jnp.sum(p * cv_ref[i], axis=0) / jnp.sum(p, axis=0)
        halves = XA_HEAD_DIM // LANES
        o_row = jnp.concatenate([o[j * XA_HEADS + h:j * XA_HEADS + h + 1, :]
                                 for h in range(XA_HEADS) for j in range(halves)], axis=1)
        br_ref[2, r, :] = o_row * _silu(xg)

    @pl.when(pl.program_id(1) == pl.num_programs(1) - 1)
    def _():
        for g in range(len(POOL_WINDOWS)):
            cs = slice(g * POOL_GROUP_DIM, (g + 1) * POOL_GROUP_DIM)
            mixed = _dot(diff_ref[:, cs].astype(BF), pw_ref[g]) * ps_ref[:, cs]
            br_ref[1, :, cs] = mixed * _silu(p_ref[SLAB_POOL_G, :, cs])


def _cache_rows_view(c):
    depth, nb, n_mem = c.shape[:3]
    halves = XA_HEAD_DIM // LANES
    c = c.reshape(depth, nb, n_mem, XA_HEADS, halves, LANES)
    return c.transpose(0, 1, 2, 4, 3, 5).reshape(depth, nb, n_mem, halves * XA_HEADS, LANES)


def _sample_mix(p, la, s0, hist, ck, cv, pool_w, pool_scale, gla_gain, carry, layer, sb=SAMPLE_BLOCK):
    nb = s0.shape[1]
    n_mem = ck.shape[2]
    ck, cv = _cache_rows_view(ck), _cache_rows_view(cv)
    rb = SUBLANES
    halves = rb // sb
    n_alias = 0 if carry is None else len(carry)
    n_in = 9
    kern = _drop_alias_refs(functools.partial(_sample_mix_kernel, sb=sb, layer=layer), n_in, n_alias)
    any_spec = pl.BlockSpec(memory_space=pl.ANY)
    cache_ring = (CACHE_RING, sb, n_mem, SUBLANES, LANES)
    return pl.pallas_call(
        kern,
        grid=(nb // rb, halves),
        in_specs=[
            pl.BlockSpec((N_MIX_SLABS, rb, D_MODEL), lambda i, j: (0, i, 0)),
            pl.BlockSpec((rb, GLA_KEY_WIDTH), lambda i, j: (i, 0)),
            pl.BlockSpec((1, sb, GLA_HEADS, GLA_DK, GLA_DV), lambda i, j: (layer, i * halves + j, 0, 0, 0)),
            pl.BlockSpec((1, POOL_HIST, rb, D_MODEL), lambda i, j: (layer, 0, i, 0)),
            any_spec, any_spec,
            _const_spec((len(POOL_WINDOWS), POOL_GROUP_DIM, POOL_GROUP_DIM)),
            _const_spec((1, D_MODEL)),
            _const_spec((1, D_MODEL)),
        ] + [any_spec] * n_alias,
        out_specs=[
            pl.BlockSpec((N_BRANCH, rb, D_MODEL), lambda i, j: (0, i, 0)),
            pl.BlockSpec((1, sb, GLA_HEADS, GLA_DK, GLA_DV), lambda i, j: (layer, i * halves + j, 0, 0, 0)),
            pl.BlockSpec((1, POOL_HIST, rb, D_MODEL), lambda i, j: (layer, 0, i, 0)),
        ],
        out_shape=[
            jax.ShapeDtypeStruct((N_BRANCH, nb, D_MODEL), F32),
            jax.ShapeDtypeStruct(s0.shape, F32),
            jax.ShapeDtypeStruct(hist.shape, F32),
        ],
        scratch_shapes=[pltpu.VMEM((rb, D_MODEL), F32), pltpu.VMEM(cache_ring, F32), pltpu.VMEM(cache_ring, F32),
                        pltpu.SemaphoreType.DMA((2, CACHE_RING))],
        input_output_aliases={n_in + a: 1 + a for a in range(n_alias)},
        compiler_params=_params(
            ("arbitrary", "arbitrary"),
            ((N_MIX_SLABS + N_BRANCH + 1, rb, D_MODEL), F32, 2),
            ((sb, GLA_HEADS, GLA_DK, GLA_DV), F32, 4),
            ((POOL_HIST, rb, D_MODEL), F32, 4),
            (cache_ring, F32, 2),
            ((len(POOL_WINDOWS), POOL_GROUP_DIM, POOL_GROUP_DIM), BF, 1)),
        name="sample_mix",
    )(p, la, s0, hist, ck, cv, pool_w, pool_scale, gla_gain, *(carry or ()))


def _merge_kernel(br_ref, p_ref, x_ref, wbr_ref, wo_ref, fg_ref, out_ref, *, final):
    merged = _merge_term(br_ref, p_ref, wbr_ref, 0)
    for n in range(1, N_BRANCH):
        merged = merged + _merge_term(br_ref, p_ref, wbr_ref, n)
    out_ref[...] = _merge_finish(merged, x_ref[...], wo_ref, fg_ref, final)


def _merge_out(br, p, x, w_branch, w_out, final_gain, layer, final):
    m_rows = x.shape[0]
    whole = lambda shape: pl.BlockSpec(shape, lambda i: (0,) * len(shape))
    return pl.pallas_call(
        functools.partial(_merge_kernel, final=final),
        grid=(1,),
        in_specs=[
            whole((N_BRANCH, m_rows, D_MODEL)),
            whole((N_SLABS, m_rows, D_MODEL)),
            whole((m_rows, D_MODEL)),
            _layer_spec((N_BRANCH, D_MODEL // 2, D_MODEL), layer),
            _layer_spec((D_MODEL // 2, D_MODEL), layer),
            _const_spec((1, D_MODEL)),
        ],
        out_specs=whole((m_rows, D_MODEL)),
        out_shape=jax.ShapeDtypeStruct((m_rows, D_MODEL), F32),
        compiler_params=_params(("arbitrary",),
                                ((N_BRANCH + N_SLABS + 2, m_rows, D_MODEL), F32, 2),
                                ((D_MODEL // 2, (N_BRANCH + 1) * D_MODEL), jnp.uint32, 1)),
        name="merge_out",
    )(br, p, x, w_branch, w_out, final_gain)


def kernel(x_prompt, x_sample, mem_prompt, cache_mem_k, cache_mem_v, state_gla, state_pool, w_in, w_a2, b_a, gla_gain, pool_w, pool_scale, w_mk, w_mv, w_branch, w_out, norm_gain, final_gain):
    batch, seq, _ = x_prompt.shape
    nb = x_sample.shape[0]
    n_mem = mem_prompt.shape[1]
    depth = w_in.shape[0]
    tm = PROMPT_TILE
    assert seq % tm == 0 and nb % SUBLANES == 0 and x_prompt.shape[2] == D_MODEL

    xp = x_prompt.reshape(batch * seq, D_MODEL)
    xs = x_sample.reshape(nb, D_MODEL)
    mem = mem_prompt.reshape(batch * n_mem, D_MODEL)
    fgain = final_gain.reshape(1, D_MODEL)

    mk, mv, mk_bf, mv_bf = _kvproj(mem, w_mk.astype(BF), w_mv.astype(BF), batch, n_mem)

    w_in_t = jnp.swapaxes(w_in, 1, 2)
    pool_rows = jnp.swapaxes(state_pool, 1, 2)
    w_proj = _pack_w_in(w_in_t)
    wb = _pack_matrices(w_branch.reshape(depth * N_BRANCH, D_MODEL, D_MODEL))
    wb = wb.reshape(depth, N_BRANCH, D_MODEL // 2, D_MODEL)
    wo = _pack_matrices(w_out)

    carry_p, carry_s = None, None
    for l in range(depth):
        final = l == depth - 1
        w_alr = w_in_t[l, ALR_START:ALR_START + GATE_RANK, :]
        wa2 = w_a2[l].astype(BF)
        ba = b_a[l].reshape(1, GLA_KEY_WIDTH)
        ngain = norm_gain[l].reshape(1, D_MODEL)
        ggain = gla_gain[l].reshape(1, D_MODEL)
        pscale = pool_scale[l].reshape(1, D_MODEL)
        pw = pool_w[l].astype(BF)

        xp, s_all, hist_all = _prompt_layer(xp, ngain, w_proj, w_alr, wa2, ba, mk_bf, mv_bf, pw, pscale,
                                            ggain, wb, wo, fgain, carry_p, l, depth, batch, seq, n_mem, tm, final)
        carry_p = (s_all, hist_all)

        ps, las = _inproj(xs, ngain, w_proj, w_alr, wa2, ba, l, nb, F32)
        brs, s_new, hist_new = _sample_mix(ps, las, state_gla, pool_rows, cache_mem_k, cache_mem_v,
                                           pw, pscale, ggain, carry_s, l)
        carry_s = (s_new, hist_new)
        xs = _merge_out(brs, ps, xs, wb, wo, fgain, l, final)

    return (xp.reshape(batch, seq, D_MODEL), xs.reshape(nb, 1, D_MODEL),
            mk, mv, carry_p[0], carry_p[1], carry_s[0], jnp.swapaxes(carry_s[1], 1, 2))
```

```python
import functools
import math

import jax
import jax.numpy as jnp
from jax import lax
from jax.experimental import pallas as pl
from jax.experimental.pallas import tpu as pltpu

D_MODEL = 1024
GLA_HEADS = 4
GLA_DK = 128
GLA_DV = 256
GLA_KEY_WIDTH = GLA_HEADS * GLA_DK
GATE_RANK = 16
GATE_TAU = 16.0
CHUNK = 128
POOL_WINDOWS = (2, 4, 8, 16)
POOL_GROUP_DIM = 256
POOL_HIST = 15
HIST_PAD = 16
XA_HEADS = 4
XA_HEAD_DIM = 256
N_BRANCH = 3
EPS = 1e-6
SUBLANES = 8
LANES = 128

SLAB_QK, SLAB_V, SLAB_GLA_G, SLAB_U, SLAB_POOL_G, SLAB_XQ, SLAB_XG, SLAB_MERGE = 0, 1, 2, 3, 4, 5, 6, 7
N_SLABS = 10
N_MIX_SLABS = 7
N_HEAD_SLABS = 3
ALR_START = N_HEAD_SLABS * D_MODEL

BF = jnp.bfloat16
F32 = jnp.float32
MIB = 1 << 20

PROMPT_TILE = 256
SAMPLE_BLOCK = 4
CACHE_RING = 3
VMEM_COMPILER_SCRATCH = 8 * MIB


def _dot(a, b):
    return jnp.dot(a, b, preferred_element_type=F32)


def _dot_nt(a, b):
    return lax.dot_general(a, b, (((1,), (1,)), ((), ())), preferred_element_type=F32)


def _dot_tn(a, b):
    return lax.dot_general(a, b, (((0,), (0,)), ((), ())), preferred_element_type=F32)


def _pack_rows(w):
    return pltpu.bitcast(w.astype(BF), jnp.uint32)


def _unpack_rows(w_words):
    return pltpu.bitcast(w_words, BF)


def _params(sem, *buffers):
    need = sum(math.prod(shape) * jnp.dtype(dtype).itemsize * copies for shape, dtype, copies in buffers)
    return pltpu.CompilerParams(dimension_semantics=sem, vmem_limit_bytes=need + VMEM_COMPILER_SCRATCH)


def _pack_w_in_kernel(a_ref, b_ref, o_ref):
    j = pl.program_id(1)

    @pl.when(j < N_HEAD_SLABS)
    def _():
        o_ref[0] = _pack_rows(a_ref[0].T)

    @pl.when(j >= N_HEAD_SLABS)
    def _():
        o_ref[0] = _pack_rows(jnp.concatenate([a_ref[0, GATE_RANK:, :], b_ref[0]], axis=0).T)


def _pack_w_in(w_in_t):
    depth = w_in_t.shape[0]
    return pl.pallas_call(
        _pack_w_in_kernel,
        grid=(depth, N_SLABS),
        in_specs=[
            pl.BlockSpec((1, D_MODEL, D_MODEL), lambda l, j: (l, j, 0)),
            pl.BlockSpec((1, GATE_RANK, D_MODEL), lambda l, j: (l, (j + 1) * (D_MODEL // GATE_RANK), 0)),
        ],
        out_specs=pl.BlockSpec((1, D_MODEL // 2, D_MODEL), lambda l, j: (l, 0, j)),
        out_shape=jax.ShapeDtypeStruct((depth, D_MODEL // 2, N_SLABS * D_MODEL), jnp.uint32),
        compiler_params=_params(("parallel", "parallel"),
                                ((D_MODEL, D_MODEL), F32, 2), ((GATE_RANK, D_MODEL), F32, 2),
                                ((D_MODEL // 2, D_MODEL), jnp.uint32, 2)),
        name="pack_w_in",
    )(w_in_t, w_in_t)


def _pack_matrices_kernel(a_ref, o_ref):
    o_ref[0] = _pack_rows(a_ref[0])


def _pack_matrices(w):
    return pl.pallas_call(
        _pack_matrices_kernel,
        grid=(w.shape[0],),
        in_specs=[pl.BlockSpec((1, D_MODEL, D_MODEL), lambda r: (r, 0, 0))],
        out_specs=pl.BlockSpec((1, D_MODEL // 2, D_MODEL), lambda r: (r, 0, 0)),
        out_shape=jax.ShapeDtypeStruct((w.shape[0], D_MODEL // 2, D_MODEL), jnp.uint32),
        compiler_params=_params(("parallel",), ((D_MODEL, D_MODEL), F32, 2), ((D_MODEL // 2, D_MODEL), jnp.uint32, 2)),
        name="pack_matrices",
    )(w)


def _silu(x):
    return x * jax.nn.sigmoid(x)


def _rms(x, gain):
    ms = jnp.mean(x * x, axis=-1, keepdims=True)
    return x * lax.rsqrt(ms + EPS) * gain


def _const_spec(shape):
    zeros = (0,) * len(shape)
    return pl.BlockSpec(shape, lambda *_: zeros, pipeline_mode=pl.Buffered(1))


def _layer_spec(shape, layer):
    index = (layer,) + (0,) * len(shape)
    return pl.BlockSpec((1,) + tuple(shape), lambda *_: index, pipeline_mode=pl.Buffered(1))


def _drop_alias_refs(body, n_in, n_alias):
    def kern(*refs):
        return body(*refs[:n_in], *refs[n_in + n_alias:])
    return kern


def _kvproj_kernel(m_ref, wk_ref, wv_ref, k_ref, v_ref, kb_ref, vb_ref):
    m = m_ref[...].astype(BF)
    k = _dot(m, wk_ref[0])
    v = _dot(m, wv_ref[0])
    for h in range(XA_HEADS):
        cs = slice(h * XA_HEAD_DIM, (h + 1) * XA_HEAD_DIM)
        k_ref[0, 0, :, h, :] = k[:, cs]
        v_ref[0, 0, :, h, :] = v[:, cs]
    kb_ref[0, 0] = k.T.astype(BF)
    vb_ref[0] = v.astype(BF)


def _kvproj(mem, wk, wv, batch, n_mem):
    depth = wk.shape[0]
    w_spec = pl.BlockSpec((1, D_MODEL, D_MODEL), lambda l, b: (l, 0, 0))
    out5 = pl.BlockSpec((1, 1, n_mem, XA_HEADS, XA_HEAD_DIM), lambda l, b: (l, b, 0, 0, 0))
    out_bf = pl.BlockSpec((1, n_mem, D_MODEL), lambda l, b: (l, b, 0))
    return pl.pallas_call(
        _kvproj_kernel,
        grid=(depth, batch),
        in_specs=[pl.BlockSpec((n_mem, D_MODEL), lambda l, b: (b, 0)), w_spec, w_spec],
        out_specs=[out5, out5, pl.BlockSpec((1, 1, D_MODEL, n_mem), lambda l, b: (l, b, 0, 0)), out_bf],
        out_shape=[jax.ShapeDtypeStruct((depth, batch, n_mem, XA_HEADS, XA_HEAD_DIM), F32)] * 2
        + [jax.ShapeDtypeStruct((depth, batch, D_MODEL, n_mem), BF),
           jax.ShapeDtypeStruct((depth, batch * n_mem, D_MODEL), BF)],
        compiler_params=_params(("parallel", "parallel"),
                                ((n_mem, D_MODEL), F32, 2), ((D_MODEL, D_MODEL), BF, 4),
                                ((n_mem, D_MODEL), F32, 4), ((n_mem, D_MODEL), BF, 4)),
        name="kvproj",
    )(mem, wk, wv)


def _inproj_slab(h, w_ref, p_out, j):
    w = w_ref[0, :, j * D_MODEL:(j + 1) * D_MODEL]
    p_out[j] = _dot(h, _unpack_rows(w)).astype(p_out.dtype)


def _inproj_gate_lowrank(h, walr_ref):
    return _dot_nt(h, walr_ref[...].astype(BF)).astype(BF)


def _inproj_gate(alr, wa2_ref, ba_ref, la_out):
    z = _dot(alr, wa2_ref[...]) + ba_ref[...]
    la_out[...] = (jnp.minimum(z, 0.0) - jnp.log(1.0 + jnp.exp(-jnp.abs(z)))) * (1.0 / GATE_TAU)


def _merge_term(br_ref, p, wbr_ref, n):
    return jax.nn.sigmoid(p[SLAB_MERGE + n].astype(F32)) * _dot(br_ref[n].astype(BF), _unpack_rows(wbr_ref[0, n]))


def _merge_finish(merged, x, wo_ref, fg_ref, final):
    x_new = x + _dot(merged.astype(BF), _unpack_rows(wo_ref[0]))
    return _rms(x_new, fg_ref[...]) if final else x_new


def _chunk_cumsum_matrix(tm):
    row = lax.broadcasted_iota(jnp.int32, (tm, tm), 0)
    col = lax.broadcasted_iota(jnp.int32, (tm, tm), 1)
    return (((row // CHUNK) == (col // CHUNK)) & (row >= col)).astype(BF)


def _window_matrices(tm):
    row = lax.broadcasted_iota(jnp.int32, (tm, tm), 0)
    col = lax.broadcasted_iota(jnp.int32, (tm, tm), 1)
    return jnp.stack([((row >= col) & (row - col < w)).astype(BF) for w in POOL_WINDOWS])


def _gla_cumdecay(cum_ref, la_ref):
    la = la_ref[...]
    la_hi = la.astype(BF)
    la_lo = (la - la_hi.astype(F32)).astype(BF)
    return _dot(cum_ref[...], la_hi) + _dot(cum_ref[...], la_lo)


def _gla_chunk(p, bcum, gg_ref, br_ref, st_ref, c, filler):
    crow = lax.broadcasted_iota(jnp.int32, (CHUNK, CHUNK), 0)
    ccol = lax.broadcasted_iota(jnp.int32, (CHUNK, CHUNK), 1)
    causal = crow >= ccol
    rows = slice(c * CHUNK, (c + 1) * CHUNK)
    heads = range(GLA_HEADS)
    q_dec, k_end, decay, att = [], [], [], []
    for h in heads:
        kc = slice(h * GLA_DK, (h + 1) * GLA_DK)
        kc2 = slice(GLA_KEY_WIDTH + h * GLA_DK, GLA_KEY_WIDTH + (h + 1) * GLA_DK)
        b = bcum[rows, kc]
        b_mid = b[CHUNK // 2 - 1:CHUNK // 2, :]
        b_last = b[CHUNK - 1:CHUNK, :]
        q = p[SLAB_QK, rows, kc].astype(F32) * (GLA_DK ** -0.5)
        k = p[SLAB_QK, rows, kc2].astype(F32)
        q_dec.append((q * jnp.exp(b)).astype(BF))
        q_mid = (q * jnp.exp(b - b_mid)).astype(BF)
        k_mid = (k * jnp.exp(b_mid - b)).astype(BF)
        k_end.append((k * jnp.exp(b_last - b)).astype(BF))
        decay.append(jnp.exp(b_last))
        att.append(_dot_nt(q_mid, k_mid))
    filler()
    o = []
    for h in heads:
        vc = slice(h * GLA_DV, (h + 1) * GLA_DV)
        a = jnp.where(causal, att[h], 0.0).astype(BF)
        s_before = st_ref[h].T.astype(BF)
        o.append(_dot(jnp.concatenate([a, q_dec[h]], axis=1),
                      jnp.concatenate([p[SLAB_V, rows, vc], s_before], axis=0)))
    for h in heads:
        vc = slice(h * GLA_DV, (h + 1) * GLA_DV)
        st_ref[h] = decay[h] * st_ref[h] + _dot_tn(k_end[h], p[SLAB_V, rows, vc]).T
    for h in heads:
        vc = slice(h * GLA_DV, (h + 1) * GLA_DV)
        g = p[SLAB_GLA_G, rows, vc].astype(F32)
        br_ref[0, rows, vc] = (_rms(o[h], gg_ref[:, vc]) * _silu(g)).astype(BF)


def _pool_window_sums(p, win_ref):
    sums = []
    for g in range(len(POOL_WINDOWS)):
        cs = slice(g * POOL_GROUP_DIM, (g + 1) * POOL_GROUP_DIM)
        sums.append(_dot(win_ref[g], p[SLAB_U, :, cs]))
    return sums


def _pool_branch(p, sums, pw_ref, ps_ref, br_ref, ubuf_ref, t, tm):
    u = p[SLAB_U].astype(F32)
    ubuf_ref[HIST_PAD:2 * HIST_PAD, :] = u[0:HIST_PAD]
    pos = t * tm + lax.broadcasted_iota(jnp.int32, (tm, 1), 0)
    for g, w in enumerate(POOL_WINDOWS):
        cs = slice(g * POOL_GROUP_DIM, (g + 1) * POOL_GROUP_DIM)
        ug = u[:, cs]
        head = ug[0:HIST_PAD]
        for j in range(1, w):
            head = head + ubuf_ref[HIST_PAD - j:2 * HIST_PAD - j, cs]
        s = jnp.concatenate([head, sums[g][HIST_PAD:]], axis=0)
        cnt = jnp.minimum(w, pos + 1).astype(F32)
        diff = s / cnt - ug
        mixed = _dot(diff.astype(BF), pw_ref[g]) * ps_ref[:, cs]
        pg = p[SLAB_POOL_G, :, cs].astype(F32)
        br_ref[1, :, cs] = (mixed * _silu(pg)).astype(BF)
    ubuf_ref[0:HIST_PAD, :] = u[tm - HIST_PAD:tm]


def _xattn_probs(p, mk_ref):
    out = []
    for h in range(XA_HEADS):
        cs = slice(h * XA_HEAD_DIM, (h + 1) * XA_HEAD_DIM)
        s = _dot(p[SLAB_XQ, :, cs], mk_ref[0, 0, cs, :]) * (XA_HEAD_DIM ** -0.5)
        pr = jnp.exp(s - jnp.max(s, axis=-1, keepdims=True))
        out.append((pr.astype(BF), jnp.sum(pr, axis=-1, keepdims=True)))
    return out


def _xattn_branch(p, probs, mv_ref, br_ref):
    for h in range(XA_HEADS):
        cs = slice(h * XA_HEAD_DIM, (h + 1) * XA_HEAD_DIM)
        pr, denom = probs[h]
        o = _dot(pr, mv_ref[0, :, cs]) / denom
        xg = p[SLAB_XG, :, cs].astype(F32)
        br_ref[2, :, cs] = (o * _silu(xg)).astype(BF)


def _prompt_layer_kernel(xn_ref, xc_ref, g_ref, w_ref, walr_ref, wa2_ref, ba_ref, cum_ref, win_ref,
                         mk_ref, mv_ref, pw_ref, ps_ref, gg_ref, wbr_ref, wo_ref, fg_ref,
                         out_ref, s_out_ref, hist_out_ref,
                         h_scr, p_scr, la_scr, br_scr, st_ref, ubuf_ref, *, tm, nt, final):
    s = pl.program_id(0)
    t = jnp.maximum(s - 1, 0) % nt
    slot_w = s % 2
    slot_r = 1 - slot_w

    @pl.when(s == 0)
    def _():
        p_scr[1] = jnp.zeros(p_scr.shape[1:], p_scr.dtype)
        la_scr[1] = jnp.zeros(la_scr.shape[1:], la_scr.dtype)

    @pl.when(t == 0)
    def _():
        st_ref[...] = jnp.zeros_like(st_ref)
        ubuf_ref[0:HIST_PAD, :] = jnp.zeros((HIST_PAD, D_MODEL), F32)

    @pl.when(s == 0)
    def _():
        h_scr[0] = _rms(xc_ref[...], g_ref[...]).astype(BF)

    p_next = p_scr.at[slot_w]
    p = p_scr.at[slot_r]

    slabs = list(range(N_SLABS))

    def filler(n=1):
        for _ in range(n):
            _inproj_slab(h_scr[slot_w], w_ref, p_next, slabs.pop(0))

    alr = _inproj_gate_lowrank(h_scr[slot_w], walr_ref)
    bcum = _gla_cumdecay(cum_ref, la_scr.at[slot_r])
    filler()
    _inproj_gate(alr, wa2_ref, ba_ref, la_scr.at[slot_w])
    per_site = (N_SLABS - 2) // (2 * (tm // CHUNK))
    probs, sums = None, None
    for c in range(tm // CHUNK):
        _gla_chunk(p, bcum, gg_ref, br_scr, st_ref, c, functools.partial(filler, per_site))
        filler(per_site)
        if c == 0:
            probs = _xattn_probs(p, mk_ref)
            sums = _pool_window_sums(p, win_ref)
        elif c == 1:
            _xattn_branch(p, probs, mv_ref, br_scr)
            _pool_branch(p, sums, pw_ref, ps_ref, br_scr, ubuf_ref, t, tm)
    h_scr[slot_r] = _rms(xn_ref[...], g_ref[...]).astype(BF)
    merged = _merge_term(br_scr, p, wbr_ref, 2)
    merged = merged + _merge_term(br_scr, p, wbr_ref, 1)
    merged = merged + _merge_term(br_scr, p, wbr_ref, 0)
    filler()
    assert not slabs
    out_ref[...] = _merge_finish(merged, xc_ref[...], wo_ref, fg_ref, final)

    @pl.when((s > 0) & (t == nt - 1))
    def _():
        for h in range(GLA_HEADS):
            s_out_ref[0, 0, h] = st_ref[h].T
        hist_out_ref[0, 0] = ubuf_ref[1:HIST_PAD, :]


def _prompt_layer(x, ngain, w_proj, w_alr, wa2, ba, mk, mv, pool_w, pool_scale, gla_gain,
                  w_branch, w_out, fgain, carry, layer, depth, batch, seq, n_mem, tm, final):
    nt = seq // tm
    n_tiles = batch * nt
    assert tm // CHUNK == 2 and (N_SLABS - 2) % (2 * (tm // CHUNK)) == 0, "slab placement assumes two GLA chunks"
    n_alias = 0 if carry is None else len(carry)
    n_in = 17
    kern = _drop_alias_refs(functools.partial(_prompt_layer_kernel, tm=tm, nt=nt, final=final), n_in, n_alias)
    any_spec = pl.BlockSpec(memory_space=pl.ANY)
    scratch = [
        ((2, tm, D_MODEL), BF, 1),
        ((2, N_SLABS, tm, D_MODEL), BF, 1),
        ((2, tm, GLA_KEY_WIDTH), F32, 1),
        ((N_BRANCH, tm, D_MODEL), BF, 1),
        ((GLA_HEADS, GLA_DV, GLA_DK), F32, 1),
        ((2 * HIST_PAD, D_MODEL), F32, 1),
    ]

    def cur(s):
        return jnp.maximum(s - 1, 0)

    def seq_of(s):
        return cur(s) // nt

    return pl.pallas_call(
        kern,
        grid=(n_tiles + 1,),
        in_specs=[
            pl.BlockSpec((tm, D_MODEL), lambda s: (jnp.minimum(s + 1, n_tiles - 1), 0)),
            pl.BlockSpec((tm, D_MODEL), lambda s: (cur(s), 0)),
            _const_spec((1, D_MODEL)),
            _layer_spec((D_MODEL // 2, N_SLABS * D_MODEL), layer),
            _const_spec((GATE_RANK, D_MODEL)),
            _const_spec((GATE_RANK, GLA_KEY_WIDTH)),
            _const_spec((1, GLA_KEY_WIDTH)),
            _const_spec((tm, tm)),
            _const_spec((len(POOL_WINDOWS), tm, tm)),
            pl.BlockSpec((1, 1, D_MODEL, n_mem), lambda s: (layer, seq_of(s), 0, 0)),
            pl.BlockSpec((1, n_mem, D_MODEL), lambda s: (layer, seq_of(s), 0)),
            _const_spec((len(POOL_WINDOWS), POOL_GROUP_DIM, POOL_GROUP_DIM)),
            _const_spec((1, D_MODEL)),
            _const_spec((1, D_MODEL)),
            _layer_spec((N_BRANCH, D_MODEL // 2, D_MODEL), layer),
            _layer_spec((D_MODEL // 2, D_MODEL), layer),
            _const_spec((1, D_MODEL)),
        ] + [any_spec] * n_alias,
        out_specs=[
            pl.BlockSpec((tm, D_MODEL), lambda s: (cur(s), 0)),
            pl.BlockSpec((1, 1, GLA_HEADS, GLA_DK, GLA_DV), lambda s: (layer, seq_of(s), 0, 0, 0)),
            pl.BlockSpec((1, 1, POOL_HIST, D_MODEL), lambda s: (layer, seq_of(s), 0, 0)),
        ],
        out_shape=[
            jax.ShapeDtypeStruct((n_tiles * tm, D_MODEL), F32),
            jax.ShapeDtypeStruct((depth, batch, GLA_HEADS, GLA_DK, GLA_DV), F32),
            jax.ShapeDtypeStruct((depth, batch, POOL_HIST, D_MODEL), F32),
        ],
        scratch_shapes=[pltpu.VMEM(shape, dtype) for shape, dtype, _ in scratch],
        input_output_aliases={n_in + a: 1 + a for a in range(n_alias)},
        compiler_params=_params(
            ("arbitrary",), *scratch,
            ((tm, D_MODEL), F32, 6),
            ((D_MODEL // 2, (N_SLABS + N_BRANCH + 1) * D_MODEL), jnp.uint32, 1),
            ((n_mem, D_MODEL), BF, 4),
            ((tm, tm), BF, 1 + len(POOL_WINDOWS)),
            ((len(POOL_WINDOWS), POOL_GROUP_DIM, POOL_GROUP_DIM), BF, 1),
            ((GLA_HEADS, GLA_DK, GLA_DV), F32, 2)),
        name="prompt_layer",
    )(x, x, ngain, w_proj, w_alr, wa2, ba, _chunk_cumsum_matrix(tm), _window_matrices(tm),
      mk, mv, pool_w, pool_scale, gla_gain,
      w_branch, w_out, fgain, *(carry or ()))


def _inproj_kernel(x_ref, g_ref, w_ref, walr_ref, wa2_ref, ba_ref, p_ref, la_ref):
    h = _rms(x_ref[...], g_ref[...]).astype(BF)
    alr = _inproj_gate_lowrank(h, walr_ref)
    for j in range(N_SLABS):
        _inproj_slab(h, w_ref, p_ref, j)
    _inproj_gate(alr, wa2_ref, ba_ref, la_ref)


def _inproj(x, gain, w_proj, w_alr, w_a2, b_a, layer, tm, out_dtype):
    m_rows = x.shape[0]
    return pl.pallas_call(
        _inproj_kernel,
        grid=(m_rows // tm,),
        in_specs=[
            pl.BlockSpec((tm, D_MODEL), lambda i: (i, 0)),
            _const_spec((1, D_MODEL)),
            _layer_spec((D_MODEL // 2, N_SLABS * D_MODEL), layer),
            _const_spec((GATE_RANK, D_MODEL)),
            _const_spec((GATE_RANK, GLA_KEY_WIDTH)),
            _const_spec((1, GLA_KEY_WIDTH)),
        ],
        out_specs=[
            pl.BlockSpec((N_SLABS, tm, D_MODEL), lambda i: (0, i, 0)),
            pl.BlockSpec((tm, GLA_KEY_WIDTH), lambda i: (i, 0)),
        ],
        out_shape=[
            jax.ShapeDtypeStruct((N_SLABS, m_rows, D_MODEL), out_dtype),
            jax.ShapeDtypeStruct((m_rows, GLA_KEY_WIDTH), F32),
        ],
        compiler_params=_params(("parallel",),
                                ((tm, D_MODEL), F32, 2), ((D_MODEL // 2, N_SLABS * D_MODEL), jnp.uint32, 1),
                                ((N_SLABS, tm, D_MODEL), out_dtype, 2), ((tm, GLA_KEY_WIDTH), F32, 2)),
        name="inproj",
    )(x, gain, w_proj, w_alr, w_a2, b_a)


def _sample_mix_kernel(p_ref, la_ref, s0_ref, hist_ref, ck_hbm, cv_hbm, pw_ref, ps_ref, gg_ref,
                       br_ref, s_out_ref, hist_out_ref, diff_ref, kbuf, vbuf, cache_sem, *, sb, layer):
    n_steps = pl.num_programs(0) * pl.num_programs(1)
    step = pl.program_id(0) * pl.num_programs(1) + pl.program_id(1)

    def cache_copy(which, t):
        src, buf = ((ck_hbm, kbuf), (cv_hbm, vbuf))[which]
        slot = t % CACHE_RING
        return pltpu.make_async_copy(src.at[layer, pl.ds(t * sb, sb)], buf.at[slot], cache_sem.at[which, slot])

    @pl.when(step == 0)
    def _():
        for t in range(CACHE_RING - 1):
            cache_copy(0, t).start()
            cache_copy(1, t).start()

    @pl.when(step + CACHE_RING - 1 < n_steps)
    def _():
        cache_copy(0, step + CACHE_RING - 1).start()
        cache_copy(1, step + CACHE_RING - 1).start()

    cache_copy(0, step).wait()
    cache_copy(1, step).wait()
    ck_ref = kbuf.at[step % CACHE_RING]
    cv_ref = vbuf.at[step % CACHE_RING]

    r0 = pl.program_id(1) * sb
    erow = lax.broadcasted_iota(jnp.int32, (GLA_DK, GLA_DK), 0)
    ecol = lax.broadcasted_iota(jnp.int32, (GLA_DK, GLA_DK), 1)
    eye = erow == ecol

    def to_col(x):
        return jnp.sum(jnp.where(eye, jnp.broadcast_to(x, (GLA_DK, GLA_DK)), 0.0), axis=1, keepdims=True)

    for i in range(sb):
        r = pl.ds(r0 + i, 1)
        la = la_ref[r, :]
        qk = p_ref[SLAB_QK, r, :]
        vv = p_ref[SLAB_V, r, :]
        gla_g = p_ref[SLAB_GLA_G, r, :]
        for h in range(GLA_HEADS):
            kc = slice(h * GLA_DK, (h + 1) * GLA_DK)
            kc2 = slice(GLA_KEY_WIDTH + h * GLA_DK, GLA_KEY_WIDTH + (h + 1) * GLA_DK)
            vc = slice(h * GLA_DV, (h + 1) * GLA_DV)
            a_col = to_col(jnp.exp(la[:, kc]))
            q_col = to_col(qk[:, kc] * (GLA_DK ** -0.5))
            k_col = to_col(qk[:, kc2])
            s_new = a_col * s0_ref[0, i, h] + k_col * vv[:, vc]
            s_out_ref[0, i, h] = s_new
            o = jnp.sum(q_col * s_new, axis=0, keepdims=True)
            br_ref[0, r, vc] = _rms(o, gg_ref[:, vc]) * _silu(gla_g[:, vc])

        u = p_ref[SLAB_U, r, :]
        for g, w in enumerate(POOL_WINDOWS):
            cs = slice(g * POOL_GROUP_DIM, (g + 1) * POOL_GROUP_DIM)
            past = jnp.sum(hist_ref[0, POOL_HIST - (w - 1):POOL_HIST, r, cs], axis=0)
            diff_ref[r, cs] = (u[:, cs] + past) / float(w) - u[:, cs]
        hist_out_ref[0, 0:POOL_HIST - 1, r, :] = hist_ref[0, 1:POOL_HIST, r, :]
        hist_out_ref[0, POOL_HIST - 1, r, :] = u

        xq = p_ref[SLAB_XQ, r, :]
        xg = p_ref[SLAB_XG, r, :]
        half_cols = [slice(h * XA_HEAD_DIM + j * LANES, h * XA_HEAD_DIM + (j + 1) * LANES)
                     for j in range(XA_HEAD_DIM // LANES) for h in range(XA_HEADS)]
        xq_rows = jnp.concatenate([xq[:, cs] for cs in half_cols], axis=0)
        n_mem = ck_ref.shape[1]
        prod = (ck_ref[i] * xq_rows[None]).reshape(n_mem * SUBLANES, LANES).astype(BF)
        part = _dot(prod, jnp.ones((LANES, LANES), BF)).reshape(n_mem, SUBLANES, LANES)
        s = (part + pltpu.roll(part, XA_HEADS, axis=1)) * (XA_HEAD_DIM ** -0.5)
        p = jnp.exp(s - jnp.max(s, axis=0, keepdims=True))
        o = jnp.sum(p * cv_ref[i], axis=0) / jnp.sum(p, axis=0)
        halves = XA_HEAD_DIM // LANES
        o_row = jnp.concatenate([o[j * XA_HEADS + h:j * XA_HEADS + h + 1, :]
                                 for h in range(XA_HEADS) for j in range(halves)], axis=1)
        br_ref[2, r, :] = o_row * _silu(xg)

    @pl.when(pl.program_id(1) == pl.num_programs(1) - 1)
    def _():
        for g in range(len(POOL_WINDOWS)):
            cs = slice(g * POOL_GROUP_DIM, (g + 1) * POOL_GROUP_DIM)
            mixed = _dot(diff_ref[:, cs].astype(BF), pw_ref[g]) * ps_ref[:, cs]
            br_ref[1, :, cs] = mixed * _silu(p_ref[SLAB_POOL_G, :, cs])


def _cache_rows_view(c):
    depth, nb, n_mem = c.shape[:3]
    halves = XA_HEAD_DIM // LANES
    c = c.reshape(depth, nb, n_mem, XA_HEADS, halves, LANES)
    return c.transpose(0, 1, 2, 4, 3, 5).reshape(depth, nb, n_mem, halves * XA_HEADS, LANES)


def _sample_mix(p, la, s0, hist, ck, cv, pool_w, pool_scale, gla_gain, carry, layer, sb=SAMPLE_BLOCK):
    nb = s0.shape[1]
    n_mem = ck.shape[2]
    ck, cv = _cache_rows_view(ck), _cache_rows_view(cv)
    rb = SUBLANES
    halves = rb // sb
    n_alias = 0 if carry is None else len(carry)
    n_in = 9
    kern = _drop_alias_refs(functools.partial(_sample_mix_kernel, sb=sb, layer=layer), n_in, n_alias)
    any_spec = pl.BlockSpec(memory_space=pl.ANY)
    cache_ring = (CACHE_RING, sb, n_mem, SUBLANES, LANES)
    return pl.pallas_call(
        kern,
        grid=(nb // rb, halves),
        in_specs=[
            pl.BlockSpec((N_MIX_SLABS, rb, D_MODEL), lambda i, j: (0, i, 0)),
            pl.BlockSpec((rb, GLA_KEY_WIDTH), lambda i, j: (i, 0)),
            pl.BlockSpec((1, sb, GLA_HEADS, GLA_DK, GLA_DV), lambda i, j: (layer, i * halves + j, 0, 0, 0)),
            pl.BlockSpec((1, POOL_HIST, rb, D_MODEL), lambda i, j: (layer, 0, i, 0)),
            any_spec, any_spec,
            _const_spec((len(POOL_WINDOWS), POOL_GROUP_DIM, POOL_GROUP_DIM)),
            _const_spec((1, D_MODEL)),
            _const_spec((1, D_MODEL)),
        ] + [any_spec] * n_alias,
        out_specs=[
            pl.BlockSpec((N_BRANCH, rb, D_MODEL), lambda i, j: (0, i, 0)),
            pl.BlockSpec((1, sb, GLA_HEADS, GLA_DK, GLA_DV), lambda i, j: (layer, i * halves + j, 0, 0, 0)),
            pl.BlockSpec((1, POOL_HIST, rb, D_MODEL), lambda i, j: (layer, 0, i, 0)),
        ],
        out_shape=[
            jax.ShapeDtypeStruct((N_BRANCH, nb, D_MODEL), F32),
            jax.ShapeDtypeStruct(s0.shape, F32),
            jax.ShapeDtypeStruct(hist.shape, F32),
        ],
        scratch_shapes=[pltpu.VMEM((rb, D_MODEL), F32), pltpu.VMEM(cache_ring, F32), pltpu.VMEM(cache_ring, F32),
                        pltpu.SemaphoreType.DMA((2, CACHE_RING))],
        input_output_aliases={n_in + a: 1 + a for a in range(n_alias)},
        compiler_params=_params(
            ("arbitrary", "arbitrary"),
            ((N_MIX_SLABS + N_BRANCH + 1, rb, D_MODEL), F32, 2),
            ((sb, GLA_HEADS, GLA_DK, GLA_DV), F32, 4),
            ((POOL_HIST, rb, D_MODEL), F32, 4),
            (cache_ring, F32, 2),
            ((len(POOL_WINDOWS), POOL_GROUP_DIM, POOL_GROUP_DIM), BF, 1)),
        name="sample_mix",
    )(p, la, s0, hist, ck, cv, pool_w, pool_scale, gla_gain, *(carry or ()))


def _merge_kernel(br_ref, p_ref, x_ref, wbr_ref, wo_ref, fg_ref, out_ref, *, final):
    merged = _merge_term(br_ref, p_ref, wbr_ref, 0)
    for n in range(1, N_BRANCH):
        merged = merged + _merge_term(br_ref, p_ref, wbr_ref, n)
    out_ref[...] = _merge_finish(merged, x_ref[...], wo_ref, fg_ref, final)


def _merge_out(br, p, x, w_branch, w_out, final_gain, layer, final):
    m_rows = x.shape[0]
    whole = lambda shape: pl.BlockSpec(shape, lambda i: (0,) * len(shape))
    return pl.pallas_call(
        functools.partial(_merge_kernel, final=final),
        grid=(1,),
        in_specs=[
            whole((N_BRANCH, m_rows, D_MODEL)),
            whole((N_SLABS, m_rows, D_MODEL)),
            whole((m_rows, D_MODEL)),
            _layer_spec((N_BRANCH, D_MODEL // 2, D_MODEL), layer),
            _layer_spec((D_MODEL // 2, D_MODEL), layer),
            _const_spec((1, D_MODEL)),
        ],
        out_specs=whole((m_rows, D_MODEL)),
        out_shape=jax.ShapeDtypeStruct((m_rows, D_MODEL), F32),
        compiler_params=_params(("arbitrary",),
                                ((N_BRANCH + N_SLABS + 2, m_rows, D_MODEL), F32, 2),
                                ((D_MODEL // 2, (N_BRANCH + 1) * D_MODEL), jnp.uint32, 1)),
        name="merge_out",
    )(br, p, x, w_branch, w_out, final_gain)


def kernel(x_prompt, x_sample, mem_prompt, cache_mem_k, cache_mem_v, state_gla, state_pool, w_in, w_a2, b_a, gla_gain, pool_w, pool_scale, w_mk, w_mv, w_branch, w_out, norm_gain, final_gain):
    batch, seq, _ = x_prompt.shape
    nb = x_sample.shape[0]
    n_mem = mem_prompt.shape[1]
    depth = w_in.shape[0]
    tm = PROMPT_TILE
    assert seq % tm == 0 and nb % SUBLANES == 0 and x_prompt.shape[2] == D_MODEL

    xp = x_prompt.reshape(batch * seq, D_MODEL)
    xs = x_sample.reshape(nb, D_MODEL)
    mem = mem_prompt.reshape(batch * n_mem, D_MODEL)
    fgain = final_gain.reshape(1, D_MODEL)

    mk, mv, mk_bf, mv_bf = _kvproj(mem, w_mk.astype(BF), w_mv.astype(BF), batch, n_mem)

    w_in_t = jnp.swapaxes(w_in, 1, 2)
    pool_rows = jnp.swapaxes(state_pool, 1, 2)
    w_proj = _pack_w_in(w_in_t)
    wb = _pack_matrices(w_branch.reshape(depth * N_BRANCH, D_MODEL, D_MODEL))
    wb = wb.reshape(depth, N_BRANCH, D_MODEL // 2, D_MODEL)
    wo = _pack_matrices(w_out)

    carry_p, carry_s = None, None
    for l in range(depth):
        final = l == depth - 1
        w_alr = w_in_t[l, ALR_START:ALR_START + GATE_RANK, :]
        wa2 = w_a2[l].astype(BF)
        ba = b_a[l].reshape(1, GLA_KEY_WIDTH)
        ngain = norm_gain[l].reshape(1, D_MODEL)
        ggain = gla_gain[l].reshape(1, D_MODEL)
        pscale = pool_scale[l].reshape(1, D_MODEL)
        pw = pool_w[l].astype(BF)

        xp, s_all, hist_all = _prompt_layer(xp, ngain, w_proj, w_alr, wa2, ba, mk_bf, mv_bf, pw, pscale,
                                            ggain, wb, wo, fgain, carry_p, l, depth, batch, seq, n_mem, tm, final)
        carry_p = (s_all, hist_all)

        ps, las = _inproj(xs, ngain, w_proj, w_alr, wa2, ba, l, nb, F32)
        brs, s_new, hist_new = _sample_mix(ps, las, state_gla, pool_rows, cache_mem_k, cache_mem_v,
                                           pw, pscale, ggain, carry_s, l)
        carry_s = (s_new, hist_new)
        xs = _merge_out(brs, ps, xs, wb, wo, fgain, l, final)

    return (xp.reshape(batch, seq, D_MODEL), xs.reshape(nb, 1, D_MODEL),
            mk, mv, carry_p[0], carry_p[1], carry_s[0], jnp.swapaxes(carry_s[1], 1, 2))
```

```python
import functools
import math

import jax
import jax.numpy as jnp
from jax import lax
from jax.experimental import pallas as pl
from jax.experimental.pallas import tpu as pltpu

D_MODEL = 1024
GLA_HEADS = 4
GLA_DK = 128
GLA_DV = 256
GLA_KEY_WIDTH = GLA_HEADS * GLA_DK
GATE_RANK = 16
GATE_TAU = 16.0
CHUNK = 128
POOL_WINDOWS = (2, 4, 8, 16)
POOL_GROUP_DIM = 256
POOL_HIST = 15
HIST_PAD = 16
XA_HEADS = 4
XA_HEAD_DIM = 256
N_BRANCH = 3
EPS = 1e-6
SUBLANES = 8
LANES = 128

SLAB_QK, SLAB_V, SLAB_GLA_G, SLAB_U, SLAB_POOL_G, SLAB_XQ, SLAB_XG, SLAB_MERGE = 0, 1, 2, 3, 4, 5, 6, 7
N_SLABS = 10
N_MIX_SLABS = 7
N_HEAD_SLABS = 3
ALR_START = N_HEAD_SLABS * D_MODEL

BF = jnp.bfloat16
F32 = jnp.float32
MIB = 1 << 20

PROMPT_TILE = 256
SAMPLE_BLOCK = 4
CACHE_RING = 3
VMEM_COMPILER_SCRATCH = 8 * MIB


def _dot(a, b):
    return jnp.dot(a, b, preferred_element_type=F32)


def _dot_nt(a, b):
    return lax.dot_general(a, b, (((1,), (1,)), ((), ())), preferred_element_type=F32)


def _dot_tn(a, b):
    return lax.dot_general(a, b, (((0,), (0,)), ((), ())), preferred_element_type=F32)


def _pack_rows(w):
    return pltpu.bitcast(w.astype(BF), jnp.uint32)


def _unpack_rows(w_words):
    return pltpu.bitcast(w_words, BF)


def _params(sem, *buffers):
    need = sum(math.prod(shape) * jnp.dtype(dtype).itemsize * copies for shape, dtype, copies in buffers)
    return pltpu.CompilerParams(dimension_semantics=sem, vmem_limit_bytes=need + VMEM_COMPILER_SCRATCH)


def _pack_w_in_kernel(a_ref, b_ref, o_ref):
    j = pl.program_id(1)

    @pl.when(j < N_HEAD_SLABS)
    def _():
        o_ref[0] = _pack_rows(a_ref[0].T)

    @pl.when(j >= N_HEAD_SLABS)
    def _():
        o_ref[0] = _pack_rows(jnp.concatenate([a_ref[0, GATE_RANK:, :], b_ref[0]], axis=0).T)


def _pack_w_in(w_in_t):
    depth = w_in_t.shape[0]
    return pl.pallas_call(
        _pack_w_in_kernel,
        grid=(depth, N_SLABS),
        in_specs=[
            pl.BlockSpec((1, D_MODEL, D_MODEL), lambda l, j: (l, j, 0)),
            pl.BlockSpec((1, GATE_RANK, D_MODEL), lambda l, j: (l, (j + 1) * (D_MODEL // GATE_RANK), 0)),
        ],
        out_specs=pl.BlockSpec((1, D_MODEL // 2, D_MODEL), lambda l, j: (l, 0, j)),
        out_shape=jax.ShapeDtypeStruct((depth, D_MODEL // 2, N_SLABS * D_MODEL), jnp.uint32),
        compiler_params=_params(("parallel", "parallel"),
                                ((D_MODEL, D_MODEL), F32, 2), ((GATE_RANK, D_MODEL), F32, 2),
                                ((D_MODEL // 2, D_MODEL), jnp.uint32, 2)),
        name="pack_w_in",
    )(w_in_t, w_in_t)


def _pack_matrices_kernel(a_ref, o_ref):
    o_ref[0] = _pack_rows(a_ref[0])


def _pack_matrices(w):
    return pl.pallas_call(
        _pack_matrices_kernel,
        grid=(w.shape[0],),
        in_specs=[pl.BlockSpec((1, D_MODEL, D_MODEL), lambda r: (r, 0, 0))],
        out_specs=pl.BlockSpec((1, D_MODEL // 2, D_MODEL), lambda r: (r, 0, 0)),
        out_shape=jax.ShapeDtypeStruct((w.shape[0], D_MODEL // 2, D_MODEL), jnp.uint32),
        compiler_params=_params(("parallel",), ((D_MODEL, D_MODEL), F32, 2), ((D_MODEL // 2, D_MODEL), jnp.uint32, 2)),
        name="pack_matrices",
    )(w)


def _silu(x):
    return x * jax.nn.sigmoid(x)


def _rms(x, gain):
    ms = jnp.mean(x * x, axis=-1, keepdims=True)
    return x * lax.rsqrt(ms + EPS) * gain


def _const_spec(shape):
    zeros = (0,) * len(shape)
    return pl.BlockSpec(shape, lambda *_: zeros, pipeline_mode=pl.Buffered(1))


def _layer_spec(shape, layer):
    index = (layer,) + (0,) * len(shape)
    return pl.BlockSpec((1,) + tuple(shape), lambda *_: index, pipeline_mode=pl.Buffered(1))


def _drop_alias_refs(body, n_in, n_alias):
    def kern(*refs):
        return body(*refs[:n_in], *refs[n_in + n_alias:])
    return kern


def _kvproj_kernel(m_ref, wk_ref, wv_ref, k_ref, v_ref, kb_ref, vb_ref):
    m = m_ref[...].astype(BF)
    k = _dot(m, wk_ref[0])
    v = _dot(m, wv_ref[0])
    for h in range(XA_HEADS):
        cs = slice(h * XA_HEAD_DIM, (h + 1) * XA_HEAD_DIM)
        k_ref[0, 0, :, h, :] = k[:, cs]
        v_ref[0, 0, :, h, :] = v[:, cs]
    kb_ref[0] = k.astype(BF)
    vb_ref[0] = v.astype(BF)


def _kvproj(mem, wk, wv, batch, n_mem):
    depth = wk.shape[0]
    w_spec = pl.BlockSpec((1, D_MODEL, D_MODEL), lambda l, b: (l, 0, 0))
    out5 = pl.BlockSpec((1, 1, n_mem, XA_HEADS, XA_HEAD_DIM), lambda l, b: (l, b, 0, 0, 0))
    out_bf = pl.BlockSpec((1, n_mem, D_MODEL), lambda l, b: (l, b, 0))
    return pl.pallas_call(
        _kvproj_kernel,
        grid=(depth, batch),
        in_specs=[pl.BlockSpec((n_mem, D_MODEL), lambda l, b: (b, 0)), w_spec, w_spec],
        out_specs=[out5, out5, out_bf, out_bf],
        out_shape=[jax.ShapeDtypeStruct((depth, batch, n_mem, XA_HEADS, XA_HEAD_DIM), F32)] * 2
        + [jax.ShapeDtypeStruct((depth, batch * n_mem, D_MODEL), BF)] * 2,
        compiler_params=_params(("parallel", "parallel"),
                                ((n_mem, D_MODEL), F32, 2), ((D_MODEL, D_MODEL), BF, 4),
                                ((n_mem, D_MODEL), F32, 4), ((n_mem, D_MODEL), BF, 4)),
        name="kvproj",
    )(mem, wk, wv)


def _inproj_slab(h, w_ref, p_out, j):
    w = w_ref[0, :, j * D_MODEL:(j + 1) * D_MODEL]
    p_out[j] = _dot(h, _unpack_rows(w)).astype(p_out.dtype)


def _inproj_gate_lowrank(h, walr_ref):
    return _dot_nt(h, walr_ref[...].astype(BF)).astype(BF)


def _inproj_gate(alr, wa2_ref, ba_ref, la_out):
    z = _dot(alr, wa2_ref[...]) + ba_ref[...]
    la_out[...] = (jnp.minimum(z, 0.0) - jnp.log(1.0 + jnp.exp(-jnp.abs(z)))) * (1.0 / GATE_TAU)


def _merge_term(br_ref, p, wbr_ref, n):
    return jax.nn.sigmoid(p[SLAB_MERGE + n].astype(F32)) * _dot(br_ref[n].astype(BF), _unpack_rows(wbr_ref[0, n]))


def _merge_finish(merged, x, wo_ref, fg_ref, final):
    x_new = x + _dot(merged.astype(BF), _unpack_rows(wo_ref[0]))
    return _rms(x_new, fg_ref[...]) if final else x_new


def _chunk_cumsum_matrix(tm):
    row = lax.broadcasted_iota(jnp.int32, (tm, tm), 0)
    col = lax.broadcasted_iota(jnp.int32, (tm, tm), 1)
    return (((row // CHUNK) == (col // CHUNK)) & (row >= col)).astype(BF)


def _window_matrices(tm):
    row = lax.broadcasted_iota(jnp.int32, (tm, tm), 0)
    col = lax.broadcasted_iota(jnp.int32, (tm, tm), 1)
    return jnp.stack([((row >= col) & (row - col < w)).astype(BF) for w in POOL_WINDOWS])


def _gla_cumdecay(cum_ref, la_ref):
    la = la_ref[...]
    la_hi = la.astype(BF)
    la_lo = (la - la_hi.astype(F32)).astype(BF)
    return _dot(cum_ref[...], la_hi) + _dot(cum_ref[...], la_lo)


def _gla_chunk(p, bcum, gg_ref, br_ref, st_ref, c, filler):
    crow = lax.broadcasted_iota(jnp.int32, (CHUNK, CHUNK), 0)
    ccol = lax.broadcasted_iota(jnp.int32, (CHUNK, CHUNK), 1)
    causal = crow >= ccol
    rows = slice(c * CHUNK, (c + 1) * CHUNK)
    heads = range(GLA_HEADS)
    q_dec, k_end, decay, att = [], [], [], []
    for h in heads:
        kc = slice(h * GLA_DK, (h + 1) * GLA_DK)
        kc2 = slice(GLA_KEY_WIDTH + h * GLA_DK, GLA_KEY_WIDTH + (h + 1) * GLA_DK)
        b = bcum[rows, kc]
        b_mid = b[CHUNK // 2 - 1:CHUNK // 2, :]
        b_last = b[CHUNK - 1:CHUNK, :]
        q = p[SLAB_QK, rows, kc].astype(F32) * (GLA_DK ** -0.5)
        k = p[SLAB_QK, rows, kc2].astype(F32)
        q_dec.append((q * jnp.exp(b)).astype(BF))
        q_mid = (q * jnp.exp(b - b_mid)).astype(BF)
        k_mid = (k * jnp.exp(b_mid - b)).astype(BF)
        k_end.append((k * jnp.exp(b_last - b)).astype(BF))
        decay.append(jnp.exp(b_last))
        att.append(_dot_nt(q_mid, k_mid))
    filler()
    o = []
    for h in heads:
        vc = slice(h * GLA_DV, (h + 1) * GLA_DV)
        a = jnp.where(causal, att[h], 0.0).astype(BF)
        s_before = st_ref[h].T.astype(BF)
        o.append(_dot(jnp.concatenate([a, q_dec[h]], axis=1),
                      jnp.concatenate([p[SLAB_V, rows, vc], s_before], axis=0)))
    for h in heads:
        vc = slice(h * GLA_DV, (h + 1) * GLA_DV)
        st_ref[h] = decay[h] * st_ref[h] + _dot_tn(k_end[h], p[SLAB_V, rows, vc]).T
    for h in heads:
        vc = slice(h * GLA_DV, (h + 1) * GLA_DV)
        g = p[SLAB_GLA_G, rows, vc].astype(F32)
        br_ref[0, rows, vc] = (_rms(o[h], gg_ref[:, vc]) * _silu(g)).astype(BF)


def _pool_window_sums(p, win_ref):
    sums = []
    for g in range(len(POOL_WINDOWS)):
        cs = slice(g * POOL_GROUP_DIM, (g + 1) * POOL_GROUP_DIM)
        sums.append(_dot(win_ref[g], p[SLAB_U, :, cs]))
    return sums


def _pool_branch(p, sums, pw_ref, ps_ref, br_ref, ubuf_ref, t, tm):
    u = p[SLAB_U].astype(F32)
    ubuf_ref[HIST_PAD:HIST_PAD + tm, :] = u
    pos = t * tm + lax.broadcasted_iota(jnp.int32, (tm, 1), 0)
    for g, w in enumerate(POOL_WINDOWS):
        cs = slice(g * POOL_GROUP_DIM, (g + 1) * POOL_GROUP_DIM)
        ug = u[:, cs]
        s = ug
        for j in range(1, w):
            s = s + ubuf_ref[HIST_PAD - j:HIST_PAD - j + tm, cs]
        cnt = jnp.minimum(w, pos + 1).astype(F32)
        diff = s / cnt - ug
        mixed = _dot(diff.astype(BF), pw_ref[g]) * ps_ref[:, cs]
        pg = p[SLAB_POOL_G, :, cs].astype(F32)
        br_ref[1, :, cs] = (mixed * _silu(pg)).astype(BF)
    ubuf_ref[0:HIST_PAD, :] = u[tm - HIST_PAD:tm]


def _xattn_probs(p, mk_ref):
    out = []
    for h in range(XA_HEADS):
        cs = slice(h * XA_HEAD_DIM, (h + 1) * XA_HEAD_DIM)
        s = _dot_nt(p[SLAB_XQ, :, cs], mk_ref[0, :, cs]) * (XA_HEAD_DIM ** -0.5)
        pr = jnp.exp(s - jnp.max(s, axis=-1, keepdims=True))
        out.append((pr.astype(BF), jnp.sum(pr, axis=-1, keepdims=True)))
    return out


def _xattn_branch(p, probs, mv_ref, br_ref):
    for h in range(XA_HEADS):
        cs = slice(h * XA_HEAD_DIM, (h + 1) * XA_HEAD_DIM)
        pr, denom = probs[h]
        o = _dot(pr, mv_ref[0, :, cs]) / denom
        xg = p[SLAB_XG, :, cs].astype(F32)
        br_ref[2, :, cs] = (o * _silu(xg)).astype(BF)


def _prompt_layer_kernel(xn_ref, xc_ref, g_ref, w_ref, walr_ref, wa2_ref, ba_ref, cum_ref, win_ref,
                         mk_ref, mv_ref, pw_ref, ps_ref, gg_ref, wbr_ref, wo_ref, fg_ref,
                         out_ref, s_out_ref, hist_out_ref,
                         h_scr, p_scr, la_scr, br_scr, st_ref, ubuf_ref, *, tm, nt, final):
    s = pl.program_id(0)
    t = jnp.maximum(s - 1, 0) % nt
    slot_w = s % 2
    slot_r = 1 - slot_w

    @pl.when(s == 0)
    def _():
        p_scr[1] = jnp.zeros(p_scr.shape[1:], p_scr.dtype)
        la_scr[1] = jnp.zeros(la_scr.shape[1:], la_scr.dtype)

    @pl.when(t == 0)
    def _():
        st_ref[...] = jnp.zeros_like(st_ref)
        ubuf_ref[0:HIST_PAD, :] = jnp.zeros((HIST_PAD, D_MODEL), F32)

    @pl.when(s == 0)
    def _():
        h_scr[0] = _rms(xc_ref[...], g_ref[...]).astype(BF)

    p_next = p_scr.at[slot_w]
    p = p_scr.at[slot_r]

    slabs = list(range(N_SLABS))

    def filler(n=1):
        for _ in range(n):
            _inproj_slab(h_scr[slot_w], w_ref, p_next, slabs.pop(0))

    alr = _inproj_gate_lowrank(h_scr[slot_w], walr_ref)
    bcum = _gla_cumdecay(cum_ref, la_scr.at[slot_r])
    filler()
    _inproj_gate(alr, wa2_ref, ba_ref, la_scr.at[slot_w])
    per_site = (N_SLABS - 2) // (2 * (tm // CHUNK))
    probs, sums = None, None
    for c in range(tm // CHUNK):
        _gla_chunk(p, bcum, gg_ref, br_scr, st_ref, c, functools.partial(filler, per_site))
        filler(per_site)
        if c == 0:
            probs = _xattn_probs(p, mk_ref)
            sums = None
        elif c == 1:
            _xattn_branch(p, probs, mv_ref, br_scr)
            _pool_branch(p, sums, pw_ref, ps_ref, br_scr, ubuf_ref, t, tm)
    h_scr[slot_r] = _rms(xn_ref[...], g_ref[...]).astype(BF)
    merged = _merge_term(br_scr, p, wbr_ref, 2)
    merged = merged + _merge_term(br_scr, p, wbr_ref, 1)
    merged = merged + _merge_term(br_scr, p, wbr_ref, 0)
    filler()
    assert not slabs
    out_ref[...] = _merge_finish(merged, xc_ref[...], wo_ref, fg_ref, final)

    @pl.when((s > 0) & (t == nt - 1))
    def _():
        for h in range(GLA_HEADS):
            s_out_ref[0, 0, h] = st_ref[h].T
        hist_out_ref[0, 0] = ubuf_ref[1:HIST_PAD, :]


def _prompt_layer(x, ngain, w_proj, w_alr, wa2, ba, mk, mv, pool_w, pool_scale, gla_gain,
                  w_branch, w_out, fgain, carry, layer, depth, batch, seq, n_mem, tm, final):
    nt = seq // tm
    n_tiles = batch * nt
    assert tm // CHUNK == 2 and (N_SLABS - 2) % (2 * (tm // CHUNK)) == 0, "slab placement assumes two GLA chunks"
    n_alias = 0 if carry is None else len(carry)
    n_in = 17
    kern = _drop_alias_refs(functools.partial(_prompt_layer_kernel, tm=tm, nt=nt, final=final), n_in, n_alias)
    any_spec = pl.BlockSpec(memory_space=pl.ANY)
    scratch = [
        ((2, tm, D_MODEL), BF, 1),
        ((2, N_SLABS, tm, D_MODEL), BF, 1),
        ((2, tm, GLA_KEY_WIDTH), F32, 1),
        ((N_BRANCH, tm, D_MODEL), BF, 1),
        ((GLA_HEADS, GLA_DV, GLA_DK), F32, 1),
        ((HIST_PAD + tm, D_MODEL), F32, 1),
    ]

    def cur(s):
        return jnp.maximum(s - 1, 0)

    def seq_of(s):
        return cur(s) // nt

    return pl.pallas_call(
        kern,
        grid=(n_tiles + 1,),
        in_specs=[
            pl.BlockSpec((tm, D_MODEL), lambda s: (jnp.minimum(s + 1, n_tiles - 1), 0)),
            pl.BlockSpec((tm, D_MODEL), lambda s: (cur(s), 0)),
            _const_spec((1, D_MODEL)),
            _layer_spec((D_MODEL // 2, N_SLABS * D_MODEL), layer),
            _const_spec((GATE_RANK, D_MODEL)),
            _const_spec((GATE_RANK, GLA_KEY_WIDTH)),
            _const_spec((1, GLA_KEY_WIDTH)),
            _const_spec((tm, tm)),
            _const_spec((len(POOL_WINDOWS), tm, tm)),
            pl.BlockSpec((1, n_mem, D_MODEL), lambda s: (layer, seq_of(s), 0)),
            pl.BlockSpec((1, n_mem, D_MODEL), lambda s: (layer, seq_of(s), 0)),
            _const_spec((len(POOL_WINDOWS), POOL_GROUP_DIM, POOL_GROUP_DIM)),
            _const_spec((1, D_MODEL)),
            _const_spec((1, D_MODEL)),
            _layer_spec((N_BRANCH, D_MODEL // 2, D_MODEL), layer),
            _layer_spec((D_MODEL // 2, D_MODEL), layer),
            _const_spec((1, D_MODEL)),
        ] + [any_spec] * n_alias,
        out_specs=[
            pl.BlockSpec((tm, D_MODEL), lambda s: (cur(s), 0)),
            pl.BlockSpec((1, 1, GLA_HEADS, GLA_DK, GLA_DV), lambda s: (layer, seq_of(s), 0, 0, 0)),
            pl.BlockSpec((1, 1, POOL_HIST, D_MODEL), lambda s: (layer, seq_of(s), 0, 0)),
        ],
        out_shape=[
            jax.ShapeDtypeStruct((n_tiles * tm, D_MODEL), F32),
            jax.ShapeDtypeStruct((depth, batch, GLA_HEADS, GLA_DK, GLA_DV), F32),
            jax.ShapeDtypeStruct((depth, batch, POOL_HIST, D_MODEL), F32),
        ],
        scratch_shapes=[pltpu.VMEM(shape, dtype) for shape, dtype, _ in scratch],
        input_output_aliases={n_in + a: 1 + a for a in range(n_alias)},
        compiler_params=_params(
            ("arbitrary",), *scratch,
            ((tm, D_MODEL), F32, 6),
            ((D_MODEL // 2, (N_SLABS + N_BRANCH + 1) * D_MODEL), jnp.uint32, 1),
            ((n_mem, D_MODEL), BF, 4),
            ((tm, tm), BF, 1 + len(POOL_WINDOWS)),
            ((len(POOL_WINDOWS), POOL_GROUP_DIM, POOL_GROUP_DIM), BF, 1),
            ((GLA_HEADS, GLA_DK, GLA_DV), F32, 2)),
        name="prompt_layer",
    )(x, x, ngain, w_proj, w_alr, wa2, ba, _chunk_cumsum_matrix(tm), _window_matrices(tm),
      mk, mv, pool_w, pool_scale, gla_gain,
      w_branch, w_out, fgain, *(carry or ()))


def _inproj_kernel(x_ref, g_ref, w_ref, walr_ref, wa2_ref, ba_ref, p_ref, la_ref):
    h = _rms(x_ref[...], g_ref[...]).astype(BF)
    alr = _inproj_gate_lowrank(h, walr_ref)
    for j in range(N_SLABS):
        _inproj_slab(h, w_ref, p_ref, j)
    _inproj_gate(alr, wa2_ref, ba_ref, la_ref)


def _inproj(x, gain, w_proj, w_alr, w_a2, b_a, layer, tm, out_dtype):
    m_rows = x.shape[0]
    return pl.pallas_call(
        _inproj_kernel,
        grid=(m_rows // tm,),
        in_specs=[
            pl.BlockSpec((tm, D_MODEL), lambda i: (i, 0)),
            _const_spec((1, D_MODEL)),
            _layer_spec((D_MODEL // 2, N_SLABS * D_MODEL), layer),
            _const_spec((GATE_RANK, D_MODEL)),
            _const_spec((GATE_RANK, GLA_KEY_WIDTH)),
            _const_spec((1, GLA_KEY_WIDTH)),
        ],
        out_specs=[
            pl.BlockSpec((N_SLABS, tm, D_MODEL), lambda i: (0, i, 0)),
            pl.BlockSpec((tm, GLA_KEY_WIDTH), lambda i: (i, 0)),
        ],
        out_shape=[
            jax.ShapeDtypeStruct((N_SLABS, m_rows, D_MODEL), out_dtype),
            jax.ShapeDtypeStruct((m_rows, GLA_KEY_WIDTH), F32),
        ],
        compiler_params=_params(("parallel",),
                                ((tm, D_MODEL), F32, 2), ((D_MODEL // 2, N_SLABS * D_MODEL), jnp.uint32, 1),
                                ((N_SLABS, tm, D_MODEL), out_dtype, 2), ((tm, GLA_KEY_WIDTH), F32, 2)),
        name="inproj",
    )(x, gain, w_proj, w_alr, w_a2, b_a)


def _sample_mix_kernel(p_ref, la_ref, s0_ref, hist_ref, ck_hbm, cv_hbm, pw_ref, ps_ref, gg_ref,
                       br_ref, s_out_ref, hist_out_ref, diff_ref, kbuf, vbuf, cache_sem, *, sb, layer):
    n_steps = pl.num_programs(0) * pl.num_programs(1)
    step = pl.program_id(0) * pl.num_programs(1) + pl.program_id(1)

    def cache_copy(which, t):
        src, buf = ((ck_hbm, kbuf), (cv_hbm, vbuf))[which]
        slot = t % CACHE_RING
        return pltpu.make_async_copy(src.at[layer, pl.ds(t * sb, sb)], buf.at[slot], cache_sem.at[which, slot])

    @pl.when(step == 0)
    def _():
        for t in range(CACHE_RING - 1):
            cache_copy(0, t).start()
            cache_copy(1, t).start()

    @pl.when(step + CACHE_RING - 1 < n_steps)
    def _():
        cache_copy(0, step + CACHE_RING - 1).start()
        cache_copy(1, step + CACHE_RING - 1).start()

    cache_copy(0, step).wait()
    cache_copy(1, step).wait()
    ck_ref = kbuf.at[step % CACHE_RING]
    cv_ref = vbuf.at[step % CACHE_RING]

    r0 = pl.program_id(1) * sb
    erow = lax.broadcasted_iota(jnp.int32, (GLA_DK, GLA_DK), 0)
    ecol = lax.broadcasted_iota(jnp.int32, (GLA_DK, GLA_DK), 1)
    eye = erow == ecol

    def to_col(x):
        return jnp.sum(jnp.where(eye, jnp.broadcast_to(x, (GLA_DK, GLA_DK)), 0.0), axis=1, keepdims=True)

    for i in range(sb):
        r = pl.ds(r0 + i, 1)
        la = la_ref[r, :]
        qk = p_ref[SLAB_QK, r, :]
        vv = p_ref[SLAB_V, r, :]
        gla_g = p_ref[SLAB_GLA_G, r, :]
        for h in range(GLA_HEADS):
            kc = slice(h * GLA_DK, (h + 1) * GLA_DK)
            kc2 = slice(GLA_KEY_WIDTH + h * GLA_DK, GLA_KEY_WIDTH + (h + 1) * GLA_DK)
            vc = slice(h * GLA_DV, (h + 1) * GLA_DV)
            a_col = to_col(jnp.exp(la[:, kc]))
            q_col = to_col(qk[:, kc] * (GLA_DK ** -0.5))
            k_col = to_col(qk[:, kc2])
            s_new = a_col * s0_ref[0, i, h] + k_col * vv[:, vc]
            s_out_ref[0, i, h] = s_new
            o = jnp.sum(q_col * s_new, axis=0, keepdims=True)
            br_ref[0, r, vc] = _rms(o, gg_ref[:, vc]) * _silu(gla_g[:, vc])

        u = p_ref[SLAB_U, r, :]
        for g, w in enumerate(POOL_WINDOWS):
            cs = slice(g * POOL_GROUP_DIM, (g + 1) * POOL_GROUP_DIM)
            past = jnp.sum(hist_ref[0, POOL_HIST - (w - 1):POOL_HIST, r, cs], axis=0)
            diff_ref[r, cs] = (u[:, cs] + past) / float(w) - u[:, cs]
        hist_out_ref[0, 0:POOL_HIST - 1, r, :] = hist_ref[0, 1:POOL_HIST, r, :]
        hist_out_ref[0, POOL_HIST - 1, r, :] = u

        xq = p_ref[SLAB_XQ, r, :]
        xg = p_ref[SLAB_XG, r, :]
        half_cols = [slice(h * XA_HEAD_DIM + j * LANES, h * XA_HEAD_DIM + (j + 1) * LANES)
                     for j in range(XA_HEAD_DIM // LANES) for h in range(XA_HEADS)]
        xq_rows = jnp.concatenate([xq[:, cs] for cs in half_cols], axis=0)
        n_mem = ck_ref.shape[1]
        prod = (ck_ref[i] * xq_rows[None]).reshape(n_mem * SUBLANES, LANES).astype(BF)
        part = _dot(prod, jnp.ones((LANES, LANES), BF)).reshape(n_mem, SUBLANES, LANES)
        s = (part + pltpu.roll(part, XA_HEADS, axis=1)) * (XA_HEAD_DIM ** -0.5)
        p = jnp.exp(s - jnp.max(s, axis=0, keepdims=True))
        o = jnp.sum(p * cv_ref[i], axis=0) / jnp.sum(p, axis=0)
        halves = XA_HEAD_DIM // LANES
        o_row = jnp.concatenate([o[j * XA_HEADS + h:j * XA_HEADS + h + 1, :]
                                 for h in range(XA_HEADS) for j in range(halves)], axis=1)
        br_ref[2, r, :] = o_row * _silu(xg)

    @pl.when(pl.program_id(1) == pl.num_programs(1) - 1)
    def _():
        for g in range(len(POOL_WINDOWS)):
            cs = slice(g * POOL_GROUP_DIM, (g + 1) * POOL_GROUP_DIM)
            mixed = _dot(diff_ref[:, cs].astype(BF), pw_ref[g]) * ps_ref[:, cs]
            br_ref[1, :, cs] = mixed * _silu(p_ref[SLAB_POOL_G, :, cs])


def _cache_rows_view(c):
    depth, nb, n_mem = c.shape[:3]
    halves = XA_HEAD_DIM // LANES
    c = c.reshape(depth, nb, n_mem, XA_HEADS, halves, LANES)
    return c.transpose(0, 1, 2, 4, 3, 5).reshape(depth, nb, n_mem, halves * XA_HEADS, LANES)


def _sample_mix(p, la, s0, hist, ck, cv, pool_w, pool_scale, gla_gain, carry, layer, sb=SAMPLE_BLOCK):
    nb = s0.shape[1]
    n_mem = ck.shape[2]
    ck, cv = _cache_rows_view(ck), _cache_rows_view(cv)
    rb = SUBLANES
    halves = rb // sb
    n_alias = 0 if carry is None else len(carry)
    n_in = 9
    kern = _drop_alias_refs(functools.partial(_sample_mix_kernel, sb=sb, layer=layer), n_in, n_alias)
    any_spec = pl.BlockSpec(memory_space=pl.ANY)
    cache_ring = (CACHE_RING, sb, n_mem, SUBLANES, LANES)
    return pl.pallas_call(
        kern,
        grid=(nb // rb, halves),
        in_specs=[
            pl.BlockSpec((N_MIX_SLABS, rb, D_MODEL), lambda i, j: (0, i, 0)),
            pl.BlockSpec((rb, GLA_KEY_WIDTH), lambda i, j: (i, 0)),
            pl.BlockSpec((1, sb, GLA_HEADS, GLA_DK, GLA_DV), lambda i, j: (layer, i * halves + j, 0, 0, 0)),
            pl.BlockSpec((1, POOL_HIST, rb, D_MODEL), lambda i, j: (layer, 0, i, 0)),
            any_spec, any_spec,
            _const_spec((len(POOL_WINDOWS), POOL_GROUP_DIM, POOL_GROUP_DIM)),
            _const_spec((1, D_MODEL)),
            _const_spec((1, D_MODEL)),
        ] + [any_spec] * n_alias,
        out_specs=[
            pl.BlockSpec((N_BRANCH, rb, D_MODEL), lambda i, j: (0, i, 0)),
            pl.BlockSpec((1, sb, GLA_HEADS, GLA_DK, GLA_DV), lambda i, j: (layer, i * halves + j, 0, 0, 0)),
            pl.BlockSpec((1, POOL_HIST, rb, D_MODEL), lambda i, j: (layer, 0, i, 0)),
        ],
        out_shape=[
            jax.ShapeDtypeStruct((N_BRANCH, nb, D_MODEL), F32),
            jax.ShapeDtypeStruct(s0.shape, F32),
            jax.ShapeDtypeStruct(hist.shape, F32),
        ],
        scratch_shapes=[pltpu.VMEM((rb, D_MODEL), F32), pltpu.VMEM(cache_ring, F32), pltpu.VMEM(cache_ring, F32),
                        pltpu.SemaphoreType.DMA((2, CACHE_RING))],
        input_output_aliases={n_in + a: 1 + a for a in range(n_alias)},
        compiler_params=_params(
            ("arbitrary", "arbitrary"),
            ((N_MIX_SLABS + N_BRANCH + 1, rb, D_MODEL), F32, 2),
            ((sb, GLA_HEADS, GLA_DK, GLA_DV), F32, 4),
            ((POOL_HIST, rb, D_MODEL), F32, 4),
            (cache_ring, F32, 2),
            ((len(POOL_WINDOWS), POOL_GROUP_DIM, POOL_GROUP_DIM), BF, 1)),
        name="sample_mix",
    )(p, la, s0, hist, ck, cv, pool_w, pool_scale, gla_gain, *(carry or ()))


def _merge_kernel(br_ref, p_ref, x_ref, wbr_ref, wo_ref, fg_ref, out_ref, *, final):
    merged = _merge_term(br_ref, p_ref, wbr_ref, 0)
    for n in range(1, N_BRANCH):
        merged = merged + _merge_term(br_ref, p_ref, wbr_ref, n)
    out_ref[...] = _merge_finish(merged, x_ref[...], wo_ref, fg_ref, final)


def _merge_out(br, p, x, w_branch, w_out, final_gain, layer, final):
    m_rows = x.shape[0]
    whole = lambda shape: pl.BlockSpec(shape, lambda i: (0,) * len(shape))
    return pl.pallas_call(
        functools.partial(_merge_kernel, final=final),
        grid=(1,),
        in_specs=[
            whole((N_BRANCH, m_rows, D_MODEL)),
            whole((N_SLABS, m_rows, D_MODEL)),
            whole((m_rows, D_MODEL)),
            _layer_spec((N_BRANCH, D_MODEL // 2, D_MODEL), layer),
            _layer_spec((D_MODEL // 2, D_MODEL), layer),
            _const_spec((1, D_MODEL)),
        ],
        out_specs=whole((m_rows, D_MODEL)),
        out_shape=jax.ShapeDtypeStruct((m_rows, D_MODEL), F32),
        compiler_params=_params(("arbitrary",),
                                ((N_BRANCH + N_SLABS + 2, m_rows, D_MODEL), F32, 2),
                                ((D_MODEL // 2, (N_BRANCH + 1) * D_MODEL), jnp.uint32, 1)),
        name="merge_out",
    )(br, p, x, w_branch, w_out, final_gain)


def kernel(x_prompt, x_sample, mem_prompt, cache_mem_k, cache_mem_v, state_gla, state_pool, w_in, w_a2, b_a, gla_gain, pool_w, pool_scale, w_mk, w_mv, w_branch, w_out, norm_gain, final_gain):
    batch, seq, _ = x_prompt.shape
    nb = x_sample.shape[0]
    n_mem = mem_prompt.shape[1]
    depth = w_in.shape[0]
    tm = PROMPT_TILE
    assert seq % tm == 0 and nb % SUBLANES == 0 and x_prompt.shape[2] == D_MODEL

    xp = x_prompt.reshape(batch * seq, D_MODEL)
    xs = x_sample.reshape(nb, D_MODEL)
    mem = mem_prompt.reshape(batch * n_mem, D_MODEL)
    fgain = final_gain.reshape(1, D_MODEL)

    mk, mv, mk_bf, mv_bf = _kvproj(mem, w_mk.astype(BF), w_mv.astype(BF), batch, n_mem)

    w_in_t = jnp.swapaxes(w_in, 1, 2)
    pool_rows = jnp.swapaxes(state_pool, 1, 2)
    w_proj = _pack_w_in(w_in_t)
    wb = _pack_matrices(w_branch.reshape(depth * N_BRANCH, D_MODEL, D_MODEL))
    wb = wb.reshape(depth, N_BRANCH, D_MODEL // 2, D_MODEL)
    wo = _pack_matrices(w_out)

    carry_p, carry_s = None, None
    for l in range(depth):
        final = l == depth - 1
        w_alr = w_in_t[l, ALR_START:ALR_START + GATE_RANK, :]
        wa2 = w_a2[l].astype(BF)
        ba = b_a[l].reshape(1, GLA_KEY_WIDTH)
        ngain = norm_gain[l].reshape(1, D_MODEL)
        ggain = gla_gain[l].reshape(1, D_MODEL)
        pscale = pool_scale[l].reshape(1, D_MODEL)
        pw = pool_w[l].astype(BF)

        xp, s_all, hist_all = _prompt_layer(xp, ngain, w_proj, w_alr, wa2, ba, mk_bf, mv_bf, pw, pscale,
                                            ggain, wb, wo, fgain, carry_p, l, depth, batch, seq, n_mem, tm, final)
        carry_p = (s_all, hist_all)

        ps, las = _inproj(xs, ngain, w_proj, w_alr, wa2, ba, l, nb, F32)
        brs, s_new, hist_new = _sample_mix(ps, las, state_gla, pool_rows, cache_mem_k, cache_mem_v,
                                           pw, pscale, ggain, carry_s, l)
        carry_s = (s_new, hist_new)
        xs = _merge_out(brs, ps, xs, wb, wo, fgain, l, final)

    return (xp.reshape(batch, seq, D_MODEL), xs.reshape(nb, 1, D_MODEL),
            mk, mv, carry_p[0], carry_p[1], carry_s[0], jnp.swapaxes(carry_s[1], 1, 2))
```

```python
import functools
import math

import jax
import jax.numpy as jnp
from jax import lax
from jax.experimental import pallas as pl
from jax.experimental.pallas import tpu as pltpu

D_MODEL = 1024
GLA_HEADS = 4
GLA_DK = 128
GLA_DV = 256
GLA_KEY_WIDTH = GLA_HEADS * GLA_DK
GATE_RANK = 16
GATE_TAU = 16.0
CHUNK = 128
POOL_WINDOWS = (2, 4, 8, 16)
POOL_GROUP_DIM = 256
POOL_HIST = 15
HIST_PAD = 16
XA_HEADS = 4
XA_HEAD_DIM = 256
N_BRANCH = 3
EPS = 1e-6
SUBLANES = 8
LANES = 128

SLAB_QK, SLAB_V, SLAB_GLA_G, SLAB_U, SLAB_POOL_G, SLAB_XQ, SLAB_XG, SLAB_MERGE = 0, 1, 2, 3, 4, 5, 6, 7
N_SLABS = 10
N_MIX_SLABS = 7
N_HEAD_SLABS = 3
ALR_START = N_HEAD_SLABS * D_MODEL

BF = jnp.bfloat16
F32 = jnp.float32
MIB = 1 << 20

PROMPT_TILE = 256
SAMPLE_BLOCK = 4
CACHE_RING = 3
VMEM_COMPILER_SCRATCH = 8 * MIB


def _dot(a, b):
    return jnp.dot(a, b, preferred_element_type=F32)


def _dot_nt(a, b):
    return lax.dot_general(a, b, (((1,), (1,)), ((), ())), preferred_element_type=F32)


def _dot_tn(a, b):
    return lax.dot_general(a, b, (((0,), (0,)), ((), ())), preferred_element_type=F32)


def _pack_rows(w):
    return pltpu.bitcast(w.astype(BF), jnp.uint32)


def _unpack_rows(w_words):
    return pltpu.bitcast(w_words, BF)


def _params(sem, *buffers):
    need = sum(math.prod(shape) * jnp.dtype(dtype).itemsize * copies for shape, dtype, copies in buffers)
    return pltpu.CompilerParams(dimension_semantics=sem, vmem_limit_bytes=need + VMEM_COMPILER_SCRATCH)


def _pack_w_in_kernel(a_ref, b_ref, o_ref):
    j = pl.program_id(1)

    @pl.when(j < N_HEAD_SLABS)
    def _():
        o_ref[0] = _pack_rows(a_ref[0].T)

    @pl.when(j >= N_HEAD_SLABS)
    def _():
        o_ref[0] = _pack_rows(jnp.concatenate([a_ref[0, GATE_RANK:, :], b_ref[0]], axis=0).T)


def _pack_w_in(w_in_t):
    depth = w_in_t.shape[0]
    return pl.pallas_call(
        _pack_w_in_kernel,
        grid=(depth, N_SLABS),
        in_specs=[
            pl.BlockSpec((1, D_MODEL, D_MODEL), lambda l, j: (l, j, 0)),
            pl.BlockSpec((1, GATE_RANK, D_MODEL), lambda l, j: (l, (j + 1) * (D_MODEL // GATE_RANK), 0)),
        ],
        out_specs=pl.BlockSpec((1, D_MODEL // 2, D_MODEL), lambda l, j: (l, 0, j)),
        out_shape=jax.ShapeDtypeStruct((depth, D_MODEL // 2, N_SLABS * D_MODEL), jnp.uint32),
        compiler_params=_params(("parallel", "parallel"),
                                ((D_MODEL, D_MODEL), F32, 2), ((GATE_RANK, D_MODEL), F32, 2),
                                ((D_MODEL // 2, D_MODEL), jnp.uint32, 2)),
        name="pack_w_in",
    )(w_in_t, w_in_t)


def _pack_matrices_kernel(a_ref, o_ref):
    o_ref[0] = _pack_rows(a_ref[0])


def _pack_matrices(w):
    return pl.pallas_call(
        _pack_matrices_kernel,
        grid=(w.shape[0],),
        in_specs=[pl.BlockSpec((1, D_MODEL, D_MODEL), lambda r: (r, 0, 0))],
        out_specs=pl.BlockSpec((1, D_MODEL // 2, D_MODEL), lambda r: (r, 0, 0)),
        out_shape=jax.ShapeDtypeStruct((w.shape[0], D_MODEL // 2, D_MODEL), jnp.uint32),
        compiler_params=_params(("parallel",), ((D_MODEL, D_MODEL), F32, 2), ((D_MODEL // 2, D_MODEL), jnp.uint32, 2)),
        name="pack_matrices",
    )(w)


def _silu(x):
    return x * jax.nn.sigmoid(x)


def _rms(x, gain):
    ms = jnp.mean(x * x, axis=-1, keepdims=True)
    return x * lax.rsqrt(ms + EPS) * gain


def _const_spec(shape):
    zeros = (0,) * len(shape)
    return pl.BlockSpec(shape, lambda *_: zeros, pipeline_mode=pl.Buffered(1))


def _layer_spec(shape, layer):
    index = (layer,) + (0,) * len(shape)
    return pl.BlockSpec((1,) + tuple(shape), lambda *_: index, pipeline_mode=pl.Buffered(1))


def _drop_alias_refs(body, n_in, n_alias):
    def kern(*refs):
        return body(*refs[:n_in], *refs[n_in + n_alias:])
    return kern


def _kvproj_kernel(m_ref, wk_ref, wv_ref, k_ref, v_ref, kb_ref, vb_ref):
    m = m_ref[...].astype(BF)
    k = _dot(m, wk_ref[0])
    v = _dot(m, wv_ref[0])
    for h in range(XA_HEADS):
        cs = slice(h * XA_HEAD_DIM, (h + 1) * XA_HEAD_DIM)
        k_ref[0, 0, :, h, :] = k[:, cs]
        v_ref[0, 0, :, h, :] = v[:, cs]
    kb_ref[0, 0] = k.T.astype(BF)
    vb_ref[0] = v.astype(BF)


def _kvproj(mem, wk, wv, batch, n_mem):
    depth = wk.shape[0]
    w_spec = pl.BlockSpec((1, D_MODEL, D_MODEL), lambda l, b: (l, 0, 0))
    out5 = pl.BlockSpec((1, 1, n_mem, XA_HEADS, XA_HEAD_DIM), lambda l, b: (l, b, 0, 0, 0))
    out_bf = pl.BlockSpec((1, n_mem, D_MODEL), lambda l, b: (l, b, 0))
    return pl.pallas_call(
        _kvproj_kernel,
        grid=(depth, batch),
        in_specs=[pl.BlockSpec((n_mem, D_MODEL), lambda l, b: (b, 0)), w_spec, w_spec],
        out_specs=[out5, out5, pl.BlockSpec((1, 1, D_MODEL, n_mem), lambda l, b: (l, b, 0, 0)), out_bf],
        out_shape=[jax.ShapeDtypeStruct((depth, batch, n_mem, XA_HEADS, XA_HEAD_DIM), F32)] * 2
        + [jax.ShapeDtypeStruct((depth, batch, D_MODEL, n_mem), BF),
           jax.ShapeDtypeStruct((depth, batch * n_mem, D_MODEL), BF)],
        compiler_params=_params(("parallel", "parallel"),
                                ((n_mem, D_MODEL), F32, 2), ((D_MODEL, D_MODEL), BF, 4),
                                ((n_mem, D_MODEL), F32, 4), ((n_mem, D_MODEL), BF, 4)),
        name="kvproj",
    )(mem, wk, wv)


def _inproj_slab(h, w_ref, p_out, j):
    w = w_ref[0, :, j * D_MODEL:(j + 1) * D_MODEL]
    p_out[j] = _dot(h, _unpack_rows(w)).astype(p_out.dtype)


def _inproj_gate_lowrank(h, walr_ref):
    return _dot_nt(h, walr_ref[...].astype(BF)).astype(BF)


def _inproj_gate(alr, wa2_ref, ba_ref, la_out):
    z = _dot(alr, wa2_ref[...]) + ba_ref[...]
    la_out[...] = (jnp.minimum(z, 0.0) - jnp.log(1.0 + jnp.exp(-jnp.abs(z)))) * (1.0 / GATE_TAU)


def _merge_term(br_ref, p, wbr_ref, n):
    return jax.nn.sigmoid(p[SLAB_MERGE + n].astype(F32)) * _dot(br_ref[n].astype(BF), _unpack_rows(wbr_ref[0, n]))


def _merge_finish(merged, x, wo_ref, fg_ref, final):
    x_new = x + _dot(merged.astype(BF), _unpack_rows(wo_ref[0]))
    return _rms(x_new, fg_ref[...]) if final else x_new


def _chunk_cumsum_matrix(tm):
    row = lax.broadcasted_iota(jnp.int32, (tm, tm), 0)
    col = lax.broadcasted_iota(jnp.int32, (tm, tm), 1)
    return (((row // CHUNK) == (col // CHUNK)) & (row >= col)).astype(BF)


def _window_matrices(tm):
    row = lax.broadcasted_iota(jnp.int32, (tm, tm), 0)
    col = lax.broadcasted_iota(jnp.int32, (tm, tm), 1)
    return jnp.stack([((row >= col) & (row - col < w)).astype(BF) for w in POOL_WINDOWS])


def _gla_cumdecay(cum_ref, la_ref):
    la = la_ref[...]
    la_hi = la.astype(BF)
    la_lo = (la - la_hi.astype(F32)).astype(BF)
    return _dot(cum_ref[...], la_hi) + _dot(cum_ref[...], la_lo)


def _gla_chunk(p, bcum, gg_ref, br_ref, st_ref, c, filler):
    crow = lax.broadcasted_iota(jnp.int32, (CHUNK, CHUNK), 0)
    ccol = lax.broadcasted_iota(jnp.int32, (CHUNK, CHUNK), 1)
    causal = crow >= ccol
    rows = slice(c * CHUNK, (c + 1) * CHUNK)
    heads = range(GLA_HEADS)
    q_dec, k_end, decay, att = [], [], [], []
    for h in heads:
        kc = slice(h * GLA_DK, (h + 1) * GLA_DK)
        kc2 = slice(GLA_KEY_WIDTH + h * GLA_DK, GLA_KEY_WIDTH + (h + 1) * GLA_DK)
        b = bcum[rows, kc]
        b_mid = b[CHUNK // 2 - 1:CHUNK // 2, :]
        b_last = b[CHUNK - 1:CHUNK, :]
        q = p[SLAB_QK, rows, kc].astype(F32) * (GLA_DK ** -0.5)
        k = p[SLAB_QK, rows, kc2].astype(F32)
        q_dec.append((q * jnp.exp(b)).astype(BF))
        q_mid = (q * jnp.exp(b - b_mid)).astype(BF)
        k_mid = (k * jnp.exp(b_mid - b)).astype(BF)
        k_end.append((k * jnp.exp(b_last - b)).astype(BF))
        decay.append(jnp.exp(b_last))
        att.append(_dot_nt(q_mid, k_mid))
    filler()
    o = []
    for h in heads:
        vc = slice(h * GLA_DV, (h + 1) * GLA_DV)
        a = jnp.where(causal, att[h], 0.0).astype(BF)
        s_before = st_ref[h].T.astype(BF)
        o.append(_dot(jnp.concatenate([a, q_dec[h]], axis=1),
                      jnp.concatenate([p[SLAB_V, rows, vc], s_before], axis=0)))
    for h in heads:
        vc = slice(h * GLA_DV, (h + 1) * GLA_DV)
        st_ref[h] = decay[h] * st_ref[h] + _dot_tn(k_end[h], p[SLAB_V, rows, vc]).T
    for h in heads:
        vc = slice(h * GLA_DV, (h + 1) * GLA_DV)
        g = p[SLAB_GLA_G, rows, vc].astype(F32)
        br_ref[0, rows, vc] = (_rms(o[h], gg_ref[:, vc]) * _silu(g)).astype(BF)


def _pool_window_sums(p, win_ref):
    sums = []
    for g in range(len(POOL_WINDOWS)):
        cs = slice(g * POOL_GROUP_DIM, (g + 1) * POOL_GROUP_DIM)
        sums.append(_dot(win_ref[g], p[SLAB_U, :, cs]))
    return sums


def _pool_branch(p, sums, pw_ref, ps_ref, br_ref, ubuf_ref, t, tm):
    u = p[SLAB_U].astype(F32)
    ubuf_ref[HIST_PAD:HIST_PAD + tm, :] = u
    pos = t * tm + lax.broadcasted_iota(jnp.int32, (tm, 1), 0)
    for g, w in enumerate(POOL_WINDOWS):
        cs = slice(g * POOL_GROUP_DIM, (g + 1) * POOL_GROUP_DIM)
        ug = u[:, cs]
        s = ug
        for j in range(1, w):
            s = s + ubuf_ref[HIST_PAD - j:HIST_PAD - j + tm, cs]
        cnt = jnp.minimum(w, pos + 1).astype(F32)
        diff = s / cnt - ug
        mixed = _dot(diff.astype(BF), pw_ref[g]) * ps_ref[:, cs]
        pg = p[SLAB_POOL_G, :, cs].astype(F32)
        br_ref[1, :, cs] = (mixed * _silu(pg)).astype(BF)
    ubuf_ref[0:HIST_PAD, :] = u[tm - HIST_PAD:tm]


def _xattn_probs(p, mk_ref):
    out = []
    for h in range(XA_HEADS):
        cs = slice(h * XA_HEAD_DIM, (h + 1) * XA_HEAD_DIM)
        s = _dot(p[SLAB_XQ, :, cs], mk_ref[0, 0, cs, :]) * (XA_HEAD_DIM ** -0.5)
        pr = jnp.exp(s - jnp.max(s, axis=-1, keepdims=True))
        out.append((pr.astype(BF), jnp.sum(pr, axis=-1, keepdims=True)))
    return out


def _xattn_branch(p, probs, mv_ref, br_ref):
    for h in range(XA_HEADS):
        cs = slice(h * XA_HEAD_DIM, (h + 1) * XA_HEAD_DIM)
        pr, denom = probs[h]
        o = _dot(pr, mv_ref[0, :, cs]) / denom
        xg = p[SLAB_XG, :, cs].astype(F32)
        br_ref[2, :, cs] = (o * _silu(xg)).astype(BF)


def _prompt_layer_kernel(xn_ref, xc_ref, g_ref, w_ref, walr_ref, wa2_ref, ba_ref, cum_ref, win_ref,
                         mk_ref, mv_ref, pw_ref, ps_ref, gg_ref, wbr_ref, wo_ref, fg_ref,
                         out_ref, s_out_ref, hist_out_ref,
                         h_scr, p_scr, la_scr, br_scr, st_ref, ubuf_ref, *, tm, nt, final):
    s = pl.program_id(0)
    t = jnp.maximum(s - 1, 0) % nt
    slot_w = s % 2
    slot_r = 1 - slot_w

    @pl.when(s == 0)
    def _():
        p_scr[1] = jnp.zeros(p_scr.shape[1:], p_scr.dtype)
        la_scr[1] = jnp.zeros(la_scr.shape[1:], la_scr.dtype)

    @pl.when(t == 0)
    def _():
        st_ref[...] = jnp.zeros_like(st_ref)
        ubuf_ref[0:HIST_PAD, :] = jnp.zeros((HIST_PAD, D_MODEL), F32)

    @pl.when(s == 0)
    def _():
        h_scr[0] = _rms(xc_ref[...], g_ref[...]).astype(BF)

    p_next = p_scr.at[slot_w]
    p = p_scr.at[slot_r]

    slabs = list(range(N_SLABS))

    def filler(n=1):
        for _ in range(n):
            _inproj_slab(h_scr[slot_w], w_ref, p_next, slabs.pop(0))

    alr = _inproj_gate_lowrank(h_scr[slot_w], walr_ref)
    bcum = _gla_cumdecay(cum_ref, la_scr.at[slot_r])
    filler()
    _inproj_gate(alr, wa2_ref, ba_ref, la_scr.at[slot_w])
    per_site = (N_SLABS - 2) // (2 * (tm // CHUNK))
    probs, sums = None, None
    for c in range(tm // CHUNK):
        _gla_chunk(p, bcum, gg_ref, br_scr, st_ref, c, functools.partial(filler, per_site))
        filler(per_site)
        if c == 0:
            probs = _xattn_probs(p, mk_ref)
            sums = None
        elif c == 1:
            _xattn_branch(p, probs, mv_ref, br_scr)
            _pool_branch(p, sums, pw_ref, ps_ref, br_scr, ubuf_ref, t, tm)
    h_scr[slot_r] = _rms(xn_ref[...], g_ref[...]).astype(BF)
    merged = _merge_term(br_scr, p, wbr_ref, 2)
    merged = merged + _merge_term(br_scr, p, wbr_ref, 1)
    merged = merged + _merge_term(br_scr, p, wbr_ref, 0)
    filler()
    assert not slabs
    out_ref[...] = _merge_finish(merged, xc_ref[...], wo_ref, fg_ref, final)

    @pl.when((s > 0) & (t == nt - 1))
    def _():
        for h in range(GLA_HEADS):
            s_out_ref[0, 0, h] = st_ref[h].T
        hist_out_ref[0, 0] = ubuf_ref[1:HIST_PAD, :]


def _prompt_layer(x, ngain, w_proj, w_alr, wa2, ba, mk, mv, pool_w, pool_scale, gla_gain,
                  w_branch, w_out, fgain, carry, layer, depth, batch, seq, n_mem, tm, final):
    nt = seq // tm
    n_tiles = batch * nt
    assert tm // CHUNK == 2 and (N_SLABS - 2) % (2 * (tm // CHUNK)) == 0, "slab placement assumes two GLA chunks"
    n_alias = 0 if carry is None else len(carry)
    n_in = 17
    kern = _drop_alias_refs(functools.partial(_prompt_layer_kernel, tm=tm, nt=nt, final=final), n_in, n_alias)
    any_spec = pl.BlockSpec(memory_space=pl.ANY)
    scratch = [
        ((2, tm, D_MODEL), BF, 1),
        ((2, N_SLABS, tm, D_MODEL), BF, 1),
        ((2, tm, GLA_KEY_WIDTH), F32, 1),
        ((N_BRANCH, tm, D_MODEL), BF, 1),
        ((GLA_HEADS, GLA_DV, GLA_DK), F32, 1),
        ((HIST_PAD + tm, D_MODEL), F32, 1),
    ]

    def cur(s):
        return jnp.maximum(s - 1, 0)

    def seq_of(s):
        return cur(s) // nt

    return pl.pallas_call(
        kern,
        grid=(n_tiles + 1,),
        in_specs=[
            pl.BlockSpec((tm, D_MODEL), lambda s: (jnp.minimum(s + 1, n_tiles - 1), 0)),
            pl.BlockSpec((tm, D_MODEL), lambda s: (cur(s), 0)),
            _const_spec((1, D_MODEL)),
            _layer_spec((D_MODEL // 2, N_SLABS * D_MODEL), layer),
            _const_spec((GATE_RANK, D_MODEL)),
            _const_spec((GATE_RANK, GLA_KEY_WIDTH)),
            _const_spec((1, GLA_KEY_WIDTH)),
            _const_spec((tm, tm)),
            _const_spec((len(POOL_WINDOWS), tm, tm)),
            pl.BlockSpec((1, 1, D_MODEL, n_mem), lambda s: (layer, seq_of(s), 0, 0)),
            pl.BlockSpec((1, n_mem, D_MODEL), lambda s: (layer, seq_of(s), 0)),
            _const_spec((len(POOL_WINDOWS), POOL_GROUP_DIM, POOL_GROUP_DIM)),
            _const_spec((1, D_MODEL)),
            _const_spec((1, D_MODEL)),
            _layer_spec((N_BRANCH, D_MODEL // 2, D_MODEL), layer),
            _layer_spec((D_MODEL // 2, D_MODEL), layer),
            _const_spec((1, D_MODEL)),
        ] + [any_spec] * n_alias,
        out_specs=[
            pl.BlockSpec((tm, D_MODEL), lambda s: (cur(s), 0)),
            pl.BlockSpec((1, 1, GLA_HEADS, GLA_DK, GLA_DV), lambda s: (layer, seq_of(s), 0, 0, 0)),
            pl.BlockSpec((1, 1, POOL_HIST, D_MODEL), lambda s: (layer, seq_of(s), 0, 0)),
        ],
        out_shape=[
            jax.ShapeDtypeStruct((n_tiles * tm, D_MODEL), F32),
            jax.ShapeDtypeStruct((depth, batch, GLA_HEADS, GLA_DK, GLA_DV), F32),
            jax.ShapeDtypeStruct((depth, batch, POOL_HIST, D_MODEL), F32),
        ],
        scratch_shapes=[pltpu.VMEM(shape, dtype) for shape, dtype, _ in scratch],
        input_output_aliases={n_in + a: 1 + a for a in range(n_alias)},
        compiler_params=_params(
            ("arbitrary",), *scratch,
            ((tm, D_MODEL), F32, 6),
            ((D_MODEL // 2, (N_SLABS + N_BRANCH + 1) * D_MODEL), jnp.uint32, 1),
            ((n_mem, D_MODEL), BF, 4),
            ((tm, tm), BF, 1 + len(POOL_WINDOWS)),
            ((len(POOL_WINDOWS), POOL_GROUP_DIM, POOL_GROUP_DIM), BF, 1),
            ((GLA_HEADS, GLA_DK, GLA_DV), F32, 2)),
        name="prompt_layer",
    )(x, x, ngain, w_proj, w_alr, wa2, ba, _chunk_cumsum_matrix(tm), _window_matrices(tm),
      mk, mv, pool_w, pool_scale, gla_gain,
      w_branch, w_out, fgain, *(carry or ()))


def _inproj_kernel(x_ref, g_ref, w_ref, walr_ref, wa2_ref, ba_ref, p_ref, la_ref):
    h = _rms(x_ref[...], g_ref[...]).astype(BF)
    alr = _inproj_gate_lowrank(h, walr_ref)
    for j in range(N_SLABS):
        _inproj_slab(h, w_ref, p_ref, j)
    _inproj_gate(alr, wa2_ref, ba_ref, la_ref)


def _inproj(x, gain, w_proj, w_alr, w_a2, b_a, layer, tm, out_dtype):
    m_rows = x.shape[0]
    return pl.pallas_call(
        _inproj_kernel,
        grid=(m_rows // tm,),
        in_specs=[
            pl.BlockSpec((tm, D_MODEL), lambda i: (i, 0)),
            _const_spec((1, D_MODEL)),
            _layer_spec((D_MODEL // 2, N_SLABS * D_MODEL), layer),
            _const_spec((GATE_RANK, D_MODEL)),
            _const_spec((GATE_RANK, GLA_KEY_WIDTH)),
            _const_spec((1, GLA_KEY_WIDTH)),
        ],
        out_specs=[
            pl.BlockSpec((N_SLABS, tm, D_MODEL), lambda i: (0, i, 0)),
            pl.BlockSpec((tm, GLA_KEY_WIDTH), lambda i: (i, 0)),
        ],
        out_shape=[
            jax.ShapeDtypeStruct((N_SLABS, m_rows, D_MODEL), out_dtype),
            jax.ShapeDtypeStruct((m_rows, GLA_KEY_WIDTH), F32),
        ],
        compiler_params=_params(("parallel",),
                                ((tm, D_MODEL), F32, 2), ((D_MODEL // 2, N_SLABS * D_MODEL), jnp.uint32, 1),
                                ((N_SLABS, tm, D_MODEL), out_dtype, 2), ((tm, GLA_KEY_WIDTH), F32, 2)),
        name="inproj",
    )(x, gain, w_proj, w_alr, w_a2, b_a)


def _sample_mix_kernel(p_ref, la_ref, s0_ref, hist_ref, ck_hbm, cv_hbm, pw_ref, ps_ref, gg_ref,
                       br_ref, s_out_ref, hist_out_ref, diff_ref, kbuf, vbuf, cache_sem, *, sb, layer):
    n_steps = pl.num_programs(0) * pl.num_programs(1)
    step = pl.program_id(0) * pl.num_programs(1) + pl.program_id(1)

    def cache_copy(which, t):
        src, buf = ((ck_hbm, kbuf), (cv_hbm, vbuf))[which]
        slot = t % CACHE_RING
        return pltpu.make_async_copy(src.at[layer, pl.ds(t * sb, sb)], buf.at[slot], cache_sem.at[which, slot])

    @pl.when(step == 0)
    def _():
        for t in range(CACHE_RING - 1):
            cache_copy(0, t).start()
            cache_copy(1, t).start()

    @pl.when(step + CACHE_RING - 1 < n_steps)
    def _():
        cache_copy(0, step + CACHE_RING - 1).start()
        cache_copy(1, step + CACHE_RING - 1).start()

    cache_copy(0, step).wait()
    cache_copy(1, step).wait()
    ck_ref = kbuf.at[step % CACHE_RING]
    cv_ref = vbuf.at[step % CACHE_RING]

    r0 = pl.program_id(1) * sb
    erow = lax.broadcasted_iota(jnp.int32, (GLA_DK, GLA_DK), 0)
    ecol = lax.broadcasted_iota(jnp.int32, (GLA_DK, GLA_DK), 1)
    eye = erow == ecol

    def to_col(x):
        return jnp.sum(jnp.where(eye, jnp.broadcast_to(x, (GLA_DK, GLA_DK)), 0.0), axis=1, keepdims=True)

    for i in range(sb):
        r = pl.ds(r0 + i, 1)
        la = la_ref[r, :]
        qk = p_ref[SLAB_QK, r, :]
        vv = p_ref[SLAB_V, r, :]
        gla_g = p_ref[SLAB_GLA_G, r, :]
        for h in range(GLA_HEADS):
            kc = slice(h * GLA_DK, (h + 1) * GLA_DK)
            kc2 = slice(GLA_KEY_WIDTH + h * GLA_DK, GLA_KEY_WIDTH + (h + 1) * GLA_DK)
            vc = slice(h * GLA_DV, (h + 1) * GLA_DV)
            a_col = to_col(jnp.exp(la[:, kc]))
            q_col = to_col(qk[:, kc] * (GLA_DK ** -0.5))
            k_col = to_col(qk[:, kc2])
            s_new = a_col * s0_ref[0, i, h] + k_col * vv[:, vc]
            s_out_ref[0, i, h] = s_new
            o = jnp.sum(q_col * s_new, axis=0, keepdims=True)
            br_ref[0, r, vc] = _rms(o, gg_ref[:, vc]) * _silu(gla_g[:, vc])

        u = p_ref[SLAB_U, r, :]
        for g, w in enumerate(POOL_WINDOWS):
            cs = slice(g * POOL_GROUP_DIM, (g + 1) * POOL_GROUP_DIM)
            past = jnp.sum(hist_ref[0, POOL_HIST - (w - 1):POOL_HIST, r, cs], axis=0)
            diff_ref[r, cs] = (u[:, cs] + past) / float(w) - u[:, cs]
        hist_out_ref[0, 0:POOL_HIST - 1, r, :] = hist_ref[0, 1:POOL_HIST, r, :]
        hist_out_ref[0, POOL_HIST - 1, r, :] = u

        xq = p_ref[SLAB_XQ, r, :]
        xg = p_ref[SLAB_XG, r, :]
        half_cols = [slice(h * XA_HEAD_DIM + j * LANES, h * XA_HEAD_DIM + (j + 1) * LANES)
                     for j in range(XA_HEAD_DIM // LANES) for h in range(XA_HEADS)]
        xq_rows = jnp.concatenate([xq[:, cs] for cs in half_cols], axis=0)
        n_mem = ck_ref.shape[1]
        prod = (ck_ref[i] * xq_rows[None]).reshape(n_mem * SUBLANES, LANES).astype(BF)
        part = _dot(prod, jnp.ones((LANES, LANES), BF)).reshape(n_mem, SUBLANES, LANES)
        s = (part + pltpu.roll(part, XA_HEADS, axis=1)) * (XA_HEAD_DIM ** -0.5)
        p = jnp.exp(s - jnp.max(s, axis=0, keepdims=True))
        o = jnp.sum(p * cv_ref[i], axis=0) / jnp.sum(p, axis=0)
        halves = XA_HEAD_DIM // LANES
        o_row = jnp.concatenate([o[j * XA_HEADS + h:j * XA_HEADS + h + 1, :]
                                 for h in range(XA_HEADS) for j in range(halves)], axis=1)
        br_ref[2, r, :] = o_row * _silu(xg)

    @pl.when(pl.program_id(1) == pl.num_programs(1) - 1)
    def _():
        for g in range(len(POOL_WINDOWS)):
            cs = slice(g * POOL_GROUP_DIM, (g + 1) * POOL_GROUP_DIM)
            mixed = _dot(diff_ref[:, cs].astype(BF), pw_ref[g]) * ps_ref[:, cs]
            br_ref[1, :, cs] = mixed * _silu(p_ref[SLAB_POOL_G, :, cs])


def _cache_rows_view(c):
    depth, nb, n_mem = c.shape[:3]
    halves = XA_HEAD_DIM // LANES
    c = c.reshape(depth, nb, n_mem, XA_HEADS, halves, LANES)
    return c.transpose(0, 1, 2, 4, 3, 5).reshape(depth, nb, n_mem, halves * XA_HEADS, LANES)


def _sample_mix(p, la, s0, hist, ck, cv, pool_w, pool_scale, gla_gain, carry, layer, sb=SAMPLE_BLOCK):
    nb = s0.shape[1]
    n_mem = ck.shape[2]
    ck, cv = _cache_rows_view(ck), _cache_rows_view(cv)
    rb = SUBLANES
    halves = rb // sb
    n_alias = 0 if carry is None else len(carry)
    n_in = 9
    kern = _drop_alias_refs(functools.partial(_sample_mix_kernel, sb=sb, layer=layer), n_in, n_alias)
    any_spec = pl.BlockSpec(memory_space=pl.ANY)
    cache_ring = (CACHE_RING, sb, n_mem, SUBLANES, LANES)
    return pl.pallas_call(
        kern,
        grid=(nb // rb, halves),
        in_specs=[
            pl.BlockSpec((N_MIX_SLABS, rb, D_MODEL), lambda i, j: (0, i, 0)),
            pl.BlockSpec((rb, GLA_KEY_WIDTH), lambda i, j: (i, 0)),
            pl.BlockSpec((1, sb, GLA_HEADS, GLA_DK, GLA_DV), lambda i, j: (layer, i * halves + j, 0, 0, 0)),
            pl.BlockSpec((1, POOL_HIST, rb, D_MODEL), lambda i, j: (layer, 0, i, 0)),
            any_spec, any_spec,
            _const_spec((len(POOL_WINDOWS), POOL_GROUP_DIM, POOL_GROUP_DIM)),
            _const_spec((1, D_MODEL)),
            _const_spec((1, D_MODEL)),
        ] + [any_spec] * n_alias,
        out_specs=[
            pl.BlockSpec((N_BRANCH, rb, D_MODEL), lambda i, j: (0, i, 0)),
            pl.BlockSpec((1, sb, GLA_HEADS, GLA_DK, GLA_DV), lambda i, j: (layer, i * halves + j, 0, 0, 0)),
            pl.BlockSpec((1, POOL_HIST, rb, D_MODEL), lambda i, j: (layer, 0, i, 0)),
        ],
        out_shape=[
            jax.ShapeDtypeStruct((N_BRANCH, nb, D_MODEL), F32),
            jax.ShapeDtypeStruct(s0.shape, F32),
            jax.ShapeDtypeStruct(hist.shape, F32),
        ],
        scratch_shapes=[pltpu.VMEM((rb, D_MODEL), F32), pltpu.VMEM(cache_ring, F32), pltpu.VMEM(cache_ring, F32),
                        pltpu.SemaphoreType.DMA((2, CACHE_RING))],
        input_output_aliases={n_in + a: 1 + a for a in range(n_alias)},
        compiler_params=_params(
            ("arbitrary", "arbitrary"),
            ((N_MIX_SLABS + N_BRANCH + 1, rb, D_MODEL), F32, 2),
            ((sb, GLA_HEADS, GLA_DK, GLA_DV), F32, 4),
            ((POOL_HIST, rb, D_MODEL), F32, 4),
            (cache_ring, F32, 2),
            ((len(POOL_WINDOWS), POOL_GROUP_DIM, POOL_GROUP_DIM), BF, 1)),
        name="sample_mix",
    )(p, la, s0, hist, ck, cv, pool_w, pool_scale, gla_gain, *(carry or ()))


def _merge_kernel(br_ref, p_ref, x_ref, wbr_ref, wo_ref, fg_ref, out_ref, *, final):
    merged = _merge_term(br_ref, p_ref, wbr_ref, 0)
    for n in range(1, N_BRANCH):
        merged = merged + _merge_term(br_ref, p_ref, wbr_ref, n)
    out_ref[...] = _merge_finish(merged, x_ref[...], wo_ref, fg_ref, final)


def _merge_out(br, p, x, w_branch, w_out, final_gain, layer, final):
    m_rows = x.shape[0]
    whole = lambda shape: pl.BlockSpec(shape, lambda i: (0,) * len(shape))
    return pl.pallas_call(
        functools.partial(_merge_kernel, final=final),
        grid=(1,),
        in_specs=[
            whole((N_BRANCH, m_rows, D_MODEL)),
            whole((N_SLABS, m_rows, D_MODEL)),
            whole((m_rows, D_MODEL)),
            _layer_spec((N_BRANCH, D_MODEL // 2, D_MODEL), layer),
            _layer_spec((D_MODEL // 2, D_MODEL), layer),
            _const_spec((1, D_MODEL)),
        ],
        out_specs=whole((m_rows, D_MODEL)),
        out_shape=jax.ShapeDtypeStruct((m_rows, D_MODEL), F32),
        compiler_params=_params(("arbitrary",),
                                ((N_BRANCH + N_SLABS + 2, m_rows, D_MODEL), F32, 2),
                                ((D_MODEL // 2, (N_BRANCH + 1) * D_MODEL), jnp.uint32, 1)),
        name="merge_out",
    )(br, p, x, w_branch, w_out, final_gain)


def kernel(x_prompt, x_sample, mem_prompt, cache_mem_k, cache_mem_v, state_gla, state_pool, w_in, w_a2, b_a, gla_gain, pool_w, pool_scale, w_mk, w_mv, w_branch, w_out, norm_gain, final_gain):
    batch, seq, _ = x_prompt.shape
    nb = x_sample.shape[0]
    n_mem = mem_prompt.shape[1]
    depth = w_in.shape[0]
    tm = PROMPT_TILE
    assert seq % tm == 0 and nb % SUBLANES == 0 and x_prompt.shape[2] == D_MODEL

    xp = x_prompt.reshape(batch * seq, D_MODEL)
    xs = x_sample.reshape(nb, D_MODEL)
    mem = mem_prompt.reshape(batch * n_mem, D_MODEL)
    fgain = final_gain.reshape(1, D_MODEL)

    mk, mv, mk_bf, mv_bf = _kvproj(mem, w_mk.astype(BF), w_mv.astype(BF), batch, n_mem)

    w_in_t = jnp.swapaxes(w_in, 1, 2)
    pool_rows = jnp.swapaxes(state_pool, 1, 2)
    w_proj = _pack_w_in(w_in_t)
    wb = _pack_matrices(w_branch.reshape(depth * N_BRANCH, D_MODEL, D_MODEL))
    wb = wb.reshape(depth, N_BRANCH, D_MODEL // 2, D_MODEL)
    wo = _pack_matrices(w_out)

    carry_p, carry_s = None, None
    for l in range(depth):
        final = l == depth - 1
        w_alr = w_in_t[l, ALR_START:ALR_START + GATE_RANK, :]
        wa2 = w_a2[l].astype(BF)
        ba = b_a[l].reshape(1, GLA_KEY_WIDTH)
        ngain = norm_gain[l].reshape(1, D_MODEL)
        ggain = gla_gain[l].reshape(1, D_MODEL)
        pscale = pool_scale[l].reshape(1, D_MODEL)
        pw = pool_w[l].astype(BF)

        xp, s_all, hist_all = _prompt_layer(xp, ngain, w_proj, w_alr, wa2, ba, mk_bf, mv_bf, pw, pscale,
                                            ggain, wb, wo, fgain, carry_p, l, depth, batch, seq, n_mem, tm, final)
        carry_p = (s_all, hist_all)

        ps, las = _inproj(xs, ngain, w_proj, w_alr, wa2, ba, l, nb, F32)
        brs, s_new, hist_new = _sample_mix(ps, las, state_gla, pool_rows, cache_mem_k, cache_mem_v,
                                           pw, pscale, ggain, carry_s, l)
        carry_s = (s_new, hist_new)
        xs = _merge_out(brs, ps, xs, wb, wo, fgain, l, final)

    return (xp.reshape(batch, seq, D_MODEL), xs.reshape(nb, 1, D_MODEL),
            mk, mv, carry_p[0], carry_p[1], carry_s[0], jnp.swapaxes(carry_s[1], 1, 2))
```

```python
import functools
import math

import jax
import jax.numpy as jnp
from jax import lax
from jax.experimental import pallas as pl
from jax.experimental.pallas import tpu as pltpu

D_MODEL = 1024
GLA_HEADS = 4
GLA_DK = 128
GLA_DV = 256
GLA_KEY_WIDTH = GLA_HEADS * GLA_DK
GATE_RANK = 16
GATE_TAU = 16.0
CHUNK = 128
POOL_WINDOWS = (2, 4, 8, 16)
POOL_GROUP_DIM = 256
POOL_HIST = 15
HIST_PAD = 16
XA_HEADS = 4
XA_HEAD_DIM = 256
N_BRANCH = 3
EPS = 1e-6
SUBLANES = 8
LANES = 128

SLAB_QK, SLAB_V, SLAB_GLA_G, SLAB_U, SLAB_POOL_G, SLAB_XQ, SLAB_XG, SLAB_MERGE = 0, 1, 2, 3, 4, 5, 6, 7
N_SLABS = 10
N_MIX_SLABS = 7
N_HEAD_SLABS = 3
ALR_START = N_HEAD_SLABS * D_MODEL

BF = jnp.bfloat16
F32 = jnp.float32
MIB = 1 << 20

PROMPT_TILE = 256
SAMPLE_BLOCK = 4
CACHE_RING = 3
VMEM_COMPILER_SCRATCH = 8 * MIB


def _dot(a, b):
    return jnp.dot(a, b, preferred_element_type=F32)


def _dot_nt(a, b):
    return lax.dot_general(a, b, (((1,), (1,)), ((), ())), preferred_element_type=F32)


def _dot_tn(a, b):
    return lax.dot_general(a, b, (((0,), (0,)), ((), ())), preferred_element_type=F32)


def _pack_rows(w):
    return pltpu.bitcast(w.astype(BF), jnp.uint32)


def _unpack_rows(w_words):
    return pltpu.bitcast(w_words, BF)


def _params(sem, *buffers):
    need = sum(math.prod(shape) * jnp.dtype(dtype).itemsize * copies for shape, dtype, copies in buffers)
    return pltpu.CompilerParams(dimension_semantics=sem, vmem_limit_bytes=need + VMEM_COMPILER_SCRATCH)


def _pack_w_in_kernel(a_ref, b_ref, o_ref):
    j = pl.program_id(1)

    @pl.when(j < N_HEAD_SLABS)
    def _():
        o_ref[0] = _pack_rows(a_ref[0].T)

    @pl.when(j >= N_HEAD_SLABS)
    def _():
        o_ref[0] = _pack_rows(jnp.concatenate([a_ref[0, GATE_RANK:, :], b_ref[0]], axis=0).T)


def _pack_w_in(w_in_t):
    depth = w_in_t.shape[0]
    return pl.pallas_call(
        _pack_w_in_kernel,
        grid=(depth, N_SLABS),
        in_specs=[
            pl.BlockSpec((1, D_MODEL, D_MODEL), lambda l, j: (l, j, 0)),
            pl.BlockSpec((1, GATE_RANK, D_MODEL), lambda l, j: (l, (j + 1) * (D_MODEL // GATE_RANK), 0)),
        ],
        out_specs=pl.BlockSpec((1, D_MODEL // 2, D_MODEL), lambda l, j: (l, 0, j)),
        out_shape=jax.ShapeDtypeStruct((depth, D_MODEL // 2, N_SLABS * D_MODEL), jnp.uint32),
        compiler_params=_params(("parallel", "parallel"),
                                ((D_MODEL, D_MODEL), F32, 2), ((GATE_RANK, D_MODEL), F32, 2),
                                ((D_MODEL // 2, D_MODEL), jnp.uint32, 2)),
        name="pack_w_in",
    )(w_in_t, w_in_t)


def _pack_matrices_kernel(a_ref, o_ref):
    o_ref[0] = _pack_rows(a_ref[0])


def _pack_matrices(w):
    return pl.pallas_call(
        _pack_matrices_kernel,
        grid=(w.shape[0],),
        in_specs=[pl.BlockSpec((1, D_MODEL, D_MODEL), lambda r: (r, 0, 0))],
        out_specs=pl.BlockSpec((1, D_MODEL // 2, D_MODEL), lambda r: (r, 0, 0)),
        out_shape=jax.ShapeDtypeStruct((w.shape[0], D_MODEL // 2, D_MODEL), jnp.uint32),
        compiler_params=_params(("parallel",), ((D_MODEL, D_MODEL), F32, 2), ((D_MODEL // 2, D_MODEL), jnp.uint32, 2)),
        name="pack_matrices",
    )(w)


def _silu(x):
    return x * jax.nn.sigmoid(x)


def _rms(x, gain):
    ms = jnp.mean(x * x, axis=-1, keepdims=True)
    return x * lax.rsqrt(ms + EPS) * gain


def _const_spec(shape):
    zeros = (0,) * len(shape)
    return pl.BlockSpec(shape, lambda *_: zeros, pipeline_mode=pl.Buffered(1))


def _layer_spec(shape, layer):
    index = (layer,) + (0,) * len(shape)
    return pl.BlockSpec((1,) + tuple(shape), lambda *_: index, pipeline_mode=pl.Buffered(1))


def _drop_alias_refs(body, n_in, n_alias):
    def kern(*refs):
        return body(*refs[:n_in], *refs[n_in + n_alias:])
    return kern


def _kvproj_kernel(m_ref, wk_ref, wv_ref, k_ref, v_ref, kb_ref, vb_ref):
    m = m_ref[...].astype(BF)
    k = _dot(m, wk_ref[0])
    v = _dot(m, wv_ref[0])
    for h in range(XA_HEADS):
        cs = slice(h * XA_HEAD_DIM, (h + 1) * XA_HEAD_DIM)
        k_ref[0, 0, :, h, :] = k[:, cs]
        v_ref[0, 0, :, h, :] = v[:, cs]
    kb_ref[0, 0] = k.T.astype(BF)
    vb_ref[0] = v.astype(BF)


def _kvproj(mem, wk, wv, batch, n_mem):
    depth = wk.shape[0]
    w_spec = pl.BlockSpec((1, D_MODEL, D_MODEL), lambda l, b: (l, 0, 0))
    out5 = pl.BlockSpec((1, 1, n_mem, XA_HEADS, XA_HEAD_DIM), lambda l, b: (l, b, 0, 0, 0))
    out_bf = pl.BlockSpec((1, n_mem, D_MODEL), lambda l, b: (l, b, 0))
    return pl.pallas_call(
        _kvproj_kernel,
        grid=(depth, batch),
        in_specs=[pl.BlockSpec((n_mem, D_MODEL), lambda l, b: (b, 0)), w_spec, w_spec],
        out_specs=[out5, out5, pl.BlockSpec((1, 1, D_MODEL, n_mem), lambda l, b: (l, b, 0, 0)), out_bf],
        out_shape=[jax.ShapeDtypeStruct((depth, batch, n_mem, XA_HEADS, XA_HEAD_DIM), F32)] * 2
        + [jax.ShapeDtypeStruct((depth, batch, D_MODEL, n_mem), BF),
           jax.ShapeDtypeStruct((depth, batch * n_mem, D_MODEL), BF)],
        compiler_params=_params(("parallel", "parallel"),
                                ((n_mem, D_MODEL), F32, 2), ((D_MODEL, D_MODEL), BF, 4),
                                ((n_mem, D_MODEL), F32, 4), ((n_mem, D_MODEL), BF, 4)),
        name="kvproj",
    )(mem, wk, wv)


def _inproj_slab(h, w_ref, p_out, j):
    w = w_ref[0, :, j * D_MODEL:(j + 1) * D_MODEL]
    p_out[j] = _dot(h, _unpack_rows(w)).astype(p_out.dtype)


def _inproj_gate_lowrank(h, walr_ref):
    return _dot_nt(h, walr_ref[...].astype(BF)).astype(BF)


def _inproj_gate(alr, wa2_ref, ba_ref, la_out):
    z = _dot(alr, wa2_ref[...]) + ba_ref[...]
    la_out[...] = (jnp.minimum(z, 0.0) - jnp.log(1.0 + jnp.exp(-jnp.abs(z)))) * (1.0 / GATE_TAU)


def _merge_term(br_ref, p, wbr_ref, n):
    return jax.nn.sigmoid(p[SLAB_MERGE + n].astype(F32)) * _dot(br_ref[n].astype(BF), _unpack_rows(wbr_ref[0, n]))


def _merge_finish(merged, x, wo_ref, fg_ref, final):
    x_new = x + _dot(merged.astype(BF), _unpack_rows(wo_ref[0]))
    return _rms(x_new, fg_ref[...]) if final else x_new


def _chunk_cumsum_matrix(tm):
    row = lax.broadcasted_iota(jnp.int32, (tm, tm), 0)
    col = lax.broadcasted_iota(jnp.int32, (tm, tm), 1)
    return (((row // CHUNK) == (col // CHUNK)) & (row >= col)).astype(BF)


def _gla_cumdecay(cum_ref, la_ref):
    la = la_ref[...]
    la_hi = la.astype(BF)
    la_lo = (la - la_hi.astype(F32)).astype(BF)
    return _dot(cum_ref[...], la_hi) + _dot(cum_ref[...], la_lo)


def _gla_chunk(p, bcum, gg_ref, br_ref, st_ref, c, filler):
    crow = lax.broadcasted_iota(jnp.int32, (CHUNK, CHUNK), 0)
    ccol = lax.broadcasted_iota(jnp.int32, (CHUNK, CHUNK), 1)
    causal = crow >= ccol
    rows = slice(c * CHUNK, (c + 1) * CHUNK)
    heads = range(GLA_HEADS)
    q_dec, k_end, decay, att = [], [], [], []
    for h in heads:
        kc = slice(h * GLA_DK, (h + 1) * GLA_DK)
        kc2 = slice(GLA_KEY_WIDTH + h * GLA_DK, GLA_KEY_WIDTH + (h + 1) * GLA_DK)
        b = bcum[rows, kc]
        b_mid = b[CHUNK // 2 - 1:CHUNK // 2, :]
        b_last = b[CHUNK - 1:CHUNK, :]
        q = p[SLAB_QK, rows, kc].astype(F32) * (GLA_DK ** -0.5)
        k = p[SLAB_QK, rows, kc2].astype(F32)
        q_dec.append((q * jnp.exp(b)).astype(BF))
        q_mid = (q * jnp.exp(b - b_mid)).astype(BF)
        k_mid = (k * jnp.exp(b_mid - b)).astype(BF)
        k_end.append((k * jnp.exp(b_last - b)).astype(BF))
        decay.append(jnp.exp(b_last))
        att.append(_dot_nt(q_mid, k_mid))
    filler()
    o = []
    for h in heads:
        vc = slice(h * GLA_DV, (h + 1) * GLA_DV)
        a = jnp.where(causal, att[h], 0.0).astype(BF)
        s_before = st_ref[h].T.astype(BF)
        o.append(_dot(jnp.concatenate([a, q_dec[h]], axis=1),
                      jnp.concatenate([p[SLAB_V, rows, vc], s_before], axis=0)))
    for h in heads:
        vc = slice(h * GLA_DV, (h + 1) * GLA_DV)
        st_ref[h] = decay[h] * st_ref[h] + _dot_tn(k_end[h], p[SLAB_V, rows, vc]).T
    for h in heads:
        vc = slice(h * GLA_DV, (h + 1) * GLA_DV)
        g = p[SLAB_GLA_G, rows, vc].astype(F32)
        br_ref[0, rows, vc] = (_rms(o[h], gg_ref[:, vc]) * _silu(g)).astype(BF)


def _pool_branch(p, pw_ref, ps_ref, br_ref, ubuf_ref, t, tm):
    u = p[SLAB_U].astype(F32)
    ubuf_ref[HIST_PAD:HIST_PAD + tm, :] = u
    pos = t * tm + lax.broadcasted_iota(jnp.int32, (tm, 1), 0)
    for g, w in enumerate(POOL_WINDOWS):
        cs = slice(g * POOL_GROUP_DIM, (g + 1) * POOL_GROUP_DIM)
        ug = u[:, cs]
        s = ug
        for j in range(1, w):
            s = s + ubuf_ref[HIST_PAD - j:HIST_PAD - j + tm, cs]
        cnt = jnp.minimum(w, pos + 1).astype(F32)
        diff = s / cnt - ug
        mixed = _dot(diff.astype(BF), pw_ref[g]) * ps_ref[:, cs]
        pg = p[SLAB_POOL_G, :, cs].astype(F32)
        br_ref[1, :, cs] = (mixed * _silu(pg)).astype(BF)
    ubuf_ref[0:HIST_PAD, :] = u[tm - HIST_PAD:tm]


def _xattn_probs(p, mk_ref):
    out = []
    for h in range(XA_HEADS):
        cs = slice(h * XA_HEAD_DIM, (h + 1) * XA_HEAD_DIM)
        s = _dot(p[SLAB_XQ, :, cs], mk_ref[0, 0, cs, :]) * (XA_HEAD_DIM ** -0.5)
        pr = jnp.exp(s - jnp.max(s, axis=-1, keepdims=True))
        out.append((pr.astype(BF), jnp.sum(pr, axis=-1, keepdims=True)))
    return out


def _xattn_branch(p, probs, mv_ref, br_ref):
    for h in range(XA_HEADS):
        cs = slice(h * XA_HEAD_DIM, (h + 1) * XA_HEAD_DIM)
        pr, denom = probs[h]
        o = _dot(pr, mv_ref[0, :, cs]) / denom
        xg = p[SLAB_XG, :, cs].astype(F32)
        br_ref[2, :, cs] = (o * _silu(xg)).astype(BF)


def _prompt_layer_kernel(xn_ref, xc_ref, g_ref, w_ref, walr_ref, wa2_ref, ba_ref, cum_ref,
                         mk_ref, mv_ref, pw_ref, ps_ref, gg_ref, wbr_ref, wo_ref, fg_ref,
                         out_ref, s_out_ref, hist_out_ref,
                         h_scr, p_scr, la_scr, br_scr, st_ref, ubuf_ref, *, tm, nt, final):
    s = pl.program_id(0)
    t = jnp.maximum(s - 1, 0) % nt
    slot_w = s % 2
    slot_r = 1 - slot_w

    @pl.when(s == 0)
    def _():
        p_scr[1] = jnp.zeros(p_scr.shape[1:], p_scr.dtype)
        la_scr[1] = jnp.zeros(la_scr.shape[1:], la_scr.dtype)

    @pl.when(t == 0)
    def _():
        st_ref[...] = jnp.zeros_like(st_ref)
        ubuf_ref[0:HIST_PAD, :] = jnp.zeros((HIST_PAD, D_MODEL), F32)

    @pl.when(s == 0)
    def _():
        h_scr[0] = _rms(xc_ref[...], g_ref[...]).astype(BF)

    p_next = p_scr.at[slot_w]
    p = p_scr.at[slot_r]

    slabs = list(range(N_SLABS))

    def filler(n=1):
        for _ in range(n):
            _inproj_slab(h_scr[slot_w], w_ref, p_next, slabs.pop(0))

    alr = _inproj_gate_lowrank(h_scr[slot_w], walr_ref)
    bcum = _gla_cumdecay(cum_ref, la_scr.at[slot_r])
    filler()
    _inproj_gate(alr, wa2_ref, ba_ref, la_scr.at[slot_w])
    per_site = (N_SLABS - 2) // (2 * (tm // CHUNK))
    probs = None
    for c in range(tm // CHUNK):
        _gla_chunk(p, bcum, gg_ref, br_scr, st_ref, c, functools.partial(filler, per_site))
        filler(per_site)
        if c == 0:
            probs = _xattn_probs(p, mk_ref)
        elif c == 1:
            _xattn_branch(p, probs, mv_ref, br_scr)
            _pool_branch(p, pw_ref, ps_ref, br_scr, ubuf_ref, t, tm)
    h_scr[slot_r] = _rms(xn_ref[...], g_ref[...]).astype(BF)
    merged = _merge_term(br_scr, p, wbr_ref, 2)
    merged = merged + _merge_term(br_scr, p, wbr_ref, 1)
    merged = merged + _merge_term(br_scr, p, wbr_ref, 0)
    filler()
    assert not slabs
    out_ref[...] = _merge_finish(merged, xc_ref[...], wo_ref, fg_ref, final)

    @pl.when((s > 0) & (t == nt - 1))
    def _():
        for h in range(GLA_HEADS):
            s_out_ref[0, 0, h] = st_ref[h].T
        hist_out_ref[0, 0] = ubuf_ref[1:HIST_PAD, :]


def _prompt_layer(x, ngain, w_proj, w_alr, wa2, ba, mk, mv, pool_w, pool_scale, gla_gain,
                  w_branch, w_out, fgain, carry, layer, depth, batch, seq, n_mem, tm, final):
    nt = seq // tm
    n_tiles = batch * nt
    assert tm // CHUNK == 2 and (N_SLABS - 2) % (2 * (tm // CHUNK)) == 0, "slab placement assumes two GLA chunks"
    n_alias = 0 if carry is None else len(carry)
    n_in = 16
    kern = _drop_alias_refs(functools.partial(_prompt_layer_kernel, tm=tm, nt=nt, final=final), n_in, n_alias)
    any_spec = pl.BlockSpec(memory_space=pl.ANY)
    scratch = [
        ((2, tm, D_MODEL), BF, 1),
        ((2, N_SLABS, tm, D_MODEL), BF, 1),
        ((2, tm, GLA_KEY_WIDTH), F32, 1),
        ((N_BRANCH, tm, D_MODEL), BF, 1),
        ((GLA_HEADS, GLA_DV, GLA_DK), F32, 1),
        ((HIST_PAD + tm, D_MODEL), F32, 1),
    ]

    def cur(s):
        return jnp.maximum(s - 1, 0)

    def seq_of(s):
        return cur(s) // nt

    return pl.pallas_call(
        kern,
        grid=(n_tiles + 1,),
        in_specs=[
            pl.BlockSpec((tm, D_MODEL), lambda s: (jnp.minimum(s + 1, n_tiles - 1), 0)),
            pl.BlockSpec((tm, D_MODEL), lambda s: (cur(s), 0)),
            _const_spec((1, D_MODEL)),
            _layer_spec((D_MODEL // 2, N_SLABS * D_MODEL), layer),
            _const_spec((GATE_RANK, D_MODEL)),
            _const_spec((GATE_RANK, GLA_KEY_WIDTH)),
            _const_spec((1, GLA_KEY_WIDTH)),
            _const_spec((tm, tm)),
            pl.BlockSpec((1, 1, D_MODEL, n_mem), lambda s: (layer, seq_of(s), 0, 0)),
            pl.BlockSpec((1, n_mem, D_MODEL), lambda s: (layer, seq_of(s), 0)),
            _const_spec((len(POOL_WINDOWS), POOL_GROUP_DIM, POOL_GROUP_DIM)),
            _const_spec((1, D_MODEL)),
            _const_spec((1, D_MODEL)),
            _layer_spec((N_BRANCH, D_MODEL // 2, D_MODEL), layer),
            _layer_spec((D_MODEL // 2, D_MODEL), layer),
            _const_spec((1, D_MODEL)),
        ] + [any_spec] * n_alias,
        out_specs=[
            pl.BlockSpec((tm, D_MODEL), lambda s: (cur(s), 0)),
            pl.BlockSpec((1, 1, GLA_HEADS, GLA_DK, GLA_DV), lambda s: (layer, seq_of(s), 0, 0, 0)),
            pl.BlockSpec((1, 1, POOL_HIST, D_MODEL), lambda s: (layer, seq_of(s), 0, 0)),
        ],
        out_shape=[
            jax.ShapeDtypeStruct((n_tiles * tm, D_MODEL), F32),
            jax.ShapeDtypeStruct((depth, batch, GLA_HEADS, GLA_DK, GLA_DV), F32),
            jax.ShapeDtypeStruct((depth, batch, POOL_HIST, D_MODEL), F32),
        ],
        scratch_shapes=[pltpu.VMEM(shape, dtype) for shape, dtype, _ in scratch],
        input_output_aliases={n_in + a: 1 + a for a in range(n_alias)},
        compiler_params=_params(
            ("arbitrary",), *scratch,
            ((tm, D_MODEL), F32, 6),
            ((D_MODEL // 2, (N_SLABS + N_BRANCH + 1) * D_MODEL), jnp.uint32, 1),
            ((n_mem, D_MODEL), BF, 4),
            ((tm, tm), BF, 1),
            ((len(POOL_WINDOWS), POOL_GROUP_DIM, POOL_GROUP_DIM), BF, 1),
            ((GLA_HEADS, GLA_DK, GLA_DV), F32, 2)),
        name="prompt_layer",
    )(x, x, ngain, w_proj, w_alr, wa2, ba, _chunk_cumsum_matrix(tm),
      mk, mv, pool_w, pool_scale, gla_gain,
      w_branch, w_out, fgain, *(carry or ()))


def _inproj_kernel(x_ref, g_ref, w_ref, walr_ref, wa2_ref, ba_ref, p_ref, la_ref):
    h = _rms(x_ref[...], g_ref[...]).astype(BF)
    alr = _inproj_gate_lowrank(h, walr_ref)
    for j in range(N_SLABS):
        _inproj_slab(h, w_ref, p_ref, j)
    _inproj_gate(alr, wa2_ref, ba_ref, la_ref)


def _inproj(x, gain, w_proj, w_alr, w_a2, b_a, layer, tm, out_dtype):
    m_rows = x.shape[0]
    return pl.pallas_call(
        _inproj_kernel,
        grid=(m_rows // tm,),
        in_specs=[
            pl.BlockSpec((tm, D_MODEL), lambda i: (i, 0)),
            _const_spec((1, D_MODEL)),
            _layer_spec((D_MODEL // 2, N_SLABS * D_MODEL), layer),
            _const_spec((GATE_RANK, D_MODEL)),
            _const_spec((GATE_RANK, GLA_KEY_WIDTH)),
            _const_spec((1, GLA_KEY_WIDTH)),
        ],
        out_specs=[
            pl.BlockSpec((N_SLABS, tm, D_MODEL), lambda i: (0, i, 0)),
            pl.BlockSpec((tm, GLA_KEY_WIDTH), lambda i: (i, 0)),
        ],
        out_shape=[
            jax.ShapeDtypeStruct((N_SLABS, m_rows, D_MODEL), out_dtype),
            jax.ShapeDtypeStruct((m_rows, GLA_KEY_WIDTH), F32),
        ],
        compiler_params=_params(("parallel",),
                                ((tm, D_MODEL), F32, 2), ((D_MODEL // 2, N_SLABS * D_MODEL), jnp.uint32, 1),
                                ((N_SLABS, tm, D_MODEL), out_dtype, 2), ((tm, GLA_KEY_WIDTH), F32, 2)),
        name="inproj",
    )(x, gain, w_proj, w_alr, w_a2, b_a)


def _sample_mix_kernel(p_ref, la_ref, s0_ref, hist_ref, ck_hbm, cv_hbm, pw_ref, ps_ref, gg_ref,
                       br_ref, s_out_ref, hist_out_ref, diff_ref, kbuf, vbuf, cache_sem, *, sb, layer):
    n_steps = pl.num_programs(0) * pl.num_programs(1)
    step = pl.program_id(0) * pl.num_programs(1) + pl.program_id(1)

    def cache_copy(which, t):
        src, buf = ((ck_hbm, kbuf), (cv_hbm, vbuf))[which]
        slot = t % CACHE_RING
        return pltpu.make_async_copy(src.at[layer, pl.ds(t * sb, sb)], buf.at[slot], cache_sem.at[which, slot])

    @pl.when(step == 0)
    def _():
        for t in range(CACHE_RING - 1):
            cache_copy(0, t).start()
            cache_copy(1, t).start()

    @pl.when(step + CACHE_RING - 1 < n_steps)
    def _():
        cache_copy(0, step + CACHE_RING - 1).start()
        cache_copy(1, step + CACHE_RING - 1).start()

    cache_copy(0, step).wait()
    cache_copy(1, step).wait()
    ck_ref = kbuf.at[step % CACHE_RING]
    cv_ref = vbuf.at[step % CACHE_RING]

    r0 = pl.program_id(1) * sb
    erow = lax.broadcasted_iota(jnp.int32, (GLA_DK, GLA_DK), 0)
    ecol = lax.broadcasted_iota(jnp.int32, (GLA_DK, GLA_DK), 1)
    eye = erow == ecol

    def to_col(x):
        return jnp.sum(jnp.where(eye, jnp.broadcast_to(x, (GLA_DK, GLA_DK)), 0.0), axis=1, keepdims=True)

    for i in range(sb):
        r = pl.ds(r0 + i, 1)
        la = la_ref[r, :]
        qk = p_ref[SLAB_QK, r, :]
        vv = p_ref[SLAB_V, r, :]
        gla_g = p_ref[SLAB_GLA_G, r, :]
        for h in range(GLA_HEADS):
            kc = slice(h * GLA_DK, (h + 1) * GLA_DK)
            kc2 = slice(GLA_KEY_WIDTH + h * GLA_DK, GLA_KEY_WIDTH + (h + 1) * GLA_DK)
            vc = slice(h * GLA_DV, (h + 1) * GLA_DV)
            a_col = to_col(jnp.exp(la[:, kc]))
            q_col = to_col(qk[:, kc] * (GLA_DK ** -0.5))
            k_col = to_col(qk[:, kc2])
            s_new = a_col * s0_ref[0, i, h] + k_col * vv[:, vc]
            s_out_ref[0, i, h] = s_new
            o = jnp.sum(q_col * s_new, axis=0, keepdims=True)
            br_ref[0, r, vc] = _rms(o, gg_ref[:, vc]) * _silu(gla_g[:, vc])

        u = p_ref[SLAB_U, r, :]
        for g, w in enumerate(POOL_WINDOWS):
            cs = slice(g * POOL_GROUP_DIM, (g + 1) * POOL_GROUP_DIM)
            past = jnp.sum(hist_ref[0, POOL_HIST - (w - 1):POOL_HIST, r, cs], axis=0)
            diff_ref[r, cs] = (u[:, cs] + past) / float(w) - u[:, cs]
        hist_out_ref[0, 0:POOL_HIST - 1, r, :] = hist_ref[0, 1:POOL_HIST, r, :]
        hist_out_ref[0, POOL_HIST - 1, r, :] = u

        xq = p_ref[SLAB_XQ, r, :]
        xg = p_ref[SLAB_XG, r, :]
        half_cols = [slice(h * XA_HEAD_DIM + j * LANES, h * XA_HEAD_DIM + (j + 1) * LANES)
                     for j in range(XA_HEAD_DIM // LANES) for h in range(XA_HEADS)]
        xq_rows = jnp.concatenate([xq[:, cs] for cs in half_cols], axis=0)
        n_mem = ck_ref.shape[1]
        prod = (ck_ref[i] * xq_rows[None]).reshape(n_mem * SUBLANES, LANES).astype(BF)
        part = _dot(prod, jnp.ones((LANES, LANES), BF)).reshape(n_mem, SUBLANES, LANES)
        s = (part + pltpu.roll(part, XA_HEADS, axis=1)) * (XA_HEAD_DIM ** -0.5)
        p = jnp.exp(s - jnp.max(s, axis=0, keepdims=True))
        o = jnp.sum(p * cv_ref[i], axis=0) / jnp.sum(p, axis=0)
        halves = XA_HEAD_DIM // LANES
        o_row = jnp.concatenate([o[j * XA_HEADS + h:j * XA_HEADS + h + 1, :]
                                 for h in range(XA_HEADS) for j in range(halves)], axis=1)
        br_ref[2, r, :] = o_row * _silu(xg)

    @pl.when(pl.program_id(1) == pl.num_programs(1) - 1)
    def _():
        for g in range(len(POOL_WINDOWS)):
            cs = slice(g * POOL_GROUP_DIM, (g + 1) * POOL_GROUP_DIM)
            mixed = _dot(diff_ref[:, cs].astype(BF), pw_ref[g]) * ps_ref[:, cs]
            br_ref[1, :, cs] = mixed * _silu(p_ref[SLAB_POOL_G, :, cs])


def _cache_rows_view(c):
    depth, nb, n_mem = c.shape[:3]
    halves = XA_HEAD_DIM // LANES
    c = c.reshape(depth, nb, n_mem, XA_HEADS, halves, LANES)
    return c.transpose(0, 1, 2, 4, 3, 5).reshape(depth, nb, n_mem, halves * XA_HEADS, LANES)


def _sample_mix(p, la, s0, hist, ck, cv, pool_w, pool_scale, gla_gain, carry, layer, sb=SAMPLE_BLOCK):
    nb = s0.shape[1]
    n_mem = ck.shape[2]
    ck, cv = _cache_rows_view(ck), _cache_rows_view(cv)
    rb = SUBLANES
    halves = rb // sb
    n_alias = 0 if carry is None else len(carry)
    n_in = 9
    kern = _drop_alias_refs(functools.partial(_sample_mix_kernel, sb=sb, layer=layer), n_in, n_alias)
    any_spec = pl.BlockSpec(memory_space=pl.ANY)
    cache_ring = (CACHE_RING, sb, n_mem, SUBLANES, LANES)
    return pl.pallas_call(
        kern,
        grid=(nb // rb, halves),
        in_specs=[
            pl.BlockSpec((N_MIX_SLABS, rb, D_MODEL), lambda i, j: (0, i, 0)),
            pl.BlockSpec((rb, GLA_KEY_WIDTH), lambda i, j: (i, 0)),
            pl.BlockSpec((1, sb, GLA_HEADS, GLA_DK, GLA_DV), lambda i, j: (layer, i * halves + j, 0, 0, 0)),
            pl.BlockSpec((1, POOL_HIST, rb, D_MODEL), lambda i, j: (layer, 0, i, 0)),
            any_spec, any_spec,
            _const_spec((len(POOL_WINDOWS), POOL_GROUP_DIM, POOL_GROUP_DIM)),
            _const_spec((1, D_MODEL)),
            _const_spec((1, D_MODEL)),
        ] + [any_spec] * n_alias,
        out_specs=[
            pl.BlockSpec((N_BRANCH, rb, D_MODEL), lambda i, j: (0, i, 0)),
            pl.BlockSpec((1, sb, GLA_HEADS, GLA_DK, GLA_DV), lambda i, j: (layer, i * halves + j, 0, 0, 0)),
            pl.BlockSpec((1, POOL_HIST, rb, D_MODEL), lambda i, j: (layer, 0, i, 0)),
        ],
        out_shape=[
            jax.ShapeDtypeStruct((N_BRANCH, nb, D_MODEL), F32),
            jax.ShapeDtypeStruct(s0.shape, F32),
            jax.ShapeDtypeStruct(hist.shape, F32),
        ],
        scratch_shapes=[pltpu.VMEM((rb, D_MODEL), F32), pltpu.VMEM(cache_ring, F32), pltpu.VMEM(cache_ring, F32),
                        pltpu.SemaphoreType.DMA((2, CACHE_RING))],
        input_output_aliases={n_in + a: 1 + a for a in range(n_alias)},
        compiler_params=_params(
            ("arbitrary", "arbitrary"),
            ((N_MIX_SLABS + N_BRANCH + 1, rb, D_MODEL), F32, 2),
            ((sb, GLA_HEADS, GLA_DK, GLA_DV), F32, 4),
            ((POOL_HIST, rb, D_MODEL), F32, 4),
            (cache_ring, F32, 2),
            ((len(POOL_WINDOWS), POOL_GROUP_DIM, POOL_GROUP_DIM), BF, 1)),
        name="sample_mix",
    )(p, la, s0, hist, ck, cv, pool_w, pool_scale, gla_gain, *(carry or ()))


def _merge_kernel(br_ref, p_ref, x_ref, wbr_ref, wo_ref, fg_ref, out_ref, *, final):
    merged = _merge_term(br_ref, p_ref, wbr_ref, 0)
    for n in range(1, N_BRANCH):
        merged = merged + _merge_term(br_ref, p_ref, wbr_ref, n)
    out_ref[...] = _merge_finish(merged, x_ref[...], wo_ref, fg_ref, final)


def _merge_out(br, p, x, w_branch, w_out, final_gain, layer, final):
    m_rows = x.shape[0]
    whole = lambda shape: pl.BlockSpec(shape, lambda i: (0,) * len(shape))
    return pl.pallas_call(
        functools.partial(_merge_kernel, final=final),
        grid=(1,),
        in_specs=[
            whole((N_BRANCH, m_rows, D_MODEL)),
            whole((N_SLABS, m_rows, D_MODEL)),
            whole((m_rows, D_MODEL)),
            _layer_spec((N_BRANCH, D_MODEL // 2, D_MODEL), layer),
            _layer_spec((D_MODEL // 2, D_MODEL), layer),
            _const_spec((1, D_MODEL)),
        ],
        out_specs=whole((m_rows, D_MODEL)),
        out_shape=jax.ShapeDtypeStruct((m_rows, D_MODEL), F32),
        compiler_params=_params(("arbitrary",),
                                ((N_BRANCH + N_SLABS + 2, m_rows, D_MODEL), F32, 2),
                                ((D_MODEL // 2, (N_BRANCH + 1) * D_MODEL), jnp.uint32, 1)),
        name="merge_out",
    )(br, p, x, w_branch, w_out, final_gain)


def kernel(x_prompt, x_sample, mem_prompt, cache_mem_k, cache_mem_v, state_gla, state_pool, w_in, w_a2, b_a, gla_gain, pool_w, pool_scale, w_mk, w_mv, w_branch, w_out, norm_gain, final_gain):
    batch, seq, _ = x_prompt.shape
    nb = x_sample.shape[0]
    n_mem = mem_prompt.shape[1]
    depth = w_in.shape[0]
    tm = PROMPT_TILE
    assert seq % tm == 0 and nb % SUBLANES == 0 and x_prompt.shape[2] == D_MODEL

    xp = x_prompt.reshape(batch * seq, D_MODEL)
    xs = x_sample.reshape(nb, D_MODEL)
    mem = mem_prompt.reshape(batch * n_mem, D_MODEL)
    fgain = final_gain.reshape(1, D_MODEL)

    mk, mv, mk_bf, mv_bf = _kvproj(mem, w_mk.astype(BF), w_mv.astype(BF), batch, n_mem)

    w_in_t = jnp.swapaxes(w_in, 1, 2)
    pool_rows = jnp.swapaxes(state_pool, 1, 2)
    w_proj = _pack_w_in(w_in_t)
    wb = _pack_matrices(w_branch.reshape(depth * N_BRANCH, D_MODEL, D_MODEL))
    wb = wb.reshape(depth, N_BRANCH, D_MODEL // 2, D_MODEL)
    wo = _pack_matrices(w_out)

    carry_p, carry_s = None, None
    for l in range(depth):
        final = l == depth - 1
        w_alr = w_in_t[l, ALR_START:ALR_START + GATE_RANK, :]
        wa2 = w_a2[l].astype(BF)
        ba = b_a[l].reshape(1, GLA_KEY_WIDTH)
        ngain = norm_gain[l].reshape(1, D_MODEL)
        ggain = gla_gain[l].reshape(1, D_MODEL)
        pscale = pool_scale[l].reshape(1, D_MODEL)
        pw = pool_w[l].astype(BF)

        xp, s_all, hist_all = _prompt_layer(xp, ngain, w_proj, w_alr, wa2, ba, mk_bf, mv_bf, pw, pscale,
                                            ggain, wb, wo, fgain, carry_p, l, depth, batch, seq, n_mem, tm, final)
        carry_p = (s_all, hist_all)

        ps, las = _inproj(xs, ngain, w_proj, w_alr, wa2, ba, l, nb, F32)
        brs, s_new, hist_new = _sample_mix(ps, las, state_gla, pool_rows, cache_mem_k, cache_mem_v,
                                           pw, pscale, ggain, carry_s, l)
        carry_s = (s_new, hist_new)
        xs = _merge_out(brs, ps, xs, wb, wo, fgain, l, final)

    return (xp.reshape(batch, seq, D_MODEL), xs.reshape(nb, 1, D_MODEL),
            mk, mv, carry_p[0], carry_p[1], carry_s[0], jnp.swapaxes(carry_s[1], 1, 2))
```

```python
import functools
import math

import jax
import jax.numpy as jnp
from jax import lax
from jax.experimental import pallas as pl
from jax.experimental.pallas import tpu as pltpu

D_MODEL = 1024
GLA_HEADS = 4
GLA_DK = 128
GLA_DV = 256
GLA_KEY_WIDTH = GLA_HEADS * GLA_DK
GATE_RANK = 16
GATE_TAU = 16.0
CHUNK = 128
POOL_WINDOWS = (2, 4, 8, 16)
POOL_GROUP_DIM = 256
POOL_HIST = 15
HIST_PAD = 16
XA_HEADS = 4
XA_HEAD_DIM = 256
N_BRANCH = 3
EPS = 1e-6
SUBLANES = 8
LANES = 128

SLAB_QK, SLAB_V, SLAB_GLA_G, SLAB_U, SLAB_POOL_G, SLAB_XQ, SLAB_XG, SLAB_MERGE = 0, 1, 2, 3, 4, 5, 6, 7
N_SLABS = 10
N_MIX_SLABS = 7
N_HEAD_SLABS = 3
ALR_START = N_HEAD_SLABS * D_MODEL

BF = jnp.bfloat16
F32 = jnp.float32
MIB = 1 << 20

PROMPT_TILE = 256
SAMPLE_BLOCK = 4
CACHE_RING = 3
VMEM_COMPILER_SCRATCH = 8 * MIB


def _dot(a, b):
    return jnp.dot(a, b, preferred_element_type=F32)


def _dot_nt(a, b):
    return lax.dot_general(a, b, (((1,), (1,)), ((), ())), preferred_element_type=F32)


def _dot_tn(a, b):
    return lax.dot_general(a, b, (((0,), (0,)), ((), ())), preferred_element_type=F32)


def _pack_rows(w):
    return pltpu.bitcast(w.astype(BF), jnp.uint32)


def _unpack_rows(w_words):
    return pltpu.bitcast(w_words, BF)


def _params(sem, *buffers):
    need = sum(math.prod(shape) * jnp.dtype(dtype).itemsize * copies for shape, dtype, copies in buffers)
    return pltpu.CompilerParams(dimension_semantics=sem, vmem_limit_bytes=need + VMEM_COMPILER_SCRATCH)


def _pack_w_in_kernel(a_ref, b_ref, o_ref):
    j = pl.program_id(1)

    @pl.when(j < N_HEAD_SLABS)
    def _():
        o_ref[0] = _pack_rows(a_ref[0].T)

    @pl.when(j >= N_HEAD_SLABS)
    def _():
        o_ref[0] = _pack_rows(jnp.concatenate([a_ref[0, GATE_RANK:, :], b_ref[0]], axis=0).T)


def _pack_w_in(w_in_t):
    depth = w_in_t.shape[0]
    return pl.pallas_call(
        _pack_w_in_kernel,
        grid=(depth, N_SLABS),
        in_specs=[
            pl.BlockSpec((1, D_MODEL, D_MODEL), lambda l, j: (l, j, 0)),
            pl.BlockSpec((1, GATE_RANK, D_MODEL), lambda l, j: (l, (j + 1) * (D_MODEL // GATE_RANK), 0)),
        ],
        out_specs=pl.BlockSpec((1, D_MODEL // 2, D_MODEL), lambda l, j: (l, 0, j)),
        out_shape=jax.ShapeDtypeStruct((depth, D_MODEL // 2, N_SLABS * D_MODEL), jnp.uint32),
        compiler_params=_params(("parallel", "parallel"),
                                ((D_MODEL, D_MODEL), F32, 2), ((GATE_RANK, D_MODEL), F32, 2),
                                ((D_MODEL // 2, D_MODEL), jnp.uint32, 2)),
        name="pack_w_in",
    )(w_in_t, w_in_t)


def _pack_matrices_kernel(a_ref, o_ref):
    o_ref[0] = _pack_rows(a_ref[0])


def _pack_matrices(w):
    return pl.pallas_call(
        _pack_matrices_kernel,
        grid=(w.shape[0],),
        in_specs=[pl.BlockSpec((1, D_MODEL, D_MODEL), lambda r: (r, 0, 0))],
        out_specs=pl.BlockSpec((1, D_MODEL // 2, D_MODEL), lambda r: (r, 0, 0)),
        out_shape=jax.ShapeDtypeStruct((w.shape[0], D_MODEL // 2, D_MODEL), jnp.uint32),
        compiler_params=_params(("parallel",), ((D_MODEL, D_MODEL), F32, 2), ((D_MODEL // 2, D_MODEL), jnp.uint32, 2)),
        name="pack_matrices",
    )(w)


def _silu(x):
    return x * jax.nn.sigmoid(x)


def _rms(x, gain):
    ms = jnp.mean(x * x, axis=-1, keepdims=True)
    return x * lax.rsqrt(ms + EPS) * gain


def _const_spec(shape):
    zeros = (0,) * len(shape)
    return pl.BlockSpec(shape, lambda *_: zeros, pipeline_mode=pl.Buffered(1))


def _layer_spec(shape, layer):
    index = (layer,) + (0,) * len(shape)
    return pl.BlockSpec((1,) + tuple(shape), lambda *_: index, pipeline_mode=pl.Buffered(1))


def _drop_alias_refs(body, n_in, n_alias):
    def kern(*refs):
        return body(*refs[:n_in], *refs[n_in + n_alias:])
    return kern


def _kvproj_kernel(m_ref, wk_ref, wv_ref, k_ref, v_ref, kb_ref, vb_ref):
    m = m_ref[...].astype(BF)
    k = _dot(m, wk_ref[0])
    v = _dot(m, wv_ref[0])
    for h in range(XA_HEADS):
        cs = slice(h * XA_HEAD_DIM, (h + 1) * XA_HEAD_DIM)
        k_ref[0, 0, :, h, :] = k[:, cs]
        v_ref[0, 0, :, h, :] = v[:, cs]
    kb_ref[0, 0] = k.T.astype(BF)
    vb_ref[0] = v.astype(BF)


def _kvproj(mem, wk, wv, batch, n_mem):
    depth = wk.shape[0]
    w_spec = pl.BlockSpec((1, D_MODEL, D_MODEL), lambda l, b: (l, 0, 0))
    out5 = pl.BlockSpec((1, 1, n_mem, XA_HEADS, XA_HEAD_DIM), lambda l, b: (l, b, 0, 0, 0))
    out_bf = pl.BlockSpec((1, n_mem, D_MODEL), lambda l, b: (l, b, 0))
    return pl.pallas_call(
        _kvproj_kernel,
        grid=(depth, batch),
        in_specs=[pl.BlockSpec((n_mem, D_MODEL), lambda l, b: (b, 0)), w_spec, w_spec],
        out_specs=[out5, out5, pl.BlockSpec((1, 1, D_MODEL, n_mem), lambda l, b: (l, b, 0, 0)), out_bf],
        out_shape=[jax.ShapeDtypeStruct((depth, batch, n_mem, XA_HEADS, XA_HEAD_DIM), F32)] * 2
        + [jax.ShapeDtypeStruct((depth, batch, D_MODEL, n_mem), BF),
           jax.ShapeDtypeStruct((depth, batch * n_mem, D_MODEL), BF)],
        compiler_params=_params(("parallel", "parallel"),
                                ((n_mem, D_MODEL), F32, 2), ((D_MODEL, D_MODEL), BF, 4),
                                ((n_mem, D_MODEL), F32, 4), ((n_mem, D_MODEL), BF, 4)),
        name="kvproj",
    )(mem, wk, wv)


def _inproj_slab(h, w_ref, p_out, j):
    w = w_ref[0, :, j * D_MODEL:(j + 1) * D_MODEL]
    p_out[j] = _dot(h, _unpack_rows(w)).astype(p_out.dtype)


def _inproj_gate_lowrank(h, walr_ref):
    return _dot_nt(h, walr_ref[...].astype(BF)).astype(BF)


def _inproj_gate(alr, wa2_ref, ba_ref, la_out):
    z = _dot(alr, wa2_ref[...]) + ba_ref[...]
    la_out[...] = (jnp.minimum(z, 0.0) - jnp.log(1.0 + jnp.exp(-jnp.abs(z)))) * (1.0 / GATE_TAU)


def _merge_term(br_ref, p, wbr_ref, n):
    return jax.nn.sigmoid(p[SLAB_MERGE + n].astype(F32)) * _dot(br_ref[n].astype(BF), _unpack_rows(wbr_ref[0, n]))


def _merge_finish(merged, x, wo_ref, fg_ref, final):
    x_new = x + _dot(merged.astype(BF), _unpack_rows(wo_ref[0]))
    return _rms(x_new, fg_ref[...]) if final else x_new


def _chunk_cumsum_matrix(tm):
    row = lax.broadcasted_iota(jnp.int32, (tm, tm), 0)
    col = lax.broadcasted_iota(jnp.int32, (tm, tm), 1)
    return (((row // CHUNK) == (col // CHUNK)) & (row >= col)).astype(BF)


def _gla_cumdecay(cum_ref, la_ref):
    la = la_ref[...]
    la_hi = la.astype(BF)
    la_lo = (la - la_hi.astype(F32)).astype(BF)
    return _dot(cum_ref[...], la_hi) + _dot(cum_ref[...], la_lo)


def _gla_chunk(p, bcum, gg_ref, br_ref, st_ref, c, filler):
    crow = lax.broadcasted_iota(jnp.int32, (CHUNK, CHUNK), 0)
    ccol = lax.broadcasted_iota(jnp.int32, (CHUNK, CHUNK), 1)
    causal = crow >= ccol
    rows = slice(c * CHUNK, (c + 1) * CHUNK)
    heads = range(GLA_HEADS)
    q_dec, k_end, decay, att = [], [], [], []
    for h in heads:
        kc = slice(h * GLA_DK, (h + 1) * GLA_DK)
        kc2 = slice(GLA_KEY_WIDTH + h * GLA_DK, GLA_KEY_WIDTH + (h + 1) * GLA_DK)
        b = bcum[rows, kc]
        b_mid = b[CHUNK // 2 - 1:CHUNK // 2, :]
        b_last = b[CHUNK - 1:CHUNK, :]
        q = p[SLAB_QK, rows, kc].astype(F32) * (GLA_DK ** -0.5)
        k = p[SLAB_QK, rows, kc2].astype(F32)
        q_dec.append((q * jnp.exp(b)).astype(BF))
        q_mid = (q * jnp.exp(b - b_mid)).astype(BF)
        k_mid = (k * jnp.exp(b_mid - b)).astype(BF)
        k_end.append((k * jnp.exp(b_last - b)).astype(BF))
        decay.append(jnp.exp(b_last))
        att.append(_dot_nt(q_mid, k_mid))
    filler()
    o = []
    for h in heads:
        vc = slice(h * GLA_DV, (h + 1) * GLA_DV)
        a = jnp.where(causal, att[h], 0.0).astype(BF)
        s_before = st_ref[h].T.astype(BF)
        o.append(_dot(jnp.concatenate([a, q_dec[h]], axis=1),
                      jnp.concatenate([p[SLAB_V, rows, vc], s_before], axis=0)))
    for h in heads:
        vc = slice(h * GLA_DV, (h + 1) * GLA_DV)
        st_ref[h] = decay[h] * st_ref[h] + _dot_tn(k_end[h], p[SLAB_V, rows, vc]).T
    for h in heads:
        vc = slice(h * GLA_DV, (h + 1) * GLA_DV)
        g = p[SLAB_GLA_G, rows, vc].astype(F32)
        br_ref[0, rows, vc] = (_rms(o[h], gg_ref[:, vc]) * _silu(g)).astype(BF)


def _pool_branch(p, pw_ref, ps_ref, br_ref, ubuf_ref, t, tm):
    u = p[SLAB_U].astype(F32)
    ubuf_ref[HIST_PAD:HIST_PAD + tm, :] = u
    pos = t * tm + lax.broadcasted_iota(jnp.int32, (tm, 1), 0)
    for g, w in enumerate(POOL_WINDOWS):
        cs = slice(g * POOL_GROUP_DIM, (g + 1) * POOL_GROUP_DIM)
        ug = u[:, cs]
        s = ug
        for j in range(1, w):
            s = s + ubuf_ref[HIST_PAD - j:HIST_PAD - j + tm, cs]
        cnt = jnp.minimum(w, pos + 1).astype(F32)
        diff = s / cnt - ug
        mixed = _dot(diff.astype(BF), pw_ref[g]) * ps_ref[:, cs]
        pg = p[SLAB_POOL_G, :, cs].astype(F32)
        br_ref[1, :, cs] = (mixed * _silu(pg)).astype(BF)
    ubuf_ref[0:HIST_PAD, :] = u[tm - HIST_PAD:tm]


def _xattn_probs(p, mk_ref):
    out = []
    for h in range(XA_HEADS):
        cs = slice(h * XA_HEAD_DIM, (h + 1) * XA_HEAD_DIM)
        s = _dot(p[SLAB_XQ, :, cs], mk_ref[0, 0, cs, :]) * (XA_HEAD_DIM ** -0.5)
        pr = jnp.exp(s - jnp.max(s, axis=-1, keepdims=True))
        out.append((pr.astype(BF), jnp.sum(pr, axis=-1, keepdims=True)))
    return out


def _xattn_branch(p, probs, mv_ref, br_ref):
    for h in range(XA_HEADS):
        cs = slice(h * XA_HEAD_DIM, (h + 1) * XA_HEAD_DIM)
        pr, denom = probs[h]
        o = _dot(pr, mv_ref[0, :, cs]) / denom
        xg = p[SLAB_XG, :, cs].astype(F32)
        br_ref[2, :, cs] = (o * _silu(xg)).astype(BF)


def _prompt_layer_kernel(xn_ref, xc_ref, g_ref, w_ref, walr_ref, wa2_ref, ba_ref, cum_ref,
                         mk_ref, mv_ref, pw_ref, ps_ref, gg_ref, wbr_ref, wo_ref, fg_ref,
                         out_ref, s_out_ref, hist_out_ref,
                         h_scr, p_scr, la_scr, br_scr, st_ref, ubuf_ref, *, tm, nt, final):
    s = pl.program_id(0)
    t = jnp.maximum(s - 1, 0) % nt
    slot_w = s % 2
    slot_r = 1 - slot_w

    @pl.when(s == 0)
    def _():
        p_scr[1] = jnp.zeros(p_scr.shape[1:], p_scr.dtype)
        la_scr[1] = jnp.zeros(la_scr.shape[1:], la_scr.dtype)

    @pl.when(t == 0)
    def _():
        st_ref[...] = jnp.zeros_like(st_ref)
        ubuf_ref[0:HIST_PAD, :] = jnp.zeros((HIST_PAD, D_MODEL), F32)

    @pl.when(s == 0)
    def _():
        h_scr[0] = _rms(xc_ref[...], g_ref[...]).astype(BF)

    p_next = p_scr.at[slot_w]
    p = p_scr.at[slot_r]

    slabs = list(range(N_SLABS))

    def filler(n=1):
        for _ in range(n):
            _inproj_slab(h_scr[slot_w], w_ref, p_next, slabs.pop(0))

    alr = _inproj_gate_lowrank(h_scr[slot_w], walr_ref)
    bcum = _gla_cumdecay(cum_ref, la_scr.at[slot_r])
    filler()
    _inproj_gate(alr, wa2_ref, ba_ref, la_scr.at[slot_w])
    per_site = (N_SLABS - 2) // (2 * (tm // CHUNK))
    probs = None
    for c in range(tm // CHUNK):
        _gla_chunk(p, bcum, gg_ref, br_scr, st_ref, c, functools.partial(filler, per_site))
        filler(per_site)
        if c == 0:
            probs = _xattn_probs(p, mk_ref)
        elif c == 1:
            _xattn_branch(p, probs, mv_ref, br_scr)
            _pool_branch(p, pw_ref, ps_ref, br_scr, ubuf_ref, t, tm)
    h_scr[slot_r] = _rms(xn_ref[...], g_ref[...]).astype(BF)
    merged = _merge_term(br_scr, p, wbr_ref, 2)
    merged = merged + _merge_term(br_scr, p, wbr_ref, 1)
    merged = merged + _merge_term(br_scr, p, wbr_ref, 0)
    filler()
    assert not slabs
    out_ref[...] = _merge_finish(merged, xc_ref[...], wo_ref, fg_ref, final)

    @pl.when((s > 0) & (t == nt - 1))
    def _():
        for h in range(GLA_HEADS):
            s_out_ref[0, 0, h] = st_ref[h].T
        hist_out_ref[0, 0] = ubuf_ref[1:HIST_PAD, :]


def _prompt_layer(x, ngain, w_proj, w_alr, wa2, ba, mk, mv, pool_w, pool_scale, gla_gain,
                  w_branch, w_out, fgain, carry, layer, depth, batch, seq, n_mem, tm, final):
    nt = seq // tm
    n_tiles = batch * nt
    assert tm // CHUNK == 2 and (N_SLABS - 2) % (2 * (tm // CHUNK)) == 0, "slab placement assumes two GLA chunks"
    n_alias = 0 if carry is None else len(carry)
    n_in = 16
    kern = _drop_alias_refs(functools.partial(_prompt_layer_kernel, tm=tm, nt=nt, final=final), n_in, n_alias)
    any_spec = pl.BlockSpec(memory_space=pl.ANY)
    scratch = [
        ((2, tm, D_MODEL), BF, 1),
        ((2, N_SLABS, tm, D_MODEL), BF, 1),
        ((2, tm, GLA_KEY_WIDTH), F32, 1),
        ((N_BRANCH, tm, D_MODEL), BF, 1),
        ((GLA_HEADS, GLA_DV, GLA_DK), F32, 1),
        ((HIST_PAD + tm, D_MODEL), F32, 1),
    ]

    def cur(s):
        return jnp.maximum(s - 1, 0)

    def seq_of(s):
        return cur(s) // nt

    return pl.pallas_call(
        kern,
        grid=(n_tiles + 1,),
        in_specs=[
            pl.BlockSpec((tm, D_MODEL), lambda s: (jnp.minimum(s + 1, n_tiles - 1), 0)),
            pl.BlockSpec((tm, D_MODEL), lambda s: (cur(s), 0)),
            _const_spec((1, D_MODEL)),
            _layer_spec((D_MODEL // 2, N_SLABS * D_MODEL), layer),
            _const_spec((GATE_RANK, D_MODEL)),
            _const_spec((GATE_RANK, GLA_KEY_WIDTH)),
            _const_spec((1, GLA_KEY_WIDTH)),
            _const_spec((tm, tm)),
            pl.BlockSpec((1, 1, D_MODEL, n_mem), lambda s: (layer, seq_of(s), 0, 0)),
            pl.BlockSpec((1, n_mem, D_MODEL), lambda s: (layer, seq_of(s), 0)),
            _const_spec((len(POOL_WINDOWS), POOL_GROUP_DIM, POOL_GROUP_DIM)),
            _const_spec((1, D_MODEL)),
            _const_spec((1, D_MODEL)),
            _layer_spec((N_BRANCH, D_MODEL // 2, D_MODEL), layer),
            _layer_spec((D_MODEL // 2, D_MODEL), layer),
            _const_spec((1, D_MODEL)),
        ] + [any_spec] * n_alias,
        out_specs=[
            pl.BlockSpec((tm, D_MODEL), lambda s: (cur(s), 0)),
            pl.BlockSpec((1, 1, GLA_HEADS, GLA_DK, GLA_DV), lambda s: (layer, seq_of(s), 0, 0, 0)),
            pl.BlockSpec((1, 1, POOL_HIST, D_MODEL), lambda s: (layer, seq_of(s), 0, 0)),
        ],
        out_shape=[
            jax.ShapeDtypeStruct((n_tiles * tm, D_MODEL), F32),
            jax.ShapeDtypeStruct((depth, batch, GLA_HEADS, GLA_DK, GLA_DV), F32),
            jax.ShapeDtypeStruct((depth, batch, POOL_HIST, D_MODEL), F32),
        ],
        scratch_shapes=[pltpu.VMEM(shape, dtype) for shape, dtype, _ in scratch],
        input_output_aliases={n_in + a: 1 + a for a in range(n_alias)},
        compiler_params=_params(
            ("arbitrary",), *scratch,
            ((tm, D_MODEL), F32, 6),
            ((D_MODEL // 2, (N_SLABS + N_BRANCH + 1) * D_MODEL), jnp.uint32, 1),
            ((n_mem, D_MODEL), BF, 4),
            ((tm, tm), BF, 1),
            ((len(POOL_WINDOWS), POOL_GROUP_DIM, POOL_GROUP_DIM), BF, 1),
            ((GLA_HEADS, GLA_DK, GLA_DV), F32, 2)),
        name="prompt_layer",
    )(x, x, ngain, w_proj, w_alr, wa2, ba, _chunk_cumsum_matrix(tm),
      mk, mv, pool_w, pool_scale, gla_gain,
      w_branch, w_out, fgain, *(carry or ()))


def _inproj_kernel(x_ref, g_ref, w_ref, walr_ref, wa2_ref, ba_ref, p_ref, la_ref):
    h = _rms(x_ref[...], g_ref[...]).astype(BF)
    alr = _inproj_gate_lowrank(h, walr_ref)
    for j in range(N_SLABS):
        _inproj_slab(h, w_ref, p_ref, j)
    _inproj_gate(alr, wa2_ref, ba_ref, la_ref)


def _inproj(x, gain, w_proj, w_alr, w_a2, b_a, layer, tm, out_dtype):
    m_rows = x.shape[0]
    return pl.pallas_call(
        _inproj_kernel,
        grid=(m_rows // tm,),
        in_specs=[
            pl.BlockSpec((tm, D_MODEL), lambda i: (i, 0)),
            _const_spec((1, D_MODEL)),
            _layer_spec((D_MODEL // 2, N_SLABS * D_MODEL), layer),
            _const_spec((GATE_RANK, D_MODEL)),
            _const_spec((GATE_RANK, GLA_KEY_WIDTH)),
            _const_spec((1, GLA_KEY_WIDTH)),
        ],
        out_specs=[
            pl.BlockSpec((N_SLABS, tm, D_MODEL), lambda i: (0, i, 0)),
            pl.BlockSpec((tm, GLA_KEY_WIDTH), lambda i: (i, 0)),
        ],
        out_shape=[
            jax.ShapeDtypeStruct((N_SLABS, m_rows, D_MODEL), out_dtype),
            jax.ShapeDtypeStruct((m_rows, GLA_KEY_WIDTH), F32),
        ],
        compiler_params=_params(("parallel",),
                                ((tm, D_MODEL), F32, 2), ((D_MODEL // 2, N_SLABS * D_MODEL), jnp.uint32, 1),
                                ((N_SLABS, tm, D_MODEL), out_dtype, 2), ((tm, GLA_KEY_WIDTH), F32, 2)),
        name="inproj",
    )(x, gain, w_proj, w_alr, w_a2, b_a)


def _sample_mix_kernel(p_ref, la_ref, s0_ref, hist_ref, ck_hbm, cv_hbm, pw_ref, ps_ref, gg_ref,
                       br_ref, s_out_ref, hist_out_ref, diff_ref, kbuf, vbuf, cache_sem, *, sb, layer):
    n_steps = pl.num_programs(0) * pl.num_programs(1)
    step = pl.program_id(0) * pl.num_programs(1) + pl.program_id(1)

    def cache_copy(which, t):
        src, buf = ((ck_hbm, kbuf), (cv_hbm, vbuf))[which]
        slot = t % CACHE_RING
        return pltpu.make_async_copy(src.at[layer, pl.ds(t * sb, sb)], buf.at[slot], cache_sem.at[which, slot])

    @pl.when(step == 0)
    def _():
        for t in range(CACHE_RING - 1):
            cache_copy(0, t).start()
            cache_copy(1, t).start()

    @pl.when(step + CACHE_RING - 1 < n_steps)
    def _():
        cache_copy(0, step + CACHE_RING - 1).start()
        cache_copy(1, step + CACHE_RING - 1).start()

    r0 =pl.program_id(1) * sb
    erow = lax.broadcasted_iota(jnp.int32, (GLA_DK, GLA_DK), 0)
    ecol = lax.broadcasted_iota(jnp.int32, (GLA_DK, GLA_DK), 1)
    eye = erow == ecol

    def to_col(x):
        return jnp.sum(jnp.where(eye, jnp.broadcast_to(x, (GLA_DK, GLA_DK)), 0.0), axis=1, keepdims=True)

    for i in range(sb):
        r = pl.ds(r0 + i, 1)
        la = la_ref[r, :]
        qk = p_ref[SLAB_QK, r, :]
        vv = p_ref[SLAB_V, r, :]
        gla_g = p_ref[SLAB_GLA_G, r, :]
        for h in range(GLA_HEADS):
            kc = slice(h * GLA_DK, (h + 1) * GLA_DK)
            kc2 = slice(GLA_KEY_WIDTH + h * GLA_DK, GLA_KEY_WIDTH + (h + 1) * GLA_DK)
            vc = slice(h * GLA_DV, (h + 1) * GLA_DV)
            a_col = to_col(jnp.exp(la[:, kc]))
            q_col = to_col(qk[:, kc] * (GLA_DK ** -0.5))
            k_col = to_col(qk[:, kc2])
            s_new = a_col * s0_ref[0, i, h] + k_col * vv[:, vc]
            s_out_ref[0, i, h] = s_new
            o = jnp.sum(q_col * s_new, axis=0, keepdims=True)
            br_ref[0, r, vc] = _rms(o, gg_ref[:, vc]) * _silu(gla_g[:, vc])

        u = p_ref[SLAB_U, r, :]
        for g, w in enumerate(POOL_WINDOWS):
            cs = slice(g * POOL_GROUP_DIM, (g + 1) * POOL_GROUP_DIM)
            past = jnp.sum(hist_ref[0, POOL_HIST - (w - 1):POOL_HIST, r, cs], axis=0)
            diff_ref[r, cs] = (u[:, cs] + past) / float(w) - u[:, cs]
        hist_out_ref[0, 0:POOL_HIST - 1, r, :] = hist_ref[0, 1:POOL_HIST, r, :]
        hist_out_ref[0, POOL_HIST - 1, r, :] = u

    cache_copy(0, step).wait()
    cache_copy(1, step).wait()
    ck_ref = kbuf.at[step % CACHE_RING]
    cv_ref = vbuf.at[step % CACHE_RING]

    for i in range(sb):
        r = pl.ds(r0 + i, 1)
        xq = p_ref[SLAB_XQ, r, :]
        xg = p_ref[SLAB_XG, r, :]
        half_cols = [slice(h * XA_HEAD_DIM + j * LANES, h * XA_HEAD_DIM + (j + 1) * LANES)
                     for j in range(XA_HEAD_DIM // LANES) for h in range(XA_HEADS)]
        xq_rows = jnp.concatenate([xq[:, cs] for cs in half_cols], axis=0)
        n_mem = ck_ref.shape[1]
        prod = (ck_ref[i] * xq_rows[None]).reshape(n_mem * SUBLANES, LANES).astype(BF)
        part = _dot(prod, jnp.ones((LANES, LANES), BF)).reshape(n_mem, SUBLANES, LANES)
        s = (part + pltpu.roll(part, XA_HEADS, axis=1)) * (XA_HEAD_DIM ** -0.5)
        p = jnp.exp(s - jnp.max(s, axis=0, keepdims=True))
        o = jnp.sum(p * cv_ref[i], axis=0) / jnp.sum(p, axis=0)
        halves = XA_HEAD_DIM // LANES
        o_row = jnp.concatenate([o[j * XA_HEADS + h:j * XA_HEADS + h + 1, :]
                                 for h in range(XA_HEADS) for j in range(halves)], axis=1)
        br_ref[2, r, :] = o_row * _silu(xg)

    @pl.when(pl.program_id(1) == pl.num_programs(1) - 1)
    def _():
        for g in range(len(POOL_WINDOWS)):
            cs = slice(g * POOL_GROUP_DIM, (g + 1) * POOL_GROUP_DIM)
            mixed = _dot(diff_ref[:, cs].astype(BF), pw_ref[g]) * ps_ref[:, cs]
            br_ref[1, :, cs] = mixed * _silu(p_ref[SLAB_POOL_G, :, cs])


def _cache_rows_view(c):
    depth, nb, n_mem = c.shape[:3]
    halves = XA_HEAD_DIM // LANES
    c = c.reshape(depth, nb, n_mem, XA_HEADS, halves, LANES)
    return c.transpose(0, 1, 2, 4, 3, 5).reshape(depth, nb, n_mem, halves * XA_HEADS, LANES)


def _sample_mix(p, la, s0, hist, ck, cv, pool_w, pool_scale, gla_gain, carry, layer, sb=SAMPLE_BLOCK):
    nb = s0.shape[1]
    n_mem = ck.shape[2]
    ck, cv = _cache_rows_view(ck), _cache_rows_view(cv)
    rb = SUBLANES
    halves = rb // sb
    n_alias = 0 if carry is None else len(carry)
    n_in = 9
    kern = _drop_alias_refs(functools.partial(_sample_mix_kernel, sb=sb, layer=layer), n_in, n_alias)
    any_spec = pl.BlockSpec(memory_space=pl.ANY)
    cache_ring = (CACHE_RING, sb, n_mem, SUBLANES, LANES)
    return pl.pallas_call(
        kern,
        grid=(nb // rb, halves),
        in_specs=[
            pl.BlockSpec((N_MIX_SLABS, rb, D_MODEL), lambda i, j: (0, i, 0)),
            pl.BlockSpec((rb, GLA_KEY_WIDTH), lambda i, j: (i, 0)),
            pl.BlockSpec((1, sb, GLA_HEADS, GLA_DK, GLA_DV), lambda i, j: (layer, i * halves + j, 0, 0, 0)),
            pl.BlockSpec((1, POOL_HIST, rb, D_MODEL), lambda i, j: (layer, 0, i, 0)),
            any_spec, any_spec,
            _const_spec((len(POOL_WINDOWS), POOL_GROUP_DIM, POOL_GROUP_DIM)),
            _const_spec((1, D_MODEL)),
            _const_spec((1, D_MODEL)),
        ] + [any_spec] * n_alias,
        out_specs=[
            pl.BlockSpec((N_BRANCH, rb, D_MODEL), lambda i, j: (0, i, 0)),
            pl.BlockSpec((1, sb, GLA_HEADS, GLA_DK, GLA_DV), lambda i, j: (layer, i * halves + j, 0, 0, 0)),
            pl.BlockSpec((1, POOL_HIST, rb, D_MODEL), lambda i, j: (layer, 0, i, 0)),
        ],
        out_shape=[
            jax.ShapeDtypeStruct((N_BRANCH, nb, D_MODEL), F32),
            jax.ShapeDtypeStruct(s0.shape, F32),
            jax.ShapeDtypeStruct(hist.shape, F32),
        ],
        scratch_shapes=[pltpu.VMEM((rb, D_MODEL), F32), pltpu.VMEM(cache_ring, F32), pltpu.VMEM(cache_ring, F32),
                        pltpu.SemaphoreType.DMA((2, CACHE_RING))],
        input_output_aliases={n_in + a: 1 + a for a in range(n_alias)},
        compiler_params=_params(
            ("arbitrary", "arbitrary"),
            ((N_MIX_SLABS + N_BRANCH + 1, rb, D_MODEL), F32, 2),
            ((sb, GLA_HEADS, GLA_DK, GLA_DV), F32, 4),
            ((POOL_HIST, rb, D_MODEL), F32, 4),
            (cache_ring, F32, 2),
            ((len(POOL_WINDOWS), POOL_GROUP_DIM, POOL_GROUP_DIM), BF, 1)),
        name="sample_mix",
    )(p, la, s0, hist, ck, cv, pool_w, pool_scale, gla_gain, *(carry or ()))


def _merge_kernel(br_ref, p_ref, x_ref, wbr_ref, wo_ref, fg_ref, out_ref, *, final):
    merged = _merge_term(br_ref, p_ref, wbr_ref, 0)
    for n in range(1, N_BRANCH):
        merged = merged + _merge_term(br_ref, p_ref, wbr_ref, n)
    out_ref[...] = _merge_finish(merged, x_ref[...], wo_ref, fg_ref, final)


def _merge_out(br, p, x, w_branch, w_out, final_gain, layer, final):
    m_rows = x.shape[0]
    whole = lambda shape: pl.BlockSpec(shape, lambda i: (0,) * len(shape))
    return pl.pallas_call(
        functools.partial(_merge_kernel, final=final),
        grid=(1,),
        in_specs=[
            whole((N_BRANCH, m_rows, D_MODEL)),
            whole((N_SLABS, m_rows, D_MODEL)),
            whole((m_rows, D_MODEL)),
            _layer_spec((N_BRANCH, D_MODEL // 2, D_MODEL), layer),
            _layer_spec((D_MODEL // 2, D_MODEL), layer),
            _const_spec((1, D_MODEL)),
        ],
        out_specs=whole((m_rows, D_MODEL)),
        out_shape=jax.ShapeDtypeStruct((m_rows, D_MODEL), F32),
        compiler_params=_params(("arbitrary",),
                                ((N_BRANCH + N_SLABS + 2, m_rows, D_MODEL), F32, 2),
                                ((D_MODEL // 2, (N_BRANCH + 1) * D_MODEL), jnp.uint32, 1)),
        name="merge_out",
    )(br, p, x, w_branch, w_out, final_gain)


def kernel(x_prompt, x_sample, mem_prompt, cache_mem_k, cache_mem_v, state_gla, state_pool, w_in, w_a2, b_a, gla_gain, pool_w, pool_scale, w_mk, w_mv, w_branch, w_out, norm_gain, final_gain):
    batch, seq, _ = x_prompt.shape
    nb = x_sample.shape[0]
    n_mem = mem_prompt.shape[1]
    depth = w_in.shape[0]
    tm = PROMPT_TILE
    assert seq % tm == 0 and nb % SUBLANES == 0 and x_prompt.shape[2] == D_MODEL

    xp = x_prompt.reshape(batch * seq, D_MODEL)
    xs = x_sample.reshape(nb, D_MODEL)
    mem = mem_prompt.reshape(batch * n_mem, D_MODEL)
    fgain = final_gain.reshape(1, D_MODEL)

    mk, mv, mk_bf, mv_bf = _kvproj(mem, w_mk.astype(BF), w_mv.astype(BF), batch, n_mem)

    w_in_t = jnp.swapaxes(w_in, 1, 2)
    pool_rows = jnp.swapaxes(state_pool, 1, 2)
    w_proj = _pack_w_in(w_in_t)
    wb = _pack_matrices(w_branch.reshape(depth * N_BRANCH, D_MODEL, D_MODEL))
    wb = wb.reshape(depth, N_BRANCH, D_MODEL // 2, D_MODEL)
    wo = _pack_matrices(w_out)

    carry_p, carry_s = None, None
    for l in range(depth):
        final = l == depth - 1
        w_alr = w_in_t[l, ALR_START:ALR_START + GATE_RANK, :]
        wa2 = w_a2[l].astype(BF)
        ba = b_a[l].reshape(1, GLA_KEY_WIDTH)
        ngain = norm_gain[l].reshape(1, D_MODEL)
        ggain = gla_gain[l].reshape(1, D_MODEL)
        pscale = pool_scale[l].reshape(1, D_MODEL)
        pw = pool_w[l].astype(BF)

        xp, s_all, hist_all = _prompt_layer(xp, ngain, w_proj, w_alr, wa2, ba, mk_bf, mv_bf, pw, pscale,
                                            ggain, wb, wo, fgain, carry_p, l, depth, batch, seq, n_mem, tm, final)
        carry_p = (s_all, hist_all)

        ps, las = _inproj(xs, ngain, w_proj, w_alr, wa2, ba, l, nb, F32)
        brs, s_new, hist_new = _sample_mix(ps, las, state_gla, pool_rows, cache_mem_k, cache_mem_v,
                                           pw, pscale, ggain, carry_s, l)
        carry_s = (s_new, hist_new)
        xs = _merge_out(brs, ps, xs, wb, wo, fgain, l, final)

    return (xp.reshape(batch, seq, D_MODEL), xs.reshape(nb, 1, D_MODEL),
            mk, mv, carry_p[0], carry_p[1], carry_s[0], jnp.swapaxes(carry_s[1], 1, 2))
```

```python
import functools
import math

import jax
import jax.numpy as jnp
from jax import lax
from jax.experimental import pallas as pl
from jax.experimental.pallas import tpu as pltpu

D_MODEL = 1024
GLA_HEADS = 4
GLA_DK = 128
GLA_DV = 256
GLA_KEY_WIDTH = GLA_HEADS * GLA_DK
GATE_RANK = 16
GATE_TAU = 16.0
CHUNK = 128
POOL_WINDOWS = (2, 4, 8, 16)
POOL_GROUP_DIM = 256
POOL_HIST = 15
HIST_PAD = 16
XA_HEADS = 4
XA_HEAD_DIM = 256
N_BRANCH = 3
EPS = 1e-6
SUBLANES = 8
LANES = 128

SLAB_QK, SLAB_V, SLAB_GLA_G, SLAB_U, SLAB_POOL_G, SLAB_XQ, SLAB_XG, SLAB_MERGE = 0, 1, 2, 3, 4, 5, 6, 7
N_SLABS = 10
N_MIX_SLABS = 7
N_HEAD_SLABS = 3
ALR_START = N_HEAD_SLABS * D_MODEL

BF = jnp.bfloat16
F32 = jnp.float32
MIB = 1 << 20

PROMPT_TILE = 256
SAMPLE_BLOCK = 4
CACHE_RING = 3
VMEM_COMPILER_SCRATCH = 8 * MIB


def _dot(a, b):
    return jnp.dot(a, b, preferred_element_type=F32)


def _dot_nt(a, b):
    return lax.dot_general(a, b, (((1,), (1,)), ((), ())), preferred_element_type=F32)


def _dot_tn(a, b):
    return lax.dot_general(a, b, (((0,), (0,)), ((), ())), preferred_element_type=F32)


def _pack_rows(w):
    return pltpu.bitcast(w.astype(BF), jnp.uint32)


def _unpack_rows(w_words):
    return pltpu.bitcast(w_words, BF)


def _params(sem, *buffers):
    need = sum(math.prod(shape) * jnp.dtype(dtype).itemsize * copies for shape, dtype, copies in buffers)
    return pltpu.CompilerParams(dimension_semantics=sem, vmem_limit_bytes=need + VMEM_COMPILER_SCRATCH)


def _pack_w_in_kernel(a_ref, b_ref, o_ref):
    j = pl.program_id(1)

    @pl.when(j < N_HEAD_SLABS)
    def _():
        o_ref[0] = _pack_rows(a_ref[0].T)

    @pl.when(j >= N_HEAD_SLABS)
    def _():
        o_ref[0] = _pack_rows(jnp.concatenate([a_ref[0, GATE_RANK:, :], b_ref[0]], axis=0).T)


def _pack_w_in(w_in_t):
    depth = w_in_t.shape[0]
    return pl.pallas_call(
        _pack_w_in_kernel,
        grid=(depth, N_SLABS),
        in_specs=[
            pl.BlockSpec((1, D_MODEL, D_MODEL), lambda l, j: (l, j, 0)),
            pl.BlockSpec((1, GATE_RANK, D_MODEL), lambda l, j: (l, (j + 1) * (D_MODEL // GATE_RANK), 0)),
        ],
        out_specs=pl.BlockSpec((1, D_MODEL // 2, D_MODEL), lambda l, j: (l, 0, j)),
        out_shape=jax.ShapeDtypeStruct((depth, D_MODEL // 2, N_SLABS * D_MODEL), jnp.uint32),
        compiler_params=_params(("parallel", "parallel"),
                                ((D_MODEL, D_MODEL), F32, 2), ((GATE_RANK, D_MODEL), F32, 2),
                                ((D_MODEL // 2, D_MODEL), jnp.uint32, 2)),
        name="pack_w_in",
    )(w_in_t, w_in_t)


def _pack_matrices_kernel(a_ref, o_ref):
    o_ref[0] = _pack_rows(a_ref[0])


def _pack_matrices(w):
    return pl.pallas_call(
        _pack_matrices_kernel,
        grid=(w.shape[0],),
        in_specs=[pl.BlockSpec((1, D_MODEL, D_MODEL), lambda r: (r, 0, 0))],
        out_specs=pl.BlockSpec((1, D_MODEL // 2, D_MODEL), lambda r: (r, 0, 0)),
        out_shape=jax.ShapeDtypeStruct((w.shape[0], D_MODEL // 2, D_MODEL), jnp.uint32),
        compiler_params=_params(("parallel",), ((D_MODEL, D_MODEL), F32, 2), ((D_MODEL // 2, D_MODEL), jnp.uint32, 2)),
        name="pack_matrices",
    )(w)


def _silu(x):
    return x * jax.nn.sigmoid(x)


def _rms(x, gain):
    ms = jnp.mean(x * x, axis=-1, keepdims=True)
    return x * lax.rsqrt(ms + EPS) * gain


def _const_spec(shape):
    zeros = (0,) * len(shape)
    return pl.BlockSpec(shape, lambda *_: zeros, pipeline_mode=pl.Buffered(1))


def _layer_spec(shape, layer):
    index = (layer,) + (0,) * len(shape)
    return pl.BlockSpec((1,) + tuple(shape), lambda *_: index, pipeline_mode=pl.Buffered(1))


def _drop_alias_refs(body, n_in, n_alias):
    def kern(*refs):
        return body(*refs[:n_in], *refs[n_in + n_alias:])
    return kern


def _kvproj_kernel(m_ref, wk_ref, wv_ref, k_ref, v_ref, kb_ref, vb_ref):
    m = m_ref[...].astype(BF)
    k = _dot(m, wk_ref[0])
    v = _dot(m, wv_ref[0])
    for h in range(XA_HEADS):
        cs = slice(h * XA_HEAD_DIM, (h + 1) * XA_HEAD_DIM)
        k_ref[0, 0, :, h, :] = k[:, cs]
        v_ref[0, 0, :, h, :] = v[:, cs]
    kb_ref[0, 0] = k.T.astype(BF)
    vb_ref[0] = v.astype(BF)


def _kvproj(mem, wk, wv, batch, n_mem):
    depth = wk.shape[0]
    w_spec = pl.BlockSpec((1, D_MODEL, D_MODEL), lambda l, b: (l, 0, 0))
    out5 = pl.BlockSpec((1, 1, n_mem, XA_HEADS, XA_HEAD_DIM), lambda l, b: (l, b, 0, 0, 0))
    out_bf = pl.BlockSpec((1, n_mem, D_MODEL), lambda l, b: (l, b, 0))
    return pl.pallas_call(
        _kvproj_kernel,
        grid=(depth, batch),
        in_specs=[pl.BlockSpec((n_mem, D_MODEL), lambda l, b: (b, 0)), w_spec, w_spec],
        out_specs=[out5, out5, pl.BlockSpec((1, 1, D_MODEL, n_mem), lambda l, b: (l, b, 0, 0)), out_bf],
        out_shape=[jax.ShapeDtypeStruct((depth, batch, n_mem, XA_HEADS, XA_HEAD_DIM), F32)] * 2
        + [jax.ShapeDtypeStruct((depth, batch, D_MODEL, n_mem), BF),
           jax.ShapeDtypeStruct((depth, batch * n_mem, D_MODEL), BF)],
        compiler_params=_params(("parallel", "parallel"),
                                ((n_mem, D_MODEL), F32, 2), ((D_MODEL, D_MODEL), BF, 4),
                                ((n_mem, D_MODEL), F32, 4), ((n_mem, D_MODEL), BF, 4)),
        name="kvproj",
    )(mem, wk, wv)


def _inproj_slab(h, w_ref, p_out, j):
    w = w_ref[0, :, j * D_MODEL:(j + 1) * D_MODEL]
    p_out[j] = _dot(h, _unpack_rows(w)).astype(p_out.dtype)


def _inproj_gate_lowrank(h, walr_ref):
    return _dot_nt(h, walr_ref[...].astype(BF)).astype(BF)


def _inproj_gate(alr, wa2_ref, ba_ref, la_out):
    z = _dot(alr, wa2_ref[...]) + ba_ref[...]
    la_out[...] = (jnp.minimum(z, 0.0) - jnp.log(1.0 + jnp.exp(-jnp.abs(z)))) * (1.0 / GATE_TAU)


def _merge_term(br_ref, p, wbr_ref, n):
    return jax.nn.sigmoid(p[SLAB_MERGE + n].astype(F32)) * _dot(br_ref[n].astype(BF), _unpack_rows(wbr_ref[0, n]))


def _merge_finish(merged, x, wo_ref, fg_ref, final):
    x_new = x + _dot(merged.astype(BF), _unpack_rows(wo_ref[0]))
    return _rms(x_new, fg_ref[...]) if final else x_new


def _chunk_cumsum_matrix(tm):
    row = lax.broadcasted_iota(jnp.int32, (tm, tm), 0)
    col = lax.broadcasted_iota(jnp.int32, (tm, tm), 1)
    return (((row // CHUNK) == (col // CHUNK)) & (row >= col)).astype(BF)


def _gla_cumdecay(cum_ref, la_ref):
    la = la_ref[...]
    la_hi = la.astype(BF)
    la_lo = (la - la_hi.astype(F32)).astype(BF)
    return _dot(cum_ref[...], la_hi) + _dot(cum_ref[...], la_lo)


def _gla_chunk(p, bcum, gg_ref, br_ref, st_ref, c, filler):
    crow = lax.broadcasted_iota(jnp.int32, (CHUNK, CHUNK), 0)
    ccol = lax.broadcasted_iota(jnp.int32, (CHUNK, CHUNK), 1)
    causal = crow >= ccol
    rows = slice(c * CHUNK, (c + 1) * CHUNK)
    heads = range(GLA_HEADS)
    q_dec, k_end, decay, att = [], [], [], []
    for h in heads:
        kc = slice(h * GLA_DK, (h + 1) * GLA_DK)
        kc2 = slice(GLA_KEY_WIDTH + h * GLA_DK, GLA_KEY_WIDTH + (h + 1) * GLA_DK)
        b = bcum[rows, kc]
        b_mid = b[CHUNK // 2 - 1:CHUNK // 2, :]
        b_last = b[CHUNK - 1:CHUNK, :]
        q = p[SLAB_QK, rows, kc].astype(F32) * (GLA_DK ** -0.5)
        k = p[SLAB_QK, rows, kc2].astype(F32)
        q_dec.append((q * jnp.exp(b)).astype(BF))
        q_mid = (q * jnp.exp(b - b_mid)).astype(BF)
        k_mid = (k * jnp.exp(b_mid - b)).astype(BF)
        k_end.append((k * jnp.exp(b_last - b)).astype(BF))
        decay.append(jnp.exp(b_last))
        att.append(_dot_nt(q_mid, k_mid))
    filler()
    o = []
    for h in heads:
        vc = slice(h * GLA_DV, (h + 1) * GLA_DV)
        a = jnp.where(causal, att[h], 0.0).astype(BF)
        s_before = st_ref[h].T.astype(BF)
        o.append(_dot(jnp.concatenate([a, q_dec[h]], axis=1),
                      jnp.concatenate([p[SLAB_V, rows, vc], s_before], axis=0)))
    for h in heads:
        vc = slice(h * GLA_DV, (h + 1) * GLA_DV)
        st_ref[h] = decay[h] * st_ref[h] + _dot_tn(k_end[h], p[SLAB_V, rows, vc]).T
    for h in heads:
        vc = slice(h * GLA_DV, (h + 1) * GLA_DV)
        g = p[SLAB_GLA_G, rows, vc].astype(F32)
        br_ref[0, rows, vc] = (_rms(o[h], gg_ref[:, vc]) * _silu(g)).astype(BF)


def _pool_branch(p, pw_ref, ps_ref, br_ref, ubuf_ref, t, tm):
    u = p[SLAB_U].astype(F32)
    ubuf_ref[HIST_PAD:HIST_PAD + tm, :] = u
    pos = t * tm + lax.broadcasted_iota(jnp.int32, (tm, 1), 0)
    for g, w in enumerate(POOL_WINDOWS):
        cs = slice(g * POOL_GROUP_DIM, (g + 1) * POOL_GROUP_DIM)
        ug = u[:, cs]
        s = ug
        for j in range(1, w):
            s = s + ubuf_ref[HIST_PAD - j:HIST_PAD - j + tm, cs]
        cnt = jnp.minimum(w, pos + 1).astype(F32)
        diff = s / cnt - ug
        mixed = _dot(diff.astype(BF), pw_ref[g]) * ps_ref[:, cs]
        pg = p[SLAB_POOL_G, :, cs].astype(F32)
        br_ref[1, :, cs] = (mixed * _silu(pg)).astype(BF)
    ubuf_ref[0:HIST_PAD, :] = u[tm - HIST_PAD:tm]


def _xattn_probs(p, mk_ref):
    out = []
    for h in range(XA_HEADS):
        cs = slice(h * XA_HEAD_DIM, (h + 1) * XA_HEAD_DIM)
        s = _dot(p[SLAB_XQ, :, cs], mk_ref[0, 0, cs, :]) * (XA_HEAD_DIM ** -0.5)
        pr = jnp.exp(s - jnp.max(s, axis=-1, keepdims=True))
        out.append((pr.astype(BF), jnp.sum(pr, axis=-1, keepdims=True)))
    return out


def _xattn_branch(p, probs, mv_ref, br_ref):
    for h in range(XA_HEADS):
        cs = slice(h * XA_HEAD_DIM, (h + 1) * XA_HEAD_DIM)
        pr, denom = probs[h]
        o = _dot(pr, mv_ref[0, :, cs]) / denom
        xg = p[SLAB_XG, :, cs].astype(F32)
        br_ref[2, :, cs] = (o * _silu(xg)).astype(BF)


def _prompt_layer_kernel(xn_ref, xc_ref, g_ref, w_ref, walr_ref, wa2_ref, ba_ref, cum_ref,
                         mk_ref, mv_ref, pw_ref, ps_ref, gg_ref, wbr_ref, wo_ref, fg_ref,
                         out_ref, s_out_ref, hist_out_ref,
                         h_scr, p_scr, la_scr, br_scr, st_ref, ubuf_ref, *, tm, nt, final):
    s = pl.program_id(0)
    t = jnp.maximum(s - 1, 0) % nt
    slot_w = s % 2
    slot_r = 1 - slot_w

    @pl.when(s == 0)
    def _():
        p_scr[1] = jnp.zeros(p_scr.shape[1:], p_scr.dtype)
        la_scr[1] = jnp.zeros(la_scr.shape[1:], la_scr.dtype)

    @pl.when(t == 0)
    def _():
        st_ref[...] = jnp.zeros_like(st_ref)
        ubuf_ref[0:HIST_PAD, :] = jnp.zeros((HIST_PAD, D_MODEL), F32)

    @pl.when(s == 0)
    def _():
        h_scr[0] = _rms(xc_ref[...], g_ref[...]).astype(BF)

    p_next = p_scr.at[slot_w]
    p = p_scr.at[slot_r]

    slabs = list(range(N_SLABS))

    def filler(n=1):
        for _ in range(n):
            _inproj_slab(h_scr[slot_w], w_ref, p_next, slabs.pop(0))

    alr = _inproj_gate_lowrank(h_scr[slot_w], walr_ref)
    bcum = _gla_cumdecay(cum_ref, la_scr.at[slot_r])
    filler()
    _inproj_gate(alr, wa2_ref, ba_ref, la_scr.at[slot_w])
    per_site = (N_SLABS - 2) // (2 * (tm // CHUNK))
    probs = None
    for c in range(tm // CHUNK):
        _gla_chunk(p, bcum, gg_ref, br_scr, st_ref, c, functools.partial(filler, per_site))
        filler(per_site)
        if c == 0:
            probs = _xattn_probs(p, mk_ref)
        elif c == 1:
            _xattn_branch(p, probs, mv_ref, br_scr)
            _pool_branch(p, pw_ref, ps_ref, br_scr, ubuf_ref, t, tm)
    h_scr[slot_r] = _rms(xn_ref[...], g_ref[...]).astype(BF)
    merged = _merge_term(br_scr, p, wbr_ref, 2)
    merged = merged + _merge_term(br_scr, p, wbr_ref, 1)
    merged = merged + _merge_term(br_scr, p, wbr_ref, 0)
    filler()
    assert not slabs
    out_ref[...] = _merge_finish(merged, xc_ref[...], wo_ref, fg_ref, final)

    @pl.when((s > 0) & (t == nt - 1))
    def _():
        for h in range(GLA_HEADS):
            s_out_ref[0, 0, h] = st_ref[h].T
        hist_out_ref[0, 0] = ubuf_ref[1:HIST_PAD, :]


def _prompt_layer(x, ngain, w_proj, w_alr, wa2, ba, mk, mv, pool_w, pool_scale, gla_gain,
                  w_branch, w_out, fgain, carry, layer, depth, batch, seq, n_mem, tm, final):
    nt = seq // tm
    n_tiles = batch * nt
    assert tm // CHUNK == 2 and (N_SLABS - 2) % (2 * (tm // CHUNK)) == 0, "slab placement assumes two GLA chunks"
    n_alias = 0 if carry is None else len(carry)
    n_in = 16
    kern = _drop_alias_refs(functools.partial(_prompt_layer_kernel, tm=tm, nt=nt, final=final), n_in, n_alias)
    any_spec = pl.BlockSpec(memory_space=pl.ANY)
    scratch = [
        ((2, tm, D_MODEL), BF, 1),
        ((2, N_SLABS, tm, D_MODEL), BF, 1),
        ((2, tm, GLA_KEY_WIDTH), F32, 1),
        ((N_BRANCH, tm, D_MODEL), BF, 1),
        ((GLA_HEADS, GLA_DV, GLA_DK), F32, 1),
        ((HIST_PAD + tm, D_MODEL), F32, 1),
    ]

    def cur(s):
        return jnp.maximum(s - 1, 0)

    def seq_of(s):
        return cur(s) // nt

    return pl.pallas_call(
        kern,
        grid=(n_tiles + 1,),
        in_specs=[
            pl.BlockSpec((tm, D_MODEL), lambda s: (jnp.minimum(s + 1, n_tiles - 1), 0)),
            pl.BlockSpec((tm, D_MODEL), lambda s: (cur(s), 0)),
            _const_spec((1, D_MODEL)),
            _layer_spec((D_MODEL // 2, N_SLABS * D_MODEL), layer),
            _const_spec((GATE_RANK, D_MODEL)),
            _const_spec((GATE_RANK, GLA_KEY_WIDTH)),
            _const_spec((1, GLA_KEY_WIDTH)),
            _const_spec((tm, tm)),
            pl.BlockSpec((1, 1, D_MODEL, n_mem), lambda s: (layer, seq_of(s), 0, 0)),
            pl.BlockSpec((1, n_mem, D_MODEL), lambda s: (layer, seq_of(s), 0)),
            _const_spec((len(POOL_WINDOWS), POOL_GROUP_DIM, POOL_GROUP_DIM)),
            _const_spec((1, D_MODEL)),
            _const_spec((1, D_MODEL)),
            _layer_spec((N_BRANCH, D_MODEL // 2, D_MODEL), layer),
            _layer_spec((D_MODEL // 2, D_MODEL), layer),
            _const_spec((1, D_MODEL)),
        ] + [any_spec] * n_alias,
        out_specs=[
            pl.BlockSpec((tm, D_MODEL), lambda s: (cur(s), 0)),
            pl.BlockSpec((1, 1, GLA_HEADS, GLA_DK, GLA_DV), lambda s: (layer, seq_of(s), 0, 0, 0)),
            pl.BlockSpec((1, 1, POOL_HIST, D_MODEL), lambda s: (layer, seq_of(s), 0, 0)),
        ],
        out_shape=[
            jax.ShapeDtypeStruct((n_tiles * tm, D_MODEL), F32),
            jax.ShapeDtypeStruct((depth, batch, GLA_HEADS, GLA_DK, GLA_DV), F32),
            jax.ShapeDtypeStruct((depth, batch, POOL_HIST, D_MODEL), F32),
        ],
        scratch_shapes=[pltpu.VMEM(shape, dtype) for shape, dtype, _ in scratch],
        input_output_aliases={n_in + a: 1 + a for a in range(n_alias)},
        compiler_params=_params(
            ("arbitrary",), *scratch,
            ((tm, D_MODEL), F32, 6),
            ((D_MODEL // 2, (N_SLABS + N_BRANCH + 1) * D_MODEL), jnp.uint32, 1),
            ((n_mem, D_MODEL), BF, 4),
            ((tm, tm), BF, 1),
            ((len(POOL_WINDOWS), POOL_GROUP_DIM, POOL_GROUP_DIM), BF, 1),
            ((GLA_HEADS, GLA_DK, GLA_DV), F32, 2)),
        name="prompt_layer",
    )(x, x, ngain, w_proj, w_alr, wa2, ba, _chunk_cumsum_matrix(tm),
      mk, mv, pool_w, pool_scale, gla_gain,
      w_branch, w_out, fgain, *(carry or ()))


def _inproj_kernel(x_ref, g_ref, w_ref, walr_ref, wa2_ref, ba_ref, p_ref, la_ref):
    h = _rms(x_ref[...], g_ref[...]).astype(BF)
    p_ref[0] = _dot(h, _unpack_rows(w_ref[0])).astype(p_ref.dtype)

    @pl.when(pl.program_id(0) == 0)
    def _():
        _inproj_gate(_inproj_gate_lowrank(h, walr_ref), wa2_ref, ba_ref, la_ref)


def _inproj(x, gain, w_proj, w_alr, w_a2, b_a, layer, tm, out_dtype):
    m_rows = x.shape[0]
    assert m_rows == tm, "the sample group is projected as one row block"
    return pl.pallas_call(
        _inproj_kernel,
        grid=(N_SLABS,),
        in_specs=[
            _const_spec((tm, D_MODEL)),
            _const_spec((1, D_MODEL)),
            pl.BlockSpec((1, D_MODEL // 2, D_MODEL), lambda j: (layer, 0, j)),
            _const_spec((GATE_RANK, D_MODEL)),
            _const_spec((GATE_RANK, GLA_KEY_WIDTH)),
            _const_spec((1, GLA_KEY_WIDTH)),
        ],
        out_specs=[
            pl.BlockSpec((1, tm, D_MODEL), lambda j: (j, 0, 0)),
            pl.BlockSpec((tm, GLA_KEY_WIDTH), lambda j: (0, 0)),
        ],
        out_shape=[
            jax.ShapeDtypeStruct((N_SLABS, m_rows, D_MODEL), out_dtype),
            jax.ShapeDtypeStruct((m_rows, GLA_KEY_WIDTH), F32),
        ],
        compiler_params=_params(("arbitrary",),
                                ((tm, D_MODEL), F32, 1), ((D_MODEL // 2, D_MODEL), jnp.uint32, 2),
                                ((tm, D_MODEL), out_dtype, 2), ((tm, GLA_KEY_WIDTH), F32, 2)),
        name="inproj",
    )(x, gain, w_proj, w_alr, w_a2, b_a)


def _sample_mix_kernel(p_ref, la_ref, s0_ref, hist_ref, ck_hbm, cv_hbm, pw_ref, ps_ref, gg_ref,
                       br_ref, s_out_ref, hist_out_ref, diff_ref, kbuf, vbuf, cache_sem, *, sb, layer):
    n_steps = pl.num_programs(0) * pl.num_programs(1)
    step = pl.program_id(0) * pl.num_programs(1) + pl.program_id(1)

    def cache_copy(which, t):
        src, buf = ((ck_hbm, kbuf), (cv_hbm, vbuf))[which]
        slot = t % CACHE_RING
        return pltpu.make_async_copy(src.at[layer, pl.ds(t * sb, sb)], buf.at[slot], cache_sem.at[which, slot])

    @pl.when(step == 0)
    def _():
        for t in range(CACHE_RING - 1):
            cache_copy(0, t).start()
            cache_copy(1, t).start()

    @pl.when(step + CACHE_RING - 1 < n_steps)
    def _():
        cache_copy(0, step + CACHE_RING - 1).start()
        cache_copy(1, step + CACHE_RING - 1).start()

    cache_copy(0, step).wait()
    cache_copy(1, step).wait()
    ck_ref = kbuf.at[step % CACHE_RING]
    cv_ref = vbuf.at[step % CACHE_RING]

    r0 = pl.program_id(1) * sb
    erow = lax.broadcasted_iota(jnp.int32, (GLA_DK, GLA_DK), 0)
    ecol = lax.broadcasted_iota(jnp.int32, (GLA_DK, GLA_DK), 1)
    eye = erow == ecol

    def to_col(x):
        return jnp.sum(jnp.where(eye, jnp.broadcast_to(x, (GLA_DK, GLA_DK)), 0.0), axis=1, keepdims=True)

    for i in range(sb):
        r = pl.ds(r0 + i, 1)
        la = la_ref[r, :]
        qk = p_ref[SLAB_QK, r, :]
        vv = p_ref[SLAB_V, r, :]
        gla_g = p_ref[SLAB_GLA_G, r, :]
        for h in range(GLA_HEADS):
            kc = slice(h * GLA_DK, (h + 1) * GLA_DK)
            kc2 = slice(GLA_KEY_WIDTH + h * GLA_DK, GLA_KEY_WIDTH + (h + 1) * GLA_DK)
            vc = slice(h * GLA_DV, (h + 1) * GLA_DV)
            a_col = to_col(jnp.exp(la[:, kc]))
            q_col = to_col(qk[:, kc] * (GLA_DK ** -0.5))
            k_col = to_col(qk[:, kc2])
            s_new = a_col * s0_ref[0, i, h] + k_col * vv[:, vc]
            s_out_ref[0, i, h] = s_new
            o = jnp.sum(q_col * s_new, axis=0, keepdims=True)
            br_ref[0, r, vc] = _rms(o, gg_ref[:, vc]) * _silu(gla_g[:, vc])

        u = p_ref[SLAB_U, r, :]
        for g, w in enumerate(POOL_WINDOWS):
            cs = slice(g * POOL_GROUP_DIM, (g + 1) * POOL_GROUP_DIM)
            past = jnp.sum(hist_ref[0, POOL_HIST - (w - 1):POOL_HIST, r, cs], axis=0)
            diff_ref[r, cs] = (u[:, cs] + past) / float(w) - u[:, cs]
        hist_out_ref[0, 0:POOL_HIST - 1, r, :] = hist_ref[0, 1:POOL_HIST, r, :]
        hist_out_ref[0, POOL_HIST - 1, r, :] = u

        xq = p_ref[SLAB_XQ, r, :]
        xg = p_ref[SLAB_XG, r, :]
        half_cols = [slice(h * XA_HEAD_DIM + j * LANES, h * XA_HEAD_DIM + (j + 1) * LANES)
                     for j in range(XA_HEAD_DIM // LANES) for h in range(XA_HEADS)]
        xq_rows = jnp.concatenate([xq[:, cs] for cs in half_cols], axis=0)
        n_mem = ck_ref.shape[1]
        prod = (ck_ref[i] * xq_rows[None]).reshape(n_mem * SUBLANES, LANES).astype(BF)
        part = _dot(prod, jnp.ones((LANES, LANES), BF)).reshape(n_mem, SUBLANES, LANES)
        s = (part + pltpu.roll(part, XA_HEADS, axis=1)) * (XA_HEAD_DIM ** -0.5)
        p = jnp.exp(s - jnp.max(s, axis=0, keepdims=True))
        o = jnp.sum(p * cv_ref[i], axis=0) / jnp.sum(p, axis=0)
        halves = XA_HEAD_DIM // LANES
        o_row = jnp.concatenate([o[j * XA_HEADS + h:j * XA_HEADS + h + 1, :]
                                 for h in range(XA_HEADS) for j in range(halves)], axis=1)
        br_ref[2, r, :] = o_row * _silu(xg)

    @pl.when(pl.program_id(1) == pl.num_programs(1) - 1)
    def _():
        for g in range(len(POOL_WINDOWS)):
            cs = slice(g * POOL_GROUP_DIM, (g + 1) * POOL_GROUP_DIM)
            mixed = _dot(diff_ref[:, cs].astype(BF), pw_ref[g]) * ps_ref[:, cs]
            br_ref[1, :, cs] = mixed * _silu(p_ref[SLAB_POOL_G, :, cs])


def _cache_rows_view(c):
    depth, nb, n_mem = c.shape[:3]
    halves = XA_HEAD_DIM // LANES
    c = c.reshape(depth, nb, n_mem, XA_HEADS, halves, LANES)
    return c.transpose(0, 1, 2, 4, 3, 5).reshape(depth, nb, n_mem, halves * XA_HEADS, LANES)


def _sample_mix(p, la, s0, hist, ck, cv, pool_w, pool_scale, gla_gain, carry, layer, sb=SAMPLE_BLOCK):
    nb = s0.shape[1]
    n_mem = ck.shape[2]
    ck, cv = _cache_rows_view(ck), _cache_rows_view(cv)
    rb = SUBLANES
    halves = rb // sb
    n_alias = 0 if carry is None else len(carry)
    n_in = 9
    kern = _drop_alias_refs(functools.partial(_sample_mix_kernel, sb=sb, layer=layer), n_in, n_alias)
    any_spec = pl.BlockSpec(memory_space=pl.ANY)
    cache_ring = (CACHE_RING, sb, n_mem, SUBLANES, LANES)
    return pl.pallas_call(
        kern,
        grid=(nb // rb, halves),
        in_specs=[
            pl.BlockSpec((N_MIX_SLABS, rb, D_MODEL), lambda i, j: (0, i, 0)),
            pl.BlockSpec((rb, GLA_KEY_WIDTH), lambda i, j: (i, 0)),
            pl.BlockSpec((1, sb, GLA_HEADS, GLA_DK, GLA_DV), lambda i, j: (layer, i * halves + j, 0, 0, 0)),
            pl.BlockSpec((1, POOL_HIST, rb, D_MODEL), lambda i, j: (layer, 0, i, 0)),
            any_spec, any_spec,
            _const_spec((len(POOL_WINDOWS), POOL_GROUP_DIM, POOL_GROUP_DIM)),
            _const_spec((1, D_MODEL)),
            _const_spec((1, D_MODEL)),
        ] + [any_spec] * n_alias,
        out_specs=[
            pl.BlockSpec((N_BRANCH, rb, D_MODEL), lambda i, j: (0, i, 0)),
            pl.BlockSpec((1, sb, GLA_HEADS, GLA_DK, GLA_DV), lambda i, j: (layer, i * halves + j, 0, 0, 0)),
            pl.BlockSpec((1, POOL_HIST, rb, D_MODEL), lambda i, j: (layer, 0, i, 0)),
        ],
        out_shape=[
            jax.ShapeDtypeStruct((N_BRANCH, nb, D_MODEL), F32),
            jax.ShapeDtypeStruct(s0.shape, F32),
            jax.ShapeDtypeStruct(hist.shape, F32),
        ],
        scratch_shapes=[pltpu.VMEM((rb, D_MODEL), F32), pltpu.VMEM(cache_ring, F32), pltpu.VMEM(cache_ring, F32),
                        pltpu.SemaphoreType.DMA((2, CACHE_RING))],
        input_output_aliases={n_in + a: 1 + a for a in range(n_alias)},
        compiler_params=_params(
            ("arbitrary", "arbitrary"),
            ((N_MIX_SLABS + N_BRANCH + 1, rb, D_MODEL), F32, 2),
            ((sb, GLA_HEADS, GLA_DK, GLA_DV), F32, 4),
            ((POOL_HIST, rb, D_MODEL), F32, 4),
            (cache_ring, F32, 2),
            ((len(POOL_WINDOWS), POOL_GROUP_DIM, POOL_GROUP_DIM), BF, 1)),
        name="sample_mix",
    )(p, la, s0, hist, ck, cv, pool_w, pool_scale, gla_gain, *(carry or ()))


def _merge_kernel(br_ref, p_ref, x_ref, wbr_ref, wo_ref, fg_ref, out_ref, *, final):
    merged = _merge_term(br_ref, p_ref, wbr_ref, 0)
    for n in range(1, N_BRANCH):
        merged = merged + _merge_term(br_ref, p_ref, wbr_ref, n)
    out_ref[...] = _merge_finish(merged, x_ref[...], wo_ref, fg_ref, final)


def _merge_out(br, p, x, w_branch, w_out, final_gain, layer, final):
    m_rows = x.shape[0]
    whole = lambda shape: pl.BlockSpec(shape, lambda i: (0,) * len(shape))
    return pl.pallas_call(
        functools.partial(_merge_kernel, final=final),
        grid=(1,),
        in_specs=[
            whole((N_BRANCH, m_rows, D_MODEL)),
            whole((N_SLABS, m_rows, D_MODEL)),
            whole((m_rows, D_MODEL)),
            _layer_spec((N_BRANCH, D_MODEL // 2, D_MODEL), layer),
            _layer_spec((D_MODEL // 2, D_MODEL), layer),
            _const_spec((1, D_MODEL)),
        ],
        out_specs=whole((m_rows, D_MODEL)),
        out_shape=jax.ShapeDtypeStruct((m_rows, D_MODEL), F32),
        compiler_params=_params(("arbitrary",),
                                ((N_BRANCH + N_SLABS + 2, m_rows, D_MODEL), F32, 2),
                                ((D_MODEL // 2, (N_BRANCH + 1) * D_MODEL), jnp.uint32, 1)),
        name="merge_out",
    )(br, p, x, w_branch, w_out, final_gain)


def kernel(x_prompt, x_sample, mem_prompt, cache_mem_k, cache_mem_v, state_gla, state_pool, w_in, w_a2, b_a, gla_gain, pool_w, pool_scale, w_mk, w_mv, w_branch, w_out, norm_gain, final_gain):
    batch, seq, _ = x_prompt.shape
    nb = x_sample.shape[0]
    n_mem = mem_prompt.shape[1]
    depth = w_in.shape[0]
    tm = PROMPT_TILE
    assert seq % tm == 0 and nb % SUBLANES == 0 and x_prompt.shape[2] == D_MODEL

    xp = x_prompt.reshape(batch * seq, D_MODEL)
    xs = x_sample.reshape(nb, D_MODEL)
    mem = mem_prompt.reshape(batch * n_mem, D_MODEL)
    fgain = final_gain.reshape(1, D_MODEL)

    mk, mv, mk_bf, mv_bf = _kvproj(mem, w_mk.astype(BF), w_mv.astype(BF), batch, n_mem)

    w_in_t = jnp.swapaxes(w_in, 1, 2)
    pool_rows = jnp.swapaxes(state_pool, 1, 2)
    w_proj = _pack_w_in(w_in_t)
    wb = _pack_matrices(w_branch.reshape(depth * N_BRANCH, D_MODEL, D_MODEL))
    wb = wb.reshape(depth, N_BRANCH, D_MODEL // 2, D_MODEL)
    wo = _pack_matrices(w_out)

    carry_p, carry_s = None, None
    for l in range(depth):
        final = l == depth - 1
        w_alr = w_in_t[l, ALR_START:ALR_START + GATE_RANK, :]
        wa2 = w_a2[l].astype(BF)
        ba = b_a[l].reshape(1, GLA_KEY_WIDTH)
        ngain = norm_gain[l].reshape(1, D_MODEL)
        ggain = gla_gain[l].reshape(1, D_MODEL)
        pscale = pool_scale[l].reshape(1, D_MODEL)
        pw = pool_w[l].astype(BF)

        xp, s_all, hist_all = _prompt_layer(xp, ngain, w_proj, w_alr, wa2, ba, mk_bf, mv_bf, pw, pscale,
                                            ggain, wb, wo, fgain, carry_p, l, depth, batch, seq, n_mem, tm, final)
        carry_p = (s_all, hist_all)

        ps, las = _inproj(xs, ngain, w_proj, w_alr, wa2, ba, l, nb, F32)
        brs, s_new, hist_new = _sample_mix(ps, las, state_gla, pool_rows, cache_mem_k, cache_mem_v,
                                           pw, pscale, ggain, carry_s, l)
        carry_s = (s_new, hist_new)
        xs = _merge_out(brs, ps, xs, wb, wo, fgain, l, final)

    return (xp.reshape(batch, seq, D_MODEL), xs.reshape(nb, 1, D_MODEL),
            mk, mv, carry_p[0], carry_p[1], carry_s[0], jnp.swapaxes(carry_s[1], 1, 2))
```
